```python
import jax, jax.numpy as jnp
from jax import lax
import numpy as np

D_MODEL = 1024
BATCH = 8
SEQ = 8192
DEPTH = 2

CHUNK = 64
CONV_WIDTH = D_MODEL
CONV_GROUPS = 16
CONV_K = 3
LRU_WIDTH = D_MODEL
LRU_HEADS = 16
LRU_HEAD_DIM = LRU_WIDTH // LRU_HEADS
LRU_CONV_K = 4
LRU_C = 8.0
D_FF = ((-(-8 * D_MODEL // 3)) + 255) // 256 * 256
RMS_EPS = 1e-6
IN_WIDTHS = (CONV_WIDTH, CONV_WIDTH, CONV_WIDTH, LRU_WIDTH, LRU_WIDTH, D_MODEL, D_MODEL)
IN_TOTAL = sum(IN_WIDTHS)
SPLIT_POINTS = tuple(int(v) for v in np.cumsum(IN_WIDTHS)[:-1])

kernel_name = "hybrid_shortconv_rglru_gated_merge"


def rmsnorm(x, g):
    xf = x.astype(jnp.float32)
    var = jnp.mean(xf * xf, axis=-1, keepdims=True)
    return (xf * lax.rsqrt(var + RMS_EPS) * g.astype(jnp.float32)).astype(x.dtype)


def causal_depthwise_conv(x, w, b=None):
    K = w.shape[0]
    S = x.shape[1]
    xp = jnp.pad(x, ((0, 0), (K - 1, 0), (0, 0)))
    y = xp[:, 0:S] * w[0]
    for k in range(1, K):
        y = y + xp[:, k:k + S] * w[k]
    if b is not None:
        y = y + b
    return y


def rg_lru(x, w_a, b_a, w_x, b_x, lam):
    Bsz, S, W = x.shape
    f32 = jnp.float32
    xf = x.astype(f32)
    xh = xf.reshape(Bsz, S, LRU_HEADS, LRU_HEAD_DIM)
    r = jax.nn.sigmoid(jnp.einsum('bshd,hde->bshe', xh, w_a.astype(f32)).reshape(Bsz, S, W) + b_a.astype(f32))
    i = jax.nn.sigmoid(jnp.einsum('bshd,hde->bshe', xh, w_x.astype(f32)).reshape(Bsz, S, W) + b_x.astype(f32))
    log_a = -LRU_C * r * jax.nn.softplus(-lam.astype(f32))
    a = jnp.exp(log_a)
    b = jnp.sqrt(-jnp.expm1(2.0 * log_a)) * (i * xf)
    n_chunks = S // CHUNK
    a_c = a.reshape(Bsz, n_chunks, CHUNK, W)
    b_c = b.reshape(Bsz, n_chunks, CHUNK, W)

    def combine(left, right):
        a_l, b_l = left
        a_r, b_r = right
        return a_l * a_r, a_r * b_l + b_r

    a_cum, h_loc = lax.associative_scan(combine, (a_c, b_c), axis=2)

    def step(h_prev, inp):
        a_cum_k, h_loc_k = inp
        h = h_loc_k + a_cum_k * h_prev[:, None, :]
        return h[:, -1], h

    h0 = jnp.zeros((Bsz, W), f32)
    _, hs = lax.scan(step, h0, (jnp.moveaxis(a_cum, 1, 0), jnp.moveaxis(h_loc, 1, 0)))
    return jnp.moveaxis(hs, 0, 1).reshape(Bsz, S, W).astype(x.dtype)


def _fwd_setup_inputs(seed: int = 0) -> dict:
    key = jax.random.key(seed)
    ks = jax.random.split(key, 24)
    f32 = jnp.float32
    nrm = lambda k, shape, fan_in: jax.random.normal(k, shape, f32) * (fan_in ** -0.5)
    x = jax.random.normal(ks[0], (BATCH, SEQ, D_MODEL), f32)
    ln1_g = 1.0 + 0.02 * jax.random.normal(ks[1], (DEPTH, D_MODEL), f32)
    w_in = nrm(ks[2], (DEPTH, D_MODEL, IN_TOTAL), D_MODEL)
    conv_a_w = nrm(ks[3], (DEPTH, CONV_K, CONV_WIDTH), CONV_K)
    conv_b_w = nrm(ks[4], (DEPTH, LRU_CONV_K, LRU_WIDTH), LRU_CONV_K)
    conv_b_b = 0.02 * jax.random.normal(ks[5], (DEPTH, LRU_WIDTH), f32)
    lru_wa = nrm(ks[6], (DEPTH, LRU_HEADS, LRU_HEAD_DIM, LRU_HEAD_DIM), LRU_HEAD_DIM)
    lru_ba = 0.02 * jax.random.normal(ks[7], (DEPTH, LRU_WIDTH), f32)
    lru_wx = nrm(ks[8], (DEPTH, LRU_HEADS, LRU_HEAD_DIM, LRU_HEAD_DIM), LRU_HEAD_DIM)
    lru_bx = 0.02 * jax.random.normal(ks[9], (DEPTH, LRU_WIDTH), f32)
    u = jax.random.uniform(ks[10], (DEPTH, LRU_WIDTH), f32, 0.9, 0.999)
    s = u ** (1.0 / LRU_C)
    lru_lambda = jnp.log(s) - jnp.log1p(-s)
    w_out_a = nrm(ks[11], (DEPTH, CONV_WIDTH, D_MODEL), CONV_WIDTH)
    w_out_b = nrm(ks[12], (DEPTH, LRU_WIDTH, D_MODEL), LRU_WIDTH)
    gate_bias = 0.02 * jax.random.normal(ks[13], (DEPTH, 2, D_MODEL), f32)
    w_o = nrm(ks[14], (DEPTH, D_MODEL, D_MODEL), D_MODEL)
    ln2_g = 1.0 + 0.02 * jax.random.normal(ks[15], (DEPTH, D_MODEL), f32)
    w_ffn_gate = nrm(ks[16], (DEPTH, D_MODEL, D_FF), D_MODEL)
    w_ffn_up = nrm(ks[17], (DEPTH, D_MODEL, D_FF), D_MODEL)
    w_ffn_down = nrm(ks[18], (DEPTH, D_FF, D_MODEL), D_FF)
    final_g = 1.0 + 0.02 * jax.random.normal(ks[19], (D_MODEL,), f32)
    return {"x": x, "ln1_g": ln1_g, "w_in": w_in, "conv_a_w": conv_a_w,
            "conv_b_w": conv_b_w, "conv_b_b": conv_b_b, "lru_wa": lru_wa,
            "lru_ba": lru_ba, "lru_wx": lru_wx, "lru_bx": lru_bx,
            "lru_lambda": lru_lambda, "w_out_a": w_out_a, "w_out_b": w_out_b,
            "gate_bias": gate_bias, "w_o": w_o, "ln2_g": ln2_g,
            "w_ffn_gate": w_ffn_gate, "w_ffn_up": w_ffn_up,
            "w_ffn_down": w_ffn_down, "final_g": final_g}


def _fwd_reference(x, ln1_g, w_in, conv_a_w, conv_b_w, conv_b_b, lru_wa, lru_ba,
              lru_wx, lru_bx, lru_lambda, w_out_a, w_out_b, gate_bias, w_o,
              ln2_g, w_ffn_gate, w_ffn_up, w_ffn_down, final_g):
    for l in range(DEPTH):
        h = rmsnorm(x, ln1_g[l])
        proj = h @ w_in[l]
        b_a, c_a, x_a, x_b, g_b, gate_a_logit, gate_b_logit = jnp.split(proj, SPLIT_POINTS, axis=-1)
        y_a = b_a * causal_depthwise_conv(c_a * x_a, conv_a_w[l])
        u_b = causal_depthwise_conv(x_b, conv_b_w[l], conv_b_b[l])
        y_b = rg_lru(u_b, lru_wa[l], lru_ba[l], lru_wx[l], lru_bx[l], lru_lambda[l])
        y_b = y_b * jax.nn.gelu(g_b)
        merged = (jax.nn.sigmoid(gate_a_logit + gate_bias[l, 0]) * (y_a @ w_out_a[l])
                  + jax.nn.sigmoid(gate_b_logit + gate_bias[l, 1]) * (y_b @ w_out_b[l]))
        x = x + merged @ w_o[l]
        h = rmsnorm(x, ln2_g[l])
        x = x + (jax.nn.silu(h @ w_ffn_gate[l]) * (h @ w_ffn_up[l])) @ w_ffn_down[l]
    return rmsnorm(x, final_g)


import jax as _jax
import jax.numpy as _jnp

TWIN_FORMAT = 'train_step'
FWD_PARAMS = ['x', 'ln1_g', 'w_in', 'conv_a_w', 'conv_b_w', 'conv_b_b', 'lru_wa', 'lru_ba', 'lru_wx', 'lru_bx', 'lru_lambda', 'w_out_a', 'w_out_b', 'gate_bias', 'w_o', 'ln2_g', 'w_ffn_gate', 'w_ffn_up', 'w_ffn_down', 'final_g']
TWIN_WEIGHTS = ['ln1_g', 'w_in', 'conv_a_w', 'conv_b_w', 'conv_b_b', 'lru_wa', 'lru_ba', 'lru_wx', 'lru_bx', 'lru_lambda', 'w_out_a', 'w_out_b', 'gate_bias', 'w_o', 'ln2_g', 'w_ffn_gate', 'w_ffn_up', 'w_ffn_down', 'final_g']
TWIN_DIFF_INPUT = 'x'
TWIN_INPUTS = ['x', 'ln1_g', 'w_in', 'conv_a_w', 'conv_b_w', 'conv_b_b', 'lru_wa', 'lru_ba', 'lru_wx', 'lru_bx', 'lru_lambda', 'w_out_a', 'w_out_b', 'gate_bias', 'w_o', 'ln2_g', 'w_ffn_gate', 'w_ffn_up', 'w_ffn_down', 'final_g', 'loss_target', 'm_ln1_g', 'm_w_in', 'm_conv_a_w', 'm_conv_b_w', 'm_conv_b_b', 'm_lru_wa', 'm_lru_ba', 'm_lru_wx', 'm_lru_bx', 'm_lru_lambda', 'm_w_out_a', 'm_w_out_b', 'm_gate_bias', 'm_w_o', 'm_ln2_g', 'm_w_ffn_gate', 'm_w_ffn_up', 'm_w_ffn_down', 'm_final_g', 'v_ln1_g', 'v_w_in', 'v_conv_a_w', 'v_conv_b_w', 'v_conv_b_b', 'v_lru_wa', 'v_lru_ba', 'v_lru_wx', 'v_lru_bx', 'v_lru_lambda', 'v_w_out_a', 'v_w_out_b', 'v_gate_bias', 'v_w_o', 'v_ln2_g', 'v_w_ffn_gate', 'v_w_ffn_up', 'v_w_ffn_down', 'v_final_g']
TWIN_OUTPUTS = ['loss', 'grad_x', 'grad_ln1_g', 'grad_w_in', 'grad_conv_a_w', 'grad_conv_b_w', 'grad_conv_b_b', 'grad_lru_wa', 'grad_lru_ba', 'grad_lru_wx', 'grad_lru_bx', 'grad_lru_lambda', 'grad_w_out_a', 'grad_w_out_b', 'grad_gate_bias', 'grad_w_o', 'grad_ln2_g', 'grad_w_ffn_gate', 'grad_w_ffn_up', 'grad_w_ffn_down', 'grad_final_g', 'delta_ln1_g', 'delta_w_in', 'delta_conv_a_w', 'delta_conv_b_w', 'delta_conv_b_b', 'delta_lru_wa', 'delta_lru_ba', 'delta_lru_wx', 'delta_lru_bx', 'delta_lru_lambda', 'delta_w_out_a', 'delta_w_out_b', 'delta_gate_bias', 'delta_w_o', 'delta_ln2_g', 'delta_w_ffn_gate', 'delta_w_ffn_up', 'delta_w_ffn_down', 'delta_final_g', 'new_m_ln1_g', 'new_m_w_in', 'new_m_conv_a_w', 'new_m_conv_b_w', 'new_m_conv_b_b', 'new_m_lru_wa', 'new_m_lru_ba', 'new_m_lru_wx', 'new_m_lru_bx', 'new_m_lru_lambda', 'new_m_w_out_a', 'new_m_w_out_b', 'new_m_gate_bias', 'new_m_w_o', 'new_m_ln2_g', 'new_m_w_ffn_gate', 'new_m_w_ffn_up', 'new_m_w_ffn_down', 'new_m_final_g', 'new_v_ln1_g', 'new_v_w_in', 'new_v_conv_a_w', 'new_v_conv_b_w', 'new_v_conv_b_b', 'new_v_lru_wa', 'new_v_lru_ba', 'new_v_lru_wx', 'new_v_lru_bx', 'new_v_lru_lambda', 'new_v_w_out_a', 'new_v_w_out_b', 'new_v_gate_bias', 'new_v_w_o', 'new_v_ln2_g', 'new_v_w_ffn_gate', 'new_v_w_ffn_up', 'new_v_w_ffn_down', 'new_v_final_g']
TWIN_LEAF_KINDS = {'loss': 'loss', 'grad_x': 'grad_x', 'grad_ln1_g': 'grad_w', 'grad_w_in': 'grad_w', 'grad_conv_a_w': 'grad_w', 'grad_conv_b_w': 'grad_w', 'grad_conv_b_b': 'grad_w', 'grad_lru_wa': 'grad_w', 'grad_lru_ba': 'grad_w', 'grad_lru_wx': 'grad_w', 'grad_lru_bx': 'grad_w', 'grad_lru_lambda': 'grad_w', 'grad_w_out_a': 'grad_w', 'grad_w_out_b': 'grad_w', 'grad_gate_bias': 'grad_w', 'grad_w_o': 'grad_w', 'grad_ln2_g': 'grad_w', 'grad_w_ffn_gate': 'grad_w', 'grad_w_ffn_up': 'grad_w', 'grad_w_ffn_down': 'grad_w', 'grad_final_g': 'grad_w', 'delta_ln1_g': 'delta_w', 'delta_w_in': 'delta_w', 'delta_conv_a_w': 'delta_w', 'delta_conv_b_w': 'delta_w', 'delta_conv_b_b': 'delta_w', 'delta_lru_wa': 'delta_w', 'delta_lru_ba': 'delta_w', 'delta_lru_wx': 'delta_w', 'delta_lru_bx': 'delta_w', 'delta_lru_lambda': 'delta_w', 'delta_w_out_a': 'delta_w', 'delta_w_out_b': 'delta_w', 'delta_gate_bias': 'delta_w', 'delta_w_o': 'delta_w', 'delta_ln2_g': 'delta_w', 'delta_w_ffn_gate': 'delta_w', 'delta_w_ffn_up': 'delta_w', 'delta_w_ffn_down': 'delta_w', 'delta_final_g': 'delta_w', 'new_m_ln1_g': 'new_m', 'new_m_w_in': 'new_m', 'new_m_conv_a_w': 'new_m', 'new_m_conv_b_w': 'new_m', 'new_m_conv_b_b': 'new_m', 'new_m_lru_wa': 'new_m', 'new_m_lru_ba': 'new_m', 'new_m_lru_wx': 'new_m', 'new_m_lru_bx': 'new_m', 'new_m_lru_lambda': 'new_m', 'new_m_w_out_a': 'new_m', 'new_m_w_out_b': 'new_m', 'new_m_gate_bias': 'new_m', 'new_m_w_o': 'new_m', 'new_m_ln2_g': 'new_m', 'new_m_w_ffn_gate': 'new_m', 'new_m_w_ffn_up': 'new_m', 'new_m_w_ffn_down': 'new_m', 'new_m_final_g': 'new_m', 'new_v_ln1_g': 'new_v', 'new_v_w_in': 'new_v', 'new_v_conv_a_w': 'new_v', 'new_v_conv_b_w': 'new_v', 'new_v_conv_b_b': 'new_v', 'new_v_lru_wa': 'new_v', 'new_v_lru_ba': 'new_v', 'new_v_lru_wx': 'new_v', 'new_v_lru_bx': 'new_v', 'new_v_lru_lambda': 'new_v', 'new_v_w_out_a': 'new_v', 'new_v_w_out_b': 'new_v', 'new_v_gate_bias': 'new_v', 'new_v_w_o': 'new_v', 'new_v_ln2_g': 'new_v', 'new_v_w_ffn_gate': 'new_v', 'new_v_w_ffn_up': 'new_v', 'new_v_w_ffn_down': 'new_v', 'new_v_final_g': 'new_v'}


def _forward(args):
    return _fwd_reference(*[args[k] for k in FWD_PARAMS])


def _output_shape():
    def fwd():
        inp = _fwd_setup_inputs(0)
        return _fwd_reference(*[inp[k] for k in FWD_PARAMS])
    out = _jax.eval_shape(fwd)
    return out.shape, out.dtype

N_MICROBATCH = 1
ADAM_LR = 0.001
ADAM_B1 = 0.9
ADAM_B2 = 0.999
ADAM_EPS = 1e-08
ADAM_WD = 0.01
ADAM_STEP = 10
PER_EXAMPLE_BATCH_AXIS = {'x': 0, 'loss_target': 0}
SHARED_INPUTS = []
_WEIGHT_DTYPES = {'ln1_g': _jnp.float32, 'w_in': _jnp.float32, 'conv_a_w': _jnp.float32, 'conv_b_w': _jnp.float32, 'conv_b_b': _jnp.float32, 'lru_wa': _jnp.float32, 'lru_ba': _jnp.float32, 'lru_wx': _jnp.float32, 'lru_bx': _jnp.float32, 'lru_lambda': _jnp.float32, 'w_out_a': _jnp.float32, 'w_out_b': _jnp.float32, 'gate_bias': _jnp.float32, 'w_o': _jnp.float32, 'ln2_g': _jnp.float32, 'w_ffn_gate': _jnp.float32, 'w_ffn_up': _jnp.float32, 'w_ffn_down': _jnp.float32, 'final_g': _jnp.float32}
MOMENT_SCALE = {'ln1_g': 2.549475e-01, 'w_in': 9.639507e-02, 'conv_a_w': 1.369388e-01, 'conv_b_w': 7.113414e-02, 'conv_b_b': 7.898534e-01, 'lru_wa': 2.778537e-02, 'lru_ba': 2.037361e-02, 'lru_wx': 5.155413e-02, 'lru_bx': 2.372156e-02, 'lru_lambda': 3.818336e-02, 'w_out_a': 1.348016e-01, 'w_out_b': 7.575621e-02, 'gate_bias': 4.056192e-02, 'w_o': 1.511711e-01, 'ln2_g': 1.836078e-01, 'w_ffn_gate': 7.654400e-02, 'w_ffn_up': 7.409063e-02, 'w_ffn_down': 1.227967e-01, 'final_g': 6.382920e+01}


def _to_microbatches(a, axis):
    t = _jnp.moveaxis(a, axis, 0)
    t = t.reshape((N_MICROBATCH, t.shape[0] // N_MICROBATCH) + t.shape[1:])
    return _jnp.moveaxis(t, 1, axis + 1)


def setup_inputs(seed: int = 0) -> dict:
    inp = _fwd_setup_inputs(seed)
    key = _jax.random.fold_in(_jax.random.key(seed), 7919)
    shape, _ = _output_shape()
    out = dict(inp)
    out["loss_target"] = _jax.random.normal(_jax.random.fold_in(key, 0), shape, _jnp.float32)
    for i, name in enumerate(TWIN_WEIGHTS):
        w = inp[name].astype(_jnp.float32)
        if MOMENT_SCALE is None:
            s = _jnp.sqrt(_jnp.mean(_jnp.square(w)) + 1e-30)
        else:
            s = MOMENT_SCALE[name]
        km, kv = _jax.random.split(_jax.random.fold_in(key, i + 1))
        out[name] = w
        out["m_" + name] = s * _jax.random.normal(km, w.shape, _jnp.float32)
        out["v_" + name] = (s * s) * _jax.random.uniform(kv, w.shape, _jnp.float32, 0.5, 1.5)
    if N_MICROBATCH > 1:
        for name, axis in PER_EXAMPLE_BATCH_AXIS.items():
            out[name] = _to_microbatches(out[name], axis)
    return {'x': out['x'], 'ln1_g': out['ln1_g'], 'w_in': out['w_in'], 'conv_a_w': out['conv_a_w'], 'conv_b_w': out['conv_b_w'], 'conv_b_b': out['conv_b_b'], 'lru_wa': out['lru_wa'], 'lru_ba': out['lru_ba'], 'lru_wx': out['lru_wx'], 'lru_bx': out['lru_bx'], 'lru_lambda': out['lru_lambda'], 'w_out_a': out['w_out_a'], 'w_out_b': out['w_out_b'], 'gate_bias': out['gate_bias'], 'w_o': out['w_o'], 'ln2_g': out['ln2_g'], 'w_ffn_gate': out['w_ffn_gate'], 'w_ffn_up': out['w_ffn_up'], 'w_ffn_down': out['w_ffn_down'], 'final_g': out['final_g'], 'loss_target': out['loss_target'], 'm_ln1_g': out['m_ln1_g'], 'm_w_in': out['m_w_in'], 'm_conv_a_w': out['m_conv_a_w'], 'm_conv_b_w': out['m_conv_b_w'], 'm_conv_b_b': out['m_conv_b_b'], 'm_lru_wa': out['m_lru_wa'], 'm_lru_ba': out['m_lru_ba'], 'm_lru_wx': out['m_lru_wx'], 'm_lru_bx': out['m_lru_bx'], 'm_lru_lambda': out['m_lru_lambda'], 'm_w_out_a': out['m_w_out_a'], 'm_w_out_b': out['m_w_out_b'], 'm_gate_bias': out['m_gate_bias'], 'm_w_o': out['m_w_o'], 'm_ln2_g': out['m_ln2_g'], 'm_w_ffn_gate': out['m_w_ffn_gate'], 'm_w_ffn_up': out['m_w_ffn_up'], 'm_w_ffn_down': out['m_w_ffn_down'], 'm_final_g': out['m_final_g'], 'v_ln1_g': out['v_ln1_g'], 'v_w_in': out['v_w_in'], 'v_conv_a_w': out['v_conv_a_w'], 'v_conv_b_w': out['v_conv_b_w'], 'v_conv_b_b': out['v_conv_b_b'], 'v_lru_wa': out['v_lru_wa'], 'v_lru_ba': out['v_lru_ba'], 'v_lru_wx': out['v_lru_wx'], 'v_lru_bx': out['v_lru_bx'], 'v_lru_lambda': out['v_lru_lambda'], 'v_w_out_a': out['v_w_out_a'], 'v_w_out_b': out['v_w_out_b'], 'v_gate_bias': out['v_gate_bias'], 'v_w_o': out['v_w_o'], 'v_ln2_g': out['v_ln2_g'], 'v_w_ffn_gate': out['v_w_ffn_gate'], 'v_w_ffn_up': out['v_w_ffn_up'], 'v_w_ffn_down': out['v_w_ffn_down'], 'v_final_g': out['v_final_g']}


def _loss(weights, diff, rest, loss_target):
    with _jax.named_scope("forward"):
        args = {**rest, TWIN_DIFF_INPUT: diff, **{k: w.astype(_WEIGHT_DTYPES[k]) for k, w in weights.items()}}
        y = _forward(args)
    with _jax.named_scope("loss_head"):
        err = _jnp.square(y.astype(_jnp.float32) - loss_target)
        return 0.5 * _jnp.sum(_jnp.mean(err, axis=-1)) if err.ndim else 0.5 * err


def _adamw(w, g, m, v):
    m = ADAM_B1 * m + (1.0 - ADAM_B1) * g
    v = ADAM_B2 * v + (1.0 - ADAM_B2) * _jnp.square(g)
    m_hat = m / (1.0 - ADAM_B1 ** ADAM_STEP)
    v_hat = v / (1.0 - ADAM_B2 ** ADAM_STEP)
    delta = -ADAM_LR * (m_hat / (_jnp.sqrt(v_hat) + ADAM_EPS) + ADAM_WD * w)
    return delta, m, v


def reference(x, ln1_g, w_in, conv_a_w, conv_b_w, conv_b_b, lru_wa, lru_ba, lru_wx, lru_bx, lru_lambda, w_out_a, w_out_b, gate_bias, w_o, ln2_g, w_ffn_gate, w_ffn_up, w_ffn_down, final_g, loss_target, m_ln1_g, m_w_in, m_conv_a_w, m_conv_b_w, m_conv_b_b, m_lru_wa, m_lru_ba, m_lru_wx, m_lru_bx, m_lru_lambda, m_w_out_a, m_w_out_b, m_gate_bias, m_w_o, m_ln2_g, m_w_ffn_gate, m_w_ffn_up, m_w_ffn_down, m_final_g, v_ln1_g, v_w_in, v_conv_a_w, v_conv_b_w, v_conv_b_b, v_lru_wa, v_lru_ba, v_lru_wx, v_lru_bx, v_lru_lambda, v_w_out_a, v_w_out_b, v_gate_bias, v_w_o, v_ln2_g, v_w_ffn_gate, v_w_ffn_up, v_w_ffn_down, v_final_g):
    given = dict(x=x, ln1_g=ln1_g, w_in=w_in, conv_a_w=conv_a_w, conv_b_w=conv_b_w, conv_b_b=conv_b_b, lru_wa=lru_wa, lru_ba=lru_ba, lru_wx=lru_wx, lru_bx=lru_bx, lru_lambda=lru_lambda, w_out_a=w_out_a, w_out_b=w_out_b, gate_bias=gate_bias, w_o=w_o, ln2_g=ln2_g, w_ffn_gate=w_ffn_gate, w_ffn_up=w_ffn_up, w_ffn_down=w_ffn_down, final_g=final_g, loss_target=loss_target, m_ln1_g=m_ln1_g, m_w_in=m_w_in, m_conv_a_w=m_conv_a_w, m_conv_b_w=m_conv_b_w, m_conv_b_b=m_conv_b_b, m_lru_wa=m_lru_wa, m_lru_ba=m_lru_ba, m_lru_wx=m_lru_wx, m_lru_bx=m_lru_bx, m_lru_lambda=m_lru_lambda, m_w_out_a=m_w_out_a, m_w_out_b=m_w_out_b, m_gate_bias=m_gate_bias, m_w_o=m_w_o, m_ln2_g=m_ln2_g, m_w_ffn_gate=m_w_ffn_gate, m_w_ffn_up=m_w_ffn_up, m_w_ffn_down=m_w_ffn_down, m_final_g=m_final_g, v_ln1_g=v_ln1_g, v_w_in=v_w_in, v_conv_a_w=v_conv_a_w, v_conv_b_w=v_conv_b_w, v_conv_b_b=v_conv_b_b, v_lru_wa=v_lru_wa, v_lru_ba=v_lru_ba, v_lru_wx=v_lru_wx, v_lru_bx=v_lru_bx, v_lru_lambda=v_lru_lambda, v_w_out_a=v_w_out_a, v_w_out_b=v_w_out_b, v_gate_bias=v_gate_bias, v_w_o=v_w_o, v_ln2_g=v_ln2_g, v_w_ffn_gate=v_w_ffn_gate, v_w_ffn_up=v_w_ffn_up, v_w_ffn_down=v_w_ffn_down, v_final_g=v_final_g)
    weights = {n: given[n] for n in TWIN_WEIGHTS}
    shared = {n: given[n] for n in SHARED_INPUTS}
    per_example = {n: given[n] for n in ['x']}
    grad_fn = _jax.value_and_grad(_loss, argnums=(0, 1))

    def one_microbatch(ex, loss_target):
        ex = dict(ex)
        diff = ex.pop(TWIN_DIFF_INPUT)
        return grad_fn(weights, diff, {**shared, **ex}, loss_target)

    if N_MICROBATCH == 1:
        loss, (grad_w, grad_x) = one_microbatch(per_example, given["loss_target"])
    else:
        def body(carry, xs):
            loss_sum, grad_sum = carry
            l_k, (gw_k, gx_k) = one_microbatch(xs[0], xs[1])
            with _jax.named_scope("update"):
                return (loss_sum + l_k, _jax.tree.map(_jnp.add, grad_sum, gw_k)), gx_k

        init = (_jnp.zeros((), _jnp.float32), _jax.tree.map(_jnp.zeros_like, weights))
        (loss, grad_w), grad_x = _jax.lax.scan(body, init, (per_example, given["loss_target"]))
    with _jax.named_scope("update"):
        delta_w, new_m, new_v = {}, {}, {}
        for n in TWIN_WEIGHTS:
            delta_w[n], new_m[n], new_v[n] = _adamw(weights[n], grad_w[n], given["m_" + n], given["v_" + n])
    return (loss, grad_x, *[grad_w[n] for n in TWIN_WEIGHTS], *[delta_w[n] for n in TWIN_WEIGHTS],
            *[new_m[n] for n in TWIN_WEIGHTS], *[new_v[n] for n in TWIN_WEIGHTS])
```

```python
import functools
import math

import jax
import jax.numpy as jnp
from jax import lax
from jax.experimental import pallas as pl
from jax.experimental.pallas import tpu as pltpu

F32 = jnp.float32
BF16 = jnp.bfloat16

N_DEV = 8
N_PROJ = 7
LRU_HEADS = 16
LRU_C = 8.0
RMS_EPS = 1e-6
CONV_A_K = 3
CONV_B_K = 4
GELU_C = math.sqrt(2.0 / math.pi)
GELU_A = 0.044715

ADAM_LR = 0.001
ADAM_B1 = 0.9
ADAM_B2 = 0.999
ADAM_EPS = 1e-08
ADAM_WD = 0.01
ADAM_STEP = 10

LANE = 128
SUBLANE = 8
MXU_TILE = 256
VMEM_LIMIT = 52 << 20
MESH = pl.DeviceIdType.MESH
ANY = pl.BlockSpec(memory_space=pl.ANY)


def _dot_nn(a, b):
    return lax.dot_general(a, b, (((1,), (0,)), ((), ())), preferred_element_type=F32)


def _dot_nt(a, b):
    return lax.dot_general(a, b, (((1,), (1,)), ((), ())), preferred_element_type=F32)


def _dot_tn(a, b):
    return lax.dot_general(a, b, (((0,), (0,)), ((), ())), preferred_element_type=F32)


def _sigmoid(x):
    return 1.0 / (1.0 + jnp.exp(-x))


def _gelu_and_grad(x):
    x2 = x * x
    t = jnp.tanh(GELU_C * x * (1.0 + GELU_A * x2))
    g = 0.5 * x * (1.0 + t)
    dg = 0.5 * (1.0 + t) + 0.5 * x * (1.0 - t * t) * GELU_C * (1.0 + 3.0 * GELU_A * x2)
    return g, dg


def _one_minus_exp(y):
    series = -y * (1.0 + y * (0.5 + y * (1.0 / 6.0 + y * (1.0 / 24.0))))
    return jnp.where(y > -0.01, series, 1.0 - jnp.exp(y))


def _zero(ref):
    ref[...] = jnp.zeros(ref.shape, ref.dtype)


def _fit_rows(r, row_bytes, budget=1 << 20):
    fits = [t for t in range(16, r + 1, 16) if r % t == 0 and t * row_bytes <= budget]
    return max(fits) if fits else r


def _row_tile(t, want):
    tm = min(want, t // 2)
    assert t % tm == 0 and tm % SUBLANE == 0, (t, tm)
    return tm


def _params(n_grid=1, **kw):
    return pltpu.CompilerParams(dimension_semantics=("arbitrary",) * n_grid, vmem_limit_bytes=VMEM_LIMIT, **kw)


class _Layout:
    def __init__(self, d, ff):
        self.d, self.ff = d, ff
        self.dd, self.ffs = d // N_DEV, ff // N_DEV
        self.rows = {"in": N_PROJ * self.dd, "oa": self.dd, "ob": self.dd, "o": self.dd,
                     "g": self.ffs, "u": self.ffs, "d": self.ffs}
        self.off, o = {}, 0
        for k in ("in", "oa", "ob", "o", "g", "u", "d"):
            self.off[k] = o
            o += self.rows[k]
        self.total = o


def _weight_copies(g_ref, layer, lay, key, dst, sems, base):
    rows, off = lay.rows[key], lay.off[key]
    return [pltpu.make_async_copy(g_ref.at[p, layer, pl.ds(off, rows), :], dst.at[pl.ds(p * rows, rows), :],
                                  sems.at[base + p]) for p in range(N_DEV)]


def _load_weights(g_ref, layer, lay, keys, dsts, sems):
    copies = []
    for n, (k, d) in enumerate(zip(keys, dsts)):
        copies += _weight_copies(g_ref, layer, lay, k, d, sems, n * N_DEV)
    for c in copies:
        c.start()
    for c in copies:
        c.wait()


def _all_gather(x, name):
    def body(x_ref, out_ref, send_sems, recv_sems, local_sem):
        mx, my, mc = lax.axis_index("x"), lax.axis_index("y"), lax.axis_index("c")
        me, sibling = (mx, my, mc), (mx, my, 1 - mc)
        chips = [(1 - mx, my), (mx, 1 - my), (1 - mx, 1 - my)]

        def slot(px, py, pc):
            return out_ref.at[4 * px + 2 * py + pc]

        def copy(k, block, to, src=None):
            return pltpu.make_async_remote_copy(
                src_ref=slot(*block) if src is None else src, dst_ref=slot(*block),
                send_sem=send_sems.at[k], recv_sem=recv_sems.at[k], device_id=to, device_id_type=MESH)

        mine = pltpu.make_async_copy(x_ref, slot(*me), local_sem)
        mine.start()
        first = [copy(0, me, sibling, src=x_ref)]
        first += [copy(1 + j, me, (*chip, mc), src=x_ref) for j, chip in enumerate(chips)]
        for cp in first:
            cp.start()
        passed = [copy(4 + j, (*chip, mc), sibling) for j, chip in enumerate(chips)]
        for j, chip in enumerate(chips):
            copy(1 + j, (*chip, mc), me).wait_recv()
            passed[j].start()
        copy(0, sibling, me).wait_recv()
        for j, chip in enumerate(chips):
            copy(4 + j, (*chip, 1 - mc), me).wait_recv()
        for cp in first + passed:
            cp.wait_send()
        mine.wait()

    return pl.pallas_call(
        body, name=name,
        out_shape=jax.ShapeDtypeStruct((N_DEV,) + x.shape, x.dtype),
        in_specs=[ANY], out_specs=ANY,
        scratch_shapes=[pltpu.SemaphoreType.DMA((7,)), pltpu.SemaphoreType.DMA((7,)), pltpu.SemaphoreType.DMA],
    )(x)


def _exchange_grads(grads, lay, name):
    n_layers = len(grads)
    keys = ("in", "oa", "ob", "o", "g", "u", "d")
    flat = [grads[l][k] for l in range(n_layers) for k in keys]
    d = flat[0].shape[1]

    def body(*refs):
        g_refs = refs[:len(flat)]
        out_ref, send_sems, recv_sems, local_sem = refs[len(flat):]
        mx, my, mc = lax.axis_index("x"), lax.axis_index("y"), lax.axis_index("c")
        me = 4 * mx + 2 * my + mc

        def pieces(q, dst_slot):
            out = []
            for l in range(n_layers):
                for n, k in enumerate(keys):
                    rows = lay.rows[k]
                    src = g_refs[l * len(keys) + n].at[pl.ds(pl.multiple_of(q * rows, 16), rows), :]
                    out.append((src, out_ref.at[dst_slot, l, pl.ds(lay.off[k], rows), :]))
            return out

        local = [pltpu.make_async_copy(s, t, local_sem) for s, t in pieces(me, me)]
        for cp in local:
            cp.start()
        for k in range(1, N_DEV):
            px, py, pc = mx ^ ((k >> 2) & 1), my ^ ((k >> 1) & 1), mc ^ (k & 1)
            for s, t in pieces(4 * px + 2 * py + pc, me):
                pltpu.make_async_remote_copy(src_ref=s, dst_ref=t, send_sem=send_sems.at[k - 1],
                                             recv_sem=recv_sems.at[k - 1], device_id=(px, py, pc),
                                             device_id_type=MESH).start()
        whole = out_ref.at[0]
        for k in range(1, N_DEV):
            done = pltpu.make_async_remote_copy(src_ref=whole, dst_ref=whole, send_sem=send_sems.at[k - 1],
                                                recv_sem=recv_sems.at[k - 1], device_id=(mx, my, mc),
                                                device_id_type=MESH)
            done.wait_send()
            done.wait_recv()
        pltpu.make_async_copy(whole, whole, local_sem).wait()

    return pl.pallas_call(
        body, name=name,
        out_shape=jax.ShapeDtypeStruct((N_DEV, n_layers, lay.total, d), BF16),
        in_specs=[ANY] * len(flat), out_specs=ANY,
        scratch_shapes=[pltpu.SemaphoreType.DMA((7,)), pltpu.SemaphoreType.DMA((7,)), pltpu.SemaphoreType.DMA],
    )(*flat)


def _sum_slots(x, name):
    n, r, c = x.shape
    tr = _fit_rows(r, c * 4)

    def body(x_ref, o_ref):
        acc = x_ref[0].astype(F32)
        for p in range(1, n):
            acc = acc + x_ref[p].astype(F32)
        o_ref[...] = acc

    return pl.pallas_call(
        body, name=name, grid=(r // tr,),
        out_shape=jax.ShapeDtypeStruct((r, c), F32),
        in_specs=[pl.BlockSpec((n, tr, c), lambda i: (0, i, 0))],
        out_specs=pl.BlockSpec((tr, c), lambda i: (i, 0)),
        compiler_params=_params(),
    )(x)


def _in_proj_fwd(x, g_row, gath, layer, lay, name):
    t, d = x.shape
    tm = _row_tile(t, 256)
    n_in = N_PROJ * d

    def body(x_ref, g_ref, gath_ref, p_ref, w_in, sems):
        @pl.when(pl.program_id(0) == 0)
        def _():
            _load_weights(gath_ref, layer, lay, ["in"], [w_in], sems)

        xf = x_ref[...]
        rstd = lax.rsqrt(jnp.mean(xf * xf, axis=-1, keepdims=True) + RMS_EPS)
        h = (xf * rstd * g_ref[...]).astype(BF16)
        for k in range(N_PROJ):
            p_ref[:, k * d:(k + 1) * d] = _dot_nt(h, w_in[k * d:(k + 1) * d, :]).astype(BF16)

    return pl.pallas_call(
        body, name=name, grid=(t // tm,),
        out_shape=jax.ShapeDtypeStruct((t, n_in), BF16),
        in_specs=[pl.BlockSpec((tm, d), lambda i: (i, 0)), pl.BlockSpec((1, d), lambda i: (0, 0)), ANY],
        out_specs=pl.BlockSpec((tm, n_in), lambda i: (i, 0)),
        scratch_shapes=[pltpu.VMEM((n_in, d), BF16), pltpu.SemaphoreType.DMA((N_DEV,))],
        compiler_params=_params(),
    )(x, g_row, gath)


def _lru_gates(ub, u, bda, bdx, ba, bx, sp):
    r = _sigmoid(_dot_nn(ub, bda) + ba)
    i = _sigmoid(_dot_nn(ub, bdx) + bx)
    log_a = (-LRU_C) * r * sp
    a = jnp.exp(log_a)
    s = jnp.sqrt(_one_minus_exp(2.0 * log_a))
    return r, i, a, s


def _mixer_fwd(p, caw, cbw, vec, bda, bdx, name):
    t = p.shape[0]
    d = p.shape[1] // N_PROJ
    tm = _row_tile(t, 256)
    cw = min(MXU_TILE, d)
    nb = d // cw

    def body(ba_ref, ca_ref, xa_ref, xb_ref, gb_ref, caw_ref, cbw_ref, vec_ref, bda_ref, bdx_ref,
             ya_ref, yb_ref, u_ref, h_ref, zbuf, xbuf, abuf, bbuf, z_tail, x_tail, h_carry):
        @pl.when(pl.program_id(0) == 0)
        def _():
            _zero(z_tail)
            _zero(x_tail)
            _zero(h_carry)

        row = lax.broadcasted_iota(jnp.int32, (SUBLANE, cw), 0)
        for j in range(nb):
            cs = slice(j * cw, (j + 1) * cw)
            z = ca_ref[:, cs].astype(F32) * xa_ref[:, cs].astype(F32)
            zbuf[0:SUBLANE, :] = z_tail[:, cs]
            zbuf[SUBLANE:, :] = z
            z_tail[:, cs] = z[tm - SUBLANE:, :]
            cz = (caw_ref[0:1, cs] * zbuf[pl.ds(SUBLANE - 2, tm), :] + caw_ref[1:2, cs] * zbuf[pl.ds(SUBLANE - 1, tm), :]
                  + caw_ref[2:3, cs] * z)
            ya_ref[:, cs] = (ba_ref[:, cs].astype(F32) * cz).astype(BF16)
            xb = xb_ref[:, cs].astype(F32)
            xbuf[0:SUBLANE, :] = x_tail[:, cs]
            xbuf[SUBLANE:, :] = xb
            x_tail[:, cs] = xb[tm - SUBLANE:, :]
            u = (cbw_ref[0:1, cs] * xbuf[pl.ds(SUBLANE - 3, tm), :] + cbw_ref[1:2, cs] * xbuf[pl.ds(SUBLANE - 2, tm), :]
                 + cbw_ref[2:3, cs] * xbuf[pl.ds(SUBLANE - 1, tm), :] + cbw_ref[3:4, cs] * xb + vec_ref[0:1, cs])
            ub = u.astype(BF16)
            u = ub.astype(F32)
            _, gi, a, s = _lru_gates(ub, u, bda_ref[j], bdx_ref[j], vec_ref[1:2, cs], vec_ref[2:3, cs], vec_ref[3:4, cs])
            abuf[...] = a
            bbuf[...] = s * (gi * u)

            def block(k, carry):
                r0 = pl.multiple_of(k * SUBLANE, SUBLANE)
                ak = abuf[pl.ds(r0, SUBLANE), :]
                bk = bbuf[pl.ds(r0, SUBLANE), :]
                for sh in (1, 2, 4):
                    m = row >= sh
                    bk = jnp.where(m, ak * pltpu.roll(bk, sh, 0) + bk, bk)
                    ak = jnp.where(m, ak * pltpu.roll(ak, sh, 0), ak)
                hk = bk + ak * carry
                bbuf[pl.ds(r0, SUBLANE), :] = hk
                return hk[SUBLANE - 1:SUBLANE, :]

            h_carry[0:1, cs] = lax.fori_loop(0, tm // SUBLANE, block, h_carry[0:1, cs], unroll=4)
            h = bbuf[...]
            gel, _ = _gelu_and_grad(gb_ref[:, cs].astype(F32))
            yb_ref[:, cs] = (h * gel).astype(BF16)
            u_ref[:, cs] = ub
            h_ref[:, cs] = h.astype(BF16)

    slab = lambda s: pl.BlockSpec((tm, d), lambda i, s=s: (i, s))
    small = pl.BlockSpec((SUBLANE, d), lambda i: (0, 0))
    bd = pl.BlockSpec((nb, cw, cw), lambda i: (0, 0, 0))
    out = pl.BlockSpec((tm, d), lambda i: (i, 0))
    return pl.pallas_call(
        body, name=name, grid=(t // tm,),
        out_shape=[jax.ShapeDtypeStruct((t, d), BF16)] * 4,
        in_specs=[slab(0), slab(1), slab(2), slab(3), slab(4), small, small, small, bd, bd],
        out_specs=[out] * 4,
        scratch_shapes=[pltpu.VMEM((tm + SUBLANE, cw), F32), pltpu.VMEM((tm + SUBLANE, cw), F32),
                        pltpu.VMEM((tm, cw), F32), pltpu.VMEM((tm, cw), F32),
                        pltpu.VMEM((SUBLANE, d), F32), pltpu.VMEM((SUBLANE, d), F32), pltpu.VMEM((SUBLANE, d), F32)],
        compiler_params=_params(),
    )(p, p, p, p, p, caw, cbw, vec, bda, bdx)


def _merge_fwd(x, ya, yb, p, gbias, gath, layer, lay, name):
    t, d = x.shape
    tm = _row_tile(t, 512)

    def body(x_ref, ya_ref, yb_ref, ga_ref, gb_ref, gbias_ref, gath_ref, x1_ref, oa_ref, ob_ref, w_oa, w_ob, w_o, sems):
        @pl.when(pl.program_id(0) == 0)
        def _():
            _load_weights(gath_ref, layer, lay, ["oa", "ob", "o"], [w_oa, w_ob, w_o], sems)

        oa = _dot_nn(ya_ref[...], w_oa[...]).astype(BF16)
        ob = _dot_nn(yb_ref[...], w_ob[...]).astype(BF16)
        oa_ref[...] = oa
        ob_ref[...] = ob
        sa = _sigmoid(ga_ref[...].astype(F32) + gbias_ref[0:1, :])
        sb = _sigmoid(gb_ref[...].astype(F32) + gbias_ref[1:2, :])
        merged = (sa * oa.astype(F32) + sb * ob.astype(F32)).astype(BF16)
        x1_ref[...] = x_ref[...] + _dot_nn(merged, w_o[...])

    row = pl.BlockSpec((tm, d), lambda i: (i, 0))
    return pl.pallas_call(
        body, name=name, grid=(t // tm,),
        out_shape=[jax.ShapeDtypeStruct((t, d), F32), jax.ShapeDtypeStruct((t, d), BF16), jax.ShapeDtypeStruct((t, d), BF16)],
        in_specs=[row, row, row, pl.BlockSpec((tm, d), lambda i: (i, 5)), pl.BlockSpec((tm, d), lambda i: (i, 6)),
                  pl.BlockSpec((SUBLANE, d), lambda i: (0, 0)), ANY],
        out_specs=[row, row, row],
        scratch_shapes=[pltpu.VMEM((d, d), BF16)] * 3 + [pltpu.SemaphoreType.DMA((3 * N_DEV,))],
        compiler_params=_params(),
    )(x, ya, yb, p, p, gbias, gath)


def _ffn_fwd(x1, g_row, gath, layer, lay, name):
    t, d = x1.shape
    ff = lay.ff
    tm = _row_tile(t, 512)
    fc = MXU_TILE
    assert ff % fc == 0

    def body(x_ref, g_ref, gath_ref, x2_ref, gg_ref, uu_ref, w_g, w_u, w_d, acc, sems):
        @pl.when(pl.program_id(0) == 0)
        def _():
            _load_weights(gath_ref, layer, lay, ["g", "u", "d"], [w_g, w_u, w_d], sems)

        xf = x_ref[...]
        rstd = lax.rsqrt(jnp.mean(xf * xf, axis=-1, keepdims=True) + RMS_EPS)
        h = (xf * rstd * g_ref[...]).astype(BF16)
        acc[...] = xf
        for c in range(ff // fc):
            fs = slice(c * fc, (c + 1) * fc)
            gg = _dot_nt(h, w_g[fs, :]).astype(BF16)
            uu = _dot_nt(h, w_u[fs, :]).astype(BF16)
            gg_ref[:, fs] = gg
            uu_ref[:, fs] = uu
            g32 = gg.astype(F32)
            f = (g32 * _sigmoid(g32) * uu.astype(F32)).astype(BF16)
            acc[...] += _dot_nn(f, w_d[fs, :])
        x2_ref[...] = acc[...]

    row = pl.BlockSpec((tm, d), lambda i: (i, 0))
    wide = pl.BlockSpec((tm, ff), lambda i: (i, 0))
    return pl.pallas_call(
        body, name=name, grid=(t // tm,),
        out_shape=[jax.ShapeDtypeStruct((t, d), F32), jax.ShapeDtypeStruct((t, ff), BF16), jax.ShapeDtypeStruct((t, ff), BF16)],
        in_specs=[row, pl.BlockSpec((1, d), lambda i: (0, 0)), ANY],
        out_specs=[row, wide, wide],
        scratch_shapes=[pltpu.VMEM((ff, d), BF16)] * 3 + [pltpu.VMEM((tm, d), F32), pltpu.SemaphoreType.DMA((3 * N_DEV,))],
        compiler_params=_params(),
    )(x1, g_row, gath)


def _loss_head(x, g_row, target, name):
    t, d = x.shape
    tm = _row_tile(t, 512)

    def body(x_ref, g_ref, tgt_ref, loss_ref, dx_ref, dg_ref):
        @pl.when(pl.program_id(0) == 0)
        def _():
            _zero(loss_ref)
            _zero(dg_ref)

        xf = x_ref[...]
        rstd = lax.rsqrt(jnp.mean(xf * xf, axis=-1, keepdims=True) + RMS_EPS)
        xh = xf * rstd
        g = g_ref[...]
        err = xh * g - tgt_ref[...]
        loss_ref[...] += 0.5 * jnp.sum(jnp.sum(err * err, axis=-1, keepdims=True), axis=0, keepdims=True) * (1.0 / d)
        dy = err * (1.0 / d)
        dg_ref[0:1, :] += jnp.sum(dy * xh, axis=0, keepdims=True)
        dxh = dy * g
        dx_ref[...] = rstd * (dxh - xh * jnp.mean(dxh * xh, axis=-1, keepdims=True))

    row = pl.BlockSpec((tm, d), lambda i: (i, 0))
    return pl.pallas_call(
        body, name=name, grid=(t // tm,),
        out_shape=[jax.ShapeDtypeStruct((SUBLANE, LANE), F32), jax.ShapeDtypeStruct((t, d), F32),
                   jax.ShapeDtypeStruct((SUBLANE, d), F32)],
        in_specs=[row, pl.BlockSpec((1, d), lambda i: (0, 0)), row],
        out_specs=[pl.BlockSpec((SUBLANE, LANE), lambda i: (0, 0)), row, pl.BlockSpec((SUBLANE, d), lambda i: (0, 0))],
        compiler_params=_params(),
    )(x, g_row, target)


def _ffn_bwd(dx2, x1, gg, uu, g_row, gath, layer, lay, name):
    t, d = x1.shape
    ff = lay.ff
    tm = _row_tile(t, 256)
    fc = MXU_TILE

    def body(dx2_ref, x_ref, gg_ref, uu_ref, g_ref, gath_ref,
             dx1_ref, dx1b_ref, dgg_ref, duu_ref, f_ref, h_ref, dx2b_ref, dg_ref, w_g, w_u, w_d, acc, sems):
        @pl.when(pl.program_id(0) == 0)
        def _():
            _load_weights(gath_ref, layer, lay, ["g", "u", "d"], [w_g, w_u, w_d], sems)
            _zero(dg_ref)

        dx2 = dx2_ref[...]
        dx2b = dx2.astype(BF16)
        dx2b_ref[...] = dx2b
        _zero(acc)
        for c in range(ff // fc):
            fs = slice(c * fc, (c + 1) * fc)
            df = _dot_nt(dx2b, w_d[fs, :])
            g32 = gg_ref[:, fs].astype(F32)
            u32 = uu_ref[:, fs].astype(F32)
            sg = _sigmoid(g32)
            silu = g32 * sg
            f_ref[:, fs] = (silu * u32).astype(BF16)
            du = (df * silu).astype(BF16)
            dg = (df * u32 * (sg * (1.0 + g32 * (1.0 - sg)))).astype(BF16)
            duu_ref[:, fs] = du
            dgg_ref[:, fs] = dg
            acc[...] += _dot_nn(dg, w_g[fs, :]) + _dot_nn(du, w_u[fs, :])
        xf = x_ref[...]
        rstd = lax.rsqrt(jnp.mean(xf * xf, axis=-1, keepdims=True) + RMS_EPS)
        xh = xf * rstd
        g = g_ref[...]
        h_ref[...] = (xh * g).astype(BF16)
        dh = acc[...]
        dg_ref[0:1, :] += jnp.sum(dh * xh, axis=0, keepdims=True)
        dxh = dh * g
        dx1 = dx2 + rstd * (dxh - xh * jnp.mean(dxh * xh, axis=-1, keepdims=True))
        dx1_ref[...] = dx1
        dx1b_ref[...] = dx1.astype(BF16)

    row = pl.BlockSpec((tm, d), lambda i: (i, 0))
    wide = pl.BlockSpec((tm, ff), lambda i: (i, 0))
    sd = jax.ShapeDtypeStruct
    return pl.pallas_call(
        body, name=name, grid=(t // tm,),
        out_shape=[sd((t, d), F32), sd((t, d), BF16), sd((t, ff), BF16), sd((t, ff), BF16), sd((t, ff), BF16),
                   sd((t, d), BF16), sd((t, d), BF16), sd((SUBLANE, d), F32)],
        in_specs=[row, row, wide, wide, pl.BlockSpec((1, d), lambda i: (0, 0)), ANY],
        out_specs=[row, row, wide, wide, wide, row, row, pl.BlockSpec((SUBLANE, d), lambda i: (0, 0))],
        scratch_shapes=[pltpu.VMEM((ff, d), BF16)] * 3 + [pltpu.VMEM((tm, d), F32), pltpu.SemaphoreType.DMA((3 * N_DEV,))],
        compiler_params=_params(),
    )(dx2, x1, gg, uu, g_row, gath)


def _merge_bwd(dx1b, oa, ob, p, gbias, gath, layer, lay, name):
    t, d = oa.shape
    tm = _row_tile(t, 512)

    def body(dx_ref, oa_ref, ob_ref, ga_ref, gb_ref, gbias_ref, gath_ref,
             dya_ref, dyb_ref, dpg_ref, mg_ref, doa_ref, dob_ref, dgb_ref, w_oa, w_ob, w_o, sems):
        @pl.when(pl.program_id(0) == 0)
        def _():
            _load_weights(gath_ref, layer, lay, ["oa", "ob", "o"], [w_oa, w_ob, w_o], sems)
            _zero(dgb_ref)

        dm = _dot_nt(dx_ref[...], w_o[...])
        oa = oa_ref[...].astype(F32)
        ob = ob_ref[...].astype(F32)
        sa = _sigmoid(ga_ref[...].astype(F32) + gbias_ref[0:1, :])
        sb = _sigmoid(gb_ref[...].astype(F32) + gbias_ref[1:2, :])
        mg_ref[...] = (sa * oa + sb * ob).astype(BF16)
        doa = (dm * sa).astype(BF16)
        dob = (dm * sb).astype(BF16)
        doa_ref[...] = doa
        dob_ref[...] = dob
        dga = dm * oa * sa * (1.0 - sa)
        dgb = dm * ob * sb * (1.0 - sb)
        dpg_ref[:, 0:d] = dga.astype(BF16)
        dpg_ref[:, d:2 * d] = dgb.astype(BF16)
        dgb_ref[0:1, :] += jnp.sum(dga, axis=0, keepdims=True)
        dgb_ref[1:2, :] += jnp.sum(dgb, axis=0, keepdims=True)
        dya_ref[...] = _dot_nt(doa, w_oa[...]).astype(BF16)
        dyb_ref[...] = _dot_nt(dob, w_ob[...]).astype(BF16)

    row = pl.BlockSpec((tm, d), lambda i: (i, 0))
    sd = jax.ShapeDtypeStruct
    return pl.pallas_call(
        body, name=name, grid=(t // tm,),
        out_shape=[sd((t, d), BF16), sd((t, d), BF16), sd((t, 2 * d), BF16), sd((t, d), BF16), sd((t, d), BF16),
                   sd((t, d), BF16), sd((SUBLANE, d), F32)],
        in_specs=[row, row, row, pl.BlockSpec((tm, d), lambda i: (i, 5)), pl.BlockSpec((tm, d), lambda i: (i, 6)),
                  pl.BlockSpec((SUBLANE, d), lambda i: (0, 0)), ANY],
        out_specs=[row, row, pl.BlockSpec((tm, 2 * d), lambda i: (i, 0)), row, row, row,
                   pl.BlockSpec((SUBLANE, d), lambda i: (0, 0))],
        scratch_shapes=[pltpu.VMEM((d, d), BF16)] * 3 + [pltpu.SemaphoreType.DMA((3 * N_DEV,))],
        compiler_params=_params(),
    )(dx1b, oa, ob, p, p, gbias, gath)


DV_CONV_B_B, DV_BA, DV_BX, DV_SP, DV_CONV_A, DV_CONV_B = 0, 1, 2, 3, 4, 7
DV_ROWS = 16


def _mixer_bwd(dya, dyb, dpg, p, u_s, h_s, caw, cbw, vec, bda, bdx, name):
    t, d = dya.shape
    tm = _row_tile(t, 256)
    n_t = t // tm
    cw = min(MXU_TILE, d)
    nb = d // cw
    hb = tm // SUBLANE

    def body(dya_ref, dyb_ref, dpg_ref, ba_ref, ca_ref, xa_ref, xb_ref, gb_ref, cah_ref, xah_ref, xbh_ref,
             u_ref, h_ref, hh_ref, caw_ref, cbw_ref, vec_ref, bda_ref, bdx_ref,
             dp_ref, dv_ref, dwa_ref, dwx_ref,
             zbuf, xbuf, hbuf, dczbuf, dubuf, a2buf, a1buf, lbuf, dcz_head, du_head, a_head, lam_head):
        i = pl.program_id(0)

        @pl.when(i == 0)
        def _():
            for ref in (dv_ref, dwa_ref, dwx_ref, dcz_head, du_head, a_head, lam_head):
                _zero(ref)

        has_prev = jnp.where(i < n_t - 1, 1.0, 0.0).astype(F32)
        row = lax.broadcasted_iota(jnp.int32, (SUBLANE, cw), 0)
        dp_ref[:, 5 * d:7 * d] = dpg_ref[...]

        def colsum(v):
            return jnp.sum(v, axis=0, keepdims=True)

        for j in range(nb):
            cs = slice(j * cw, (j + 1) * cw)
            ca = ca_ref[:, cs].astype(F32)
            xa = xa_ref[:, cs].astype(F32)
            z = ca * xa
            zbuf[0:SUBLANE, :] = cah_ref[:, cs].astype(F32) * xah_ref[:, cs].astype(F32) * has_prev
            zbuf[SUBLANE:, :] = z
            z1 = zbuf[pl.ds(SUBLANE - 1, tm), :]
            z2 = zbuf[pl.ds(SUBLANE - 2, tm), :]
            w0, w1, w2 = caw_ref[0:1, cs], caw_ref[1:2, cs], caw_ref[2:3, cs]
            cz = w0 * z2 + w1 * z1 + w2 * z
            dya = dya_ref[:, cs].astype(F32)
            dp_ref[:, 0 * d + j * cw:0 * d + (j + 1) * cw] = (dya * cz).astype(BF16)
            dcz = dya * ba_ref[:, cs].astype(F32)
            dczbuf[0:tm, :] = dcz
            dczbuf[tm:, :] = dcz_head[:, cs]
            dcz_head[:, cs] = dcz[0:SUBLANE, :]
            dz = w2 * dcz + w1 * dczbuf[pl.ds(1, tm), :] + w0 * dczbuf[pl.ds(2, tm), :]
            dv_ref[DV_CONV_A + 0:DV_CONV_A + 1, cs] += colsum(dcz * z2)
            dv_ref[DV_CONV_A + 1:DV_CONV_A + 2, cs] += colsum(dcz * z1)
            dv_ref[DV_CONV_A + 2:DV_CONV_A + 3, cs] += colsum(dcz * z)
            dp_ref[:, 1 * d + j * cw:1 * d + (j + 1) * cw] = (dz * xa).astype(BF16)
            dp_ref[:, 2 * d + j * cw:2 * d + (j + 1) * cw] = (dz * ca).astype(BF16)
            h = h_ref[:, cs].astype(F32)
            hbuf[0:SUBLANE, :] = hh_ref[:, cs].astype(F32) * has_prev
            hbuf[SUBLANE:, :] = h
            h_prev = hbuf[pl.ds(SUBLANE - 1, tm), :]
            dyb = dyb_ref[:, cs].astype(F32)
            gel, dgel = _gelu_and_grad(gb_ref[:, cs].astype(F32))
            dp_ref[:, 4 * d + j * cw:4 * d + (j + 1) * cw] = (dyb * h * dgel).astype(BF16)
            ub = u_ref[:, cs]
            u = ub.astype(F32)
            sp = vec_ref[3:4, cs]
            r, gi, a, s = _lru_gates(ub, u, bda_ref[j], bdx_ref[j], vec_ref[1:2, cs], vec_ref[2:3, cs], sp)
            a2buf[0:tm, :] = a
            a2buf[tm:, :] = a_head[:, cs]
            a1buf[...] = a2buf[pl.ds(1, tm), :]
            lbuf[...] = dyb * gel

            def block(kk, carry):
                r0 = pl.multiple_of((hb - 1 - kk) * SUBLANE, SUBLANE)
                ak = a1buf[pl.ds(r0, SUBLANE), :]
                bk = lbuf[pl.ds(r0, SUBLANE), :]
                for sh in (1, 2, 4):
                    m = row < SUBLANE - sh
                    bk = jnp.where(m, ak * pltpu.roll(bk, SUBLANE - sh, 0) + bk, bk)
                    ak = jnp.where(m, ak * pltpu.roll(ak, SUBLANE - sh, 0), ak)
                lk = bk + ak * carry
                lbuf[pl.ds(r0, SUBLANE), :] = lk
                return lk[0:1, :]

            lam0 = lax.fori_loop(0, hb, block, lam_head[0:1, cs], unroll=4)
            lam_head[:, cs] = jnp.broadcast_to(lam0, (SUBLANE, cw))
            a_head[:, cs] = jnp.broadcast_to(a[0:1, :], (SUBLANE, cw))
            lam = lbuf[...]
            da = lam * h_prev
            iu = gi * u
            ds = lam * iu
            di = lam * s * u
            du = lam * s * gi
            dlog_a = da * a - ds * (a * a) / s
            dv_ref[DV_SP:DV_SP + 1, cs] += colsum(dlog_a * r) * (-LRU_C)
            dpr = dlog_a * ((-LRU_C) * sp) * r * (1.0 - r)
            dpi = di * gi * (1.0 - gi)
            dv_ref[DV_BA:DV_BA + 1, cs] += colsum(dpr)
            dv_ref[DV_BX:DV_BX + 1, cs] += colsum(dpi)
            dprb = dpr.astype(BF16)
            dpib = dpi.astype(BF16)
            du = du + _dot_nt(dprb, bda_ref[j]) + _dot_nt(dpib, bdx_ref[j])
            dwa_ref[j] += _dot_tn(ub, dprb)
            dwx_ref[j] += _dot_tn(ub, dpib)
            xb = xb_ref[:, cs].astype(F32)
            xbuf[0:SUBLANE, :] = xbh_ref[:, cs].astype(F32) * has_prev
            xbuf[SUBLANE:, :] = xb
            dubuf[0:tm, :] = du
            dubuf[tm:, :] = du_head[:, cs]
            du_head[:, cs] = du[0:SUBLANE, :]
            v0, v1, v2, v3 = cbw_ref[0:1, cs], cbw_ref[1:2, cs], cbw_ref[2:3, cs], cbw_ref[3:4, cs]
            dxb = v3 * du + v2 * dubuf[pl.ds(1, tm), :] + v1 * dubuf[pl.ds(2, tm), :] + v0 * dubuf[pl.ds(3, tm), :]
            dp_ref[:, 3 * d + j * cw:3 * d + (j + 1) * cw] = dxb.astype(BF16)
            dv_ref[DV_CONV_B_B:DV_CONV_B_B + 1, cs] += colsum(du)
            dv_ref[DV_CONV_B + 0:DV_CONV_B + 1, cs] += colsum(du * xbuf[pl.ds(SUBLANE - 3, tm), :])
            dv_ref[DV_CONV_B + 1:DV_CONV_B + 2, cs] += colsum(du * xbuf[pl.ds(SUBLANE - 2, tm), :])
            dv_ref[DV_CONV_B + 2:DV_CONV_B + 3, cs] += colsum(du * xbuf[pl.ds(SUBLANE - 1, tm), :])
            dv_ref[DV_CONV_B + 3:DV_CONV_B + 4, cs] += colsum(du * xb)

    rt = lambda i: n_t - 1 - i
    row_spec = pl.BlockSpec((tm, d), lambda i: (rt(i), 0))
    slab = lambda s: pl.BlockSpec((tm, d), lambda i, s=s: (rt(i), s))
    halo = lambda s: pl.BlockSpec((SUBLANE, d), lambda i, s=s: (jnp.maximum(rt(i) * hb - 1, 0), s))
    small = pl.BlockSpec((SUBLANE, d), lambda i: (0, 0))
    bd = pl.BlockSpec((nb, cw, cw), lambda i: (0, 0, 0))
    sd = jax.ShapeDtypeStruct
    wbuf = lambda: pltpu.VMEM((tm + SUBLANE, cw), F32)
    head = lambda: pltpu.VMEM((SUBLANE, d), F32)
    return pl.pallas_call(
        body, name=name, grid=(n_t,),
        out_shape=[sd((t, N_PROJ * d), BF16), sd((DV_ROWS, d), F32), sd((nb, cw, cw), F32), sd((nb, cw, cw), F32)],
        in_specs=[row_spec, row_spec, pl.BlockSpec((tm, 2 * d), lambda i: (rt(i), 0)),
                  slab(0), slab(1), slab(2), slab(3), slab(4), halo(1), halo(2), halo(3),
                  row_spec, row_spec, halo(0), small, small, small, bd, bd],
        out_specs=[pl.BlockSpec((tm, N_PROJ * d), lambda i: (rt(i), 0)), pl.BlockSpec((DV_ROWS, d), lambda i: (0, 0)), bd, bd],
        scratch_shapes=[wbuf(), wbuf(), wbuf(), wbuf(), wbuf(), wbuf(),
                        pltpu.VMEM((tm, cw), F32), pltpu.VMEM((tm, cw), F32), head(), head(), head(), head()],
        compiler_params=_params(),
    )(dya, dyb, dpg, p, p, p, p, p, p, p, p, u_s, h_s, h_s, caw, cbw, vec, bda, bdx)


def _in_proj_bwd(dp, x, dx1, g_row, gath, layer, lay, name):
    t, d = x.shape
    tm = _row_tile(t, 256)
    n_in = N_PROJ * d

    def body(dp_ref, x_ref, dx1_ref, g_ref, gath_ref, dx_ref, h_ref, dg_ref, w_in, sems):
        @pl.when(pl.program_id(0) == 0)
        def _():
            _load_weights(gath_ref, layer, lay, ["in"], [w_in], sems)
            _zero(dg_ref)

        dh = _dot_nn(dp_ref[:, 0:d], w_in[0:d, :])
        for k in range(1, N_PROJ):
            dh = dh + _dot_nn(dp_ref[:, k * d:(k + 1) * d], w_in[k * d:(k + 1) * d, :])
        xf = x_ref[...]
        rstd = lax.rsqrt(jnp.mean(xf * xf, axis=-1, keepdims=True) + RMS_EPS)
        xh = xf * rstd
        g = g_ref[...]
        h_ref[...] = (xh * g).astype(BF16)
        dg_ref[0:1, :] += jnp.sum(dh * xh, axis=0, keepdims=True)
        dxh = dh * g
        dx_ref[...] = dx1_ref[...] + rstd * (dxh - xh * jnp.mean(dxh * xh, axis=-1, keepdims=True))

    row = pl.BlockSpec((tm, d), lambda i: (i, 0))
    sd = jax.ShapeDtypeStruct
    return pl.pallas_call(
        body, name=name, grid=(t // tm,),
        out_shape=[sd((t, d), F32), sd((t, d), BF16), sd((SUBLANE, d), F32)],
        in_specs=[pl.BlockSpec((tm, n_in), lambda i: (i, 0)), row, row, pl.BlockSpec((1, d), lambda i: (0, 0)), ANY],
        out_specs=[row, row, pl.BlockSpec((SUBLANE, d), lambda i: (0, 0))],
        scratch_shapes=[pltpu.VMEM((n_in, d), BF16), pltpu.SemaphoreType.DMA((N_DEV,))],
        compiler_params=_params(),
    )(dp, x, dx1, g_row, gath)


def _weight_grad(a, b, name):
    t, m = a.shape
    n = b.shape[1]
    bt = _row_tile(t, 512)
    bm = m
    for div in (1, 2, 4, 8):
        if m % div == 0 and (m // div) % LANE == 0 and (m // div) * n * 4 <= (12 << 20):
            bm = m // div
            break
    n_t = t // bt

    def body(a_ref, b_ref, o_ref, acc):
        k = pl.program_id(1)

        @pl.when(k == 0)
        def _():
            _zero(acc)

        acc[...] += _dot_tn(a_ref[...], b_ref[...])

        @pl.when(k == n_t - 1)
        def _():
            o_ref[...] = acc[...].astype(BF16)

    return pl.pallas_call(
        body, name=name, grid=(m // bm, n_t),
        out_shape=jax.ShapeDtypeStruct((m, n), BF16),
        in_specs=[pl.BlockSpec((bt, bm), lambda i, k: (k, i)), pl.BlockSpec((bt, n), lambda i, k: (k, 0))],
        out_specs=pl.BlockSpec((bm, n), lambda i, k: (i, 0)),
        scratch_shapes=[pltpu.VMEM((bm, n), F32)],
        compiler_params=_params(2),
    )(a, b)


def _adamw(w, g, m, v, name):
    r, c = w.shape
    tr = _fit_rows(r, c * 4)
    c1 = 1.0 - ADAM_B1 ** ADAM_STEP
    c2 = 1.0 - ADAM_B2 ** ADAM_STEP

    def body(w_ref, g_ref, m_ref, v_ref, d_ref, nm_ref, nv_ref):
        g32 = g_ref[...]
        nm = ADAM_B1 * m_ref[...] + (1.0 - ADAM_B1) * g32
        nv = ADAM_B2 * v_ref[...] + (1.0 - ADAM_B2) * (g32 * g32)
        nm_ref[...] = nm
        nv_ref[...] = nv
        d_ref[...] = -ADAM_LR * ((nm / c1) / (jnp.sqrt(nv / c2) + ADAM_EPS) + ADAM_WD * w_ref[...])

    spec = pl.BlockSpec((tr, c), lambda i: (i, 0))
    return pl.pallas_call(
        body, name=name, grid=(r // tr,),
        out_shape=[jax.ShapeDtypeStruct((r, c), F32)] * 3,
        in_specs=[spec] * 4, out_specs=[spec] * 3,
        compiler_params=_params(),
    )(w, g, m, v)


def _pad_rows(a, mult=SUBLANE):
    pad = (-a.shape[0]) % mult
    return a if pad == 0 else jnp.concatenate([a, jnp.zeros((pad,) + a.shape[1:], a.dtype)], axis=0)


REPLICATED = ("ln1_g", "conv_b_b", "lru_wa", "lru_ba", "lru_wx", "lru_bx", "lru_lambda", "ln2_g", "final_g")
SMALL_SHARDED = ("conv_a_w", "conv_b_w", "gate_bias")
MATRICES = ("w_in", "w_out_a", "w_out_b", "w_o", "w_ffn_gate", "w_ffn_up", "w_ffn_down")
ORDER = ("ln1_g", "w_in", "conv_a_w", "conv_b_w", "conv_b_b", "lru_wa", "lru_ba", "lru_wx", "lru_bx", "lru_lambda",
         "w_out_a", "w_out_b", "gate_bias", "w_o", "ln2_g", "w_ffn_gate", "w_ffn_up", "w_ffn_down", "final_g")


def kernel(x, ln1_g, w_in, conv_a_w, conv_b_w, conv_b_b, lru_wa, lru_ba, lru_wx, lru_bx, lru_lambda, w_out_a, w_out_b, gate_bias, w_o, ln2_g, w_ffn_gate, w_ffn_up, w_ffn_down, final_g, loss_target, m_ln1_g, m_w_in, m_conv_a_w, m_conv_b_w, m_conv_b_b, m_lru_wa, m_lru_ba, m_lru_wx, m_lru_bx, m_lru_lambda, m_w_out_a, m_w_out_b, m_gate_bias, m_w_o, m_ln2_g, m_w_ffn_gate, m_w_ffn_up, m_w_ffn_down, m_final_g, v_ln1_g, v_w_in, v_conv_a_w, v_conv_b_w, v_conv_b_b, v_lru_wa, v_lru_ba, v_lru_wx, v_lru_bx, v_lru_lambda, v_w_out_a, v_w_out_b, v_gate_bias, v_w_o, v_ln2_g, v_w_ffn_gate, v_w_ffn_up, v_w_ffn_down, v_final_g):
    w = dict(ln1_g=ln1_g, w_in=w_in, conv_a_w=conv_a_w, conv_b_w=conv_b_w, conv_b_b=conv_b_b, lru_wa=lru_wa,
             lru_ba=lru_ba, lru_wx=lru_wx, lru_bx=lru_bx, lru_lambda=lru_lambda, w_out_a=w_out_a, w_out_b=w_out_b,
             gate_bias=gate_bias, w_o=w_o, ln2_g=ln2_g, w_ffn_gate=w_ffn_gate, w_ffn_up=w_ffn_up,
             w_ffn_down=w_ffn_down, final_g=final_g)
    mom = dict(ln1_g=m_ln1_g, w_in=m_w_in, conv_a_w=m_conv_a_w, conv_b_w=m_conv_b_w, conv_b_b=m_conv_b_b,
               lru_wa=m_lru_wa, lru_ba=m_lru_ba, lru_wx=m_lru_wx, lru_bx=m_lru_bx, lru_lambda=m_lru_lambda,
               w_out_a=m_w_out_a, w_out_b=m_w_out_b, gate_bias=m_gate_bias, w_o=m_w_o, ln2_g=m_ln2_g,
               w_ffn_gate=m_w_ffn_gate, w_ffn_up=m_w_ffn_up, w_ffn_down=m_w_ffn_down, final_g=m_final_g)
    var = dict(ln1_g=v_ln1_g, w_in=v_w_in, conv_a_w=v_conv_a_w, conv_b_w=v_conv_b_w, conv_b_b=v_conv_b_b,
               lru_wa=v_lru_wa, lru_ba=v_lru_ba, lru_wx=v_lru_wx, lru_bx=v_lru_bx, lru_lambda=v_lru_lambda,
               w_out_a=v_w_out_a, w_out_b=v_w_out_b, gate_bias=v_gate_bias, w_o=v_w_o, ln2_g=v_ln2_g,
               w_ffn_gate=v_w_ffn_gate, w_ffn_up=v_w_ffn_up, w_ffn_down=v_w_ffn_down, final_g=v_final_g)

    _, t, d = x.shape
    n_layers = w_in.shape[0]
    ff = w_ffn_down.shape[1] * N_DEV
    dd = d // N_DEV
    hd = d // LRU_HEADS
    cw = min(MXU_TILE, d)
    nb = d // cw
    hpt = cw // hd
    lay = _Layout(d, ff)
    me = 4 * lax.axis_index("x") + 2 * lax.axis_index("y") + lax.axis_index("c")
    x0 = x[0]
    target = loss_target[0]

    packed = jnp.concatenate(
        [jnp.swapaxes(w_in, 1, 2), w_out_a, w_out_b, w_o, jnp.swapaxes(w_ffn_gate, 1, 2),
         jnp.swapaxes(w_ffn_up, 1, 2), w_ffn_down], axis=1).astype(BF16)
    gath = _all_gather(packed, "gather_weights")
    n_small = CONV_A_K + CONV_B_K + 2
    small = jnp.concatenate([conv_a_w, conv_b_w, gate_bias], axis=1).reshape(n_layers * n_small, dd)
    small_g = _all_gather(_pad_rows(small), "gather_vectors")
    small_full = jnp.swapaxes(small_g[:, :n_layers * n_small], 0, 1).reshape(n_layers, n_small, d)
    caw = [_pad_rows(small_full[l, 0:CONV_A_K]) for l in range(n_layers)]
    cbw = [_pad_rows(small_full[l, CONV_A_K:CONV_A_K + CONV_B_K]) for l in range(n_layers)]
    gbias = [_pad_rows(small_full[l, CONV_A_K + CONV_B_K:]) for l in range(n_layers)]
    sp = jax.nn.softplus(-lru_lambda)
    vec = [_pad_rows(jnp.stack([conv_b_b[l], lru_ba[l], lru_bx[l], sp[l]])) for l in range(n_layers)]
    eye = jnp.eye(hpt, dtype=F32)

    def block_diag(wh):
        return jnp.einsum("jkab,kl->jkalb", wh.reshape(nb, hpt, hd, hd), eye).reshape(nb, cw, cw).astype(BF16)

    bda = [block_diag(lru_wa[l]) for l in range(n_layers)]
    bdx = [block_diag(lru_wx[l]) for l in range(n_layers)]

    saved = []
    xl = x0
    for l in range(n_layers):
        p = _in_proj_fwd(xl, ln1_g[l][None], gath, l, lay, f"in_proj_fwd_{l}")
        ya, yb, u_s, h_s = _mixer_fwd(p, caw[l], cbw[l], vec[l], bda[l], bdx[l], f"mixer_fwd_{l}")
        x1, oa, ob = _merge_fwd(xl, ya, yb, p, gbias[l], gath, l, lay, f"merge_fwd_{l}")
        x2, gg, uu = _ffn_fwd(x1, ln2_g[l][None], gath, l, lay, f"ffn_fwd_{l}")
        saved.append(dict(x=xl, p=p, ya=ya, yb=yb, u=u_s, h=h_s, x1=x1, oa=oa, ob=ob, gg=gg, uu=uu))
        xl = x2
    loss_tile, dx, dfinal = _loss_head(xl, final_g[None], target, "loss_head")
    loss = lax.psum(loss_tile[0, 0], ("x", "y", "c"))

    grads = [None] * n_layers
    small_grads = [None] * n_layers
    for l in reversed(range(n_layers)):
        s = saved[l]
        dx1, dx1b, dgg, duu, f, h2b, dx2b, dln2 = _ffn_bwd(dx, s["x1"], s["gg"], s["uu"], ln2_g[l][None], gath, l, lay,
                                                           f"ffn_bwd_{l}")
        dya, dyb, dpg, mg, doa, dob, dgbias = _merge_bwd(dx1b, s["oa"], s["ob"], s["p"], gbias[l], gath, l, lay,
                                                         f"merge_bwd_{l}")
        dp, dv, dwa, dwx = _mixer_bwd(dya, dyb, dpg, s["p"], s["u"], s["h"], caw[l], cbw[l], vec[l], bda[l], bdx[l],
                                      f"mixer_bwd_{l}")
        dx, h1b, dln1 = _in_proj_bwd(dp, s["x"], dx1, ln1_g[l][None], gath, l, lay, f"in_proj_bwd_{l}")
        grads[l] = {
            "in": _weight_grad(dp, h1b, f"dw_in_{l}"),
            "oa": _weight_grad(s["ya"], doa, f"dw_out_a_{l}"),
            "ob": _weight_grad(s["yb"], dob, f"dw_out_b_{l}"),
            "o": _weight_grad(mg, dx1b, f"dw_o_{l}"),
            "g": _weight_grad(dgg, h2b, f"dw_ffn_gate_{l}"),
            "u": _weight_grad(duu, h2b, f"dw_ffn_up_{l}"),
            "d": _weight_grad(f, dx2b, f"dw_ffn_down_{l}"),
        }

        def heads(dwb):
            blocks = jnp.diagonal(dwb.reshape(nb, hpt, hd, hpt, hd), axis1=1, axis2=3)
            return jnp.moveaxis(blocks, 3, 1).reshape(hd, d)

        small_grads[l] = {
            "ln1_g": dln1[0:1], "conv_b_b": dv[DV_CONV_B_B:DV_CONV_B_B + 1], "lru_wa": heads(dwa),
            "lru_ba": dv[DV_BA:DV_BA + 1], "lru_wx": heads(dwx), "lru_bx": dv[DV_BX:DV_BX + 1],
            "lru_lambda": dv[DV_SP:DV_SP + 1] * (-jax.nn.sigmoid(-lru_lambda[l]))[None], "ln2_g": dln2[0:1],
            "conv_a_w": dv[DV_CONV_A:DV_CONV_A + CONV_A_K], "conv_b_w": dv[DV_CONV_B:DV_CONV_B + CONV_B_K],
            "gate_bias": dgbias[0:2],
        }
    grad_x = dx[None]

    recv = _exchange_grads(grads, lay, "exchange_grads")
    gsum = _sum_slots(recv.reshape(N_DEV, n_layers * lay.total, d), "sum_grads").reshape(n_layers, lay.total, d)

    def part(key):
        return gsum[:, lay.off[key]:lay.off[key] + lay.rows[key]]

    g = {"w_in": jnp.swapaxes(part("in"), 1, 2), "w_out_a": part("oa"), "w_out_b": part("ob"), "w_o": part("o"),
         "w_ffn_gate": jnp.swapaxes(part("g"), 1, 2), "w_ffn_up": jnp.swapaxes(part("u"), 1, 2), "w_ffn_down": part("d")}

    def stack_layers(name):
        return jnp.concatenate([small_grads[l][name] for l in range(n_layers)], axis=0)

    rep_names = [n for n in REPLICATED if n != "final_g"]
    rep_rows = [stack_layers(n) for n in rep_names] + [dfinal[0:1]]
    sh_rows = [stack_layers(n) for n in SMALL_SHARDED]
    n_rep = sum(r.shape[0] for r in rep_rows)
    n_sh = sum(r.shape[0] for r in sh_rows)
    partial = _pad_rows(jnp.concatenate(rep_rows + sh_rows, axis=0))
    small_sum = _sum_slots(_all_gather(partial, "gather_small_grads"), "sum_small_grads")
    o = 0
    for n, r in zip(rep_names + ["final_g"], rep_rows):
        g[n] = small_sum[o:o + r.shape[0]].reshape(w[n].shape)
        o += r.shape[0]
    for n, r in zip(SMALL_SHARDED, sh_rows):
        k = r.shape[0] // n_layers
        g[n] = lax.dynamic_slice_in_dim(small_sum[o:o + r.shape[0]], me * dd, dd, axis=1).reshape(n_layers, k, dd)
        o += r.shape[0]

    delta, new_m, new_v = {}, {}, {}
    for n in MATRICES:
        shape = w[n].shape
        flat = lambda a: a.reshape(shape[0] * shape[1], shape[2])
        dl, nm, nv = _adamw(flat(w[n]), flat(g[n]), flat(mom[n]), flat(var[n]), f"adamw_{n}")
        delta[n], new_m[n], new_v[n] = dl.reshape(shape), nm.reshape(shape), nv.reshape(shape)
    for group, width, name in ((REPLICATED, d, "adamw_replicated"), (SMALL_SHARDED, dd, "adamw_vectors")):
        cat = lambda src: _pad_rows(jnp.concatenate([src[n].reshape(-1, width) for n in group], axis=0))
        dl, nm, nv = _adamw(cat(w), cat(g), cat(mom), cat(var), name)
        o = 0
        for n in group:
            rows = w[n].size // width
            delta[n], new_m[n], new_v[n] = (a[o:o + rows].reshape(w[n].shape) for a in (dl, nm, nv))
            o += rows

    return (loss, grad_x, *[g[n] for n in ORDER], *[delta[n] for n in ORDER], *[new_m[n] for n in ORDER],
            *[new_v[n] for n in ORDER])
```

```python
import math

import jax
import jax.numpy as jnp
from jax import lax
from jax.experimental import pallas as pl
from jax.experimental.pallas import tpu as pltpu

F32 = jnp.float32
BF16 = jnp.bfloat16

N_DEV = 8
N_PROJ = 7
LRU_HEADS = 16
LRU_C = 8.0
RMS_EPS = 1e-6
CONV_A_K = 3
CONV_B_K = 4
GELU_C = math.sqrt(2.0 / math.pi)
GELU_A = 0.044715

ADAM_LR = 0.001
ADAM_B1 = 0.9
ADAM_B2 = 0.999
ADAM_EPS = 1e-08
ADAM_WD = 0.01
ADAM_STEP = 10

LANE = 128
SUBLANE = 8
MXU_TILE = 256
VMEM_LIMIT = 52 << 20
MESH = pl.DeviceIdType.MESH
ANY = pl.BlockSpec(memory_space=pl.ANY)


def _dot_nn(a, b):
    return lax.dot_general(a, b, (((1,), (0,)), ((), ())), preferred_element_type=F32)


def _dot_nt(a, b):
    return lax.dot_general(a, b, (((1,), (1,)), ((), ())), preferred_element_type=F32)


def _dot_tn(a, b):
    return lax.dot_general(a, b, (((0,), (0,)), ((), ())), preferred_element_type=F32)


def _sigmoid(x):
    return 1.0 / (1.0 + jnp.exp(-x))


def _gelu_and_grad(x):
    x2 = x * x
    t = jnp.tanh(GELU_C * x * (1.0 + GELU_A * x2))
    g = 0.5 * x * (1.0 + t)
    dg = 0.5 * (1.0 + t) + 0.5 * x * (1.0 - t * t) * GELU_C * (1.0 + 3.0 * GELU_A * x2)
    return g, dg


def _one_minus_exp(y):
    series = -y * (1.0 + y * (0.5 + y * (1.0 / 6.0 + y * (1.0 / 24.0))))
    return jnp.where(y > -0.01, series, 1.0 - jnp.exp(y))


def _zero(ref):
    ref[...] = jnp.zeros(ref.shape, ref.dtype)


def _fit_rows(r, row_bytes, budget=1 << 20):
    fits = [t for t in range(16, r + 1, 16) if r % t == 0 and t * row_bytes <= budget]
    return max(fits) if fits else r


def _row_tile(t, want):
    tm = min(want, t // 2)
    assert t % tm == 0 and tm % SUBLANE == 0, (t, tm)
    return tm


def _params(n_grid=1, **kw):
    return pltpu.CompilerParams(dimension_semantics=("arbitrary",) * n_grid, vmem_limit_bytes=VMEM_LIMIT, **kw)


class _Group:
    def __init__(self, keys, rows):
        self.keys = keys
        self.rows = dict(zip(keys, rows))
        self.off, o = {}, 0
        for k in keys:
            self.off[k] = o
            o += self.rows[k]
        self.total = o


def _groups(d, ff):
    dd, ffs = d // N_DEV, ff // N_DEV
    return {"in": _Group(("in",), (N_PROJ * dd,)),
            "rest": _Group(("oa", "ob", "o", "g", "u", "d"), (dd, dd, dd, ffs, ffs, ffs))}


def _load_weights(g_ref, grp, keys, dsts, sems):
    copies = []
    for n, (k, dst) in enumerate(zip(keys, dsts)):
        rows, off = grp.rows[k], grp.off[k]
        copies += [pltpu.make_async_copy(g_ref.at[p, pl.ds(off, rows), :], dst.at[pl.ds(p * rows, rows), :],
                                         sems.at[n * N_DEV + p]) for p in range(N_DEV)]
    for c in copies:
        c.start()
    for c in copies:
        c.wait()


def _comm_sems():
    return [pltpu.SemaphoreType.DMA((N_DEV - 1,)), pltpu.SemaphoreType.DMA((N_DEV - 1,)), pltpu.SemaphoreType.DMA]


class _Gather:
    def __init__(self, x):
        self.inputs = [x]
        self.out_shape = [jax.ShapeDtypeStruct((N_DEV,) + x.shape, x.dtype)]
        self.scratch = _comm_sems()

    def _plan(self, ins, outs, scr):
        (x_ref,), (out_ref,), (send_sems, recv_sems, local_sem) = ins, outs, scr
        mx, my, mc = lax.axis_index("x"), lax.axis_index("y"), lax.axis_index("c")
        me, sibling = (mx, my, mc), (mx, my, 1 - mc)
        chips = [(1 - mx, my), (mx, 1 - my), (1 - mx, 1 - my)]

        def slot(px, py, pc):
            return out_ref.at[4 * px + 2 * py + pc]

        def copy(k, block, to, src=None):
            return pltpu.make_async_remote_copy(
                src_ref=slot(*block) if src is None else src, dst_ref=slot(*block),
                send_sem=send_sems.at[k], recv_sem=recv_sems.at[k], device_id=to, device_id_type=MESH)

        mine = lambda: pltpu.make_async_copy(x_ref, slot(*me), local_sem)
        first = [lambda: copy(0, me, sibling, src=x_ref)]
        first += [lambda j=j, chip=chip: copy(1 + j, me, (*chip, mc), src=x_ref) for j, chip in enumerate(chips)]
        landed = [lambda j=j, chip=chip: copy(1 + j, (*chip, mc), me) for j, chip in enumerate(chips)]
        passed = [lambda j=j, chip=chip: copy(4 + j, (*chip, mc), sibling) for j, chip in enumerate(chips)]
        from_sibling = [lambda: copy(0, sibling, me)]
        from_sibling += [lambda j=j, chip=chip: copy(4 + j, (*chip, 1 - mc), me) for j, chip in enumerate(chips)]
        return mine, first, landed, passed, from_sibling

    def start(self, ins, outs, scr):
        mine, first, _, _, _ = self._plan(ins, outs, scr)
        mine().start()
        for cp in first:
            cp().start()

    def mid(self, ins, outs, scr):
        _, _, landed, passed, _ = self._plan(ins, outs, scr)
        for got, fwd in zip(landed, passed):
            got().wait_recv()
            fwd().start()

    def finish(self, ins, outs, scr):
        mine, first, _, passed, from_sibling = self._plan(ins, outs, scr)
        for cp in from_sibling:
            cp().wait_recv()
        for cp in first + passed:
            cp().wait_send()
        mine().wait()


class _Exchange:
    def __init__(self, mats, grp):
        self.grp = grp
        self.inputs = [mats[k] for k in grp.keys]
        self.out_shape = [jax.ShapeDtypeStruct((N_DEV, grp.total, self.inputs[0].shape[1]), BF16)]
        self.scratch = _comm_sems()

    def _pieces(self, g_refs, out_ref, q, dst_slot):
        out = []
        for g_ref, k in zip(g_refs, self.grp.keys):
            rows = self.grp.rows[k]
            out.append((g_ref.at[pl.ds(pl.multiple_of(q * rows, 16), rows), :],
                        out_ref.at[dst_slot, pl.ds(self.grp.off[k], rows), :]))
        return out

    def start(self, ins, outs, scr):
        (out_ref,), (send_sems, recv_sems, local_sem) = outs, scr
        mx, my, mc = lax.axis_index("x"), lax.axis_index("y"), lax.axis_index("c")
        me = 4 * mx + 2 * my + mc
        for s, t in self._pieces(ins, out_ref, me, me):
            pltpu.make_async_copy(s, t, local_sem).start()
        for k in range(1, N_DEV):
            px, py, pc = mx ^ ((k >> 2) & 1), my ^ ((k >> 1) & 1), mc ^ (k & 1)
            for s, t in self._pieces(ins, out_ref, 4 * px + 2 * py + pc, me):
                pltpu.make_async_remote_copy(src_ref=s, dst_ref=t, send_sem=send_sems.at[k - 1],
                                             recv_sem=recv_sems.at[k - 1], device_id=(px, py, pc),
                                             device_id_type=MESH).start()

    def mid(self, ins, outs, scr):
        pass

    def finish(self, ins, outs, scr):
        (out_ref,), (send_sems, recv_sems, local_sem) = outs, scr
        mx, my, mc = lax.axis_index("x"), lax.axis_index("y"), lax.axis_index("c")
        whole = out_ref.at[0]
        for k in range(1, N_DEV):
            done = pltpu.make_async_remote_copy(src_ref=whole, dst_ref=whole, send_sem=send_sems.at[k - 1],
                                                recv_sem=recv_sems.at[k - 1], device_id=(mx, my, mc),
                                                device_id_type=MESH)
            done.wait_send()
            done.wait_recv()
        pltpu.make_async_copy(whole, whole, local_sem).wait()


def _split(refs, sizes):
    out, pos = [], 0
    for n in sizes:
        out.append(refs[pos:pos + n])
        pos += n
    return out


def _hosted_call(body, comms, *, name, grid, in_specs, out_specs, out_shape, scratch_shapes, args):
    n_steps = grid[0]
    nc = len(comms)
    sizes = ([len(in_specs)] + [len(c.inputs) for c in comms] + [len(out_specs)] + [len(c.out_shape) for c in comms]
             + [len(scratch_shapes)] + [len(c.scratch) for c in comms])

    def hosted(*refs):
        parts = _split(refs, sizes)
        ins, c_ins = parts[0], parts[1:1 + nc]
        outs, c_outs = parts[1 + nc], parts[2 + nc:2 + 2 * nc]
        scr, c_scr = parts[2 + 2 * nc], parts[3 + 2 * nc:]
        step = pl.program_id(0)
        if comms:
            @pl.when(step == 0)
            def _():
                for c, a, b, s in zip(comms, c_ins, c_outs, c_scr):
                    c.start(a, b, s)

            @pl.when(step == max(n_steps - 2, 0))
            def _():
                for c, a, b, s in zip(comms, c_ins, c_outs, c_scr):
                    c.mid(a, b, s)

        body(*ins, *outs, *scr)
        if comms:
            @pl.when(step == n_steps - 1)
            def _():
                for c, a, b, s in zip(comms, c_ins, c_outs, c_scr):
                    c.finish(a, b, s)

    res = pl.pallas_call(
        hosted, name=name, grid=grid,
        out_shape=[*out_shape, *[o for c in comms for o in c.out_shape]],
        in_specs=[*in_specs, *[ANY for c in comms for _ in c.inputs]],
        out_specs=[*out_specs, *[ANY for c in comms for _ in c.out_shape]],
        scratch_shapes=[*scratch_shapes, *[s for c in comms for s in c.scratch]],
        compiler_params=_params(),
    )(*args, *[a for c in comms for a in c.inputs])
    main, rest = res[:len(out_specs)], res[len(out_specs):]
    return main, _split(rest, [len(c.out_shape) for c in comms])


def _comm_call(comm, name):
    def body(*refs):
        ins, outs, scr = _split(refs, [len(comm.inputs), len(comm.out_shape), len(comm.scratch)])
        comm.start(ins, outs, scr)
        comm.mid(ins, outs, scr)
        comm.finish(ins, outs, scr)

    return pl.pallas_call(
        body, name=name, out_shape=comm.out_shape, in_specs=[ANY] * len(comm.inputs),
        out_specs=[ANY] * len(comm.out_shape), scratch_shapes=comm.scratch,
    )(*comm.inputs)


def _sum_slots(x, name):
    n, r, c = x.shape
    tr = _fit_rows(r, c * 4)

    def body(x_ref, o_ref):
        acc = x_ref[0].astype(F32)
        for p in range(1, n):
            acc = acc + x_ref[p].astype(F32)
        o_ref[...] = acc

    return pl.pallas_call(
        body, name=name, grid=(r // tr,),
        out_shape=jax.ShapeDtypeStruct((r, c), F32),
        in_specs=[pl.BlockSpec((n, tr, c), lambda i: (0, i, 0))],
        out_specs=pl.BlockSpec((tr, c), lambda i: (i, 0)),
        compiler_params=_params(),
    )(x)


def _in_proj_fwd(x, g_row, gath, grp, name, comms=()):
    t, d = x.shape
    tm = _row_tile(t, 256)
    n_in = N_PROJ * d

    def body(x_ref, g_ref, gath_ref, p_ref, h_ref, w_in, sems):
        @pl.when(pl.program_id(0) == 0)
        def _():
            _load_weights(gath_ref, grp, ["in"], [w_in], sems)

        xf = x_ref[...]
        rstd = lax.rsqrt(jnp.mean(xf * xf, axis=-1, keepdims=True) + RMS_EPS)
        h = (xf * rstd * g_ref[...]).astype(BF16)
        h_ref[...] = h
        for k in range(N_PROJ):
            p_ref[:, k * d:(k + 1) * d] = _dot_nt(h, w_in[k * d:(k + 1) * d, :]).astype(BF16)

    row = pl.BlockSpec((tm, d), lambda i: (i, 0))
    return _hosted_call(
        body, comms, name=name, grid=(t // tm,),
        out_shape=[jax.ShapeDtypeStruct((t, n_in), BF16), jax.ShapeDtypeStruct((t, d), BF16)],
        in_specs=[row, pl.BlockSpec((1, d), lambda i: (0, 0)), ANY],
        out_specs=[pl.BlockSpec((tm, n_in), lambda i: (i, 0)), row],
        scratch_shapes=[pltpu.VMEM((n_in, d), BF16), pltpu.SemaphoreType.DMA((N_DEV,))],
        args=(x, g_row, gath))


def _lru_gates(ub, bda, bdx, ba, bx, sp):
    r = _sigmoid(_dot_nn(ub, bda) + ba)
    i = _sigmoid(_dot_nn(ub, bdx) + bx)
    log_a = (-LRU_C) * r * sp
    a = jnp.exp(log_a)
    s = jnp.sqrt(_one_minus_exp(2.0 * log_a))
    return r, i, a, s


def _mixer_fwd(p, caw, cbw, vec, bda, bdx, name, comms=()):
    t = p.shape[0]
    d = p.shape[1] // N_PROJ
    tm = _row_tile(t, 256)
    cw = min(MXU_TILE, d)
    nb = d // cw

    def body(ba_ref, ca_ref, xa_ref, xb_ref, gb_ref, caw_ref, cbw_ref, vec_ref, bda_ref, bdx_ref,
             ya_ref, yb_ref, u_ref, h_ref, zbuf, xbuf, abuf, bbuf, z_tail, x_tail, h_carry):
        @pl.when(pl.program_id(0) == 0)
        def _():
            _zero(z_tail)
            _zero(x_tail)
            _zero(h_carry)

        row = lax.broadcasted_iota(jnp.int32, (SUBLANE, cw), 0)
        for j in range(nb):
            cs = slice(j * cw, (j + 1) * cw)
            z = ca_ref[:, cs].astype(F32) * xa_ref[:, cs].astype(F32)
            zbuf[0:SUBLANE, :] = z_tail[:, cs]
            zbuf[SUBLANE:, :] = z
            z_tail[:, cs] = z[tm - SUBLANE:, :]
            cz = (caw_ref[0:1, cs] * zbuf[pl.ds(SUBLANE - 2, tm), :] + caw_ref[1:2, cs] * zbuf[pl.ds(SUBLANE - 1, tm), :]
                  + caw_ref[2:3, cs] * z)
            ya_ref[:, cs] = (ba_ref[:, cs].astype(F32) * cz).astype(BF16)
            xb = xb_ref[:, cs].astype(F32)
            xbuf[0:SUBLANE, :] = x_tail[:, cs]
            xbuf[SUBLANE:, :] = xb
            x_tail[:, cs] = xb[tm - SUBLANE:, :]
            u = (cbw_ref[0:1, cs] * xbuf[pl.ds(SUBLANE - 3, tm), :] + cbw_ref[1:2, cs] * xbuf[pl.ds(SUBLANE - 2, tm), :]
                 + cbw_ref[2:3, cs] * xbuf[pl.ds(SUBLANE - 1, tm), :] + cbw_ref[3:4, cs] * xb + vec_ref[0:1, cs])
            ub = u.astype(BF16)
            u = ub.astype(F32)
            _, gi, a, s = _lru_gates(ub, bda_ref[j], bdx_ref[j], vec_ref[1:2, cs], vec_ref[2:3, cs], vec_ref[3:4, cs])
            abuf[...] = a
            bbuf[...] = s * (gi * u)

            def block(k, carry):
                r0 = pl.multiple_of(k * SUBLANE, SUBLANE)
                ak = abuf[pl.ds(r0, SUBLANE), :]
                bk = bbuf[pl.ds(r0, SUBLANE), :]
                for sh in (1, 2, 4):
                    m = row >= sh
                    bk = jnp.where(m, ak * pltpu.roll(bk, sh, 0) + bk, bk)
                    ak = jnp.where(m, ak * pltpu.roll(ak, sh, 0), ak)
                hk = bk + ak * carry
                bbuf[pl.ds(r0, SUBLANE), :] = hk
                return hk[SUBLANE - 1:SUBLANE, :]

            h_carry[0:1, cs] = lax.fori_loop(0, tm // SUBLANE, block, h_carry[0:1, cs], unroll=4)
            h = bbuf[...]
            gel, _ = _gelu_and_grad(gb_ref[:, cs].astype(F32))
            yb_ref[:, cs] = (h * gel).astype(BF16)
            u_ref[:, cs] = ub
            h_ref[:, cs] = h.astype(BF16)

    slab = lambda s: pl.BlockSpec((tm, d), lambda i, s=s: (i, s))
    small = pl.BlockSpec((SUBLANE, d), lambda i: (0, 0))
    bd = pl.BlockSpec((nb, cw, cw), lambda i: (0, 0, 0))
    out = pl.BlockSpec((tm, d), lambda i: (i, 0))
    return _hosted_call(
        body, comms, name=name, grid=(t // tm,),
        out_shape=[jax.ShapeDtypeStruct((t, d), BF16)] * 4,
        in_specs=[slab(0), slab(1), slab(2), slab(3), slab(4), small, small, small, bd, bd],
        out_specs=[out] * 4,
        scratch_shapes=[pltpu.VMEM((tm + SUBLANE, cw), F32), pltpu.VMEM((tm + SUBLANE, cw), F32),
                        pltpu.VMEM((tm, cw), F32), pltpu.VMEM((tm, cw), F32),
                        pltpu.VMEM((SUBLANE, d), F32), pltpu.VMEM((SUBLANE, d), F32), pltpu.VMEM((SUBLANE, d), F32)],
        args=(p, p, p, p, p, caw, cbw, vec, bda, bdx))


def _merge_fwd(x, ya, yb, p, gbias, gath, grp, name):
    t, d = x.shape
    tm = _row_tile(t, 512)

    def body(x_ref, ya_ref, yb_ref, ga_ref, gb_ref, gbias_ref, gath_ref, x1_ref, oa_ref, ob_ref, w_oa, w_ob, w_o, sems):
        @pl.when(pl.program_id(0) == 0)
        def _():
            _load_weights(gath_ref, grp, ["oa", "ob", "o"], [w_oa, w_ob, w_o], sems)

        oa = _dot_nn(ya_ref[...], w_oa[...]).astype(BF16)
        ob = _dot_nn(yb_ref[...], w_ob[...]).astype(BF16)
        oa_ref[...] = oa
        ob_ref[...] = ob
        sa = _sigmoid(ga_ref[...].astype(F32) + gbias_ref[0:1, :])
        sb = _sigmoid(gb_ref[...].astype(F32) + gbias_ref[1:2, :])
        merged = (sa * oa.astype(F32) + sb * ob.astype(F32)).astype(BF16)
        x1_ref[...] = x_ref[...] + _dot_nn(merged, w_o[...])

    row = pl.BlockSpec((tm, d), lambda i: (i, 0))
    return pl.pallas_call(
        body, name=name, grid=(t // tm,),
        out_shape=[jax.ShapeDtypeStruct((t, d), F32), jax.ShapeDtypeStruct((t, d), BF16), jax.ShapeDtypeStruct((t, d), BF16)],
        in_specs=[row, row, row, pl.BlockSpec((tm, d), lambda i: (i, 5)), pl.BlockSpec((tm, d), lambda i: (i, 6)),
                  pl.BlockSpec((SUBLANE, d), lambda i: (0, 0)), ANY],
        out_specs=[row, row, row],
        scratch_shapes=[pltpu.VMEM((d, d), BF16)] * 3 + [pltpu.SemaphoreType.DMA((3 * N_DEV,))],
        compiler_params=_params(),
    )(x, ya, yb, p, p, gbias, gath)


def _ffn_fwd(x1, g_row, gath, grp, name):
    t, d = x1.shape
    ff = grp.rows["g"] * N_DEV
    tm = _row_tile(t, 512)
    fc = MXU_TILE
    assert ff % fc == 0

    def body(x_ref, g_ref, gath_ref, x2_ref, gg_ref, uu_ref, w_g, w_u, w_d, acc, sems):
        @pl.when(pl.program_id(0) == 0)
        def _():
            _load_weights(gath_ref, grp, ["g", "u", "d"], [w_g, w_u, w_d], sems)

        xf = x_ref[...]
        rstd = lax.rsqrt(jnp.mean(xf * xf, axis=-1, keepdims=True) + RMS_EPS)
        h = (xf * rstd * g_ref[...]).astype(BF16)
        acc[...] = xf
        for c in range(ff // fc):
            fs = slice(c * fc, (c + 1) * fc)
            gg = _dot_nt(h, w_g[fs, :]).astype(BF16)
            uu = _dot_nt(h, w_u[fs, :]).astype(BF16)
            gg_ref[:, fs] = gg
            uu_ref[:, fs] = uu
            g32 = gg.astype(F32)
            f = (g32 * _sigmoid(g32) * uu.astype(F32)).astype(BF16)
            acc[...] += _dot_nn(f, w_d[fs, :])
        x2_ref[...] = acc[...]

    row = pl.BlockSpec((tm, d), lambda i: (i, 0))
    wide = pl.BlockSpec((tm, ff), lambda i: (i, 0))
    return pl.pallas_call(
        body, name=name, grid=(t // tm,),
        out_shape=[jax.ShapeDtypeStruct((t, d), F32), jax.ShapeDtypeStruct((t, ff), BF16), jax.ShapeDtypeStruct((t, ff), BF16)],
        in_specs=[row, pl.BlockSpec((1, d), lambda i: (0, 0)), ANY],
        out_specs=[row, wide, wide],
        scratch_shapes=[pltpu.VMEM((ff, d), BF16)] * 3 + [pltpu.VMEM((tm, d), F32), pltpu.SemaphoreType.DMA((3 * N_DEV,))],
        compiler_params=_params(),
    )(x1, g_row, gath)


def _loss_head(x, g_row, target, name):
    t, d = x.shape
    tm = _row_tile(t, 512)

    def body(x_ref, g_ref, tgt_ref, loss_ref, dx_ref, dg_ref):
        @pl.when(pl.program_id(0) == 0)
        def _():
            _zero(loss_ref)
            _zero(dg_ref)

        xf = x_ref[...]
        rstd = lax.rsqrt(jnp.mean(xf * xf, axis=-1, keepdims=True) + RMS_EPS)
        xh = xf * rstd
        g = g_ref[...]
        err = xh * g - tgt_ref[...]
        loss_ref[...] += 0.5 * jnp.sum(jnp.sum(err * err, axis=-1, keepdims=True), axis=0, keepdims=True) * (1.0 / d)
        dy = err * (1.0 / d)
        dg_ref[0:1, :] += jnp.sum(dy * xh, axis=0, keepdims=True)
        dxh = dy * g
        dx_ref[...] = rstd * (dxh - xh * jnp.mean(dxh * xh, axis=-1, keepdims=True))

    row = pl.BlockSpec((tm, d), lambda i: (i, 0))
    return pl.pallas_call(
        body, name=name, grid=(t // tm,),
        out_shape=[jax.ShapeDtypeStruct((SUBLANE, LANE), F32), jax.ShapeDtypeStruct((t, d), F32),
                   jax.ShapeDtypeStruct((SUBLANE, d), F32)],
        in_specs=[row, pl.BlockSpec((1, d), lambda i: (0, 0)), row],
        out_specs=[pl.BlockSpec((SUBLANE, LANE), lambda i: (0, 0)), row, pl.BlockSpec((SUBLANE, d), lambda i: (0, 0))],
        compiler_params=_params(),
    )(x, g_row, target)


def _ffn_bwd(dx2, x1, gg, uu, g_row, gath, grp, name, comms=()):
    t, d = x1.shape
    ff = grp.rows["g"] * N_DEV
    tm = _row_tile(t, 256)
    fc = MXU_TILE

    def body(dx2_ref, x_ref, gg_ref, uu_ref, g_ref, gath_ref,
             dx1_ref, dx1b_ref, dgg_ref, duu_ref, f_ref, h_ref, dx2b_ref, dg_ref, w_g, w_u, w_d, acc, sems):
        @pl.when(pl.program_id(0) == 0)
        def _():
            _load_weights(gath_ref, grp, ["g", "u", "d"], [w_g, w_u, w_d], sems)
            _zero(dg_ref)

        dx2 = dx2_ref[...]
        dx2b = dx2.astype(BF16)
        dx2b_ref[...] = dx2b
        _zero(acc)
        for c in range(ff // fc):
            fs = slice(c * fc, (c + 1) * fc)
            df = _dot_nt(dx2b, w_d[fs, :])
            g32 = gg_ref[:, fs].astype(F32)
            u32 = uu_ref[:, fs].astype(F32)
            sg = _sigmoid(g32)
            silu = g32 * sg
            f_ref[:, fs] = (silu * u32).astype(BF16)
            du = (df * silu).astype(BF16)
            dg = (df * u32 * (sg * (1.0 + g32 * (1.0 - sg)))).astype(BF16)
            duu_ref[:, fs] = du
            dgg_ref[:, fs] = dg
            acc[...] += _dot_nn(dg, w_g[fs, :]) + _dot_nn(du, w_u[fs, :])
        xf = x_ref[...]
        rstd = lax.rsqrt(jnp.mean(xf * xf, axis=-1, keepdims=True) + RMS_EPS)
        xh = xf * rstd
        g = g_ref[...]
        h_ref[...] = (xh * g).astype(BF16)
        dh = acc[...]
        dg_ref[0:1, :] += jnp.sum(dh * xh, axis=0, keepdims=True)
        dxh = dh * g
        dx1 = dx2 + rstd * (dxh - xh * jnp.mean(dxh * xh, axis=-1, keepdims=True))
        dx1_ref[...] = dx1
        dx1b_ref[...] = dx1.astype(BF16)

    row = pl.BlockSpec((tm, d), lambda i: (i, 0))
    wide = pl.BlockSpec((tm, ff), lambda i: (i, 0))
    sd = jax.ShapeDtypeStruct
    return _hosted_call(
        body, comms, name=name, grid=(t // tm,),
        out_shape=[sd((t, d), F32), sd((t, d), BF16), sd((t, ff), BF16), sd((t, ff), BF16), sd((t, ff), BF16),
                   sd((t, d), BF16), sd((t, d), BF16), sd((SUBLANE, d), F32)],
        in_specs=[row, row, wide, wide, pl.BlockSpec((1, d), lambda i: (0, 0)), ANY],
        out_specs=[row, row, wide, wide, wide, row, row, pl.BlockSpec((SUBLANE, d), lambda i: (0, 0))],
        scratch_shapes=[pltpu.VMEM((ff, d), BF16)] * 3 + [pltpu.VMEM((tm, d), F32), pltpu.SemaphoreType.DMA((3 * N_DEV,))],
        args=(dx2, x1, gg, uu, g_row, gath))


def _merge_bwd(dx1b, oa, ob, p, gbias, gath, grp, name):
    t, d = oa.shape
    tm = _row_tile(t, 512)

    def body(dx_ref, oa_ref, ob_ref, ga_ref, gb_ref, gbias_ref, gath_ref,
             dya_ref, dyb_ref, dpg_ref, mg_ref, doa_ref, dob_ref, dgb_ref, w_oa, w_ob, w_o, sems):
        @pl.when(pl.program_id(0) == 0)
        def _():
            _load_weights(gath_ref, grp, ["oa", "ob", "o"], [w_oa, w_ob, w_o], sems)
            _zero(dgb_ref)

        dm = _dot_nt(dx_ref[...], w_o[...])
        oa = oa_ref[...].astype(F32)
        ob = ob_ref[...].astype(F32)
        sa = _sigmoid(ga_ref[...].astype(F32) + gbias_ref[0:1, :])
        sb = _sigmoid(gb_ref[...].astype(F32) + gbias_ref[1:2, :])
        mg_ref[...] = (sa * oa + sb * ob).astype(BF16)
        doa = (dm * sa).astype(BF16)
        dob = (dm * sb).astype(BF16)
        doa_ref[...] = doa
        dob_ref[...] = dob
        dga = dm * oa * sa * (1.0 - sa)
        dgb = dm * ob * sb * (1.0 - sb)
        dpg_ref[:, 0:d] = dga.astype(BF16)
        dpg_ref[:, d:2 * d] = dgb.astype(BF16)
        dgb_ref[0:1, :] += jnp.sum(dga, axis=0, keepdims=True)
        dgb_ref[1:2, :] += jnp.sum(dgb, axis=0, keepdims=True)
        dya_ref[...] = _dot_nt(doa, w_oa[...]).astype(BF16)
        dyb_ref[...] = _dot_nt(dob, w_ob[...]).astype(BF16)

    row = pl.BlockSpec((tm, d), lambda i: (i, 0))
    sd = jax.ShapeDtypeStruct
    return pl.pallas_call(
        body, name=name, grid=(t // tm,),
        out_shape=[sd((t, d), BF16), sd((t, d), BF16), sd((t, 2 * d), BF16), sd((t, d), BF16), sd((t, d), BF16),
                   sd((t, d), BF16), sd((SUBLANE, d), F32)],
        in_specs=[row, row, row, pl.BlockSpec((tm, d), lambda i: (i, 5)), pl.BlockSpec((tm, d), lambda i: (i, 6)),
                  pl.BlockSpec((SUBLANE, d), lambda i: (0, 0)), ANY],
        out_specs=[row, row, pl.BlockSpec((tm, 2 * d), lambda i: (i, 0)), row, row, row,
                   pl.BlockSpec((SUBLANE, d), lambda i: (0, 0))],
        scratch_shapes=[pltpu.VMEM((d, d), BF16)] * 3 + [pltpu.SemaphoreType.DMA((3 * N_DEV,))],
        compiler_params=_params(),
    )(dx1b, oa, ob, p, p, gbias, gath)


DV_CONV_B_B, DV_BA, DV_BX, DV_SP, DV_CONV_A, DV_CONV_B = 0, 1, 2, 3, 4, 7
DV_ROWS = 16


def _mixer_bwd(dya, dyb, dpg, p, u_s, h_s, caw, cbw, vec, bda, bdx, name, comms=()):
    t, d = dya.shape
    tm = _row_tile(t, 256)
    n_t = t // tm
    cw = min(MXU_TILE, d)
    nb = d // cw
    hb = tm // SUBLANE

    def body(dya_ref, dyb_ref, dpg_ref, ba_ref, ca_ref, xa_ref, xb_ref, gb_ref, cah_ref, xah_ref, xbh_ref,
             u_ref, h_ref, hh_ref, caw_ref, cbw_ref, vec_ref, bda_ref, bdx_ref,
             dp_ref, dv_ref, dwa_ref, dwx_ref,
             zbuf, xbuf, hbuf, dczbuf, dubuf, a2buf, a1buf, lbuf, dcz_head, du_head, a_head, lam_head):
        i = pl.program_id(0)

        @pl.when(i == 0)
        def _():
            for ref in (dv_ref, dwa_ref, dwx_ref, dcz_head, du_head, a_head, lam_head):
                _zero(ref)

        has_prev = jnp.where(i < n_t - 1, 1.0, 0.0).astype(F32)
        row = lax.broadcasted_iota(jnp.int32, (SUBLANE, cw), 0)
        dp_ref[:, 5 * d:7 * d] = dpg_ref[...]

        def colsum(v):
            return jnp.sum(v, axis=0, keepdims=True)

        for j in range(nb):
            cs = slice(j * cw, (j + 1) * cw)
            ca = ca_ref[:, cs].astype(F32)
            xa = xa_ref[:, cs].astype(F32)
            z = ca * xa
            zbuf[0:SUBLANE, :] = cah_ref[:, cs].astype(F32) * xah_ref[:, cs].astype(F32) * has_prev
            zbuf[SUBLANE:, :] = z
            z1 = zbuf[pl.ds(SUBLANE - 1, tm), :]
            z2 = zbuf[pl.ds(SUBLANE - 2, tm), :]
            w0, w1, w2 = caw_ref[0:1, cs], caw_ref[1:2, cs], caw_ref[2:3, cs]
            cz = w0 * z2 + w1 * z1 + w2 * z
            dya = dya_ref[:, cs].astype(F32)
            dp_ref[:, 0 * d + j * cw:0 * d + (j + 1) * cw] = (dya * cz).astype(BF16)
            dcz = dya * ba_ref[:, cs].astype(F32)
            dczbuf[0:tm, :] = dcz
            dczbuf[tm:, :] = dcz_head[:, cs]
            dcz_head[:, cs] = dcz[0:SUBLANE, :]
            dz = w2 * dcz + w1 * dczbuf[pl.ds(1, tm), :] + w0 * dczbuf[pl.ds(2, tm), :]
            dv_ref[DV_CONV_A + 0:DV_CONV_A + 1, cs] += colsum(dcz * z2)
            dv_ref[DV_CONV_A + 1:DV_CONV_A + 2, cs] += colsum(dcz * z1)
            dv_ref[DV_CONV_A + 2:DV_CONV_A + 3, cs] += colsum(dcz * z)
            dp_ref[:, 1 * d + j * cw:1 * d + (j + 1) * cw] = (dz * xa).astype(BF16)
            dp_ref[:, 2 * d + j * cw:2 * d + (j + 1) * cw] = (dz * ca).astype(BF16)
            h = h_ref[:, cs].astype(F32)
            hbuf[0:SUBLANE, :] = hh_ref[:, cs].astype(F32) * has_prev
            hbuf[SUBLANE:, :] = h
            h_prev = hbuf[pl.ds(SUBLANE - 1, tm), :]
            dyb = dyb_ref[:, cs].astype(F32)
            gel, dgel = _gelu_and_grad(gb_ref[:, cs].astype(F32))
            dp_ref[:, 4 * d + j * cw:4 * d + (j + 1) * cw] = (dyb * h * dgel).astype(BF16)
            ub = u_ref[:, cs]
            u = ub.astype(F32)
            sp = vec_ref[3:4, cs]
            r, gi, a, s = _lru_gates(ub, bda_ref[j], bdx_ref[j], vec_ref[1:2, cs], vec_ref[2:3, cs], sp)
            a2buf[0:tm, :] = a
            a2buf[tm:, :] = a_head[:, cs]
            a1buf[...] = a2buf[pl.ds(1, tm), :]
            lbuf[...] = dyb * gel

            def block(kk, carry):
                r0 = pl.multiple_of((hb - 1 - kk) * SUBLANE, SUBLANE)
                ak = a1buf[pl.ds(r0, SUBLANE), :]
                bk = lbuf[pl.ds(r0, SUBLANE), :]
                for sh in (1, 2, 4):
                    m = row < SUBLANE - sh
                    bk = jnp.where(m, ak * pltpu.roll(bk, SUBLANE - sh, 0) + bk, bk)
                    ak = jnp.where(m, ak * pltpu.roll(ak, SUBLANE - sh, 0), ak)
                lk = bk + ak * carry
                lbuf[pl.ds(r0, SUBLANE), :] = lk
                return lk[0:1, :]

            lam0 = lax.fori_loop(0, hb, block, lam_head[0:1, cs], unroll=4)
            lam_head[:, cs] = jnp.broadcast_to(lam0, (SUBLANE, cw))
            a_head[:, cs] = jnp.broadcast_to(a[0:1, :], (SUBLANE, cw))
            lam = lbuf[...]
            da = lam * h_prev
            iu = gi * u
            ds = lam * iu
            di = lam * s * u
            du = lam * s * gi
            dlog_a = da * a - ds * (a * a) / s
            dv_ref[DV_SP:DV_SP + 1, cs] += colsum(dlog_a * r) * (-LRU_C)
            dpr = dlog_a * ((-LRU_C) * sp) * r * (1.0 - r)
            dpi = di * gi * (1.0 - gi)
            dv_ref[DV_BA:DV_BA + 1, cs] += colsum(dpr)
            dv_ref[DV_BX:DV_BX + 1, cs] += colsum(dpi)
            dprb = dpr.astype(BF16)
            dpib = dpi.astype(BF16)
            du = du + _dot_nt(dprb, bda_ref[j]) + _dot_nt(dpib, bdx_ref[j])
            dwa_ref[j] += _dot_tn(ub, dprb)
            dwx_ref[j] += _dot_tn(ub, dpib)
            xb = xb_ref[:, cs].astype(F32)
            xbuf[0:SUBLANE, :] = xbh_ref[:, cs].astype(F32) * has_prev
            xbuf[SUBLANE:, :] = xb
            dubuf[0:tm, :] = du
            dubuf[tm:, :] = du_head[:, cs]
            du_head[:, cs] = du[0:SUBLANE, :]
            v0, v1, v2, v3 = cbw_ref[0:1, cs], cbw_ref[1:2, cs], cbw_ref[2:3, cs], cbw_ref[3:4, cs]
            dxb = v3 * du + v2 * dubuf[pl.ds(1, tm), :] + v1 * dubuf[pl.ds(2, tm), :] + v0 * dubuf[pl.ds(3, tm), :]
            dp_ref[:, 3 * d + j * cw:3 * d + (j + 1) * cw] = dxb.astype(BF16)
            dv_ref[DV_CONV_B_B:DV_CONV_B_B + 1, cs] += colsum(du)
            dv_ref[DV_CONV_B + 0:DV_CONV_B + 1, cs] += colsum(du * xbuf[pl.ds(SUBLANE - 3, tm), :])
            dv_ref[DV_CONV_B + 1:DV_CONV_B + 2, cs] += colsum(du * xbuf[pl.ds(SUBLANE - 2, tm), :])
            dv_ref[DV_CONV_B + 2:DV_CONV_B + 3, cs] += colsum(du * xbuf[pl.ds(SUBLANE - 1, tm), :])
            dv_ref[DV_CONV_B + 3:DV_CONV_B + 4, cs] += colsum(du * xb)

    rt = lambda i: n_t - 1 - i
    row_spec = pl.BlockSpec((tm, d), lambda i: (rt(i), 0))
    slab = lambda s: pl.BlockSpec((tm, d), lambda i, s=s: (rt(i), s))
    halo = lambda s: pl.BlockSpec((SUBLANE, d), lambda i, s=s: (jnp.maximum(rt(i) * hb - 1, 0), s))
    small = pl.BlockSpec((SUBLANE, d), lambda i: (0, 0))
    bd = pl.BlockSpec((nb, cw, cw), lambda i: (0, 0, 0))
    sd = jax.ShapeDtypeStruct
    wbuf = lambda: pltpu.VMEM((tm + SUBLANE, cw), F32)
    head = lambda: pltpu.VMEM((SUBLANE, d), F32)
    return _hosted_call(
        body, comms, name=name, grid=(n_t,),
        out_shape=[sd((t, N_PROJ * d), BF16), sd((DV_ROWS, d), F32), sd((nb, cw, cw), F32), sd((nb, cw, cw), F32)],
        in_specs=[row_spec, row_spec, pl.BlockSpec((tm, 2 * d), lambda i: (rt(i), 0)),
                  slab(0), slab(1), slab(2), slab(3), slab(4), halo(1), halo(2), halo(3),
                  row_spec, row_spec, halo(0), small, small, small, bd, bd],
        out_specs=[pl.BlockSpec((tm, N_PROJ * d), lambda i: (rt(i), 0)), pl.BlockSpec((DV_ROWS, d), lambda i: (0, 0)), bd, bd],
        scratch_shapes=[wbuf(), wbuf(), wbuf(), wbuf(), wbuf(), wbuf(),
                        pltpu.VMEM((tm, cw), F32), pltpu.VMEM((tm, cw), F32), head(), head(), head(), head()],
        args=(dya, dyb, dpg, p, p, p, p, p, p, p, p, u_s, h_s, h_s, caw, cbw, vec, bda, bdx))


def _in_proj_bwd(dp, x, dx1, g_row, gath, grp, name, comms=()):
    t, d = x.shape
    tm = _row_tile(t, 256)
    n_in = N_PROJ * d

    def body(dp_ref, x_ref, dx1_ref, g_ref, gath_ref, dx_ref, dg_ref, w_in, sems):
        @pl.when(pl.program_id(0) == 0)
        def _():
            _load_weights(gath_ref, grp, ["in"], [w_in], sems)
            _zero(dg_ref)

        dh = _dot_nn(dp_ref[:, 0:d], w_in[0:d, :])
        for k in range(1, N_PROJ):
            dh = dh + _dot_nn(dp_ref[:, k * d:(k + 1) * d], w_in[k * d:(k + 1) * d, :])
        xf = x_ref[...]
        rstd = lax.rsqrt(jnp.mean(xf * xf, axis=-1, keepdims=True) + RMS_EPS)
        xh = xf * rstd
        g = g_ref[...]
        dg_ref[0:1, :] += jnp.sum(dh * xh, axis=0, keepdims=True)
        dxh = dh * g
        dx_ref[...] = dx1_ref[...] + rstd * (dxh - xh * jnp.mean(dxh * xh, axis=-1, keepdims=True))

    row = pl.BlockSpec((tm, d), lambda i: (i, 0))
    sd = jax.ShapeDtypeStruct
    return _hosted_call(
        body, comms, name=name, grid=(t // tm,),
        out_shape=[sd((t, d), F32), sd((SUBLANE, d), F32)],
        in_specs=[pl.BlockSpec((tm, n_in), lambda i: (i, 0)), row, row, pl.BlockSpec((1, d), lambda i: (0, 0)), ANY],
        out_specs=[row, pl.BlockSpec((SUBLANE, d), lambda i: (0, 0))],
        scratch_shapes=[pltpu.VMEM((n_in, d), BF16), pltpu.SemaphoreType.DMA((N_DEV,))],
        args=(dp, x, dx1, g_row, gath))


def _weight_grad(a, b, name):
    t, m = a.shape
    n = b.shape[1]
    bt = _row_tile(t, 512)
    bm = m
    for div in (1, 2, 4, 8):
        if m % div == 0 and (m // div) % LANE == 0 and (m // div) * n * 4 <= (12 << 20):
            bm = m // div
            break
    n_t = t // bt

    def body(a_ref, b_ref, o_ref, acc):
        k = pl.program_id(1)

        @pl.when(k == 0)
        def _():
            _zero(acc)

        acc[...] += _dot_tn(a_ref[...], b_ref[...])

        @pl.when(k == n_t - 1)
        def _():
            o_ref[...] = acc[...].astype(BF16)

    return pl.pallas_call(
        body, name=name, grid=(m // bm, n_t),
        out_shape=jax.ShapeDtypeStruct((m, n), BF16),
        in_specs=[pl.BlockSpec((bt, bm), lambda i, k: (k, i)), pl.BlockSpec((bt, n), lambda i, k: (k, 0))],
        out_specs=pl.BlockSpec((bm, n), lambda i, k: (i, 0)),
        scratch_shapes=[pltpu.VMEM((bm, n), F32)],
        compiler_params=_params(2),
    )(a, b)


def _adamw(w, g, m, v, name):
    r, c = w.shape
    tr = _fit_rows(r, c * 4)
    c1 = 1.0 - ADAM_B1 ** ADAM_STEP
    c2 = 1.0 - ADAM_B2 ** ADAM_STEP

    def body(w_ref, g_ref, m_ref, v_ref, d_ref, nm_ref, nv_ref):
        g32 = g_ref[...]
        nm = ADAM_B1 * m_ref[...] + (1.0 - ADAM_B1) * g32
        nv = ADAM_B2 * v_ref[...] + (1.0 - ADAM_B2) * (g32 * g32)
        nm_ref[...] = nm
        nv_ref[...] = nv
        d_ref[...] = -ADAM_LR * ((nm / c1) / (jnp.sqrt(nv / c2) + ADAM_EPS) + ADAM_WD * w_ref[...])

    spec = pl.BlockSpec((tr, c), lambda i: (i, 0))
    return pl.pallas_call(
        body, name=name, grid=(r // tr,),
        out_shape=[jax.ShapeDtypeStruct((r, c), F32)] * 3,
        in_specs=[spec] * 4, out_specs=[spec] * 3,
        compiler_params=_params(),
    )(w, g, m, v)


def _pad_rows(a, mult=SUBLANE):
    pad = (-a.shape[0]) % mult
    return a if pad == 0 else jnp.concatenate([a, jnp.zeros((pad,) + a.shape[1:], a.dtype)], axis=0)


REPLICATED = ("ln1_g", "conv_b_b", "lru_wa", "lru_ba", "lru_wx", "lru_bx", "lru_lambda", "ln2_g", "final_g")
SMALL_SHARDED = ("conv_a_w", "conv_b_w", "gate_bias")
MATRICES = ("w_in", "w_out_a", "w_out_b", "w_o", "w_ffn_gate", "w_ffn_up", "w_ffn_down")
ORDER = ("ln1_g", "w_in", "conv_a_w", "conv_b_w", "conv_b_b", "lru_wa", "lru_ba", "lru_wx", "lru_bx", "lru_lambda",
         "w_out_a", "w_out_b", "gate_bias", "w_o", "ln2_g", "w_ffn_gate", "w_ffn_up", "w_ffn_down", "final_g")


def kernel(x, ln1_g, w_in, conv_a_w, conv_b_w, conv_b_b, lru_wa, lru_ba, lru_wx, lru_bx, lru_lambda, w_out_a, w_out_b, gate_bias, w_o, ln2_g, w_ffn_gate, w_ffn_up, w_ffn_down, final_g, loss_target, m_ln1_g, m_w_in, m_conv_a_w, m_conv_b_w, m_conv_b_b, m_lru_wa, m_lru_ba, m_lru_wx, m_lru_bx, m_lru_lambda, m_w_out_a, m_w_out_b, m_gate_bias, m_w_o, m_ln2_g, m_w_ffn_gate, m_w_ffn_up, m_w_ffn_down, m_final_g, v_ln1_g, v_w_in, v_conv_a_w, v_conv_b_w, v_conv_b_b, v_lru_wa, v_lru_ba, v_lru_wx, v_lru_bx, v_lru_lambda, v_w_out_a, v_w_out_b, v_gate_bias, v_w_o, v_ln2_g, v_w_ffn_gate, v_w_ffn_up, v_w_ffn_down, v_final_g):
    w = dict(ln1_g=ln1_g, w_in=w_in, conv_a_w=conv_a_w, conv_b_w=conv_b_w, conv_b_b=conv_b_b, lru_wa=lru_wa,
             lru_ba=lru_ba, lru_wx=lru_wx, lru_bx=lru_bx, lru_lambda=lru_lambda, w_out_a=w_out_a, w_out_b=w_out_b,
             gate_bias=gate_bias, w_o=w_o, ln2_g=ln2_g, w_ffn_gate=w_ffn_gate, w_ffn_up=w_ffn_up,
             w_ffn_down=w_ffn_down, final_g=final_g)
    mom = dict(ln1_g=m_ln1_g, w_in=m_w_in, conv_a_w=m_conv_a_w, conv_b_w=m_conv_b_w, conv_b_b=m_conv_b_b,
               lru_wa=m_lru_wa, lru_ba=m_lru_ba, lru_wx=m_lru_wx, lru_bx=m_lru_bx, lru_lambda=m_lru_lambda,
               w_out_a=m_w_out_a, w_out_b=m_w_out_b, gate_bias=m_gate_bias, w_o=m_w_o, ln2_g=m_ln2_g,
               w_ffn_gate=m_w_ffn_gate, w_ffn_up=m_w_ffn_up, w_ffn_down=m_w_ffn_down, final_g=m_final_g)
    var = dict(ln1_g=v_ln1_g, w_in=v_w_in, conv_a_w=v_conv_a_w, conv_b_w=v_conv_b_w, conv_b_b=v_conv_b_b,
               lru_wa=v_lru_wa, lru_ba=v_lru_ba, lru_wx=v_lru_wx, lru_bx=v_lru_bx, lru_lambda=v_lru_lambda,
               w_out_a=v_w_out_a, w_out_b=v_w_out_b, gate_bias=v_gate_bias, w_o=v_w_o, ln2_g=v_ln2_g,
               w_ffn_gate=v_w_ffn_gate, w_ffn_up=v_w_ffn_up, w_ffn_down=v_w_ffn_down, final_g=v_final_g)

    _, t, d = x.shape
    n_layers = w_in.shape[0]
    ff = w_ffn_down.shape[1] * N_DEV
    dd = d // N_DEV
    hd = d // LRU_HEADS
    cw = min(MXU_TILE, d)
    nb = d // cw
    hpt = cw // hd
    grp = _groups(d, ff)
    me = 4 * lax.axis_index("x") + 2 * lax.axis_index("y") + lax.axis_index("c")
    x0 = x[0]
    target = loss_target[0]

    packed = [{"in": jnp.swapaxes(w_in[l], 0, 1).astype(BF16),
               "rest": jnp.concatenate([w_out_a[l], w_out_b[l], w_o[l], jnp.swapaxes(w_ffn_gate[l], 0, 1),
                                        jnp.swapaxes(w_ffn_up[l], 0, 1), w_ffn_down[l]], axis=0).astype(BF16)}
              for l in range(n_layers)]
    n_small = CONV_A_K + CONV_B_K + 2
    small = _pad_rows(jnp.concatenate([conv_a_w, conv_b_w, gate_bias], axis=1).reshape(n_layers * n_small, dd))
    sp = jax.nn.softplus(-lru_lambda)
    vec = [_pad_rows(jnp.stack([conv_b_b[l], lru_ba[l], lru_bx[l], sp[l]])) for l in range(n_layers)]
    eye = jnp.eye(hpt, dtype=F32)

    def block_diag(wh):
        return jnp.einsum("jkab,kl->jkalb", wh.reshape(nb, hpt, hd, hd), eye).reshape(nb, cw, cw).astype(BF16)

    bda = [block_diag(lru_wa[l]) for l in range(n_layers)]
    bdx = [block_diag(lru_wx[l]) for l in range(n_layers)]

    gath = [dict() for _ in range(n_layers)]
    (gath[0]["in"],) = _comm_call(_Gather(packed[0]["in"]), "gather_in_0")
    saved = []
    xl = x0
    for l in range(n_layers):
        comms = [_Gather(packed[0]["rest"]), _Gather(small)] if l == 0 else []
        (p, h1b), got = _in_proj_fwd(xl, ln1_g[l][None], gath[l]["in"], grp["in"], f"in_proj_fwd_{l}", comms)
        if l == 0:
            gath[0]["rest"], small_g = got[0][0], got[1][0]
            small_full = jnp.swapaxes(small_g[:, :n_layers * n_small], 0, 1).reshape(n_layers, n_small, d)
            caw = [_pad_rows(small_full[k, 0:CONV_A_K]) for k in range(n_layers)]
            cbw = [_pad_rows(small_full[k, CONV_A_K:CONV_A_K + CONV_B_K]) for k in range(n_layers)]
            gbias = [_pad_rows(small_full[k, CONV_A_K + CONV_B_K:]) for k in range(n_layers)]
        comms = [_Gather(packed[l + 1]["in"]), _Gather(packed[l + 1]["rest"])] if l + 1 < n_layers else []
        (ya, yb, u_s, h_s), got = _mixer_fwd(p, caw[l], cbw[l], vec[l], bda[l], bdx[l], f"mixer_fwd_{l}", comms)
        if l + 1 < n_layers:
            gath[l + 1]["in"], gath[l + 1]["rest"] = got[0][0], got[1][0]
        x1, oa, ob = _merge_fwd(xl, ya, yb, p, gbias[l], gath[l]["rest"], grp["rest"], f"merge_fwd_{l}")
        x2, gg, uu = _ffn_fwd(x1, ln2_g[l][None], gath[l]["rest"], grp["rest"], f"ffn_fwd_{l}")
        saved.append(dict(x=xl, p=p, h1b=h1b, ya=ya, yb=yb, u=u_s, h=h_s, x1=x1, oa=oa, ob=ob, gg=gg, uu=uu))
        xl = x2
    loss_tile, dx, dfinal = _loss_head(xl, final_g[None], target, "loss_head")
    loss = lax.psum(loss_tile[0, 0], ("x", "y", "c"))

    def heads(dwb):
        blocks = jnp.diagonal(dwb.reshape(nb, hpt, hd, hpt, hd), axis1=1, axis2=3)
        return jnp.moveaxis(blocks, 3, 1).reshape(hd, d)

    def stack_layers(name):
        return jnp.concatenate([small_grads[k][name] for k in range(n_layers)], axis=0)

    recv = [dict() for _ in range(n_layers)]
    small_grads = [None] * n_layers
    rep_names = [n for n in REPLICATED if n != "final_g"]
    pending = None
    for l in reversed(range(n_layers)):
        s = saved[l]
        comms = [pending[1]] if pending else []
        (dx1, dx1b, dgg, duu, f, h2b, dx2b, dln2), got = _ffn_bwd(
            dx, s["x1"], s["gg"], s["uu"], ln2_g[l][None], gath[l]["rest"], grp["rest"], f"ffn_bwd_{l}", comms)
        if pending:
            (recv[pending[0]]["in"],) = got[0]
        dya, dyb, dpg, mg, doa, dob, dgbias = _merge_bwd(dx1b, s["oa"], s["ob"], s["p"], gbias[l], gath[l]["rest"],
                                                         grp["rest"], f"merge_bwd_{l}")
        rest = {"oa": _weight_grad(s["ya"], doa, f"dw_out_a_{l}"), "ob": _weight_grad(s["yb"], dob, f"dw_out_b_{l}"),
                "o": _weight_grad(mg, dx1b, f"dw_o_{l}"), "g": _weight_grad(dgg, h2b, f"dw_ffn_gate_{l}"),
                "u": _weight_grad(duu, h2b, f"dw_ffn_up_{l}"), "d": _weight_grad(f, dx2b, f"dw_ffn_down_{l}")}
        (dp, dv, dwa, dwx), got = _mixer_bwd(dya, dyb, dpg, s["p"], s["u"], s["h"], caw[l], cbw[l], vec[l], bda[l], bdx[l],
                                             f"mixer_bwd_{l}", [_Exchange(rest, grp["rest"])])
        (recv[l]["rest"],) = got[0]
        small_grads[l] = {
            "ln1_g": jnp.zeros((1, d), F32), "conv_b_b": dv[DV_CONV_B_B:DV_CONV_B_B + 1], "lru_wa": heads(dwa),
            "lru_ba": dv[DV_BA:DV_BA + 1], "lru_wx": heads(dwx), "lru_bx": dv[DV_BX:DV_BX + 1],
            "lru_lambda": dv[DV_SP:DV_SP + 1] * (-jax.nn.sigmoid(-lru_lambda[l]))[None], "ln2_g": dln2[0:1],
            "conv_a_w": dv[DV_CONV_A:DV_CONV_A + CONV_A_K], "conv_b_w": dv[DV_CONV_B:DV_CONV_B + CONV_B_K],
            "gate_bias": dgbias[0:2],
        }
        pending = (l, _Exchange({"in": _weight_grad(dp, s["h1b"], f"dw_in_{l}")}, grp["in"]))
        comms = []
        if l == 0:
            rep_rows = [stack_layers(n) for n in rep_names] + [dfinal[0:1]]
            sh_rows = [stack_layers(n) for n in SMALL_SHARDED]
            comms = [pending[1], _Gather(_pad_rows(jnp.concatenate(rep_rows + sh_rows, axis=0)))]
        (dx, dln1), got = _in_proj_bwd(dp, s["x"], dx1, ln1_g[l][None], gath[l]["in"], grp["in"], f"in_proj_bwd_{l}", comms)
        small_grads[l]["ln1_g"] = dln1[0:1]
        if l == 0:
            (recv[0]["in"],), (small_all,) = got
    grad_x = dx[None]

    g = {}
    gsum = [{k: _sum_slots(recv[l][k], f"sum_{k}_{l}") for k in ("in", "rest")} for l in range(n_layers)]

    def part(key):
        k = "in" if key == "in" else "rest"
        o, r = grp[k].off[key], grp[k].rows[key]
        return jnp.stack([gsum[l][k][o:o + r] for l in range(n_layers)])

    g = {"w_in": jnp.swapaxes(part("in"), 1, 2), "w_out_a": part("oa"), "w_out_b": part("ob"), "w_o": part("o"),
         "w_ffn_gate": jnp.swapaxes(part("g"), 1, 2), "w_ffn_up": jnp.swapaxes(part("u"), 1, 2), "w_ffn_down": part("d")}
    small_sum = _sum_slots(small_all, "sum_small_grads")
    (ln1_all,) = _comm_call(_Gather(_pad_rows(stack_layers("ln1_g"))), "gather_ln1_grads")
    ln1_sum = _sum_slots(ln1_all, "sum_ln1_grads")
    o = 0
    for n, r in zip(rep_names + ["final_g"], rep_rows):
        g[n] = small_sum[o:o + r.shape[0]].reshape(w[n].shape)
        o += r.shape[0]
    g["ln1_g"] = ln1_sum[:n_layers]
    for n, r in zip(SMALL_SHARDED, sh_rows):
        k = r.shape[0] // n_layers
        g[n] = lax.dynamic_slice_in_dim(small_sum[o:o + r.shape[0]], me * dd, dd, axis=1).reshape(n_layers, k, dd)
        o += r.shape[0]

    delta, new_m, new_v = {}, {}, {}
    for n in MATRICES:
        shape = w[n].shape
        flat = lambda a: a.reshape(shape[0] * shape[1], shape[2])
        dl, nm, nv = _adamw(flat(w[n]), flat(g[n]), flat(mom[n]), flat(var[n]), f"adamw_{n}")
        delta[n], new_m[n], new_v[n] = dl.reshape(shape), nm.reshape(shape), nv.reshape(shape)
    for group, width, name in ((REPLICATED, d, "adamw_replicated"), (SMALL_SHARDED, dd, "adamw_vectors")):
        cat = lambda src: _pad_rows(jnp.concatenate([src[n].reshape(-1, width) for n in group], axis=0))
        dl, nm, nv = _adamw(cat(w), cat(g), cat(mom), cat(var), name)
        o = 0
        for n in group:
            rows = w[n].size // width
            delta[n], new_m[n], new_v[n] = (a[o:o + rows].reshape(w[n].shape) for a in (dl, nm, nv))
            o += rows

    return (loss, grad_x, *[g[n] for n in ORDER], *[delta[n] for n in ORDER], *[new_m[n] for n in ORDER],
            *[new_v[n] for n in ORDER])
```

```python
import math

import jax
import jax.numpy as jnp
from jax import lax
from jax.experimental import pallas as pl
from jax.experimental.pallas import tpu as pltpu

F32 = jnp.float32
BF16 = jnp.bfloat16

N_DEV = 8
N_PROJ = 7
LRU_HEADS = 16
LRU_C = 8.0
RMS_EPS = 1e-6
CONV_A_K = 3
CONV_B_K = 4
GELU_C = math.sqrt(2.0 / math.pi)
GELU_A = 0.044715

ADAM_LR = 0.001
ADAM_B1 = 0.9
ADAM_B2 = 0.999
ADAM_EPS = 1e-08
ADAM_WD = 0.01
ADAM_STEP = 10

LANE = 128
SUBLANE = 8
MXU_TILE = 256
VMEM_LIMIT = 52 << 20
MESH = pl.DeviceIdType.MESH
ANY = pl.BlockSpec(memory_space=pl.ANY)


def _dot_nn(a, b):
    return lax.dot_general(a, b, (((1,), (0,)), ((), ())), preferred_element_type=F32)


def _dot_nt(a, b):
    return lax.dot_general(a, b, (((1,), (1,)), ((), ())), preferred_element_type=F32)


def _dot_tn(a, b):
    return lax.dot_general(a, b, (((0,), (0,)), ((), ())), preferred_element_type=F32)


def _sigmoid(x):
    return 1.0 / (1.0 + jnp.exp(-x))


def _gelu_and_grad(x):
    x2 = x * x
    t = jnp.tanh(GELU_C * x * (1.0 + GELU_A * x2))
    g = 0.5 * x * (1.0 + t)
    dg = 0.5 * (1.0 + t) + 0.5 * x * (1.0 - t * t) * GELU_C * (1.0 + 3.0 * GELU_A * x2)
    return g, dg


def _zero(ref):
    ref[...] = jnp.zeros(ref.shape, ref.dtype)


def _fit_rows(r, row_bytes, budget=1 << 20):
    fits = [t for t in range(16, r + 1, 16) if r % t == 0 and t * row_bytes <= budget]
    return max(fits) if fits else r


def _row_tile(t, want):
    tm = min(want, t // 2)
    assert t % tm == 0 and tm % SUBLANE == 0, (t, tm)
    return tm


def _params(n_grid=1, **kw):
    return pltpu.CompilerParams(dimension_semantics=("arbitrary",) * n_grid, vmem_limit_bytes=VMEM_LIMIT, **kw)


class _Group:
    def __init__(self, keys, rows):
        self.keys = keys
        self.rows = dict(zip(keys, rows))
        self.off, o = {}, 0
        for k in keys:
            self.off[k] = o
            o += self.rows[k]
        self.total = o


def _groups(d, ff):
    dd, ffs = d // N_DEV, ff // N_DEV
    return {"in": _Group(("in",), (N_PROJ * dd,)),
            "rest": _Group(("oa", "ob", "o", "g", "u", "d"), (dd, dd, dd, ffs, ffs, ffs))}


def _load_weights(g_ref, grp, keys, dsts, sems):
    copies = []
    for n, (k, dst) in enumerate(zip(keys, dsts)):
        rows, off = grp.rows[k], grp.off[k]
        copies += [pltpu.make_async_copy(g_ref.at[p, pl.ds(off, rows), :], dst.at[pl.ds(p * rows, rows), :],
                                         sems.at[n * N_DEV + p]) for p in range(N_DEV)]
    for c in copies:
        c.start()
    for c in copies:
        c.wait()


def _comm_sems():
    return [pltpu.SemaphoreType.DMA((N_DEV - 1,)), pltpu.SemaphoreType.DMA((N_DEV - 1,)), pltpu.SemaphoreType.DMA]


class _Gather:
    def __init__(self, x):
        self.inputs = [x]
        self.out_shape = [jax.ShapeDtypeStruct((N_DEV,) + x.shape, x.dtype)]
        self.scratch = _comm_sems()

    def _plan(self, ins, outs, scr):
        (x_ref,), (out_ref,), (send_sems, recv_sems, local_sem) = ins, outs, scr
        mx, my, mc = lax.axis_index("x"), lax.axis_index("y"), lax.axis_index("c")
        me, sibling = (mx, my, mc), (mx, my, 1 - mc)
        chips = [(1 - mx, my), (mx, 1 - my), (1 - mx, 1 - my)]

        def slot(px, py, pc):
            return out_ref.at[4 * px + 2 * py + pc]

        def copy(k, block, to, src=None):
            return pltpu.make_async_remote_copy(
                src_ref=slot(*block) if src is None else src, dst_ref=slot(*block),
                send_sem=send_sems.at[k], recv_sem=recv_sems.at[k], device_id=to, device_id_type=MESH)

        mine = lambda: pltpu.make_async_copy(x_ref, slot(*me), local_sem)
        first = [lambda: copy(0, me, sibling, src=x_ref)]
        first += [lambda j=j, chip=chip: copy(1 + j, me, (*chip, mc), src=x_ref) for j, chip in enumerate(chips)]
        landed = [lambda j=j, chip=chip: copy(1 + j, (*chip, mc), me) for j, chip in enumerate(chips)]
        passed = [lambda j=j, chip=chip: copy(4 + j, (*chip, mc), sibling) for j, chip in enumerate(chips)]
        from_sibling = [lambda: copy(0, sibling, me)]
        from_sibling += [lambda j=j, chip=chip: copy(4 + j, (*chip, 1 - mc), me) for j, chip in enumerate(chips)]
        return mine, first, landed, passed, from_sibling

    def start(self, ins, outs, scr):
        mine, first, _, _, _ = self._plan(ins, outs, scr)
        mine().start()
        for cp in first:
            cp().start()

    def mid(self, ins, outs, scr):
        _, _, landed, passed, _ = self._plan(ins, outs, scr)
        for got, fwd in zip(landed, passed):
            got().wait_recv()
            fwd().start()

    def finish(self, ins, outs, scr):
        mine, first, _, passed, from_sibling = self._plan(ins, outs, scr)
        for cp in from_sibling:
            cp().wait_recv()
        for cp in first + passed:
            cp().wait_send()
        mine().wait()


class _Exchange:
    def __init__(self, mats, grp):
        self.grp = grp
        self.inputs = [mats[k] for k in grp.keys]
        self.out_shape = [jax.ShapeDtypeStruct((N_DEV, grp.total, self.inputs[0].shape[1]), BF16)]
        self.scratch = _comm_sems()

    def _pieces(self, g_refs, out_ref, q, dst_slot):
        out = []
        for g_ref, k in zip(g_refs, self.grp.keys):
            rows = self.grp.rows[k]
            out.append((g_ref.at[pl.ds(pl.multiple_of(q * rows, 16), rows), :],
                        out_ref.at[dst_slot, pl.ds(self.grp.off[k], rows), :]))
        return out

    def start(self, ins, outs, scr):
        (out_ref,), (send_sems, recv_sems, local_sem) = outs, scr
        mx, my, mc = lax.axis_index("x"), lax.axis_index("y"), lax.axis_index("c")
        me = 4 * mx + 2 * my + mc
        for s, t in self._pieces(ins, out_ref, me, me):
            pltpu.make_async_copy(s, t, local_sem).start()
        for k in range(1, N_DEV):
            px, py, pc = mx ^ ((k >> 2) & 1), my ^ ((k >> 1) & 1), mc ^ (k & 1)
            for s, t in self._pieces(ins, out_ref, 4 * px + 2 * py + pc, me):
                pltpu.make_async_remote_copy(src_ref=s, dst_ref=t, send_sem=send_sems.at[k - 1],
                                             recv_sem=recv_sems.at[k - 1], device_id=(px, py, pc),
                                             device_id_type=MESH).start()

    def mid(self, ins, outs, scr):
        pass

    def finish(self, ins, outs, scr):
        (out_ref,), (send_sems, recv_sems, local_sem) = outs, scr
        mx, my, mc = lax.axis_index("x"), lax.axis_index("y"), lax.axis_index("c")
        whole = out_ref.at[0]
        for k in range(1, N_DEV):
            done = pltpu.make_async_remote_copy(src_ref=whole, dst_ref=whole, send_sem=send_sems.at[k - 1],
                                                recv_sem=recv_sems.at[k - 1], device_id=(mx, my, mc),
                                                device_id_type=MESH)
            done.wait_send()
            done.wait_recv()
        pltpu.make_async_copy(whole, whole, local_sem).wait()


def _split(refs, sizes):
    out, pos = [], 0
    for n in sizes:
        out.append(refs[pos:pos + n])
        pos += n
    return out


def _hosted_call(body, comms, *, name, grid, in_specs, out_specs, out_shape, scratch_shapes, args):
    n_steps = grid[0]
    nc = len(comms)
    sizes = ([len(in_specs)] + [len(c.inputs) for c in comms] + [len(out_specs)] + [len(c.out_shape) for c in comms]
             + [len(scratch_shapes)] + [len(c.scratch) for c in comms])

    def hosted(*refs):
        parts = _split(refs, sizes)
        ins, c_ins = parts[0], parts[1:1 + nc]
        outs, c_outs = parts[1 + nc], parts[2 + nc:2 + 2 * nc]
        scr, c_scr = parts[2 + 2 * nc], parts[3 + 2 * nc:]
        step = pl.program_id(0)
        if comms:
            @pl.when(step == 0)
            def _():
                for c, a, b, s in zip(comms, c_ins, c_outs, c_scr):
                    c.start(a, b, s)

            @pl.when(step == max(n_steps - 2, 0))
            def _():
                for c, a, b, s in zip(comms, c_ins, c_outs, c_scr):
                    c.mid(a, b, s)

        body(*ins, *outs, *scr)
        if comms:
            @pl.when(step == n_steps - 1)
            def _():
                for c, a, b, s in zip(comms, c_ins, c_outs, c_scr):
                    c.finish(a, b, s)

    res = pl.pallas_call(
        hosted, name=name, grid=grid,
        out_shape=[*out_shape, *[o for c in comms for o in c.out_shape]],
        in_specs=[*in_specs, *[ANY for c in comms for _ in c.inputs]],
        out_specs=[*out_specs, *[ANY for c in comms for _ in c.out_shape]],
        scratch_shapes=[*scratch_shapes, *[s for c in comms for s in c.scratch]],
        compiler_params=_params(),
    )(*args, *[a for c in comms for a in c.inputs])
    main, rest = res[:len(out_specs)], res[len(out_specs):]
    return main, _split(rest, [len(c.out_shape) for c in comms])


def _comm_call(comm, name):
    def body(*refs):
        ins, outs, scr = _split(refs, [len(comm.inputs), len(comm.out_shape), len(comm.scratch)])
        comm.start(ins, outs, scr)
        comm.mid(ins, outs, scr)
        comm.finish(ins, outs, scr)

    return pl.pallas_call(
        body, name=name, out_shape=comm.out_shape, in_specs=[ANY] * len(comm.inputs),
        out_specs=[ANY] * len(comm.out_shape), scratch_shapes=comm.scratch,
    )(*comm.inputs)


def _sum_slots(x, name):
    n, r, c = x.shape
    tr = _fit_rows(r, c * 4)

    def body(x_ref, o_ref):
        acc = x_ref[0].astype(F32)
        for p in range(1, n):
            acc = acc + x_ref[p].astype(F32)
        o_ref[...] = acc

    return pl.pallas_call(
        body, name=name, grid=(r // tr,),
        out_shape=jax.ShapeDtypeStruct((r, c), F32),
        in_specs=[pl.BlockSpec((n, tr, c), lambda i: (0, i, 0))],
        out_specs=pl.BlockSpec((tr, c), lambda i: (i, 0)),
        compiler_params=_params(),
    )(x)


def _in_proj_fwd(x, g_row, gath, grp, name, comms=()):
    t, d = x.shape
    tm = _row_tile(t, 256)
    n_in = N_PROJ * d

    def body(x_ref, g_ref, gath_ref, p_ref, h_ref, w_in, sems):
        @pl.when(pl.program_id(0) == 0)
        def _():
            _load_weights(gath_ref, grp, ["in"], [w_in], sems)

        xf = x_ref[...]
        rstd = lax.rsqrt(jnp.mean(xf * xf, axis=-1, keepdims=True) + RMS_EPS)
        h = (xf * rstd * g_ref[...]).astype(BF16)
        h_ref[...] = h
        for k in range(N_PROJ):
            p_ref[:, k * d:(k + 1) * d] = _dot_nt(h, w_in[k * d:(k + 1) * d, :]).astype(BF16)

    row = pl.BlockSpec((tm, d), lambda i: (i, 0))
    return _hosted_call(
        body, comms, name=name, grid=(t // tm,),
        out_shape=[jax.ShapeDtypeStruct((t, n_in), BF16), jax.ShapeDtypeStruct((t, d), BF16)],
        in_specs=[row, pl.BlockSpec((1, d), lambda i: (0, 0)), ANY],
        out_specs=[pl.BlockSpec((tm, n_in), lambda i: (i, 0)), row],
        scratch_shapes=[pltpu.VMEM((n_in, d), BF16), pltpu.SemaphoreType.DMA((N_DEV,))],
        args=(x, g_row, gath))


def _time_tile(t):
    return _row_tile(t, 256)


def _to_tile_order(a, tm):
    t, c = a.shape
    return jnp.swapaxes(a.reshape(t // tm, SUBLANE, tm // SUBLANE, c), 1, 2).reshape(t, c)


def _from_tile_order(a, tm):
    t, c = a.shape
    return jnp.swapaxes(a.reshape(t // tm, tm // SUBLANE, SUBLANE, c), 1, 2).reshape(t, c)


def _causal_fill(buf, v, prev_tail, n, row):
    tm = v.shape[0]
    for q in range(n):
        cur = v[tm - SUBLANE * (n - q):tm - SUBLANE * (n - q - 1), :]
        prv = prev_tail[SUBLANE * q:SUBLANE * (q + 1), :]
        buf[SUBLANE * q:SUBLANE * (q + 1), :] = jnp.where(row == 0, pltpu.roll(prv, 1, 0), pltpu.roll(cur, 1, 0))
    buf[SUBLANE * n:, :] = v


def _anticausal_fill(buf, v, next_head, n, row):
    tm = v.shape[0]
    buf[0:tm, :] = v
    for q in range(n):
        cur = v[SUBLANE * q:SUBLANE * (q + 1), :]
        nxt = next_head[SUBLANE * q:SUBLANE * (q + 1), :]
        buf[tm + SUBLANE * q:tm + SUBLANE * (q + 1), :] = jnp.where(
            row == SUBLANE - 1, pltpu.roll(nxt, SUBLANE - 1, 0), pltpu.roll(cur, SUBLANE - 1, 0))


def _chain_scan(abuf, bbuf, nk, reverse):
    cw = abuf.shape[1]

    def step(n, carry):
        h, c = carry
        r0 = pl.multiple_of((nk - 1 - n if reverse else n) * SUBLANE, SUBLANE)
        ak = abuf[pl.ds(r0, SUBLANE), :]
        h = ak * h + bbuf[pl.ds(r0, SUBLANE), :]
        c = ak * c
        bbuf[pl.ds(r0, SUBLANE), :] = h
        abuf[pl.ds(r0, SUBLANE), :] = c
        return h, c

    return lax.fori_loop(0, nk, step, (jnp.zeros((SUBLANE, cw), F32), jnp.ones((SUBLANE, cw), F32)), unroll=8)


def _sublane_scan(a, b, row, reverse):
    for sh in (1, 2, 4):
        if reverse:
            m = row < SUBLANE - sh
            b = jnp.where(m, a * pltpu.roll(b, SUBLANE - sh, 0) + b, b)
            a = jnp.where(m, a * pltpu.roll(a, SUBLANE - sh, 0), a)
        else:
            m = row >= sh
            b = jnp.where(m, a * pltpu.roll(b, sh, 0) + b, b)
            a = jnp.where(m, a * pltpu.roll(a, sh, 0), a)
    return a, b


def _lru_gates(ub, bda, bdx, ba, bx, sp):
    r = _sigmoid(_dot_nn(ub, bda) + ba)
    i = _sigmoid(_dot_nn(ub, bdx) + bx)
    log_a = (-LRU_C) * r * sp
    a = jnp.exp(log_a)
    s = jnp.sqrt(-jnp.tanh(log_a) * (1.0 + a * a))
    return r, i, a, s


def _mixer_fwd(p, caw, cbw, vec, bda, bdx, name, comms=()):
    t = p.shape[0]
    d = p.shape[1] // N_PROJ
    tm = _time_tile(t)
    nk = tm // SUBLANE
    cw = min(MXU_TILE, d)
    nb = d // cw

    def body(ba_ref, ca_ref, xa_ref, xb_ref, gb_ref, caw_ref, cbw_ref, vec_ref, bda_ref, bdx_ref,
             ya_ref, yb_ref, u_ref, h_ref, zbuf, xbuf, abuf, bbuf, z_tail, x_tail, h_carry):
        @pl.when(pl.program_id(0) == 0)
        def _():
            _zero(z_tail)
            _zero(x_tail)
            _zero(h_carry)

        row = lax.broadcasted_iota(jnp.int32, (SUBLANE, cw), 0)
        for j in range(nb):
            cs = slice(j * cw, (j + 1) * cw)
            z = ca_ref[:, cs].astype(F32) * xa_ref[:, cs].astype(F32)
            _causal_fill(zbuf, z, z_tail[:, cs], CONV_A_K - 1, row)
            z_tail[:, cs] = z[tm - (CONV_A_K - 1) * SUBLANE:, :]
            cz = caw_ref[0:1, cs] * zbuf[0:tm, :] + caw_ref[1:2, cs] * zbuf[SUBLANE:SUBLANE + tm, :] + caw_ref[2:3, cs] * z
            ya_ref[:, cs] = (ba_ref[:, cs].astype(F32) * cz).astype(BF16)
            xb = xb_ref[:, cs].astype(F32)
            _causal_fill(xbuf, xb, x_tail[:, cs], CONV_B_K - 1, row)
            x_tail[:, cs] = xb[tm - (CONV_B_K - 1) * SUBLANE:, :]
            u = (cbw_ref[0:1, cs] * xbuf[0:tm, :] + cbw_ref[1:2, cs] * xbuf[SUBLANE:SUBLANE + tm, :]
                 + cbw_ref[2:3, cs] * xbuf[2 * SUBLANE:2 * SUBLANE + tm, :] + cbw_ref[3:4, cs] * xb + vec_ref[0:1, cs])
            ub = u.astype(BF16)
            u = ub.astype(F32)
            _, gi, a, s = _lru_gates(ub, bda_ref[j], bdx_ref[j], vec_ref[1:2, cs], vec_ref[2:3, cs], vec_ref[3:4, cs])
            abuf[...] = a
            bbuf[...] = s * (gi * u)
            h_end, a_prod = _chain_scan(abuf, bbuf, nk, reverse=False)
            a_inc, h_inc = _sublane_scan(a_prod, h_end, row, reverse=False)
            carry = h_carry[:, cs]
            ends = h_inc + a_inc * carry
            starts = jnp.where(row == 0, carry, pltpu.roll(ends, 1, 0))
            h_carry[:, cs] = jnp.broadcast_to(ends[SUBLANE - 1:SUBLANE, :], (SUBLANE, cw))
            h = (bbuf[...].reshape(nk, SUBLANE, cw) + abuf[...].reshape(nk, SUBLANE, cw) * starts[None]).reshape(tm, cw)
            gel, _ = _gelu_and_grad(gb_ref[:, cs].astype(F32))
            yb_ref[:, cs] = (h * gel).astype(BF16)
            u_ref[:, cs] = ub
            h_ref[:, cs] = h.astype(BF16)

    slab = lambda s: pl.BlockSpec((tm, d), lambda i, s=s: (i, s))
    small = pl.BlockSpec((SUBLANE, d), lambda i: (0, 0))
    bd = pl.BlockSpec((nb, cw, cw), lambda i: (0, 0, 0))
    out = pl.BlockSpec((tm, d), lambda i: (i, 0))
    return _hosted_call(
        body, comms, name=name, grid=(t // tm,),
        out_shape=[jax.ShapeDtypeStruct((t, d), BF16)] * 4,
        in_specs=[slab(0), slab(1), slab(2), slab(3), slab(4), small, small, small, bd, bd],
        out_specs=[out] * 4,
        scratch_shapes=[pltpu.VMEM((tm + (CONV_A_K - 1) * SUBLANE, cw), F32), pltpu.VMEM((tm + (CONV_B_K - 1) * SUBLANE, cw), F32),
                        pltpu.VMEM((tm, cw), F32), pltpu.VMEM((tm, cw), F32),
                        pltpu.VMEM(((CONV_A_K - 1) * SUBLANE, d), F32), pltpu.VMEM(((CONV_B_K - 1) * SUBLANE, d), F32),
                        pltpu.VMEM((SUBLANE, d), F32)],
        args=(p, p, p, p, p, caw, cbw, vec, bda, bdx))


def _merge_fwd(x, ya, yb, p, gbias, gath, grp, name):
    t, d = x.shape
    tm = _row_tile(t, 512)

    def body(x_ref, ya_ref, yb_ref, ga_ref, gb_ref, gbias_ref, gath_ref, x1_ref, oa_ref, ob_ref, w_oa, w_ob, w_o, sems):
        @pl.when(pl.program_id(0) == 0)
        def _():
            _load_weights(gath_ref, grp, ["oa", "ob", "o"], [w_oa, w_ob, w_o], sems)

        oa = _dot_nn(ya_ref[...], w_oa[...]).astype(BF16)
        ob = _dot_nn(yb_ref[...], w_ob[...]).astype(BF16)
        oa_ref[...] = oa
        ob_ref[...] = ob
        sa = _sigmoid(ga_ref[...].astype(F32) + gbias_ref[0:1, :])
        sb = _sigmoid(gb_ref[...].astype(F32) + gbias_ref[1:2, :])
        merged = (sa * oa.astype(F32) + sb * ob.astype(F32)).astype(BF16)
        x1_ref[...] = x_ref[...] + _dot_nn(merged, w_o[...])

    row = pl.BlockSpec((tm, d), lambda i: (i, 0))
    return pl.pallas_call(
        body, name=name, grid=(t // tm,),
        out_shape=[jax.ShapeDtypeStruct((t, d), F32), jax.ShapeDtypeStruct((t, d), BF16), jax.ShapeDtypeStruct((t, d), BF16)],
        in_specs=[row, row, row, pl.BlockSpec((tm, d), lambda i: (i, 5)), pl.BlockSpec((tm, d), lambda i: (i, 6)),
                  pl.BlockSpec((SUBLANE, d), lambda i: (0, 0)), ANY],
        out_specs=[row, row, row],
        scratch_shapes=[pltpu.VMEM((d, d), BF16)] * 3 + [pltpu.SemaphoreType.DMA((3 * N_DEV,))],
        compiler_params=_params(),
    )(x, ya, yb, p, p, gbias, gath)


def _ffn_fwd(x1, g_row, gath, grp, name):
    t, d = x1.shape
    ff = grp.rows["g"] * N_DEV
    tm = _row_tile(t, 512)
    fc = MXU_TILE
    assert ff % fc == 0

    def body(x_ref, g_ref, gath_ref, x2_ref, gg_ref, uu_ref, w_g, w_u, w_d, acc, sems):
        @pl.when(pl.program_id(0) == 0)
        def _():
            _load_weights(gath_ref, grp, ["g", "u", "d"], [w_g, w_u, w_d], sems)

        xf = x_ref[...]
        rstd = lax.rsqrt(jnp.mean(xf * xf, axis=-1, keepdims=True) + RMS_EPS)
        h = (xf * rstd * g_ref[...]).astype(BF16)
        acc[...] = xf
        for c in range(ff // fc):
            fs = slice(c * fc, (c + 1) * fc)
            gg = _dot_nt(h, w_g[fs, :]).astype(BF16)
            uu = _dot_nt(h, w_u[fs, :]).astype(BF16)
            gg_ref[:, fs] = gg
            uu_ref[:, fs] = uu
            g32 = gg.astype(F32)
            f = (g32 * _sigmoid(g32) * uu.astype(F32)).astype(BF16)
            acc[...] += _dot_nn(f, w_d[fs, :])
        x2_ref[...] = acc[...]

    row = pl.BlockSpec((tm, d), lambda i: (i, 0))
    wide = pl.BlockSpec((tm, ff), lambda i: (i, 0))
    return pl.pallas_call(
        body, name=name, grid=(t // tm,),
        out_shape=[jax.ShapeDtypeStruct((t, d), F32), jax.ShapeDtypeStruct((t, ff), BF16), jax.ShapeDtypeStruct((t, ff), BF16)],
        in_specs=[row, pl.BlockSpec((1, d), lambda i: (0, 0)), ANY],
        out_specs=[row, wide, wide],
        scratch_shapes=[pltpu.VMEM((ff, d), BF16)] * 3 + [pltpu.VMEM((tm, d), F32), pltpu.SemaphoreType.DMA((3 * N_DEV,))],
        compiler_params=_params(),
    )(x1, g_row, gath)


def _loss_head(x, g_row, target, name):
    t, d = x.shape
    tm = _row_tile(t, 512)

    def body(x_ref, g_ref, tgt_ref, loss_ref, dx_ref, dg_ref):
        @pl.when(pl.program_id(0) == 0)
        def _():
            _zero(loss_ref)
            _zero(dg_ref)

        xf = x_ref[...]
        rstd = lax.rsqrt(jnp.mean(xf * xf, axis=-1, keepdims=True) + RMS_EPS)
        xh = xf * rstd
        g = g_ref[...]
        err = xh * g - tgt_ref[...]
        loss_ref[...] += 0.5 * jnp.sum(jnp.sum(err * err, axis=-1, keepdims=True), axis=0, keepdims=True) * (1.0 / d)
        dy = err * (1.0 / d)
        dg_ref[0:1, :] += jnp.sum(dy * xh, axis=0, keepdims=True)
        dxh = dy * g
        dx_ref[...] = rstd * (dxh - xh * jnp.mean(dxh * xh, axis=-1, keepdims=True))

    row = pl.BlockSpec((tm, d), lambda i: (i, 0))
    return pl.pallas_call(
        body, name=name, grid=(t // tm,),
        out_shape=[jax.ShapeDtypeStruct((SUBLANE, LANE), F32), jax.ShapeDtypeStruct((t, d), F32),
                   jax.ShapeDtypeStruct((SUBLANE, d), F32)],
        in_specs=[row, pl.BlockSpec((1, d), lambda i: (0, 0)), row],
        out_specs=[pl.BlockSpec((SUBLANE, LANE), lambda i: (0, 0)), row, pl.BlockSpec((SUBLANE, d), lambda i: (0, 0))],
        compiler_params=_params(),
    )(x, g_row, target)


def _ffn_bwd_act(dx2, gg, uu, gath, grp, name):
    t, d = dx2.shape
    ff = grp.rows["g"] * N_DEV
    tm = _row_tile(t, 512)
    fc = MXU_TILE

    def body(dx2_ref, gg_ref, uu_ref, gath_ref, dgg_ref, duu_ref, f_ref, dx2b_ref, w_d, sems):
        @pl.when(pl.program_id(0) == 0)
        def _():
            _load_weights(gath_ref, grp, ["d"], [w_d], sems)

        dx2b = dx2_ref[...].astype(BF16)
        dx2b_ref[...] = dx2b
        for c in range(ff // fc):
            fs = slice(c * fc, (c + 1) * fc)
            df = _dot_nt(dx2b, w_d[fs, :])
            g32 = gg_ref[:, fs].astype(F32)
            u32 = uu_ref[:, fs].astype(F32)
            sg = _sigmoid(g32)
            silu = g32 * sg
            f_ref[:, fs] = (silu * u32).astype(BF16)
            duu_ref[:, fs] = (df * silu).astype(BF16)
            dgg_ref[:, fs] = (df * u32 * (sg * (1.0 + g32 * (1.0 - sg)))).astype(BF16)

    row = pl.BlockSpec((tm, d), lambda i: (i, 0))
    wide = pl.BlockSpec((tm, ff), lambda i: (i, 0))
    sd = jax.ShapeDtypeStruct
    return pl.pallas_call(
        body, name=name, grid=(t // tm,),
        out_shape=[sd((t, ff), BF16), sd((t, ff), BF16), sd((t, ff), BF16), sd((t, d), BF16)],
        in_specs=[row, wide, wide, ANY],
        out_specs=[wide, wide, wide, row],
        scratch_shapes=[pltpu.VMEM((ff, d), BF16), pltpu.SemaphoreType.DMA((N_DEV,))],
        compiler_params=_params(),
    )(dx2, gg, uu, gath)


def _ffn_bwd_in(dgg, duu, dx2, x1, g_row, gath, grp, name, comms=()):
    t, d = x1.shape
    ff = grp.rows["g"] * N_DEV
    tm = _row_tile(t, 512)

    def body(dgg_ref, duu_ref, dx2_ref, x_ref, g_ref, gath_ref, dx1_ref, dx1b_ref, h_ref, dg_ref, w_g, w_u, sems):
        @pl.when(pl.program_id(0) == 0)
        def _():
            _load_weights(gath_ref, grp, ["g", "u"], [w_g, w_u], sems)
            _zero(dg_ref)

        dh = _dot_nn(dgg_ref[...], w_g[...]) + _dot_nn(duu_ref[...], w_u[...])
        xf = x_ref[...]
        rstd = lax.rsqrt(jnp.mean(xf * xf, axis=-1, keepdims=True) + RMS_EPS)
        xh = xf * rstd
        g = g_ref[...]
        h_ref[...] = (xh * g).astype(BF16)
        dg_ref[0:1, :] += jnp.sum(dh * xh, axis=0, keepdims=True)
        dxh = dh * g
        dx1 = dx2_ref[...] + rstd * (dxh - xh * jnp.mean(dxh * xh, axis=-1, keepdims=True))
        dx1_ref[...] = dx1
        dx1b_ref[...] = dx1.astype(BF16)

    row = pl.BlockSpec((tm, d), lambda i: (i, 0))
    wide = pl.BlockSpec((tm, ff), lambda i: (i, 0))
    sd = jax.ShapeDtypeStruct
    return _hosted_call(
        body, comms, name=name, grid=(t // tm,),
        out_shape=[sd((t, d), F32), sd((t, d), BF16), sd((t, d), BF16), sd((SUBLANE, d), F32)],
        in_specs=[wide, wide, row, row, pl.BlockSpec((1, d), lambda i: (0, 0)), ANY],
        out_specs=[row, row, row, pl.BlockSpec((SUBLANE, d), lambda i: (0, 0))],
        scratch_shapes=[pltpu.VMEM((ff, d), BF16)] * 2 + [pltpu.SemaphoreType.DMA((2 * N_DEV,))],
        args=(dgg, duu, dx2, x1, g_row, gath))


def _merge_bwd(dx1b, oa, ob, p, gbias, gath, grp, name):
    t, d = oa.shape
    tm = _row_tile(t, 512)

    def body(dx_ref, oa_ref, ob_ref, ga_ref, gb_ref, gbias_ref, gath_ref,
             dya_ref, dyb_ref, dpg_ref, mg_ref, doa_ref, dob_ref, dgb_ref, w_oa, w_ob, w_o, sems):
        @pl.when(pl.program_id(0) == 0)
        def _():
            _load_weights(gath_ref, grp, ["oa", "ob", "o"], [w_oa, w_ob, w_o], sems)
            _zero(dgb_ref)

        dm = _dot_nt(dx_ref[...], w_o[...])
        oa = oa_ref[...].astype(F32)
        ob = ob_ref[...].astype(F32)
        sa = _sigmoid(ga_ref[...].astype(F32) + gbias_ref[0:1, :])
        sb = _sigmoid(gb_ref[...].astype(F32) + gbias_ref[1:2, :])
        mg_ref[...] = (sa * oa + sb * ob).astype(BF16)
        doa = (dm * sa).astype(BF16)
        dob = (dm * sb).astype(BF16)
        doa_ref[...] = doa
        dob_ref[...] = dob
        dga = dm * oa * sa * (1.0 - sa)
        dgb = dm * ob * sb * (1.0 - sb)
        dpg_ref[:, 0:d] = dga.astype(BF16)
        dpg_ref[:, d:2 * d] = dgb.astype(BF16)
        dgb_ref[0:1, :] += jnp.sum(dga, axis=0, keepdims=True)
        dgb_ref[1:2, :] += jnp.sum(dgb, axis=0, keepdims=True)
        dya_ref[...] = _dot_nt(doa, w_oa[...]).astype(BF16)
        dyb_ref[...] = _dot_nt(dob, w_ob[...]).astype(BF16)

    row = pl.BlockSpec((tm, d), lambda i: (i, 0))
    sd = jax.ShapeDtypeStruct
    return pl.pallas_call(
        body, name=name, grid=(t // tm,),
        out_shape=[sd((t, d), BF16), sd((t, d), BF16), sd((t, 2 * d), BF16), sd((t, d), BF16), sd((t, d), BF16),
                   sd((t, d), BF16), sd((SUBLANE, d), F32)],
        in_specs=[row, row, row, pl.BlockSpec((tm, d), lambda i: (i, 5)), pl.BlockSpec((tm, d), lambda i: (i, 6)),
                  pl.BlockSpec((SUBLANE, d), lambda i: (0, 0)), ANY],
        out_specs=[row, row, pl.BlockSpec((tm, 2 * d), lambda i: (i, 0)), row, row, row,
                   pl.BlockSpec((SUBLANE, d), lambda i: (0, 0))],
        scratch_shapes=[pltpu.VMEM((d, d), BF16)] * 3 + [pltpu.SemaphoreType.DMA((3 * N_DEV,))],
        compiler_params=_params(),
    )(dx1b, oa, ob, p, p, gbias, gath)


DV_CONV_B_B, DV_BA, DV_BX, DV_SP, DV_CONV_A, DV_CONV_B = 0, 1, 2, 3, 4, 7
DV_ROWS = 16


def _mixer_bwd(dya, dyb, dpg, p, u_s, h_s, caw, cbw, vec, bda, bdx, name, comms=()):
    t, d = dya.shape
    tm = _time_tile(t)
    n_t = t // tm
    nk = tm // SUBLANE
    cw = min(MXU_TILE, d)
    nb = d // cw
    halo = 4 * SUBLANE
    ka, kb = CONV_A_K - 1, CONV_B_K - 1

    def body(dya_ref, dyb_ref, dpg_ref, ba_ref, ca_ref, xa_ref, xb_ref, gb_ref, cah_ref, xah_ref, xbh_ref,
             u_ref, h_ref, hh_ref, caw_ref, cbw_ref, vec_ref, bda_ref, bdx_ref,
             dp_ref, dv_ref, dwa_ref, dwx_ref,
             zbuf, xbuf, hbuf, dczbuf, dubuf, a2buf, a1buf, lbuf, dcz_head, du_head, a_head, lam_head):
        i = pl.program_id(0)

        @pl.when(i == 0)
        def _():
            for ref in (dv_ref, dwa_ref, dwx_ref, dcz_head, du_head, a_head, lam_head):
                _zero(ref)

        has_prev = jnp.where(i < n_t - 1, 1.0, 0.0).astype(F32)
        row = lax.broadcasted_iota(jnp.int32, (SUBLANE, cw), 0)
        dp_ref[:, 5 * d:7 * d] = dpg_ref[...]

        def colsum(v):
            return jnp.sum(v, axis=0, keepdims=True)

        for j in range(nb):
            cs = slice(j * cw, (j + 1) * cw)
            ca = ca_ref[:, cs].astype(F32)
            xa = xa_ref[:, cs].astype(F32)
            z = ca * xa
            z_before = cah_ref[:, cs].astype(F32) * xah_ref[:, cs].astype(F32) * has_prev
            _causal_fill(zbuf, z, z_before[halo - ka * SUBLANE:, :], ka, row)
            z2 = zbuf[0:tm, :]
            z1 = zbuf[SUBLANE:SUBLANE + tm, :]
            w0, w1, w2 = caw_ref[0:1, cs], caw_ref[1:2, cs], caw_ref[2:3, cs]
            cz = w0 * z2 + w1 * z1 + w2 * z
            dya = dya_ref[:, cs].astype(F32)
            dp_ref[:, 0 * d + j * cw:0 * d + (j + 1) * cw] = (dya * cz).astype(BF16)
            dcz = dya * ba_ref[:, cs].astype(F32)
            _anticausal_fill(dczbuf, dcz, dcz_head[:, cs], ka, row)
            dcz_head[:, cs] = dcz[0:ka * SUBLANE, :]
            dz = w2 * dcz + w1 * dczbuf[SUBLANE:SUBLANE + tm, :] + w0 * dczbuf[2 * SUBLANE:2 * SUBLANE + tm, :]
            dv_ref[DV_CONV_A + 0:DV_CONV_A + 1, cs] += colsum(dcz * z2)
            dv_ref[DV_CONV_A + 1:DV_CONV_A + 2, cs] += colsum(dcz * z1)
            dv_ref[DV_CONV_A + 2:DV_CONV_A + 3, cs] += colsum(dcz * z)
            dp_ref[:, 1 * d + j * cw:1 * d + (j + 1) * cw] = (dz * xa).astype(BF16)
            dp_ref[:, 2 * d + j * cw:2 * d + (j + 1) * cw] = (dz * ca).astype(BF16)
            h = h_ref[:, cs].astype(F32)
            h_before = hh_ref[:, cs].astype(F32) * has_prev
            _causal_fill(hbuf, h, h_before[halo - SUBLANE:, :], 1, row)
            h_prev = hbuf[0:tm, :]
            dyb = dyb_ref[:, cs].astype(F32)
            gel, dgel = _gelu_and_grad(gb_ref[:, cs].astype(F32))
            dp_ref[:, 4 * d + j * cw:4 * d + (j + 1) * cw] = (dyb * h * dgel).astype(BF16)
            ub = u_ref[:, cs]
            u = ub.astype(F32)
            sp = vec_ref[3:4, cs]
            r, gi, a, s = _lru_gates(ub, bda_ref[j], bdx_ref[j], vec_ref[1:2, cs], vec_ref[2:3, cs], sp)
            _anticausal_fill(a2buf, a, a_head[:, cs], 1, row)
            a_head[:, cs] = a[0:SUBLANE, :]
            a1buf[...] = a2buf[SUBLANE:SUBLANE + tm, :]
            lbuf[...] = dyb * gel
            l_end, a_prod = _chain_scan(a1buf, lbuf, nk, reverse=True)
            a_inc, l_inc = _sublane_scan(a_prod, l_end, row, reverse=True)
            carry = lam_head[:, cs]
            ends = l_inc + a_inc * carry
            starts = jnp.where(row == SUBLANE - 1, carry, pltpu.roll(ends, SUBLANE - 1, 0))
            lam_head[:, cs] = jnp.broadcast_to(ends[0:1, :], (SUBLANE, cw))
            lam = (lbuf[...].reshape(nk, SUBLANE, cw) + a1buf[...].reshape(nk, SUBLANE, cw) * starts[None]).reshape(tm, cw)
            da = lam * h_prev
            iu = gi * u
            ds = lam * iu
            di = lam * s * u
            du = lam * s * gi
            dlog_a = da * a - ds * (a * a) / s
            dv_ref[DV_SP:DV_SP + 1, cs] += colsum(dlog_a * r) * (-LRU_C)
            dpr = dlog_a * ((-LRU_C) * sp) * r * (1.0 - r)
            dpi = di * gi * (1.0 - gi)
            dv_ref[DV_BA:DV_BA + 1, cs] += colsum(dpr)
            dv_ref[DV_BX:DV_BX + 1, cs] += colsum(dpi)
            dprb = dpr.astype(BF16)
            dpib = dpi.astype(BF16)
            du = du + _dot_nt(dprb, bda_ref[j]) + _dot_nt(dpib, bdx_ref[j])
            dwa_ref[j] += _dot_tn(ub, dprb)
            dwx_ref[j] += _dot_tn(ub, dpib)
            xb = xb_ref[:, cs].astype(F32)
            x_before = xbh_ref[:, cs].astype(F32) * has_prev
            _causal_fill(xbuf, xb, x_before[halo - kb * SUBLANE:, :], kb, row)
            _anticausal_fill(dubuf, du, du_head[:, cs], kb, row)
            du_head[:, cs] = du[0:kb * SUBLANE, :]
            v0, v1, v2, v3 = cbw_ref[0:1, cs], cbw_ref[1:2, cs], cbw_ref[2:3, cs], cbw_ref[3:4, cs]
            dxb = (v3 * du + v2 * dubuf[SUBLANE:SUBLANE + tm, :] + v1 * dubuf[2 * SUBLANE:2 * SUBLANE + tm, :]
                   + v0 * dubuf[3 * SUBLANE:3 * SUBLANE + tm, :])
            dp_ref[:, 3 * d + j * cw:3 * d + (j + 1) * cw] = dxb.astype(BF16)
            dv_ref[DV_CONV_B_B:DV_CONV_B_B + 1, cs] += colsum(du)
            dv_ref[DV_CONV_B + 0:DV_CONV_B + 1, cs] += colsum(du * xbuf[0:tm, :])
            dv_ref[DV_CONV_B + 1:DV_CONV_B + 2, cs] += colsum(du * xbuf[SUBLANE:SUBLANE + tm, :])
            dv_ref[DV_CONV_B + 2:DV_CONV_B + 3, cs] += colsum(du * xbuf[2 * SUBLANE:2 * SUBLANE + tm, :])
            dv_ref[DV_CONV_B + 3:DV_CONV_B + 4, cs] += colsum(du * xb)

    rt = lambda i: n_t - 1 - i
    row_spec = pl.BlockSpec((tm, d), lambda i: (rt(i), 0))
    slab = lambda s: pl.BlockSpec((tm, d), lambda i, s=s: (rt(i), s))
    before = lambda s: pl.BlockSpec((halo, d), lambda i, s=s: (jnp.maximum(rt(i) * (tm // halo) - 1, 0), s))
    small = pl.BlockSpec((SUBLANE, d), lambda i: (0, 0))
    bd = pl.BlockSpec((nb, cw, cw), lambda i: (0, 0, 0))
    sd = jax.ShapeDtypeStruct
    wbuf = lambda n: pltpu.VMEM((tm + n * SUBLANE, cw), F32)
    head = lambda n: pltpu.VMEM((n * SUBLANE, d), F32)
    return _hosted_call(
        body, comms, name=name, grid=(n_t,),
        out_shape=[sd((t, N_PROJ * d), BF16), sd((DV_ROWS, d), F32), sd((nb, cw, cw), F32), sd((nb, cw, cw), F32)],
        in_specs=[row_spec, row_spec, pl.BlockSpec((tm, 2 * d), lambda i: (rt(i), 0)),
                  slab(0), slab(1), slab(2), slab(3), slab(4), before(1), before(2), before(3),
                  row_spec, row_spec, before(0), small, small, small, bd, bd],
        out_specs=[pl.BlockSpec((tm, N_PROJ * d), lambda i: (rt(i), 0)), pl.BlockSpec((DV_ROWS, d), lambda i: (0, 0)), bd, bd],
        scratch_shapes=[wbuf(ka), wbuf(kb), wbuf(1), wbuf(ka), wbuf(kb), wbuf(1),
                        pltpu.VMEM((tm, cw), F32), pltpu.VMEM((tm, cw), F32), head(ka), head(kb), head(1), head(1)],
        args=(dya, dyb, dpg, p, p, p, p, p, p, p, p, u_s, h_s, h_s, caw, cbw, vec, bda, bdx))


def _in_proj_bwd(dp, x, dx1, g_row, gath, grp, name, comms=()):
    t, d = x.shape
    tm = _row_tile(t, 256)
    n_in = N_PROJ * d

    def body(dp_ref, x_ref, dx1_ref, g_ref, gath_ref, dx_ref, dg_ref, w_in, sems):
        @pl.when(pl.program_id(0) == 0)
        def _():
            _load_weights(gath_ref, grp, ["in"], [w_in], sems)
            _zero(dg_ref)

        dh = _dot_nn(dp_ref[:, 0:d], w_in[0:d, :])
        for k in range(1, N_PROJ):
            dh = dh + _dot_nn(dp_ref[:, k * d:(k + 1) * d], w_in[k * d:(k + 1) * d, :])
        xf = x_ref[...]
        rstd = lax.rsqrt(jnp.mean(xf * xf, axis=-1, keepdims=True) + RMS_EPS)
        xh = xf * rstd
        g = g_ref[...]
        dg_ref[0:1, :] += jnp.sum(dh * xh, axis=0, keepdims=True)
        dxh = dh * g
        dx_ref[...] = dx1_ref[...] + rstd * (dxh - xh * jnp.mean(dxh * xh, axis=-1, keepdims=True))

    row = pl.BlockSpec((tm, d), lambda i: (i, 0))
    sd = jax.ShapeDtypeStruct
    return _hosted_call(
        body, comms, name=name, grid=(t // tm,),
        out_shape=[sd((t, d), F32), sd((SUBLANE, d), F32)],
        in_specs=[pl.BlockSpec((tm, n_in), lambda i: (i, 0)), row, row, pl.BlockSpec((1, d), lambda i: (0, 0)), ANY],
        out_specs=[row, pl.BlockSpec((SUBLANE, d), lambda i: (0, 0))],
        scratch_shapes=[pltpu.VMEM((n_in, d), BF16), pltpu.SemaphoreType.DMA((N_DEV,))],
        args=(dp, x, dx1, g_row, gath))


def _weight_grad(a, b, name):
    t, m = a.shape
    n = b.shape[1]
    bt = _row_tile(t, 512)
    bm = m
    for div in (1, 2, 4, 8):
        if m % div == 0 and (m // div) % LANE == 0 and (m // div) * n * 4 <= (12 << 20):
            bm = m // div
            break
    n_t = t // bt

    def body(a_ref, b_ref, o_ref, acc):
        k = pl.program_id(1)

        @pl.when(k == 0)
        def _():
            _zero(acc)

        acc[...] += _dot_tn(a_ref[...], b_ref[...])

        @pl.when(k == n_t - 1)
        def _():
            o_ref[...] = acc[...].astype(BF16)

    return pl.pallas_call(
        body, name=name, grid=(m // bm, n_t),
        out_shape=jax.ShapeDtypeStruct((m, n), BF16),
        in_specs=[pl.BlockSpec((bt, bm), lambda i, k: (k, i)), pl.BlockSpec((bt, n), lambda i, k: (k, 0))],
        out_specs=pl.BlockSpec((bm, n), lambda i, k: (i, 0)),
        scratch_shapes=[pltpu.VMEM((bm, n), F32)],
        compiler_params=_params(2),
    )(a, b)


def _adamw(w, g, m, v, name):
    r, c = w.shape
    tr = _fit_rows(r, c * 4)
    c1 = 1.0 - ADAM_B1 ** ADAM_STEP
    c2 = 1.0 - ADAM_B2 ** ADAM_STEP

    def body(w_ref, g_ref, m_ref, v_ref, d_ref, nm_ref, nv_ref):
        g32 = g_ref[...]
        nm = ADAM_B1 * m_ref[...] + (1.0 - ADAM_B1) * g32
        nv = ADAM_B2 * v_ref[...] + (1.0 - ADAM_B2) * (g32 * g32)
        nm_ref[...] = nm
        nv_ref[...] = nv
        d_ref[...] = -ADAM_LR * ((nm / c1) / (jnp.sqrt(nv / c2) + ADAM_EPS) + ADAM_WD * w_ref[...])

    spec = pl.BlockSpec((tr, c), lambda i: (i, 0))
    return pl.pallas_call(
        body, name=name, grid=(r // tr,),
        out_shape=[jax.ShapeDtypeStruct((r, c), F32)] * 3,
        in_specs=[spec] * 4, out_specs=[spec] * 3,
        compiler_params=_params(),
    )(w, g, m, v)


def _pad_rows(a, mult=SUBLANE):
    pad = (-a.shape[0]) % mult
    return a if pad == 0 else jnp.concatenate([a, jnp.zeros((pad,) + a.shape[1:], a.dtype)], axis=0)


REPLICATED = ("ln1_g", "conv_b_b", "lru_wa", "lru_ba", "lru_wx", "lru_bx", "lru_lambda", "ln2_g", "final_g")
SMALL_SHARDED = ("conv_a_w", "conv_b_w", "gate_bias")
MATRICES = ("w_in", "w_out_a", "w_out_b", "w_o", "w_ffn_gate", "w_ffn_up", "w_ffn_down")
ORDER = ("ln1_g", "w_in", "conv_a_w", "conv_b_w", "conv_b_b", "lru_wa", "lru_ba", "lru_wx", "lru_bx", "lru_lambda",
         "w_out_a", "w_out_b", "gate_bias", "w_o", "ln2_g", "w_ffn_gate", "w_ffn_up", "w_ffn_down", "final_g")


def kernel(x, ln1_g, w_in, conv_a_w, conv_b_w, conv_b_b, lru_wa, lru_ba, lru_wx, lru_bx, lru_lambda, w_out_a, w_out_b, gate_bias, w_o, ln2_g, w_ffn_gate, w_ffn_up, w_ffn_down, final_g, loss_target, m_ln1_g, m_w_in, m_conv_a_w, m_conv_b_w, m_conv_b_b, m_lru_wa, m_lru_ba, m_lru_wx, m_lru_bx, m_lru_lambda, m_w_out_a, m_w_out_b, m_gate_bias, m_w_o, m_ln2_g, m_w_ffn_gate, m_w_ffn_up, m_w_ffn_down, m_final_g, v_ln1_g, v_w_in, v_conv_a_w, v_conv_b_w, v_conv_b_b, v_lru_wa, v_lru_ba, v_lru_wx, v_lru_bx, v_lru_lambda, v_w_out_a, v_w_out_b, v_gate_bias, v_w_o, v_ln2_g, v_w_ffn_gate, v_w_ffn_up, v_w_ffn_down, v_final_g):
    w = dict(ln1_g=ln1_g, w_in=w_in, conv_a_w=conv_a_w, conv_b_w=conv_b_w, conv_b_b=conv_b_b, lru_wa=lru_wa,
             lru_ba=lru_ba, lru_wx=lru_wx, lru_bx=lru_bx, lru_lambda=lru_lambda, w_out_a=w_out_a, w_out_b=w_out_b,
             gate_bias=gate_bias, w_o=w_o, ln2_g=ln2_g, w_ffn_gate=w_ffn_gate, w_ffn_up=w_ffn_up,
             w_ffn_down=w_ffn_down, final_g=final_g)
    mom = dict(ln1_g=m_ln1_g, w_in=m_w_in, conv_a_w=m_conv_a_w, conv_b_w=m_conv_b_w, conv_b_b=m_conv_b_b,
               lru_wa=m_lru_wa, lru_ba=m_lru_ba, lru_wx=m_lru_wx, lru_bx=m_lru_bx, lru_lambda=m_lru_lambda,
               w_out_a=m_w_out_a, w_out_b=m_w_out_b, gate_bias=m_gate_bias, w_o=m_w_o, ln2_g=m_ln2_g,
               w_ffn_gate=m_w_ffn_gate, w_ffn_up=m_w_ffn_up, w_ffn_down=m_w_ffn_down, final_g=m_final_g)
    var = dict(ln1_g=v_ln1_g, w_in=v_w_in, conv_a_w=v_conv_a_w, conv_b_w=v_conv_b_w, conv_b_b=v_conv_b_b,
               lru_wa=v_lru_wa, lru_ba=v_lru_ba, lru_wx=v_lru_wx, lru_bx=v_lru_bx, lru_lambda=v_lru_lambda,
               w_out_a=v_w_out_a, w_out_b=v_w_out_b, gate_bias=v_gate_bias, w_o=v_w_o, ln2_g=v_ln2_g,
               w_ffn_gate=v_w_ffn_gate, w_ffn_up=v_w_ffn_up, w_ffn_down=v_w_ffn_down, final_g=v_final_g)

    _, t, d = x.shape
    n_layers = w_in.shape[0]
    ff = w_ffn_down.shape[1] * N_DEV
    dd = d // N_DEV
    hd = d // LRU_HEADS
    cw = min(MXU_TILE, d)
    nb = d // cw
    hpt = cw // hd
    grp = _groups(d, ff)
    me = 4 * lax.axis_index("x") + 2 * lax.axis_index("y") + lax.axis_index("c")
    tm_time = _time_tile(t)
    x0 = _to_tile_order(x[0], tm_time)
    target = _to_tile_order(loss_target[0], tm_time)

    packed = [{"in": jnp.swapaxes(w_in[l], 0, 1).astype(BF16),
               "rest": jnp.concatenate([w_out_a[l], w_out_b[l], w_o[l], jnp.swapaxes(w_ffn_gate[l], 0, 1),
                                        jnp.swapaxes(w_ffn_up[l], 0, 1), w_ffn_down[l]], axis=0).astype(BF16)}
              for l in range(n_layers)]
    n_small = CONV_A_K + CONV_B_K + 2
    small = _pad_rows(jnp.concatenate([conv_a_w, conv_b_w, gate_bias], axis=1).reshape(n_layers * n_small, dd))
    sp = jax.nn.softplus(-lru_lambda)
    vec = [_pad_rows(jnp.stack([conv_b_b[l], lru_ba[l], lru_bx[l], sp[l]])) for l in range(n_layers)]
    eye = jnp.eye(hpt, dtype=F32)

    def block_diag(wh):
        return jnp.einsum("jkab,kl->jkalb", wh.reshape(nb, hpt, hd, hd), eye).reshape(nb, cw, cw).astype(BF16)

    bda = [block_diag(lru_wa[l]) for l in range(n_layers)]
    bdx = [block_diag(lru_wx[l]) for l in range(n_layers)]

    gath = [dict() for _ in range(n_layers)]
    (gath[0]["in"],) = _comm_call(_Gather(packed[0]["in"]), "gather_in_0")
    saved = []
    xl = x0
    for l in range(n_layers):
        comms = [_Gather(packed[0]["rest"]), _Gather(small)] if l == 0 else []
        (p, h1b), got = _in_proj_fwd(xl, ln1_g[l][None], gath[l]["in"], grp["in"], f"in_proj_fwd_{l}", comms)
        if l == 0:
            gath[0]["rest"], small_g = got[0][0], got[1][0]
            small_full = jnp.swapaxes(small_g[:, :n_layers * n_small], 0, 1).reshape(n_layers, n_small, d)
            caw = [_pad_rows(small_full[k, 0:CONV_A_K]) for k in range(n_layers)]
            cbw = [_pad_rows(small_full[k, CONV_A_K:CONV_A_K + CONV_B_K]) for k in range(n_layers)]
            gbias = [_pad_rows(small_full[k, CONV_A_K + CONV_B_K:]) for k in range(n_layers)]
        comms = [_Gather(packed[l + 1]["in"]), _Gather(packed[l + 1]["rest"])] if l + 1 < n_layers else []
        (ya, yb, u_s, h_s), got = _mixer_fwd(p, caw[l], cbw[l], vec[l], bda[l], bdx[l], f"mixer_fwd_{l}", comms)
        if l + 1 < n_layers:
            gath[l + 1]["in"], gath[l + 1]["rest"] = got[0][0], got[1][0]
        x1, oa, ob = _merge_fwd(xl, ya, yb, p, gbias[l], gath[l]["rest"], grp["rest"], f"merge_fwd_{l}")
        x2, gg, uu = _ffn_fwd(x1, ln2_g[l][None], gath[l]["rest"], grp["rest"], f"ffn_fwd_{l}")
        saved.append(dict(x=xl, p=p, h1b=h1b, ya=ya, yb=yb, u=u_s, h=h_s, x1=x1, oa=oa, ob=ob, gg=gg, uu=uu))
        xl = x2
    loss_tile, dx, dfinal = _loss_head(xl, final_g[None], target, "loss_head")
    loss = lax.psum(loss_tile[0, 0], ("x", "y", "c"))

    def heads(dwb):
        blocks = jnp.diagonal(dwb.reshape(nb, hpt, hd, hpt, hd), axis1=1, axis2=3)
        return jnp.moveaxis(blocks, 3, 1).reshape(hd, d)

    layer_names = [n for n in REPLICATED if n != "final_g"] + list(SMALL_SHARDED)

    def layer_block(k):
        return jnp.concatenate([small_grads[k][n] for n in layer_names], axis=0)

    recv = [dict() for _ in range(n_layers)]
    small_grads = [None] * n_layers
    early_all = None
    for l in reversed(range(n_layers)):
        s = saved[l]
        dgg, duu, f, dx2b = _ffn_bwd_act(dx, s["gg"], s["uu"], gath[l]["rest"], grp["rest"], f"ffn_bwd_act_{l}")
        comms = []
        if l == 0:
            early = [layer_block(k) for k in range(1, n_layers)] + [_pad_rows(dfinal[0:1])]
            comms = [_Gather(jnp.concatenate(early, axis=0))]
        (dx1, dx1b, h2b, dln2), got = _ffn_bwd_in(dgg, duu, dx, s["x1"], ln2_g[l][None], gath[l]["rest"], grp["rest"],
                                                  f"ffn_bwd_in_{l}", comms)
        if l == 0:
            ((early_all,),) = got
        dya, dyb, dpg, mg, doa, dob, dgbias = _merge_bwd(dx1b, s["oa"], s["ob"], s["p"], gbias[l], gath[l]["rest"],
                                                         grp["rest"], f"merge_bwd_{l}")
        rest = {"oa": _weight_grad(s["ya"], doa, f"dw_out_a_{l}"), "ob": _weight_grad(s["yb"], dob, f"dw_out_b_{l}"),
                "o": _weight_grad(mg, dx1b, f"dw_o_{l}"), "g": _weight_grad(dgg, h2b, f"dw_ffn_gate_{l}"),
                "u": _weight_grad(duu, h2b, f"dw_ffn_up_{l}"), "d": _weight_grad(f, dx2b, f"dw_ffn_down_{l}")}
        (dp, dv, dwa, dwx), got = _mixer_bwd(dya, dyb, dpg, s["p"], s["u"], s["h"], caw[l], cbw[l], vec[l], bda[l], bdx[l],
                                             f"mixer_bwd_{l}", [_Exchange(rest, grp["rest"])])
        (recv[l]["rest"],) = got[0]
        small_grads[l] = {
            "conv_b_b": dv[DV_CONV_B_B:DV_CONV_B_B + 1], "lru_wa": heads(dwa),
            "lru_ba": dv[DV_BA:DV_BA + 1], "lru_wx": heads(dwx), "lru_bx": dv[DV_BX:DV_BX + 1],
            "lru_lambda": dv[DV_SP:DV_SP + 1] * (-jax.nn.sigmoid(-lru_lambda[l]))[None], "ln2_g": dln2[0:1],
            "conv_a_w": dv[DV_CONV_A:DV_CONV_A + CONV_A_K], "conv_b_w": dv[DV_CONV_B:DV_CONV_B + CONV_B_K],
            "gate_bias": dgbias[0:2],
        }
        dw_in = _Exchange({"in": _weight_grad(dp, s["h1b"], f"dw_in_{l}")}, grp["in"])
        (dx, dln1), got = _in_proj_bwd(dp, s["x"], dx1, ln1_g[l][None], gath[l]["in"], grp["in"], f"in_proj_bwd_{l}", [dw_in])
        ((recv[l]["in"],),) = got
        small_grads[l]["ln1_g"] = dln1[0:1]
    grad_x = _from_tile_order(dx, tm_time)[None]

    g = {}
    gsum = [{k: _sum_slots(recv[l][k], f"sum_{k}_{l}") for k in ("in", "rest")} for l in range(n_layers)]

    def part(key):
        k = "in" if key == "in" else "rest"
        o, r = grp[k].off[key], grp[k].rows[key]
        return jnp.stack([gsum[l][k][o:o + r] for l in range(n_layers)])

    g = {"w_in": jnp.swapaxes(part("in"), 1, 2), "w_out_a": part("oa"), "w_out_b": part("ob"), "w_o": part("o"),
         "w_ffn_gate": jnp.swapaxes(part("g"), 1, 2), "w_ffn_up": jnp.swapaxes(part("u"), 1, 2), "w_ffn_down": part("d")}
    (late_all,) = _comm_call(_Gather(layer_block(0)), "gather_small_grads_0")
    early_sum = _sum_slots(early_all, "sum_small_grads")
    block_rows = late_all.shape[1]
    per_layer = [_sum_slots(late_all, "sum_small_grads_0")]
    per_layer += [early_sum[(k - 1) * block_rows:k * block_rows] for k in range(1, n_layers)]
    g["final_g"] = early_sum[(n_layers - 1) * block_rows].reshape(w["final_g"].shape)
    o = 0
    for n in layer_names:
        rows = small_grads[0][n].shape[0]
        stacked = jnp.concatenate([per_layer[k][o:o + rows] for k in range(n_layers)], axis=0)
        if n in SMALL_SHARDED:
            g[n] = lax.dynamic_slice_in_dim(stacked, me * dd, dd, axis=1).reshape(n_layers, rows, dd)
        else:
            g[n] = stacked.reshape(w[n].shape)
        o += rows

    delta, new_m, new_v = {}, {}, {}
    for n in MATRICES:
        shape = w[n].shape
        flat = lambda a: a.reshape(shape[0] * shape[1], shape[2])
        dl, nm, nv = _adamw(flat(w[n]), flat(g[n]), flat(mom[n]), flat(var[n]), f"adamw_{n}")
        delta[n], new_m[n], new_v[n] = dl.reshape(shape), nm.reshape(shape), nv.reshape(shape)
    for group, width, name in ((REPLICATED, d, "adamw_replicated"), (SMALL_SHARDED, dd, "adamw_vectors")):
        cat = lambda src: _pad_rows(jnp.concatenate([src[n].reshape(-1, width) for n in group], axis=0))
        dl, nm, nv = _adamw(cat(w), cat(g), cat(mom), cat(var), name)
        o = 0
        for n in group:
            rows = w[n].size // width
            delta[n], new_m[n], new_v[n] = (a[o:o + rows].reshape(w[n].shape) for a in (dl, nm, nv))
            o += rows

    return (loss, grad_x, *[g[n] for n in ORDER], *[delta[n] for n in ORDER], *[new_m[n] for n in ORDER],
            *[new_v[n] for n in ORDER])
```

```python
import math

import jax
import jax.numpy as jnp
from jax import lax
from jax.experimental import pallas as pl
from jax.experimental.pallas import tpu as pltpu

F32 = jnp.float32
BF16 = jnp.bfloat16

N_DEV = 8
N_PROJ = 7
LRU_HEADS = 16
LRU_C = 8.0
RMS_EPS = 1e-6
CONV_A_K = 3
CONV_B_K = 4
GELU_C = math.sqrt(2.0 / math.pi)
GELU_A = 0.044715

ADAM_LR = 0.001
ADAM_B1 = 0.9
ADAM_B2 = 0.999
ADAM_EPS = 1e-08
ADAM_WD = 0.01
ADAM_STEP = 10

LANE = 128
SUBLANE = 8
MXU_TILE = 256
VMEM_LIMIT = 52 << 20
MESH = pl.DeviceIdType.MESH
ANY = pl.BlockSpec(memory_space=pl.ANY)


def _dot_nn(a, b):
    return lax.dot_general(a, b, (((1,), (0,)), ((), ())), preferred_element_type=F32)


def _dot_nt(a, b):
    return lax.dot_general(a, b, (((1,), (1,)), ((), ())), preferred_element_type=F32)


def _dot_tn(a, b):
    return lax.dot_general(a, b, (((0,), (0,)), ((), ())), preferred_element_type=F32)


def _sigmoid(x):
    return 1.0 / (1.0 + jnp.exp(-x))


def _gelu_and_grad(x):
    x2 = x * x
    t = jnp.tanh(GELU_C * x * (1.0 + GELU_A * x2))
    g = 0.5 * x * (1.0 + t)
    dg = 0.5 * (1.0 + t) + 0.5 * x * (1.0 - t * t) * GELU_C * (1.0 + 3.0 * GELU_A * x2)
    return g, dg


def _zero(ref):
    ref[...] = jnp.zeros(ref.shape, ref.dtype)


def _fit_rows(r, row_bytes, budget=1 << 20):
    fits = [t for t in range(16, r + 1, 16) if r % t == 0 and t * row_bytes <= budget]
    return max(fits) if fits else r


def _row_tile(t, want):
    tm = min(want, t // 2)
    assert t % tm == 0 and tm % SUBLANE == 0, (t, tm)
    return tm


def _params(n_grid=1, **kw):
    return pltpu.CompilerParams(dimension_semantics=("arbitrary",) * n_grid, vmem_limit_bytes=VMEM_LIMIT, **kw)


class _Group:
    def __init__(self, keys, rows):
        self.keys = keys
        self.rows = dict(zip(keys, rows))
        self.off, o = {}, 0
        for k in keys:
            self.off[k] = o
            o += self.rows[k]
        self.total = o


def _groups(d, ff):
    dd, ffs = d // N_DEV, ff // N_DEV
    return {"in": _Group(("in",), (N_PROJ * dd,)),
            "rest": _Group(("oa", "ob", "o", "g", "u", "d"), (dd, dd, dd, ffs, ffs, ffs))}


def _load_weights(g_ref, grp, keys, dsts, sems):
    copies = []
    for n, (k, dst) in enumerate(zip(keys, dsts)):
        rows, off = grp.rows[k], grp.off[k]
        copies += [pltpu.make_async_copy(g_ref.at[p, pl.ds(off, rows), :], dst.at[pl.ds(p * rows, rows), :],
                                         sems.at[n * N_DEV + p]) for p in range(N_DEV)]
    for c in copies:
        c.start()
    for c in copies:
        c.wait()


def _comm_sems():
    return [pltpu.SemaphoreType.DMA((N_DEV - 1,)), pltpu.SemaphoreType.DMA((N_DEV - 1,)), pltpu.SemaphoreType.DMA]


class _Gather:
    def __init__(self, x):
        self.inputs = [x]
        self.out_shape = [jax.ShapeDtypeStruct((N_DEV,) + x.shape, x.dtype)]
        self.scratch = _comm_sems()

    def _plan(self, ins, outs, scr):
        (x_ref,), (out_ref,), (send_sems, recv_sems, local_sem) = ins, outs, scr
        mx, my, mc = lax.axis_index("x"), lax.axis_index("y"), lax.axis_index("c")
        me, sibling = (mx, my, mc), (mx, my, 1 - mc)
        chips = [(1 - mx, my), (mx, 1 - my), (1 - mx, 1 - my)]

        def slot(px, py, pc):
            return out_ref.at[4 * px + 2 * py + pc]

        def copy(k, block, to, src=None):
            return pltpu.make_async_remote_copy(
                src_ref=slot(*block) if src is None else src, dst_ref=slot(*block),
                send_sem=send_sems.at[k], recv_sem=recv_sems.at[k], device_id=to, device_id_type=MESH)

        mine = lambda: pltpu.make_async_copy(x_ref, slot(*me), local_sem)
        first = [lambda: copy(0, me, sibling, src=x_ref)]
        first += [lambda j=j, chip=chip: copy(1 + j, me, (*chip, mc), src=x_ref) for j, chip in enumerate(chips)]
        landed = [lambda j=j, chip=chip: copy(1 + j, (*chip, mc), me) for j, chip in enumerate(chips)]
        passed = [lambda j=j, chip=chip: copy(4 + j, (*chip, mc), sibling) for j, chip in enumerate(chips)]
        from_sibling = [lambda: copy(0, sibling, me)]
        from_sibling += [lambda j=j, chip=chip: copy(4 + j, (*chip, 1 - mc), me) for j, chip in enumerate(chips)]
        return mine, first, landed, passed, from_sibling

    def start(self, ins, outs, scr):
        mine, first, _, _, _ = self._plan(ins, outs, scr)
        mine().start()
        for cp in first:
            cp().start()

    def mid(self, ins, outs, scr):
        _, _, landed, passed, _ = self._plan(ins, outs, scr)
        for got, fwd in zip(landed, passed):
            got().wait_recv()
            fwd().start()

    def finish(self, ins, outs, scr):
        mine, first, _, passed, from_sibling = self._plan(ins, outs, scr)
        for cp in from_sibling:
            cp().wait_recv()
        for cp in first + passed:
            cp().wait_send()
        mine().wait()


class _Exchange:
    def __init__(self, mats, grp):
        self.grp = grp
        self.inputs = [mats[k] for k in grp.keys]
        self.out_shape = [jax.ShapeDtypeStruct((N_DEV, grp.total, self.inputs[0].shape[1]), BF16)]
        self.scratch = _comm_sems()

    def _pieces(self, g_refs, out_ref, q, dst_slot):
        out = []
        for g_ref, k in zip(g_refs, self.grp.keys):
            rows = self.grp.rows[k]
            out.append((g_ref.at[pl.ds(pl.multiple_of(q * rows, 16), rows), :],
                        out_ref.at[dst_slot, pl.ds(self.grp.off[k], rows), :]))
        return out

    def start(self, ins, outs, scr):
        (out_ref,), (send_sems, recv_sems, local_sem) = outs, scr
        mx, my, mc = lax.axis_index("x"), lax.axis_index("y"), lax.axis_index("c")
        me = 4 * mx + 2 * my + mc
        for s, t in self._pieces(ins, out_ref, me, me):
            pltpu.make_async_copy(s, t, local_sem).start()
        for k in range(1, N_DEV):
            px, py, pc = mx ^ ((k >> 2) & 1), my ^ ((k >> 1) & 1), mc ^ (k & 1)
            for s, t in self._pieces(ins, out_ref, 4 * px + 2 * py + pc, me):
                pltpu.make_async_remote_copy(src_ref=s, dst_ref=t, send_sem=send_sems.at[k - 1],
                                             recv_sem=recv_sems.at[k - 1], device_id=(px, py, pc),
                                             device_id_type=MESH).start()

    def mid(self, ins, outs, scr):
        pass

    def finish(self, ins, outs, scr):
        (out_ref,), (send_sems, recv_sems, local_sem) = outs, scr
        mx, my, mc = lax.axis_index("x"), lax.axis_index("y"), lax.axis_index("c")
        whole = out_ref.at[0]
        for k in range(1, N_DEV):
            done = pltpu.make_async_remote_copy(src_ref=whole, dst_ref=whole, send_sem=send_sems.at[k - 1],
                                                recv_sem=recv_sems.at[k - 1], device_id=(mx, my, mc),
                                                device_id_type=MESH)
            done.wait_send()
            done.wait_recv()
        pltpu.make_async_copy(whole, whole, local_sem).wait()


def _split(refs, sizes):
    out, pos = [], 0
    for n in sizes:
        out.append(refs[pos:pos + n])
        pos += n
    return out


def _hosted_call(body, comms, *, name, grid, in_specs, out_specs, out_shape, scratch_shapes, args):
    n_steps = grid[0]
    nc = len(comms)
    sizes = ([len(in_specs)] + [len(c.inputs) for c in comms] + [len(out_specs)] + [len(c.out_shape) for c in comms]
             + [len(scratch_shapes)] + [len(c.scratch) for c in comms])

    def hosted(*refs):
        parts = _split(refs, sizes)
        ins, c_ins = parts[0], parts[1:1 + nc]
        outs, c_outs = parts[1 + nc], parts[2 + nc:2 + 2 * nc]
        scr, c_scr = parts[2 + 2 * nc], parts[3 + 2 * nc:]
        step = pl.program_id(0)
        if comms:
            @pl.when(step == 0)
            def _():
                for c, a, b, s in zip(comms, c_ins, c_outs, c_scr):
                    c.start(a, b, s)

            @pl.when(step == max(n_steps - 2, 0))
            def _():
                for c, a, b, s in zip(comms, c_ins, c_outs, c_scr):
                    c.mid(a, b, s)

        body(*ins, *outs, *scr)
        if comms:
            @pl.when(step == n_steps - 1)
            def _():
                for c, a, b, s in zip(comms, c_ins, c_outs, c_scr):
                    c.finish(a, b, s)

    res = pl.pallas_call(
        hosted, name=name, grid=grid,
        out_shape=[*out_shape, *[o for c in comms for o in c.out_shape]],
        in_specs=[*in_specs, *[ANY for c in comms for _ in c.inputs]],
        out_specs=[*out_specs, *[ANY for c in comms for _ in c.out_shape]],
        scratch_shapes=[*scratch_shapes, *[s for c in comms for s in c.scratch]],
        compiler_params=_params(),
    )(*args, *[a for c in comms for a in c.inputs])
    main, rest = res[:len(out_specs)], res[len(out_specs):]
    return main, _split(rest, [len(c.out_shape) for c in comms])


def _comm_call(comm, name):
    def body(*refs):
        ins, outs, scr = _split(refs, [len(comm.inputs), len(comm.out_shape), len(comm.scratch)])
        comm.start(ins, outs, scr)
        comm.mid(ins, outs, scr)
        comm.finish(ins, outs, scr)

    return pl.pallas_call(
        body, name=name, out_shape=comm.out_shape, in_specs=[ANY] * len(comm.inputs),
        out_specs=[ANY] * len(comm.out_shape), scratch_shapes=comm.scratch,
    )(*comm.inputs)


def _sum_slots(x, name):
    n, r, c = x.shape
    tr = _fit_rows(r, c * 4)

    def body(x_ref, o_ref):
        acc = x_ref[0].astype(F32)
        for p in range(1, n):
            acc = acc + x_ref[p].astype(F32)
        o_ref[...] = acc

    return pl.pallas_call(
        body, name=name, grid=(r // tr,),
        out_shape=jax.ShapeDtypeStruct((r, c), F32),
        in_specs=[pl.BlockSpec((n, tr, c), lambda i: (0, i, 0))],
        out_specs=pl.BlockSpec((tr, c), lambda i: (i, 0)),
        compiler_params=_params(),
    )(x)


def _in_proj_fwd(x, g_row, gath, grp, name, comms=()):
    t, d = x.shape
    tm = _row_tile(t, 256)
    n_in = N_PROJ * d

    def body(x_ref, g_ref, gath_ref, p_ref, h_ref, w_in, sems):
        @pl.when(pl.program_id(0) == 0)
        def _():
            _load_weights(gath_ref, grp, ["in"], [w_in], sems)

        xf = x_ref[...]
        rstd = lax.rsqrt(jnp.mean(xf * xf, axis=-1, keepdims=True) + RMS_EPS)
        h = (xf * rstd * g_ref[...]).astype(BF16)
        h_ref[...] = h
        for k in range(N_PROJ):
            p_ref[:, k * d:(k + 1) * d] = _dot_nt(h, w_in[k * d:(k + 1) * d, :]).astype(BF16)

    row = pl.BlockSpec((tm, d), lambda i: (i, 0))
    return _hosted_call(
        body, comms, name=name, grid=(t // tm,),
        out_shape=[jax.ShapeDtypeStruct((t, n_in), BF16), jax.ShapeDtypeStruct((t, d), BF16)],
        in_specs=[row, pl.BlockSpec((1, d), lambda i: (0, 0)), ANY],
        out_specs=[pl.BlockSpec((tm, n_in), lambda i: (i, 0)), row],
        scratch_shapes=[pltpu.VMEM((n_in, d), BF16), pltpu.SemaphoreType.DMA((N_DEV,))],
        args=(x, g_row, gath))


def _time_tile(t):
    return _row_tile(t, 256)


def _to_tile_order(a, tm):
    t, c = a.shape
    return jnp.swapaxes(a.reshape(t // tm, SUBLANE, tm // SUBLANE, c), 1, 2).reshape(t, c)


def _from_tile_order(a, tm):
    t, c = a.shape
    return jnp.swapaxes(a.reshape(t // tm, tm // SUBLANE, SUBLANE, c), 1, 2).reshape(t, c)


def _causal_fill(buf, v, prev_tail, n, row):
    tm = v.shape[0]
    for q in range(n):
        cur = v[tm - SUBLANE * (n - q):tm - SUBLANE * (n - q - 1), :]
        prv = prev_tail[SUBLANE * q:SUBLANE * (q + 1), :]
        buf[SUBLANE * q:SUBLANE * (q + 1), :] = jnp.where(row == 0, pltpu.roll(prv, 1, 0), pltpu.roll(cur, 1, 0))
    buf[SUBLANE * n:, :] = v


def _anticausal_fill(buf, v, next_head, n, row):
    tm = v.shape[0]
    buf[0:tm, :] = v
    for q in range(n):
        cur = v[SUBLANE * q:SUBLANE * (q + 1), :]
        nxt = next_head[SUBLANE * q:SUBLANE * (q + 1), :]
        buf[tm + SUBLANE * q:tm + SUBLANE * (q + 1), :] = jnp.where(
            row == SUBLANE - 1, pltpu.roll(nxt, SUBLANE - 1, 0), pltpu.roll(cur, SUBLANE - 1, 0))


def _chain_scan(abuf, bbuf, nk, reverse):
    cw = abuf.shape[1]

    def step(n, carry):
        h, c = carry
        r0 = pl.multiple_of((nk - 1 - n if reverse else n) * SUBLANE, SUBLANE)
        ak = abuf[pl.ds(r0, SUBLANE), :]
        h = ak * h + bbuf[pl.ds(r0, SUBLANE), :]
        c = ak * c
        bbuf[pl.ds(r0, SUBLANE), :] = h
        abuf[pl.ds(r0, SUBLANE), :] = c
        return h, c

    return lax.fori_loop(0, nk, step, (jnp.zeros((SUBLANE, cw), F32), jnp.ones((SUBLANE, cw), F32)), unroll=8)


def _sublane_scan(a, b, row, reverse):
    for sh in (1, 2, 4):
        if reverse:
            m = row < SUBLANE - sh
            b = jnp.where(m, a * pltpu.roll(b, SUBLANE - sh, 0) + b, b)
            a = jnp.where(m, a * pltpu.roll(a, SUBLANE - sh, 0), a)
        else:
            m = row >= sh
            b = jnp.where(m, a * pltpu.roll(b, sh, 0) + b, b)
            a = jnp.where(m, a * pltpu.roll(a, sh, 0), a)
    return a, b


def _lru_gates(ub, bda, bdx, ba, bx, sp):
    r = _sigmoid(_dot_nn(ub, bda) + ba)
    i = _sigmoid(_dot_nn(ub, bdx) + bx)
    log_a = (-LRU_C) * r * sp
    a = jnp.exp(log_a)
    s = jnp.sqrt(-jnp.tanh(log_a) * (1.0 + a * a))
    return r, i, a, s


def _mixer_fwd(p, caw, cbw, vec, bda, bdx, name, comms=()):
    t = p.shape[0]
    d = p.shape[1] // N_PROJ
    tm = _time_tile(t)
    nk = tm // SUBLANE
    cw = min(MXU_TILE, d)
    nb = d // cw

    def body(ba_ref, ca_ref, xa_ref, xb_ref, gb_ref, caw_ref, cbw_ref, vec_ref, bda_ref, bdx_ref,
             ya_ref, yb_ref, u_ref, h_ref, zbuf, xbuf, abuf, bbuf, z_tail, x_tail, h_carry):
        @pl.when(pl.program_id(0) == 0)
        def _():
            _zero(z_tail)
            _zero(x_tail)
            _zero(h_carry)

        row = lax.broadcasted_iota(jnp.int32, (SUBLANE, cw), 0)
        for j in range(nb):
            cs = slice(j * cw, (j + 1) * cw)
            z = ca_ref[:, cs].astype(F32) * xa_ref[:, cs].astype(F32)
            _causal_fill(zbuf, z, z_tail[:, cs], CONV_A_K - 1, row)
            z_tail[:, cs] = z[tm - (CONV_A_K - 1) * SUBLANE:, :]
            cz = caw_ref[0:1, cs] * zbuf[0:tm, :] + caw_ref[1:2, cs] * zbuf[SUBLANE:SUBLANE + tm, :] + caw_ref[2:3, cs] * z
            ya_ref[:, cs] = (ba_ref[:, cs].astype(F32) * cz).astype(BF16)
            xb = xb_ref[:, cs].astype(F32)
            _causal_fill(xbuf, xb, x_tail[:, cs], CONV_B_K - 1, row)
            x_tail[:, cs] = xb[tm - (CONV_B_K - 1) * SUBLANE:, :]
            u = (cbw_ref[0:1, cs] * xbuf[0:tm, :] + cbw_ref[1:2, cs] * xbuf[SUBLANE:SUBLANE + tm, :]
                 + cbw_ref[2:3, cs] * xbuf[2 * SUBLANE:2 * SUBLANE + tm, :] + cbw_ref[3:4, cs] * xb + vec_ref[0:1, cs])
            ub = u.astype(BF16)
            u = ub.astype(F32)
            _, gi, a, s = _lru_gates(ub, bda_ref[j], bdx_ref[j], vec_ref[1:2, cs], vec_ref[2:3, cs], vec_ref[3:4, cs])
            abuf[...] = a
            bbuf[...] = s * (gi * u)
            h_end, a_prod = _chain_scan(abuf, bbuf, nk, reverse=False)
            a_inc, h_inc = _sublane_scan(a_prod, h_end, row, reverse=False)
            carry = h_carry[:, cs]
            ends = h_inc + a_inc * carry
            starts = jnp.where(row == 0, carry, pltpu.roll(ends, 1, 0))
            h_carry[:, cs] = jnp.broadcast_to(ends[SUBLANE - 1:SUBLANE, :], (SUBLANE, cw))
            h = (bbuf[...].reshape(nk, SUBLANE, cw) + abuf[...].reshape(nk, SUBLANE, cw) * starts[None]).reshape(tm, cw)
            gel, _ = _gelu_and_grad(gb_ref[:, cs].astype(F32))
            yb_ref[:, cs] = (h * gel).astype(BF16)
            u_ref[:, cs] = ub
            h_ref[:, cs] = h.astype(BF16)

    slab = lambda s: pl.BlockSpec((tm, d), lambda i, s=s: (i, s))
    small = pl.BlockSpec((SUBLANE, d), lambda i: (0, 0))
    bd = pl.BlockSpec((nb, cw, cw), lambda i: (0, 0, 0))
    out = pl.BlockSpec((tm, d), lambda i: (i, 0))
    return _hosted_call(
        body, comms, name=name, grid=(t // tm,),
        out_shape=[jax.ShapeDtypeStruct((t, d), BF16)] * 4,
        in_specs=[slab(0), slab(1), slab(2), slab(3), slab(4), small, small, small, bd, bd],
        out_specs=[out] * 4,
        scratch_shapes=[pltpu.VMEM((tm + (CONV_A_K - 1) * SUBLANE, cw), F32), pltpu.VMEM((tm + (CONV_B_K - 1) * SUBLANE, cw), F32),
                        pltpu.VMEM((tm, cw), F32), pltpu.VMEM((tm, cw), F32),
                        pltpu.VMEM(((CONV_A_K - 1) * SUBLANE, d), F32), pltpu.VMEM(((CONV_B_K - 1) * SUBLANE, d), F32),
                        pltpu.VMEM((SUBLANE, d), F32)],
        args=(p, p, p, p, p, caw, cbw, vec, bda, bdx))


def _merge_fwd(x, ya, yb, p, gbias, gath, grp, name):
    t, d = x.shape
    tm = _row_tile(t, 512)

    def body(x_ref, ya_ref, yb_ref, ga_ref, gb_ref, gbias_ref, gath_ref, x1_ref, oa_ref, ob_ref, w_oa, w_ob, w_o, sems):
        @pl.when(pl.program_id(0) == 0)
        def _():
            _load_weights(gath_ref, grp, ["oa", "ob", "o"], [w_oa, w_ob, w_o], sems)

        oa = _dot_nn(ya_ref[...], w_oa[...]).astype(BF16)
        ob = _dot_nn(yb_ref[...], w_ob[...]).astype(BF16)
        oa_ref[...] = oa
        ob_ref[...] = ob
        sa = _sigmoid(ga_ref[...].astype(F32) + gbias_ref[0:1, :])
        sb = _sigmoid(gb_ref[...].astype(F32) + gbias_ref[1:2, :])
        merged = (sa * oa.astype(F32) + sb * ob.astype(F32)).astype(BF16)
        x1_ref[...] = x_ref[...] + _dot_nn(merged, w_o[...])

    row = pl.BlockSpec((tm, d), lambda i: (i, 0))
    return pl.pallas_call(
        body, name=name, grid=(t // tm,),
        out_shape=[jax.ShapeDtypeStruct((t, d), F32), jax.ShapeDtypeStruct((t, d), BF16), jax.ShapeDtypeStruct((t, d), BF16)],
        in_specs=[row, row, row, pl.BlockSpec((tm, d), lambda i: (i, 5)), pl.BlockSpec((tm, d), lambda i: (i, 6)),
                  pl.BlockSpec((SUBLANE, d), lambda i: (0, 0)), ANY],
        out_specs=[row, row, row],
        scratch_shapes=[pltpu.VMEM((d, d), BF16)] * 3 + [pltpu.SemaphoreType.DMA((3 * N_DEV,))],
        compiler_params=_params(),
    )(x, ya, yb, p, p, gbias, gath)


def _ffn_fwd(x1, g_row, gath, grp, name, comms=()):
    t, d = x1.shape
    ff = grp.rows["g"] * N_DEV
    tm = _row_tile(t, 512)
    fc = MXU_TILE
    assert ff % fc == 0

    def body(x_ref, g_ref, gath_ref, x2_ref, gg_ref, uu_ref, w_g, w_u, w_d, acc, sems):
        @pl.when(pl.program_id(0) == 0)
        def _():
            _load_weights(gath_ref, grp, ["g", "u", "d"], [w_g, w_u, w_d], sems)

        xf = x_ref[...]
        rstd = lax.rsqrt(jnp.mean(xf * xf, axis=-1, keepdims=True) + RMS_EPS)
        h = (xf * rstd * g_ref[...]).astype(BF16)
        acc[...] = xf
        for c in range(ff // fc):
            fs = slice(c * fc, (c + 1) * fc)
            gg = _dot_nt(h, w_g[fs, :]).astype(BF16)
            uu = _dot_nt(h, w_u[fs, :]).astype(BF16)
            gg_ref[:, fs] = gg
            uu_ref[:, fs] = uu
            g32 = gg.astype(F32)
            f = (g32 * _sigmoid(g32) * uu.astype(F32)).astype(BF16)
            acc[...] += _dot_nn(f, w_d[fs, :])
        x2_ref[...] = acc[...]

    row = pl.BlockSpec((tm, d), lambda i: (i, 0))
    wide = pl.BlockSpec((tm, ff), lambda i: (i, 0))
    return _hosted_call(
        body, comms, name=name, grid=(t // tm,),
        out_shape=[jax.ShapeDtypeStruct((t, d), F32), jax.ShapeDtypeStruct((t, ff), BF16), jax.ShapeDtypeStruct((t, ff), BF16)],
        in_specs=[row, pl.BlockSpec((1, d), lambda i: (0, 0)), ANY],
        out_specs=[row, wide, wide],
        scratch_shapes=[pltpu.VMEM((ff, d), BF16)] * 3 + [pltpu.VMEM((tm, d), F32), pltpu.SemaphoreType.DMA((3 * N_DEV,))],
        args=(x1, g_row, gath))


def _loss_head(x, g_row, target, name):
    t, d = x.shape
    tm = _row_tile(t, 512)

    def body(x_ref, g_ref, tgt_ref, loss_ref, dx_ref, dg_ref):
        @pl.when(pl.program_id(0) == 0)
        def _():
            _zero(loss_ref)
            _zero(dg_ref)

        xf = x_ref[...]
        rstd = lax.rsqrt(jnp.mean(xf * xf, axis=-1, keepdims=True) + RMS_EPS)
        xh = xf * rstd
        g = g_ref[...]
        err = xh * g - tgt_ref[...]
        loss_ref[...] += 0.5 * jnp.sum(jnp.sum(err * err, axis=-1, keepdims=True), axis=0, keepdims=True) * (1.0 / d)
        dy = err * (1.0 / d)
        dg_ref[0:1, :] += jnp.sum(dy * xh, axis=0, keepdims=True)
        dxh = dy * g
        dx_ref[...] = rstd * (dxh - xh * jnp.mean(dxh * xh, axis=-1, keepdims=True))

    row = pl.BlockSpec((tm, d), lambda i: (i, 0))
    return pl.pallas_call(
        body, name=name, grid=(t // tm,),
        out_shape=[jax.ShapeDtypeStruct((SUBLANE, LANE), F32), jax.ShapeDtypeStruct((t, d), F32),
                   jax.ShapeDtypeStruct((SUBLANE, d), F32)],
        in_specs=[row, pl.BlockSpec((1, d), lambda i: (0, 0)), row],
        out_specs=[pl.BlockSpec((SUBLANE, LANE), lambda i: (0, 0)), row, pl.BlockSpec((SUBLANE, d), lambda i: (0, 0))],
        compiler_params=_params(),
    )(x, g_row, target)


def _ffn_bwd_act(dx2, gg, uu, gath, grp, name):
    t, d = dx2.shape
    ff = grp.rows["g"] * N_DEV
    tm = _row_tile(t, 512)
    fc = MXU_TILE
    n_t = t // tm

    def body(dx2_ref, gg_ref, uu_ref, gath_ref, dgg_ref, duu_ref, dwd_ref, w_d, acc, sems):
        @pl.when(pl.program_id(0) == 0)
        def _():
            _load_weights(gath_ref, grp, ["d"], [w_d], sems)
            _zero(acc)

        dx2b = dx2_ref[...].astype(BF16)
        for c in range(ff // fc):
            fs = slice(c * fc, (c + 1) * fc)
            df = _dot_nt(dx2b, w_d[fs, :])
            g32 = gg_ref[:, fs].astype(F32)
            u32 = uu_ref[:, fs].astype(F32)
            sg = _sigmoid(g32)
            silu = g32 * sg
            acc[fs, :] += _dot_tn((silu * u32).astype(BF16), dx2b)
            duu_ref[:, fs] = (df * silu).astype(BF16)
            dgg_ref[:, fs] = (df * u32 * (sg * (1.0 + g32 * (1.0 - sg)))).astype(BF16)

        @pl.when(pl.program_id(0) == n_t - 1)
        def _():
            w_d[...] = acc[...].astype(BF16)
            out = pltpu.make_async_copy(w_d, dwd_ref, sems.at[0])
            out.start()
            out.wait()

    row = pl.BlockSpec((tm, d), lambda i: (i, 0))
    wide = pl.BlockSpec((tm, ff), lambda i: (i, 0))
    sd = jax.ShapeDtypeStruct
    return pl.pallas_call(
        body, name=name, grid=(n_t,),
        out_shape=[sd((t, ff), BF16), sd((t, ff), BF16), sd((ff, d), BF16)],
        in_specs=[row, wide, wide, ANY],
        out_specs=[wide, wide, ANY],
        scratch_shapes=[pltpu.VMEM((ff, d), BF16), pltpu.VMEM((ff, d), F32), pltpu.SemaphoreType.DMA((N_DEV,))],
        compiler_params=_params(),
    )(dx2, gg, uu, gath)


def _ffn_bwd_in(dgg, duu, dx2, x1, g_row, gath, grp, name, comms=()):
    t, d = x1.shape
    ff = grp.rows["g"] * N_DEV
    tm = _row_tile(t, 512)

    def body(dgg_ref, duu_ref, dx2_ref, x_ref, g_ref, gath_ref, dx1_ref, dx1b_ref, h_ref, dg_ref, w_g, w_u, sems):
        @pl.when(pl.program_id(0) == 0)
        def _():
            _load_weights(gath_ref, grp, ["g", "u"], [w_g, w_u], sems)
            _zero(dg_ref)

        dh = _dot_nn(dgg_ref[...], w_g[...]) + _dot_nn(duu_ref[...], w_u[...])
        xf = x_ref[...]
        rstd = lax.rsqrt(jnp.mean(xf * xf, axis=-1, keepdims=True) + RMS_EPS)
        xh = xf * rstd
        g = g_ref[...]
        h_ref[...] = (xh * g).astype(BF16)
        dg_ref[0:1, :] += jnp.sum(dh * xh, axis=0, keepdims=True)
        dxh = dh * g
        dx1 = dx2_ref[...] + rstd * (dxh - xh * jnp.mean(dxh * xh, axis=-1, keepdims=True))
        dx1_ref[...] = dx1
        dx1b_ref[...] = dx1.astype(BF16)

    row = pl.BlockSpec((tm, d), lambda i: (i, 0))
    wide = pl.BlockSpec((tm, ff), lambda i: (i, 0))
    sd = jax.ShapeDtypeStruct
    return _hosted_call(
        body, comms, name=name, grid=(t // tm,),
        out_shape=[sd((t, d), F32), sd((t, d), BF16), sd((t, d), BF16), sd((SUBLANE, d), F32)],
        in_specs=[wide, wide, row, row, pl.BlockSpec((1, d), lambda i: (0, 0)), ANY],
        out_specs=[row, row, row, pl.BlockSpec((SUBLANE, d), lambda i: (0, 0))],
        scratch_shapes=[pltpu.VMEM((ff, d), BF16)] * 2 + [pltpu.SemaphoreType.DMA((2 * N_DEV,))],
        args=(dgg, duu, dx2, x1, g_row, gath))


def _merge_bwd(dx1b, oa, ob, ya, yb, p, gbias, gath, grp, name):
    t, d = oa.shape
    tm = _row_tile(t, 512)
    n_t = t // tm

    def body(dx_ref, oa_ref, ob_ref, ya_ref, yb_ref, ga_ref, gb_ref, gbias_ref, gath_ref,
             dya_ref, dyb_ref, dpg_ref, dgb_ref, dwoa_ref, dwob_ref, dwo_ref,
             w_oa, w_ob, w_o, acc_oa, acc_ob, acc_o, sems):
        @pl.when(pl.program_id(0) == 0)
        def _():
            _load_weights(gath_ref, grp, ["oa", "ob", "o"], [w_oa, w_ob, w_o], sems)
            for ref in (dgb_ref, acc_oa, acc_ob, acc_o):
                _zero(ref)

        dxb = dx_ref[...]
        dm = _dot_nt(dxb, w_o[...])
        oa = oa_ref[...].astype(F32)
        ob = ob_ref[...].astype(F32)
        sa = _sigmoid(ga_ref[...].astype(F32) + gbias_ref[0:1, :])
        sb = _sigmoid(gb_ref[...].astype(F32) + gbias_ref[1:2, :])
        acc_o[...] += _dot_tn((sa * oa + sb * ob).astype(BF16), dxb)
        doa = (dm * sa).astype(BF16)
        dob = (dm * sb).astype(BF16)
        acc_oa[...] += _dot_tn(ya_ref[...], doa)
        acc_ob[...] += _dot_tn(yb_ref[...], dob)
        dga = dm * oa * sa * (1.0 - sa)
        dgb = dm * ob * sb * (1.0 - sb)
        dpg_ref[:, 0:d] = dga.astype(BF16)
        dpg_ref[:, d:2 * d] = dgb.astype(BF16)
        dgb_ref[0:1, :] += jnp.sum(dga, axis=0, keepdims=True)
        dgb_ref[1:2, :] += jnp.sum(dgb, axis=0, keepdims=True)
        dya_ref[...] = _dot_nt(doa, w_oa[...]).astype(BF16)
        dyb_ref[...] = _dot_nt(dob, w_ob[...]).astype(BF16)

        @pl.when(pl.program_id(0) == n_t - 1)
        def _():
            outs = []
            for n, (acc, stage, dst) in enumerate(((acc_oa, w_oa, dwoa_ref), (acc_ob, w_ob, dwob_ref), (acc_o, w_o, dwo_ref))):
                stage[...] = acc[...].astype(BF16)
                outs.append(pltpu.make_async_copy(stage, dst, sems.at[n]))
                outs[-1].start()
            for cp in outs:
                cp.wait()

    row = pl.BlockSpec((tm, d), lambda i: (i, 0))
    sd = jax.ShapeDtypeStruct
    return pl.pallas_call(
        body, name=name, grid=(n_t,),
        out_shape=[sd((t, d), BF16), sd((t, d), BF16), sd((t, 2 * d), BF16), sd((SUBLANE, d), F32),
                   sd((d, d), BF16), sd((d, d), BF16), sd((d, d), BF16)],
        in_specs=[row, row, row, row, row, pl.BlockSpec((tm, d), lambda i: (i, 5)), pl.BlockSpec((tm, d), lambda i: (i, 6)),
                  pl.BlockSpec((SUBLANE, d), lambda i: (0, 0)), ANY],
        out_specs=[row, row, pl.BlockSpec((tm, 2 * d), lambda i: (i, 0)), pl.BlockSpec((SUBLANE, d), lambda i: (0, 0)),
                   ANY, ANY, ANY],
        scratch_shapes=[pltpu.VMEM((d, d), BF16)] * 3 + [pltpu.VMEM((d, d), F32)] * 3 + [pltpu.SemaphoreType.DMA((3 * N_DEV,))],
        compiler_params=_params(),
    )(dx1b, oa, ob, ya, yb, p, p, gbias, gath)


DV_CONV_B_B, DV_BA, DV_BX, DV_SP, DV_CONV_A, DV_CONV_B = 0, 1, 2, 3, 4, 7
DV_ROWS = 16


def _mixer_bwd(dya, dyb, dpg, p, u_s, h_s, caw, cbw, vec, bda, bdx, name, comms=()):
    t, d = dya.shape
    tm = _time_tile(t)
    n_t = t // tm
    nk = tm // SUBLANE
    cw = min(MXU_TILE, d)
    nb = d // cw
    halo = 4 * SUBLANE
    ka, kb = CONV_A_K - 1, CONV_B_K - 1

    def body(dya_ref, dyb_ref, dpg_ref, ba_ref, ca_ref, xa_ref, xb_ref, gb_ref, cah_ref, xah_ref, xbh_ref,
             u_ref, h_ref, hh_ref, caw_ref, cbw_ref, vec_ref, bda_ref, bdx_ref,
             dp_ref, dv_ref, dwa_ref, dwx_ref,
             zbuf, xbuf, hbuf, dczbuf, dubuf, a2buf, a1buf, lbuf, dcz_head, du_head, a_head, lam_head):
        i = pl.program_id(0)

        @pl.when(i == 0)
        def _():
            for ref in (dv_ref, dwa_ref, dwx_ref, dcz_head, du_head, a_head, lam_head):
                _zero(ref)

        has_prev = jnp.where(i < n_t - 1, 1.0, 0.0).astype(F32)
        row = lax.broadcasted_iota(jnp.int32, (SUBLANE, cw), 0)
        dp_ref[:, 5 * d:7 * d] = dpg_ref[...]

        def colsum(v):
            return jnp.sum(v, axis=0, keepdims=True)

        for j in range(nb):
            cs = slice(j * cw, (j + 1) * cw)
            ca = ca_ref[:, cs].astype(F32)
            xa = xa_ref[:, cs].astype(F32)
            z = ca * xa
            z_before = cah_ref[:, cs].astype(F32) * xah_ref[:, cs].astype(F32) * has_prev
            _causal_fill(zbuf, z, z_before[halo - ka * SUBLANE:, :], ka, row)
            z2 = zbuf[0:tm, :]
            z1 = zbuf[SUBLANE:SUBLANE + tm, :]
            w0, w1, w2 = caw_ref[0:1, cs], caw_ref[1:2, cs], caw_ref[2:3, cs]
            cz = w0 * z2 + w1 * z1 + w2 * z
            dya = dya_ref[:, cs].astype(F32)
            dp_ref[:, 0 * d + j * cw:0 * d + (j + 1) * cw] = (dya * cz).astype(BF16)
            dcz = dya * ba_ref[:, cs].astype(F32)
            _anticausal_fill(dczbuf, dcz, dcz_head[:, cs], ka, row)
            dcz_head[:, cs] = dcz[0:ka * SUBLANE, :]
            dz = w2 * dcz + w1 * dczbuf[SUBLANE:SUBLANE + tm, :] + w0 * dczbuf[2 * SUBLANE:2 * SUBLANE + tm, :]
            dv_ref[DV_CONV_A + 0:DV_CONV_A + 1, cs] += colsum(dcz * z2)
            dv_ref[DV_CONV_A + 1:DV_CONV_A + 2, cs] += colsum(dcz * z1)
            dv_ref[DV_CONV_A + 2:DV_CONV_A + 3, cs] += colsum(dcz * z)
            dp_ref[:, 1 * d + j * cw:1 * d + (j + 1) * cw] = (dz * xa).astype(BF16)
            dp_ref[:, 2 * d + j * cw:2 * d + (j + 1) * cw] = (dz * ca).astype(BF16)
            h = h_ref[:, cs].astype(F32)
            h_before = hh_ref[:, cs].astype(F32) * has_prev
            _causal_fill(hbuf, h, h_before[halo - SUBLANE:, :], 1, row)
            h_prev = hbuf[0:tm, :]
            dyb = dyb_ref[:, cs].astype(F32)
            gel, dgel = _gelu_and_grad(gb_ref[:, cs].astype(F32))
            dp_ref[:, 4 * d + j * cw:4 * d + (j + 1) * cw] = (dyb * h * dgel).astype(BF16)
            ub = u_ref[:, cs]
            u = ub.astype(F32)
            sp = vec_ref[3:4, cs]
            r, gi, a, s = _lru_gates(ub, bda_ref[j], bdx_ref[j], vec_ref[1:2, cs], vec_ref[2:3, cs], sp)
            _anticausal_fill(a2buf, a, a_head[:, cs], 1, row)
            a_head[:, cs] = a[0:SUBLANE, :]
            a1buf[...] = a2buf[SUBLANE:SUBLANE + tm, :]
            lbuf[...] = dyb * gel
            l_end, a_prod = _chain_scan(a1buf, lbuf, nk, reverse=True)
            a_inc, l_inc = _sublane_scan(a_prod, l_end, row, reverse=True)
            carry = lam_head[:, cs]
            ends = l_inc + a_inc * carry
            starts = jnp.where(row == SUBLANE - 1, carry, pltpu.roll(ends, SUBLANE - 1, 0))
            lam_head[:, cs] = jnp.broadcast_to(ends[0:1, :], (SUBLANE, cw))
            lam = (lbuf[...].reshape(nk, SUBLANE, cw) + a1buf[...].reshape(nk, SUBLANE, cw) * starts[None]).reshape(tm, cw)
            da = lam * h_prev
            iu = gi * u
            ds = lam * iu
            di = lam * s * u
            du = lam * s * gi
            dlog_a = da * a - ds * (a * a) / s
            dv_ref[DV_SP:DV_SP + 1, cs] += colsum(dlog_a * r) * (-LRU_C)
            dpr = dlog_a * ((-LRU_C) * sp) * r * (1.0 - r)
            dpi = di * gi * (1.0 - gi)
            dv_ref[DV_BA:DV_BA + 1, cs] += colsum(dpr)
            dv_ref[DV_BX:DV_BX + 1, cs] += colsum(dpi)
            dprb = dpr.astype(BF16)
            dpib = dpi.astype(BF16)
            du = du + _dot_nt(dprb, bda_ref[j]) + _dot_nt(dpib, bdx_ref[j])
            dwa_ref[j] += _dot_tn(ub, dprb)
            dwx_ref[j] += _dot_tn(ub, dpib)
            xb = xb_ref[:, cs].astype(F32)
            x_before = xbh_ref[:, cs].astype(F32) * has_prev
            _causal_fill(xbuf, xb, x_before[halo - kb * SUBLANE:, :], kb, row)
            _anticausal_fill(dubuf, du, du_head[:, cs], kb, row)
            du_head[:, cs] = du[0:kb * SUBLANE, :]
            v0, v1, v2, v3 = cbw_ref[0:1, cs], cbw_ref[1:2, cs], cbw_ref[2:3, cs], cbw_ref[3:4, cs]
            dxb = (v3 * du + v2 * dubuf[SUBLANE:SUBLANE + tm, :] + v1 * dubuf[2 * SUBLANE:2 * SUBLANE + tm, :]
                   + v0 * dubuf[3 * SUBLANE:3 * SUBLANE + tm, :])
            dp_ref[:, 3 * d + j * cw:3 * d + (j + 1) * cw] = dxb.astype(BF16)
            dv_ref[DV_CONV_B_B:DV_CONV_B_B + 1, cs] += colsum(du)
            dv_ref[DV_CONV_B + 0:DV_CONV_B + 1, cs] += colsum(du * xbuf[0:tm, :])
            dv_ref[DV_CONV_B + 1:DV_CONV_B + 2, cs] += colsum(du * xbuf[SUBLANE:SUBLANE + tm, :])
            dv_ref[DV_CONV_B + 2:DV_CONV_B + 3, cs] += colsum(du * xbuf[2 * SUBLANE:2 * SUBLANE + tm, :])
            dv_ref[DV_CONV_B + 3:DV_CONV_B + 4, cs] += colsum(du * xb)

    rt = lambda i: n_t - 1 - i
    row_spec = pl.BlockSpec((tm, d), lambda i: (rt(i), 0))
    slab = lambda s: pl.BlockSpec((tm, d), lambda i, s=s: (rt(i), s))
    before = lambda s: pl.BlockSpec((halo, d), lambda i, s=s: (jnp.maximum(rt(i) * (tm // halo) - 1, 0), s))
    small = pl.BlockSpec((SUBLANE, d), lambda i: (0, 0))
    bd = pl.BlockSpec((nb, cw, cw), lambda i: (0, 0, 0))
    sd = jax.ShapeDtypeStruct
    wbuf = lambda n: pltpu.VMEM((tm + n * SUBLANE, cw), F32)
    head = lambda n: pltpu.VMEM((n * SUBLANE, d), F32)
    return _hosted_call(
        body, comms, name=name, grid=(n_t,),
        out_shape=[sd((t, N_PROJ * d), BF16), sd((DV_ROWS, d), F32), sd((nb, cw, cw), F32), sd((nb, cw, cw), F32)],
        in_specs=[row_spec, row_spec, pl.BlockSpec((tm, 2 * d), lambda i: (rt(i), 0)),
                  slab(0), slab(1), slab(2), slab(3), slab(4), before(1), before(2), before(3),
                  row_spec, row_spec, before(0), small, small, small, bd, bd],
        out_specs=[pl.BlockSpec((tm, N_PROJ * d), lambda i: (rt(i), 0)), pl.BlockSpec((DV_ROWS, d), lambda i: (0, 0)), bd, bd],
        scratch_shapes=[wbuf(ka), wbuf(kb), wbuf(1), wbuf(ka), wbuf(kb), wbuf(1),
                        pltpu.VMEM((tm, cw), F32), pltpu.VMEM((tm, cw), F32), head(ka), head(kb), head(1), head(1)],
        args=(dya, dyb, dpg, p, p, p, p, p, p, p, p, u_s, h_s, h_s, caw, cbw, vec, bda, bdx))


def _in_proj_bwd(dp, x, dx1, g_row, gath, grp, name, comms=()):
    t, d = x.shape
    tm = _row_tile(t, 256)
    n_in = N_PROJ * d

    def body(dp_ref, x_ref, dx1_ref, g_ref, gath_ref, dx_ref, dg_ref, w_in, sems):
        @pl.when(pl.program_id(0) == 0)
        def _():
            _load_weights(gath_ref, grp, ["in"], [w_in], sems)
            _zero(dg_ref)

        dh = _dot_nn(dp_ref[:, 0:d], w_in[0:d, :])
        for k in range(1, N_PROJ):
            dh = dh + _dot_nn(dp_ref[:, k * d:(k + 1) * d], w_in[k * d:(k + 1) * d, :])
        xf = x_ref[...]
        rstd = lax.rsqrt(jnp.mean(xf * xf, axis=-1, keepdims=True) + RMS_EPS)
        xh = xf * rstd
        g = g_ref[...]
        dg_ref[0:1, :] += jnp.sum(dh * xh, axis=0, keepdims=True)
        dxh = dh * g
        dx_ref[...] = dx1_ref[...] + rstd * (dxh - xh * jnp.mean(dxh * xh, axis=-1, keepdims=True))

    row = pl.BlockSpec((tm, d), lambda i: (i, 0))
    sd = jax.ShapeDtypeStruct
    return _hosted_call(
        body, comms, name=name, grid=(t // tm,),
        out_shape=[sd((t, d), F32), sd((SUBLANE, d), F32)],
        in_specs=[pl.BlockSpec((tm, n_in), lambda i: (i, 0)), row, row, pl.BlockSpec((1, d), lambda i: (0, 0)), ANY],
        out_specs=[row, pl.BlockSpec((SUBLANE, d), lambda i: (0, 0))],
        scratch_shapes=[pltpu.VMEM((n_in, d), BF16), pltpu.SemaphoreType.DMA((N_DEV,))],
        args=(dp, x, dx1, g_row, gath))


def _weight_grad(a, b, name):
    t, m = a.shape
    n = b.shape[1]
    bt = _row_tile(t, 512)
    bm = m
    for div in (1, 2, 4, 8):
        if m % div == 0 and (m // div) % LANE == 0 and (m // div) * n * 4 <= (12 << 20):
            bm = m // div
            break
    n_t = t // bt

    def body(a_ref, b_ref, o_ref, acc):
        k = pl.program_id(1)

        @pl.when(k == 0)
        def _():
            _zero(acc)

        acc[...] += _dot_tn(a_ref[...], b_ref[...])

        @pl.when(k == n_t - 1)
        def _():
            o_ref[...] = acc[...].astype(BF16)

    return pl.pallas_call(
        body, name=name, grid=(m // bm, n_t),
        out_shape=jax.ShapeDtypeStruct((m, n), BF16),
        in_specs=[pl.BlockSpec((bt, bm), lambda i, k: (k, i)), pl.BlockSpec((bt, n), lambda i, k: (k, 0))],
        out_specs=pl.BlockSpec((bm, n), lambda i, k: (i, 0)),
        scratch_shapes=[pltpu.VMEM((bm, n), F32)],
        compiler_params=_params(2),
    )(a, b)


def _adamw(w, g, m, v, name):
    r, c = w.shape
    tr = _fit_rows(r, c * 4)
    c1 = 1.0 - ADAM_B1 ** ADAM_STEP
    c2 = 1.0 - ADAM_B2 ** ADAM_STEP

    def body(w_ref, g_ref, m_ref, v_ref, d_ref, nm_ref, nv_ref):
        g32 = g_ref[...]
        nm = ADAM_B1 * m_ref[...] + (1.0 - ADAM_B1) * g32
        nv = ADAM_B2 * v_ref[...] + (1.0 - ADAM_B2) * (g32 * g32)
        nm_ref[...] = nm
        nv_ref[...] = nv
        d_ref[...] = -ADAM_LR * ((nm / c1) / (jnp.sqrt(nv / c2) + ADAM_EPS) + ADAM_WD * w_ref[...])

    spec = pl.BlockSpec((tr, c), lambda i: (i, 0))
    return pl.pallas_call(
        body, name=name, grid=(r // tr,),
        out_shape=[jax.ShapeDtypeStruct((r, c), F32)] * 3,
        in_specs=[spec] * 4, out_specs=[spec] * 3,
        compiler_params=_params(),
    )(w, g, m, v)


def _pad_rows(a, mult=SUBLANE):
    pad = (-a.shape[0]) % mult
    return a if pad == 0 else jnp.concatenate([a, jnp.zeros((pad,) + a.shape[1:], a.dtype)], axis=0)


REPLICATED = ("ln1_g", "conv_b_b", "lru_wa", "lru_ba", "lru_wx", "lru_bx", "lru_lambda", "ln2_g", "final_g")
SMALL_SHARDED = ("conv_a_w", "conv_b_w", "gate_bias")
MATRICES = ("w_in", "w_out_a", "w_out_b", "w_o", "w_ffn_gate", "w_ffn_up", "w_ffn_down")
ORDER = ("ln1_g", "w_in", "conv_a_w", "conv_b_w", "conv_b_b", "lru_wa", "lru_ba", "lru_wx", "lru_bx", "lru_lambda",
         "w_out_a", "w_out_b", "gate_bias", "w_o", "ln2_g", "w_ffn_gate", "w_ffn_up", "w_ffn_down", "final_g")


def kernel(x, ln1_g, w_in, conv_a_w, conv_b_w, conv_b_b, lru_wa, lru_ba, lru_wx, lru_bx, lru_lambda, w_out_a, w_out_b, gate_bias, w_o, ln2_g, w_ffn_gate, w_ffn_up, w_ffn_down, final_g, loss_target, m_ln1_g, m_w_in, m_conv_a_w, m_conv_b_w, m_conv_b_b, m_lru_wa, m_lru_ba, m_lru_wx, m_lru_bx, m_lru_lambda, m_w_out_a, m_w_out_b, m_gate_bias, m_w_o, m_ln2_g, m_w_ffn_gate, m_w_ffn_up, m_w_ffn_down, m_final_g, v_ln1_g, v_w_in, v_conv_a_w, v_conv_b_w, v_conv_b_b, v_lru_wa, v_lru_ba, v_lru_wx, v_lru_bx, v_lru_lambda, v_w_out_a, v_w_out_b, v_gate_bias, v_w_o, v_ln2_g, v_w_ffn_gate, v_w_ffn_up, v_w_ffn_down, v_final_g):
    w = dict(ln1_g=ln1_g, w_in=w_in, conv_a_w=conv_a_w, conv_b_w=conv_b_w, conv_b_b=conv_b_b, lru_wa=lru_wa,
             lru_ba=lru_ba, lru_wx=lru_wx, lru_bx=lru_bx, lru_lambda=lru_lambda, w_out_a=w_out_a, w_out_b=w_out_b,
             gate_bias=gate_bias, w_o=w_o, ln2_g=ln2_g, w_ffn_gate=w_ffn_gate, w_ffn_up=w_ffn_up,
             w_ffn_down=w_ffn_down, final_g=final_g)
    mom = dict(ln1_g=m_ln1_g, w_in=m_w_in, conv_a_w=m_conv_a_w, conv_b_w=m_conv_b_w, conv_b_b=m_conv_b_b,
               lru_wa=m_lru_wa, lru_ba=m_lru_ba, lru_wx=m_lru_wx, lru_bx=m_lru_bx, lru_lambda=m_lru_lambda,
               w_out_a=m_w_out_a, w_out_b=m_w_out_b, gate_bias=m_gate_bias, w_o=m_w_o, ln2_g=m_ln2_g,
               w_ffn_gate=m_w_ffn_gate, w_ffn_up=m_w_ffn_up, w_ffn_down=m_w_ffn_down, final_g=m_final_g)
    var = dict(ln1_g=v_ln1_g, w_in=v_w_in, conv_a_w=v_conv_a_w, conv_b_w=v_conv_b_w, conv_b_b=v_conv_b_b,
               lru_wa=v_lru_wa, lru_ba=v_lru_ba, lru_wx=v_lru_wx, lru_bx=v_lru_bx, lru_lambda=v_lru_lambda,
               w_out_a=v_w_out_a, w_out_b=v_w_out_b, gate_bias=v_gate_bias, w_o=v_w_o, ln2_g=v_ln2_g,
               w_ffn_gate=v_w_ffn_gate, w_ffn_up=v_w_ffn_up, w_ffn_down=v_w_ffn_down, final_g=v_final_g)

    _, t, d = x.shape
    n_layers = w_in.shape[0]
    ff = w_ffn_down.shape[1] * N_DEV
    dd = d // N_DEV
    hd = d // LRU_HEADS
    cw = min(MXU_TILE, d)
    nb = d // cw
    hpt = cw // hd
    grp = _groups(d, ff)
    me = 4 * lax.axis_index("x") + 2 * lax.axis_index("y") + lax.axis_index("c")
    tm_time = _time_tile(t)
    x0 = _to_tile_order(x[0], tm_time)
    target = _to_tile_order(loss_target[0], tm_time)

    packed = [{"in": jnp.swapaxes(w_in[l], 0, 1).astype(BF16),
               "rest": jnp.concatenate([w_out_a[l], w_out_b[l], w_o[l], jnp.swapaxes(w_ffn_gate[l], 0, 1),
                                        jnp.swapaxes(w_ffn_up[l], 0, 1), w_ffn_down[l]], axis=0).astype(BF16)}
              for l in range(n_layers)]
    n_small = CONV_A_K + CONV_B_K + 2
    small = _pad_rows(jnp.concatenate([conv_a_w, conv_b_w, gate_bias], axis=1).reshape(n_layers * n_small, dd))
    sp = jax.nn.softplus(-lru_lambda)
    vec = [_pad_rows(jnp.stack([conv_b_b[l], lru_ba[l], lru_bx[l], sp[l]])) for l in range(n_layers)]
    eye = jnp.eye(hpt, dtype=F32)

    def block_diag(wh):
        return jnp.einsum("jkab,kl->jkalb", wh.reshape(nb, hpt, hd, hd), eye).reshape(nb, cw, cw).astype(BF16)

    bda = [block_diag(lru_wa[l]) for l in range(n_layers)]
    bdx = [block_diag(lru_wx[l]) for l in range(n_layers)]

    gath = [dict() for _ in range(n_layers)]
    (gath[0]["in"],) = _comm_call(_Gather(packed[0]["in"]), "gather_in_0")
    saved = []
    xl = x0
    for l in range(n_layers):
        comms = [_Gather(packed[0]["rest"]), _Gather(small)] if l == 0 else []
        (p, h1b), got = _in_proj_fwd(xl, ln1_g[l][None], gath[l]["in"], grp["in"], f"in_proj_fwd_{l}", comms)
        if l == 0:
            gath[0]["rest"], small_g = got[0][0], got[1][0]
            small_full = jnp.swapaxes(small_g[:, :n_layers * n_small], 0, 1).reshape(n_layers, n_small, d)
            caw = [_pad_rows(small_full[k, 0:CONV_A_K]) for k in range(n_layers)]
            cbw = [_pad_rows(small_full[k, CONV_A_K:CONV_A_K + CONV_B_K]) for k in range(n_layers)]
            gbias = [_pad_rows(small_full[k, CONV_A_K + CONV_B_K:]) for k in range(n_layers)]
        more = l + 1 < n_layers
        (ya, yb, u_s, h_s), got = _mixer_fwd(p, caw[l], cbw[l], vec[l], bda[l], bdx[l], f"mixer_fwd_{l}",
                                             [_Gather(packed[l + 1]["in"])] if more else [])
        if more:
            ((gath[l + 1]["in"],),) = got
        x1, oa, ob = _merge_fwd(xl, ya, yb, p, gbias[l], gath[l]["rest"], grp["rest"], f"merge_fwd_{l}")
        (x2, gg, uu), got = _ffn_fwd(x1, ln2_g[l][None], gath[l]["rest"], grp["rest"], f"ffn_fwd_{l}",
                                     [_Gather(packed[l + 1]["rest"])] if more else [])
        if more:
            ((gath[l + 1]["rest"],),) = got
        saved.append(dict(x=xl, p=p, h1b=h1b, ya=ya, yb=yb, u=u_s, h=h_s, x1=x1, oa=oa, ob=ob, gg=gg, uu=uu))
        xl = x2
    loss_tile, dx, dfinal = _loss_head(xl, final_g[None], target, "loss_head")
    loss = lax.psum(loss_tile[0, 0], ("x", "y", "c"))

    def heads(dwb):
        blocks = jnp.diagonal(dwb.reshape(nb, hpt, hd, hpt, hd), axis1=1, axis2=3)
        return jnp.moveaxis(blocks, 3, 1).reshape(hd, d)

    layer_names = [n for n in REPLICATED if n != "final_g"] + list(SMALL_SHARDED)

    def layer_block(k):
        return jnp.concatenate([small_grads[k][n] for n in layer_names], axis=0)

    recv = [dict() for _ in range(n_layers)]
    small_grads = [None] * n_layers
    early_all = None
    for l in reversed(range(n_layers)):
        s = saved[l]
        dgg, duu, dw_d = _ffn_bwd_act(dx, s["gg"], s["uu"], gath[l]["rest"], grp["rest"], f"ffn_bwd_act_{l}")
        comms = []
        if l == 0:
            early = [layer_block(k) for k in range(1, n_layers)] + [_pad_rows(dfinal[0:1])]
            comms = [_Gather(jnp.concatenate(early, axis=0))]
        (dx1, dx1b, h2b, dln2), got = _ffn_bwd_in(dgg, duu, dx, s["x1"], ln2_g[l][None], gath[l]["rest"], grp["rest"],
                                                  f"ffn_bwd_in_{l}", comms)
        if l == 0:
            ((early_all,),) = got
        dya, dyb, dpg, dgbias, dw_oa, dw_ob, dw_o = _merge_bwd(dx1b, s["oa"], s["ob"], s["ya"], s["yb"], s["p"], gbias[l],
                                                               gath[l]["rest"], grp["rest"], f"merge_bwd_{l}")
        rest = {"oa": dw_oa, "ob": dw_ob, "o": dw_o, "g": _weight_grad(dgg, h2b, f"dw_ffn_gate_{l}"),
                "u": _weight_grad(duu, h2b, f"dw_ffn_up_{l}"), "d": dw_d}
        (dp, dv, dwa, dwx), got = _mixer_bwd(dya, dyb, dpg, s["p"], s["u"], s["h"], caw[l], cbw[l], vec[l], bda[l], bdx[l],
                                             f"mixer_bwd_{l}", [_Exchange(rest, grp["rest"])])
        (recv[l]["rest"],) = got[0]
        small_grads[l] = {
            "conv_b_b": dv[DV_CONV_B_B:DV_CONV_B_B + 1], "lru_wa": heads(dwa),
            "lru_ba": dv[DV_BA:DV_BA + 1], "lru_wx": heads(dwx), "lru_bx": dv[DV_BX:DV_BX + 1],
            "lru_lambda": dv[DV_SP:DV_SP + 1] * (-jax.nn.sigmoid(-lru_lambda[l]))[None], "ln2_g": dln2[0:1],
            "conv_a_w": dv[DV_CONV_A:DV_CONV_A + CONV_A_K], "conv_b_w": dv[DV_CONV_B:DV_CONV_B + CONV_B_K],
            "gate_bias": dgbias[0:2],
        }
        dw_in = _Exchange({"in": _weight_grad(dp, s["h1b"], f"dw_in_{l}")}, grp["in"])
        (dx, dln1), got = _in_proj_bwd(dp, s["x"], dx1, ln1_g[l][None], gath[l]["in"], grp["in"], f"in_proj_bwd_{l}", [dw_in])
        ((recv[l]["in"],),) = got
        small_grads[l]["ln1_g"] = dln1[0:1]
    grad_x = _from_tile_order(dx, tm_time)[None]

    g = {}
    gsum = [{k: _sum_slots(recv[l][k], f"sum_{k}_{l}") for k in ("in", "rest")} for l in range(n_layers)]

    def part(key):
        k = "in" if key == "in" else "rest"
        o, r = grp[k].off[key], grp[k].rows[key]
        return jnp.stack([gsum[l][k][o:o + r] for l in range(n_layers)])

    g = {"w_in": jnp.swapaxes(part("in"), 1, 2), "w_out_a": part("oa"), "w_out_b": part("ob"), "w_o": part("o"),
         "w_ffn_gate": jnp.swapaxes(part("g"), 1, 2), "w_ffn_up": jnp.swapaxes(part("u"), 1, 2), "w_ffn_down": part("d")}
    (late_all,) = _comm_call(_Gather(layer_block(0)), "gather_small_grads_0")
    early_sum = _sum_slots(early_all, "sum_small_grads")
    block_rows = late_all.shape[1]
    per_layer = [_sum_slots(late_all, "sum_small_grads_0")]
    per_layer += [early_sum[(k - 1) * block_rows:k * block_rows] for k in range(1, n_layers)]
    g["final_g"] = early_sum[(n_layers - 1) * block_rows].reshape(w["final_g"].shape)
    o = 0
    for n in layer_names:
        rows = small_grads[0][n].shape[0]
        stacked = jnp.concatenate([per_layer[k][o:o + rows] for k in range(n_layers)], axis=0)
        if n in SMALL_SHARDED:
            g[n] = lax.dynamic_slice_in_dim(stacked, me * dd, dd, axis=1).reshape(n_layers, rows, dd)
        else:
            g[n] = stacked.reshape(w[n].shape)
        o += rows

    delta, new_m, new_v = {}, {}, {}
    for n in MATRICES:
        shape = w[n].shape
        flat = lambda a: a.reshape(shape[0] * shape[1], shape[2])
        dl, nm, nv = _adamw(flat(w[n]), flat(g[n]), flat(mom[n]), flat(var[n]), f"adamw_{n}")
        delta[n], new_m[n], new_v[n] = dl.reshape(shape), nm.reshape(shape), nv.reshape(shape)
    for group, width, name in ((REPLICATED, d, "adamw_replicated"), (SMALL_SHARDED, dd, "adamw_vectors")):
        cat = lambda src: _pad_rows(jnp.concatenate([src[n].reshape(-1, width) for n in group], axis=0))
        dl, nm, nv = _adamw(cat(w), cat(g), cat(mom), cat(var), name)
        o = 0
        for n in group:
            rows = w[n].size // width
            delta[n], new_m[n], new_v[n] = (a[o:o + rows].reshape(w[n].shape) for a in (dl, nm, nv))
            o += rows

    return (loss, grad_x, *[g[n] for n in ORDER], *[delta[n] for n in ORDER], *[new_m[n] for n in ORDER],
            *[new_v[n] for n in ORDER])
```

```python
import math

import jax
import jax.numpy as jnp
from jax import lax
from jax.experimental import pallas as pl
from jax.experimental.pallas import tpu as pltpu

F32 = jnp.float32
BF16 = jnp.bfloat16

N_DEV = 8
N_PROJ = 7
LRU_HEADS = 16
LRU_C = 8.0
RMS_EPS = 1e-6
CONV_A_K = 3
CONV_B_K = 4
GELU_C = math.sqrt(2.0 / math.pi)
GELU_A = 0.044715

ADAM_LR = 0.001
ADAM_B1 = 0.9
ADAM_B2 = 0.999
ADAM_EPS = 1e-08
ADAM_WD = 0.01
ADAM_STEP = 10

LANE = 128
SUBLANE = 8
MXU_TILE = 256
VMEM_LIMIT = 52 << 20
MESH = pl.DeviceIdType.MESH
ANY = pl.BlockSpec(memory_space=pl.ANY)


def _dot_nn(a, b):
    return lax.dot_general(a, b, (((1,), (0,)), ((), ())), preferred_element_type=F32)


def _dot_nt(a, b):
    return lax.dot_general(a, b, (((1,), (1,)), ((), ())), preferred_element_type=F32)


def _dot_tn(a, b):
    return lax.dot_general(a, b, (((0,), (0,)), ((), ())), preferred_element_type=F32)


def _sigmoid(x):
    return 1.0 / (1.0 + jnp.exp(-x))


def _gelu_and_grad(x):
    x2 = x * x
    t = jnp.tanh(GELU_C * x * (1.0 + GELU_A * x2))
    g = 0.5 * x * (1.0 + t)
    dg = 0.5 * (1.0 + t) + 0.5 * x * (1.0 - t * t) * GELU_C * (1.0 + 3.0 * GELU_A * x2)
    return g, dg


def _zero(ref):
    ref[...] = jnp.zeros(ref.shape, ref.dtype)


def _fit_rows(r, row_bytes, budget=1 << 20):
    fits = [t for t in range(16, r + 1, 16) if r % t == 0 and t * row_bytes <= budget]
    return max(fits) if fits else r


def _row_tile(t, want):
    tm = min(want, t // 2)
    assert t % tm == 0 and tm % SUBLANE == 0, (t, tm)
    return tm


def _params(n_grid=1, **kw):
    return pltpu.CompilerParams(dimension_semantics=("arbitrary",) * n_grid, vmem_limit_bytes=VMEM_LIMIT, **kw)


class _Group:
    def __init__(self, keys, rows):
        self.keys = keys
        self.rows = dict(zip(keys, rows))
        self.off, o = {}, 0
        for k in keys:
            self.off[k] = o
            o += self.rows[k]
        self.total = o


def _groups(d, ff):
    dd, ffs = d // N_DEV, ff // N_DEV
    return {"in": _Group(("in",), (N_PROJ * dd,)),
            "rest": _Group(("oa", "ob", "o", "g", "u", "d"), (dd, dd, dd, ffs, ffs, ffs))}


def _load_weights(g_ref, grp, keys, dsts, sems):
    copies = []
    for n, (k, dst) in enumerate(zip(keys, dsts)):
        rows, off = grp.rows[k], grp.off[k]
        copies += [pltpu.make_async_copy(g_ref.at[p, pl.ds(off, rows), :], dst.at[pl.ds(p * rows, rows), :],
                                         sems.at[n * N_DEV + p]) for p in range(N_DEV)]
    for c in copies:
        c.start()
    for c in copies:
        c.wait()


def _comm_sems():
    return [pltpu.SemaphoreType.DMA((N_DEV - 1,)), pltpu.SemaphoreType.DMA((N_DEV - 1,)), pltpu.SemaphoreType.DMA]


class _Gather:
    def __init__(self, x):
        self.inputs = [x]
        self.out_shape = [jax.ShapeDtypeStruct((N_DEV,) + x.shape, x.dtype)]
        self.scratch = _comm_sems()

    def _plan(self, ins, outs, scr):
        (x_ref,), (out_ref,), (send_sems, recv_sems, local_sem) = ins, outs, scr
        mx, my, mc = lax.axis_index("x"), lax.axis_index("y"), lax.axis_index("c")
        me, sibling = (mx, my, mc), (mx, my, 1 - mc)
        chips = [(1 - mx, my), (mx, 1 - my), (1 - mx, 1 - my)]

        def slot(px, py, pc):
            return out_ref.at[4 * px + 2 * py + pc]

        def copy(k, block, to, src=None):
            return pltpu.make_async_remote_copy(
                src_ref=slot(*block) if src is None else src, dst_ref=slot(*block),
                send_sem=send_sems.at[k], recv_sem=recv_sems.at[k], device_id=to, device_id_type=MESH)

        mine = lambda: pltpu.make_async_copy(x_ref, slot(*me), local_sem)
        first = [lambda: copy(0, me, sibling, src=x_ref)]
        first += [lambda j=j, chip=chip: copy(1 + j, me, (*chip, mc), src=x_ref) for j, chip in enumerate(chips)]
        landed = [lambda j=j, chip=chip: copy(1 + j, (*chip, mc), me) for j, chip in enumerate(chips)]
        passed = [lambda j=j, chip=chip: copy(4 + j, (*chip, mc), sibling) for j, chip in enumerate(chips)]
        from_sibling = [lambda: copy(0, sibling, me)]
        from_sibling += [lambda j=j, chip=chip: copy(4 + j, (*chip, 1 - mc), me) for j, chip in enumerate(chips)]
        return mine, first, landed, passed, from_sibling

    def start(self, ins, outs, scr):
        mine, first, _, _, _ = self._plan(ins, outs, scr)
        mine().start()
        for cp in first:
            cp().start()

    def mid(self, ins, outs, scr):
        _, _, landed, passed, _ = self._plan(ins, outs, scr)
        for got, fwd in zip(landed, passed):
            got().wait_recv()
            fwd().start()

    def finish(self, ins, outs, scr):
        mine, first, _, passed, from_sibling = self._plan(ins, outs, scr)
        for cp in from_sibling:
            cp().wait_recv()
        for cp in first + passed:
            cp().wait_send()
        mine().wait()


class _Exchange:
    def __init__(self, mats, grp):
        self.grp = grp
        self.inputs = [mats[k] for k in grp.keys]
        self.out_shape = [jax.ShapeDtypeStruct((N_DEV, grp.total, self.inputs[0].shape[1]), BF16)]
        self.scratch = _comm_sems()

    def _pieces(self, g_refs, out_ref, q, dst_slot):
        out = []
        for g_ref, k in zip(g_refs, self.grp.keys):
            rows = self.grp.rows[k]
            out.append((g_ref.at[pl.ds(pl.multiple_of(q * rows, 16), rows), :],
                        out_ref.at[dst_slot, pl.ds(self.grp.off[k], rows), :]))
        return out

    def start(self, ins, outs, scr):
        (out_ref,), (send_sems, recv_sems, local_sem) = outs, scr
        mx, my, mc = lax.axis_index("x"), lax.axis_index("y"), lax.axis_index("c")
        me = 4 * mx + 2 * my + mc
        for s, t in self._pieces(ins, out_ref, me, me):
            pltpu.make_async_copy(s, t, local_sem).start()
        for k in range(1, N_DEV):
            px, py, pc = mx ^ ((k >> 2) & 1), my ^ ((k >> 1) & 1), mc ^ (k & 1)
            for s, t in self._pieces(ins, out_ref, 4 * px + 2 * py + pc, me):
                pltpu.make_async_remote_copy(src_ref=s, dst_ref=t, send_sem=send_sems.at[k - 1],
                                             recv_sem=recv_sems.at[k - 1], device_id=(px, py, pc),
                                             device_id_type=MESH).start()

    def mid(self, ins, outs, scr):
        pass

    def finish(self, ins, outs, scr):
        (out_ref,), (send_sems, recv_sems, local_sem) = outs, scr
        mx, my, mc = lax.axis_index("x"), lax.axis_index("y"), lax.axis_index("c")
        whole = out_ref.at[0]
        for k in range(1, N_DEV):
            done = pltpu.make_async_remote_copy(src_ref=whole, dst_ref=whole, send_sem=send_sems.at[k - 1],
                                                recv_sem=recv_sems.at[k - 1], device_id=(mx, my, mc),
                                                device_id_type=MESH)
            done.wait_send()
            done.wait_recv()
        pltpu.make_async_copy(whole, whole, local_sem).wait()


def _split(refs, sizes):
    out, pos = [], 0
    for n in sizes:
        out.append(refs[pos:pos + n])
        pos += n
    return out


def _hosted_call(body, comms, *, name, grid, in_specs, out_specs, out_shape, scratch_shapes, args):
    n_steps = grid[0]
    nc = len(comms)
    sizes = ([len(in_specs)] + [len(c.inputs) for c in comms] + [len(out_specs)] + [len(c.out_shape) for c in comms]
             + [len(scratch_shapes)] + [len(c.scratch) for c in comms])

    def hosted(*refs):
        parts = _split(refs, sizes)
        ins, c_ins = parts[0], parts[1:1 + nc]
        outs, c_outs = parts[1 + nc], parts[2 + nc:2 + 2 * nc]
        scr, c_scr = parts[2 + 2 * nc], parts[3 + 2 * nc:]
        step = pl.program_id(0)
        if comms:
            @pl.when(step == 0)
            def _():
                for c, a, b, s in zip(comms, c_ins, c_outs, c_scr):
                    c.start(a, b, s)

            @pl.when(step == max(n_steps - 2, 0))
            def _():
                for c, a, b, s in zip(comms, c_ins, c_outs, c_scr):
                    c.mid(a, b, s)

        body(*ins, *outs, *scr)
        if comms:
            @pl.when(step == n_steps - 1)
            def _():
                for c, a, b, s in zip(comms, c_ins, c_outs, c_scr):
                    c.finish(a, b, s)

    res = pl.pallas_call(
        hosted, name=name, grid=grid,
        out_shape=[*out_shape, *[o for c in comms for o in c.out_shape]],
        in_specs=[*in_specs, *[ANY for c in comms for _ in c.inputs]],
        out_specs=[*out_specs, *[ANY for c in comms for _ in c.out_shape]],
        scratch_shapes=[*scratch_shapes, *[s for c in comms for s in c.scratch]],
        compiler_params=_params(),
    )(*args, *[a for c in comms for a in c.inputs])
    main, rest = res[:len(out_specs)], res[len(out_specs):]
    return main, _split(rest, [len(c.out_shape) for c in comms])


def _comm_call(comm, name):
    def body(*refs):
        ins, outs, scr = _split(refs, [len(comm.inputs), len(comm.out_shape), len(comm.scratch)])
        comm.start(ins, outs, scr)
        comm.mid(ins, outs, scr)
        comm.finish(ins, outs, scr)

    return pl.pallas_call(
        body, name=name, out_shape=comm.out_shape, in_specs=[ANY] * len(comm.inputs),
        out_specs=[ANY] * len(comm.out_shape), scratch_shapes=comm.scratch,
    )(*comm.inputs)


def _sum_slots(x, name):
    n, r, c = x.shape
    tr = _fit_rows(r, c * 4)

    def body(x_ref, o_ref):
        acc = x_ref[0].astype(F32)
        for p in range(1, n):
            acc = acc + x_ref[p].astype(F32)
        o_ref[...] = acc

    return pl.pallas_call(
        body, name=name, grid=(r // tr,),
        out_shape=jax.ShapeDtypeStruct((r, c), F32),
        in_specs=[pl.BlockSpec((n, tr, c), lambda i: (0, i, 0))],
        out_specs=pl.BlockSpec((tr, c), lambda i: (i, 0)),
        compiler_params=_params(),
    )(x)


def _in_proj_fwd(x, g_row, gath, grp, name, comms=()):
    t, d = x.shape
    tm = _row_tile(t, 512)
    n_in = N_PROJ * d

    def body(x_ref, g_ref, gath_ref, p_ref, h_ref, w_in, sems):
        @pl.when(pl.program_id(0) == 0)
        def _():
            _load_weights(gath_ref, grp, ["in"], [w_in], sems)

        xf = x_ref[...]
        rstd = lax.rsqrt(jnp.mean(xf * xf, axis=-1, keepdims=True) + RMS_EPS)
        h = (xf * rstd * g_ref[...]).astype(BF16)
        h_ref[...] = h
        for k in range(N_PROJ):
            p_ref[:, k * d:(k + 1) * d] = _dot_nt(h, w_in[k * d:(k + 1) * d, :]).astype(BF16)

    row = pl.BlockSpec((tm, d), lambda i: (i, 0))
    return _hosted_call(
        body, comms, name=name, grid=(t // tm,),
        out_shape=[jax.ShapeDtypeStruct((t, n_in), BF16), jax.ShapeDtypeStruct((t, d), BF16)],
        in_specs=[row, pl.BlockSpec((1, d), lambda i: (0, 0)), ANY],
        out_specs=[pl.BlockSpec((tm, n_in), lambda i: (i, 0)), row],
        scratch_shapes=[pltpu.VMEM((n_in, d), BF16), pltpu.SemaphoreType.DMA((N_DEV,))],
        args=(x, g_row, gath))


def _time_tile(t):
    return _row_tile(t, 256)


def _to_tile_order(a, tm):
    t, c = a.shape
    return jnp.swapaxes(a.reshape(t // tm, SUBLANE, tm // SUBLANE, c), 1, 2).reshape(t, c)


def _from_tile_order(a, tm):
    t, c = a.shape
    return jnp.swapaxes(a.reshape(t // tm, tm // SUBLANE, SUBLANE, c), 1, 2).reshape(t, c)


def _causal_fill(buf, v, prev_tail, n, row):
    tm = v.shape[0]
    for q in range(n):
        cur = v[tm - SUBLANE * (n - q):tm - SUBLANE * (n - q - 1), :]
        prv = prev_tail[SUBLANE * q:SUBLANE * (q + 1), :]
        buf[SUBLANE * q:SUBLANE * (q + 1), :] = jnp.where(row == 0, pltpu.roll(prv, 1, 0), pltpu.roll(cur, 1, 0))
    buf[SUBLANE * n:, :] = v


def _anticausal_fill(buf, v, next_head, n, row):
    tm = v.shape[0]
    buf[0:tm, :] = v
    for q in range(n):
        cur = v[SUBLANE * q:SUBLANE * (q + 1), :]
        nxt = next_head[SUBLANE * q:SUBLANE * (q + 1), :]
        buf[tm + SUBLANE * q:tm + SUBLANE * (q + 1), :] = jnp.where(
            row == SUBLANE - 1, pltpu.roll(nxt, SUBLANE - 1, 0), pltpu.roll(cur, SUBLANE - 1, 0))


def _chain_scan(abuf, bbuf, nk, reverse):
    cw = abuf.shape[1]

    def step(n, carry):
        h, c = carry
        r0 = pl.multiple_of((nk - 1 - n if reverse else n) * SUBLANE, SUBLANE)
        ak = abuf[pl.ds(r0, SUBLANE), :]
        h = ak * h + bbuf[pl.ds(r0, SUBLANE), :]
        c = ak * c
        bbuf[pl.ds(r0, SUBLANE), :] = h
        abuf[pl.ds(r0, SUBLANE), :] = c
        return h, c

    return lax.fori_loop(0, nk, step, (jnp.zeros((SUBLANE, cw), F32), jnp.ones((SUBLANE, cw), F32)), unroll=8)


def _sublane_scan(a, b, row, reverse):
    for sh in (1, 2, 4):
        if reverse:
            m = row < SUBLANE - sh
            b = jnp.where(m, a * pltpu.roll(b, SUBLANE - sh, 0) + b, b)
            a = jnp.where(m, a * pltpu.roll(a, SUBLANE - sh, 0), a)
        else:
            m = row >= sh
            b = jnp.where(m, a * pltpu.roll(b, sh, 0) + b, b)
            a = jnp.where(m, a * pltpu.roll(a, sh, 0), a)
    return a, b


def _lru_gates(ub, bda, bdx, ba, bx, sp):
    r = _sigmoid(_dot_nn(ub, bda) + ba)
    i = _sigmoid(_dot_nn(ub, bdx) + bx)
    log_a = (-LRU_C) * r * sp
    a = jnp.exp(log_a)
    s2 = -jnp.tanh(log_a) * (1.0 + a * a)
    inv_s = lax.rsqrt(s2)
    s = jnp.where(s2 > 0.0, s2 * inv_s, 0.0)
    return r, i, a, s, inv_s


def _mixer_fwd(p, caw, cbw, vec, bda, bdx, name, comms=()):
    t = p.shape[0]
    d = p.shape[1] // N_PROJ
    tm = _time_tile(t)
    nk = tm // SUBLANE
    cw = min(MXU_TILE, d)
    nb = d // cw

    def body(ba_ref, ca_ref, xa_ref, xb_ref, gb_ref, caw_ref, cbw_ref, vec_ref, bda_ref, bdx_ref,
             ya_ref, yb_ref, u_ref, h_ref, zbuf, xbuf, abuf, bbuf, z_tail, x_tail, h_carry):
        @pl.when(pl.program_id(0) == 0)
        def _():
            _zero(z_tail)
            _zero(x_tail)
            _zero(h_carry)

        row = lax.broadcasted_iota(jnp.int32, (SUBLANE, cw), 0)
        for j in range(nb):
            cs = slice(j * cw, (j + 1) * cw)
            z = ca_ref[:, cs].astype(F32) * xa_ref[:, cs].astype(F32)
            _causal_fill(zbuf, z, z_tail[:, cs], CONV_A_K - 1, row)
            z_tail[:, cs] = z[tm - (CONV_A_K - 1) * SUBLANE:, :]
            cz = caw_ref[0:1, cs] * zbuf[0:tm, :] + caw_ref[1:2, cs] * zbuf[SUBLANE:SUBLANE + tm, :] + caw_ref[2:3, cs] * z
            ya_ref[:, cs] = (ba_ref[:, cs].astype(F32) * cz).astype(BF16)
            xb = xb_ref[:, cs].astype(F32)
            _causal_fill(xbuf, xb, x_tail[:, cs], CONV_B_K - 1, row)
            x_tail[:, cs] = xb[tm - (CONV_B_K - 1) * SUBLANE:, :]
            u = (cbw_ref[0:1, cs] * xbuf[0:tm, :] + cbw_ref[1:2, cs] * xbuf[SUBLANE:SUBLANE + tm, :]
                 + cbw_ref[2:3, cs] * xbuf[2 * SUBLANE:2 * SUBLANE + tm, :] + cbw_ref[3:4, cs] * xb + vec_ref[0:1, cs])
            ub = u.astype(BF16)
            u = ub.astype(F32)
            _, gi, a, s, _ = _lru_gates(ub, bda_ref[j], bdx_ref[j], vec_ref[1:2, cs], vec_ref[2:3, cs], vec_ref[3:4, cs])
            abuf[...] = a
            bbuf[...] = s * (gi * u)
            h_end, a_prod = _chain_scan(abuf, bbuf, nk, reverse=False)
            a_inc, h_inc = _sublane_scan(a_prod, h_end, row, reverse=False)
            carry = h_carry[:, cs]
            ends = h_inc + a_inc * carry
            starts = jnp.where(row == 0, carry, pltpu.roll(ends, 1, 0))
            h_carry[:, cs] = jnp.broadcast_to(ends[SUBLANE - 1:SUBLANE, :], (SUBLANE, cw))
            h = (bbuf[...].reshape(nk, SUBLANE, cw) + abuf[...].reshape(nk, SUBLANE, cw) * starts[None]).reshape(tm, cw)
            gel, _ = _gelu_and_grad(gb_ref[:, cs].astype(F32))
            yb_ref[:, cs] = (h * gel).astype(BF16)
            u_ref[:, cs] = ub
            h_ref[:, cs] = h.astype(BF16)

    slab = lambda s: pl.BlockSpec((tm, d), lambda i, s=s: (i, s))
    small = pl.BlockSpec((SUBLANE, d), lambda i: (0, 0))
    bd = pl.BlockSpec((nb, cw, cw), lambda i: (0, 0, 0))
    out = pl.BlockSpec((tm, d), lambda i: (i, 0))
    return _hosted_call(
        body, comms, name=name, grid=(t // tm,),
        out_shape=[jax.ShapeDtypeStruct((t, d), BF16)] * 4,
        in_specs=[slab(0), slab(1), slab(2), slab(3), slab(4), small, small, small, bd, bd],
        out_specs=[out] * 4,
        scratch_shapes=[pltpu.VMEM((tm + (CONV_A_K - 1) * SUBLANE, cw), F32), pltpu.VMEM((tm + (CONV_B_K - 1) * SUBLANE, cw), F32),
                        pltpu.VMEM((tm, cw), F32), pltpu.VMEM((tm, cw), F32),
                        pltpu.VMEM(((CONV_A_K - 1) * SUBLANE, d), F32), pltpu.VMEM(((CONV_B_K - 1) * SUBLANE, d), F32),
                        pltpu.VMEM((SUBLANE, d), F32)],
        args=(p, p, p, p, p, caw, cbw, vec, bda, bdx))


def _merge_fwd(x, ya, yb, p, gbias, gath, grp, name):
    t, d = x.shape
    tm = _row_tile(t, 512)

    def body(x_ref, ya_ref, yb_ref, ga_ref, gb_ref, gbias_ref, gath_ref, x1_ref, oa_ref, ob_ref, w_oa, w_ob, w_o, sems):
        @pl.when(pl.program_id(0) == 0)
        def _():
            _load_weights(gath_ref, grp, ["oa", "ob", "o"], [w_oa, w_ob, w_o], sems)

        oa = _dot_nn(ya_ref[...], w_oa[...]).astype(BF16)
        ob = _dot_nn(yb_ref[...], w_ob[...]).astype(BF16)
        oa_ref[...] = oa
        ob_ref[...] = ob
        sa = _sigmoid(ga_ref[...] + gbias_ref[0:1, :].astype(BF16))
        sb = _sigmoid(gb_ref[...] + gbias_ref[1:2, :].astype(BF16))
        x1_ref[...] = x_ref[...] + _dot_nn(sa * oa + sb * ob, w_o[...])

    row = pl.BlockSpec((tm, d), lambda i: (i, 0))
    return pl.pallas_call(
        body, name=name, grid=(t // tm,),
        out_shape=[jax.ShapeDtypeStruct((t, d), F32), jax.ShapeDtypeStruct((t, d), BF16), jax.ShapeDtypeStruct((t, d), BF16)],
        in_specs=[row, row, row, pl.BlockSpec((tm, d), lambda i: (i, 5)), pl.BlockSpec((tm, d), lambda i: (i, 6)),
                  pl.BlockSpec((SUBLANE, d), lambda i: (0, 0)), ANY],
        out_specs=[row, row, row],
        scratch_shapes=[pltpu.VMEM((d, d), BF16)] * 3 + [pltpu.SemaphoreType.DMA((3 * N_DEV,))],
        compiler_params=_params(),
    )(x, ya, yb, p, p, gbias, gath)


def _ffn_fwd(x1, g_row, gath, grp, name, comms=()):
    t, d = x1.shape
    ff = grp.rows["g"] * N_DEV
    tm = _row_tile(t, 512)
    fc = MXU_TILE
    assert ff % fc == 0

    def body(x_ref, g_ref, gath_ref, x2_ref, gg_ref, uu_ref, w_g, w_u, w_d, acc, sems):
        @pl.when(pl.program_id(0) == 0)
        def _():
            _load_weights(gath_ref, grp, ["g", "u", "d"], [w_g, w_u, w_d], sems)

        xf = x_ref[...]
        rstd = lax.rsqrt(jnp.mean(xf * xf, axis=-1, keepdims=True) + RMS_EPS)
        h = (xf * rstd * g_ref[...]).astype(BF16)
        acc[...] = xf
        for c in range(ff // fc):
            fs = slice(c * fc, (c + 1) * fc)
            gg = _dot_nt(h, w_g[fs, :]).astype(BF16)
            uu = _dot_nt(h, w_u[fs, :]).astype(BF16)
            gg_ref[:, fs] = gg
            uu_ref[:, fs] = uu
            acc[...] += _dot_nn(gg * _sigmoid(gg) * uu, w_d[fs, :])
        x2_ref[...] = acc[...]

    row = pl.BlockSpec((tm, d), lambda i: (i, 0))
    wide = pl.BlockSpec((tm, ff), lambda i: (i, 0))
    return _hosted_call(
        body, comms, name=name, grid=(t // tm,),
        out_shape=[jax.ShapeDtypeStruct((t, d), F32), jax.ShapeDtypeStruct((t, ff), BF16), jax.ShapeDtypeStruct((t, ff), BF16)],
        in_specs=[row, pl.BlockSpec((1, d), lambda i: (0, 0)), ANY],
        out_specs=[row, wide, wide],
        scratch_shapes=[pltpu.VMEM((ff, d), BF16)] * 3 + [pltpu.VMEM((tm, d), F32), pltpu.SemaphoreType.DMA((3 * N_DEV,))],
        args=(x1, g_row, gath))


def _loss_head(x, g_row, target, name):
    t, d = x.shape
    tm = _row_tile(t, 512)

    def body(x_ref, g_ref, tgt_ref, loss_ref, dx_ref, dg_ref):
        @pl.when(pl.program_id(0) == 0)
        def _():
            _zero(loss_ref)
            _zero(dg_ref)

        xf = x_ref[...]
        rstd = lax.rsqrt(jnp.mean(xf * xf, axis=-1, keepdims=True) + RMS_EPS)
        xh = xf * rstd
        g = g_ref[...]
        err = xh * g - tgt_ref[...]
        loss_ref[...] += 0.5 * jnp.sum(jnp.sum(err * err, axis=-1, keepdims=True), axis=0, keepdims=True) * (1.0 / d)
        dy = err * (1.0 / d)
        dg_ref[0:1, :] += jnp.sum(dy * xh, axis=0, keepdims=True)
        dxh = dy * g
        dx_ref[...] = rstd * (dxh - xh * jnp.mean(dxh * xh, axis=-1, keepdims=True))

    row = pl.BlockSpec((tm, d), lambda i: (i, 0))
    return pl.pallas_call(
        body, name=name, grid=(t // tm,),
        out_shape=[jax.ShapeDtypeStruct((SUBLANE, LANE), F32), jax.ShapeDtypeStruct((t, d), F32),
                   jax.ShapeDtypeStruct((SUBLANE, d), F32)],
        in_specs=[row, pl.BlockSpec((1, d), lambda i: (0, 0)), row],
        out_specs=[pl.BlockSpec((SUBLANE, LANE), lambda i: (0, 0)), row, pl.BlockSpec((SUBLANE, d), lambda i: (0, 0))],
        compiler_params=_params(),
    )(x, g_row, target)


def _ffn_bwd_act(dx2, gg, uu, gath, grp, name):
    t, d = dx2.shape
    ff = grp.rows["g"] * N_DEV
    tm = _row_tile(t, 512)
    fc = MXU_TILE
    n_t = t // tm

    def body(dx2_ref, gg_ref, uu_ref, gath_ref, dgg_ref, duu_ref, dwd_ref, w_d, acc, sems):
        @pl.when(pl.program_id(0) == 0)
        def _():
            _load_weights(gath_ref, grp, ["d"], [w_d], sems)
            _zero(acc)

        dx2b = dx2_ref[...].astype(BF16)
        for c in range(ff // fc):
            fs = slice(c * fc, (c + 1) * fc)
            df = _dot_nt(dx2b, w_d[fs, :]).astype(BF16)
            g = gg_ref[:, fs]
            u = uu_ref[:, fs]
            sg = _sigmoid(g)
            silu = g * sg
            acc[fs, :] += _dot_tn(silu * u, dx2b)
            duu_ref[:, fs] = df * silu
            dgg_ref[:, fs] = df * u * (sg * (1.0 + g * (1.0 - sg)))

        @pl.when(pl.program_id(0) == n_t - 1)
        def _():
            w_d[...] = acc[...].astype(BF16)
            out = pltpu.make_async_copy(w_d, dwd_ref, sems.at[0])
            out.start()
            out.wait()

    row = pl.BlockSpec((tm, d), lambda i: (i, 0))
    wide = pl.BlockSpec((tm, ff), lambda i: (i, 0))
    sd = jax.ShapeDtypeStruct
    return pl.pallas_call(
        body, name=name, grid=(n_t,),
        out_shape=[sd((t, ff), BF16), sd((t, ff), BF16), sd((ff, d), BF16)],
        in_specs=[row, wide, wide, ANY],
        out_specs=[wide, wide, ANY],
        scratch_shapes=[pltpu.VMEM((ff, d), BF16), pltpu.VMEM((ff, d), F32), pltpu.SemaphoreType.DMA((N_DEV,))],
        compiler_params=_params(),
    )(dx2, gg, uu, gath)


def _ffn_bwd_in(dgg, duu, dx2, x1, g_row, gath, grp, name, comms=()):
    t, d = x1.shape
    ff = grp.rows["g"] * N_DEV
    tm = _row_tile(t, 512)

    def body(dgg_ref, duu_ref, dx2_ref, x_ref, g_ref, gath_ref, dx1_ref, dx1b_ref, h_ref, dg_ref, w_g, w_u, sems):
        @pl.when(pl.program_id(0) == 0)
        def _():
            _load_weights(gath_ref, grp, ["g", "u"], [w_g, w_u], sems)
            _zero(dg_ref)

        dh = _dot_nn(dgg_ref[...], w_g[...]) + _dot_nn(duu_ref[...], w_u[...])
        xf = x_ref[...]
        rstd = lax.rsqrt(jnp.mean(xf * xf, axis=-1, keepdims=True) + RMS_EPS)
        xh = xf * rstd
        g = g_ref[...]
        h_ref[...] = (xh * g).astype(BF16)
        dg_ref[0:1, :] += jnp.sum(dh * xh, axis=0, keepdims=True)
        dxh = dh * g
        dx1 = dx2_ref[...] + rstd * (dxh - xh * jnp.mean(dxh * xh, axis=-1, keepdims=True))
        dx1_ref[...] = dx1
        dx1b_ref[...] = dx1.astype(BF16)

    row = pl.BlockSpec((tm, d), lambda i: (i, 0))
    wide = pl.BlockSpec((tm, ff), lambda i: (i, 0))
    sd = jax.ShapeDtypeStruct
    return _hosted_call(
        body, comms, name=name, grid=(t // tm,),
        out_shape=[sd((t, d), F32), sd((t, d), BF16), sd((t, d), BF16), sd((SUBLANE, d), F32)],
        in_specs=[wide, wide, row, row, pl.BlockSpec((1, d), lambda i: (0, 0)), ANY],
        out_specs=[row, row, row, pl.BlockSpec((SUBLANE, d), lambda i: (0, 0))],
        scratch_shapes=[pltpu.VMEM((ff, d), BF16)] * 2 + [pltpu.SemaphoreType.DMA((2 * N_DEV,))],
        args=(dgg, duu, dx2, x1, g_row, gath))


def _merge_bwd(dx1b, oa, ob, ya, yb, p, gbias, gath, grp, name):
    t, d = oa.shape
    tm = _row_tile(t, 512)
    n_t = t // tm

    def body(dx_ref, oa_ref, ob_ref, ya_ref, yb_ref, ga_ref, gb_ref, gbias_ref, gath_ref,
             dya_ref, dyb_ref, dpg_ref, dgb_ref, dwoa_ref, dwob_ref, dwo_ref,
             w_oa, w_ob, w_o, acc_oa, acc_ob, acc_o, sems):
        @pl.when(pl.program_id(0) == 0)
        def _():
            _load_weights(gath_ref, grp, ["oa", "ob", "o"], [w_oa, w_ob, w_o], sems)
            for ref in (dgb_ref, acc_oa, acc_ob, acc_o):
                _zero(ref)

        dxb = dx_ref[...]
        dm = _dot_nt(dxb, w_o[...]).astype(BF16)
        oa = oa_ref[...]
        ob = ob_ref[...]
        sa = _sigmoid(ga_ref[...] + gbias_ref[0:1, :].astype(BF16))
        sb = _sigmoid(gb_ref[...] + gbias_ref[1:2, :].astype(BF16))
        acc_o[...] += _dot_tn(sa * oa + sb * ob, dxb)
        doa = dm * sa
        dob = dm * sb
        acc_oa[...] += _dot_tn(ya_ref[...], doa)
        acc_ob[...] += _dot_tn(yb_ref[...], dob)
        dga = dm * oa * sa * (1.0 - sa)
        dgb = dm * ob * sb * (1.0 - sb)
        dpg_ref[:, 0:d] = dga
        dpg_ref[:, d:2 * d] = dgb
        ones = jnp.ones((SUBLANE, tm), BF16)
        dgb_ref[0:1, :] += _dot_nn(ones, dga)[0:1, :]
        dgb_ref[1:2, :] += _dot_nn(ones, dgb)[0:1, :]
        dya_ref[...] = _dot_nt(doa, w_oa[...]).astype(BF16)
        dyb_ref[...] = _dot_nt(dob, w_ob[...]).astype(BF16)

        @pl.when(pl.program_id(0) == n_t - 1)
        def _():
            outs = []
            for n, (acc, stage, dst) in enumerate(((acc_oa, w_oa, dwoa_ref), (acc_ob, w_ob, dwob_ref), (acc_o, w_o, dwo_ref))):
                stage[...] = acc[...].astype(BF16)
                outs.append(pltpu.make_async_copy(stage, dst, sems.at[n]))
                outs[-1].start()
            for cp in outs:
                cp.wait()

    row = pl.BlockSpec((tm, d), lambda i: (i, 0))
    sd = jax.ShapeDtypeStruct
    return pl.pallas_call(
        body, name=name, grid=(n_t,),
        out_shape=[sd((t, d), BF16), sd((t, d), BF16), sd((t, 2 * d), BF16), sd((SUBLANE, d), F32),
                   sd((d, d), BF16), sd((d, d), BF16), sd((d, d), BF16)],
        in_specs=[row, row, row, row, row, pl.BlockSpec((tm, d), lambda i: (i, 5)), pl.BlockSpec((tm, d), lambda i: (i, 6)),
                  pl.BlockSpec((SUBLANE, d), lambda i: (0, 0)), ANY],
        out_specs=[row, row, pl.BlockSpec((tm, 2 * d), lambda i: (i, 0)), pl.BlockSpec((SUBLANE, d), lambda i: (0, 0)),
                   ANY, ANY, ANY],
        scratch_shapes=[pltpu.VMEM((d, d), BF16)] * 3 + [pltpu.VMEM((d, d), F32)] * 3 + [pltpu.SemaphoreType.DMA((3 * N_DEV,))],
        compiler_params=_params(),
    )(dx1b, oa, ob, ya, yb, p, p, gbias, gath)


DV_CONV_B_B, DV_BA, DV_BX, DV_SP, DV_CONV_A, DV_CONV_B = 0, 1, 2, 3, 4, 7
DV_ROWS = 16


def _mixer_bwd(dya, dyb, dpg, p, u_s, h_s, caw, cbw, vec, bda, bdx, name, comms=()):
    t, d = dya.shape
    tm = _time_tile(t)
    n_t = t // tm
    nk = tm // SUBLANE
    cw = min(MXU_TILE, d)
    nb = d // cw
    halo = 4 * SUBLANE
    ka, kb = CONV_A_K - 1, CONV_B_K - 1

    def body(dya_ref, dyb_ref, dpg_ref, ba_ref, ca_ref, xa_ref, xb_ref, gb_ref, cah_ref, xah_ref, xbh_ref,
             u_ref, h_ref, hh_ref, caw_ref, cbw_ref, vec_ref, bda_ref, bdx_ref,
             dp_ref, dv_ref, dwa_ref, dwx_ref,
             zbuf, xbuf, hbuf, dczbuf, dubuf, a2buf, a1buf, lbuf, dcz_head, du_head, a_head, lam_head):
        i = pl.program_id(0)

        @pl.when(i == 0)
        def _():
            for ref in (dv_ref, dwa_ref, dwx_ref, dcz_head, du_head, a_head, lam_head):
                _zero(ref)

        has_prev = jnp.where(i < n_t - 1, 1.0, 0.0).astype(F32)
        row = lax.broadcasted_iota(jnp.int32, (SUBLANE, cw), 0)
        dp_ref[:, 5 * d:7 * d] = dpg_ref[...]

        def colsum(v):
            return jnp.sum(v, axis=0, keepdims=True)

        for j in range(nb):
            cs = slice(j * cw, (j + 1) * cw)
            ca = ca_ref[:, cs].astype(F32)
            xa = xa_ref[:, cs].astype(F32)
            z = ca * xa
            z_before = cah_ref[:, cs].astype(F32) * xah_ref[:, cs].astype(F32) * has_prev
            _causal_fill(zbuf, z, z_before[halo - ka * SUBLANE:, :], ka, row)
            z2 = zbuf[0:tm, :]
            z1 = zbuf[SUBLANE:SUBLANE + tm, :]
            w0, w1, w2 = caw_ref[0:1, cs], caw_ref[1:2, cs], caw_ref[2:3, cs]
            cz = w0 * z2 + w1 * z1 + w2 * z
            dya = dya_ref[:, cs].astype(F32)
            dp_ref[:, 0 * d + j * cw:0 * d + (j + 1) * cw] = (dya * cz).astype(BF16)
            dcz = dya * ba_ref[:, cs].astype(F32)
            _anticausal_fill(dczbuf, dcz, dcz_head[:, cs], ka, row)
            dcz_head[:, cs] = dcz[0:ka * SUBLANE, :]
            dz = w2 * dcz + w1 * dczbuf[SUBLANE:SUBLANE + tm, :] + w0 * dczbuf[2 * SUBLANE:2 * SUBLANE + tm, :]
            dv_ref[DV_CONV_A + 0:DV_CONV_A + 1, cs] += colsum(dcz * z2)
            dv_ref[DV_CONV_A + 1:DV_CONV_A + 2, cs] += colsum(dcz * z1)
            dv_ref[DV_CONV_A + 2:DV_CONV_A + 3, cs] += colsum(dcz * z)
            dp_ref[:, 1 * d + j * cw:1 * d + (j + 1) * cw] = (dz * xa).astype(BF16)
            dp_ref[:, 2 * d + j * cw:2 * d + (j + 1) * cw] = (dz * ca).astype(BF16)
            h = h_ref[:, cs].astype(F32)
            h_before = hh_ref[:, cs].astype(F32) * has_prev
            _causal_fill(hbuf, h, h_before[halo - SUBLANE:, :], 1, row)
            h_prev = hbuf[0:tm, :]
            dyb = dyb_ref[:, cs].astype(F32)
            gel, dgel = _gelu_and_grad(gb_ref[:, cs].astype(F32))
            dp_ref[:, 4 * d + j * cw:4 * d + (j + 1) * cw] = (dyb * h * dgel).astype(BF16)
            ub = u_ref[:, cs]
            u = ub.astype(F32)
            sp = vec_ref[3:4, cs]
            r, gi, a, s, inv_s = _lru_gates(ub, bda_ref[j], bdx_ref[j], vec_ref[1:2, cs], vec_ref[2:3, cs], sp)
            _anticausal_fill(a2buf, a, a_head[:, cs], 1, row)
            a_head[:, cs] = a[0:SUBLANE, :]
            a1buf[...] = a2buf[SUBLANE:SUBLANE + tm, :]
            lbuf[...] = dyb * gel
            l_end, a_prod = _chain_scan(a1buf, lbuf, nk, reverse=True)
            a_inc, l_inc = _sublane_scan(a_prod, l_end, row, reverse=True)
            carry = lam_head[:, cs]
            ends = l_inc + a_inc * carry
            starts = jnp.where(row == SUBLANE - 1, carry, pltpu.roll(ends, SUBLANE - 1, 0))
            lam_head[:, cs] = jnp.broadcast_to(ends[0:1, :], (SUBLANE, cw))
            lam = (lbuf[...].reshape(nk, SUBLANE, cw) + a1buf[...].reshape(nk, SUBLANE, cw) * starts[None]).reshape(tm, cw)
            da = lam * h_prev
            iu = gi * u
            ds = lam * iu
            di = lam * s * u
            du = lam * s * gi
            dlog_a = da * a - ds * (a * a) * inv_s
            dv_ref[DV_SP:DV_SP + 1, cs] += colsum(dlog_a * r) * (-LRU_C)
            dpr = dlog_a * ((-LRU_C) * sp) * r * (1.0 - r)
            dpi = di * gi * (1.0 - gi)
            dv_ref[DV_BA:DV_BA + 1, cs] += colsum(dpr)
            dv_ref[DV_BX:DV_BX + 1, cs] += colsum(dpi)
            dprb = dpr.astype(BF16)
            dpib = dpi.astype(BF16)
            du = du + _dot_nt(dprb, bda_ref[j]) + _dot_nt(dpib, bdx_ref[j])
            dwa_ref[j] += _dot_tn(ub, dprb)
            dwx_ref[j] += _dot_tn(ub, dpib)
            xb = xb_ref[:, cs].astype(F32)
            x_before = xbh_ref[:, cs].astype(F32) * has_prev
            _causal_fill(xbuf, xb, x_before[halo - kb * SUBLANE:, :], kb, row)
            _anticausal_fill(dubuf, du, du_head[:, cs], kb, row)
            du_head[:, cs] = du[0:kb * SUBLANE, :]
            v0, v1, v2, v3 = cbw_ref[0:1, cs], cbw_ref[1:2, cs], cbw_ref[2:3, cs], cbw_ref[3:4, cs]
            dxb = (v3 * du + v2 * dubuf[SUBLANE:SUBLANE + tm, :] + v1 * dubuf[2 * SUBLANE:2 * SUBLANE + tm, :]
                   + v0 * dubuf[3 * SUBLANE:3 * SUBLANE + tm, :])
            dp_ref[:, 3 * d + j * cw:3 * d + (j + 1) * cw] = dxb.astype(BF16)
            dv_ref[DV_CONV_B_B:DV_CONV_B_B + 1, cs] += colsum(du)
            dv_ref[DV_CONV_B + 0:DV_CONV_B + 1, cs] += colsum(du * xbuf[0:tm, :])
            dv_ref[DV_CONV_B + 1:DV_CONV_B + 2, cs] += colsum(du * xbuf[SUBLANE:SUBLANE + tm, :])
            dv_ref[DV_CONV_B + 2:DV_CONV_B + 3, cs] += colsum(du * xbuf[2 * SUBLANE:2 * SUBLANE + tm, :])
            dv_ref[DV_CONV_B + 3:DV_CONV_B + 4, cs] += colsum(du * xb)

    rt = lambda i: n_t - 1 - i
    row_spec = pl.BlockSpec((tm, d), lambda i: (rt(i), 0))
    slab = lambda s: pl.BlockSpec((tm, d), lambda i, s=s: (rt(i), s))
    before = lambda s: pl.BlockSpec((halo, d), lambda i, s=s: (jnp.maximum(rt(i) * (tm // halo) - 1, 0), s))
    small = pl.BlockSpec((SUBLANE, d), lambda i: (0, 0))
    bd = pl.BlockSpec((nb, cw, cw), lambda i: (0, 0, 0))
    sd = jax.ShapeDtypeStruct
    wbuf = lambda n: pltpu.VMEM((tm + n * SUBLANE, cw), F32)
    head = lambda n: pltpu.VMEM((n * SUBLANE, d), F32)
    return _hosted_call(
        body, comms, name=name, grid=(n_t,),
        out_shape=[sd((t, N_PROJ * d), BF16), sd((DV_ROWS, d), F32), sd((nb, cw, cw), F32), sd((nb, cw, cw), F32)],
        in_specs=[row_spec, row_spec, pl.BlockSpec((tm, 2 * d), lambda i: (rt(i), 0)),
                  slab(0), slab(1), slab(2), slab(3), slab(4), before(1), before(2), before(3),
                  row_spec, row_spec, before(0), small, small, small, bd, bd],
        out_specs=[pl.BlockSpec((tm, N_PROJ * d), lambda i: (rt(i), 0)), pl.BlockSpec((DV_ROWS, d), lambda i: (0, 0)), bd, bd],
        scratch_shapes=[wbuf(ka), wbuf(kb), wbuf(1), wbuf(ka), wbuf(kb), wbuf(1),
                        pltpu.VMEM((tm, cw), F32), pltpu.VMEM((tm, cw), F32), head(ka), head(kb), head(1), head(1)],
        args=(dya, dyb, dpg, p, p, p, p, p, p, p, p, u_s, h_s, h_s, caw, cbw, vec, bda, bdx))


def _in_proj_bwd(dp, x, dx1, g_row, gath, grp, name, comms=()):
    t, d = x.shape
    tm = _row_tile(t, 512)
    n_in = N_PROJ * d

    def body(dp_ref, x_ref, dx1_ref, g_ref, gath_ref, dx_ref, dg_ref, w_in, sems):
        @pl.when(pl.program_id(0) == 0)
        def _():
            _load_weights(gath_ref, grp, ["in"], [w_in], sems)
            _zero(dg_ref)

        dh = _dot_nn(dp_ref[:, 0:d], w_in[0:d, :])
        for k in range(1, N_PROJ):
            dh = dh + _dot_nn(dp_ref[:, k * d:(k + 1) * d], w_in[k * d:(k + 1) * d, :])
        xf = x_ref[...]
        rstd = lax.rsqrt(jnp.mean(xf * xf, axis=-1, keepdims=True) + RMS_EPS)
        xh = xf * rstd
        g = g_ref[...]
        dg_ref[0:1, :] += jnp.sum(dh * xh, axis=0, keepdims=True)
        dxh = dh * g
        dx_ref[...] = dx1_ref[...] + rstd * (dxh - xh * jnp.mean(dxh * xh, axis=-1, keepdims=True))

    row = pl.BlockSpec((tm, d), lambda i: (i, 0))
    sd = jax.ShapeDtypeStruct
    return _hosted_call(
        body, comms, name=name, grid=(t // tm,),
        out_shape=[sd((t, d), F32), sd((SUBLANE, d), F32)],
        in_specs=[pl.BlockSpec((tm, n_in), lambda i: (i, 0)), row, row, pl.BlockSpec((1, d), lambda i: (0, 0)), ANY],
        out_specs=[row, pl.BlockSpec((SUBLANE, d), lambda i: (0, 0))],
        scratch_shapes=[pltpu.VMEM((n_in, d), BF16), pltpu.SemaphoreType.DMA((N_DEV,))],
        args=(dp, x, dx1, g_row, gath))


def _weight_grad(a, b, name):
    t, m = a.shape
    n = b.shape[1]
    bt = _row_tile(t, 1024)
    bm = m
    for div in (1, 2, 4, 8):
        if m % div == 0 and (m // div) % LANE == 0 and (m // div) * n * 4 <= (12 << 20):
            bm = m // div
            break
    n_t = t // bt

    def body(a_ref, b_ref, o_ref, acc):
        k = pl.program_id(1)

        @pl.when(k == 0)
        def _():
            _zero(acc)

        acc[...] += _dot_tn(a_ref[...], b_ref[...])

        @pl.when(k == n_t - 1)
        def _():
            o_ref[...] = acc[...].astype(BF16)

    return pl.pallas_call(
        body, name=name, grid=(m // bm, n_t),
        out_shape=jax.ShapeDtypeStruct((m, n), BF16),
        in_specs=[pl.BlockSpec((bt, bm), lambda i, k: (k, i)), pl.BlockSpec((bt, n), lambda i, k: (k, 0))],
        out_specs=pl.BlockSpec((bm, n), lambda i, k: (i, 0)),
        scratch_shapes=[pltpu.VMEM((bm, n), F32)],
        compiler_params=_params(2),
    )(a, b)


def _adamw(w, g, m, v, name):
    r, c = w.shape
    tr = _fit_rows(r, c * 4)
    c1 = 1.0 - ADAM_B1 ** ADAM_STEP
    c2 = 1.0 - ADAM_B2 ** ADAM_STEP

    def body(w_ref, g_ref, m_ref, v_ref, d_ref, nm_ref, nv_ref):
        g32 = g_ref[...]
        nm = ADAM_B1 * m_ref[...] + (1.0 - ADAM_B1) * g32
        nv = ADAM_B2 * v_ref[...] + (1.0 - ADAM_B2) * (g32 * g32)
        nm_ref[...] = nm
        nv_ref[...] = nv
        d_ref[...] = -ADAM_LR * ((nm / c1) / (jnp.sqrt(nv / c2) + ADAM_EPS) + ADAM_WD * w_ref[...])

    spec = pl.BlockSpec((tr, c), lambda i: (i, 0))
    return pl.pallas_call(
        body, name=name, grid=(r // tr,),
        out_shape=[jax.ShapeDtypeStruct((r, c), F32)] * 3,
        in_specs=[spec] * 4, out_specs=[spec] * 3,
        compiler_params=_params(),
    )(w, g, m, v)


def _pad_rows(a, mult=SUBLANE):
    pad = (-a.shape[0]) % mult
    return a if pad == 0 else jnp.concatenate([a, jnp.zeros((pad,) + a.shape[1:], a.dtype)], axis=0)


REPLICATED = ("ln1_g", "conv_b_b", "lru_wa", "lru_ba", "lru_wx", "lru_bx", "lru_lambda", "ln2_g", "final_g")
SMALL_SHARDED = ("conv_a_w", "conv_b_w", "gate_bias")
MATRICES = ("w_in", "w_out_a", "w_out_b", "w_o", "w_ffn_gate", "w_ffn_up", "w_ffn_down")
ORDER = ("ln1_g", "w_in", "conv_a_w", "conv_b_w", "conv_b_b", "lru_wa", "lru_ba", "lru_wx", "lru_bx", "lru_lambda",
         "w_out_a", "w_out_b", "gate_bias", "w_o", "ln2_g", "w_ffn_gate", "w_ffn_up", "w_ffn_down", "final_g")


def kernel(x, ln1_g, w_in, conv_a_w, conv_b_w, conv_b_b, lru_wa, lru_ba, lru_wx, lru_bx, lru_lambda, w_out_a, w_out_b, gate_bias, w_o, ln2_g, w_ffn_gate, w_ffn_up, w_ffn_down, final_g, loss_target, m_ln1_g, m_w_in, m_conv_a_w, m_conv_b_w, m_conv_b_b, m_lru_wa, m_lru_ba, m_lru_wx, m_lru_bx, m_lru_lambda, m_w_out_a, m_w_out_b, m_gate_bias, m_w_o, m_ln2_g, m_w_ffn_gate, m_w_ffn_up, m_w_ffn_down, m_final_g, v_ln1_g, v_w_in, v_conv_a_w, v_conv_b_w, v_conv_b_b, v_lru_wa, v_lru_ba, v_lru_wx, v_lru_bx, v_lru_lambda, v_w_out_a, v_w_out_b, v_gate_bias, v_w_o, v_ln2_g, v_w_ffn_gate, v_w_ffn_up, v_w_ffn_down, v_final_g):
    w = dict(ln1_g=ln1_g, w_in=w_in, conv_a_w=conv_a_w, conv_b_w=conv_b_w, conv_b_b=conv_b_b, lru_wa=lru_wa,
             lru_ba=lru_ba, lru_wx=lru_wx, lru_bx=lru_bx, lru_lambda=lru_lambda, w_out_a=w_out_a, w_out_b=w_out_b,
             gate_bias=gate_bias, w_o=w_o, ln2_g=ln2_g, w_ffn_gate=w_ffn_gate, w_ffn_up=w_ffn_up,
             w_ffn_down=w_ffn_down, final_g=final_g)
    mom = dict(ln1_g=m_ln1_g, w_in=m_w_in, conv_a_w=m_conv_a_w, conv_b_w=m_conv_b_w, conv_b_b=m_conv_b_b,
               lru_wa=m_lru_wa, lru_ba=m_lru_ba, lru_wx=m_lru_wx, lru_bx=m_lru_bx, lru_lambda=m_lru_lambda,
               w_out_a=m_w_out_a, w_out_b=m_w_out_b, gate_bias=m_gate_bias, w_o=m_w_o, ln2_g=m_ln2_g,
               w_ffn_gate=m_w_ffn_gate, w_ffn_up=m_w_ffn_up, w_ffn_down=m_w_ffn_down, final_g=m_final_g)
    var = dict(ln1_g=v_ln1_g, w_in=v_w_in, conv_a_w=v_conv_a_w, conv_b_w=v_conv_b_w, conv_b_b=v_conv_b_b,
               lru_wa=v_lru_wa, lru_ba=v_lru_ba, lru_wx=v_lru_wx, lru_bx=v_lru_bx, lru_lambda=v_lru_lambda,
               w_out_a=v_w_out_a, w_out_b=v_w_out_b, gate_bias=v_gate_bias, w_o=v_w_o, ln2_g=v_ln2_g,
               w_ffn_gate=v_w_ffn_gate, w_ffn_up=v_w_ffn_up, w_ffn_down=v_w_ffn_down, final_g=v_final_g)

    _, t, d = x.shape
    n_layers = w_in.shape[0]
    ff = w_ffn_down.shape[1] * N_DEV
    dd = d // N_DEV
    hd = d // LRU_HEADS
    cw = min(MXU_TILE, d)
    nb = d // cw
    hpt = cw // hd
    grp = _groups(d, ff)
    me = 4 * lax.axis_index("x") + 2 * lax.axis_index("y") + lax.axis_index("c")
    tm_time = _time_tile(t)
    x0 = _to_tile_order(x[0], tm_time)
    target = _to_tile_order(loss_target[0], tm_time)

    packed = [{"in": jnp.swapaxes(w_in[l], 0, 1).astype(BF16),
               "rest": jnp.concatenate([w_out_a[l], w_out_b[l], w_o[l], jnp.swapaxes(w_ffn_gate[l], 0, 1),
                                        jnp.swapaxes(w_ffn_up[l], 0, 1), w_ffn_down[l]], axis=0).astype(BF16)}
              for l in range(n_layers)]
    n_small = CONV_A_K + CONV_B_K + 2
    small = _pad_rows(jnp.concatenate([conv_a_w, conv_b_w, gate_bias], axis=1).reshape(n_layers * n_small, dd))
    sp = jax.nn.softplus(-lru_lambda)
    vec = [_pad_rows(jnp.stack([conv_b_b[l], lru_ba[l], lru_bx[l], sp[l]])) for l in range(n_layers)]
    eye = jnp.eye(hpt, dtype=F32)

    def block_diag(wh):
        return jnp.einsum("jkab,kl->jkalb", wh.reshape(nb, hpt, hd, hd), eye).reshape(nb, cw, cw).astype(BF16)

    bda = [block_diag(lru_wa[l]) for l in range(n_layers)]
    bdx = [block_diag(lru_wx[l]) for l in range(n_layers)]

    gath = [dict() for _ in range(n_layers)]
    (gath[0]["in"],) = _comm_call(_Gather(packed[0]["in"]), "gather_in_0")
    saved = []
    xl = x0
    for l in range(n_layers):
        comms = [_Gather(packed[0]["rest"]), _Gather(small)] if l == 0 else []
        (p, h1b), got = _in_proj_fwd(xl, ln1_g[l][None], gath[l]["in"], grp["in"], f"in_proj_fwd_{l}", comms)
        if l == 0:
            gath[0]["rest"], small_g = got[0][0], got[1][0]
            small_full = jnp.swapaxes(small_g[:, :n_layers * n_small], 0, 1).reshape(n_layers, n_small, d)
            caw = [_pad_rows(small_full[k, 0:CONV_A_K]) for k in range(n_layers)]
            cbw = [_pad_rows(small_full[k, CONV_A_K:CONV_A_K + CONV_B_K]) for k in range(n_layers)]
            gbias = [_pad_rows(small_full[k, CONV_A_K + CONV_B_K:]) for k in range(n_layers)]
        more = l + 1 < n_layers
        (ya, yb, u_s, h_s), got = _mixer_fwd(p, caw[l], cbw[l], vec[l], bda[l], bdx[l], f"mixer_fwd_{l}",
                                             [_Gather(packed[l + 1]["in"])] if more else [])
        if more:
            ((gath[l + 1]["in"],),) = got
        x1, oa, ob = _merge_fwd(xl, ya, yb, p, gbias[l], gath[l]["rest"], grp["rest"], f"merge_fwd_{l}")
        (x2, gg, uu), got = _ffn_fwd(x1, ln2_g[l][None], gath[l]["rest"], grp["rest"], f"ffn_fwd_{l}",
                                     [_Gather(packed[l + 1]["rest"])] if more else [])
        if more:
            ((gath[l + 1]["rest"],),) = got
        saved.append(dict(x=xl, p=p, h1b=h1b, ya=ya, yb=yb, u=u_s, h=h_s, x1=x1, oa=oa, ob=ob, gg=gg, uu=uu))
        xl = x2
    loss_tile, dx, dfinal = _loss_head(xl, final_g[None], target, "loss_head")
    loss = lax.psum(loss_tile[0, 0], ("x", "y", "c"))

    def heads(dwb):
        blocks = jnp.diagonal(dwb.reshape(nb, hpt, hd, hpt, hd), axis1=1, axis2=3)
        return jnp.moveaxis(blocks, 3, 1).reshape(hd, d)

    layer_names = [n for n in REPLICATED if n != "final_g"] + list(SMALL_SHARDED)

    def layer_block(k):
        return jnp.concatenate([small_grads[k][n] for n in layer_names], axis=0)

    recv = [dict() for _ in range(n_layers)]
    small_grads = [None] * n_layers
    early_all = None
    for l in reversed(range(n_layers)):
        s = saved[l]
        dgg, duu, dw_d = _ffn_bwd_act(dx, s["gg"], s["uu"], gath[l]["rest"], grp["rest"], f"ffn_bwd_act_{l}")
        comms = []
        if l == 0:
            early = [layer_block(k) for k in range(1, n_layers)] + [_pad_rows(dfinal[0:1])]
            comms = [_Gather(jnp.concatenate(early, axis=0))]
        (dx1, dx1b, h2b, dln2), got = _ffn_bwd_in(dgg, duu, dx, s["x1"], ln2_g[l][None], gath[l]["rest"], grp["rest"],
                                                  f"ffn_bwd_in_{l}", comms)
        if l == 0:
            ((early_all,),) = got
        dya, dyb, dpg, dgbias, dw_oa, dw_ob, dw_o = _merge_bwd(dx1b, s["oa"], s["ob"], s["ya"], s["yb"], s["p"], gbias[l],
                                                               gath[l]["rest"], grp["rest"], f"merge_bwd_{l}")
        rest = {"oa": dw_oa, "ob": dw_ob, "o": dw_o, "g": _weight_grad(dgg, h2b, f"dw_ffn_gate_{l}"),
                "u": _weight_grad(duu, h2b, f"dw_ffn_up_{l}"), "d": dw_d}
        (dp, dv, dwa, dwx), got = _mixer_bwd(dya, dyb, dpg, s["p"], s["u"], s["h"], caw[l], cbw[l], vec[l], bda[l], bdx[l],
                                             f"mixer_bwd_{l}", [_Exchange(rest, grp["rest"])])
        (recv[l]["rest"],) = got[0]
        small_grads[l] = {
            "conv_b_b": dv[DV_CONV_B_B:DV_CONV_B_B + 1], "lru_wa": heads(dwa),
            "lru_ba": dv[DV_BA:DV_BA + 1], "lru_wx": heads(dwx), "lru_bx": dv[DV_BX:DV_BX + 1],
            "lru_lambda": dv[DV_SP:DV_SP + 1] * (-jax.nn.sigmoid(-lru_lambda[l]))[None], "ln2_g": dln2[0:1],
            "conv_a_w": dv[DV_CONV_A:DV_CONV_A + CONV_A_K], "conv_b_w": dv[DV_CONV_B:DV_CONV_B + CONV_B_K],
            "gate_bias": dgbias[0:2],
        }
        dw_in = _Exchange({"in": _weight_grad(dp, s["h1b"], f"dw_in_{l}")}, grp["in"])
        (dx, dln1), got = _in_proj_bwd(dp, s["x"], dx1, ln1_g[l][None], gath[l]["in"], grp["in"], f"in_proj_bwd_{l}", [dw_in])
        ((recv[l]["in"],),) = got
        small_grads[l]["ln1_g"] = dln1[0:1]
    grad_x = _from_tile_order(dx, tm_time)[None]

    g = {}
    gsum = [{k: _sum_slots(recv[l][k], f"sum_{k}_{l}") for k in ("in", "rest")} for l in range(n_layers)]

    def part(key):
        k = "in" if key == "in" else "rest"
        o, r = grp[k].off[key], grp[k].rows[key]
        return jnp.stack([gsum[l][k][o:o + r] for l in range(n_layers)])

    g = {"w_in": jnp.swapaxes(part("in"), 1, 2), "w_out_a": part("oa"), "w_out_b": part("ob"), "w_o": part("o"),
         "w_ffn_gate": jnp.swapaxes(part("g"), 1, 2), "w_ffn_up": jnp.swapaxes(part("u"), 1, 2), "w_ffn_down": part("d")}
    (late_all,) = _comm_call(_Gather(layer_block(0)), "gather_small_grads_0")
    early_sum = _sum_slots(early_all, "sum_small_grads")
    block_rows = late_all.shape[1]
    per_layer = [_sum_slots(late_all, "sum_small_grads_0")]
    per_layer += [early_sum[(k - 1) * block_rows:k * block_rows] for k in range(1, n_layers)]
    g["final_g"] = early_sum[(n_layers - 1) * block_rows].reshape(w["final_g"].shape)
    o = 0
    for n in layer_names:
        rows = small_grads[0][n].shape[0]
        stacked = jnp.concatenate([per_layer[k][o:o + rows] for k in range(n_layers)], axis=0)
        if n in SMALL_SHARDED:
            g[n] = lax.dynamic_slice_in_dim(stacked, me * dd, dd, axis=1).reshape(n_layers, rows, dd)
        else:
            g[n] = stacked.reshape(w[n].shape)
        o += rows

    delta, new_m, new_v = {}, {}, {}
    for n in MATRICES:
        shape = w[n].shape
        flat = lambda a: a.reshape(shape[0] * shape[1], shape[2])
        dl, nm, nv = _adamw(flat(w[n]), flat(g[n]), flat(mom[n]), flat(var[n]), f"adamw_{n}")
        delta[n], new_m[n], new_v[n] = dl.reshape(shape), nm.reshape(shape), nv.reshape(shape)
    for group, width, name in ((REPLICATED, d, "adamw_replicated"), (SMALL_SHARDED, dd, "adamw_vectors")):
        cat = lambda src: _pad_rows(jnp.concatenate([src[n].reshape(-1, width) for n in group], axis=0))
        dl, nm, nv = _adamw(cat(w), cat(g), cat(mom), cat(var), name)
        o = 0
        for n in group:
            rows = w[n].size // width
            delta[n], new_m[n], new_v[n] = (a[o:o + rows].reshape(w[n].shape) for a in (dl, nm, nv))
            o += rows

    return (loss, grad_x, *[g[n] for n in ORDER], *[delta[n] for n in ORDER], *[new_m[n] for n in ORDER],
            *[new_v[n] for n in ORDER])
```

```python
import math

import jax
import jax.numpy as jnp
from jax import lax
from jax.experimental import pallas as pl
from jax.experimental.pallas import tpu as pltpu

F32 = jnp.float32
BF16 = jnp.bfloat16

N_DEV = 8
N_PROJ = 7
LRU_HEADS = 16
LRU_C = 8.0
RMS_EPS = 1e-6
CONV_A_K = 3
CONV_B_K = 4
GELU_C = math.sqrt(2.0 / math.pi)
GELU_A = 0.044715

ADAM_LR = 0.001
ADAM_B1 = 0.9
ADAM_B2 = 0.999
ADAM_EPS = 1e-08
ADAM_WD = 0.01
ADAM_STEP = 10

LANE = 128
SUBLANE = 8
MXU_TILE = 256
VMEM_LIMIT = 52 << 20
MESH = pl.DeviceIdType.MESH
ANY = pl.BlockSpec(memory_space=pl.ANY)


def _dot_nn(a, b):
    return lax.dot_general(a, b, (((1,), (0,)), ((), ())), preferred_element_type=F32)


def _dot_nt(a, b):
    return lax.dot_general(a, b, (((1,), (1,)), ((), ())), preferred_element_type=F32)


def _dot_tn(a, b):
    return lax.dot_general(a, b, (((0,), (0,)), ((), ())), preferred_element_type=F32)


def _sigmoid(x):
    return 1.0 / (1.0 + jnp.exp(-x))


def _gelu_and_grad(x):
    x2 = x * x
    t = jnp.tanh(GELU_C * x * (1.0 + GELU_A * x2))
    g = 0.5 * x * (1.0 + t)
    dg = 0.5 * (1.0 + t) + 0.5 * x * (1.0 - t * t) * GELU_C * (1.0 + 3.0 * GELU_A * x2)
    return g, dg


def _zero(ref):
    ref[...] = jnp.zeros(ref.shape, ref.dtype)


def _fit_rows(r, row_bytes, budget=1 << 20):
    fits = [t for t in range(16, r + 1, 16) if r % t == 0 and t * row_bytes <= budget]
    return max(fits) if fits else r


def _row_tile(t, want):
    tm = min(want, t // 2)
    assert t % tm == 0 and tm % SUBLANE == 0, (t, tm)
    return tm


def _params(n_grid=1, **kw):
    return pltpu.CompilerParams(dimension_semantics=("arbitrary",) * n_grid, vmem_limit_bytes=VMEM_LIMIT, **kw)


class _Group:
    def __init__(self, keys, rows):
        self.keys = keys
        self.rows = dict(zip(keys, rows))
        self.off, o = {}, 0
        for k in keys:
            self.off[k] = o
            o += self.rows[k]
        self.total = o


def _groups(d, ff):
    dd, ffs = d // N_DEV, ff // N_DEV
    return {"in": _Group(("in",), (N_PROJ * dd,)),
            "rest": _Group(("oa", "ob", "o", "g", "u", "d"), (dd, dd, dd, ffs, ffs, ffs))}


def _load_weights(g_ref, grp, keys, dsts, sems):
    copies = []
    for n, (k, dst) in enumerate(zip(keys, dsts)):
        rows, off = grp.rows[k], grp.off[k]
        copies += [pltpu.make_async_copy(g_ref.at[p, pl.ds(off, rows), :], dst.at[pl.ds(p * rows, rows), :],
                                         sems.at[n * N_DEV + p]) for p in range(N_DEV)]
    for c in copies:
        c.start()
    for c in copies:
        c.wait()


def _comm_sems():
    return [pltpu.SemaphoreType.DMA((N_DEV - 1,)), pltpu.SemaphoreType.DMA((N_DEV - 1,)), pltpu.SemaphoreType.DMA]


class _Gather:
    def __init__(self, x):
        self.inputs = [x]
        self.out_shape = [jax.ShapeDtypeStruct((N_DEV,) + x.shape, x.dtype)]
        self.scratch = _comm_sems()

    def _plan(self, ins, outs, scr):
        (x_ref,), (out_ref,), (send_sems, recv_sems, local_sem) = ins, outs, scr
        mx, my, mc = lax.axis_index("x"), lax.axis_index("y"), lax.axis_index("c")
        me, sibling = (mx, my, mc), (mx, my, 1 - mc)
        chips = [(1 - mx, my), (mx, 1 - my), (1 - mx, 1 - my)]

        def slot(px, py, pc):
            return out_ref.at[4 * px + 2 * py + pc]

        def copy(k, block, to, src=None):
            return pltpu.make_async_remote_copy(
                src_ref=slot(*block) if src is None else src, dst_ref=slot(*block),
                send_sem=send_sems.at[k], recv_sem=recv_sems.at[k], device_id=to, device_id_type=MESH)

        mine = lambda: pltpu.make_async_copy(x_ref, slot(*me), local_sem)
        first = [lambda: copy(0, me, sibling, src=x_ref)]
        first += [lambda j=j, chip=chip: copy(1 + j, me, (*chip, mc), src=x_ref) for j, chip in enumerate(chips)]
        landed = [lambda j=j, chip=chip: copy(1 + j, (*chip, mc), me) for j, chip in enumerate(chips)]
        passed = [lambda j=j, chip=chip: copy(4 + j, (*chip, mc), sibling) for j, chip in enumerate(chips)]
        from_sibling = [lambda: copy(0, sibling, me)]
        from_sibling += [lambda j=j, chip=chip: copy(4 + j, (*chip, 1 - mc), me) for j, chip in enumerate(chips)]
        return mine, first, landed, passed, from_sibling

    def start(self, ins, outs, scr):
        mine, first, _, _, _ = self._plan(ins, outs, scr)
        mine().start()
        for cp in first:
            cp().start()

    def mid(self, ins, outs, scr):
        _, _, landed, passed, _ = self._plan(ins, outs, scr)
        for got, fwd in zip(landed, passed):
            got().wait_recv()
            fwd().start()

    def finish(self, ins, outs, scr):
        mine, first, _, passed, from_sibling = self._plan(ins, outs, scr)
        for cp in from_sibling:
            cp().wait_recv()
        for cp in first + passed:
            cp().wait_send()
        mine().wait()


class _Exchange:
    def __init__(self, mats, grp):
        self.grp = grp
        self.inputs = [mats[k] for k in grp.keys]
        self.out_shape = [jax.ShapeDtypeStruct((N_DEV, grp.total, self.inputs[0].shape[1]), BF16)]
        self.scratch = _comm_sems()

    def _pieces(self, g_refs, out_ref, q, dst_slot):
        out = []
        for g_ref, k in zip(g_refs, self.grp.keys):
            rows = self.grp.rows[k]
            out.append((g_ref.at[pl.ds(pl.multiple_of(q * rows, 16), rows), :],
                        out_ref.at[dst_slot, pl.ds(self.grp.off[k], rows), :]))
        return out

    def start(self, ins, outs, scr):
        (out_ref,), (send_sems, recv_sems, local_sem) = outs, scr
        mx, my, mc = lax.axis_index("x"), lax.axis_index("y"), lax.axis_index("c")
        me = 4 * mx + 2 * my + mc
        for s, t in self._pieces(ins, out_ref, me, me):
            pltpu.make_async_copy(s, t, local_sem).start()
        for k in range(1, N_DEV):
            px, py, pc = mx ^ ((k >> 2) & 1), my ^ ((k >> 1) & 1), mc ^ (k & 1)
            for s, t in self._pieces(ins, out_ref, 4 * px + 2 * py + pc, me):
                pltpu.make_async_remote_copy(src_ref=s, dst_ref=t, send_sem=send_sems.at[k - 1],
                                             recv_sem=recv_sems.at[k - 1], device_id=(px, py, pc),
                                             device_id_type=MESH).start()

    def mid(self, ins, outs, scr):
        pass

    def finish(self, ins, outs, scr):
        (out_ref,), (send_sems, recv_sems, local_sem) = outs, scr
        mx, my, mc = lax.axis_index("x"), lax.axis_index("y"), lax.axis_index("c")
        whole = out_ref.at[0]
        for k in range(1, N_DEV):
            done = pltpu.make_async_remote_copy(src_ref=whole, dst_ref=whole, send_sem=send_sems.at[k - 1],
                                                recv_sem=recv_sems.at[k - 1], device_id=(mx, my, mc),
                                                device_id_type=MESH)
            done.wait_send()
            done.wait_recv()
        pltpu.make_async_copy(whole, whole, local_sem).wait()


def _split(refs, sizes):
    out, pos = [], 0
    for n in sizes:
        out.append(refs[pos:pos + n])
        pos += n
    return out


def _hosted_call(body, comms, *, name, grid, in_specs, out_specs, out_shape, scratch_shapes, args):
    n_steps = grid[0]
    nc = len(comms)
    sizes = ([len(in_specs)] + [len(c.inputs) for c in comms] + [len(out_specs)] + [len(c.out_shape) for c in comms]
             + [len(scratch_shapes)] + [len(c.scratch) for c in comms])

    def hosted(*refs):
        parts = _split(refs, sizes)
        ins, c_ins = parts[0], parts[1:1 + nc]
        outs, c_outs = parts[1 + nc], parts[2 + nc:2 + 2 * nc]
        scr, c_scr = parts[2 + 2 * nc], parts[3 + 2 * nc:]
        step = pl.program_id(0)
        if comms:
            @pl.when(step == 0)
            def _():
                for c, a, b, s in zip(comms, c_ins, c_outs, c_scr):
                    c.start(a, b, s)

            @pl.when(step == max(n_steps - 2, 0))
            def _():
                for c, a, b, s in zip(comms, c_ins, c_outs, c_scr):
                    c.mid(a, b, s)

        body(*ins, *outs, *scr)
        if comms:
            @pl.when(step == n_steps - 1)
            def _():
                for c, a, b, s in zip(comms, c_ins, c_outs, c_scr):
                    c.finish(a, b, s)

    res = pl.pallas_call(
        hosted, name=name, grid=grid,
        out_shape=[*out_shape, *[o for c in comms for o in c.out_shape]],
        in_specs=[*in_specs, *[ANY for c in comms for _ in c.inputs]],
        out_specs=[*out_specs, *[ANY for c in comms for _ in c.out_shape]],
        scratch_shapes=[*scratch_shapes, *[s for c in comms for s in c.scratch]],
        compiler_params=_params(),
    )(*args, *[a for c in comms for a in c.inputs])
    main, rest = res[:len(out_specs)], res[len(out_specs):]
    return main, _split(rest, [len(c.out_shape) for c in comms])


def _comm_call(comm, name):
    def body(*refs):
        ins, outs, scr = _split(refs, [len(comm.inputs), len(comm.out_shape), len(comm.scratch)])
        comm.start(ins, outs, scr)
        comm.mid(ins, outs, scr)
        comm.finish(ins, outs, scr)

    return pl.pallas_call(
        body, name=name, out_shape=comm.out_shape, in_specs=[ANY] * len(comm.inputs),
        out_specs=[ANY] * len(comm.out_shape), scratch_shapes=comm.scratch,
    )(*comm.inputs)


def _sum_slots(x, name):
    n, r, c = x.shape
    tr = _fit_rows(r, c * 4)

    def body(x_ref, o_ref):
        acc = x_ref[0].astype(F32)
        for p in range(1, n):
            acc = acc + x_ref[p].astype(F32)
        o_ref[...] = acc

    return pl.pallas_call(
        body, name=name, grid=(r // tr,),
        out_shape=jax.ShapeDtypeStruct((r, c), F32),
        in_specs=[pl.BlockSpec((n, tr, c), lambda i: (0, i, 0))],
        out_specs=pl.BlockSpec((tr, c), lambda i: (i, 0)),
        compiler_params=_params(),
    )(x)


def _in_proj_fwd(x, g_row, gath, grp, name, comms=()):
    t, d = x.shape
    tm = _row_tile(t, 512)
    n_in = N_PROJ * d

    def body(x_ref, g_ref, gath_ref, p_ref, h_ref, w_in, sems):
        @pl.when(pl.program_id(0) == 0)
        def _():
            _load_weights(gath_ref, grp, ["in"], [w_in], sems)

        xf = x_ref[...]
        rstd = lax.rsqrt(jnp.mean(xf * xf, axis=-1, keepdims=True) + RMS_EPS)
        h = (xf * rstd * g_ref[...]).astype(BF16)
        h_ref[...] = h
        for k in range(N_PROJ):
            p_ref[:, k * d:(k + 1) * d] = _dot_nt(h, w_in[k * d:(k + 1) * d, :]).astype(BF16)

    row = pl.BlockSpec((tm, d), lambda i: (i, 0))
    return _hosted_call(
        body, comms, name=name, grid=(t // tm,),
        out_shape=[jax.ShapeDtypeStruct((t, n_in), BF16), jax.ShapeDtypeStruct((t, d), BF16)],
        in_specs=[row, pl.BlockSpec((1, d), lambda i: (0, 0)), ANY],
        out_specs=[pl.BlockSpec((tm, n_in), lambda i: (i, 0)), row],
        scratch_shapes=[pltpu.VMEM((n_in, d), BF16), pltpu.SemaphoreType.DMA((N_DEV,))],
        args=(x, g_row, gath))


def _time_tile(t):
    return _row_tile(t, 256)


def _to_tile_order(a, tm):
    t, c = a.shape
    return jnp.swapaxes(a.reshape(t // tm, SUBLANE, tm // SUBLANE, c), 1, 2).reshape(t, c)


def _from_tile_order(a, tm):
    t, c = a.shape
    return jnp.swapaxes(a.reshape(t // tm, tm // SUBLANE, SUBLANE, c), 1, 2).reshape(t, c)


def _causal_fill(buf, v, prev_tail, n, row):
    tm = v.shape[0]
    for q in range(n):
        cur = v[tm - SUBLANE * (n - q):tm - SUBLANE * (n - q - 1), :]
        prv = prev_tail[SUBLANE * q:SUBLANE * (q + 1), :]
        buf[SUBLANE * q:SUBLANE * (q + 1), :] = jnp.where(row == 0, pltpu.roll(prv, 1, 0), pltpu.roll(cur, 1, 0))
    buf[SUBLANE * n:, :] = v


def _anticausal_fill(buf, v, next_head, n, row):
    tm = v.shape[0]
    buf[0:tm, :] = v
    for q in range(n):
        cur = v[SUBLANE * q:SUBLANE * (q + 1), :]
        nxt = next_head[SUBLANE * q:SUBLANE * (q + 1), :]
        buf[tm + SUBLANE * q:tm + SUBLANE * (q + 1), :] = jnp.where(
            row == SUBLANE - 1, pltpu.roll(nxt, SUBLANE - 1, 0), pltpu.roll(cur, SUBLANE - 1, 0))


def _chain_scan(abuf, bbuf, nk, reverse):
    cw = abuf.shape[1]

    def step(n, carry):
        h, c = carry
        r0 = pl.multiple_of((nk - 1 - n if reverse else n) * SUBLANE, SUBLANE)
        ak = abuf[pl.ds(r0, SUBLANE), :]
        h = ak * h + bbuf[pl.ds(r0, SUBLANE), :]
        c = ak * c
        bbuf[pl.ds(r0, SUBLANE), :] = h
        abuf[pl.ds(r0, SUBLANE), :] = c
        return h, c

    return lax.fori_loop(0, nk, step, (jnp.zeros((SUBLANE, cw), F32), jnp.ones((SUBLANE, cw), F32)), unroll=8)


def _sublane_scan(a, b, row, reverse):
    for sh in (1, 2, 4):
        if reverse:
            m = row < SUBLANE - sh
            b = jnp.where(m, a * pltpu.roll(b, SUBLANE - sh, 0) + b, b)
            a = jnp.where(m, a * pltpu.roll(a, SUBLANE - sh, 0), a)
        else:
            m = row >= sh
            b = jnp.where(m, a * pltpu.roll(b, sh, 0) + b, b)
            a = jnp.where(m, a * pltpu.roll(a, sh, 0), a)
    return a, b


def _lru_decay(r, sp):
    log_a = (-LRU_C) * r * sp
    a = jnp.exp(log_a)
    s2 = -jnp.tanh(log_a) * (1.0 + a * a)
    inv_s = lax.rsqrt(s2)
    s = jnp.where(s2 > 0.0, s2 * inv_s, 0.0)
    return a, s, inv_s


def _mixer_fwd(p, caw, cbw, vec, bda, bdx, name, comms=()):
    t = p.shape[0]
    d = p.shape[1] // N_PROJ
    tm = _time_tile(t)
    nk = tm // SUBLANE
    cw = min(MXU_TILE, d)
    nb = d // cw

    def body(ba_ref, ca_ref, xa_ref, xb_ref, gb_ref, caw_ref, cbw_ref, vec_ref, bda_ref, bdx_ref,
             ya_ref, yb_ref, u_ref, h_ref, r_ref, gi_ref, gel_ref, dgel_ref,
             zbuf, xbuf, abuf, bbuf, z_tail, x_tail, h_carry):
        @pl.when(pl.program_id(0) == 0)
        def _():
            _zero(z_tail)
            _zero(x_tail)
            _zero(h_carry)

        row = lax.broadcasted_iota(jnp.int32, (SUBLANE, cw), 0)
        for j in range(nb):
            cs = slice(j * cw, (j + 1) * cw)
            z = ca_ref[:, cs].astype(F32) * xa_ref[:, cs].astype(F32)
            _causal_fill(zbuf, z, z_tail[:, cs], CONV_A_K - 1, row)
            z_tail[:, cs] = z[tm - (CONV_A_K - 1) * SUBLANE:, :]
            cz = caw_ref[0:1, cs] * zbuf[0:tm, :] + caw_ref[1:2, cs] * zbuf[SUBLANE:SUBLANE + tm, :] + caw_ref[2:3, cs] * z
            ya_ref[:, cs] = (ba_ref[:, cs].astype(F32) * cz).astype(BF16)
            xb = xb_ref[:, cs].astype(F32)
            _causal_fill(xbuf, xb, x_tail[:, cs], CONV_B_K - 1, row)
            x_tail[:, cs] = xb[tm - (CONV_B_K - 1) * SUBLANE:, :]
            u = (cbw_ref[0:1, cs] * xbuf[0:tm, :] + cbw_ref[1:2, cs] * xbuf[SUBLANE:SUBLANE + tm, :]
                 + cbw_ref[2:3, cs] * xbuf[2 * SUBLANE:2 * SUBLANE + tm, :] + cbw_ref[3:4, cs] * xb + vec_ref[0:1, cs])
            ub = u.astype(BF16)
            u = ub.astype(F32)
            grb = _sigmoid(_dot_nn(ub, bda_ref[j]) + vec_ref[1:2, cs]).astype(BF16)
            gib = _sigmoid(_dot_nn(ub, bdx_ref[j]) + vec_ref[2:3, cs]).astype(BF16)
            gi = gib.astype(F32)
            a, s, _ = _lru_decay(grb.astype(F32), vec_ref[3:4, cs])
            abuf[...] = a
            bbuf[...] = s * (gi * u)
            h_end, a_prod = _chain_scan(abuf, bbuf, nk, reverse=False)
            a_inc, h_inc = _sublane_scan(a_prod, h_end, row, reverse=False)
            carry = h_carry[:, cs]
            ends = h_inc + a_inc * carry
            starts = jnp.where(row == 0, carry, pltpu.roll(ends, 1, 0))
            h_carry[:, cs] = jnp.broadcast_to(ends[SUBLANE - 1:SUBLANE, :], (SUBLANE, cw))
            h = (bbuf[...].reshape(nk, SUBLANE, cw) + abuf[...].reshape(nk, SUBLANE, cw) * starts[None]).reshape(tm, cw)
            gel, dgel = _gelu_and_grad(gb_ref[:, cs].astype(F32))
            yb_ref[:, cs] = (h * gel).astype(BF16)
            u_ref[:, cs] = ub
            h_ref[:, cs] = h.astype(BF16)
            r_ref[:, cs] = grb
            gi_ref[:, cs] = gib
            gel_ref[:, cs] = gel.astype(BF16)
            dgel_ref[:, cs] = dgel.astype(BF16)

    slab = lambda s: pl.BlockSpec((tm, d), lambda i, s=s: (i, s))
    small = pl.BlockSpec((SUBLANE, d), lambda i: (0, 0))
    bd = pl.BlockSpec((nb, cw, cw), lambda i: (0, 0, 0))
    out = pl.BlockSpec((tm, d), lambda i: (i, 0))
    return _hosted_call(
        body, comms, name=name, grid=(t // tm,),
        out_shape=[jax.ShapeDtypeStruct((t, d), BF16)] * 8,
        in_specs=[slab(0), slab(1), slab(2), slab(3), slab(4), small, small, small, bd, bd],
        out_specs=[out] * 8,
        scratch_shapes=[pltpu.VMEM((tm + (CONV_A_K - 1) * SUBLANE, cw), F32), pltpu.VMEM((tm + (CONV_B_K - 1) * SUBLANE, cw), F32),
                        pltpu.VMEM((tm, cw), F32), pltpu.VMEM((tm, cw), F32),
                        pltpu.VMEM(((CONV_A_K - 1) * SUBLANE, d), F32), pltpu.VMEM(((CONV_B_K - 1) * SUBLANE, d), F32),
                        pltpu.VMEM((SUBLANE, d), F32)],
        args=(p, p, p, p, p, caw, cbw, vec, bda, bdx))


def _merge_fwd(x, ya, yb, p, gbias, gath, grp, name):
    t, d = x.shape
    tm = _row_tile(t, 512)

    def body(x_ref, ya_ref, yb_ref, ga_ref, gb_ref, gbias_ref, gath_ref, x1_ref, oa_ref, ob_ref, w_oa, w_ob, w_o, sems):
        @pl.when(pl.program_id(0) == 0)
        def _():
            _load_weights(gath_ref, grp, ["oa", "ob", "o"], [w_oa, w_ob, w_o], sems)

        oa = _dot_nn(ya_ref[...], w_oa[...]).astype(BF16)
        ob = _dot_nn(yb_ref[...], w_ob[...]).astype(BF16)
        oa_ref[...] = oa
        ob_ref[...] = ob
        sa = _sigmoid(ga_ref[...] + gbias_ref[0:1, :].astype(BF16))
        sb = _sigmoid(gb_ref[...] + gbias_ref[1:2, :].astype(BF16))
        x1_ref[...] = x_ref[...] + _dot_nn(sa * oa + sb * ob, w_o[...])

    row = pl.BlockSpec((tm, d), lambda i: (i, 0))
    return pl.pallas_call(
        body, name=name, grid=(t // tm,),
        out_shape=[jax.ShapeDtypeStruct((t, d), F32), jax.ShapeDtypeStruct((t, d), BF16), jax.ShapeDtypeStruct((t, d), BF16)],
        in_specs=[row, row, row, pl.BlockSpec((tm, d), lambda i: (i, 5)), pl.BlockSpec((tm, d), lambda i: (i, 6)),
                  pl.BlockSpec((SUBLANE, d), lambda i: (0, 0)), ANY],
        out_specs=[row, row, row],
        scratch_shapes=[pltpu.VMEM((d, d), BF16)] * 3 + [pltpu.SemaphoreType.DMA((3 * N_DEV,))],
        compiler_params=_params(),
    )(x, ya, yb, p, p, gbias, gath)


def _ffn_fwd(x1, g_row, gath, grp, name, comms=()):
    t, d = x1.shape
    ff = grp.rows["g"] * N_DEV
    tm = _row_tile(t, 512)
    fc = MXU_TILE
    assert ff % fc == 0

    def body(x_ref, g_ref, gath_ref, x2_ref, gg_ref, uu_ref, w_g, w_u, w_d, acc, sems):
        @pl.when(pl.program_id(0) == 0)
        def _():
            _load_weights(gath_ref, grp, ["g", "u", "d"], [w_g, w_u, w_d], sems)

        xf = x_ref[...]
        rstd = lax.rsqrt(jnp.mean(xf * xf, axis=-1, keepdims=True) + RMS_EPS)
        h = (xf * rstd * g_ref[...]).astype(BF16)
        acc[...] = xf
        for c in range(ff // fc):
            fs = slice(c * fc, (c + 1) * fc)
            gg = _dot_nt(h, w_g[fs, :]).astype(BF16)
            uu = _dot_nt(h, w_u[fs, :]).astype(BF16)
            gg_ref[:, fs] = gg
            uu_ref[:, fs] = uu
            acc[...] += _dot_nn(gg * _sigmoid(gg) * uu, w_d[fs, :])
        x2_ref[...] = acc[...]

    row = pl.BlockSpec((tm, d), lambda i: (i, 0))
    wide = pl.BlockSpec((tm, ff), lambda i: (i, 0))
    return _hosted_call(
        body, comms, name=name, grid=(t // tm,),
        out_shape=[jax.ShapeDtypeStruct((t, d), F32), jax.ShapeDtypeStruct((t, ff), BF16), jax.ShapeDtypeStruct((t, ff), BF16)],
        in_specs=[row, pl.BlockSpec((1, d), lambda i: (0, 0)), ANY],
        out_specs=[row, wide, wide],
        scratch_shapes=[pltpu.VMEM((ff, d), BF16)] * 3 + [pltpu.VMEM((tm, d), F32), pltpu.SemaphoreType.DMA((3 * N_DEV,))],
        args=(x1, g_row, gath))


def _loss_head(x, g_row, target, name):
    t, d = x.shape
    tm = _row_tile(t, 512)

    def body(x_ref, g_ref, tgt_ref, loss_ref, dx_ref, dg_ref):
        @pl.when(pl.program_id(0) == 0)
        def _():
            _zero(loss_ref)
            _zero(dg_ref)

        xf = x_ref[...]
        rstd = lax.rsqrt(jnp.mean(xf * xf, axis=-1, keepdims=True) + RMS_EPS)
        xh = xf * rstd
        g = g_ref[...]
        err = xh * g - tgt_ref[...]
        loss_ref[...] += 0.5 * jnp.sum(jnp.sum(err * err, axis=-1, keepdims=True), axis=0, keepdims=True) * (1.0 / d)
        dy = err * (1.0 / d)
        dg_ref[0:1, :] += jnp.sum(dy * xh, axis=0, keepdims=True)
        dxh = dy * g
        dx_ref[...] = rstd * (dxh - xh * jnp.mean(dxh * xh, axis=-1, keepdims=True))

    row = pl.BlockSpec((tm, d), lambda i: (i, 0))
    return pl.pallas_call(
        body, name=name, grid=(t // tm,),
        out_shape=[jax.ShapeDtypeStruct((SUBLANE, LANE), F32), jax.ShapeDtypeStruct((t, d), F32),
                   jax.ShapeDtypeStruct((SUBLANE, d), F32)],
        in_specs=[row, pl.BlockSpec((1, d), lambda i: (0, 0)), row],
        out_specs=[pl.BlockSpec((SUBLANE, LANE), lambda i: (0, 0)), row, pl.BlockSpec((SUBLANE, d), lambda i: (0, 0))],
        compiler_params=_params(),
    )(x, g_row, target)


def _ffn_bwd_act(dx2, gg, uu, gath, grp, name):
    t, d = dx2.shape
    ff = grp.rows["g"] * N_DEV
    tm = _row_tile(t, 512)
    fc = MXU_TILE
    n_t = t // tm

    def body(dx2_ref, gg_ref, uu_ref, gath_ref, dgg_ref, duu_ref, dwd_ref, w_d, acc, sems):
        @pl.when(pl.program_id(0) == 0)
        def _():
            _load_weights(gath_ref, grp, ["d"], [w_d], sems)
            _zero(acc)

        dx2b = dx2_ref[...].astype(BF16)
        for c in range(ff // fc):
            fs = slice(c * fc, (c + 1) * fc)
            df = _dot_nt(dx2b, w_d[fs, :]).astype(BF16)
            g = gg_ref[:, fs]
            u = uu_ref[:, fs]
            sg = _sigmoid(g)
            silu = g * sg
            acc[fs, :] += _dot_tn(silu * u, dx2b)
            duu_ref[:, fs] = df * silu
            dgg_ref[:, fs] = df * u * (sg * (1.0 + g * (1.0 - sg)))

        @pl.when(pl.program_id(0) == n_t - 1)
        def _():
            w_d[...] = acc[...].astype(BF16)
            out = pltpu.make_async_copy(w_d, dwd_ref, sems.at[0])
            out.start()
            out.wait()

    row = pl.BlockSpec((tm, d), lambda i: (i, 0))
    wide = pl.BlockSpec((tm, ff), lambda i: (i, 0))
    sd = jax.ShapeDtypeStruct
    return pl.pallas_call(
        body, name=name, grid=(n_t,),
        out_shape=[sd((t, ff), BF16), sd((t, ff), BF16), sd((ff, d), BF16)],
        in_specs=[row, wide, wide, ANY],
        out_specs=[wide, wide, ANY],
        scratch_shapes=[pltpu.VMEM((ff, d), BF16), pltpu.VMEM((ff, d), F32), pltpu.SemaphoreType.DMA((N_DEV,))],
        compiler_params=_params(),
    )(dx2, gg, uu, gath)


def _ffn_bwd_in(dgg, duu, dx2, x1, g_row, gath, grp, name, comms=()):
    t, d = x1.shape
    ff = grp.rows["g"] * N_DEV
    tm = _row_tile(t, 512)

    def body(dgg_ref, duu_ref, dx2_ref, x_ref, g_ref, gath_ref, dx1_ref, dx1b_ref, h_ref, dg_ref, w_g, w_u, sems):
        @pl.when(pl.program_id(0) == 0)
        def _():
            _load_weights(gath_ref, grp, ["g", "u"], [w_g, w_u], sems)
            _zero(dg_ref)

        dh = _dot_nn(dgg_ref[...], w_g[...]) + _dot_nn(duu_ref[...], w_u[...])
        xf = x_ref[...]
        rstd = lax.rsqrt(jnp.mean(xf * xf, axis=-1, keepdims=True) + RMS_EPS)
        xh = xf * rstd
        g = g_ref[...]
        h_ref[...] = (xh * g).astype(BF16)
        dg_ref[0:1, :] += jnp.sum(dh * xh, axis=0, keepdims=True)
        dxh = dh * g
        dx1 = dx2_ref[...] + rstd * (dxh - xh * jnp.mean(dxh * xh, axis=-1, keepdims=True))
        dx1_ref[...] = dx1
        dx1b_ref[...] = dx1.astype(BF16)

    row = pl.BlockSpec((tm, d), lambda i: (i, 0))
    wide = pl.BlockSpec((tm, ff), lambda i: (i, 0))
    sd = jax.ShapeDtypeStruct
    return _hosted_call(
        body, comms, name=name, grid=(t // tm,),
        out_shape=[sd((t, d), F32), sd((t, d), BF16), sd((t, d), BF16), sd((SUBLANE, d), F32)],
        in_specs=[wide, wide, row, row, pl.BlockSpec((1, d), lambda i: (0, 0)), ANY],
        out_specs=[row, row, row, pl.BlockSpec((SUBLANE, d), lambda i: (0, 0))],
        scratch_shapes=[pltpu.VMEM((ff, d), BF16)] * 2 + [pltpu.SemaphoreType.DMA((2 * N_DEV,))],
        args=(dgg, duu, dx2, x1, g_row, gath))


def _merge_bwd(dx1b, oa, ob, ya, yb, p, gbias, gath, grp, name):
    t, d = oa.shape
    tm = _row_tile(t, 512)
    n_t = t // tm

    def body(dx_ref, oa_ref, ob_ref, ya_ref, yb_ref, ga_ref, gb_ref, gbias_ref, gath_ref,
             dya_ref, dyb_ref, dpg_ref, dgb_ref, dwoa_ref, dwob_ref, dwo_ref,
             w_oa, w_ob, w_o, acc_oa, acc_ob, acc_o, sems):
        @pl.when(pl.program_id(0) == 0)
        def _():
            _load_weights(gath_ref, grp, ["oa", "ob", "o"], [w_oa, w_ob, w_o], sems)
            for ref in (dgb_ref, acc_oa, acc_ob, acc_o):
                _zero(ref)

        dxb = dx_ref[...]
        dm = _dot_nt(dxb, w_o[...]).astype(BF16)
        oa = oa_ref[...]
        ob = ob_ref[...]
        sa = _sigmoid(ga_ref[...] + gbias_ref[0:1, :].astype(BF16))
        sb = _sigmoid(gb_ref[...] + gbias_ref[1:2, :].astype(BF16))
        acc_o[...] += _dot_tn(sa * oa + sb * ob, dxb)
        doa = dm * sa
        dob = dm * sb
        acc_oa[...] += _dot_tn(ya_ref[...], doa)
        acc_ob[...] += _dot_tn(yb_ref[...], dob)
        dga = dm * oa * sa * (1.0 - sa)
        dgb = dm * ob * sb * (1.0 - sb)
        dpg_ref[:, 0:d] = dga
        dpg_ref[:, d:2 * d] = dgb
        ones = jnp.ones((SUBLANE, tm), BF16)
        dgb_ref[0:1, :] += _dot_nn(ones, dga)[0:1, :]
        dgb_ref[1:2, :] += _dot_nn(ones, dgb)[0:1, :]
        dya_ref[...] = _dot_nt(doa, w_oa[...]).astype(BF16)
        dyb_ref[...] = _dot_nt(dob, w_ob[...]).astype(BF16)

        @pl.when(pl.program_id(0) == n_t - 1)
        def _():
            outs = []
            for n, (acc, stage, dst) in enumerate(((acc_oa, w_oa, dwoa_ref), (acc_ob, w_ob, dwob_ref), (acc_o, w_o, dwo_ref))):
                stage[...] = acc[...].astype(BF16)
                outs.append(pltpu.make_async_copy(stage, dst, sems.at[n]))
                outs[-1].start()
            for cp in outs:
                cp.wait()

    row = pl.BlockSpec((tm, d), lambda i: (i, 0))
    sd = jax.ShapeDtypeStruct
    return pl.pallas_call(
        body, name=name, grid=(n_t,),
        out_shape=[sd((t, d), BF16), sd((t, d), BF16), sd((t, 2 * d), BF16), sd((SUBLANE, d), F32),
                   sd((d, d), BF16), sd((d, d), BF16), sd((d, d), BF16)],
        in_specs=[row, row, row, row, row, pl.BlockSpec((tm, d), lambda i: (i, 5)), pl.BlockSpec((tm, d), lambda i: (i, 6)),
                  pl.BlockSpec((SUBLANE, d), lambda i: (0, 0)), ANY],
        out_specs=[row, row, pl.BlockSpec((tm, 2 * d), lambda i: (i, 0)), pl.BlockSpec((SUBLANE, d), lambda i: (0, 0)),
                   ANY, ANY, ANY],
        scratch_shapes=[pltpu.VMEM((d, d), BF16)] * 3 + [pltpu.VMEM((d, d), F32)] * 3 + [pltpu.SemaphoreType.DMA((3 * N_DEV,))],
        compiler_params=_params(),
    )(dx1b, oa, ob, ya, yb, p, p, gbias, gath)


DV_CONV_B_B, DV_BA, DV_BX, DV_SP, DV_CONV_A, DV_CONV_B = 0, 1, 2, 3, 4, 7
DV_ROWS = 16


def _mixer_bwd(dya, dyb, dpg, p, u_s, h_s, r_s, gi_s, gel_s, dgel_s, caw, cbw, vec, bda, bdx, name, comms=()):
    t, d = dya.shape
    tm = _time_tile(t)
    n_t = t // tm
    nk = tm // SUBLANE
    cw = min(MXU_TILE, d)
    nb = d // cw
    halo = 4 * SUBLANE
    ka, kb = CONV_A_K - 1, CONV_B_K - 1

    def body(dya_ref, dyb_ref, dpg_ref, ba_ref, ca_ref, xa_ref, xb_ref, cah_ref, xah_ref, xbh_ref,
             u_ref, h_ref, hh_ref, r_ref, gi_ref, gel_ref, dgel_ref, caw_ref, cbw_ref, vec_ref, bda_ref, bdx_ref,
             dp_ref, dv_ref, dwa_ref, dwx_ref,
             zbuf, xbuf, hbuf, dczbuf, dubuf, a2buf, a1buf, lbuf, dcz_head, du_head, a_head, lam_head):
        i = pl.program_id(0)

        @pl.when(i == 0)
        def _():
            for ref in (dv_ref, dwa_ref, dwx_ref, dcz_head, du_head, a_head, lam_head):
                _zero(ref)

        has_prev = jnp.where(i < n_t - 1, 1.0, 0.0).astype(F32)
        row = lax.broadcasted_iota(jnp.int32, (SUBLANE, cw), 0)
        dp_ref[:, 5 * d:7 * d] = dpg_ref[...]

        def colsum(v):
            return jnp.sum(v, axis=0, keepdims=True)

        for j in range(nb):
            cs = slice(j * cw, (j + 1) * cw)
            ca = ca_ref[:, cs].astype(F32)
            xa = xa_ref[:, cs].astype(F32)
            z = ca * xa
            z_before = cah_ref[:, cs].astype(F32) * xah_ref[:, cs].astype(F32) * has_prev
            _causal_fill(zbuf, z, z_before[halo - ka * SUBLANE:, :], ka, row)
            z2 = zbuf[0:tm, :]
            z1 = zbuf[SUBLANE:SUBLANE + tm, :]
            w0, w1, w2 = caw_ref[0:1, cs], caw_ref[1:2, cs], caw_ref[2:3, cs]
            cz = w0 * z2 + w1 * z1 + w2 * z
            dya = dya_ref[:, cs].astype(F32)
            dp_ref[:, 0 * d + j * cw:0 * d + (j + 1) * cw] = (dya * cz).astype(BF16)
            dcz = dya * ba_ref[:, cs].astype(F32)
            _anticausal_fill(dczbuf, dcz, dcz_head[:, cs], ka, row)
            dcz_head[:, cs] = dcz[0:ka * SUBLANE, :]
            dz = w2 * dcz + w1 * dczbuf[SUBLANE:SUBLANE + tm, :] + w0 * dczbuf[2 * SUBLANE:2 * SUBLANE + tm, :]
            dv_ref[DV_CONV_A + 0:DV_CONV_A + 1, cs] += colsum(dcz * z2)
            dv_ref[DV_CONV_A + 1:DV_CONV_A + 2, cs] += colsum(dcz * z1)
            dv_ref[DV_CONV_A + 2:DV_CONV_A + 3, cs] += colsum(dcz * z)
            dp_ref[:, 1 * d + j * cw:1 * d + (j + 1) * cw] = (dz * xa).astype(BF16)
            dp_ref[:, 2 * d + j * cw:2 * d + (j + 1) * cw] = (dz * ca).astype(BF16)
            h = h_ref[:, cs].astype(F32)
            h_before = hh_ref[:, cs].astype(F32) * has_prev
            _causal_fill(hbuf, h, h_before[halo - SUBLANE:, :], 1, row)
            h_prev = hbuf[0:tm, :]
            dyb = dyb_ref[:, cs].astype(F32)
            dp_ref[:, 4 * d + j * cw:4 * d + (j + 1) * cw] = (dyb * h * dgel_ref[:, cs].astype(F32)).astype(BF16)
            gel = gel_ref[:, cs].astype(F32)
            ub = u_ref[:, cs]
            u = ub.astype(F32)
            sp = vec_ref[3:4, cs]
            r = r_ref[:, cs].astype(F32)
            gi = gi_ref[:, cs].astype(F32)
            a, s, inv_s = _lru_decay(r, sp)
            _anticausal_fill(a2buf, a, a_head[:, cs], 1, row)
            a_head[:, cs] = a[0:SUBLANE, :]
            a1buf[...] = a2buf[SUBLANE:SUBLANE + tm, :]
            lbuf[...] = dyb * gel
            l_end, a_prod = _chain_scan(a1buf, lbuf, nk, reverse=True)
            a_inc, l_inc = _sublane_scan(a_prod, l_end, row, reverse=True)
            carry = lam_head[:, cs]
            ends = l_inc + a_inc * carry
            starts = jnp.where(row == SUBLANE - 1, carry, pltpu.roll(ends, SUBLANE - 1, 0))
            lam_head[:, cs] = jnp.broadcast_to(ends[0:1, :], (SUBLANE, cw))
            lam = (lbuf[...].reshape(nk, SUBLANE, cw) + a1buf[...].reshape(nk, SUBLANE, cw) * starts[None]).reshape(tm, cw)
            da = lam * h_prev
            iu = gi * u
            ds = lam * iu
            di = lam * s * u
            du = lam * s * gi
            dlog_a = da * a - ds * (a * a) * inv_s
            dv_ref[DV_SP:DV_SP + 1, cs] += colsum(dlog_a * r) * (-LRU_C)
            dpr = dlog_a * ((-LRU_C) * sp) * r * (1.0 - r)
            dpi = di * gi * (1.0 - gi)
            dv_ref[DV_BA:DV_BA + 1, cs] += colsum(dpr)
            dv_ref[DV_BX:DV_BX + 1, cs] += colsum(dpi)
            dprb = dpr.astype(BF16)
            dpib = dpi.astype(BF16)
            du = du + _dot_nt(dprb, bda_ref[j]) + _dot_nt(dpib, bdx_ref[j])
            dwa_ref[j] += _dot_tn(ub, dprb)
            dwx_ref[j] += _dot_tn(ub, dpib)
            xb = xb_ref[:, cs].astype(F32)
            x_before = xbh_ref[:, cs].astype(F32) * has_prev
            _causal_fill(xbuf, xb, x_before[halo - kb * SUBLANE:, :], kb, row)
            _anticausal_fill(dubuf, du, du_head[:, cs], kb, row)
            du_head[:, cs] = du[0:kb * SUBLANE, :]
            v0, v1, v2, v3 = cbw_ref[0:1, cs], cbw_ref[1:2, cs], cbw_ref[2:3, cs], cbw_ref[3:4, cs]
            dxb = (v3 * du + v2 * dubuf[SUBLANE:SUBLANE + tm, :] + v1 * dubuf[2 * SUBLANE:2 * SUBLANE + tm, :]
                   + v0 * dubuf[3 * SUBLANE:3 * SUBLANE + tm, :])
            dp_ref[:, 3 * d + j * cw:3 * d + (j + 1) * cw] = dxb.astype(BF16)
            dv_ref[DV_CONV_B_B:DV_CONV_B_B + 1, cs] += colsum(du)
            dv_ref[DV_CONV_B + 0:DV_CONV_B + 1, cs] += colsum(du * xbuf[0:tm, :])
            dv_ref[DV_CONV_B + 1:DV_CONV_B + 2, cs] += colsum(du * xbuf[SUBLANE:SUBLANE + tm, :])
            dv_ref[DV_CONV_B + 2:DV_CONV_B + 3, cs] += colsum(du * xbuf[2 * SUBLANE:2 * SUBLANE + tm, :])
            dv_ref[DV_CONV_B + 3:DV_CONV_B + 4, cs] += colsum(du * xb)

    rt = lambda i: n_t - 1 - i
    row_spec = pl.BlockSpec((tm, d), lambda i: (rt(i), 0))
    slab = lambda s: pl.BlockSpec((tm, d), lambda i, s=s: (rt(i), s))
    before = lambda s: pl.BlockSpec((halo, d), lambda i, s=s: (jnp.maximum(rt(i) * (tm // halo) - 1, 0), s))
    small = pl.BlockSpec((SUBLANE, d), lambda i: (0, 0))
    bd = pl.BlockSpec((nb, cw, cw), lambda i: (0, 0, 0))
    sd = jax.ShapeDtypeStruct
    wbuf = lambda n: pltpu.VMEM((tm + n * SUBLANE, cw), F32)
    head = lambda n: pltpu.VMEM((n * SUBLANE, d), F32)
    return _hosted_call(
        body, comms, name=name, grid=(n_t,),
        out_shape=[sd((t, N_PROJ * d), BF16), sd((DV_ROWS, d), F32), sd((nb, cw, cw), F32), sd((nb, cw, cw), F32)],
        in_specs=[row_spec, row_spec, pl.BlockSpec((tm, 2 * d), lambda i: (rt(i), 0)),
                  slab(0), slab(1), slab(2), slab(3), before(1), before(2), before(3),
                  row_spec, row_spec, before(0), row_spec, row_spec, row_spec, row_spec, small, small, small, bd, bd],
        out_specs=[pl.BlockSpec((tm, N_PROJ * d), lambda i: (rt(i), 0)), pl.BlockSpec((DV_ROWS, d), lambda i: (0, 0)), bd, bd],
        scratch_shapes=[wbuf(ka), wbuf(kb), wbuf(1), wbuf(ka), wbuf(kb), wbuf(1),
                        pltpu.VMEM((tm, cw), F32), pltpu.VMEM((tm, cw), F32), head(ka), head(kb), head(1), head(1)],
        args=(dya, dyb, dpg, p, p, p, p, p, p, p, u_s, h_s, h_s, r_s, gi_s, gel_s, dgel_s, caw, cbw, vec, bda, bdx))


def _in_proj_bwd(dp, x, dx1, g_row, gath, grp, name, comms=()):
    t, d = x.shape
    tm = _row_tile(t, 512)
    n_in = N_PROJ * d

    def body(dp_ref, x_ref, dx1_ref, g_ref, gath_ref, dx_ref, dg_ref, w_in, sems):
        @pl.when(pl.program_id(0) == 0)
        def _():
            _load_weights(gath_ref, grp, ["in"], [w_in], sems)
            _zero(dg_ref)

        dh = _dot_nn(dp_ref[:, 0:d], w_in[0:d, :])
        for k in range(1, N_PROJ):
            dh = dh + _dot_nn(dp_ref[:, k * d:(k + 1) * d], w_in[k * d:(k + 1) * d, :])
        xf = x_ref[...]
        rstd = lax.rsqrt(jnp.mean(xf * xf, axis=-1, keepdims=True) + RMS_EPS)
        xh = xf * rstd
        g = g_ref[...]
        dg_ref[0:1, :] += jnp.sum(dh * xh, axis=0, keepdims=True)
        dxh = dh * g
        dx_ref[...] = dx1_ref[...] + rstd * (dxh - xh * jnp.mean(dxh * xh, axis=-1, keepdims=True))

    row = pl.BlockSpec((tm, d), lambda i: (i, 0))
    sd = jax.ShapeDtypeStruct
    return _hosted_call(
        body, comms, name=name, grid=(t // tm,),
        out_shape=[sd((t, d), F32), sd((SUBLANE, d), F32)],
        in_specs=[pl.BlockSpec((tm, n_in), lambda i: (i, 0)), row, row, pl.BlockSpec((1, d), lambda i: (0, 0)), ANY],
        out_specs=[row, pl.BlockSpec((SUBLANE, d), lambda i: (0, 0))],
        scratch_shapes=[pltpu.VMEM((n_in, d), BF16), pltpu.SemaphoreType.DMA((N_DEV,))],
        args=(dp, x, dx1, g_row, gath))


def _weight_grad(a, b, name):
    t, m = a.shape
    n = b.shape[1]
    bt = _row_tile(t, 1024)
    bm = m
    for div in (1, 2, 4, 8):
        if m % div == 0 and (m // div) % LANE == 0 and (m // div) * n * 4 <= (12 << 20):
            bm = m // div
            break
    n_t = t // bt

    def body(a_ref, b_ref, o_ref, acc):
        k = pl.program_id(1)

        @pl.when(k == 0)
        def _():
            _zero(acc)

        acc[...] += _dot_tn(a_ref[...], b_ref[...])

        @pl.when(k == n_t - 1)
        def _():
            o_ref[...] = acc[...].astype(BF16)

    return pl.pallas_call(
        body, name=name, grid=(m // bm, n_t),
        out_shape=jax.ShapeDtypeStruct((m, n), BF16),
        in_specs=[pl.BlockSpec((bt, bm), lambda i, k: (k, i)), pl.BlockSpec((bt, n), lambda i, k: (k, 0))],
        out_specs=pl.BlockSpec((bm, n), lambda i, k: (i, 0)),
        scratch_shapes=[pltpu.VMEM((bm, n), F32)],
        compiler_params=_params(2),
    )(a, b)


def _adamw(w, g, m, v, name):
    r, c = w.shape
    tr = _fit_rows(r, c * 4)
    c1 = 1.0 - ADAM_B1 ** ADAM_STEP
    c2 = 1.0 - ADAM_B2 ** ADAM_STEP

    def body(w_ref, g_ref, m_ref, v_ref, d_ref, nm_ref, nv_ref):
        g32 = g_ref[...]
        nm = ADAM_B1 * m_ref[...] + (1.0 - ADAM_B1) * g32
        nv = ADAM_B2 * v_ref[...] + (1.0 - ADAM_B2) * (g32 * g32)
        nm_ref[...] = nm
        nv_ref[...] = nv
        d_ref[...] = -ADAM_LR * ((nm / c1) / (jnp.sqrt(nv / c2) + ADAM_EPS) + ADAM_WD * w_ref[...])

    spec = pl.BlockSpec((tr, c), lambda i: (i, 0))
    return pl.pallas_call(
        body, name=name, grid=(r // tr,),
        out_shape=[jax.ShapeDtypeStruct((r, c), F32)] * 3,
        in_specs=[spec] * 4, out_specs=[spec] * 3,
        compiler_params=_params(),
    )(w, g, m, v)


def _pad_rows(a, mult=SUBLANE):
    pad = (-a.shape[0]) % mult
    return a if pad == 0 else jnp.concatenate([a, jnp.zeros((pad,) + a.shape[1:], a.dtype)], axis=0)


REPLICATED = ("ln1_g", "conv_b_b", "lru_wa", "lru_ba", "lru_wx", "lru_bx", "lru_lambda", "ln2_g", "final_g")
SMALL_SHARDED = ("conv_a_w", "conv_b_w", "gate_bias")
MATRICES = ("w_in", "w_out_a", "w_out_b", "w_o", "w_ffn_gate", "w_ffn_up", "w_ffn_down")
ORDER = ("ln1_g", "w_in", "conv_a_w", "conv_b_w", "conv_b_b", "lru_wa", "lru_ba", "lru_wx", "lru_bx", "lru_lambda",
         "w_out_a", "w_out_b", "gate_bias", "w_o", "ln2_g", "w_ffn_gate", "w_ffn_up", "w_ffn_down", "final_g")


def kernel(x, ln1_g, w_in, conv_a_w, conv_b_w, conv_b_b, lru_wa, lru_ba, lru_wx, lru_bx, lru_lambda, w_out_a, w_out_b, gate_bias, w_o, ln2_g, w_ffn_gate, w_ffn_up, w_ffn_down, final_g, loss_target, m_ln1_g, m_w_in, m_conv_a_w, m_conv_b_w, m_conv_b_b, m_lru_wa, m_lru_ba, m_lru_wx, m_lru_bx, m_lru_lambda, m_w_out_a, m_w_out_b, m_gate_bias, m_w_o, m_ln2_g, m_w_ffn_gate, m_w_ffn_up, m_w_ffn_down, m_final_g, v_ln1_g, v_w_in, v_conv_a_w, v_conv_b_w, v_conv_b_b, v_lru_wa, v_lru_ba, v_lru_wx, v_lru_bx, v_lru_lambda, v_w_out_a, v_w_out_b, v_gate_bias, v_w_o, v_ln2_g, v_w_ffn_gate, v_w_ffn_up, v_w_ffn_down, v_final_g):
    w = dict(ln1_g=ln1_g, w_in=w_in, conv_a_w=conv_a_w, conv_b_w=conv_b_w, conv_b_b=conv_b_b, lru_wa=lru_wa,
             lru_ba=lru_ba, lru_wx=lru_wx, lru_bx=lru_bx, lru_lambda=lru_lambda, w_out_a=w_out_a, w_out_b=w_out_b,
             gate_bias=gate_bias, w_o=w_o, ln2_g=ln2_g, w_ffn_gate=w_ffn_gate, w_ffn_up=w_ffn_up,
             w_ffn_down=w_ffn_down, final_g=final_g)
    mom = dict(ln1_g=m_ln1_g, w_in=m_w_in, conv_a_w=m_conv_a_w, conv_b_w=m_conv_b_w, conv_b_b=m_conv_b_b,
               lru_wa=m_lru_wa, lru_ba=m_lru_ba, lru_wx=m_lru_wx, lru_bx=m_lru_bx, lru_lambda=m_lru_lambda,
               w_out_a=m_w_out_a, w_out_b=m_w_out_b, gate_bias=m_gate_bias, w_o=m_w_o, ln2_g=m_ln2_g,
               w_ffn_gate=m_w_ffn_gate, w_ffn_up=m_w_ffn_up, w_ffn_down=m_w_ffn_down, final_g=m_final_g)
    var = dict(ln1_g=v_ln1_g, w_in=v_w_in, conv_a_w=v_conv_a_w, conv_b_w=v_conv_b_w, conv_b_b=v_conv_b_b,
               lru_wa=v_lru_wa, lru_ba=v_lru_ba, lru_wx=v_lru_wx, lru_bx=v_lru_bx, lru_lambda=v_lru_lambda,
               w_out_a=v_w_out_a, w_out_b=v_w_out_b, gate_bias=v_gate_bias, w_o=v_w_o, ln2_g=v_ln2_g,
               w_ffn_gate=v_w_ffn_gate, w_ffn_up=v_w_ffn_up, w_ffn_down=v_w_ffn_down, final_g=v_final_g)

    _, t, d = x.shape
    n_layers = w_in.shape[0]
    ff = w_ffn_down.shape[1] * N_DEV
    dd = d // N_DEV
    hd = d // LRU_HEADS
    cw = min(MXU_TILE, d)
    nb = d // cw
    hpt = cw // hd
    grp = _groups(d, ff)
    me = 4 * lax.axis_index("x") + 2 * lax.axis_index("y") + lax.axis_index("c")
    tm_time = _time_tile(t)
    x0 = _to_tile_order(x[0], tm_time)
    target = _to_tile_order(loss_target[0], tm_time)

    packed = [{"in": jnp.swapaxes(w_in[l], 0, 1).astype(BF16),
               "rest": jnp.concatenate([w_out_a[l], w_out_b[l], w_o[l], jnp.swapaxes(w_ffn_gate[l], 0, 1),
                                        jnp.swapaxes(w_ffn_up[l], 0, 1), w_ffn_down[l]], axis=0).astype(BF16)}
              for l in range(n_layers)]
    n_small = CONV_A_K + CONV_B_K + 2
    small = _pad_rows(jnp.concatenate([conv_a_w, conv_b_w, gate_bias], axis=1).reshape(n_layers * n_small, dd))
    sp = jax.nn.softplus(-lru_lambda)
    vec = [_pad_rows(jnp.stack([conv_b_b[l], lru_ba[l], lru_bx[l], sp[l]])) for l in range(n_layers)]
    eye = jnp.eye(hpt, dtype=F32)

    def block_diag(wh):
        return jnp.einsum("jkab,kl->jkalb", wh.reshape(nb, hpt, hd, hd), eye).reshape(nb, cw, cw).astype(BF16)

    bda = [block_diag(lru_wa[l]) for l in range(n_layers)]
    bdx = [block_diag(lru_wx[l]) for l in range(n_layers)]

    gath = [dict() for _ in range(n_layers)]
    (gath[0]["in"],) = _comm_call(_Gather(packed[0]["in"]), "gather_in_0")
    saved = []
    xl = x0
    for l in range(n_layers):
        comms = [_Gather(packed[0]["rest"]), _Gather(small)] if l == 0 else []
        (p, h1b), got = _in_proj_fwd(xl, ln1_g[l][None], gath[l]["in"], grp["in"], f"in_proj_fwd_{l}", comms)
        if l == 0:
            gath[0]["rest"], small_g = got[0][0], got[1][0]
            small_full = jnp.swapaxes(small_g[:, :n_layers * n_small], 0, 1).reshape(n_layers, n_small, d)
            caw = [_pad_rows(small_full[k, 0:CONV_A_K]) for k in range(n_layers)]
            cbw = [_pad_rows(small_full[k, CONV_A_K:CONV_A_K + CONV_B_K]) for k in range(n_layers)]
            gbias = [_pad_rows(small_full[k, CONV_A_K + CONV_B_K:]) for k in range(n_layers)]
        more = l + 1 < n_layers
        (ya, yb, *kept), got = _mixer_fwd(p, caw[l], cbw[l], vec[l], bda[l], bdx[l], f"mixer_fwd_{l}",
                                          [_Gather(packed[l + 1]["in"])] if more else [])
        if more:
            ((gath[l + 1]["in"],),) = got
        x1, oa, ob = _merge_fwd(xl, ya, yb, p, gbias[l], gath[l]["rest"], grp["rest"], f"merge_fwd_{l}")
        (x2, gg, uu), got = _ffn_fwd(x1, ln2_g[l][None], gath[l]["rest"], grp["rest"], f"ffn_fwd_{l}",
                                     [_Gather(packed[l + 1]["rest"])] if more else [])
        if more:
            ((gath[l + 1]["rest"],),) = got
        saved.append(dict(x=xl, p=p, h1b=h1b, ya=ya, yb=yb, mixer=kept, x1=x1, oa=oa, ob=ob, gg=gg, uu=uu))
        xl = x2
    loss_tile, dx, dfinal = _loss_head(xl, final_g[None], target, "loss_head")
    loss = lax.psum(loss_tile[0, 0], ("x", "y", "c"))

    def heads(dwb):
        blocks = jnp.diagonal(dwb.reshape(nb, hpt, hd, hpt, hd), axis1=1, axis2=3)
        return jnp.moveaxis(blocks, 3, 1).reshape(hd, d)

    layer_names = [n for n in REPLICATED if n != "final_g"] + list(SMALL_SHARDED)

    def layer_block(k):
        return jnp.concatenate([small_grads[k][n] for n in layer_names], axis=0)

    recv = [dict() for _ in range(n_layers)]
    small_grads = [None] * n_layers
    early_all = None
    for l in reversed(range(n_layers)):
        s = saved[l]
        dgg, duu, dw_d = _ffn_bwd_act(dx, s["gg"], s["uu"], gath[l]["rest"], grp["rest"], f"ffn_bwd_act_{l}")
        comms = []
        if l == 0:
            early = [layer_block(k) for k in range(1, n_layers)] + [_pad_rows(dfinal[0:1])]
            comms = [_Gather(jnp.concatenate(early, axis=0))]
        (dx1, dx1b, h2b, dln2), got = _ffn_bwd_in(dgg, duu, dx, s["x1"], ln2_g[l][None], gath[l]["rest"], grp["rest"],
                                                  f"ffn_bwd_in_{l}", comms)
        if l == 0:
            ((early_all,),) = got
        dya, dyb, dpg, dgbias, dw_oa, dw_ob, dw_o = _merge_bwd(dx1b, s["oa"], s["ob"], s["ya"], s["yb"], s["p"], gbias[l],
                                                               gath[l]["rest"], grp["rest"], f"merge_bwd_{l}")
        rest = {"oa": dw_oa, "ob": dw_ob, "o": dw_o, "g": _weight_grad(dgg, h2b, f"dw_ffn_gate_{l}"),
                "u": _weight_grad(duu, h2b, f"dw_ffn_up_{l}"), "d": dw_d}
        (dp, dv, dwa, dwx), got = _mixer_bwd(dya, dyb, dpg, s["p"], *s["mixer"], caw[l], cbw[l], vec[l], bda[l], bdx[l],
                                             f"mixer_bwd_{l}", [_Exchange(rest, grp["rest"])])
        (recv[l]["rest"],) = got[0]
        small_grads[l] = {
            "conv_b_b": dv[DV_CONV_B_B:DV_CONV_B_B + 1], "lru_wa": heads(dwa),
            "lru_ba": dv[DV_BA:DV_BA + 1], "lru_wx": heads(dwx), "lru_bx": dv[DV_BX:DV_BX + 1],
            "lru_lambda": dv[DV_SP:DV_SP + 1] * (-jax.nn.sigmoid(-lru_lambda[l]))[None], "ln2_g": dln2[0:1],
            "conv_a_w": dv[DV_CONV_A:DV_CONV_A + CONV_A_K], "conv_b_w": dv[DV_CONV_B:DV_CONV_B + CONV_B_K],
            "gate_bias": dgbias[0:2],
        }
        dw_in = _Exchange({"in": _weight_grad(dp, s["h1b"], f"dw_in_{l}")}, grp["in"])
        (dx, dln1), got = _in_proj_bwd(dp, s["x"], dx1, ln1_g[l][None], gath[l]["in"], grp["in"], f"in_proj_bwd_{l}", [dw_in])
        ((recv[l]["in"],),) = got
        small_grads[l]["ln1_g"] = dln1[0:1]
    grad_x = _from_tile_order(dx, tm_time)[None]

    g = {}
    gsum = [{k: _sum_slots(recv[l][k], f"sum_{k}_{l}") for k in ("in", "rest")} for l in range(n_layers)]

    def part(key):
        k = "in" if key == "in" else "rest"
        o, r = grp[k].off[key], grp[k].rows[key]
        return jnp.stack([gsum[l][k][o:o + r] for l in range(n_layers)])

    g = {"w_in": jnp.swapaxes(part("in"), 1, 2), "w_out_a": part("oa"), "w_out_b": part("ob"), "w_o": part("o"),
         "w_ffn_gate": jnp.swapaxes(part("g"), 1, 2), "w_ffn_up": jnp.swapaxes(part("u"), 1, 2), "w_ffn_down": part("d")}
    (late_all,) = _comm_call(_Gather(layer_block(0)), "gather_small_grads_0")
    early_sum = _sum_slots(early_all, "sum_small_grads")
    block_rows = late_all.shape[1]
    per_layer = [_sum_slots(late_all, "sum_small_grads_0")]
    per_layer += [early_sum[(k - 1) * block_rows:k * block_rows] for k in range(1, n_layers)]
    g["final_g"] = early_sum[(n_layers - 1) * block_rows].reshape(w["final_g"].shape)
    o = 0
    for n in layer_names:
        rows = small_grads[0][n].shape[0]
        stacked = jnp.concatenate([per_layer[k][o:o + rows] for k in range(n_layers)], axis=0)
        if n in SMALL_SHARDED:
            g[n] = lax.dynamic_slice_in_dim(stacked, me * dd, dd, axis=1).reshape(n_layers, rows, dd)
        else:
            g[n] = stacked.reshape(w[n].shape)
        o += rows

    delta, new_m, new_v = {}, {}, {}
    gate_maps = ("lru_wa", "lru_wx")
    for n in MATRICES + gate_maps:
        shape = w[n].shape
        flat = lambda a: a.reshape(-1, d if n in gate_maps else shape[-1])
        dl, nm, nv = _adamw(flat(w[n]), flat(g[n]), flat(mom[n]), flat(var[n]), f"adamw_{n}")
        delta[n], new_m[n], new_v[n] = dl.reshape(shape), nm.reshape(shape), nv.reshape(shape)
    vectors = tuple(n for n in REPLICATED if n not in gate_maps)
    for group, width, name in ((vectors, d, "adamw_replicated"), (SMALL_SHARDED, dd, "adamw_vectors")):
        cat = lambda src: _pad_rows(jnp.concatenate([src[n].reshape(-1, width) for n in group], axis=0))
        dl, nm, nv = _adamw(cat(w), cat(g), cat(mom), cat(var), name)
        o = 0
        for n in group:
            rows = w[n].size // width
            delta[n], new_m[n], new_v[n] = (a[o:o + rows].reshape(w[n].shape) for a in (dl, nm, nv))
            o += rows

    return (loss, grad_x, *[g[n] for n in ORDER], *[delta[n] for n in ORDER], *[new_m[n] for n in ORDER],
            *[new_v[n] for n in ORDER])
```

```python
import math

import jax
import jax.numpy as jnp
from jax import lax
from jax.experimental import pallas as pl
from jax.experimental.pallas import tpu as pltpu

F32 = jnp.float32
BF16 = jnp.bfloat16

N_DEV = 8
N_PROJ = 7
LRU_HEADS = 16
LRU_C = 8.0
RMS_EPS = 1e-6
CONV_A_K = 3
CONV_B_K = 4
GELU_C = math.sqrt(2.0 / math.pi)
GELU_A = 0.044715

ADAM_LR = 0.001
ADAM_B1 = 0.9
ADAM_B2 = 0.999
ADAM_EPS = 1e-08
ADAM_WD = 0.01
ADAM_STEP = 10

LANE = 128
SUBLANE = 8
MXU_TILE = 256
VMEM_LIMIT = 52 << 20
MESH = pl.DeviceIdType.MESH
ANY = pl.BlockSpec(memory_space=pl.ANY)


def _dot_nn(a, b):
    return lax.dot_general(a, b, (((1,), (0,)), ((), ())), preferred_element_type=F32)


def _dot_nt(a, b):
    return lax.dot_general(a, b, (((1,), (1,)), ((), ())), preferred_element_type=F32)


def _dot_tn(a, b):
    return lax.dot_general(a, b, (((0,), (0,)), ((), ())), preferred_element_type=F32)


def _sigmoid(x):
    return 1.0 / (1.0 + jnp.exp(-x))


def _gelu_and_grad(x):
    x2 = x * x
    t = jnp.tanh(GELU_C * x * (1.0 + GELU_A * x2))
    g = 0.5 * x * (1.0 + t)
    dg = 0.5 * (1.0 + t) + 0.5 * x * (1.0 - t * t) * GELU_C * (1.0 + 3.0 * GELU_A * x2)
    return g, dg


def _zero(ref):
    ref[...] = jnp.zeros(ref.shape, ref.dtype)


def _fit_rows(r, row_bytes, budget=1 << 20):
    fits = [t for t in range(16, r + 1, 16) if r % t == 0 and t * row_bytes <= budget]
    return max(fits) if fits else r


def _row_tile(t, want):
    tm = min(want, t // 2)
    assert t % tm == 0 and tm % SUBLANE == 0, (t, tm)
    return tm


def _params(n_grid=1, **kw):
    return pltpu.CompilerParams(dimension_semantics=("arbitrary",) * n_grid, vmem_limit_bytes=VMEM_LIMIT, **kw)


class _Group:
    def __init__(self, keys, rows):
        self.keys = keys
        self.rows = dict(zip(keys, rows))
        self.off, o = {}, 0
        for k in keys:
            self.off[k] = o
            o += self.rows[k]
        self.total = o


def _groups(d, ff):
    dd, ffs = d // N_DEV, ff // N_DEV
    return {"in": _Group(("in",), (N_PROJ * dd,)),
            "rest": _Group(("oa", "ob", "o", "g", "u", "d"), (dd, dd, dd, ffs, ffs, ffs))}


def _load_weights(g_ref, grp, keys, dsts, sems):
    copies = []
    for n, (k, dst) in enumerate(zip(keys, dsts)):
        rows, off = grp.rows[k], grp.off[k]
        copies += [pltpu.make_async_copy(g_ref.at[p, pl.ds(off, rows), :], dst.at[pl.ds(p * rows, rows), :],
                                         sems.at[n * N_DEV + p]) for p in range(N_DEV)]
    for c in copies:
        c.start()
    for c in copies:
        c.wait()


def _comm_sems():
    return [pltpu.SemaphoreType.DMA((N_DEV - 1,)), pltpu.SemaphoreType.DMA((N_DEV - 1,)), pltpu.SemaphoreType.DMA]


class _Gather:
    def __init__(self, x):
        self.inputs = [x]
        self.out_shape = [jax.ShapeDtypeStruct((N_DEV,) + x.shape, x.dtype)]
        self.scratch = _comm_sems()

    def _plan(self, ins, outs, scr):
        (x_ref,), (out_ref,), (send_sems, recv_sems, local_sem) = ins, outs, scr
        mx, my, mc = lax.axis_index("x"), lax.axis_index("y"), lax.axis_index("c")
        me, sibling = (mx, my, mc), (mx, my, 1 - mc)
        chips = [(1 - mx, my), (mx, 1 - my), (1 - mx, 1 - my)]

        def slot(px, py, pc):
            return out_ref.at[4 * px + 2 * py + pc]

        def copy(k, block, to, src=None):
            return pltpu.make_async_remote_copy(
                src_ref=slot(*block) if src is None else src, dst_ref=slot(*block),
                send_sem=send_sems.at[k], recv_sem=recv_sems.at[k], device_id=to, device_id_type=MESH)

        mine = lambda: pltpu.make_async_copy(x_ref, slot(*me), local_sem)
        first = [lambda: copy(0, me, sibling, src=x_ref)]
        first += [lambda j=j, chip=chip: copy(1 + j, me, (*chip, mc), src=x_ref) for j, chip in enumerate(chips)]
        landed = [lambda j=j, chip=chip: copy(1 + j, (*chip, mc), me) for j, chip in enumerate(chips)]
        passed = [lambda j=j, chip=chip: copy(4 + j, (*chip, mc), sibling) for j, chip in enumerate(chips)]
        from_sibling = [lambda: copy(0, sibling, me)]
        from_sibling += [lambda j=j, chip=chip: copy(4 + j, (*chip, 1 - mc), me) for j, chip in enumerate(chips)]
        return mine, first, landed, passed, from_sibling

    def start(self, ins, outs, scr):
        mine, first, _, _, _ = self._plan(ins, outs, scr)
        mine().start()
        for cp in first:
            cp().start()

    def mid(self, ins, outs, scr):
        _, _, landed, passed, _ = self._plan(ins, outs, scr)
        for got, fwd in zip(landed, passed):
            got().wait_recv()
            fwd().start()

    def finish(self, ins, outs, scr):
        mine, first, _, passed, from_sibling = self._plan(ins, outs, scr)
        for cp in from_sibling:
            cp().wait_recv()
        for cp in first + passed:
            cp().wait_send()
        mine().wait()


class _Exchange:
    def __init__(self, mats, grp):
        self.grp = grp
        self.inputs = [mats[k] for k in grp.keys]
        self.out_shape = [jax.ShapeDtypeStruct((N_DEV, grp.total, self.inputs[0].shape[1]), BF16)]
        self.scratch = _comm_sems()

    def _pieces(self, g_refs, out_ref, q, dst_slot):
        out = []
        for g_ref, k in zip(g_refs, self.grp.keys):
            rows = self.grp.rows[k]
            out.append((g_ref.at[pl.ds(pl.multiple_of(q * rows, 16), rows), :],
                        out_ref.at[dst_slot, pl.ds(self.grp.off[k], rows), :]))
        return out

    def start(self, ins, outs, scr):
        (out_ref,), (send_sems, recv_sems, local_sem) = outs, scr
        mx, my, mc = lax.axis_index("x"), lax.axis_index("y"), lax.axis_index("c")
        me = 4 * mx + 2 * my + mc
        for s, t in self._pieces(ins, out_ref, me, me):
            pltpu.make_async_copy(s, t, local_sem).start()
        for k in range(1, N_DEV):
            px, py, pc = mx ^ ((k >> 2) & 1), my ^ ((k >> 1) & 1), mc ^ (k & 1)
            for s, t in self._pieces(ins, out_ref, 4 * px + 2 * py + pc, me):
                pltpu.make_async_remote_copy(src_ref=s, dst_ref=t, send_sem=send_sems.at[k - 1],
                                             recv_sem=recv_sems.at[k - 1], device_id=(px, py, pc),
                                             device_id_type=MESH).start()

    def mid(self, ins, outs, scr):
        pass

    def finish(self, ins, outs, scr):
        (out_ref,), (send_sems, recv_sems, local_sem) = outs, scr
        mx, my, mc = lax.axis_index("x"), lax.axis_index("y"), lax.axis_index("c")
        whole = out_ref.at[0]
        for k in range(1, N_DEV):
            done = pltpu.make_async_remote_copy(src_ref=whole, dst_ref=whole, send_sem=send_sems.at[k - 1],
                                                recv_sem=recv_sems.at[k - 1], device_id=(mx, my, mc),
                                                device_id_type=MESH)
            done.wait_send()
            done.wait_recv()
        pltpu.make_async_copy(whole, whole, local_sem).wait()


def _split(refs, sizes):
    out, pos = [], 0
    for n in sizes:
        out.append(refs[pos:pos + n])
        pos += n
    return out


def _hosted_call(body, comms, *, name, grid, in_specs, out_specs, out_shape, scratch_shapes, args, aliases=None):
    n_steps = grid[0]
    nc = len(comms)
    sizes = ([len(in_specs)] + [len(c.inputs) for c in comms] + [len(out_specs)] + [len(c.out_shape) for c in comms]
             + [len(scratch_shapes)] + [len(c.scratch) for c in comms])

    def hosted(*refs):
        parts = _split(refs, sizes)
        ins, c_ins = parts[0], parts[1:1 + nc]
        outs, c_outs = parts[1 + nc], parts[2 + nc:2 + 2 * nc]
        scr, c_scr = parts[2 + 2 * nc], parts[3 + 2 * nc:]
        step = pl.program_id(0)
        if comms:
            @pl.when(step == 0)
            def _():
                for c, a, b, s in zip(comms, c_ins, c_outs, c_scr):
                    c.start(a, b, s)

            @pl.when(step == max(n_steps - 2, 0))
            def _():
                for c, a, b, s in zip(comms, c_ins, c_outs, c_scr):
                    c.mid(a, b, s)

        body(*ins, *outs, *scr)
        if comms:
            @pl.when(step == n_steps - 1)
            def _():
                for c, a, b, s in zip(comms, c_ins, c_outs, c_scr):
                    c.finish(a, b, s)

    res = pl.pallas_call(
        hosted, name=name, grid=grid,
        out_shape=[*out_shape, *[o for c in comms for o in c.out_shape]],
        in_specs=[*in_specs, *[ANY for c in comms for _ in c.inputs]],
        out_specs=[*out_specs, *[ANY for c in comms for _ in c.out_shape]],
        scratch_shapes=[*scratch_shapes, *[s for c in comms for s in c.scratch]],
        input_output_aliases=aliases or {},
        compiler_params=_params(),
    )(*args, *[a for c in comms for a in c.inputs])
    main, rest = res[:len(out_specs)], res[len(out_specs):]
    return main, _split(rest, [len(c.out_shape) for c in comms])


def _comm_call(comm, name):
    def body(*refs):
        ins, outs, scr = _split(refs, [len(comm.inputs), len(comm.out_shape), len(comm.scratch)])
        comm.start(ins, outs, scr)
        comm.mid(ins, outs, scr)
        comm.finish(ins, outs, scr)

    return pl.pallas_call(
        body, name=name, out_shape=comm.out_shape, in_specs=[ANY] * len(comm.inputs),
        out_specs=[ANY] * len(comm.out_shape), scratch_shapes=comm.scratch,
    )(*comm.inputs)


def _sum_slots(x, name):
    n, r, c = x.shape
    tr = _fit_rows(r, c * 4)

    def body(x_ref, o_ref):
        acc = x_ref[0].astype(F32)
        for p in range(1, n):
            acc = acc + x_ref[p].astype(F32)
        o_ref[...] = acc

    return pl.pallas_call(
        body, name=name, grid=(r // tr,),
        out_shape=jax.ShapeDtypeStruct((r, c), F32),
        in_specs=[pl.BlockSpec((n, tr, c), lambda i: (0, i, 0))],
        out_specs=pl.BlockSpec((tr, c), lambda i: (i, 0)),
        compiler_params=_params(),
    )(x)


def _in_proj_fwd(x, g_row, gath, grp, name, comms=()):
    t, d = x.shape
    tm = _row_tile(t, 512)
    n_in = N_PROJ * d

    def body(x_ref, g_ref, gath_ref, p_ref, h_ref, w_in, sems):
        @pl.when(pl.program_id(0) == 0)
        def _():
            _load_weights(gath_ref, grp, ["in"], [w_in], sems)

        xf = x_ref[...]
        rstd = lax.rsqrt(jnp.mean(xf * xf, axis=-1, keepdims=True) + RMS_EPS)
        h = (xf * rstd * g_ref[...]).astype(BF16)
        h_ref[...] = h
        for k in range(N_PROJ):
            p_ref[:, k * d:(k + 1) * d] = _dot_nt(h, w_in[k * d:(k + 1) * d, :]).astype(BF16)

    row = pl.BlockSpec((tm, d), lambda i: (i, 0))
    return _hosted_call(
        body, comms, name=name, grid=(t // tm,),
        out_shape=[jax.ShapeDtypeStruct((t, n_in), BF16), jax.ShapeDtypeStruct((t, d), BF16)],
        in_specs=[row, pl.BlockSpec((1, d), lambda i: (0, 0)), ANY],
        out_specs=[pl.BlockSpec((tm, n_in), lambda i: (i, 0)), row],
        scratch_shapes=[pltpu.VMEM((n_in, d), BF16), pltpu.SemaphoreType.DMA((N_DEV,))],
        args=(x, g_row, gath))


def _time_tile(t):
    return _row_tile(t, 256)


def _to_tile_order(a, tm):
    t, c = a.shape
    return jnp.swapaxes(a.reshape(t // tm, SUBLANE, tm // SUBLANE, c), 1, 2).reshape(t, c)


def _from_tile_order(a, tm):
    t, c = a.shape
    return jnp.swapaxes(a.reshape(t // tm, tm // SUBLANE, SUBLANE, c), 1, 2).reshape(t, c)


def _causal_fill(buf, v, prev_tail, n, row):
    tm = v.shape[0]
    for q in range(n):
        cur = v[tm - SUBLANE * (n - q):tm - SUBLANE * (n - q - 1), :]
        prv = prev_tail[SUBLANE * q:SUBLANE * (q + 1), :]
        buf[SUBLANE * q:SUBLANE * (q + 1), :] = jnp.where(row == 0, pltpu.roll(prv, 1, 0), pltpu.roll(cur, 1, 0))
    buf[SUBLANE * n:, :] = v


def _anticausal_fill(buf, v, next_head, n, row):
    tm = v.shape[0]
    buf[0:tm, :] = v
    for q in range(n):
        cur = v[SUBLANE * q:SUBLANE * (q + 1), :]
        nxt = next_head[SUBLANE * q:SUBLANE * (q + 1), :]
        buf[tm + SUBLANE * q:tm + SUBLANE * (q + 1), :] = jnp.where(
            row == SUBLANE - 1, pltpu.roll(nxt, SUBLANE - 1, 0), pltpu.roll(cur, SUBLANE - 1, 0))


def _chain_scan(abuf, bbuf, nk, reverse):
    cw = abuf.shape[1]

    def step(n, carry):
        h, c = carry
        r0 = pl.multiple_of((nk - 1 - n if reverse else n) * SUBLANE, SUBLANE)
        ak = abuf[pl.ds(r0, SUBLANE), :]
        h = ak * h + bbuf[pl.ds(r0, SUBLANE), :]
        c = ak * c
        bbuf[pl.ds(r0, SUBLANE), :] = h
        abuf[pl.ds(r0, SUBLANE), :] = c
        return h, c

    return lax.fori_loop(0, nk, step, (jnp.zeros((SUBLANE, cw), F32), jnp.ones((SUBLANE, cw), F32)), unroll=8)


def _sublane_scan(a, b, row, reverse):
    for sh in (1, 2, 4):
        if reverse:
            m = row < SUBLANE - sh
            b = jnp.where(m, a * pltpu.roll(b, SUBLANE - sh, 0) + b, b)
            a = jnp.where(m, a * pltpu.roll(a, SUBLANE - sh, 0), a)
        else:
            m = row >= sh
            b = jnp.where(m, a * pltpu.roll(b, sh, 0) + b, b)
            a = jnp.where(m, a * pltpu.roll(a, sh, 0), a)
    return a, b


def _lru_gates(ub, bda, bdx, ba, bx, sp):
    r = _sigmoid(_dot_nn(ub, bda) + ba)
    i = _sigmoid(_dot_nn(ub, bdx) + bx)
    log_a = (-LRU_C) * r * sp
    a = jnp.exp(log_a)
    s2 = -jnp.tanh(log_a) * (1.0 + a * a)
    inv_s = lax.rsqrt(s2)
    s = jnp.where(s2 > 0.0, s2 * inv_s, 0.0)
    return r, i, a, s, inv_s


def _mixer_fwd(p, caw, cbw, vec, bda, bdx, name, comms=()):
    t = p.shape[0]
    d = p.shape[1] // N_PROJ
    tm = _time_tile(t)
    nk = tm // SUBLANE
    cw = min(MXU_TILE, d)
    nb = d // cw

    def body(ba_ref, ca_ref, xa_ref, xb_ref, gb_ref, caw_ref, cbw_ref, vec_ref, bda_ref, bdx_ref,
             ya_ref, yb_ref, u_ref, h_ref, zbuf, xbuf, abuf, bbuf, z_tail, x_tail, h_carry):
        @pl.when(pl.program_id(0) == 0)
        def _():
            _zero(z_tail)
            _zero(x_tail)
            _zero(h_carry)

        row = lax.broadcasted_iota(jnp.int32, (SUBLANE, cw), 0)
        for j in range(nb):
            cs = slice(j * cw, (j + 1) * cw)
            z = ca_ref[:, cs].astype(F32) * xa_ref[:, cs].astype(F32)
            _causal_fill(zbuf, z, z_tail[:, cs], CONV_A_K - 1, row)
            z_tail[:, cs] = z[tm - (CONV_A_K - 1) * SUBLANE:, :]
            cz = caw_ref[0:1, cs] * zbuf[0:tm, :] + caw_ref[1:2, cs] * zbuf[SUBLANE:SUBLANE + tm, :] + caw_ref[2:3, cs] * z
            ya_ref[:, cs] = (ba_ref[:, cs].astype(F32) * cz).astype(BF16)
            xb = xb_ref[:, cs].astype(F32)
            _causal_fill(xbuf, xb, x_tail[:, cs], CONV_B_K - 1, row)
            x_tail[:, cs] = xb[tm - (CONV_B_K - 1) * SUBLANE:, :]
            u = (cbw_ref[0:1, cs] * xbuf[0:tm, :] + cbw_ref[1:2, cs] * xbuf[SUBLANE:SUBLANE + tm, :]
                 + cbw_ref[2:3, cs] * xbuf[2 * SUBLANE:2 * SUBLANE + tm, :] + cbw_ref[3:4, cs] * xb + vec_ref[0:1, cs])
            ub = u.astype(BF16)
            u = ub.astype(F32)
            _, gi, a, s, _ = _lru_gates(ub, bda_ref[j], bdx_ref[j], vec_ref[1:2, cs], vec_ref[2:3, cs], vec_ref[3:4, cs])
            abuf[...] = a
            bbuf[...] = s * (gi * u)
            h_end, a_prod = _chain_scan(abuf, bbuf, nk, reverse=False)
            a_inc, h_inc = _sublane_scan(a_prod, h_end, row, reverse=False)
            carry = h_carry[:, cs]
            ends = h_inc + a_inc * carry
            starts = jnp.where(row == 0, carry, pltpu.roll(ends, 1, 0))
            h_carry[:, cs] = jnp.broadcast_to(ends[SUBLANE - 1:SUBLANE, :], (SUBLANE, cw))
            h = (bbuf[...].reshape(nk, SUBLANE, cw) + abuf[...].reshape(nk, SUBLANE, cw) * starts[None]).reshape(tm, cw)
            gel, _ = _gelu_and_grad(gb_ref[:, cs].astype(F32))
            yb_ref[:, cs] = (h * gel).astype(BF16)
            u_ref[:, cs] = ub
            h_ref[:, cs] = h.astype(BF16)

    slab = lambda s: pl.BlockSpec((tm, d), lambda i, s=s: (i, s))
    small = pl.BlockSpec((SUBLANE, d), lambda i: (0, 0))
    bd = pl.BlockSpec((nb, cw, cw), lambda i: (0, 0, 0))
    out = pl.BlockSpec((tm, d), lambda i: (i, 0))
    return _hosted_call(
        body, comms, name=name, grid=(t // tm,),
        out_shape=[jax.ShapeDtypeStruct((t, d), BF16)] * 4,
        in_specs=[slab(0), slab(1), slab(2), slab(3), slab(4), small, small, small, bd, bd],
        out_specs=[out] * 4,
        scratch_shapes=[pltpu.VMEM((tm + (CONV_A_K - 1) * SUBLANE, cw), F32), pltpu.VMEM((tm + (CONV_B_K - 1) * SUBLANE, cw), F32),
                        pltpu.VMEM((tm, cw), F32), pltpu.VMEM((tm, cw), F32),
                        pltpu.VMEM(((CONV_A_K - 1) * SUBLANE, d), F32), pltpu.VMEM(((CONV_B_K - 1) * SUBLANE, d), F32),
                        pltpu.VMEM((SUBLANE, d), F32)],
        args=(p, p, p, p, p, caw, cbw, vec, bda, bdx))


def _merge_fwd(x, ya, yb, p, gbias, gath, grp, name):
    t, d = x.shape
    tm = _row_tile(t, 512)

    def body(x_ref, ya_ref, yb_ref, ga_ref, gb_ref, gbias_ref, gath_ref, x1_ref, oa_ref, ob_ref, w_oa, w_ob, w_o, sems):
        @pl.when(pl.program_id(0) == 0)
        def _():
            _load_weights(gath_ref, grp, ["oa", "ob", "o"], [w_oa, w_ob, w_o], sems)

        oa = _dot_nn(ya_ref[...], w_oa[...]).astype(BF16)
        ob = _dot_nn(yb_ref[...], w_ob[...]).astype(BF16)
        oa_ref[...] = oa
        ob_ref[...] = ob
        sa = _sigmoid(ga_ref[...] + gbias_ref[0:1, :].astype(BF16))
        sb = _sigmoid(gb_ref[...] + gbias_ref[1:2, :].astype(BF16))
        x1_ref[...] = x_ref[...] + _dot_nn(sa * oa + sb * ob, w_o[...])

    row = pl.BlockSpec((tm, d), lambda i: (i, 0))
    return pl.pallas_call(
        body, name=name, grid=(t // tm,),
        out_shape=[jax.ShapeDtypeStruct((t, d), F32), jax.ShapeDtypeStruct((t, d), BF16), jax.ShapeDtypeStruct((t, d), BF16)],
        in_specs=[row, row, row, pl.BlockSpec((tm, d), lambda i: (i, 5)), pl.BlockSpec((tm, d), lambda i: (i, 6)),
                  pl.BlockSpec((SUBLANE, d), lambda i: (0, 0)), ANY],
        out_specs=[row, row, row],
        scratch_shapes=[pltpu.VMEM((d, d), BF16)] * 3 + [pltpu.SemaphoreType.DMA((3 * N_DEV,))],
        compiler_params=_params(),
    )(x, ya, yb, p, p, gbias, gath)


def _ffn_fwd(x1, g_row, gath, grp, name, comms=()):
    t, d = x1.shape
    ff = grp.rows["g"] * N_DEV
    tm = _row_tile(t, 512)
    fc = MXU_TILE
    assert ff % fc == 0

    def body(x_ref, g_ref, gath_ref, x2_ref, gg_ref, uu_ref, w_g, w_u, w_d, acc, sems):
        @pl.when(pl.program_id(0) == 0)
        def _():
            _load_weights(gath_ref, grp, ["g", "u", "d"], [w_g, w_u, w_d], sems)

        xf = x_ref[...]
        rstd = lax.rsqrt(jnp.mean(xf * xf, axis=-1, keepdims=True) + RMS_EPS)
        h = (xf * rstd * g_ref[...]).astype(BF16)
        acc[...] = xf
        for c in range(ff // fc):
            fs = slice(c * fc, (c + 1) * fc)
            gg = _dot_nt(h, w_g[fs, :]).astype(BF16)
            uu = _dot_nt(h, w_u[fs, :]).astype(BF16)
            gg_ref[:, fs] = gg
            uu_ref[:, fs] = uu
            acc[...] += _dot_nn(gg * _sigmoid(gg) * uu, w_d[fs, :])
        x2_ref[...] = acc[...]

    row = pl.BlockSpec((tm, d), lambda i: (i, 0))
    wide = pl.BlockSpec((tm, ff), lambda i: (i, 0))
    return _hosted_call(
        body, comms, name=name, grid=(t // tm,),
        out_shape=[jax.ShapeDtypeStruct((t, d), F32), jax.ShapeDtypeStruct((t, ff), BF16), jax.ShapeDtypeStruct((t, ff), BF16)],
        in_specs=[row, pl.BlockSpec((1, d), lambda i: (0, 0)), ANY],
        out_specs=[row, wide, wide],
        scratch_shapes=[pltpu.VMEM((ff, d), BF16)] * 3 + [pltpu.VMEM((tm, d), F32), pltpu.SemaphoreType.DMA((3 * N_DEV,))],
        args=(x1, g_row, gath))


def _loss_head(x, g_row, target, name):
    t, d = x.shape
    tm = _row_tile(t, 512)

    def body(x_ref, g_ref, tgt_ref, loss_ref, dx_ref, dg_ref):
        @pl.when(pl.program_id(0) == 0)
        def _():
            _zero(loss_ref)
            _zero(dg_ref)

        xf = x_ref[...]
        rstd = lax.rsqrt(jnp.mean(xf * xf, axis=-1, keepdims=True) + RMS_EPS)
        xh = xf * rstd
        g = g_ref[...]
        err = xh * g - tgt_ref[...]
        loss_ref[...] += 0.5 * jnp.sum(jnp.sum(err * err, axis=-1, keepdims=True), axis=0, keepdims=True) * (1.0 / d)
        dy = err * (1.0 / d)
        dg_ref[0:1, :] += jnp.sum(dy * xh, axis=0, keepdims=True)
        dxh = dy * g
        dx_ref[...] = rstd * (dxh - xh * jnp.mean(dxh * xh, axis=-1, keepdims=True))

    row = pl.BlockSpec((tm, d), lambda i: (i, 0))
    return pl.pallas_call(
        body, name=name, grid=(t // tm,),
        out_shape=[jax.ShapeDtypeStruct((SUBLANE, LANE), F32), jax.ShapeDtypeStruct((t, d), F32),
                   jax.ShapeDtypeStruct((SUBLANE, d), F32)],
        in_specs=[row, pl.BlockSpec((1, d), lambda i: (0, 0)), row],
        out_specs=[pl.BlockSpec((SUBLANE, LANE), lambda i: (0, 0)), row, pl.BlockSpec((SUBLANE, d), lambda i: (0, 0))],
        compiler_params=_params(),
    )(x, g_row, target)


def _ffn_bwd_act(dx2, gg, uu, gath, grp, name):
    t, d = dx2.shape
    ff = grp.rows["g"] * N_DEV
    tm = _row_tile(t, 512)
    fc = MXU_TILE
    n_t = t // tm

    def body(dx2_ref, gg_ref, uu_ref, gath_ref, dgg_ref, duu_ref, dwd_ref, w_d, acc, sems):
        @pl.when(pl.program_id(0) == 0)
        def _():
            _load_weights(gath_ref, grp, ["d"], [w_d], sems)
            _zero(acc)

        dx2b = dx2_ref[...].astype(BF16)
        for c in range(ff // fc):
            fs = slice(c * fc, (c + 1) * fc)
            df = _dot_nt(dx2b, w_d[fs, :]).astype(BF16)
            g = gg_ref[:, fs]
            u = uu_ref[:, fs]
            sg = _sigmoid(g)
            silu = g * sg
            acc[fs, :] += _dot_tn(silu * u, dx2b)
            duu_ref[:, fs] = df * silu
            dgg_ref[:, fs] = df * u * (sg * (1.0 + g * (1.0 - sg)))

        @pl.when(pl.program_id(0) == n_t - 1)
        def _():
            w_d[...] = acc[...].astype(BF16)
            out = pltpu.make_async_copy(w_d, dwd_ref, sems.at[0])
            out.start()
            out.wait()

    row = pl.BlockSpec((tm, d), lambda i: (i, 0))
    wide = pl.BlockSpec((tm, ff), lambda i: (i, 0))
    sd = jax.ShapeDtypeStruct
    return pl.pallas_call(
        body, name=name, grid=(n_t,),
        out_shape=[sd((t, ff), BF16), sd((t, ff), BF16), sd((ff, d), BF16)],
        in_specs=[row, wide, wide, ANY],
        out_specs=[wide, wide, ANY],
        scratch_shapes=[pltpu.VMEM((ff, d), BF16), pltpu.VMEM((ff, d), F32), pltpu.SemaphoreType.DMA((N_DEV,))],
        compiler_params=_params(),
    )(dx2, gg, uu, gath)


def _ffn_bwd_in(dgg, duu, dx2, x1, g_row, gath, grp, name, comms=()):
    t, d = x1.shape
    ff = grp.rows["g"] * N_DEV
    tm = _row_tile(t, 512)

    def body(dgg_ref, duu_ref, dx2_ref, x_ref, g_ref, gath_ref, dx1_ref, dx1b_ref, h_ref, dg_ref, w_g, w_u, sems):
        @pl.when(pl.program_id(0) == 0)
        def _():
            _load_weights(gath_ref, grp, ["g", "u"], [w_g, w_u], sems)
            _zero(dg_ref)

        dh = _dot_nn(dgg_ref[...], w_g[...]) + _dot_nn(duu_ref[...], w_u[...])
        xf = x_ref[...]
        rstd = lax.rsqrt(jnp.mean(xf * xf, axis=-1, keepdims=True) + RMS_EPS)
        xh = xf * rstd
        g = g_ref[...]
        h_ref[...] = (xh * g).astype(BF16)
        dg_ref[0:1, :] += jnp.sum(dh * xh, axis=0, keepdims=True)
        dxh = dh * g
        dx1 = dx2_ref[...] + rstd * (dxh - xh * jnp.mean(dxh * xh, axis=-1, keepdims=True))
        dx1_ref[...] = dx1
        dx1b_ref[...] = dx1.astype(BF16)

    row = pl.BlockSpec((tm, d), lambda i: (i, 0))
    wide = pl.BlockSpec((tm, ff), lambda i: (i, 0))
    sd = jax.ShapeDtypeStruct
    return _hosted_call(
        body, comms, name=name, grid=(t // tm,),
        out_shape=[sd((t, d), F32), sd((t, d), BF16), sd((t, d), BF16), sd((SUBLANE, d), F32)],
        in_specs=[wide, wide, row, row, pl.BlockSpec((1, d), lambda i: (0, 0)), ANY],
        out_specs=[row, row, row, pl.BlockSpec((SUBLANE, d), lambda i: (0, 0))],
        scratch_shapes=[pltpu.VMEM((ff, d), BF16)] * 2 + [pltpu.SemaphoreType.DMA((2 * N_DEV,))],
        args=(dgg, duu, dx2, x1, g_row, gath))


def _merge_bwd(dx1b, oa, ob, ya, yb, p, gbias, gath, grp, name):
    t, d = oa.shape
    tm = _row_tile(t, 512)
    n_t = t // tm

    def body(dx_ref, oa_ref, ob_ref, ya_ref, yb_ref, ga_ref, gb_ref, gbias_ref, gath_ref,
             dya_ref, dyb_ref, dp_ref, dgb_ref, dwoa_ref, dwob_ref, dwo_ref,
             w_oa, w_ob, w_o, acc_oa, acc_ob, acc_o, stage, sems, out_sems):
        @pl.when(pl.program_id(0) == 0)
        def _():
            _load_weights(gath_ref, grp, ["oa", "ob", "o"], [w_oa, w_ob, w_o], sems)
            for ref in (dgb_ref, acc_oa, acc_ob, acc_o):
                _zero(ref)

        dxb = dx_ref[...]
        dm = _dot_nt(dxb, w_o[...]).astype(BF16)
        oa = oa_ref[...]
        ob = ob_ref[...]
        sa = _sigmoid(ga_ref[...] + gbias_ref[0:1, :].astype(BF16))
        sb = _sigmoid(gb_ref[...] + gbias_ref[1:2, :].astype(BF16))
        acc_o[...] += _dot_tn(sa * oa + sb * ob, dxb)
        doa = dm * sa
        dob = dm * sb
        acc_oa[...] += _dot_tn(ya_ref[...], doa)
        acc_ob[...] += _dot_tn(yb_ref[...], dob)
        dga = dm * oa * sa * (1.0 - sa)
        dgb = dm * ob * sb * (1.0 - sb)
        step = pl.program_id(0)
        slot = step % 2

        def to_dp(k, at_step):
            return pltpu.make_async_copy(stage.at[k], dp_ref.at[pl.ds(at_step * tm, tm), pl.ds(5 * d, 2 * d)], out_sems.at[k])

        @pl.when(step >= 2)
        def _():
            to_dp(slot, step - 2).wait()

        stage[slot, :, 0:d] = dga
        stage[slot, :, d:2 * d] = dgb
        to_dp(slot, step).start()
        ones = jnp.ones((SUBLANE, tm), BF16)
        dgb_ref[0:1, :] += _dot_nn(ones, dga)[0:1, :]
        dgb_ref[1:2, :] += _dot_nn(ones, dgb)[0:1, :]
        dya_ref[...] = _dot_nt(doa, w_oa[...]).astype(BF16)
        dyb_ref[...] = _dot_nt(dob, w_ob[...]).astype(BF16)

        @pl.when(pl.program_id(0) == n_t - 1)
        def _():
            outs = []
            for n, (acc, stage, dst) in enumerate(((acc_oa, w_oa, dwoa_ref), (acc_ob, w_ob, dwob_ref), (acc_o, w_o, dwo_ref))):
                stage[...] = acc[...].astype(BF16)
                outs.append(pltpu.make_async_copy(stage, dst, sems.at[n]))
                outs[-1].start()
            for cp in outs:
                cp.wait()
            for back in range(min(2, n_t)):
                to_dp((n_t - 1 - back) % 2, n_t - 1 - back).wait()

    row = pl.BlockSpec((tm, d), lambda i: (i, 0))
    sd = jax.ShapeDtypeStruct
    return pl.pallas_call(
        body, name=name, grid=(n_t,),
        out_shape=[sd((t, d), BF16), sd((t, d), BF16), sd((t, N_PROJ * d), BF16), sd((SUBLANE, d), F32),
                   sd((d, d), BF16), sd((d, d), BF16), sd((d, d), BF16)],
        in_specs=[row, row, row, row, row, pl.BlockSpec((tm, d), lambda i: (i, 5)), pl.BlockSpec((tm, d), lambda i: (i, 6)),
                  pl.BlockSpec((SUBLANE, d), lambda i: (0, 0)), ANY],
        out_specs=[row, row, ANY, pl.BlockSpec((SUBLANE, d), lambda i: (0, 0)), ANY, ANY, ANY],
        scratch_shapes=[pltpu.VMEM((d, d), BF16)] * 3 + [pltpu.VMEM((d, d), F32)] * 3
        + [pltpu.VMEM((2, tm, 2 * d), BF16), pltpu.SemaphoreType.DMA((3 * N_DEV,)), pltpu.SemaphoreType.DMA((2,))],
        compiler_params=_params(),
    )(dx1b, oa, ob, ya, yb, p, p, gbias, gath)


DV_CONV_B_B, DV_BA, DV_BX, DV_SP, DV_CONV_A, DV_CONV_B = 0, 1, 2, 3, 4, 7
DV_ROWS = 16


def _mixer_bwd(dya, dyb, dp_gates, p, u_s, h_s, caw, cbw, vec, bda, bdx, name, comms=()):
    t, d = dya.shape
    tm = _time_tile(t)
    n_t = t // tm
    nk = tm // SUBLANE
    cw = min(MXU_TILE, d)
    nb = d // cw
    halo = 4 * SUBLANE
    ka, kb = CONV_A_K - 1, CONV_B_K - 1

    def body(dya_ref, dyb_ref, _, ba_ref, ca_ref, xa_ref, xb_ref, gb_ref, cah_ref, xah_ref, xbh_ref,
             u_ref, h_ref, hh_ref, caw_ref, cbw_ref, vec_ref, bda_ref, bdx_ref,
             dp_ref, dv_ref, dwa_ref, dwx_ref,
             zbuf, xbuf, hbuf, dczbuf, dubuf, a2buf, a1buf, lbuf, dcz_head, du_head, a_head, lam_head):
        i = pl.program_id(0)

        @pl.when(i == 0)
        def _():
            for ref in (dv_ref, dwa_ref, dwx_ref, dcz_head, du_head, a_head, lam_head):
                _zero(ref)

        has_prev = jnp.where(i < n_t - 1, 1.0, 0.0).astype(F32)
        row = lax.broadcasted_iota(jnp.int32, (SUBLANE, cw), 0)

        def colsum(v):
            return jnp.sum(v, axis=0, keepdims=True)

        for j in range(nb):
            cs = slice(j * cw, (j + 1) * cw)
            ca = ca_ref[:, cs].astype(F32)
            xa = xa_ref[:, cs].astype(F32)
            z = ca * xa
            z_before = cah_ref[:, cs].astype(F32) * xah_ref[:, cs].astype(F32) * has_prev
            _causal_fill(zbuf, z, z_before[halo - ka * SUBLANE:, :], ka, row)
            z2 = zbuf[0:tm, :]
            z1 = zbuf[SUBLANE:SUBLANE + tm, :]
            w0, w1, w2 = caw_ref[0:1, cs], caw_ref[1:2, cs], caw_ref[2:3, cs]
            cz = w0 * z2 + w1 * z1 + w2 * z
            dya = dya_ref[:, cs].astype(F32)
            dp_ref[:, 0 * d + j * cw:0 * d + (j + 1) * cw] = (dya * cz).astype(BF16)
            dcz = dya * ba_ref[:, cs].astype(F32)
            _anticausal_fill(dczbuf, dcz, dcz_head[:, cs], ka, row)
            dcz_head[:, cs] = dcz[0:ka * SUBLANE, :]
            dz = w2 * dcz + w1 * dczbuf[SUBLANE:SUBLANE + tm, :] + w0 * dczbuf[2 * SUBLANE:2 * SUBLANE + tm, :]
            dv_ref[DV_CONV_A + 0:DV_CONV_A + 1, cs] += colsum(dcz * z2)
            dv_ref[DV_CONV_A + 1:DV_CONV_A + 2, cs] += colsum(dcz * z1)
            dv_ref[DV_CONV_A + 2:DV_CONV_A + 3, cs] += colsum(dcz * z)
            dp_ref[:, 1 * d + j * cw:1 * d + (j + 1) * cw] = (dz * xa).astype(BF16)
            dp_ref[:, 2 * d + j * cw:2 * d + (j + 1) * cw] = (dz * ca).astype(BF16)
            h = h_ref[:, cs].astype(F32)
            h_before = hh_ref[:, cs].astype(F32) * has_prev
            _causal_fill(hbuf, h, h_before[halo - SUBLANE:, :], 1, row)
            h_prev = hbuf[0:tm, :]
            dyb = dyb_ref[:, cs].astype(F32)
            gel, dgel = _gelu_and_grad(gb_ref[:, cs].astype(F32))
            dp_ref[:, 4 * d + j * cw:4 * d + (j + 1) * cw] = (dyb * h * dgel).astype(BF16)
            ub = u_ref[:, cs]
            u = ub.astype(F32)
            sp = vec_ref[3:4, cs]
            r, gi, a, s, inv_s = _lru_gates(ub, bda_ref[j], bdx_ref[j], vec_ref[1:2, cs], vec_ref[2:3, cs], sp)
            _anticausal_fill(a2buf, a, a_head[:, cs], 1, row)
            a_head[:, cs] = a[0:SUBLANE, :]
            a1buf[...] = a2buf[SUBLANE:SUBLANE + tm, :]
            lbuf[...] = dyb * gel
            l_end, a_prod = _chain_scan(a1buf, lbuf, nk, reverse=True)
            a_inc, l_inc = _sublane_scan(a_prod, l_end, row, reverse=True)
            carry = lam_head[:, cs]
            ends = l_inc + a_inc * carry
            starts = jnp.where(row == SUBLANE - 1, carry, pltpu.roll(ends, SUBLANE - 1, 0))
            lam_head[:, cs] = jnp.broadcast_to(ends[0:1, :], (SUBLANE, cw))
            lam = (lbuf[...].reshape(nk, SUBLANE, cw) + a1buf[...].reshape(nk, SUBLANE, cw) * starts[None]).reshape(tm, cw)
            da = lam * h_prev
            iu = gi * u
            ds = lam * iu
            di = lam * s * u
            du = lam * s * gi
            dlog_a = da * a - ds * (a * a) * inv_s
            dv_ref[DV_SP:DV_SP + 1, cs] += colsum(dlog_a * r) * (-LRU_C)
            dpr = dlog_a * ((-LRU_C) * sp) * r * (1.0 - r)
            dpi = di * gi * (1.0 - gi)
            dv_ref[DV_BA:DV_BA + 1, cs] += colsum(dpr)
            dv_ref[DV_BX:DV_BX + 1, cs] += colsum(dpi)
            dprb = dpr.astype(BF16)
            dpib = dpi.astype(BF16)
            du = du + _dot_nt(dprb, bda_ref[j]) + _dot_nt(dpib, bdx_ref[j])
            dwa_ref[j] += _dot_tn(ub, dprb)
            dwx_ref[j] += _dot_tn(ub, dpib)
            xb = xb_ref[:, cs].astype(F32)
            x_before = xbh_ref[:, cs].astype(F32) * has_prev
            _causal_fill(xbuf, xb, x_before[halo - kb * SUBLANE:, :], kb, row)
            _anticausal_fill(dubuf, du, du_head[:, cs], kb, row)
            du_head[:, cs] = du[0:kb * SUBLANE, :]
            v0, v1, v2, v3 = cbw_ref[0:1, cs], cbw_ref[1:2, cs], cbw_ref[2:3, cs], cbw_ref[3:4, cs]
            dxb = (v3 * du + v2 * dubuf[SUBLANE:SUBLANE + tm, :] + v1 * dubuf[2 * SUBLANE:2 * SUBLANE + tm, :]
                   + v0 * dubuf[3 * SUBLANE:3 * SUBLANE + tm, :])
            dp_ref[:, 3 * d + j * cw:3 * d + (j + 1) * cw] = dxb.astype(BF16)
            dv_ref[DV_CONV_B_B:DV_CONV_B_B + 1, cs] += colsum(du)
            dv_ref[DV_CONV_B + 0:DV_CONV_B + 1, cs] += colsum(du * xbuf[0:tm, :])
            dv_ref[DV_CONV_B + 1:DV_CONV_B + 2, cs] += colsum(du * xbuf[SUBLANE:SUBLANE + tm, :])
            dv_ref[DV_CONV_B + 2:DV_CONV_B + 3, cs] += colsum(du * xbuf[2 * SUBLANE:2 * SUBLANE + tm, :])
            dv_ref[DV_CONV_B + 3:DV_CONV_B + 4, cs] += colsum(du * xb)

    rt = lambda i: n_t - 1 - i
    row_spec = pl.BlockSpec((tm, d), lambda i: (rt(i), 0))
    slab = lambda s: pl.BlockSpec((tm, d), lambda i, s=s: (rt(i), s))
    before = lambda s: pl.BlockSpec((halo, d), lambda i, s=s: (jnp.maximum(rt(i) * (tm // halo) - 1, 0), s))
    small = pl.BlockSpec((SUBLANE, d), lambda i: (0, 0))
    bd = pl.BlockSpec((nb, cw, cw), lambda i: (0, 0, 0))
    sd = jax.ShapeDtypeStruct
    wbuf = lambda n: pltpu.VMEM((tm + n * SUBLANE, cw), F32)
    head = lambda n: pltpu.VMEM((n * SUBLANE, d), F32)
    return _hosted_call(
        body, comms, name=name, grid=(n_t,),
        out_shape=[sd((t, N_PROJ * d), BF16), sd((DV_ROWS, d), F32), sd((nb, cw, cw), F32), sd((nb, cw, cw), F32)],
        in_specs=[row_spec, row_spec, ANY,
                  slab(0), slab(1), slab(2), slab(3), slab(4), before(1), before(2), before(3),
                  row_spec, row_spec, before(0), small, small, small, bd, bd],
        out_specs=[pl.BlockSpec((tm, 5 * d), lambda i: (rt(i), 0)), pl.BlockSpec((DV_ROWS, d), lambda i: (0, 0)), bd, bd],
        aliases={2: 0},
        scratch_shapes=[wbuf(ka), wbuf(kb), wbuf(1), wbuf(ka), wbuf(kb), wbuf(1),
                        pltpu.VMEM((tm, cw), F32), pltpu.VMEM((tm, cw), F32), head(ka), head(kb), head(1), head(1)],
        args=(dya, dyb, dp_gates, p, p, p, p, p, p, p, p, u_s, h_s, h_s, caw, cbw, vec, bda, bdx))


def _in_proj_bwd(dp, x, dx1, g_row, gath, grp, name, comms=()):
    t, d = x.shape
    tm = _row_tile(t, 512)
    n_in = N_PROJ * d

    def body(dp_ref, x_ref, dx1_ref, g_ref, gath_ref, dx_ref, dg_ref, w_in, sems):
        @pl.when(pl.program_id(0) == 0)
        def _():
            _load_weights(gath_ref, grp, ["in"], [w_in], sems)
            _zero(dg_ref)

        dh = _dot_nn(dp_ref[:, 0:d], w_in[0:d, :])
        for k in range(1, N_PROJ):
            dh = dh + _dot_nn(dp_ref[:, k * d:(k + 1) * d], w_in[k * d:(k + 1) * d, :])
        xf = x_ref[...]
        rstd = lax.rsqrt(jnp.mean(xf * xf, axis=-1, keepdims=True) + RMS_EPS)
        xh = xf * rstd
        g = g_ref[...]
        dg_ref[0:1, :] += jnp.sum(dh * xh, axis=0, keepdims=True)
        dxh = dh * g
        dx_ref[...] = dx1_ref[...] + rstd * (dxh - xh * jnp.mean(dxh * xh, axis=-1, keepdims=True))

    row = pl.BlockSpec((tm, d), lambda i: (i, 0))
    sd = jax.ShapeDtypeStruct
    return _hosted_call(
        body, comms, name=name, grid=(t // tm,),
        out_shape=[sd((t, d), F32), sd((SUBLANE, d), F32)],
        in_specs=[pl.BlockSpec((tm, n_in), lambda i: (i, 0)), row, row, pl.BlockSpec((1, d), lambda i: (0, 0)), ANY],
        out_specs=[row, pl.BlockSpec((SUBLANE, d), lambda i: (0, 0))],
        scratch_shapes=[pltpu.VMEM((n_in, d), BF16), pltpu.SemaphoreType.DMA((N_DEV,))],
        args=(dp, x, dx1, g_row, gath))


def _weight_grad(a, b, name):
    t, m = a.shape
    n = b.shape[1]
    bt = _row_tile(t, 1024)
    bm = m
    for div in (1, 2, 4, 8):
        if m % div == 0 and (m // div) % LANE == 0 and (m // div) * n * 4 <= (12 << 20):
            bm = m // div
            break
    n_t = t // bt

    def body(a_ref, b_ref, o_ref, acc):
        k = pl.program_id(1)

        @pl.when(k == 0)
        def _():
            _zero(acc)

        acc[...] += _dot_tn(a_ref[...], b_ref[...])

        @pl.when(k == n_t - 1)
        def _():
            o_ref[...] = acc[...].astype(BF16)

    return pl.pallas_call(
        body, name=name, grid=(m // bm, n_t),
        out_shape=jax.ShapeDtypeStruct((m, n), BF16),
        in_specs=[pl.BlockSpec((bt, bm), lambda i, k: (k, i)), pl.BlockSpec((bt, n), lambda i, k: (k, 0))],
        out_specs=pl.BlockSpec((bm, n), lambda i, k: (i, 0)),
        scratch_shapes=[pltpu.VMEM((bm, n), F32)],
        compiler_params=_params(2),
    )(a, b)


def _adamw(w, g, m, v, name):
    r, c = w.shape
    tr = _fit_rows(r, c * 4)
    c1 = 1.0 - ADAM_B1 ** ADAM_STEP
    c2 = 1.0 - ADAM_B2 ** ADAM_STEP

    def body(w_ref, g_ref, m_ref, v_ref, d_ref, nm_ref, nv_ref):
        g32 = g_ref[...]
        nm = ADAM_B1 * m_ref[...] + (1.0 - ADAM_B1) * g32
        nv = ADAM_B2 * v_ref[...] + (1.0 - ADAM_B2) * (g32 * g32)
        nm_ref[...] = nm
        nv_ref[...] = nv
        d_ref[...] = -ADAM_LR * ((nm / c1) / (jnp.sqrt(nv / c2) + ADAM_EPS) + ADAM_WD * w_ref[...])

    spec = pl.BlockSpec((tr, c), lambda i: (i, 0))
    return pl.pallas_call(
        body, name=name, grid=(r // tr,),
        out_shape=[jax.ShapeDtypeStruct((r, c), F32)] * 3,
        in_specs=[spec] * 4, out_specs=[spec] * 3,
        compiler_params=_params(),
    )(w, g, m, v)


def _pad_rows(a, mult=SUBLANE):
    pad = (-a.shape[0]) % mult
    return a if pad == 0 else jnp.concatenate([a, jnp.zeros((pad,) + a.shape[1:], a.dtype)], axis=0)


REPLICATED = ("ln1_g", "conv_b_b", "lru_wa", "lru_ba", "lru_wx", "lru_bx", "lru_lambda", "ln2_g", "final_g")
SMALL_SHARDED = ("conv_a_w", "conv_b_w", "gate_bias")
MATRICES = ("w_in", "w_out_a", "w_out_b", "w_o", "w_ffn_gate", "w_ffn_up", "w_ffn_down")
ORDER = ("ln1_g", "w_in", "conv_a_w", "conv_b_w", "conv_b_b", "lru_wa", "lru_ba", "lru_wx", "lru_bx", "lru_lambda",
         "w_out_a", "w_out_b", "gate_bias", "w_o", "ln2_g", "w_ffn_gate", "w_ffn_up", "w_ffn_down", "final_g")


def kernel(x, ln1_g, w_in, conv_a_w, conv_b_w, conv_b_b, lru_wa, lru_ba, lru_wx, lru_bx, lru_lambda, w_out_a, w_out_b, gate_bias, w_o, ln2_g, w_ffn_gate, w_ffn_up, w_ffn_down, final_g, loss_target, m_ln1_g, m_w_in, m_conv_a_w, m_conv_b_w, m_conv_b_b, m_lru_wa, m_lru_ba, m_lru_wx, m_lru_bx, m_lru_lambda, m_w_out_a, m_w_out_b, m_gate_bias, m_w_o, m_ln2_g, m_w_ffn_gate, m_w_ffn_up, m_w_ffn_down, m_final_g, v_ln1_g, v_w_in, v_conv_a_w, v_conv_b_w, v_conv_b_b, v_lru_wa, v_lru_ba, v_lru_wx, v_lru_bx, v_lru_lambda, v_w_out_a, v_w_out_b, v_gate_bias, v_w_o, v_ln2_g, v_w_ffn_gate, v_w_ffn_up, v_w_ffn_down, v_final_g):
    w = dict(ln1_g=ln1_g, w_in=w_in, conv_a_w=conv_a_w, conv_b_w=conv_b_w, conv_b_b=conv_b_b, lru_wa=lru_wa,
             lru_ba=lru_ba, lru_wx=lru_wx, lru_bx=lru_bx, lru_lambda=lru_lambda, w_out_a=w_out_a, w_out_b=w_out_b,
             gate_bias=gate_bias, w_o=w_o, ln2_g=ln2_g, w_ffn_gate=w_ffn_gate, w_ffn_up=w_ffn_up,
             w_ffn_down=w_ffn_down, final_g=final_g)
    mom = dict(ln1_g=m_ln1_g, w_in=m_w_in, conv_a_w=m_conv_a_w, conv_b_w=m_conv_b_w, conv_b_b=m_conv_b_b,
               lru_wa=m_lru_wa, lru_ba=m_lru_ba, lru_wx=m_lru_wx, lru_bx=m_lru_bx, lru_lambda=m_lru_lambda,
               w_out_a=m_w_out_a, w_out_b=m_w_out_b, gate_bias=m_gate_bias, w_o=m_w_o, ln2_g=m_ln2_g,
               w_ffn_gate=m_w_ffn_gate, w_ffn_up=m_w_ffn_up, w_ffn_down=m_w_ffn_down, final_g=m_final_g)
    var = dict(ln1_g=v_ln1_g, w_in=v_w_in, conv_a_w=v_conv_a_w, conv_b_w=v_conv_b_w, conv_b_b=v_conv_b_b,
               lru_wa=v_lru_wa, lru_ba=v_lru_ba, lru_wx=v_lru_wx, lru_bx=v_lru_bx, lru_lambda=v_lru_lambda,
               w_out_a=v_w_out_a, w_out_b=v_w_out_b, gate_bias=v_gate_bias, w_o=v_w_o, ln2_g=v_ln2_g,
               w_ffn_gate=v_w_ffn_gate, w_ffn_up=v_w_ffn_up, w_ffn_down=v_w_ffn_down, final_g=v_final_g)

    _, t, d = x.shape
    n_layers = w_in.shape[0]
    ff = w_ffn_down.shape[1] * N_DEV
    dd = d // N_DEV
    hd = d // LRU_HEADS
    cw = min(MXU_TILE, d)
    nb = d // cw
    hpt = cw // hd
    grp = _groups(d, ff)
    me = 4 * lax.axis_index("x") + 2 * lax.axis_index("y") + lax.axis_index("c")
    tm_time = _time_tile(t)
    x0 = _to_tile_order(x[0], tm_time)
    target = _to_tile_order(loss_target[0], tm_time)

    packed = [{"in": jnp.swapaxes(w_in[l], 0, 1).astype(BF16),
               "rest": jnp.concatenate([w_out_a[l], w_out_b[l], w_o[l], jnp.swapaxes(w_ffn_gate[l], 0, 1),
                                        jnp.swapaxes(w_ffn_up[l], 0, 1), w_ffn_down[l]], axis=0).astype(BF16)}
              for l in range(n_layers)]
    n_small = CONV_A_K + CONV_B_K + 2
    small = _pad_rows(jnp.concatenate([conv_a_w, conv_b_w, gate_bias], axis=1).reshape(n_layers * n_small, dd))
    sp = jax.nn.softplus(-lru_lambda)
    vec = [_pad_rows(jnp.stack([conv_b_b[l], lru_ba[l], lru_bx[l], sp[l]])) for l in range(n_layers)]
    eye = jnp.eye(hpt, dtype=F32)

    def block_diag(wh):
        return jnp.einsum("jkab,kl->jkalb", wh.reshape(nb, hpt, hd, hd), eye).reshape(nb, cw, cw).astype(BF16)

    bda = [block_diag(lru_wa[l]) for l in range(n_layers)]
    bdx = [block_diag(lru_wx[l]) for l in range(n_layers)]

    gath = [dict() for _ in range(n_layers)]
    (gath[0]["in"],) = _comm_call(_Gather(packed[0]["in"]), "gather_in_0")
    saved = []
    xl = x0
    for l in range(n_layers):
        comms = [_Gather(packed[0]["rest"]), _Gather(small)] if l == 0 else []
        (p, h1b), got = _in_proj_fwd(xl, ln1_g[l][None], gath[l]["in"], grp["in"], f"in_proj_fwd_{l}", comms)
        if l == 0:
            gath[0]["rest"], small_g = got[0][0], got[1][0]
            small_full = jnp.swapaxes(small_g[:, :n_layers * n_small], 0, 1).reshape(n_layers, n_small, d)
            caw = [_pad_rows(small_full[k, 0:CONV_A_K]) for k in range(n_layers)]
            cbw = [_pad_rows(small_full[k, CONV_A_K:CONV_A_K + CONV_B_K]) for k in range(n_layers)]
            gbias = [_pad_rows(small_full[k, CONV_A_K + CONV_B_K:]) for k in range(n_layers)]
        more = l + 1 < n_layers
        (ya, yb, *kept), got = _mixer_fwd(p, caw[l], cbw[l], vec[l], bda[l], bdx[l], f"mixer_fwd_{l}",
                                          [_Gather(packed[l + 1]["in"])] if more else [])
        if more:
            ((gath[l + 1]["in"],),) = got
        x1, oa, ob = _merge_fwd(xl, ya, yb, p, gbias[l], gath[l]["rest"], grp["rest"], f"merge_fwd_{l}")
        (x2, gg, uu), got = _ffn_fwd(x1, ln2_g[l][None], gath[l]["rest"], grp["rest"], f"ffn_fwd_{l}",
                                     [_Gather(packed[l + 1]["rest"])] if more else [])
        if more:
            ((gath[l + 1]["rest"],),) = got
        saved.append(dict(x=xl, p=p, h1b=h1b, ya=ya, yb=yb, mixer=kept, x1=x1, oa=oa, ob=ob, gg=gg, uu=uu))
        xl = x2
    loss_tile, dx, dfinal = _loss_head(xl, final_g[None], target, "loss_head")
    loss = lax.psum(loss_tile[0, 0], ("x", "y", "c"))

    def heads(dwb):
        blocks = jnp.diagonal(dwb.reshape(nb, hpt, hd, hpt, hd), axis1=1, axis2=3)
        return jnp.moveaxis(blocks, 3, 1).reshape(hd, d)

    layer_names = [n for n in REPLICATED if n != "final_g"] + list(SMALL_SHARDED)

    def layer_block(k):
        return jnp.concatenate([small_grads[k][n] for n in layer_names], axis=0)

    recv = [dict() for _ in range(n_layers)]
    small_grads = [None] * n_layers
    early_all = None
    for l in reversed(range(n_layers)):
        s = saved[l]
        dgg, duu, dw_d = _ffn_bwd_act(dx, s["gg"], s["uu"], gath[l]["rest"], grp["rest"], f"ffn_bwd_act_{l}")
        comms = []
        if l == 0:
            early = [layer_block(k) for k in range(1, n_layers)] + [_pad_rows(dfinal[0:1])]
            comms = [_Gather(jnp.concatenate(early, axis=0))]
        (dx1, dx1b, h2b, dln2), got = _ffn_bwd_in(dgg, duu, dx, s["x1"], ln2_g[l][None], gath[l]["rest"], grp["rest"],
                                                  f"ffn_bwd_in_{l}", comms)
        if l == 0:
            ((early_all,),) = got
        dya, dyb, dp_gates, dgbias, dw_oa, dw_ob, dw_o = _merge_bwd(dx1b, s["oa"], s["ob"], s["ya"], s["yb"], s["p"], gbias[l],
                                                               gath[l]["rest"], grp["rest"], f"merge_bwd_{l}")
        rest = {"oa": dw_oa, "ob": dw_ob, "o": dw_o, "g": _weight_grad(dgg, h2b, f"dw_ffn_gate_{l}"),
                "u": _weight_grad(duu, h2b, f"dw_ffn_up_{l}"), "d": dw_d}
        (dp, dv, dwa, dwx), got = _mixer_bwd(dya, dyb, dp_gates, s["p"], *s["mixer"], caw[l], cbw[l], vec[l], bda[l], bdx[l],
                                             f"mixer_bwd_{l}", [_Exchange(rest, grp["rest"])])
        (recv[l]["rest"],) = got[0]
        small_grads[l] = {
            "conv_b_b": dv[DV_CONV_B_B:DV_CONV_B_B + 1], "lru_wa": heads(dwa),
            "lru_ba": dv[DV_BA:DV_BA + 1], "lru_wx": heads(dwx), "lru_bx": dv[DV_BX:DV_BX + 1],
            "lru_lambda": dv[DV_SP:DV_SP + 1] * (-jax.nn.sigmoid(-lru_lambda[l]))[None], "ln2_g": dln2[0:1],
            "conv_a_w": dv[DV_CONV_A:DV_CONV_A + CONV_A_K], "conv_b_w": dv[DV_CONV_B:DV_CONV_B + CONV_B_K],
            "gate_bias": dgbias[0:2],
        }
        dw_in = _Exchange({"in": _weight_grad(dp, s["h1b"], f"dw_in_{l}")}, grp["in"])
        (dx, dln1), got = _in_proj_bwd(dp, s["x"], dx1, ln1_g[l][None], gath[l]["in"], grp["in"], f"in_proj_bwd_{l}", [dw_in])
        ((recv[l]["in"],),) = got
        small_grads[l]["ln1_g"] = dln1[0:1]
    grad_x = _from_tile_order(dx, tm_time)[None]

    g = {}
    gsum = [{k: _sum_slots(recv[l][k], f"sum_{k}_{l}") for k in ("in", "rest")} for l in range(n_layers)]

    def part(key):
        k = "in" if key == "in" else "rest"
        o, r = grp[k].off[key], grp[k].rows[key]
        return jnp.stack([gsum[l][k][o:o + r] for l in range(n_layers)])

    g = {"w_in": jnp.swapaxes(part("in"), 1, 2), "w_out_a": part("oa"), "w_out_b": part("ob"), "w_o": part("o"),
         "w_ffn_gate": jnp.swapaxes(part("g"), 1, 2), "w_ffn_up": jnp.swapaxes(part("u"), 1, 2), "w_ffn_down": part("d")}
    (late_all,) = _comm_call(_Gather(layer_block(0).astype(BF16)), "gather_small_grads_0")
    early_sum = _sum_slots(early_all, "sum_small_grads")
    block_rows = late_all.shape[1]
    per_layer = [_sum_slots(late_all, "sum_small_grads_0")]
    per_layer += [early_sum[(k - 1) * block_rows:k * block_rows] for k in range(1, n_layers)]
    g["final_g"] = early_sum[(n_layers - 1) * block_rows].reshape(w["final_g"].shape)
    o = 0
    for n in layer_names:
        rows = small_grads[0][n].shape[0]
        stacked = jnp.concatenate([per_layer[k][o:o + rows] for k in range(n_layers)], axis=0)
        if n in SMALL_SHARDED:
            g[n] = lax.dynamic_slice_in_dim(stacked, me * dd, dd, axis=1).reshape(n_layers, rows, dd)
        else:
            g[n] = stacked.reshape(w[n].shape)
        o += rows

    delta, new_m, new_v = {}, {}, {}
    gate_maps = ("lru_wa", "lru_wx")
    for n in MATRICES + gate_maps:
        shape = w[n].shape
        flat = lambda a: a.reshape(-1, d if n in gate_maps else shape[-1])
        dl, nm, nv = _adamw(flat(w[n]), flat(g[n]), flat(mom[n]), flat(var[n]), f"adamw_{n}")
        delta[n], new_m[n], new_v[n] = dl.reshape(shape), nm.reshape(shape), nv.reshape(shape)
    vectors = tuple(n for n in REPLICATED if n not in gate_maps)
    for group, width, name in ((vectors, d, "adamw_replicated"), (SMALL_SHARDED, dd, "adamw_vectors")):
        cat = lambda src: _pad_rows(jnp.concatenate([src[n].reshape(-1, width) for n in group], axis=0))
        dl, nm, nv = _adamw(cat(w), cat(g), cat(mom), cat(var), name)
        o = 0
        for n in group:
            rows = w[n].size // width
            delta[n], new_m[n], new_v[n] = (a[o:o + rows].reshape(w[n].shape) for a in (dl, nm, nv))
            o += rows

    return (loss, grad_x, *[g[n] for n in ORDER], *[delta[n] for n in ORDER], *[new_m[n] for n in ORDER],
            *[new_v[n] for n in ORDER])
```

```python
import math

import jax
import jax.numpy as jnp
from jax import lax
from jax.experimental import pallas as pl
from jax.experimental.pallas import tpu as pltpu

F32 = jnp.float32
BF16 = jnp.bfloat16

N_DEV = 8
N_PROJ = 7
LRU_HEADS = 16
LRU_C = 8.0
RMS_EPS = 1e-6
CONV_A_K = 3
CONV_B_K = 4
GELU_C = math.sqrt(2.0 / math.pi)
GELU_A = 0.044715

ADAM_LR = 0.001
ADAM_B1 = 0.9
ADAM_B2 = 0.999
ADAM_EPS = 1e-08
ADAM_WD = 0.01
ADAM_STEP = 10

LANE = 128
SUBLANE = 8
MXU_TILE = 256
VMEM_LIMIT = 52 << 20
MESH = pl.DeviceIdType.MESH
ANY = pl.BlockSpec(memory_space=pl.ANY)


def _dot_nn(a, b):
    return lax.dot_general(a, b, (((1,), (0,)), ((), ())), preferred_element_type=F32)


def _dot_nt(a, b):
    return lax.dot_general(a, b, (((1,), (1,)), ((), ())), preferred_element_type=F32)


def _dot_tn(a, b):
    return lax.dot_general(a, b, (((0,), (0,)), ((), ())), preferred_element_type=F32)


def _sigmoid(x):
    return 1.0 / (1.0 + jnp.exp(-x))


def _gelu_and_grad(x):
    x2 = x * x
    t = jnp.tanh(GELU_C * x * (1.0 + GELU_A * x2))
    g = 0.5 * x * (1.0 + t)
    dg = 0.5 * (1.0 + t) + 0.5 * x * (1.0 - t * t) * GELU_C * (1.0 + 3.0 * GELU_A * x2)
    return g, dg


def _zero(ref):
    ref[...] = jnp.zeros(ref.shape, ref.dtype)


def _fit_rows(r, row_bytes, budget=1 << 20):
    fits = [t for t in range(16, r + 1, 16) if r % t == 0 and t * row_bytes <= budget]
    return max(fits) if fits else r


def _row_tile(t, want):
    tm = min(want, t // 2)
    assert t % tm == 0 and tm % SUBLANE == 0, (t, tm)
    return tm


def _params(n_grid=1, **kw):
    return pltpu.CompilerParams(dimension_semantics=("arbitrary",) * n_grid, vmem_limit_bytes=VMEM_LIMIT, **kw)


class _Group:
    def __init__(self, keys, rows):
        self.keys = keys
        self.rows = dict(zip(keys, rows))
        self.off, o = {}, 0
        for k in keys:
            self.off[k] = o
            o += self.rows[k]
        self.total = o


def _groups(d, ff):
    dd, ffs = d // N_DEV, ff // N_DEV
    return {"in": _Group(("in",), (N_PROJ * dd,)),
            "rest": _Group(("oa", "ob", "o", "g", "u", "d"), (dd, dd, dd, ffs, ffs, ffs))}


def _load_weights(g_ref, grp, keys, dsts, sems):
    copies = []
    for n, (k, dst) in enumerate(zip(keys, dsts)):
        rows, off = grp.rows[k], grp.off[k]
        copies += [pltpu.make_async_copy(g_ref.at[p, pl.ds(off, rows), :], dst.at[pl.ds(p * rows, rows), :],
                                         sems.at[n * N_DEV + p]) for p in range(N_DEV)]
    for c in copies:
        c.start()
    for c in copies:
        c.wait()


def _comm_sems():
    return [pltpu.SemaphoreType.DMA((N_DEV - 1,)), pltpu.SemaphoreType.DMA((N_DEV - 1,)), pltpu.SemaphoreType.DMA]


class _Gather:
    def __init__(self, x):
        self.inputs = [x]
        self.out_shape = [jax.ShapeDtypeStruct((N_DEV,) + x.shape, x.dtype)]
        self.scratch = _comm_sems()

    def _plan(self, ins, outs, scr):
        (x_ref,), (out_ref,), (send_sems, recv_sems, local_sem) = ins, outs, scr
        mx, my, mc = lax.axis_index("x"), lax.axis_index("y"), lax.axis_index("c")
        me, sibling = (mx, my, mc), (mx, my, 1 - mc)
        chips = [(1 - mx, my), (mx, 1 - my), (1 - mx, 1 - my)]

        def slot(px, py, pc):
            return out_ref.at[4 * px + 2 * py + pc]

        def copy(k, block, to, src=None):
            return pltpu.make_async_remote_copy(
                src_ref=slot(*block) if src is None else src, dst_ref=slot(*block),
                send_sem=send_sems.at[k], recv_sem=recv_sems.at[k], device_id=to, device_id_type=MESH)

        mine = lambda: pltpu.make_async_copy(x_ref, slot(*me), local_sem)
        first = [lambda: copy(0, me, sibling, src=x_ref)]
        first += [lambda j=j, chip=chip: copy(1 + j, me, (*chip, mc), src=x_ref) for j, chip in enumerate(chips)]
        landed = [lambda j=j, chip=chip: copy(1 + j, (*chip, mc), me) for j, chip in enumerate(chips)]
        passed = [lambda j=j, chip=chip: copy(4 + j, (*chip, mc), sibling) for j, chip in enumerate(chips)]
        from_sibling = [lambda: copy(0, sibling, me)]
        from_sibling += [lambda j=j, chip=chip: copy(4 + j, (*chip, 1 - mc), me) for j, chip in enumerate(chips)]
        return mine, first, landed, passed, from_sibling

    def start(self, ins, outs, scr):
        mine, first, _, _, _ = self._plan(ins, outs, scr)
        mine().start()
        for cp in first:
            cp().start()

    def mid(self, ins, outs, scr):
        _, _, landed, passed, _ = self._plan(ins, outs, scr)
        for got, fwd in zip(landed, passed):
            got().wait_recv()
            fwd().start()

    def finish(self, ins, outs, scr):
        mine, first, _, passed, from_sibling = self._plan(ins, outs, scr)
        for cp in from_sibling:
            cp().wait_recv()
        for cp in first + passed:
            cp().wait_send()
        mine().wait()


class _Exchange:
    def __init__(self, mats, grp):
        self.grp = grp
        self.inputs = [mats[k] for k in grp.keys]
        self.out_shape = [jax.ShapeDtypeStruct((N_DEV, grp.total, self.inputs[0].shape[1]), BF16)]
        self.scratch = _comm_sems()

    def _pieces(self, g_refs, out_ref, q, dst_slot):
        out = []
        for g_ref, k in zip(g_refs, self.grp.keys):
            rows = self.grp.rows[k]
            out.append((g_ref.at[pl.ds(pl.multiple_of(q * rows, 16), rows), :],
                        out_ref.at[dst_slot, pl.ds(self.grp.off[k], rows), :]))
        return out

    def start(self, ins, outs, scr):
        (out_ref,), (send_sems, recv_sems, local_sem) = outs, scr
        mx, my, mc = lax.axis_index("x"), lax.axis_index("y"), lax.axis_index("c")
        me = 4 * mx + 2 * my + mc
        for s, t in self._pieces(ins, out_ref, me, me):
            pltpu.make_async_copy(s, t, local_sem).start()
        for k in range(1, N_DEV):
            px, py, pc = mx ^ ((k >> 2) & 1), my ^ ((k >> 1) & 1), mc ^ (k & 1)
            for s, t in self._pieces(ins, out_ref, 4 * px + 2 * py + pc, me):
                pltpu.make_async_remote_copy(src_ref=s, dst_ref=t, send_sem=send_sems.at[k - 1],
                                             recv_sem=recv_sems.at[k - 1], device_id=(px, py, pc),
                                             device_id_type=MESH).start()

    def mid(self, ins, outs, scr):
        pass

    def finish(self, ins, outs, scr):
        (out_ref,), (send_sems, recv_sems, local_sem) = outs, scr
        mx, my, mc = lax.axis_index("x"), lax.axis_index("y"), lax.axis_index("c")
        whole = out_ref.at[0]
        for k in range(1, N_DEV):
            done = pltpu.make_async_remote_copy(src_ref=whole, dst_ref=whole, send_sem=send_sems.at[k - 1],
                                                recv_sem=recv_sems.at[k - 1], device_id=(mx, my, mc),
                                                device_id_type=MESH)
            done.wait_send()
            done.wait_recv()
        pltpu.make_async_copy(whole, whole, local_sem).wait()


def _split(refs, sizes):
    out, pos = [], 0
    for n in sizes:
        out.append(refs[pos:pos + n])
        pos += n
    return out


def _hosted_call(body, comms, *, name, grid, in_specs, out_specs, out_shape, scratch_shapes, args, aliases=None):
    n_steps = grid[0]
    nc = len(comms)
    sizes = ([len(in_specs)] + [len(c.inputs) for c in comms] + [len(out_specs)] + [len(c.out_shape) for c in comms]
             + [len(scratch_shapes)] + [len(c.scratch) for c in comms])

    def hosted(*refs):
        parts = _split(refs, sizes)
        ins, c_ins = parts[0], parts[1:1 + nc]
        outs, c_outs = parts[1 + nc], parts[2 + nc:2 + 2 * nc]
        scr, c_scr = parts[2 + 2 * nc], parts[3 + 2 * nc:]
        step = pl.program_id(0)
        if comms:
            @pl.when(step == 0)
            def _():
                for c, a, b, s in zip(comms, c_ins, c_outs, c_scr):
                    c.start(a, b, s)

            @pl.when(step == max(n_steps - 2, 0))
            def _():
                for c, a, b, s in zip(comms, c_ins, c_outs, c_scr):
                    c.mid(a, b, s)

        body(*ins, *outs, *scr)
        if comms:
            @pl.when(step == n_steps - 1)
            def _():
                for c, a, b, s in zip(comms, c_ins, c_outs, c_scr):
                    c.finish(a, b, s)

    res = pl.pallas_call(
        hosted, name=name, grid=grid,
        out_shape=[*out_shape, *[o for c in comms for o in c.out_shape]],
        in_specs=[*in_specs, *[ANY for c in comms for _ in c.inputs]],
        out_specs=[*out_specs, *[ANY for c in comms for _ in c.out_shape]],
        scratch_shapes=[*scratch_shapes, *[s for c in comms for s in c.scratch]],
        input_output_aliases=aliases or {},
        compiler_params=_params(),
    )(*args, *[a for c in comms for a in c.inputs])
    main, rest = res[:len(out_specs)], res[len(out_specs):]
    return main, _split(rest, [len(c.out_shape) for c in comms])


def _comm_call(comm, name):
    def body(*refs):
        ins, outs, scr = _split(refs, [len(comm.inputs), len(comm.out_shape), len(comm.scratch)])
        comm.start(ins, outs, scr)
        comm.mid(ins, outs, scr)
        comm.finish(ins, outs, scr)

    return pl.pallas_call(
        body, name=name, out_shape=comm.out_shape, in_specs=[ANY] * len(comm.inputs),
        out_specs=[ANY] * len(comm.out_shape), scratch_shapes=comm.scratch,
    )(*comm.inputs)


def _sum_slots(x, name):
    n, r, c = x.shape
    tr = _fit_rows(r, c * 4)

    def body(x_ref, o_ref):
        acc = x_ref[0].astype(F32)
        for p in range(1, n):
            acc = acc + x_ref[p].astype(F32)
        o_ref[...] = acc

    return pl.pallas_call(
        body, name=name, grid=(r // tr,),
        out_shape=jax.ShapeDtypeStruct((r, c), F32),
        in_specs=[pl.BlockSpec((n, tr, c), lambda i: (0, i, 0))],
        out_specs=pl.BlockSpec((tr, c), lambda i: (i, 0)),
        compiler_params=_params(),
    )(x)


def _in_proj_fwd(x, g_row, gath, grp, name, comms=()):
    t, d = x.shape
    tm = _row_tile(t, 512)
    n_in = N_PROJ * d

    def body(x_ref, g_ref, gath_ref, p_ref, h_ref, w_in, sems):
        @pl.when(pl.program_id(0) == 0)
        def _():
            _load_weights(gath_ref, grp, ["in"], [w_in], sems)

        xf = x_ref[...]
        rstd = lax.rsqrt(jnp.mean(xf * xf, axis=-1, keepdims=True) + RMS_EPS)
        h = (xf * rstd * g_ref[...]).astype(BF16)
        h_ref[...] = h
        for k in range(N_PROJ):
            p_ref[:, k * d:(k + 1) * d] = _dot_nt(h, w_in[k * d:(k + 1) * d, :]).astype(BF16)

    row = pl.BlockSpec((tm, d), lambda i: (i, 0))
    return _hosted_call(
        body, comms, name=name, grid=(t // tm,),
        out_shape=[jax.ShapeDtypeStruct((t, n_in), BF16), jax.ShapeDtypeStruct((t, d), BF16)],
        in_specs=[row, pl.BlockSpec((1, d), lambda i: (0, 0)), ANY],
        out_specs=[pl.BlockSpec((tm, n_in), lambda i: (i, 0)), row],
        scratch_shapes=[pltpu.VMEM((n_in, d), BF16), pltpu.SemaphoreType.DMA((N_DEV,))],
        args=(x, g_row, gath))


def _time_tile(t):
    return _row_tile(t, 256)


def _to_tile_order(a, tm):
    t, c = a.shape
    return jnp.swapaxes(a.reshape(t // tm, SUBLANE, tm // SUBLANE, c), 1, 2).reshape(t, c)


def _from_tile_order(a, tm):
    t, c = a.shape
    return jnp.swapaxes(a.reshape(t // tm, tm // SUBLANE, SUBLANE, c), 1, 2).reshape(t, c)


def _causal_fill(buf, v, prev_tail, n, row):
    tm = v.shape[0]
    for q in range(n):
        cur = v[tm - SUBLANE * (n - q):tm - SUBLANE * (n - q - 1), :]
        prv = prev_tail[SUBLANE * q:SUBLANE * (q + 1), :]
        buf[SUBLANE * q:SUBLANE * (q + 1), :] = jnp.where(row == 0, pltpu.roll(prv, 1, 0), pltpu.roll(cur, 1, 0))
    buf[SUBLANE * n:, :] = v


def _anticausal_fill(buf, v, next_head, n, row):
    tm = v.shape[0]
    buf[0:tm, :] = v
    for q in range(n):
        cur = v[SUBLANE * q:SUBLANE * (q + 1), :]
        nxt = next_head[SUBLANE * q:SUBLANE * (q + 1), :]
        buf[tm + SUBLANE * q:tm + SUBLANE * (q + 1), :] = jnp.where(
            row == SUBLANE - 1, pltpu.roll(nxt, SUBLANE - 1, 0), pltpu.roll(cur, SUBLANE - 1, 0))


def _chain_scan(abuf, bbuf, nk, reverse):
    cw = abuf.shape[1]

    def step(n, carry):
        h, c = carry
        r0 = pl.multiple_of((nk - 1 - n if reverse else n) * SUBLANE, SUBLANE)
        ak = abuf[pl.ds(r0, SUBLANE), :]
        h = ak * h + bbuf[pl.ds(r0, SUBLANE), :]
        c = ak * c
        bbuf[pl.ds(r0, SUBLANE), :] = h
        abuf[pl.ds(r0, SUBLANE), :] = c
        return h, c

    return lax.fori_loop(0, nk, step, (jnp.zeros((SUBLANE, cw), F32), jnp.ones((SUBLANE, cw), F32)), unroll=True)


def _sublane_scan(a, b, row, reverse):
    for sh in (1, 2, 4):
        if reverse:
            m = row < SUBLANE - sh
            b = jnp.where(m, a * pltpu.roll(b, SUBLANE - sh, 0) + b, b)
            a = jnp.where(m, a * pltpu.roll(a, SUBLANE - sh, 0), a)
        else:
            m = row >= sh
            b = jnp.where(m, a * pltpu.roll(b, sh, 0) + b, b)
            a = jnp.where(m, a * pltpu.roll(a, sh, 0), a)
    return a, b


def _lru_gates(ub, bda, bdx, ba, bx, sp):
    r = _sigmoid(_dot_nn(ub, bda) + ba)
    i = _sigmoid(_dot_nn(ub, bdx) + bx)
    log_a = (-LRU_C) * r * sp
    a = jnp.exp(log_a)
    s2 = -jnp.tanh(log_a) * (1.0 + a * a)
    inv_s = lax.rsqrt(s2)
    s = jnp.where(s2 > 0.0, s2 * inv_s, 0.0)
    return r, i, a, s, inv_s


def _mixer_fwd(p, caw, cbw, vec, bda, bdx, name, comms=()):
    t = p.shape[0]
    d = p.shape[1] // N_PROJ
    tm = _time_tile(t)
    nk = tm // SUBLANE
    cw = min(MXU_TILE, d)
    nb = d // cw

    def body(ba_ref, ca_ref, xa_ref, xb_ref, gb_ref, caw_ref, cbw_ref, vec_ref, bda_ref, bdx_ref,
             ya_ref, yb_ref, u_ref, h_ref, zbuf, xbuf, abuf, bbuf, z_tail, x_tail, h_carry):
        @pl.when(pl.program_id(0) == 0)
        def _():
            _zero(z_tail)
            _zero(x_tail)
            _zero(h_carry)

        row = lax.broadcasted_iota(jnp.int32, (SUBLANE, cw), 0)
        for j in range(nb):
            cs = slice(j * cw, (j + 1) * cw)
            z = ca_ref[:, cs].astype(F32) * xa_ref[:, cs].astype(F32)
            _causal_fill(zbuf, z, z_tail[:, cs], CONV_A_K - 1, row)
            z_tail[:, cs] = z[tm - (CONV_A_K - 1) * SUBLANE:, :]
            cz = caw_ref[0:1, cs] * zbuf[0:tm, :] + caw_ref[1:2, cs] * zbuf[SUBLANE:SUBLANE + tm, :] + caw_ref[2:3, cs] * z
            ya_ref[:, cs] = (ba_ref[:, cs].astype(F32) * cz).astype(BF16)
            xb = xb_ref[:, cs].astype(F32)
            _causal_fill(xbuf, xb, x_tail[:, cs], CONV_B_K - 1, row)
            x_tail[:, cs] = xb[tm - (CONV_B_K - 1) * SUBLANE:, :]
            u = (cbw_ref[0:1, cs] * xbuf[0:tm, :] + cbw_ref[1:2, cs] * xbuf[SUBLANE:SUBLANE + tm, :]
                 + cbw_ref[2:3, cs] * xbuf[2 * SUBLANE:2 * SUBLANE + tm, :] + cbw_ref[3:4, cs] * xb + vec_ref[0:1, cs])
            ub = u.astype(BF16)
            u = ub.astype(F32)
            _, gi, a, s, _ = _lru_gates(ub, bda_ref[j], bdx_ref[j], vec_ref[1:2, cs], vec_ref[2:3, cs], vec_ref[3:4, cs])
            abuf[...] = a
            bbuf[...] = s * (gi * u)
            h_end, a_prod = _chain_scan(abuf, bbuf, nk, reverse=False)
            a_inc, h_inc = _sublane_scan(a_prod, h_end, row, reverse=False)
            carry = h_carry[:, cs]
            ends = h_inc + a_inc * carry
            starts = jnp.where(row == 0, carry, pltpu.roll(ends, 1, 0))
            h_carry[:, cs] = jnp.broadcast_to(ends[SUBLANE - 1:SUBLANE, :], (SUBLANE, cw))
            h = (bbuf[...].reshape(nk, SUBLANE, cw) + abuf[...].reshape(nk, SUBLANE, cw) * starts[None]).reshape(tm, cw)
            gel, _ = _gelu_and_grad(gb_ref[:, cs].astype(F32))
            yb_ref[:, cs] = (h * gel).astype(BF16)
            u_ref[:, cs] = ub
            h_ref[:, cs] = h.astype(BF16)

    slab = lambda s: pl.BlockSpec((tm, d), lambda i, s=s: (i, s))
    small = pl.BlockSpec((SUBLANE, d), lambda i: (0, 0))
    bd = pl.BlockSpec((nb, cw, cw), lambda i: (0, 0, 0))
    out = pl.BlockSpec((tm, d), lambda i: (i, 0))
    return _hosted_call(
        body, comms, name=name, grid=(t // tm,),
        out_shape=[jax.ShapeDtypeStruct((t, d), BF16)] * 4,
        in_specs=[slab(0), slab(1), slab(2), slab(3), slab(4), small, small, small, bd, bd],
        out_specs=[out] * 4,
        scratch_shapes=[pltpu.VMEM((tm + (CONV_A_K - 1) * SUBLANE, cw), F32), pltpu.VMEM((tm + (CONV_B_K - 1) * SUBLANE, cw), F32),
                        pltpu.VMEM((tm, cw), F32), pltpu.VMEM((tm, cw), F32),
                        pltpu.VMEM(((CONV_A_K - 1) * SUBLANE, d), F32), pltpu.VMEM(((CONV_B_K - 1) * SUBLANE, d), F32),
                        pltpu.VMEM((SUBLANE, d), F32)],
        args=(p, p, p, p, p, caw, cbw, vec, bda, bdx))


def _merge_fwd(x, ya, yb, p, gbias, gath, grp, name):
    t, d = x.shape
    tm = _row_tile(t, 512)

    def body(x_ref, ya_ref, yb_ref, ga_ref, gb_ref, gbias_ref, gath_ref, x1_ref, oa_ref, ob_ref, w_oa, w_ob, w_o, sems):
        @pl.when(pl.program_id(0) == 0)
        def _():
            _load_weights(gath_ref, grp, ["oa", "ob", "o"], [w_oa, w_ob, w_o], sems)

        oa = _dot_nn(ya_ref[...], w_oa[...]).astype(BF16)
        ob = _dot_nn(yb_ref[...], w_ob[...]).astype(BF16)
        oa_ref[...] = oa
        ob_ref[...] = ob
        sa = _sigmoid(ga_ref[...] + gbias_ref[0:1, :].astype(BF16))
        sb = _sigmoid(gb_ref[...] + gbias_ref[1:2, :].astype(BF16))
        x1_ref[...] = x_ref[...] + _dot_nn(sa * oa + sb * ob, w_o[...])

    row = pl.BlockSpec((tm, d), lambda i: (i, 0))
    return pl.pallas_call(
        body, name=name, grid=(t // tm,),
        out_shape=[jax.ShapeDtypeStruct((t, d), F32), jax.ShapeDtypeStruct((t, d), BF16), jax.ShapeDtypeStruct((t, d), BF16)],
        in_specs=[row, row, row, pl.BlockSpec((tm, d), lambda i: (i, 5)), pl.BlockSpec((tm, d), lambda i: (i, 6)),
                  pl.BlockSpec((SUBLANE, d), lambda i: (0, 0)), ANY],
        out_specs=[row, row, row],
        scratch_shapes=[pltpu.VMEM((d, d), BF16)] * 3 + [pltpu.SemaphoreType.DMA((3 * N_DEV,))],
        compiler_params=_params(),
    )(x, ya, yb, p, p, gbias, gath)


def _ffn_fwd(x1, g_row, gath, grp, name, comms=()):
    t, d = x1.shape
    ff = grp.rows["g"] * N_DEV
    tm = _row_tile(t, 512)
    fc = MXU_TILE
    assert ff % fc == 0

    def body(x_ref, g_ref, gath_ref, x2_ref, gg_ref, uu_ref, w_g, w_u, w_d, acc, sems):
        @pl.when(pl.program_id(0) == 0)
        def _():
            _load_weights(gath_ref, grp, ["g", "u", "d"], [w_g, w_u, w_d], sems)

        xf = x_ref[...]
        rstd = lax.rsqrt(jnp.mean(xf * xf, axis=-1, keepdims=True) + RMS_EPS)
        h = (xf * rstd * g_ref[...]).astype(BF16)
        acc[...] = xf
        for c in range(ff // fc):
            fs = slice(c * fc, (c + 1) * fc)
            gg = _dot_nt(h, w_g[fs, :]).astype(BF16)
            uu = _dot_nt(h, w_u[fs, :]).astype(BF16)
            gg_ref[:, fs] = gg
            uu_ref[:, fs] = uu
            acc[...] += _dot_nn(gg * _sigmoid(gg) * uu, w_d[fs, :])
        x2_ref[...] = acc[...]

    row = pl.BlockSpec((tm, d), lambda i: (i, 0))
    wide = pl.BlockSpec((tm, ff), lambda i: (i, 0))
    return _hosted_call(
        body, comms, name=name, grid=(t // tm,),
        out_shape=[jax.ShapeDtypeStruct((t, d), F32), jax.ShapeDtypeStruct((t, ff), BF16), jax.ShapeDtypeStruct((t, ff), BF16)],
        in_specs=[row, pl.BlockSpec((1, d), lambda i: (0, 0)), ANY],
        out_specs=[row, wide, wide],
        scratch_shapes=[pltpu.VMEM((ff, d), BF16)] * 3 + [pltpu.VMEM((tm, d), F32), pltpu.SemaphoreType.DMA((3 * N_DEV,))],
        args=(x1, g_row, gath))


def _loss_head(x, g_row, target, name):
    t, d = x.shape
    tm = _row_tile(t, 512)

    def body(x_ref, g_ref, tgt_ref, loss_ref, dx_ref, dg_ref):
        @pl.when(pl.program_id(0) == 0)
        def _():
            _zero(loss_ref)
            _zero(dg_ref)

        xf = x_ref[...]
        rstd = lax.rsqrt(jnp.mean(xf * xf, axis=-1, keepdims=True) + RMS_EPS)
        xh = xf * rstd
        g = g_ref[...]
        err = xh * g - tgt_ref[...]
        loss_ref[...] += 0.5 * jnp.sum(jnp.sum(err * err, axis=-1, keepdims=True), axis=0, keepdims=True) * (1.0 / d)
        dy = err * (1.0 / d)
        dg_ref[0:1, :] += jnp.sum(dy * xh, axis=0, keepdims=True)
        dxh = dy * g
        dx_ref[...] = rstd * (dxh - xh * jnp.mean(dxh * xh, axis=-1, keepdims=True))

    row = pl.BlockSpec((tm, d), lambda i: (i, 0))
    return pl.pallas_call(
        body, name=name, grid=(t // tm,),
        out_shape=[jax.ShapeDtypeStruct((SUBLANE, LANE), F32), jax.ShapeDtypeStruct((t, d), F32),
                   jax.ShapeDtypeStruct((SUBLANE, d), F32)],
        in_specs=[row, pl.BlockSpec((1, d), lambda i: (0, 0)), row],
        out_specs=[pl.BlockSpec((SUBLANE, LANE), lambda i: (0, 0)), row, pl.BlockSpec((SUBLANE, d), lambda i: (0, 0))],
        compiler_params=_params(),
    )(x, g_row, target)


def _ffn_bwd_act(dx2, gg, uu, gath, grp, name):
    t, d = dx2.shape
    ff = grp.rows["g"] * N_DEV
    tm = _row_tile(t, 512)
    fc = MXU_TILE
    n_t = t // tm

    def body(dx2_ref, gg_ref, uu_ref, gath_ref, dgg_ref, duu_ref, dwd_ref, w_d, acc, sems):
        @pl.when(pl.program_id(0) == 0)
        def _():
            _load_weights(gath_ref, grp, ["d"], [w_d], sems)
            _zero(acc)

        dx2b = dx2_ref[...].astype(BF16)
        for c in range(ff // fc):
            fs = slice(c * fc, (c + 1) * fc)
            df = _dot_nt(dx2b, w_d[fs, :]).astype(BF16)
            g = gg_ref[:, fs]
            u = uu_ref[:, fs]
            sg = _sigmoid(g)
            silu = g * sg
            acc[fs, :] += _dot_tn(silu * u, dx2b)
            duu_ref[:, fs] = df * silu
            dgg_ref[:, fs] = df * u * (sg * (1.0 + g * (1.0 - sg)))

        @pl.when(pl.program_id(0) == n_t - 1)
        def _():
            w_d[...] = acc[...].astype(BF16)
            out = pltpu.make_async_copy(w_d, dwd_ref, sems.at[0])
            out.start()
            out.wait()

    row = pl.BlockSpec((tm, d), lambda i: (i, 0))
    wide = pl.BlockSpec((tm, ff), lambda i: (i, 0))
    sd = jax.ShapeDtypeStruct
    return pl.pallas_call(
        body, name=name, grid=(n_t,),
        out_shape=[sd((t, ff), BF16), sd((t, ff), BF16), sd((ff, d), BF16)],
        in_specs=[row, wide, wide, ANY],
        out_specs=[wide, wide, ANY],
        scratch_shapes=[pltpu.VMEM((ff, d), BF16), pltpu.VMEM((ff, d), F32), pltpu.SemaphoreType.DMA((N_DEV,))],
        compiler_params=_params(),
    )(dx2, gg, uu, gath)


def _ffn_bwd_in(dgg, duu, dx2, x1, g_row, gath, grp, name, comms=()):
    t, d = x1.shape
    ff = grp.rows["g"] * N_DEV
    tm = _row_tile(t, 512)

    def body(dgg_ref, duu_ref, dx2_ref, x_ref, g_ref, gath_ref, dx1_ref, dx1b_ref, h_ref, dg_ref, w_g, w_u, sems):
        @pl.when(pl.program_id(0) == 0)
        def _():
            _load_weights(gath_ref, grp, ["g", "u"], [w_g, w_u], sems)
            _zero(dg_ref)

        dh = _dot_nn(dgg_ref[...], w_g[...]) + _dot_nn(duu_ref[...], w_u[...])
        xf = x_ref[...]
        rstd = lax.rsqrt(jnp.mean(xf * xf, axis=-1, keepdims=True) + RMS_EPS)
        xh = xf * rstd
        g = g_ref[...]
        h_ref[...] = (xh * g).astype(BF16)
        dg_ref[0:1, :] += jnp.sum(dh * xh, axis=0, keepdims=True)
        dxh = dh * g
        dx1 = dx2_ref[...] + rstd * (dxh - xh * jnp.mean(dxh * xh, axis=-1, keepdims=True))
        dx1_ref[...] = dx1
        dx1b_ref[...] = dx1.astype(BF16)

    row = pl.BlockSpec((tm, d), lambda i: (i, 0))
    wide = pl.BlockSpec((tm, ff), lambda i: (i, 0))
    sd = jax.ShapeDtypeStruct
    return _hosted_call(
        body, comms, name=name, grid=(t // tm,),
        out_shape=[sd((t, d), F32), sd((t, d), BF16), sd((t, d), BF16), sd((SUBLANE, d), F32)],
        in_specs=[wide, wide, row, row, pl.BlockSpec((1, d), lambda i: (0, 0)), ANY],
        out_specs=[row, row, row, pl.BlockSpec((SUBLANE, d), lambda i: (0, 0))],
        scratch_shapes=[pltpu.VMEM((ff, d), BF16)] * 2 + [pltpu.SemaphoreType.DMA((2 * N_DEV,))],
        args=(dgg, duu, dx2, x1, g_row, gath))


def _merge_bwd(dx1b, oa, ob, ya, yb, p, gbias, gath, grp, name):
    t, d = oa.shape
    tm = _row_tile(t, 512)
    n_t = t // tm

    def body(dx_ref, oa_ref, ob_ref, ya_ref, yb_ref, ga_ref, gb_ref, gbias_ref, gath_ref,
             dya_ref, dyb_ref, dp_ref, dgb_ref, dwoa_ref, dwob_ref, dwo_ref,
             w_oa, w_ob, w_o, acc_oa, acc_ob, acc_o, stage, sems, out_sems):
        @pl.when(pl.program_id(0) == 0)
        def _():
            _load_weights(gath_ref, grp, ["oa", "ob", "o"], [w_oa, w_ob, w_o], sems)
            for ref in (dgb_ref, acc_oa, acc_ob, acc_o):
                _zero(ref)

        dxb = dx_ref[...]
        dm = _dot_nt(dxb, w_o[...]).astype(BF16)
        oa = oa_ref[...]
        ob = ob_ref[...]
        sa = _sigmoid(ga_ref[...] + gbias_ref[0:1, :].astype(BF16))
        sb = _sigmoid(gb_ref[...] + gbias_ref[1:2, :].astype(BF16))
        acc_o[...] += _dot_tn(sa * oa + sb * ob, dxb)
        doa = dm * sa
        dob = dm * sb
        acc_oa[...] += _dot_tn(ya_ref[...], doa)
        acc_ob[...] += _dot_tn(yb_ref[...], dob)
        dga = dm * oa * sa * (1.0 - sa)
        dgb = dm * ob * sb * (1.0 - sb)
        step = pl.program_id(0)
        slot = step % 2

        def to_dp(k, at_step):
            return pltpu.make_async_copy(stage.at[k], dp_ref.at[pl.ds(at_step * tm, tm), pl.ds(5 * d, 2 * d)], out_sems.at[k])

        @pl.when(step >= 2)
        def _():
            to_dp(slot, step - 2).wait()

        stage[slot, :, 0:d] = dga
        stage[slot, :, d:2 * d] = dgb
        to_dp(slot, step).start()
        ones = jnp.ones((SUBLANE, tm), BF16)
        dgb_ref[0:1, :] += _dot_nn(ones, dga)[0:1, :]
        dgb_ref[1:2, :] += _dot_nn(ones, dgb)[0:1, :]
        dya_ref[...] = _dot_nt(doa, w_oa[...]).astype(BF16)
        dyb_ref[...] = _dot_nt(dob, w_ob[...]).astype(BF16)

        @pl.when(pl.program_id(0) == n_t - 1)
        def _():
            outs = []
            for n, (acc, stage, dst) in enumerate(((acc_oa, w_oa, dwoa_ref), (acc_ob, w_ob, dwob_ref), (acc_o, w_o, dwo_ref))):
                stage[...] = acc[...].astype(BF16)
                outs.append(pltpu.make_async_copy(stage, dst, sems.at[n]))
                outs[-1].start()
            for cp in outs:
                cp.wait()
            for back in range(min(2, n_t)):
                to_dp((n_t - 1 - back) % 2, n_t - 1 - back).wait()

    row = pl.BlockSpec((tm, d), lambda i: (i, 0))
    sd = jax.ShapeDtypeStruct
    return pl.pallas_call(
        body, name=name, grid=(n_t,),
        out_shape=[sd((t, d), BF16), sd((t, d), BF16), sd((t, N_PROJ * d), BF16), sd((SUBLANE, d), F32),
                   sd((d, d), BF16), sd((d, d), BF16), sd((d, d), BF16)],
        in_specs=[row, row, row, row, row, pl.BlockSpec((tm, d), lambda i: (i, 5)), pl.BlockSpec((tm, d), lambda i: (i, 6)),
                  pl.BlockSpec((SUBLANE, d), lambda i: (0, 0)), ANY],
        out_specs=[row, row, ANY, pl.BlockSpec((SUBLANE, d), lambda i: (0, 0)), ANY, ANY, ANY],
        scratch_shapes=[pltpu.VMEM((d, d), BF16)] * 3 + [pltpu.VMEM((d, d), F32)] * 3
        + [pltpu.VMEM((2, tm, 2 * d), BF16), pltpu.SemaphoreType.DMA((3 * N_DEV,)), pltpu.SemaphoreType.DMA((2,))],
        compiler_params=_params(),
    )(dx1b, oa, ob, ya, yb, p, p, gbias, gath)


DV_CONV_B_B, DV_BA, DV_BX, DV_SP, DV_CONV_A, DV_CONV_B = 0, 1, 2, 3, 4, 7
DV_ROWS = 16


def _mixer_bwd(dya, dyb, dp_gates, p, u_s, h_s, caw, cbw, vec, bda, bdx, name, comms=()):
    t, d = dya.shape
    tm = _time_tile(t)
    n_t = t // tm
    nk = tm // SUBLANE
    cw = min(MXU_TILE, d)
    nb = d // cw
    halo = 4 * SUBLANE
    ka, kb = CONV_A_K - 1, CONV_B_K - 1

    def body(dya_ref, dyb_ref, _, ba_ref, ca_ref, xa_ref, xb_ref, gb_ref, cah_ref, xah_ref, xbh_ref,
             u_ref, h_ref, hh_ref, caw_ref, cbw_ref, vec_ref, bda_ref, bdx_ref,
             dp_ref, dv_ref, dwa_ref, dwx_ref,
             zbuf, xbuf, hbuf, dczbuf, dubuf, a2buf, a1buf, lbuf, dcz_head, du_head, a_head, lam_head):
        i = pl.program_id(0)

        @pl.when(i == 0)
        def _():
            for ref in (dv_ref, dwa_ref, dwx_ref, dcz_head, du_head, a_head, lam_head):
                _zero(ref)

        has_prev = jnp.where(i < n_t - 1, 1.0, 0.0).astype(F32)
        row = lax.broadcasted_iota(jnp.int32, (SUBLANE, cw), 0)

        def colsum(v):
            return jnp.sum(v, axis=0, keepdims=True)

        for j in range(nb):
            cs = slice(j * cw, (j + 1) * cw)
            ca = ca_ref[:, cs].astype(F32)
            xa = xa_ref[:, cs].astype(F32)
            z = ca * xa
            z_before = cah_ref[:, cs].astype(F32) * xah_ref[:, cs].astype(F32) * has_prev
            _causal_fill(zbuf, z, z_before[halo - ka * SUBLANE:, :], ka, row)
            z2 = zbuf[0:tm, :]
            z1 = zbuf[SUBLANE:SUBLANE + tm, :]
            w0, w1, w2 = caw_ref[0:1, cs], caw_ref[1:2, cs], caw_ref[2:3, cs]
            cz = w0 * z2 + w1 * z1 + w2 * z
            dya = dya_ref[:, cs].astype(F32)
            dp_ref[:, 0 * d + j * cw:0 * d + (j + 1) * cw] = (dya * cz).astype(BF16)
            dcz = dya * ba_ref[:, cs].astype(F32)
            _anticausal_fill(dczbuf, dcz, dcz_head[:, cs], ka, row)
            dcz_head[:, cs] = dcz[0:ka * SUBLANE, :]
            dz = w2 * dcz + w1 * dczbuf[SUBLANE:SUBLANE + tm, :] + w0 * dczbuf[2 * SUBLANE:2 * SUBLANE + tm, :]
            dv_ref[DV_CONV_A + 0:DV_CONV_A + 1, cs] += colsum(dcz * z2)
            dv_ref[DV_CONV_A + 1:DV_CONV_A + 2, cs] += colsum(dcz * z1)
            dv_ref[DV_CONV_A + 2:DV_CONV_A + 3, cs] += colsum(dcz * z)
            dp_ref[:, 1 * d + j * cw:1 * d + (j + 1) * cw] = (dz * xa).astype(BF16)
            dp_ref[:, 2 * d + j * cw:2 * d + (j + 1) * cw] = (dz * ca).astype(BF16)
            h = h_ref[:, cs].astype(F32)
            h_before = hh_ref[:, cs].astype(F32) * has_prev
            _causal_fill(hbuf, h, h_before[halo - SUBLANE:, :], 1, row)
            h_prev = hbuf[0:tm, :]
            dyb = dyb_ref[:, cs].astype(F32)
            gel, dgel = _gelu_and_grad(gb_ref[:, cs].astype(F32))
            dp_ref[:, 4 * d + j * cw:4 * d + (j + 1) * cw] = (dyb * h * dgel).astype(BF16)
            ub = u_ref[:, cs]
            u = ub.astype(F32)
            sp = vec_ref[3:4, cs]
            r, gi, a, s, inv_s = _lru_gates(ub, bda_ref[j], bdx_ref[j], vec_ref[1:2, cs], vec_ref[2:3, cs], sp)
            _anticausal_fill(a2buf, a, a_head[:, cs], 1, row)
            a_head[:, cs] = a[0:SUBLANE, :]
            a1buf[...] = a2buf[SUBLANE:SUBLANE + tm, :]
            lbuf[...] = dyb * gel
            l_end, a_prod = _chain_scan(a1buf, lbuf, nk, reverse=True)
            a_inc, l_inc = _sublane_scan(a_prod, l_end, row, reverse=True)
            carry = lam_head[:, cs]
            ends = l_inc + a_inc * carry
            starts = jnp.where(row == SUBLANE - 1, carry, pltpu.roll(ends, SUBLANE - 1, 0))
            lam_head[:, cs] = jnp.broadcast_to(ends[0:1, :], (SUBLANE, cw))
            lam = (lbuf[...].reshape(nk, SUBLANE, cw) + a1buf[...].reshape(nk, SUBLANE, cw) * starts[None]).reshape(tm, cw)
            da = lam * h_prev
            iu = gi * u
            ds = lam * iu
            di = lam * s * u
            du = lam * s * gi
            dlog_a = da * a - ds * (a * a) * inv_s
            dv_ref[DV_SP:DV_SP + 1, cs] += colsum(dlog_a * r) * (-LRU_C)
            dpr = dlog_a * ((-LRU_C) * sp) * r * (1.0 - r)
            dpi = di * gi * (1.0 - gi)
            dv_ref[DV_BA:DV_BA + 1, cs] += colsum(dpr)
            dv_ref[DV_BX:DV_BX + 1, cs] += colsum(dpi)
            dprb = dpr.astype(BF16)
            dpib = dpi.astype(BF16)
            du = du + _dot_nt(dprb, bda_ref[j]) + _dot_nt(dpib, bdx_ref[j])
            dwa_ref[j] += _dot_tn(ub, dprb)
            dwx_ref[j] += _dot_tn(ub, dpib)
            xb = xb_ref[:, cs].astype(F32)
            x_before = xbh_ref[:, cs].astype(F32) * has_prev
            _causal_fill(xbuf, xb, x_before[halo - kb * SUBLANE:, :], kb, row)
            _anticausal_fill(dubuf, du, du_head[:, cs], kb, row)
            du_head[:, cs] = du[0:kb * SUBLANE, :]
            v0, v1, v2, v3 = cbw_ref[0:1, cs], cbw_ref[1:2, cs], cbw_ref[2:3, cs], cbw_ref[3:4, cs]
            dxb = (v3 * du + v2 * dubuf[SUBLANE:SUBLANE + tm, :] + v1 * dubuf[2 * SUBLANE:2 * SUBLANE + tm, :]
                   + v0 * dubuf[3 * SUBLANE:3 * SUBLANE + tm, :])
            dp_ref[:, 3 * d + j * cw:3 * d + (j + 1) * cw] = dxb.astype(BF16)
            dv_ref[DV_CONV_B_B:DV_CONV_B_B + 1, cs] += colsum(du)
            dv_ref[DV_CONV_B + 0:DV_CONV_B + 1, cs] += colsum(du * xbuf[0:tm, :])
            dv_ref[DV_CONV_B + 1:DV_CONV_B + 2, cs] += colsum(du * xbuf[SUBLANE:SUBLANE + tm, :])
            dv_ref[DV_CONV_B + 2:DV_CONV_B + 3, cs] += colsum(du * xbuf[2 * SUBLANE:2 * SUBLANE + tm, :])
            dv_ref[DV_CONV_B + 3:DV_CONV_B + 4, cs] += colsum(du * xb)

    rt = lambda i: n_t - 1 - i
    row_spec = pl.BlockSpec((tm, d), lambda i: (rt(i), 0))
    slab = lambda s: pl.BlockSpec((tm, d), lambda i, s=s: (rt(i), s))
    before = lambda s: pl.BlockSpec((halo, d), lambda i, s=s: (jnp.maximum(rt(i) * (tm // halo) - 1, 0), s))
    small = pl.BlockSpec((SUBLANE, d), lambda i: (0, 0))
    bd = pl.BlockSpec((nb, cw, cw), lambda i: (0, 0, 0))
    sd = jax.ShapeDtypeStruct
    wbuf = lambda n: pltpu.VMEM((tm + n * SUBLANE, cw), F32)
    head = lambda n: pltpu.VMEM((n * SUBLANE, d), F32)
    return _hosted_call(
        body, comms, name=name, grid=(n_t,),
        out_shape=[sd((t, N_PROJ * d), BF16), sd((DV_ROWS, d), F32), sd((nb, cw, cw), F32), sd((nb, cw, cw), F32)],
        in_specs=[row_spec, row_spec, ANY,
                  slab(0), slab(1), slab(2), slab(3), slab(4), before(1), before(2), before(3),
                  row_spec, row_spec, before(0), small, small, small, bd, bd],
        out_specs=[pl.BlockSpec((tm, 5 * d), lambda i: (rt(i), 0)), pl.BlockSpec((DV_ROWS, d), lambda i: (0, 0)), bd, bd],
        aliases={2: 0},
        scratch_shapes=[wbuf(ka), wbuf(kb), wbuf(1), wbuf(ka), wbuf(kb), wbuf(1),
                        pltpu.VMEM((tm, cw), F32), pltpu.VMEM((tm, cw), F32), head(ka), head(kb), head(1), head(1)],
        args=(dya, dyb, dp_gates, p, p, p, p, p, p, p, p, u_s, h_s, h_s, caw, cbw, vec, bda, bdx))


def _in_proj_bwd(dp, x, dx1, g_row, gath, grp, name, comms=()):
    t, d = x.shape
    tm = _row_tile(t, 512)
    n_in = N_PROJ * d

    def body(dp_ref, x_ref, dx1_ref, g_ref, gath_ref, dx_ref, dg_ref, w_in, sems):
        @pl.when(pl.program_id(0) == 0)
        def _():
            _load_weights(gath_ref, grp, ["in"], [w_in], sems)
            _zero(dg_ref)

        dh = _dot_nn(dp_ref[:, 0:d], w_in[0:d, :])
        for k in range(1, N_PROJ):
            dh = dh + _dot_nn(dp_ref[:, k * d:(k + 1) * d], w_in[k * d:(k + 1) * d, :])
        xf = x_ref[...]
        rstd = lax.rsqrt(jnp.mean(xf * xf, axis=-1, keepdims=True) + RMS_EPS)
        xh = xf * rstd
        g = g_ref[...]
        dg_ref[0:1, :] += jnp.sum(dh * xh, axis=0, keepdims=True)
        dxh = dh * g
        dx_ref[...] = dx1_ref[...] + rstd * (dxh - xh * jnp.mean(dxh * xh, axis=-1, keepdims=True))

    row = pl.BlockSpec((tm, d), lambda i: (i, 0))
    sd = jax.ShapeDtypeStruct
    return _hosted_call(
        body, comms, name=name, grid=(t // tm,),
        out_shape=[sd((t, d), F32), sd((SUBLANE, d), F32)],
        in_specs=[pl.BlockSpec((tm, n_in), lambda i: (i, 0)), row, row, pl.BlockSpec((1, d), lambda i: (0, 0)), ANY],
        out_specs=[row, pl.BlockSpec((SUBLANE, d), lambda i: (0, 0))],
        scratch_shapes=[pltpu.VMEM((n_in, d), BF16), pltpu.SemaphoreType.DMA((N_DEV,))],
        args=(dp, x, dx1, g_row, gath))


def _weight_grad(a, b, name):
    t, m = a.shape
    n = b.shape[1]
    bt = _row_tile(t, 1024)
    bm = m
    for div in (1, 2, 4, 8):
        if m % div == 0 and (m // div) % LANE == 0 and (m // div) * n * 4 <= (12 << 20):
            bm = m // div
            break
    n_t = t // bt

    def body(a_ref, b_ref, o_ref, acc):
        k = pl.program_id(1)

        @pl.when(k == 0)
        def _():
            _zero(acc)

        acc[...] += _dot_tn(a_ref[...], b_ref[...])

        @pl.when(k == n_t - 1)
        def _():
            o_ref[...] = acc[...].astype(BF16)

    return pl.pallas_call(
        body, name=name, grid=(m // bm, n_t),
        out_shape=jax.ShapeDtypeStruct((m, n), BF16),
        in_specs=[pl.BlockSpec((bt, bm), lambda i, k: (k, i)), pl.BlockSpec((bt, n), lambda i, k: (k, 0))],
        out_specs=pl.BlockSpec((bm, n), lambda i, k: (i, 0)),
        scratch_shapes=[pltpu.VMEM((bm, n), F32)],
        compiler_params=_params(2),
    )(a, b)


def _adamw(w, g, m, v, name):
    r, c = w.shape
    tr = _fit_rows(r, c * 4)
    c1 = 1.0 - ADAM_B1 ** ADAM_STEP
    c2 = 1.0 - ADAM_B2 ** ADAM_STEP

    def body(w_ref, g_ref, m_ref, v_ref, d_ref, nm_ref, nv_ref):
        g32 = g_ref[...]
        nm = ADAM_B1 * m_ref[...] + (1.0 - ADAM_B1) * g32
        nv = ADAM_B2 * v_ref[...] + (1.0 - ADAM_B2) * (g32 * g32)
        nm_ref[...] = nm
        nv_ref[...] = nv
        d_ref[...] = -ADAM_LR * ((nm / c1) / (jnp.sqrt(nv / c2) + ADAM_EPS) + ADAM_WD * w_ref[...])

    spec = pl.BlockSpec((tr, c), lambda i: (i, 0))
    return pl.pallas_call(
        body, name=name, grid=(r // tr,),
        out_shape=[jax.ShapeDtypeStruct((r, c), F32)] * 3,
        in_specs=[spec] * 4, out_specs=[spec] * 3,
        compiler_params=_params(),
    )(w, g, m, v)


def _pad_rows(a, mult=SUBLANE):
    pad = (-a.shape[0]) % mult
    return a if pad == 0 else jnp.concatenate([a, jnp.zeros((pad,) + a.shape[1:], a.dtype)], axis=0)


REPLICATED = ("ln1_g", "conv_b_b", "lru_wa", "lru_ba", "lru_wx", "lru_bx", "lru_lambda", "ln2_g", "final_g")
SMALL_SHARDED = ("conv_a_w", "conv_b_w", "gate_bias")
MATRICES = ("w_in", "w_out_a", "w_out_b", "w_o", "w_ffn_gate", "w_ffn_up", "w_ffn_down")
ORDER = ("ln1_g", "w_in", "conv_a_w", "conv_b_w", "conv_b_b", "lru_wa", "lru_ba", "lru_wx", "lru_bx", "lru_lambda",
         "w_out_a", "w_out_b", "gate_bias", "w_o", "ln2_g", "w_ffn_gate", "w_ffn_up", "w_ffn_down", "final_g")


def kernel(x, ln1_g, w_in, conv_a_w, conv_b_w, conv_b_b, lru_wa, lru_ba, lru_wx, lru_bx, lru_lambda, w_out_a, w_out_b, gate_bias, w_o, ln2_g, w_ffn_gate, w_ffn_up, w_ffn_down, final_g, loss_target, m_ln1_g, m_w_in, m_conv_a_w, m_conv_b_w, m_conv_b_b, m_lru_wa, m_lru_ba, m_lru_wx, m_lru_bx, m_lru_lambda, m_w_out_a, m_w_out_b, m_gate_bias, m_w_o, m_ln2_g, m_w_ffn_gate, m_w_ffn_up, m_w_ffn_down, m_final_g, v_ln1_g, v_w_in, v_conv_a_w, v_conv_b_w, v_conv_b_b, v_lru_wa, v_lru_ba, v_lru_wx, v_lru_bx, v_lru_lambda, v_w_out_a, v_w_out_b, v_gate_bias, v_w_o, v_ln2_g, v_w_ffn_gate, v_w_ffn_up, v_w_ffn_down, v_final_g):
    w = dict(ln1_g=ln1_g, w_in=w_in, conv_a_w=conv_a_w, conv_b_w=conv_b_w, conv_b_b=conv_b_b, lru_wa=lru_wa,
             lru_ba=lru_ba, lru_wx=lru_wx, lru_bx=lru_bx, lru_lambda=lru_lambda, w_out_a=w_out_a, w_out_b=w_out_b,
             gate_bias=gate_bias, w_o=w_o, ln2_g=ln2_g, w_ffn_gate=w_ffn_gate, w_ffn_up=w_ffn_up,
             w_ffn_down=w_ffn_down, final_g=final_g)
    mom = dict(ln1_g=m_ln1_g, w_in=m_w_in, conv_a_w=m_conv_a_w, conv_b_w=m_conv_b_w, conv_b_b=m_conv_b_b,
               lru_wa=m_lru_wa, lru_ba=m_lru_ba, lru_wx=m_lru_wx, lru_bx=m_lru_bx, lru_lambda=m_lru_lambda,
               w_out_a=m_w_out_a, w_out_b=m_w_out_b, gate_bias=m_gate_bias, w_o=m_w_o, ln2_g=m_ln2_g,
               w_ffn_gate=m_w_ffn_gate, w_ffn_up=m_w_ffn_up, w_ffn_down=m_w_ffn_down, final_g=m_final_g)
    var = dict(ln1_g=v_ln1_g, w_in=v_w_in, conv_a_w=v_conv_a_w, conv_b_w=v_conv_b_w, conv_b_b=v_conv_b_b,
               lru_wa=v_lru_wa, lru_ba=v_lru_ba, lru_wx=v_lru_wx, lru_bx=v_lru_bx, lru_lambda=v_lru_lambda,
               w_out_a=v_w_out_a, w_out_b=v_w_out_b, gate_bias=v_gate_bias, w_o=v_w_o, ln2_g=v_ln2_g,
               w_ffn_gate=v_w_ffn_gate, w_ffn_up=v_w_ffn_up, w_ffn_down=v_w_ffn_down, final_g=v_final_g)

    _, t, d = x.shape
    n_layers = w_in.shape[0]
    ff = w_ffn_down.shape[1] * N_DEV
    dd = d // N_DEV
    hd = d // LRU_HEADS
    cw = min(MXU_TILE, d)
    nb = d // cw
    hpt = cw // hd
    grp = _groups(d, ff)
    me = 4 * lax.axis_index("x") + 2 * lax.axis_index("y") + lax.axis_index("c")
    tm_time = _time_tile(t)
    x0 = _to_tile_order(x[0], tm_time)
    target = _to_tile_order(loss_target[0], tm_time)

    packed = [{"in": jnp.swapaxes(w_in[l], 0, 1).astype(BF16),
               "rest": jnp.concatenate([w_out_a[l], w_out_b[l], w_o[l], jnp.swapaxes(w_ffn_gate[l], 0, 1),
                                        jnp.swapaxes(w_ffn_up[l], 0, 1), w_ffn_down[l]], axis=0).astype(BF16)}
              for l in range(n_layers)]
    n_small = CONV_A_K + CONV_B_K + 2
    small = _pad_rows(jnp.concatenate([conv_a_w, conv_b_w, gate_bias], axis=1).reshape(n_layers * n_small, dd))
    sp = jax.nn.softplus(-lru_lambda)
    vec = [_pad_rows(jnp.stack([conv_b_b[l], lru_ba[l], lru_bx[l], sp[l]])) for l in range(n_layers)]
    eye = jnp.eye(hpt, dtype=F32)

    def block_diag(wh):
        return jnp.einsum("jkab,kl->jkalb", wh.reshape(nb, hpt, hd, hd), eye).reshape(nb, cw, cw).astype(BF16)

    bda = [block_diag(lru_wa[l]) for l in range(n_layers)]
    bdx = [block_diag(lru_wx[l]) for l in range(n_layers)]

    gath = [dict() for _ in range(n_layers)]
    (gath[0]["in"],) = _comm_call(_Gather(packed[0]["in"]), "gather_in_0")
    saved = []
    xl = x0
    for l in range(n_layers):
        comms = [_Gather(packed[0]["rest"]), _Gather(small)] if l == 0 else []
        (p, h1b), got = _in_proj_fwd(xl, ln1_g[l][None], gath[l]["in"], grp["in"], f"in_proj_fwd_{l}", comms)
        if l == 0:
            gath[0]["rest"], small_g = got[0][0], got[1][0]
            small_full = jnp.swapaxes(small_g[:, :n_layers * n_small], 0, 1).reshape(n_layers, n_small, d)
            caw = [_pad_rows(small_full[k, 0:CONV_A_K]) for k in range(n_layers)]
            cbw = [_pad_rows(small_full[k, CONV_A_K:CONV_A_K + CONV_B_K]) for k in range(n_layers)]
            gbias = [_pad_rows(small_full[k, CONV_A_K + CONV_B_K:]) for k in range(n_layers)]
        more = l + 1 < n_layers
        (ya, yb, *kept), got = _mixer_fwd(p, caw[l], cbw[l], vec[l], bda[l], bdx[l], f"mixer_fwd_{l}",
                                          [_Gather(packed[l + 1]["in"])] if more else [])
        if more:
            ((gath[l + 1]["in"],),) = got
        x1, oa, ob = _merge_fwd(xl, ya, yb, p, gbias[l], gath[l]["rest"], grp["rest"], f"merge_fwd_{l}")
        (x2, gg, uu), got = _ffn_fwd(x1, ln2_g[l][None], gath[l]["rest"], grp["rest"], f"ffn_fwd_{l}",
                                     [_Gather(packed[l + 1]["rest"])] if more else [])
        if more:
            ((gath[l + 1]["rest"],),) = got
        saved.append(dict(x=xl, p=p, h1b=h1b, ya=ya, yb=yb, mixer=kept, x1=x1, oa=oa, ob=ob, gg=gg, uu=uu))
        xl = x2
    loss_tile, dx, dfinal = _loss_head(xl, final_g[None], target, "loss_head")
    loss = lax.psum(loss_tile[0, 0], ("x", "y", "c"))

    def heads(dwb):
        blocks = jnp.diagonal(dwb.reshape(nb, hpt, hd, hpt, hd), axis1=1, axis2=3)
        return jnp.moveaxis(blocks, 3, 1).reshape(hd, d)

    layer_names = [n for n in REPLICATED if n != "final_g"] + list(SMALL_SHARDED)

    def layer_block(k):
        return jnp.concatenate([small_grads[k][n] for n in layer_names], axis=0)

    recv = [dict() for _ in range(n_layers)]
    small_grads = [None] * n_layers
    early_all = None
    for l in reversed(range(n_layers)):
        s = saved[l]
        dgg, duu, dw_d = _ffn_bwd_act(dx, s["gg"], s["uu"], gath[l]["rest"], grp["rest"], f"ffn_bwd_act_{l}")
        comms = []
        if l == 0:
            early = [layer_block(k) for k in range(1, n_layers)] + [_pad_rows(dfinal[0:1])]
            comms = [_Gather(jnp.concatenate(early, axis=0))]
        (dx1, dx1b, h2b, dln2), got = _ffn_bwd_in(dgg, duu, dx, s["x1"], ln2_g[l][None], gath[l]["rest"], grp["rest"],
                                                  f"ffn_bwd_in_{l}", comms)
        if l == 0:
            ((early_all,),) = got
        dya, dyb, dp_gates, dgbias, dw_oa, dw_ob, dw_o = _merge_bwd(dx1b, s["oa"], s["ob"], s["ya"], s["yb"], s["p"], gbias[l],
                                                               gath[l]["rest"], grp["rest"], f"merge_bwd_{l}")
        rest = {"oa": dw_oa, "ob": dw_ob, "o": dw_o, "g": _weight_grad(dgg, h2b, f"dw_ffn_gate_{l}"),
                "u": _weight_grad(duu, h2b, f"dw_ffn_up_{l}"), "d": dw_d}
        (dp, dv, dwa, dwx), got = _mixer_bwd(dya, dyb, dp_gates, s["p"], *s["mixer"], caw[l], cbw[l], vec[l], bda[l], bdx[l],
                                             f"mixer_bwd_{l}", [_Exchange(rest, grp["rest"])])
        (recv[l]["rest"],) = got[0]
        small_grads[l] = {
            "conv_b_b": dv[DV_CONV_B_B:DV_CONV_B_B + 1], "lru_wa": heads(dwa),
            "lru_ba": dv[DV_BA:DV_BA + 1], "lru_wx": heads(dwx), "lru_bx": dv[DV_BX:DV_BX + 1],
            "lru_lambda": dv[DV_SP:DV_SP + 1] * (-jax.nn.sigmoid(-lru_lambda[l]))[None], "ln2_g": dln2[0:1],
            "conv_a_w": dv[DV_CONV_A:DV_CONV_A + CONV_A_K], "conv_b_w": dv[DV_CONV_B:DV_CONV_B + CONV_B_K],
            "gate_bias": dgbias[0:2],
        }
        dw_in = _Exchange({"in": _weight_grad(dp, s["h1b"], f"dw_in_{l}")}, grp["in"])
        (dx, dln1), got = _in_proj_bwd(dp, s["x"], dx1, ln1_g[l][None], gath[l]["in"], grp["in"], f"in_proj_bwd_{l}", [dw_in])
        ((recv[l]["in"],),) = got
        small_grads[l]["ln1_g"] = dln1[0:1]
    grad_x = _from_tile_order(dx, tm_time)[None]

    g = {}
    gsum = [{k: _sum_slots(recv[l][k], f"sum_{k}_{l}") for k in ("in", "rest")} for l in range(n_layers)]

    def part(key):
        k = "in" if key == "in" else "rest"
        o, r = grp[k].off[key], grp[k].rows[key]
        return jnp.stack([gsum[l][k][o:o + r] for l in range(n_layers)])

    g = {"w_in": jnp.swapaxes(part("in"), 1, 2), "w_out_a": part("oa"), "w_out_b": part("ob"), "w_o": part("o"),
         "w_ffn_gate": jnp.swapaxes(part("g"), 1, 2), "w_ffn_up": jnp.swapaxes(part("u"), 1, 2), "w_ffn_down": part("d")}
    (late_all,) = _comm_call(_Gather(layer_block(0).astype(BF16)), "gather_small_grads_0")
    early_sum = _sum_slots(early_all, "sum_small_grads")
    block_rows = late_all.shape[1]
    per_layer = [_sum_slots(late_all, "sum_small_grads_0")]
    per_layer += [early_sum[(k - 1) * block_rows:k * block_rows] for k in range(1, n_layers)]
    g["final_g"] = early_sum[(n_layers - 1) * block_rows].reshape(w["final_g"].shape)
    o = 0
    for n in layer_names:
        rows = small_grads[0][n].shape[0]
        stacked = jnp.concatenate([per_layer[k][o:o + rows] for k in range(n_layers)], axis=0)
        if n in SMALL_SHARDED:
            g[n] = lax.dynamic_slice_in_dim(stacked, me * dd, dd, axis=1).reshape(n_layers, rows, dd)
        else:
            g[n] = stacked.reshape(w[n].shape)
        o += rows

    delta, new_m, new_v = {}, {}, {}
    gate_maps = ("lru_wa", "lru_wx")
    for n in MATRICES + gate_maps:
        shape = w[n].shape
        flat = lambda a: a.reshape(-1, d if n in gate_maps else shape[-1])
        dl, nm, nv = _adamw(flat(w[n]), flat(g[n]), flat(mom[n]), flat(var[n]), f"adamw_{n}")
        delta[n], new_m[n], new_v[n] = dl.reshape(shape), nm.reshape(shape), nv.reshape(shape)
    vectors = tuple(n for n in REPLICATED if n not in gate_maps)
    for group, width, name in ((vectors, d, "adamw_replicated"), (SMALL_SHARDED, dd, "adamw_vectors")):
        cat = lambda src: _pad_rows(jnp.concatenate([src[n].reshape(-1, width) for n in group], axis=0))
        dl, nm, nv = _adamw(cat(w), cat(g), cat(mom), cat(var), name)
        o = 0
        for n in group:
            rows = w[n].size // width
            delta[n], new_m[n], new_v[n] = (a[o:o + rows].reshape(w[n].shape) for a in (dl, nm, nv))
            o += rows

    return (loss, grad_x, *[g[n] for n in ORDER], *[delta[n] for n in ORDER], *[new_m[n] for n in ORDER],
            *[new_v[n] for n in ORDER])
```

```python
import math

import jax
import jax.numpy as jnp
from jax import lax
from jax.experimental import pallas as pl
from jax.experimental.pallas import tpu as pltpu

F32 = jnp.float32
BF16 = jnp.bfloat16

N_DEV = 8
N_PROJ = 7
LRU_HEADS = 16
LRU_C = 8.0
RMS_EPS = 1e-6
CONV_A_K = 3
CONV_B_K = 4
GELU_C = math.sqrt(2.0 / math.pi)
GELU_A = 0.044715

ADAM_LR = 0.001
ADAM_B1 = 0.9
ADAM_B2 = 0.999
ADAM_EPS = 1e-08
ADAM_WD = 0.01
ADAM_STEP = 10

LANE = 128
SUBLANE = 8
MXU_TILE = 256
VMEM_LIMIT = 52 << 20
ALL_PEERS = tuple(range(1, N_DEV))
NEAR_PEERS = (1, 2, 3, 4, 5)
FAR_PEERS = (6, 7)
MESH = pl.DeviceIdType.MESH
ANY = pl.BlockSpec(memory_space=pl.ANY)


def _dot_nn(a, b):
    return lax.dot_general(a, b, (((1,), (0,)), ((), ())), preferred_element_type=F32)


def _dot_nt(a, b):
    return lax.dot_general(a, b, (((1,), (1,)), ((), ())), preferred_element_type=F32)


def _dot_tn(a, b):
    return lax.dot_general(a, b, (((0,), (0,)), ((), ())), preferred_element_type=F32)


def _sigmoid(x):
    return 1.0 / (1.0 + jnp.exp(-x))


def _gelu_and_grad(x):
    x2 = x * x
    t = jnp.tanh(GELU_C * x * (1.0 + GELU_A * x2))
    g = 0.5 * x * (1.0 + t)
    dg = 0.5 * (1.0 + t) + 0.5 * x * (1.0 - t * t) * GELU_C * (1.0 + 3.0 * GELU_A * x2)
    return g, dg


def _zero(ref):
    ref[...] = jnp.zeros(ref.shape, ref.dtype)


def _fit_rows(r, row_bytes, budget=1 << 20):
    fits = [t for t in range(16, r + 1, 16) if r % t == 0 and t * row_bytes <= budget]
    return max(fits) if fits else r


def _row_tile(t, want):
    tm = min(want, t // 2)
    assert t % tm == 0 and tm % SUBLANE == 0, (t, tm)
    return tm


def _params(n_grid=1, **kw):
    return pltpu.CompilerParams(dimension_semantics=("arbitrary",) * n_grid, vmem_limit_bytes=VMEM_LIMIT, **kw)


class _Group:
    def __init__(self, keys, rows):
        self.keys = keys
        self.rows = dict(zip(keys, rows))
        self.off, o = {}, 0
        for k in keys:
            self.off[k] = o
            o += self.rows[k]
        self.total = o


def _groups(d, ff):
    dd, ffs = d // N_DEV, ff // N_DEV
    return {"in": _Group(("in",), (N_PROJ * dd,)),
            "rest": _Group(("oa", "ob", "o", "g", "u", "d"), (dd, dd, dd, ffs, ffs, ffs))}


def _load_weights(g_ref, grp, keys, dsts, sems):
    copies = []
    for n, (k, dst) in enumerate(zip(keys, dsts)):
        rows, off = grp.rows[k], grp.off[k]
        copies += [pltpu.make_async_copy(g_ref.at[p, pl.ds(off, rows), :], dst.at[pl.ds(p * rows, rows), :],
                                         sems.at[n * N_DEV + p]) for p in range(N_DEV)]
    for c in copies:
        c.start()
    for c in copies:
        c.wait()


def _comm_sems():
    return [pltpu.SemaphoreType.DMA((N_DEV - 1,)), pltpu.SemaphoreType.DMA((N_DEV - 1,)), pltpu.SemaphoreType.DMA]


class _Gather:
    def __init__(self, x):
        self.inputs = [x]
        self.out_shape = [jax.ShapeDtypeStruct((N_DEV,) + x.shape, x.dtype)]
        self.scratch = _comm_sems()

    def _plan(self, ins, outs, scr):
        (x_ref,), (out_ref,), (send_sems, recv_sems, local_sem) = ins, outs, scr
        mx, my, mc = lax.axis_index("x"), lax.axis_index("y"), lax.axis_index("c")
        me, sibling = (mx, my, mc), (mx, my, 1 - mc)
        chips = [(1 - mx, my), (mx, 1 - my), (1 - mx, 1 - my)]

        def slot(px, py, pc):
            return out_ref.at[4 * px + 2 * py + pc]

        def copy(k, block, to, src=None):
            return pltpu.make_async_remote_copy(
                src_ref=slot(*block) if src is None else src, dst_ref=slot(*block),
                send_sem=send_sems.at[k], recv_sem=recv_sems.at[k], device_id=to, device_id_type=MESH)

        mine = lambda: pltpu.make_async_copy(x_ref, slot(*me), local_sem)
        first = [lambda: copy(0, me, sibling, src=x_ref)]
        first += [lambda j=j, chip=chip: copy(1 + j, me, (*chip, mc), src=x_ref) for j, chip in enumerate(chips)]
        landed = [lambda j=j, chip=chip: copy(1 + j, (*chip, mc), me) for j, chip in enumerate(chips)]
        passed = [lambda j=j, chip=chip: copy(4 + j, (*chip, mc), sibling) for j, chip in enumerate(chips)]
        from_sibling = [lambda: copy(0, sibling, me)]
        from_sibling += [lambda j=j, chip=chip: copy(4 + j, (*chip, 1 - mc), me) for j, chip in enumerate(chips)]
        return mine, first, landed, passed, from_sibling

    def start(self, ins, outs, scr):
        mine, first, _, _, _ = self._plan(ins, outs, scr)
        mine().start()
        for cp in first:
            cp().start()

    def mid(self, ins, outs, scr):
        _, _, landed, passed, _ = self._plan(ins, outs, scr)
        for got, fwd in zip(landed, passed):
            got().wait_recv()
            fwd().start()

    def finish(self, ins, outs, scr):
        mine, first, _, passed, from_sibling = self._plan(ins, outs, scr)
        for cp in from_sibling:
            cp().wait_recv()
        for cp in first + passed:
            cp().wait_send()
        mine().wait()


class _Exchange:
    def __init__(self, mats, grp, peers=ALL_PEERS, local=True):
        self.grp, self.peers, self.local = grp, tuple(peers), local
        self.inputs = [mats[k] for k in grp.keys]
        slots = len(self.peers) + (1 if local else 0)
        self.out_shape = [jax.ShapeDtypeStruct((slots, grp.total, self.inputs[0].shape[1]), BF16)]
        self.scratch = [pltpu.SemaphoreType.DMA((len(self.peers),)), pltpu.SemaphoreType.DMA((len(self.peers),)),
                        pltpu.SemaphoreType.DMA]

    def _pieces(self, g_refs, out_ref, q, dst_slot):
        out = []
        for g_ref, k in zip(g_refs, self.grp.keys):
            rows = self.grp.rows[k]
            out.append((g_ref.at[pl.ds(pl.multiple_of(q * rows, 16), rows), :],
                        out_ref.at[dst_slot, pl.ds(self.grp.off[k], rows), :]))
        return out

    def start(self, ins, outs, scr):
        (out_ref,), (send_sems, recv_sems, local_sem) = outs, scr
        mx, my, mc = lax.axis_index("x"), lax.axis_index("y"), lax.axis_index("c")
        if self.local:
            for s, t in self._pieces(ins, out_ref, 4 * mx + 2 * my + mc, 0):
                pltpu.make_async_copy(s, t, local_sem).start()
        for n, k in enumerate(self.peers):
            px, py, pc = mx ^ ((k >> 2) & 1), my ^ ((k >> 1) & 1), mc ^ (k & 1)
            for s, t in self._pieces(ins, out_ref, 4 * px + 2 * py + pc, n + (1 if self.local else 0)):
                pltpu.make_async_remote_copy(src_ref=s, dst_ref=t, send_sem=send_sems.at[n], recv_sem=recv_sems.at[n],
                                             device_id=(px, py, pc), device_id_type=MESH).start()

    def mid(self, ins, outs, scr):
        pass

    def finish(self, ins, outs, scr):
        (out_ref,), (send_sems, recv_sems, local_sem) = outs, scr
        mx, my, mc = lax.axis_index("x"), lax.axis_index("y"), lax.axis_index("c")
        whole = out_ref.at[0]
        for n in range(len(self.peers)):
            done = pltpu.make_async_remote_copy(src_ref=whole, dst_ref=whole, send_sem=send_sems.at[n],
                                                recv_sem=recv_sems.at[n], device_id=(mx, my, mc), device_id_type=MESH)
            done.wait_send()
            done.wait_recv()
        if self.local:
            pltpu.make_async_copy(whole, whole, local_sem).wait()


def _split(refs, sizes):
    out, pos = [], 0
    for n in sizes:
        out.append(refs[pos:pos + n])
        pos += n
    return out


def _hosted_call(body, comms, *, name, grid, in_specs, out_specs, out_shape, scratch_shapes, args, aliases=None):
    n_steps = grid[0]
    nc = len(comms)
    sizes = ([len(in_specs)] + [len(c.inputs) for c in comms] + [len(out_specs)] + [len(c.out_shape) for c in comms]
             + [len(scratch_shapes)] + [len(c.scratch) for c in comms])

    def hosted(*refs):
        parts = _split(refs, sizes)
        ins, c_ins = parts[0], parts[1:1 + nc]
        outs, c_outs = parts[1 + nc], parts[2 + nc:2 + 2 * nc]
        scr, c_scr = parts[2 + 2 * nc], parts[3 + 2 * nc:]
        step = pl.program_id(0)
        if comms:
            @pl.when(step == 0)
            def _():
                for c, a, b, s in zip(comms, c_ins, c_outs, c_scr):
                    c.start(a, b, s)

            @pl.when(step == max(n_steps - 2, 0))
            def _():
                for c, a, b, s in zip(comms, c_ins, c_outs, c_scr):
                    c.mid(a, b, s)

        body(*ins, *outs, *scr)
        if comms:
            @pl.when(step == n_steps - 1)
            def _():
                for c, a, b, s in zip(comms, c_ins, c_outs, c_scr):
                    c.finish(a, b, s)

    res = pl.pallas_call(
        hosted, name=name, grid=grid,
        out_shape=[*out_shape, *[o for c in comms for o in c.out_shape]],
        in_specs=[*in_specs, *[ANY for c in comms for _ in c.inputs]],
        out_specs=[*out_specs, *[ANY for c in comms for _ in c.out_shape]],
        scratch_shapes=[*scratch_shapes, *[s for c in comms for s in c.scratch]],
        input_output_aliases=aliases or {},
        compiler_params=_params(),
    )(*args, *[a for c in comms for a in c.inputs])
    main, rest = res[:len(out_specs)], res[len(out_specs):]
    return main, _split(rest, [len(c.out_shape) for c in comms])


def _comm_call(comm, name):
    def body(*refs):
        ins, outs, scr = _split(refs, [len(comm.inputs), len(comm.out_shape), len(comm.scratch)])
        comm.start(ins, outs, scr)
        comm.mid(ins, outs, scr)
        comm.finish(ins, outs, scr)

    return pl.pallas_call(
        body, name=name, out_shape=comm.out_shape, in_specs=[ANY] * len(comm.inputs),
        out_specs=[ANY] * len(comm.out_shape), scratch_shapes=comm.scratch,
    )(*comm.inputs)


def _sum_slots(xs, name):
    _, r, c = xs[0].shape
    tr = _fit_rows(r, c * 4)

    def body(*refs):
        acc = None
        for x_ref in refs[:-1]:
            for p in range(x_ref.shape[0]):
                v = x_ref[p].astype(F32)
                acc = v if acc is None else acc + v
        refs[-1][...] = acc

    return pl.pallas_call(
        body, name=name, grid=(r // tr,),
        out_shape=jax.ShapeDtypeStruct((r, c), F32),
        in_specs=[pl.BlockSpec((x.shape[0], tr, c), lambda i: (0, i, 0)) for x in xs],
        out_specs=pl.BlockSpec((tr, c), lambda i: (i, 0)),
        compiler_params=_params(),
    )(*xs)


def _in_proj_fwd(x, g_row, gath, grp, name, comms=()):
    t, d = x.shape
    tm = _row_tile(t, 512)
    n_in = N_PROJ * d

    def body(x_ref, g_ref, gath_ref, p_ref, h_ref, w_in, sems):
        @pl.when(pl.program_id(0) == 0)
        def _():
            _load_weights(gath_ref, grp, ["in"], [w_in], sems)

        xf = x_ref[...]
        rstd = lax.rsqrt(jnp.mean(xf * xf, axis=-1, keepdims=True) + RMS_EPS)
        h = (xf * rstd * g_ref[...]).astype(BF16)
        h_ref[...] = h
        for k in range(N_PROJ):
            p_ref[:, k * d:(k + 1) * d] = _dot_nt(h, w_in[k * d:(k + 1) * d, :]).astype(BF16)

    row = pl.BlockSpec((tm, d), lambda i: (i, 0))
    return _hosted_call(
        body, comms, name=name, grid=(t // tm,),
        out_shape=[jax.ShapeDtypeStruct((t, n_in), BF16), jax.ShapeDtypeStruct((t, d), BF16)],
        in_specs=[row, pl.BlockSpec((1, d), lambda i: (0, 0)), ANY],
        out_specs=[pl.BlockSpec((tm, n_in), lambda i: (i, 0)), row],
        scratch_shapes=[pltpu.VMEM((n_in, d), BF16), pltpu.SemaphoreType.DMA((N_DEV,))],
        args=(x, g_row, gath))


def _time_tile(t):
    return _row_tile(t, 256)


def _to_tile_order(a, tm):
    t, c = a.shape
    return jnp.swapaxes(a.reshape(t // tm, SUBLANE, tm // SUBLANE, c), 1, 2).reshape(t, c)


def _from_tile_order(a, tm):
    t, c = a.shape
    return jnp.swapaxes(a.reshape(t // tm, tm // SUBLANE, SUBLANE, c), 1, 2).reshape(t, c)


def _causal_fill(buf, v, prev_tail, n, row):
    tm = v.shape[0]
    for q in range(n):
        cur = v[tm - SUBLANE * (n - q):tm - SUBLANE * (n - q - 1), :]
        prv = prev_tail[SUBLANE * q:SUBLANE * (q + 1), :]
        buf[SUBLANE * q:SUBLANE * (q + 1), :] = jnp.where(row == 0, pltpu.roll(prv, 1, 0), pltpu.roll(cur, 1, 0))
    buf[SUBLANE * n:, :] = v


def _anticausal_fill(buf, v, next_head, n, row):
    tm = v.shape[0]
    buf[0:tm, :] = v
    for q in range(n):
        cur = v[SUBLANE * q:SUBLANE * (q + 1), :]
        nxt = next_head[SUBLANE * q:SUBLANE * (q + 1), :]
        buf[tm + SUBLANE * q:tm + SUBLANE * (q + 1), :] = jnp.where(
            row == SUBLANE - 1, pltpu.roll(nxt, SUBLANE - 1, 0), pltpu.roll(cur, SUBLANE - 1, 0))


def _chain_scan(abuf, bbuf, nk, reverse):
    cw = abuf.shape[1]

    def step(n, carry):
        h, c = carry
        r0 = pl.multiple_of((nk - 1 - n if reverse else n) * SUBLANE, SUBLANE)
        ak = abuf[pl.ds(r0, SUBLANE), :]
        h = ak * h + bbuf[pl.ds(r0, SUBLANE), :]
        c = ak * c
        bbuf[pl.ds(r0, SUBLANE), :] = h
        abuf[pl.ds(r0, SUBLANE), :] = c
        return h, c

    return lax.fori_loop(0, nk, step, (jnp.zeros((SUBLANE, cw), F32), jnp.ones((SUBLANE, cw), F32)), unroll=True)


def _sublane_scan(a, b, row, reverse):
    for sh in (1, 2, 4):
        if reverse:
            m = row < SUBLANE - sh
            b = jnp.where(m, a * pltpu.roll(b, SUBLANE - sh, 0) + b, b)
            a = jnp.where(m, a * pltpu.roll(a, SUBLANE - sh, 0), a)
        else:
            m = row >= sh
            b = jnp.where(m, a * pltpu.roll(b, sh, 0) + b, b)
            a = jnp.where(m, a * pltpu.roll(a, sh, 0), a)
    return a, b


def _lru_gates(ub, bda, bdx, ba, bx, sp):
    r = _sigmoid(_dot_nn(ub, bda) + ba)
    i = _sigmoid(_dot_nn(ub, bdx) + bx)
    log_a = (-LRU_C) * r * sp
    a = jnp.exp(log_a)
    s2 = -jnp.tanh(log_a) * (1.0 + a * a)
    inv_s = lax.rsqrt(s2)
    s = jnp.where(s2 > 0.0, s2 * inv_s, 0.0)
    return r, i, a, s, inv_s


def _mixer_fwd(p, caw, cbw, vec, bda, bdx, name, comms=()):
    t = p.shape[0]
    d = p.shape[1] // N_PROJ
    tm = _time_tile(t)
    nk = tm // SUBLANE
    cw = min(MXU_TILE, d)
    nb = d // cw

    def body(ba_ref, ca_ref, xa_ref, xb_ref, gb_ref, caw_ref, cbw_ref, vec_ref, bda_ref, bdx_ref,
             ya_ref, yb_ref, u_ref, h_ref, zbuf, xbuf, abuf, bbuf, z_tail, x_tail, h_carry):
        @pl.when(pl.program_id(0) == 0)
        def _():
            _zero(z_tail)
            _zero(x_tail)
            _zero(h_carry)

        row = lax.broadcasted_iota(jnp.int32, (SUBLANE, cw), 0)
        for j in range(nb):
            cs = slice(j * cw, (j + 1) * cw)
            z = ca_ref[:, cs].astype(F32) * xa_ref[:, cs].astype(F32)
            _causal_fill(zbuf, z, z_tail[:, cs], CONV_A_K - 1, row)
            z_tail[:, cs] = z[tm - (CONV_A_K - 1) * SUBLANE:, :]
            cz = caw_ref[0:1, cs] * zbuf[0:tm, :] + caw_ref[1:2, cs] * zbuf[SUBLANE:SUBLANE + tm, :] + caw_ref[2:3, cs] * z
            ya_ref[:, cs] = (ba_ref[:, cs].astype(F32) * cz).astype(BF16)
            xb = xb_ref[:, cs].astype(F32)
            _causal_fill(xbuf, xb, x_tail[:, cs], CONV_B_K - 1, row)
            x_tail[:, cs] = xb[tm - (CONV_B_K - 1) * SUBLANE:, :]
            u = (cbw_ref[0:1, cs] * xbuf[0:tm, :] + cbw_ref[1:2, cs] * xbuf[SUBLANE:SUBLANE + tm, :]
                 + cbw_ref[2:3, cs] * xbuf[2 * SUBLANE:2 * SUBLANE + tm, :] + cbw_ref[3:4, cs] * xb + vec_ref[0:1, cs])
            ub = u.astype(BF16)
            u = ub.astype(F32)
            _, gi, a, s, _ = _lru_gates(ub, bda_ref[j], bdx_ref[j], vec_ref[1:2, cs], vec_ref[2:3, cs], vec_ref[3:4, cs])
            abuf[...] = a
            bbuf[...] = s * (gi * u)
            h_end, a_prod = _chain_scan(abuf, bbuf, nk, reverse=False)
            a_inc, h_inc = _sublane_scan(a_prod, h_end, row, reverse=False)
            carry = h_carry[:, cs]
            ends = h_inc + a_inc * carry
            starts = jnp.where(row == 0, carry, pltpu.roll(ends, 1, 0))
            h_carry[:, cs] = jnp.broadcast_to(ends[SUBLANE - 1:SUBLANE, :], (SUBLANE, cw))
            h = (bbuf[...].reshape(nk, SUBLANE, cw) + abuf[...].reshape(nk, SUBLANE, cw) * starts[None]).reshape(tm, cw)
            gel, _ = _gelu_and_grad(gb_ref[:, cs].astype(F32))
            yb_ref[:, cs] = (h * gel).astype(BF16)
            u_ref[:, cs] = ub
            h_ref[:, cs] = h.astype(BF16)

    slab = lambda s: pl.BlockSpec((tm, d), lambda i, s=s: (i, s))
    small = pl.BlockSpec((SUBLANE, d), lambda i: (0, 0))
    bd = pl.BlockSpec((nb, cw, cw), lambda i: (0, 0, 0))
    out = pl.BlockSpec((tm, d), lambda i: (i, 0))
    return _hosted_call(
        body, comms, name=name, grid=(t // tm,),
        out_shape=[jax.ShapeDtypeStruct((t, d), BF16)] * 4,
        in_specs=[slab(0), slab(1), slab(2), slab(3), slab(4), small, small, small, bd, bd],
        out_specs=[out] * 4,
        scratch_shapes=[pltpu.VMEM((tm + (CONV_A_K - 1) * SUBLANE, cw), F32), pltpu.VMEM((tm + (CONV_B_K - 1) * SUBLANE, cw), F32),
                        pltpu.VMEM((tm, cw), F32), pltpu.VMEM((tm, cw), F32),
                        pltpu.VMEM(((CONV_A_K - 1) * SUBLANE, d), F32), pltpu.VMEM(((CONV_B_K - 1) * SUBLANE, d), F32),
                        pltpu.VMEM((SUBLANE, d), F32)],
        args=(p, p, p, p, p, caw, cbw, vec, bda, bdx))


def _merge_fwd(x, ya, yb, p, gbias, gath, grp, name):
    t, d = x.shape
    tm = _row_tile(t, 512)

    def body(x_ref, ya_ref, yb_ref, ga_ref, gb_ref, gbias_ref, gath_ref, x1_ref, oa_ref, ob_ref, w_oa, w_ob, w_o, sems):
        @pl.when(pl.program_id(0) == 0)
        def _():
            _load_weights(gath_ref, grp, ["oa", "ob", "o"], [w_oa, w_ob, w_o], sems)

        oa = _dot_nn(ya_ref[...], w_oa[...]).astype(BF16)
        ob = _dot_nn(yb_ref[...], w_ob[...]).astype(BF16)
        oa_ref[...] = oa
        ob_ref[...] = ob
        sa = _sigmoid(ga_ref[...] + gbias_ref[0:1, :].astype(BF16))
        sb = _sigmoid(gb_ref[...] + gbias_ref[1:2, :].astype(BF16))
        x1_ref[...] = x_ref[...] + _dot_nn(sa * oa + sb * ob, w_o[...])

    row = pl.BlockSpec((tm, d), lambda i: (i, 0))
    return pl.pallas_call(
        body, name=name, grid=(t // tm,),
        out_shape=[jax.ShapeDtypeStruct((t, d), F32), jax.ShapeDtypeStruct((t, d), BF16), jax.ShapeDtypeStruct((t, d), BF16)],
        in_specs=[row, row, row, pl.BlockSpec((tm, d), lambda i: (i, 5)), pl.BlockSpec((tm, d), lambda i: (i, 6)),
                  pl.BlockSpec((SUBLANE, d), lambda i: (0, 0)), ANY],
        out_specs=[row, row, row],
        scratch_shapes=[pltpu.VMEM((d, d), BF16)] * 3 + [pltpu.SemaphoreType.DMA((3 * N_DEV,))],
        compiler_params=_params(),
    )(x, ya, yb, p, p, gbias, gath)


def _ffn_fwd(x1, g_row, gath, grp, name, comms=()):
    t, d = x1.shape
    ff = grp.rows["g"] * N_DEV
    tm = _row_tile(t, 512)
    fc = MXU_TILE
    assert ff % fc == 0

    def body(x_ref, g_ref, gath_ref, x2_ref, gg_ref, uu_ref, w_g, w_u, w_d, acc, sems):
        @pl.when(pl.program_id(0) == 0)
        def _():
            _load_weights(gath_ref, grp, ["g", "u", "d"], [w_g, w_u, w_d], sems)

        xf = x_ref[...]
        rstd = lax.rsqrt(jnp.mean(xf * xf, axis=-1, keepdims=True) + RMS_EPS)
        h = (xf * rstd * g_ref[...]).astype(BF16)
        acc[...] = xf
        for c in range(ff // fc):
            fs = slice(c * fc, (c + 1) * fc)
            gg = _dot_nt(h, w_g[fs, :]).astype(BF16)
            uu = _dot_nt(h, w_u[fs, :]).astype(BF16)
            gg_ref[:, fs] = gg
            uu_ref[:, fs] = uu
            acc[...] += _dot_nn(gg * _sigmoid(gg) * uu, w_d[fs, :])
        x2_ref[...] = acc[...]

    row = pl.BlockSpec((tm, d), lambda i: (i, 0))
    wide = pl.BlockSpec((tm, ff), lambda i: (i, 0))
    return _hosted_call(
        body, comms, name=name, grid=(t // tm,),
        out_shape=[jax.ShapeDtypeStruct((t, d), F32), jax.ShapeDtypeStruct((t, ff), BF16), jax.ShapeDtypeStruct((t, ff), BF16)],
        in_specs=[row, pl.BlockSpec((1, d), lambda i: (0, 0)), ANY],
        out_specs=[row, wide, wide],
        scratch_shapes=[pltpu.VMEM((ff, d), BF16)] * 3 + [pltpu.VMEM((tm, d), F32), pltpu.SemaphoreType.DMA((3 * N_DEV,))],
        args=(x1, g_row, gath))


def _loss_head(x, g_row, target, name):
    t, d = x.shape
    tm = _row_tile(t, 512)

    def body(x_ref, g_ref, tgt_ref, loss_ref, dx_ref, dg_ref):
        @pl.when(pl.program_id(0) == 0)
        def _():
            _zero(loss_ref)
            _zero(dg_ref)

        xf = x_ref[...]
        rstd = lax.rsqrt(jnp.mean(xf * xf, axis=-1, keepdims=True) + RMS_EPS)
        xh = xf * rstd
        g = g_ref[...]
        err = xh * g - tgt_ref[...]
        loss_ref[...] += 0.5 * jnp.sum(jnp.sum(err * err, axis=-1, keepdims=True), axis=0, keepdims=True) * (1.0 / d)
        dy = err * (1.0 / d)
        dg_ref[0:1, :] += jnp.sum(dy * xh, axis=0, keepdims=True)
        dxh = dy * g
        dx_ref[...] = rstd * (dxh - xh * jnp.mean(dxh * xh, axis=-1, keepdims=True))

    row = pl.BlockSpec((tm, d), lambda i: (i, 0))
    return pl.pallas_call(
        body, name=name, grid=(t // tm,),
        out_shape=[jax.ShapeDtypeStruct((SUBLANE, LANE), F32), jax.ShapeDtypeStruct((t, d), F32),
                   jax.ShapeDtypeStruct((SUBLANE, d), F32)],
        in_specs=[row, pl.BlockSpec((1, d), lambda i: (0, 0)), row],
        out_specs=[pl.BlockSpec((SUBLANE, LANE), lambda i: (0, 0)), row, pl.BlockSpec((SUBLANE, d), lambda i: (0, 0))],
        compiler_params=_params(),
    )(x, g_row, target)


def _ffn_bwd_act(dx2, gg, uu, gath, grp, name, comms=()):
    t, d = dx2.shape
    ff = grp.rows["g"] * N_DEV
    tm = _row_tile(t, 512)
    fc = MXU_TILE
    n_t = t // tm

    def body(dx2_ref, gg_ref, uu_ref, gath_ref, dgg_ref, duu_ref, dwd_ref, w_d, acc, sems):
        @pl.when(pl.program_id(0) == 0)
        def _():
            _load_weights(gath_ref, grp, ["d"], [w_d], sems)
            _zero(acc)

        dx2b = dx2_ref[...].astype(BF16)
        for c in range(ff // fc):
            fs = slice(c * fc, (c + 1) * fc)
            df = _dot_nt(dx2b, w_d[fs, :]).astype(BF16)
            g = gg_ref[:, fs]
            u = uu_ref[:, fs]
            sg = _sigmoid(g)
            silu = g * sg
            acc[fs, :] += _dot_tn(silu * u, dx2b)
            duu_ref[:, fs] = df * silu
            dgg_ref[:, fs] = df * u * (sg * (1.0 + g * (1.0 - sg)))

        @pl.when(pl.program_id(0) == n_t - 1)
        def _():
            w_d[...] = acc[...].astype(BF16)
            out = pltpu.make_async_copy(w_d, dwd_ref, sems.at[0])
            out.start()
            out.wait()

    row = pl.BlockSpec((tm, d), lambda i: (i, 0))
    wide = pl.BlockSpec((tm, ff), lambda i: (i, 0))
    sd = jax.ShapeDtypeStruct
    return _hosted_call(
        body, comms, name=name, grid=(n_t,),
        out_shape=[sd((t, ff), BF16), sd((t, ff), BF16), sd((ff, d), BF16)],
        in_specs=[row, wide, wide, ANY],
        out_specs=[wide, wide, ANY],
        scratch_shapes=[pltpu.VMEM((ff, d), BF16), pltpu.VMEM((ff, d), F32), pltpu.SemaphoreType.DMA((N_DEV,))],
        args=(dx2, gg, uu, gath))


def _ffn_bwd_in(dgg, duu, dx2, x1, g_row, gath, grp, name, comms=()):
    t, d = x1.shape
    ff = grp.rows["g"] * N_DEV
    tm = _row_tile(t, 512)

    def body(dgg_ref, duu_ref, dx2_ref, x_ref, g_ref, gath_ref, dx1_ref, dx1b_ref, h_ref, dg_ref, w_g, w_u, sems):
        @pl.when(pl.program_id(0) == 0)
        def _():
            _load_weights(gath_ref, grp, ["g", "u"], [w_g, w_u], sems)
            _zero(dg_ref)

        dh = _dot_nn(dgg_ref[...], w_g[...]) + _dot_nn(duu_ref[...], w_u[...])
        xf = x_ref[...]
        rstd = lax.rsqrt(jnp.mean(xf * xf, axis=-1, keepdims=True) + RMS_EPS)
        xh = xf * rstd
        g = g_ref[...]
        h_ref[...] = (xh * g).astype(BF16)
        dg_ref[0:1, :] += jnp.sum(dh * xh, axis=0, keepdims=True)
        dxh = dh * g
        dx1 = dx2_ref[...] + rstd * (dxh - xh * jnp.mean(dxh * xh, axis=-1, keepdims=True))
        dx1_ref[...] = dx1
        dx1b_ref[...] = dx1.astype(BF16)

    row = pl.BlockSpec((tm, d), lambda i: (i, 0))
    wide = pl.BlockSpec((tm, ff), lambda i: (i, 0))
    sd = jax.ShapeDtypeStruct
    return _hosted_call(
        body, comms, name=name, grid=(t // tm,),
        out_shape=[sd((t, d), F32), sd((t, d), BF16), sd((t, d), BF16), sd((SUBLANE, d), F32)],
        in_specs=[wide, wide, row, row, pl.BlockSpec((1, d), lambda i: (0, 0)), ANY],
        out_specs=[row, row, row, pl.BlockSpec((SUBLANE, d), lambda i: (0, 0))],
        scratch_shapes=[pltpu.VMEM((ff, d), BF16)] * 2 + [pltpu.SemaphoreType.DMA((2 * N_DEV,))],
        args=(dgg, duu, dx2, x1, g_row, gath))


def _merge_bwd(dx1b, oa, ob, ya, yb, p, gbias, gath, grp, name, comms=()):
    t, d = oa.shape
    tm = _row_tile(t, 512)
    n_t = t // tm

    def body(dx_ref, oa_ref, ob_ref, ya_ref, yb_ref, ga_ref, gb_ref, gbias_ref, gath_ref,
             dya_ref, dyb_ref, dp_ref, dgb_ref, dwoa_ref, dwob_ref, dwo_ref,
             w_oa, w_ob, w_o, acc_oa, acc_ob, acc_o, stage, sems, out_sems):
        @pl.when(pl.program_id(0) == 0)
        def _():
            _load_weights(gath_ref, grp, ["oa", "ob", "o"], [w_oa, w_ob, w_o], sems)
            for ref in (dgb_ref, acc_oa, acc_ob, acc_o):
                _zero(ref)

        dxb = dx_ref[...]
        dm = _dot_nt(dxb, w_o[...]).astype(BF16)
        oa = oa_ref[...]
        ob = ob_ref[...]
        sa = _sigmoid(ga_ref[...] + gbias_ref[0:1, :].astype(BF16))
        sb = _sigmoid(gb_ref[...] + gbias_ref[1:2, :].astype(BF16))
        acc_o[...] += _dot_tn(sa * oa + sb * ob, dxb)
        doa = dm * sa
        dob = dm * sb
        acc_oa[...] += _dot_tn(ya_ref[...], doa)
        acc_ob[...] += _dot_tn(yb_ref[...], dob)
        dga = dm * oa * sa * (1.0 - sa)
        dgb = dm * ob * sb * (1.0 - sb)
        step = pl.program_id(0)
        slot = step % 2

        def to_dp(k, at_step):
            return pltpu.make_async_copy(stage.at[k], dp_ref.at[pl.ds(at_step * tm, tm), pl.ds(5 * d, 2 * d)], out_sems.at[k])

        @pl.when(step >= 2)
        def _():
            to_dp(slot, step - 2).wait()

        stage[slot, :, 0:d] = dga
        stage[slot, :, d:2 * d] = dgb
        to_dp(slot, step).start()
        ones = jnp.ones((SUBLANE, tm), BF16)
        dgb_ref[0:1, :] += _dot_nn(ones, dga)[0:1, :]
        dgb_ref[1:2, :] += _dot_nn(ones, dgb)[0:1, :]
        dya_ref[...] = _dot_nt(doa, w_oa[...]).astype(BF16)
        dyb_ref[...] = _dot_nt(dob, w_ob[...]).astype(BF16)

        @pl.when(pl.program_id(0) == n_t - 1)
        def _():
            outs = []
            for n, (acc, stage, dst) in enumerate(((acc_oa, w_oa, dwoa_ref), (acc_ob, w_ob, dwob_ref), (acc_o, w_o, dwo_ref))):
                stage[...] = acc[...].astype(BF16)
                outs.append(pltpu.make_async_copy(stage, dst, sems.at[n]))
                outs[-1].start()
            for cp in outs:
                cp.wait()
            for back in range(min(2, n_t)):
                to_dp((n_t - 1 - back) % 2, n_t - 1 - back).wait()

    row = pl.BlockSpec((tm, d), lambda i: (i, 0))
    sd = jax.ShapeDtypeStruct
    return _hosted_call(
        body, comms, name=name, grid=(n_t,),
        out_shape=[sd((t, d), BF16), sd((t, d), BF16), sd((t, N_PROJ * d), BF16), sd((SUBLANE, d), F32),
                   sd((d, d), BF16), sd((d, d), BF16), sd((d, d), BF16)],
        in_specs=[row, row, row, row, row, pl.BlockSpec((tm, d), lambda i: (i, 5)), pl.BlockSpec((tm, d), lambda i: (i, 6)),
                  pl.BlockSpec((SUBLANE, d), lambda i: (0, 0)), ANY],
        out_specs=[row, row, ANY, pl.BlockSpec((SUBLANE, d), lambda i: (0, 0)), ANY, ANY, ANY],
        scratch_shapes=[pltpu.VMEM((d, d), BF16)] * 3 + [pltpu.VMEM((d, d), F32)] * 3
        + [pltpu.VMEM((2, tm, 2 * d), BF16), pltpu.SemaphoreType.DMA((3 * N_DEV,)), pltpu.SemaphoreType.DMA((2,))],
        args=(dx1b, oa, ob, ya, yb, p, p, gbias, gath))


DV_CONV_B_B, DV_BA, DV_BX, DV_SP, DV_CONV_A, DV_CONV_B = 0, 1, 2, 3, 4, 7
DV_ROWS = 16


def _mixer_bwd(dya, dyb, dp_gates, p, u_s, h_s, caw, cbw, vec, bda, bdx, name, comms=()):
    t, d = dya.shape
    tm = _time_tile(t)
    n_t = t // tm
    nk = tm // SUBLANE
    cw = min(MXU_TILE, d)
    nb = d // cw
    halo = 4 * SUBLANE
    ka, kb = CONV_A_K - 1, CONV_B_K - 1

    def body(dya_ref, dyb_ref, _, ba_ref, ca_ref, xa_ref, xb_ref, gb_ref, cah_ref, xah_ref, xbh_ref,
             u_ref, h_ref, hh_ref, caw_ref, cbw_ref, vec_ref, bda_ref, bdx_ref,
             dp_ref, dv_ref, dwa_ref, dwx_ref,
             zbuf, xbuf, hbuf, dczbuf, dubuf, a2buf, a1buf, lbuf, dcz_head, du_head, a_head, lam_head):
        i = pl.program_id(0)

        @pl.when(i == 0)
        def _():
            for ref in (dv_ref, dwa_ref, dwx_ref, dcz_head, du_head, a_head, lam_head):
                _zero(ref)

        has_prev = jnp.where(i < n_t - 1, 1.0, 0.0).astype(F32)
        row = lax.broadcasted_iota(jnp.int32, (SUBLANE, cw), 0)

        def colsum(v):
            return jnp.sum(v, axis=0, keepdims=True)

        for j in range(nb):
            cs = slice(j * cw, (j + 1) * cw)
            ca = ca_ref[:, cs].astype(F32)
            xa = xa_ref[:, cs].astype(F32)
            z = ca * xa
            z_before = cah_ref[:, cs].astype(F32) * xah_ref[:, cs].astype(F32) * has_prev
            _causal_fill(zbuf, z, z_before[halo - ka * SUBLANE:, :], ka, row)
            z2 = zbuf[0:tm, :]
            z1 = zbuf[SUBLANE:SUBLANE + tm, :]
            w0, w1, w2 = caw_ref[0:1, cs], caw_ref[1:2, cs], caw_ref[2:3, cs]
            cz = w0 * z2 + w1 * z1 + w2 * z
            dya = dya_ref[:, cs].astype(F32)
            dp_ref[:, 0 * d + j * cw:0 * d + (j + 1) * cw] = (dya * cz).astype(BF16)
            dcz = dya * ba_ref[:, cs].astype(F32)
            _anticausal_fill(dczbuf, dcz, dcz_head[:, cs], ka, row)
            dcz_head[:, cs] = dcz[0:ka * SUBLANE, :]
            dz = w2 * dcz + w1 * dczbuf[SUBLANE:SUBLANE + tm, :] + w0 * dczbuf[2 * SUBLANE:2 * SUBLANE + tm, :]
            dv_ref[DV_CONV_A + 0:DV_CONV_A + 1, cs] += colsum(dcz * z2)
            dv_ref[DV_CONV_A + 1:DV_CONV_A + 2, cs] += colsum(dcz * z1)
            dv_ref[DV_CONV_A + 2:DV_CONV_A + 3, cs] += colsum(dcz * z)
            dp_ref[:, 1 * d + j * cw:1 * d + (j + 1) * cw] = (dz * xa).astype(BF16)
            dp_ref[:, 2 * d + j * cw:2 * d + (j + 1) * cw] = (dz * ca).astype(BF16)
            h = h_ref[:, cs].astype(F32)
            h_before = hh_ref[:, cs].astype(F32) * has_prev
            _causal_fill(hbuf, h, h_before[halo - SUBLANE:, :], 1, row)
            h_prev = hbuf[0:tm, :]
            dyb = dyb_ref[:, cs].astype(F32)
            gel, dgel = _gelu_and_grad(gb_ref[:, cs].astype(F32))
            dp_ref[:, 4 * d + j * cw:4 * d + (j + 1) * cw] = (dyb * h * dgel).astype(BF16)
            ub = u_ref[:, cs]
            u = ub.astype(F32)
            sp = vec_ref[3:4, cs]
            r, gi, a, s, inv_s = _lru_gates(ub, bda_ref[j], bdx_ref[j], vec_ref[1:2, cs], vec_ref[2:3, cs], sp)
            _anticausal_fill(a2buf, a, a_head[:, cs], 1, row)
            a_head[:, cs] = a[0:SUBLANE, :]
            a1buf[...] = a2buf[SUBLANE:SUBLANE + tm, :]
            lbuf[...] = dyb * gel
            l_end, a_prod = _chain_scan(a1buf, lbuf, nk, reverse=True)
            a_inc, l_inc = _sublane_scan(a_prod, l_end, row, reverse=True)
            carry = lam_head[:, cs]
            ends = l_inc + a_inc * carry
            starts = jnp.where(row == SUBLANE - 1, carry, pltpu.roll(ends, SUBLANE - 1, 0))
            lam_head[:, cs] = jnp.broadcast_to(ends[0:1, :], (SUBLANE, cw))
            lam = (lbuf[...].reshape(nk, SUBLANE, cw) + a1buf[...].reshape(nk, SUBLANE, cw) * starts[None]).reshape(tm, cw)
            da = lam * h_prev
            iu = gi * u
            ds = lam * iu
            di = lam * s * u
            du = lam * s * gi
            dlog_a = da * a - ds * (a * a) * inv_s
            dv_ref[DV_SP:DV_SP + 1, cs] += colsum(dlog_a * r) * (-LRU_C)
            dpr = dlog_a * ((-LRU_C) * sp) * r * (1.0 - r)
            dpi = di * gi * (1.0 - gi)
            dv_ref[DV_BA:DV_BA + 1, cs] += colsum(dpr)
            dv_ref[DV_BX:DV_BX + 1, cs] += colsum(dpi)
            dprb = dpr.astype(BF16)
            dpib = dpi.astype(BF16)
            du = du + _dot_nt(dprb, bda_ref[j]) + _dot_nt(dpib, bdx_ref[j])
            dwa_ref[j] += _dot_tn(ub, dprb)
            dwx_ref[j] += _dot_tn(ub, dpib)
            xb = xb_ref[:, cs].astype(F32)
            x_before = xbh_ref[:, cs].astype(F32) * has_prev
            _causal_fill(xbuf, xb, x_before[halo - kb * SUBLANE:, :], kb, row)
            _anticausal_fill(dubuf, du, du_head[:, cs], kb, row)
            du_head[:, cs] = du[0:kb * SUBLANE, :]
            v0, v1, v2, v3 = cbw_ref[0:1, cs], cbw_ref[1:2, cs], cbw_ref[2:3, cs], cbw_ref[3:4, cs]
            dxb = (v3 * du + v2 * dubuf[SUBLANE:SUBLANE + tm, :] + v1 * dubuf[2 * SUBLANE:2 * SUBLANE + tm, :]
                   + v0 * dubuf[3 * SUBLANE:3 * SUBLANE + tm, :])
            dp_ref[:, 3 * d + j * cw:3 * d + (j + 1) * cw] = dxb.astype(BF16)
            dv_ref[DV_CONV_B_B:DV_CONV_B_B + 1, cs] += colsum(du)
            dv_ref[DV_CONV_B + 0:DV_CONV_B + 1, cs] += colsum(du * xbuf[0:tm, :])
            dv_ref[DV_CONV_B + 1:DV_CONV_B + 2, cs] += colsum(du * xbuf[SUBLANE:SUBLANE + tm, :])
            dv_ref[DV_CONV_B + 2:DV_CONV_B + 3, cs] += colsum(du * xbuf[2 * SUBLANE:2 * SUBLANE + tm, :])
            dv_ref[DV_CONV_B + 3:DV_CONV_B + 4, cs] += colsum(du * xb)

    rt = lambda i: n_t - 1 - i
    row_spec = pl.BlockSpec((tm, d), lambda i: (rt(i), 0))
    slab = lambda s: pl.BlockSpec((tm, d), lambda i, s=s: (rt(i), s))
    before = lambda s: pl.BlockSpec((halo, d), lambda i, s=s: (jnp.maximum(rt(i) * (tm // halo) - 1, 0), s))
    small = pl.BlockSpec((SUBLANE, d), lambda i: (0, 0))
    bd = pl.BlockSpec((nb, cw, cw), lambda i: (0, 0, 0))
    sd = jax.ShapeDtypeStruct
    wbuf = lambda n: pltpu.VMEM((tm + n * SUBLANE, cw), F32)
    head = lambda n: pltpu.VMEM((n * SUBLANE, d), F32)
    return _hosted_call(
        body, comms, name=name, grid=(n_t,),
        out_shape=[sd((t, N_PROJ * d), BF16), sd((DV_ROWS, d), F32), sd((nb, cw, cw), F32), sd((nb, cw, cw), F32)],
        in_specs=[row_spec, row_spec, ANY,
                  slab(0), slab(1), slab(2), slab(3), slab(4), before(1), before(2), before(3),
                  row_spec, row_spec, before(0), small, small, small, bd, bd],
        out_specs=[pl.BlockSpec((tm, 5 * d), lambda i: (rt(i), 0)), pl.BlockSpec((DV_ROWS, d), lambda i: (0, 0)), bd, bd],
        aliases={2: 0},
        scratch_shapes=[wbuf(ka), wbuf(kb), wbuf(1), wbuf(ka), wbuf(kb), wbuf(1),
                        pltpu.VMEM((tm, cw), F32), pltpu.VMEM((tm, cw), F32), head(ka), head(kb), head(1), head(1)],
        args=(dya, dyb, dp_gates, p, p, p, p, p, p, p, p, u_s, h_s, h_s, caw, cbw, vec, bda, bdx))


def _in_proj_bwd(dp, x, dx1, g_row, gath, grp, name, comms=()):
    t, d = x.shape
    tm = _row_tile(t, 512)
    n_in = N_PROJ * d

    def body(dp_ref, x_ref, dx1_ref, g_ref, gath_ref, dx_ref, dg_ref, w_in, sems):
        @pl.when(pl.program_id(0) == 0)
        def _():
            _load_weights(gath_ref, grp, ["in"], [w_in], sems)
            _zero(dg_ref)

        dh = _dot_nn(dp_ref[:, 0:d], w_in[0:d, :])
        for k in range(1, N_PROJ):
            dh = dh + _dot_nn(dp_ref[:, k * d:(k + 1) * d], w_in[k * d:(k + 1) * d, :])
        xf = x_ref[...]
        rstd = lax.rsqrt(jnp.mean(xf * xf, axis=-1, keepdims=True) + RMS_EPS)
        xh = xf * rstd
        g = g_ref[...]
        dg_ref[0:1, :] += jnp.sum(dh * xh, axis=0, keepdims=True)
        dxh = dh * g
        dx_ref[...] = dx1_ref[...] + rstd * (dxh - xh * jnp.mean(dxh * xh, axis=-1, keepdims=True))

    row = pl.BlockSpec((tm, d), lambda i: (i, 0))
    sd = jax.ShapeDtypeStruct
    return _hosted_call(
        body, comms, name=name, grid=(t // tm,),
        out_shape=[sd((t, d), F32), sd((SUBLANE, d), F32)],
        in_specs=[pl.BlockSpec((tm, n_in), lambda i: (i, 0)), row, row, pl.BlockSpec((1, d), lambda i: (0, 0)), ANY],
        out_specs=[row, pl.BlockSpec((SUBLANE, d), lambda i: (0, 0))],
        scratch_shapes=[pltpu.VMEM((n_in, d), BF16), pltpu.SemaphoreType.DMA((N_DEV,))],
        args=(dp, x, dx1, g_row, gath))


def _weight_grad(a, b, name):
    t, m = a.shape
    n = b.shape[1]
    bt = _row_tile(t, 1024)
    bm = m
    for div in (1, 2, 4, 8):
        if m % div == 0 and (m // div) % LANE == 0 and (m // div) * n * 4 <= (12 << 20):
            bm = m // div
            break
    n_t = t // bt

    def body(a_ref, b_ref, o_ref, acc):
        k = pl.program_id(1)

        @pl.when(k == 0)
        def _():
            _zero(acc)

        acc[...] += _dot_tn(a_ref[...], b_ref[...])

        @pl.when(k == n_t - 1)
        def _():
            o_ref[...] = acc[...].astype(BF16)

    return pl.pallas_call(
        body, name=name, grid=(m // bm, n_t),
        out_shape=jax.ShapeDtypeStruct((m, n), BF16),
        in_specs=[pl.BlockSpec((bt, bm), lambda i, k: (k, i)), pl.BlockSpec((bt, n), lambda i, k: (k, 0))],
        out_specs=pl.BlockSpec((bm, n), lambda i, k: (i, 0)),
        scratch_shapes=[pltpu.VMEM((bm, n), F32)],
        compiler_params=_params(2),
    )(a, b)


def _adamw(w, g, m, v, name):
    r, c = w.shape
    tr = _fit_rows(r, c * 4)
    c1 = 1.0 - ADAM_B1 ** ADAM_STEP
    c2 = 1.0 - ADAM_B2 ** ADAM_STEP

    def body(w_ref, g_ref, m_ref, v_ref, d_ref, nm_ref, nv_ref):
        g32 = g_ref[...]
        nm = ADAM_B1 * m_ref[...] + (1.0 - ADAM_B1) * g32
        nv = ADAM_B2 * v_ref[...] + (1.0 - ADAM_B2) * (g32 * g32)
        nm_ref[...] = nm
        nv_ref[...] = nv
        d_ref[...] = -ADAM_LR * ((nm / c1) / (jnp.sqrt(nv / c2) + ADAM_EPS) + ADAM_WD * w_ref[...])

    spec = pl.BlockSpec((tr, c), lambda i: (i, 0))
    return pl.pallas_call(
        body, name=name, grid=(r // tr,),
        out_shape=[jax.ShapeDtypeStruct((r, c), F32)] * 3,
        in_specs=[spec] * 4, out_specs=[spec] * 3,
        compiler_params=_params(),
    )(w, g, m, v)


def _pad_rows(a, mult=SUBLANE):
    pad = (-a.shape[0]) % mult
    return a if pad == 0 else jnp.concatenate([a, jnp.zeros((pad,) + a.shape[1:], a.dtype)], axis=0)


REPLICATED = ("ln1_g", "conv_b_b", "lru_wa", "lru_ba", "lru_wx", "lru_bx", "lru_lambda", "ln2_g", "final_g")
SMALL_SHARDED = ("conv_a_w", "conv_b_w", "gate_bias")
MATRICES = ("w_in", "w_out_a", "w_out_b", "w_o", "w_ffn_gate", "w_ffn_up", "w_ffn_down")
ORDER = ("ln1_g", "w_in", "conv_a_w", "conv_b_w", "conv_b_b", "lru_wa", "lru_ba", "lru_wx", "lru_bx", "lru_lambda",
         "w_out_a", "w_out_b", "gate_bias", "w_o", "ln2_g", "w_ffn_gate", "w_ffn_up", "w_ffn_down", "final_g")


def kernel(x, ln1_g, w_in, conv_a_w, conv_b_w, conv_b_b, lru_wa, lru_ba, lru_wx, lru_bx, lru_lambda, w_out_a, w_out_b, gate_bias, w_o, ln2_g, w_ffn_gate, w_ffn_up, w_ffn_down, final_g, loss_target, m_ln1_g, m_w_in, m_conv_a_w, m_conv_b_w, m_conv_b_b, m_lru_wa, m_lru_ba, m_lru_wx, m_lru_bx, m_lru_lambda, m_w_out_a, m_w_out_b, m_gate_bias, m_w_o, m_ln2_g, m_w_ffn_gate, m_w_ffn_up, m_w_ffn_down, m_final_g, v_ln1_g, v_w_in, v_conv_a_w, v_conv_b_w, v_conv_b_b, v_lru_wa, v_lru_ba, v_lru_wx, v_lru_bx, v_lru_lambda, v_w_out_a, v_w_out_b, v_gate_bias, v_w_o, v_ln2_g, v_w_ffn_gate, v_w_ffn_up, v_w_ffn_down, v_final_g):
    w = dict(ln1_g=ln1_g, w_in=w_in, conv_a_w=conv_a_w, conv_b_w=conv_b_w, conv_b_b=conv_b_b, lru_wa=lru_wa,
             lru_ba=lru_ba, lru_wx=lru_wx, lru_bx=lru_bx, lru_lambda=lru_lambda, w_out_a=w_out_a, w_out_b=w_out_b,
             gate_bias=gate_bias, w_o=w_o, ln2_g=ln2_g, w_ffn_gate=w_ffn_gate, w_ffn_up=w_ffn_up,
             w_ffn_down=w_ffn_down, final_g=final_g)
    mom = dict(ln1_g=m_ln1_g, w_in=m_w_in, conv_a_w=m_conv_a_w, conv_b_w=m_conv_b_w, conv_b_b=m_conv_b_b,
               lru_wa=m_lru_wa, lru_ba=m_lru_ba, lru_wx=m_lru_wx, lru_bx=m_lru_bx, lru_lambda=m_lru_lambda,
               w_out_a=m_w_out_a, w_out_b=m_w_out_b, gate_bias=m_gate_bias, w_o=m_w_o, ln2_g=m_ln2_g,
               w_ffn_gate=m_w_ffn_gate, w_ffn_up=m_w_ffn_up, w_ffn_down=m_w_ffn_down, final_g=m_final_g)
    var = dict(ln1_g=v_ln1_g, w_in=v_w_in, conv_a_w=v_conv_a_w, conv_b_w=v_conv_b_w, conv_b_b=v_conv_b_b,
               lru_wa=v_lru_wa, lru_ba=v_lru_ba, lru_wx=v_lru_wx, lru_bx=v_lru_bx, lru_lambda=v_lru_lambda,
               w_out_a=v_w_out_a, w_out_b=v_w_out_b, gate_bias=v_gate_bias, w_o=v_w_o, ln2_g=v_ln2_g,
               w_ffn_gate=v_w_ffn_gate, w_ffn_up=v_w_ffn_up, w_ffn_down=v_w_ffn_down, final_g=v_final_g)

    _, t, d = x.shape
    n_layers = w_in.shape[0]
    ff = w_ffn_down.shape[1] * N_DEV
    dd = d // N_DEV
    hd = d // LRU_HEADS
    cw = min(MXU_TILE, d)
    nb = d // cw
    hpt = cw // hd
    grp = _groups(d, ff)
    me = 4 * lax.axis_index("x") + 2 * lax.axis_index("y") + lax.axis_index("c")
    tm_time = _time_tile(t)
    x0 = _to_tile_order(x[0], tm_time)
    target = _to_tile_order(loss_target[0], tm_time)

    packed = [{"in": jnp.swapaxes(w_in[l], 0, 1).astype(BF16),
               "rest": jnp.concatenate([w_out_a[l], w_out_b[l], w_o[l], jnp.swapaxes(w_ffn_gate[l], 0, 1),
                                        jnp.swapaxes(w_ffn_up[l], 0, 1), w_ffn_down[l]], axis=0).astype(BF16)}
              for l in range(n_layers)]
    n_small = CONV_A_K + CONV_B_K + 2
    small = _pad_rows(jnp.concatenate([conv_a_w, conv_b_w, gate_bias], axis=1).reshape(n_layers * n_small, dd))
    sp = jax.nn.softplus(-lru_lambda)
    vec = [_pad_rows(jnp.stack([conv_b_b[l], lru_ba[l], lru_bx[l], sp[l]])) for l in range(n_layers)]
    eye = jnp.eye(hpt, dtype=F32)

    def block_diag(wh):
        return jnp.einsum("jkab,kl->jkalb", wh.reshape(nb, hpt, hd, hd), eye).reshape(nb, cw, cw).astype(BF16)

    bda = [block_diag(lru_wa[l]) for l in range(n_layers)]
    bdx = [block_diag(lru_wx[l]) for l in range(n_layers)]

    gath = [dict() for _ in range(n_layers)]
    (gath[0]["in"],) = _comm_call(_Gather(packed[0]["in"]), "gather_in_0")
    saved = []
    xl = x0
    for l in range(n_layers):
        comms = [_Gather(packed[0]["rest"]), _Gather(small)] if l == 0 else []
        (p, h1b), got = _in_proj_fwd(xl, ln1_g[l][None], gath[l]["in"], grp["in"], f"in_proj_fwd_{l}", comms)
        if l == 0:
            gath[0]["rest"], small_g = got[0][0], got[1][0]
            small_full = jnp.swapaxes(small_g[:, :n_layers * n_small], 0, 1).reshape(n_layers, n_small, d)
            caw = [_pad_rows(small_full[k, 0:CONV_A_K]) for k in range(n_layers)]
            cbw = [_pad_rows(small_full[k, CONV_A_K:CONV_A_K + CONV_B_K]) for k in range(n_layers)]
            gbias = [_pad_rows(small_full[k, CONV_A_K + CONV_B_K:]) for k in range(n_layers)]
        more = l + 1 < n_layers
        (ya, yb, *kept), got = _mixer_fwd(p, caw[l], cbw[l], vec[l], bda[l], bdx[l], f"mixer_fwd_{l}",
                                          [_Gather(packed[l + 1]["in"])] if more else [])
        if more:
            ((gath[l + 1]["in"],),) = got
        x1, oa, ob = _merge_fwd(xl, ya, yb, p, gbias[l], gath[l]["rest"], grp["rest"], f"merge_fwd_{l}")
        (x2, gg, uu), got = _ffn_fwd(x1, ln2_g[l][None], gath[l]["rest"], grp["rest"], f"ffn_fwd_{l}",
                                     [_Gather(packed[l + 1]["rest"])] if more else [])
        if more:
            ((gath[l + 1]["rest"],),) = got
        saved.append(dict(x=xl, p=p, h1b=h1b, ya=ya, yb=yb, mixer=kept, x1=x1, oa=oa, ob=ob, gg=gg, uu=uu))
        xl = x2
    loss_tile, dx, dfinal = _loss_head(xl, final_g[None], target, "loss_head")
    loss = lax.psum(loss_tile[0, 0], ("x", "y", "c"))

    def heads(dwb):
        blocks = jnp.diagonal(dwb.reshape(nb, hpt, hd, hpt, hd), axis1=1, axis2=3)
        return jnp.moveaxis(blocks, 3, 1).reshape(hd, d)

    layer_names = [n for n in REPLICATED if n != "final_g"] + list(SMALL_SHARDED)

    def layer_block(k):
        return jnp.concatenate([small_grads[k][n] for n in layer_names], axis=0)

    recv = [dict() for _ in range(n_layers)]
    small_grads = [None] * n_layers
    early_all = None
    xg = {"d": _Group(("d",), (ff // N_DEV,)), "gu": _Group(("g", "u"), (ff // N_DEV,) * 2),
          "out": _Group(("oa", "ob", "o"), (dd,) * 3), "in": grp["in"]}
    far_in = None
    for l in reversed(range(n_layers)):
        s = saved[l]
        (dgg, duu, dw_d), got = _ffn_bwd_act(dx, s["gg"], s["uu"], gath[l]["rest"], grp["rest"], f"ffn_bwd_act_{l}",
                                             [far_in] if far_in else [])
        if far_in:
            recv[l + 1]["in"].append(got[0][0])
        comms = [_Exchange({"d": dw_d}, xg["d"])]
        if l == 0:
            early = [layer_block(k) for k in range(1, n_layers)] + [_pad_rows(dfinal[0:1])]
            comms.append(_Gather(jnp.concatenate(early, axis=0)))
        (dx1, dx1b, h2b, dln2), got = _ffn_bwd_in(dgg, duu, dx, s["x1"], ln2_g[l][None], gath[l]["rest"], grp["rest"],
                                                  f"ffn_bwd_in_{l}", comms)
        recv[l]["d"] = [got[0][0]]
        if l == 0:
            early_all = got[1][0]
        dw_gu = {"g": _weight_grad(dgg, h2b, f"dw_ffn_gate_{l}"), "u": _weight_grad(duu, h2b, f"dw_ffn_up_{l}")}
        (dya, dyb, dp_gates, dgbias, dw_oa, dw_ob, dw_o), got = _merge_bwd(
            dx1b, s["oa"], s["ob"], s["ya"], s["yb"], s["p"], gbias[l], gath[l]["rest"], grp["rest"], f"merge_bwd_{l}",
            [_Exchange(dw_gu, xg["gu"])])
        recv[l]["gu"] = [got[0][0]]
        (dp, dv, dwa, dwx), got = _mixer_bwd(dya, dyb, dp_gates, s["p"], *s["mixer"], caw[l], cbw[l], vec[l], bda[l], bdx[l],
                                             f"mixer_bwd_{l}", [_Exchange({"oa": dw_oa, "ob": dw_ob, "o": dw_o}, xg["out"])])
        recv[l]["out"] = [got[0][0]]
        small_grads[l] = {
            "conv_b_b": dv[DV_CONV_B_B:DV_CONV_B_B + 1], "lru_wa": heads(dwa),
            "lru_ba": dv[DV_BA:DV_BA + 1], "lru_wx": heads(dwx), "lru_bx": dv[DV_BX:DV_BX + 1],
            "lru_lambda": dv[DV_SP:DV_SP + 1] * (-jax.nn.sigmoid(-lru_lambda[l]))[None], "ln2_g": dln2[0:1],
            "conv_a_w": dv[DV_CONV_A:DV_CONV_A + CONV_A_K], "conv_b_w": dv[DV_CONV_B:DV_CONV_B + CONV_B_K],
            "gate_bias": dgbias[0:2],
        }
        dw_in = {"in": _weight_grad(dp, s["h1b"], f"dw_in_{l}")}
        if l > 0:
            near_in, far_in = _Exchange(dw_in, xg["in"], NEAR_PEERS), _Exchange(dw_in, xg["in"], FAR_PEERS, local=False)
        else:
            near_in, far_in = _Exchange(dw_in, xg["in"]), None
        (dx, dln1), got = _in_proj_bwd(dp, s["x"], dx1, ln1_g[l][None], gath[l]["in"], grp["in"], f"in_proj_bwd_{l}", [near_in])
        recv[l]["in"] = [got[0][0]]
        small_grads[l]["ln1_g"] = dln1[0:1]
    grad_x = _from_tile_order(dx, tm_time)[None]

    g = {}
    gsum = [{k: _sum_slots(recv[l][k], f"sum_{k}_{l}") for k in xg} for l in range(n_layers)]

    def part(key):
        k = next(name for name, group in xg.items() if key in group.keys)
        o, r = xg[k].off[key], xg[k].rows[key]
        return jnp.stack([gsum[l][k][o:o + r] for l in range(n_layers)])

    g = {"w_in": jnp.swapaxes(part("in"), 1, 2), "w_out_a": part("oa"), "w_out_b": part("ob"), "w_o": part("o"),
         "w_ffn_gate": jnp.swapaxes(part("g"), 1, 2), "w_ffn_up": jnp.swapaxes(part("u"), 1, 2), "w_ffn_down": part("d")}
    (late_all,) = _comm_call(_Gather(layer_block(0).astype(BF16)), "gather_small_grads_0")
    early_sum = _sum_slots([early_all], "sum_small_grads")
    block_rows = late_all.shape[1]
    per_layer = [_sum_slots([late_all], "sum_small_grads_0")]
    per_layer += [early_sum[(k - 1) * block_rows:k * block_rows] for k in range(1, n_layers)]
    g["final_g"] = early_sum[(n_layers - 1) * block_rows].reshape(w["final_g"].shape)
    o = 0
    for n in layer_names:
        rows = small_grads[0][n].shape[0]
        stacked = jnp.concatenate([per_layer[k][o:o + rows] for k in range(n_layers)], axis=0)
        if n in SMALL_SHARDED:
            g[n] = lax.dynamic_slice_in_dim(stacked, me * dd, dd, axis=1).reshape(n_layers, rows, dd)
        else:
            g[n] = stacked.reshape(w[n].shape)
        o += rows

    delta, new_m, new_v = {}, {}, {}
    gate_maps = ("lru_wa", "lru_wx")
    for n in MATRICES + gate_maps:
        shape = w[n].shape
        flat = lambda a: a.reshape(-1, d if n in gate_maps else shape[-1])
        dl, nm, nv = _adamw(flat(w[n]), flat(g[n]), flat(mom[n]), flat(var[n]), f"adamw_{n}")
        delta[n], new_m[n], new_v[n] = dl.reshape(shape), nm.reshape(shape), nv.reshape(shape)
    vectors = tuple(n for n in REPLICATED if n not in gate_maps)
    for group, width, name in ((vectors, d, "adamw_replicated"), (SMALL_SHARDED, dd, "adamw_vectors")):
        cat = lambda src: _pad_rows(jnp.concatenate([src[n].reshape(-1, width) for n in group], axis=0))
        dl, nm, nv = _adamw(cat(w), cat(g), cat(mom), cat(var), name)
        o = 0
        for n in group:
            rows = w[n].size // width
            delta[n], new_m[n], new_v[n] = (a[o:o + rows].reshape(w[n].shape) for a in (dl, nm, nv))
            o += rows

    return (loss, grad_x, *[g[n] for n in ORDER], *[delta[n] for n in ORDER], *[new_m[n] for n in ORDER],
            *[new_v[n] for n in ORDER])
```

```python
import math

import jax
import jax.numpy as jnp
from jax import lax
from jax.experimental import pallas as pl
from jax.experimental.pallas import tpu as pltpu

F32 = jnp.float32
BF16 = jnp.bfloat16

N_DEV = 8
N_PROJ = 7
LRU_HEADS = 16
LRU_C = 8.0
RMS_EPS = 1e-6
CONV_A_K = 3
CONV_B_K = 4
GELU_C = math.sqrt(2.0 / math.pi)
GELU_A = 0.044715

ADAM_LR = 0.001
ADAM_B1 = 0.9
ADAM_B2 = 0.999
ADAM_EPS = 1e-08
ADAM_WD = 0.01
ADAM_STEP = 10

LANE = 128
SUBLANE = 8
MXU_TILE = 256
VMEM_LIMIT = 52 << 20
ALL_PEERS = tuple(range(1, N_DEV))
NEAR_PEERS = (1, 2, 3, 4, 5)
FAR_PEERS = (6, 7)
MESH = pl.DeviceIdType.MESH
ANY = pl.BlockSpec(memory_space=pl.ANY)


def _dot_nn(a, b):
    return lax.dot_general(a, b, (((1,), (0,)), ((), ())), preferred_element_type=F32)


def _dot_nt(a, b):
    return lax.dot_general(a, b, (((1,), (1,)), ((), ())), preferred_element_type=F32)


def _dot_tn(a, b):
    return lax.dot_general(a, b, (((0,), (0,)), ((), ())), preferred_element_type=F32)


def _sigmoid(x):
    return 1.0 / (1.0 + jnp.exp(-x))


def _gelu_and_grad(x):
    x2 = x * x
    t = jnp.tanh(GELU_C * x * (1.0 + GELU_A * x2))
    g = 0.5 * x * (1.0 + t)
    dg = 0.5 * (1.0 + t) + 0.5 * x * (1.0 - t * t) * GELU_C * (1.0 + 3.0 * GELU_A * x2)
    return g, dg


def _zero(ref):
    ref[...] = jnp.zeros(ref.shape, ref.dtype)


def _fit_rows(r, row_bytes, budget=1 << 20):
    fits = [t for t in range(16, r + 1, 16) if r % t == 0 and t * row_bytes <= budget]
    return max(fits) if fits else r


def _row_tile(t, want):
    tm = min(want, t // 2)
    assert t % tm == 0 and tm % SUBLANE == 0, (t, tm)
    return tm


def _params(n_grid=1, **kw):
    return pltpu.CompilerParams(dimension_semantics=("arbitrary",) * n_grid, vmem_limit_bytes=VMEM_LIMIT, **kw)


class _Group:
    def __init__(self, keys, rows):
        self.keys = keys
        self.rows = dict(zip(keys, rows))
        self.off, o = {}, 0
        for k in keys:
            self.off[k] = o
            o += self.rows[k]
        self.total = o


def _groups(d, ff):
    dd, ffs = d // N_DEV, ff // N_DEV
    return {"in": _Group(("in",), (N_PROJ * dd,)),
            "rest": _Group(("oa", "ob", "o", "g", "u", "d"), (dd, dd, dd, ffs, ffs, ffs))}


def _load_weights(g_ref, grp, keys, dsts, sems):
    copies = []
    for n, (k, dst) in enumerate(zip(keys, dsts)):
        rows, off = grp.rows[k], grp.off[k]
        copies += [pltpu.make_async_copy(g_ref.at[p, pl.ds(off, rows), :], dst.at[pl.ds(p * rows, rows), :],
                                         sems.at[n * N_DEV + p]) for p in range(N_DEV)]
    for c in copies:
        c.start()
    for c in copies:
        c.wait()


def _comm_sems():
    return [pltpu.SemaphoreType.DMA((N_DEV - 1,)), pltpu.SemaphoreType.DMA((N_DEV - 1,)), pltpu.SemaphoreType.DMA]


class _Gather:
    def __init__(self, x):
        self.inputs = [x]
        self.out_shape = [jax.ShapeDtypeStruct((N_DEV,) + x.shape, x.dtype)]
        self.scratch = _comm_sems()

    def _plan(self, ins, outs, scr):
        (x_ref,), (out_ref,), (send_sems, recv_sems, local_sem) = ins, outs, scr
        mx, my, mc = lax.axis_index("x"), lax.axis_index("y"), lax.axis_index("c")
        me, sibling = (mx, my, mc), (mx, my, 1 - mc)
        chips = [(1 - mx, my), (mx, 1 - my), (1 - mx, 1 - my)]

        def slot(px, py, pc):
            return out_ref.at[4 * px + 2 * py + pc]

        def copy(k, block, to, src=None):
            return pltpu.make_async_remote_copy(
                src_ref=slot(*block) if src is None else src, dst_ref=slot(*block),
                send_sem=send_sems.at[k], recv_sem=recv_sems.at[k], device_id=to, device_id_type=MESH)

        mine = lambda: pltpu.make_async_copy(x_ref, slot(*me), local_sem)
        first = [lambda: copy(0, me, sibling, src=x_ref)]
        first += [lambda j=j, chip=chip: copy(1 + j, me, (*chip, mc), src=x_ref) for j, chip in enumerate(chips)]
        landed = [lambda j=j, chip=chip: copy(1 + j, (*chip, mc), me) for j, chip in enumerate(chips)]
        passed = [lambda j=j, chip=chip: copy(4 + j, (*chip, mc), sibling) for j, chip in enumerate(chips)]
        from_sibling = [lambda: copy(0, sibling, me)]
        from_sibling += [lambda j=j, chip=chip: copy(4 + j, (*chip, 1 - mc), me) for j, chip in enumerate(chips)]
        return mine, first, landed, passed, from_sibling

    def start(self, ins, outs, scr):
        mine, first, _, _, _ = self._plan(ins, outs, scr)
        mine().start()
        for cp in first:
            cp().start()

    def mid(self, ins, outs, scr):
        _, _, landed, passed, _ = self._plan(ins, outs, scr)
        for got, fwd in zip(landed, passed):
            got().wait_recv()
            fwd().start()

    def finish(self, ins, outs, scr):
        mine, first, _, passed, from_sibling = self._plan(ins, outs, scr)
        for cp in from_sibling:
            cp().wait_recv()
        for cp in first + passed:
            cp().wait_send()
        mine().wait()


class _Exchange:
    def __init__(self, mats, grp, peers=ALL_PEERS, local=True):
        self.grp, self.peers, self.local = grp, tuple(peers), local
        self.inputs = [mats[k] for k in grp.keys]
        slots = len(self.peers) + (1 if local else 0)
        self.out_shape = [jax.ShapeDtypeStruct((slots, grp.total, self.inputs[0].shape[1]), BF16)]
        self.scratch = [pltpu.SemaphoreType.DMA((len(self.peers),)), pltpu.SemaphoreType.DMA((len(self.peers),)),
                        pltpu.SemaphoreType.DMA]

    def _pieces(self, g_refs, out_ref, q, dst_slot):
        out = []
        for g_ref, k in zip(g_refs, self.grp.keys):
            rows = self.grp.rows[k]
            out.append((g_ref.at[pl.ds(pl.multiple_of(q * rows, 16), rows), :],
                        out_ref.at[dst_slot, pl.ds(self.grp.off[k], rows), :]))
        return out

    def start(self, ins, outs, scr):
        (out_ref,), (send_sems, recv_sems, local_sem) = outs, scr
        mx, my, mc = lax.axis_index("x"), lax.axis_index("y"), lax.axis_index("c")
        if self.local:
            for s, t in self._pieces(ins, out_ref, 4 * mx + 2 * my + mc, 0):
                pltpu.make_async_copy(s, t, local_sem).start()
        for n, k in enumerate(self.peers):
            px, py, pc = mx ^ ((k >> 2) & 1), my ^ ((k >> 1) & 1), mc ^ (k & 1)
            for s, t in self._pieces(ins, out_ref, 4 * px + 2 * py + pc, n + (1 if self.local else 0)):
                pltpu.make_async_remote_copy(src_ref=s, dst_ref=t, send_sem=send_sems.at[n], recv_sem=recv_sems.at[n],
                                             device_id=(px, py, pc), device_id_type=MESH).start()

    def mid(self, ins, outs, scr):
        pass

    def finish(self, ins, outs, scr):
        (out_ref,), (send_sems, recv_sems, local_sem) = outs, scr
        mx, my, mc = lax.axis_index("x"), lax.axis_index("y"), lax.axis_index("c")
        whole = out_ref.at[0]
        for n in range(len(self.peers)):
            done = pltpu.make_async_remote_copy(src_ref=whole, dst_ref=whole, send_sem=send_sems.at[n],
                                                recv_sem=recv_sems.at[n], device_id=(mx, my, mc), device_id_type=MESH)
            done.wait_send()
            done.wait_recv()
        if self.local:
            pltpu.make_async_copy(whole, whole, local_sem).wait()


def _split(refs, sizes):
    out, pos = [], 0
    for n in sizes:
        out.append(refs[pos:pos + n])
        pos += n
    return out


def _hosted_call(body, comms, *, name, grid, in_specs, out_specs, out_shape, scratch_shapes, args, aliases=None):
    n_steps = grid[0]
    nc = len(comms)
    sizes = ([len(in_specs)] + [len(c.inputs) for c in comms] + [len(out_specs)] + [len(c.out_shape) for c in comms]
             + [len(scratch_shapes)] + [len(c.scratch) for c in comms])

    def hosted(*refs):
        parts = _split(refs, sizes)
        ins, c_ins = parts[0], parts[1:1 + nc]
        outs, c_outs = parts[1 + nc], parts[2 + nc:2 + 2 * nc]
        scr, c_scr = parts[2 + 2 * nc], parts[3 + 2 * nc:]
        step = pl.program_id(0)
        if comms:
            @pl.when(step == 0)
            def _():
                for c, a, b, s in zip(comms, c_ins, c_outs, c_scr):
                    c.start(a, b, s)

            @pl.when(step == max(n_steps - 2, 0))
            def _():
                for c, a, b, s in zip(comms, c_ins, c_outs, c_scr):
                    c.mid(a, b, s)

        body(*ins, *outs, *scr)
        if comms:
            @pl.when(step == n_steps - 1)
            def _():
                for c, a, b, s in zip(comms, c_ins, c_outs, c_scr):
                    c.finish(a, b, s)

    res = pl.pallas_call(
        hosted, name=name, grid=grid,
        out_shape=[*out_shape, *[o for c in comms for o in c.out_shape]],
        in_specs=[*in_specs, *[ANY for c in comms for _ in c.inputs]],
        out_specs=[*out_specs, *[ANY for c in comms for _ in c.out_shape]],
        scratch_shapes=[*scratch_shapes, *[s for c in comms for s in c.scratch]],
        input_output_aliases=aliases or {},
        compiler_params=_params(),
    )(*args, *[a for c in comms for a in c.inputs])
    main, rest = res[:len(out_specs)], res[len(out_specs):]
    return main, _split(rest, [len(c.out_shape) for c in comms])


def _comm_call(comms, name):
    sizes = [len(c.inputs) for c in comms] + [len(c.out_shape) for c in comms] + [len(c.scratch) for c in comms]
    nc = len(comms)

    def body(*refs):
        parts = _split(refs, sizes)
        triples = list(zip(comms, parts[:nc], parts[nc:2 * nc], parts[2 * nc:]))
        for phase in ("start", "mid", "finish"):
            for c, ins, outs, scr in triples:
                getattr(c, phase)(ins, outs, scr)

    res = pl.pallas_call(
        body, name=name, out_shape=[o for c in comms for o in c.out_shape],
        in_specs=[ANY for c in comms for _ in c.inputs], out_specs=[ANY for c in comms for _ in c.out_shape],
        scratch_shapes=[s for c in comms for s in c.scratch],
    )(*[a for c in comms for a in c.inputs])
    return _split(res, [len(c.out_shape) for c in comms])


def _sum_slots(xs, name):
    _, r, c = xs[0].shape
    tr = _fit_rows(r, c * 4)

    def body(*refs):
        acc = None
        for x_ref in refs[:-1]:
            for p in range(x_ref.shape[0]):
                v = x_ref[p].astype(F32)
                acc = v if acc is None else acc + v
        refs[-1][...] = acc

    return pl.pallas_call(
        body, name=name, grid=(r // tr,),
        out_shape=jax.ShapeDtypeStruct((r, c), F32),
        in_specs=[pl.BlockSpec((x.shape[0], tr, c), lambda i: (0, i, 0)) for x in xs],
        out_specs=pl.BlockSpec((tr, c), lambda i: (i, 0)),
        compiler_params=_params(),
    )(*xs)


def _time_tile(t):
    return _row_tile(t, 256)


def _to_tile_order(a, tm):
    t, c = a.shape
    return jnp.swapaxes(a.reshape(t // tm, SUBLANE, tm // SUBLANE, c), 1, 2).reshape(t, c)


def _from_tile_order(a, tm):
    t, c = a.shape
    return jnp.swapaxes(a.reshape(t // tm, tm // SUBLANE, SUBLANE, c), 1, 2).reshape(t, c)


def _causal_fill(buf, v, prev_tail, n, row):
    tm = v.shape[0]
    for q in range(n):
        cur = v[tm - SUBLANE * (n - q):tm - SUBLANE * (n - q - 1), :]
        prv = prev_tail[SUBLANE * q:SUBLANE * (q + 1), :]
        buf[SUBLANE * q:SUBLANE * (q + 1), :] = jnp.where(row == 0, pltpu.roll(prv, 1, 0), pltpu.roll(cur, 1, 0))
    buf[SUBLANE * n:, :] = v


def _anticausal_fill(buf, v, next_head, n, row):
    tm = v.shape[0]
    buf[0:tm, :] = v
    for q in range(n):
        cur = v[SUBLANE * q:SUBLANE * (q + 1), :]
        nxt = next_head[SUBLANE * q:SUBLANE * (q + 1), :]
        buf[tm + SUBLANE * q:tm + SUBLANE * (q + 1), :] = jnp.where(
            row == SUBLANE - 1, pltpu.roll(nxt, SUBLANE - 1, 0), pltpu.roll(cur, SUBLANE - 1, 0))


def _chain_scan(abuf, bbuf, nk, reverse):
    cw = abuf.shape[1]

    def step(n, carry):
        h, c = carry
        r0 = pl.multiple_of((nk - 1 - n if reverse else n) * SUBLANE, SUBLANE)
        ak = abuf[pl.ds(r0, SUBLANE), :]
        h = ak * h + bbuf[pl.ds(r0, SUBLANE), :]
        c = ak * c
        bbuf[pl.ds(r0, SUBLANE), :] = h
        abuf[pl.ds(r0, SUBLANE), :] = c
        return h, c

    return lax.fori_loop(0, nk, step, (jnp.zeros((SUBLANE, cw), F32), jnp.ones((SUBLANE, cw), F32)), unroll=True)


def _sublane_scan(a, b, row, reverse):
    for sh in (1, 2, 4):
        if reverse:
            m = row < SUBLANE - sh
            b = jnp.where(m, a * pltpu.roll(b, SUBLANE - sh, 0) + b, b)
            a = jnp.where(m, a * pltpu.roll(a, SUBLANE - sh, 0), a)
        else:
            m = row >= sh
            b = jnp.where(m, a * pltpu.roll(b, sh, 0) + b, b)
            a = jnp.where(m, a * pltpu.roll(a, sh, 0), a)
    return a, b


def _lru_gates(ub, bda, bdx, ba, bx, sp):
    r = _sigmoid(_dot_nn(ub, bda) + ba)
    i = _sigmoid(_dot_nn(ub, bdx) + bx)
    log_a = (-LRU_C) * r * sp
    a = jnp.exp(log_a)
    s2 = -jnp.tanh(log_a) * (1.0 + a * a)
    inv_s = lax.rsqrt(s2)
    s = jnp.where(s2 > 0.0, s2 * inv_s, 0.0)
    return r, i, a, s, inv_s


def _in_proj_mixer_fwd(x, g_row, gath, grp, caw, cbw, vec, bda, bdx, name, comms=()):
    t, d = x.shape
    n_in = N_PROJ * d
    tm = _time_tile(t)
    nk = tm // SUBLANE
    cw = min(MXU_TILE, d)
    nb = d // cw

    def body(x_ref, g_ref, gath_ref, caw_ref, cbw_ref, vec_ref, bda_ref, bdx_ref,
             p_ref, h1_ref, ya_ref, yb_ref, u_ref, h_ref, w_in, sems, zbuf, xbuf, abuf, bbuf, z_tail, x_tail, h_carry):
        @pl.when(pl.program_id(0) == 0)
        def _():
            _load_weights(gath_ref, grp, ["in"], [w_in], sems)
            _zero(z_tail)
            _zero(x_tail)
            _zero(h_carry)

        xf = x_ref[...]
        rstd = lax.rsqrt(jnp.mean(xf * xf, axis=-1, keepdims=True) + RMS_EPS)
        h1 = (xf * rstd * g_ref[...]).astype(BF16)
        h1_ref[...] = h1
        for k in range(N_PROJ):
            p_ref[:, k * d:(k + 1) * d] = _dot_nt(h1, w_in[k * d:(k + 1) * d, :]).astype(BF16)

        row = lax.broadcasted_iota(jnp.int32, (SUBLANE, cw), 0)
        for j in range(nb):
            cs = slice(j * cw, (j + 1) * cw)
            ba_ref, ca_ref, xa_ref, xb_ref, gb_ref = (p_ref.at[:, k * d:(k + 1) * d] for k in range(5))
            z = ca_ref[:, cs].astype(F32) * xa_ref[:, cs].astype(F32)
            _causal_fill(zbuf, z, z_tail[:, cs], CONV_A_K - 1, row)
            z_tail[:, cs] = z[tm - (CONV_A_K - 1) * SUBLANE:, :]
            cz = caw_ref[0:1, cs] * zbuf[0:tm, :] + caw_ref[1:2, cs] * zbuf[SUBLANE:SUBLANE + tm, :] + caw_ref[2:3, cs] * z
            ya_ref[:, cs] = (ba_ref[:, cs].astype(F32) * cz).astype(BF16)
            xb = xb_ref[:, cs].astype(F32)
            _causal_fill(xbuf, xb, x_tail[:, cs], CONV_B_K - 1, row)
            x_tail[:, cs] = xb[tm - (CONV_B_K - 1) * SUBLANE:, :]
            u = (cbw_ref[0:1, cs] * xbuf[0:tm, :] + cbw_ref[1:2, cs] * xbuf[SUBLANE:SUBLANE + tm, :]
                 + cbw_ref[2:3, cs] * xbuf[2 * SUBLANE:2 * SUBLANE + tm, :] + cbw_ref[3:4, cs] * xb + vec_ref[0:1, cs])
            ub = u.astype(BF16)
            u = ub.astype(F32)
            _, gi, a, s, _ = _lru_gates(ub, bda_ref[j], bdx_ref[j], vec_ref[1:2, cs], vec_ref[2:3, cs], vec_ref[3:4, cs])
            abuf[...] = a
            bbuf[...] = s * (gi * u)
            h_end, a_prod = _chain_scan(abuf, bbuf, nk, reverse=False)
            a_inc, h_inc = _sublane_scan(a_prod, h_end, row, reverse=False)
            carry = h_carry[:, cs]
            ends = h_inc + a_inc * carry
            starts = jnp.where(row == 0, carry, pltpu.roll(ends, 1, 0))
            h_carry[:, cs] = jnp.broadcast_to(ends[SUBLANE - 1:SUBLANE, :], (SUBLANE, cw))
            h = (bbuf[...].reshape(nk, SUBLANE, cw) + abuf[...].reshape(nk, SUBLANE, cw) * starts[None]).reshape(tm, cw)
            gel, _ = _gelu_and_grad(gb_ref[:, cs].astype(F32))
            yb_ref[:, cs] = (h * gel).astype(BF16)
            u_ref[:, cs] = ub
            h_ref[:, cs] = h.astype(BF16)

    small = pl.BlockSpec((SUBLANE, d), lambda i: (0, 0))
    bd = pl.BlockSpec((nb, cw, cw), lambda i: (0, 0, 0))
    row_spec = pl.BlockSpec((tm, d), lambda i: (i, 0))
    return _hosted_call(
        body, comms, name=name, grid=(t // tm,),
        out_shape=[jax.ShapeDtypeStruct((t, n_in), BF16)] + [jax.ShapeDtypeStruct((t, d), BF16)] * 5,
        in_specs=[row_spec, pl.BlockSpec((1, d), lambda i: (0, 0)), ANY, small, small, small, bd, bd],
        out_specs=[pl.BlockSpec((tm, n_in), lambda i: (i, 0))] + [row_spec] * 5,
        scratch_shapes=[pltpu.VMEM((n_in, d), BF16), pltpu.SemaphoreType.DMA((N_DEV,)),
                        pltpu.VMEM((tm + (CONV_A_K - 1) * SUBLANE, cw), F32), pltpu.VMEM((tm + (CONV_B_K - 1) * SUBLANE, cw), F32),
                        pltpu.VMEM((tm, cw), F32), pltpu.VMEM((tm, cw), F32),
                        pltpu.VMEM(((CONV_A_K - 1) * SUBLANE, d), F32), pltpu.VMEM(((CONV_B_K - 1) * SUBLANE, d), F32),
                        pltpu.VMEM((SUBLANE, d), F32)],
        args=(x, g_row, gath, caw, cbw, vec, bda, bdx))


def _merge_fwd(x, ya, yb, p, gbias, gath, grp, name, comms=()):
    t, d = x.shape
    tm = _row_tile(t, 512)

    def body(x_ref, ya_ref, yb_ref, ga_ref, gb_ref, gbias_ref, gath_ref, x1_ref, oa_ref, ob_ref, w_oa, w_ob, w_o, sems):
        @pl.when(pl.program_id(0) == 0)
        def _():
            _load_weights(gath_ref, grp, ["oa", "ob", "o"], [w_oa, w_ob, w_o], sems)

        oa = _dot_nn(ya_ref[...], w_oa[...]).astype(BF16)
        ob = _dot_nn(yb_ref[...], w_ob[...]).astype(BF16)
        oa_ref[...] = oa
        ob_ref[...] = ob
        sa = _sigmoid(ga_ref[...] + gbias_ref[0:1, :].astype(BF16))
        sb = _sigmoid(gb_ref[...] + gbias_ref[1:2, :].astype(BF16))
        x1_ref[...] = x_ref[...] + _dot_nn(sa * oa + sb * ob, w_o[...])

    row = pl.BlockSpec((tm, d), lambda i: (i, 0))
    return _hosted_call(
        body, comms, name=name, grid=(t // tm,),
        out_shape=[jax.ShapeDtypeStruct((t, d), F32), jax.ShapeDtypeStruct((t, d), BF16), jax.ShapeDtypeStruct((t, d), BF16)],
        in_specs=[row, row, row, pl.BlockSpec((tm, d), lambda i: (i, 5)), pl.BlockSpec((tm, d), lambda i: (i, 6)),
                  pl.BlockSpec((SUBLANE, d), lambda i: (0, 0)), ANY],
        out_specs=[row, row, row],
        scratch_shapes=[pltpu.VMEM((d, d), BF16)] * 3 + [pltpu.SemaphoreType.DMA((3 * N_DEV,))],
        args=(x, ya, yb, p, p, gbias, gath))


def _ffn_fwd(x1, g_row, gath, grp, name, comms=()):
    t, d = x1.shape
    ff = grp.rows["g"] * N_DEV
    tm = _row_tile(t, 512)
    fc = MXU_TILE
    assert ff % fc == 0

    def body(x_ref, g_ref, gath_ref, x2_ref, gg_ref, uu_ref, w_g, w_u, w_d, acc, sems):
        @pl.when(pl.program_id(0) == 0)
        def _():
            _load_weights(gath_ref, grp, ["g", "u", "d"], [w_g, w_u, w_d], sems)

        xf = x_ref[...]
        rstd = lax.rsqrt(jnp.mean(xf * xf, axis=-1, keepdims=True) + RMS_EPS)
        h = (xf * rstd * g_ref[...]).astype(BF16)
        acc[...] = xf
        for c in range(ff // fc):
            fs = slice(c * fc, (c + 1) * fc)
            gg = _dot_nt(h, w_g[fs, :]).astype(BF16)
            uu = _dot_nt(h, w_u[fs, :]).astype(BF16)
            gg_ref[:, fs] = gg
            uu_ref[:, fs] = uu
            acc[...] += _dot_nn(gg * _sigmoid(gg) * uu, w_d[fs, :])
        x2_ref[...] = acc[...]

    row = pl.BlockSpec((tm, d), lambda i: (i, 0))
    wide = pl.BlockSpec((tm, ff), lambda i: (i, 0))
    return _hosted_call(
        body, comms, name=name, grid=(t // tm,),
        out_shape=[jax.ShapeDtypeStruct((t, d), F32), jax.ShapeDtypeStruct((t, ff), BF16), jax.ShapeDtypeStruct((t, ff), BF16)],
        in_specs=[row, pl.BlockSpec((1, d), lambda i: (0, 0)), ANY],
        out_specs=[row, wide, wide],
        scratch_shapes=[pltpu.VMEM((ff, d), BF16)] * 3 + [pltpu.VMEM((tm, d), F32), pltpu.SemaphoreType.DMA((3 * N_DEV,))],
        args=(x1, g_row, gath))


def _loss_head(x, g_row, target, name):
    t, d = x.shape
    tm = _row_tile(t, 512)

    def body(x_ref, g_ref, tgt_ref, loss_ref, dx_ref, dg_ref):
        @pl.when(pl.program_id(0) == 0)
        def _():
            _zero(loss_ref)
            _zero(dg_ref)

        xf = x_ref[...]
        rstd = lax.rsqrt(jnp.mean(xf * xf, axis=-1, keepdims=True) + RMS_EPS)
        xh = xf * rstd
        g = g_ref[...]
        err = xh * g - tgt_ref[...]
        loss_ref[...] += 0.5 * jnp.sum(jnp.sum(err * err, axis=-1, keepdims=True), axis=0, keepdims=True) * (1.0 / d)
        dy = err * (1.0 / d)
        dg_ref[0:1, :] += jnp.sum(dy * xh, axis=0, keepdims=True)
        dxh = dy * g
        dx_ref[...] = rstd * (dxh - xh * jnp.mean(dxh * xh, axis=-1, keepdims=True))

    row = pl.BlockSpec((tm, d), lambda i: (i, 0))
    return pl.pallas_call(
        body, name=name, grid=(t // tm,),
        out_shape=[jax.ShapeDtypeStruct((SUBLANE, LANE), F32), jax.ShapeDtypeStruct((t, d), F32),
                   jax.ShapeDtypeStruct((SUBLANE, d), F32)],
        in_specs=[row, pl.BlockSpec((1, d), lambda i: (0, 0)), row],
        out_specs=[pl.BlockSpec((SUBLANE, LANE), lambda i: (0, 0)), row, pl.BlockSpec((SUBLANE, d), lambda i: (0, 0))],
        compiler_params=_params(),
    )(x, g_row, target)


def _ffn_bwd_act(dx2, gg, uu, gath, grp, name, comms=()):
    t, d = dx2.shape
    ff = grp.rows["g"] * N_DEV
    tm = _row_tile(t, 512)
    fc = MXU_TILE
    n_t = t // tm

    def body(dx2_ref, gg_ref, uu_ref, gath_ref, dgg_ref, duu_ref, dwd_ref, w_d, acc, sems):
        @pl.when(pl.program_id(0) == 0)
        def _():
            _load_weights(gath_ref, grp, ["d"], [w_d], sems)
            _zero(acc)

        dx2b = dx2_ref[...].astype(BF16)
        for c in range(ff // fc):
            fs = slice(c * fc, (c + 1) * fc)
            df = _dot_nt(dx2b, w_d[fs, :]).astype(BF16)
            g = gg_ref[:, fs]
            u = uu_ref[:, fs]
            sg = _sigmoid(g)
            silu = g * sg
            acc[fs, :] += _dot_tn(silu * u, dx2b)
            duu_ref[:, fs] = df * silu
            dgg_ref[:, fs] = df * u * (sg * (1.0 + g * (1.0 - sg)))

        @pl.when(pl.program_id(0) == n_t - 1)
        def _():
            w_d[...] = acc[...].astype(BF16)
            out = pltpu.make_async_copy(w_d, dwd_ref, sems.at[0])
            out.start()
            out.wait()

    row = pl.BlockSpec((tm, d), lambda i: (i, 0))
    wide = pl.BlockSpec((tm, ff), lambda i: (i, 0))
    sd = jax.ShapeDtypeStruct
    return _hosted_call(
        body, comms, name=name, grid=(n_t,),
        out_shape=[sd((t, ff), BF16), sd((t, ff), BF16), sd((ff, d), BF16)],
        in_specs=[row, wide, wide, ANY],
        out_specs=[wide, wide, ANY],
        scratch_shapes=[pltpu.VMEM((ff, d), BF16), pltpu.VMEM((ff, d), F32), pltpu.SemaphoreType.DMA((N_DEV,))],
        args=(dx2, gg, uu, gath))


def _ffn_bwd_in(dgg, duu, dx2, x1, g_row, gath, grp, name, comms=()):
    t, d = x1.shape
    ff = grp.rows["g"] * N_DEV
    tm = _row_tile(t, 512)

    def body(dgg_ref, duu_ref, dx2_ref, x_ref, g_ref, gath_ref, dx1_ref, dx1b_ref, h_ref, dg_ref, w_g, w_u, sems):
        @pl.when(pl.program_id(0) == 0)
        def _():
            _load_weights(gath_ref, grp, ["g", "u"], [w_g, w_u], sems)
            _zero(dg_ref)

        dh = _dot_nn(dgg_ref[...], w_g[...]) + _dot_nn(duu_ref[...], w_u[...])
        xf = x_ref[...]
        rstd = lax.rsqrt(jnp.mean(xf * xf, axis=-1, keepdims=True) + RMS_EPS)
        xh = xf * rstd
        g = g_ref[...]
        h_ref[...] = (xh * g).astype(BF16)
        dg_ref[0:1, :] += jnp.sum(dh * xh, axis=0, keepdims=True)
        dxh = dh * g
        dx1 = dx2_ref[...] + rstd * (dxh - xh * jnp.mean(dxh * xh, axis=-1, keepdims=True))
        dx1_ref[...] = dx1
        dx1b_ref[...] = dx1.astype(BF16)

    row = pl.BlockSpec((tm, d), lambda i: (i, 0))
    wide = pl.BlockSpec((tm, ff), lambda i: (i, 0))
    sd = jax.ShapeDtypeStruct
    return _hosted_call(
        body, comms, name=name, grid=(t // tm,),
        out_shape=[sd((t, d), F32), sd((t, d), BF16), sd((t, d), BF16), sd((SUBLANE, d), F32)],
        in_specs=[wide, wide, row, row, pl.BlockSpec((1, d), lambda i: (0, 0)), ANY],
        out_specs=[row, row, row, pl.BlockSpec((SUBLANE, d), lambda i: (0, 0))],
        scratch_shapes=[pltpu.VMEM((ff, d), BF16)] * 2 + [pltpu.SemaphoreType.DMA((2 * N_DEV,))],
        args=(dgg, duu, dx2, x1, g_row, gath))


def _merge_bwd(dx1b, oa, ob, ya, yb, p, gbias, gath, grp, name, comms=()):
    t, d = oa.shape
    tm = _row_tile(t, 512)
    n_t = t // tm

    def body(dx_ref, oa_ref, ob_ref, ya_ref, yb_ref, ga_ref, gb_ref, gbias_ref, gath_ref,
             dya_ref, dyb_ref, dp_ref, dgb_ref, dwoa_ref, dwob_ref, dwo_ref,
             w_oa, w_ob, w_o, acc_oa, acc_ob, acc_o, stage, sems, out_sems):
        @pl.when(pl.program_id(0) == 0)
        def _():
            _load_weights(gath_ref, grp, ["oa", "ob", "o"], [w_oa, w_ob, w_o], sems)
            for ref in (dgb_ref, acc_oa, acc_ob, acc_o):
                _zero(ref)

        dxb = dx_ref[...]
        dm = _dot_nt(dxb, w_o[...]).astype(BF16)
        oa = oa_ref[...]
        ob = ob_ref[...]
        sa = _sigmoid(ga_ref[...] + gbias_ref[0:1, :].astype(BF16))
        sb = _sigmoid(gb_ref[...] + gbias_ref[1:2, :].astype(BF16))
        acc_o[...] += _dot_tn(sa * oa + sb * ob, dxb)
        doa = dm * sa
        dob = dm * sb
        acc_oa[...] += _dot_tn(ya_ref[...], doa)
        acc_ob[...] += _dot_tn(yb_ref[...], dob)
        dga = dm * oa * sa * (1.0 - sa)
        dgb = dm * ob * sb * (1.0 - sb)
        step = pl.program_id(0)
        slot = step % 2

        def to_dp(k, at_step):
            return pltpu.make_async_copy(stage.at[k], dp_ref.at[pl.ds(at_step * tm, tm), pl.ds(5 * d, 2 * d)], out_sems.at[k])

        @pl.when(step >= 2)
        def _():
            to_dp(slot, step - 2).wait()

        stage[slot, :, 0:d] = dga
        stage[slot, :, d:2 * d] = dgb
        to_dp(slot, step).start()
        ones = jnp.ones((SUBLANE, tm), BF16)
        dgb_ref[0:1, :] += _dot_nn(ones, dga)[0:1, :]
        dgb_ref[1:2, :] += _dot_nn(ones, dgb)[0:1, :]
        dya_ref[...] = _dot_nt(doa, w_oa[...]).astype(BF16)
        dyb_ref[...] = _dot_nt(dob, w_ob[...]).astype(BF16)

        @pl.when(pl.program_id(0) == n_t - 1)
        def _():
            outs = []
            for n, (acc, stage, dst) in enumerate(((acc_oa, w_oa, dwoa_ref), (acc_ob, w_ob, dwob_ref), (acc_o, w_o, dwo_ref))):
                stage[...] = acc[...].astype(BF16)
                outs.append(pltpu.make_async_copy(stage, dst, sems.at[n]))
                outs[-1].start()
            for cp in outs:
                cp.wait()
            for back in range(min(2, n_t)):
                to_dp((n_t - 1 - back) % 2, n_t - 1 - back).wait()

    row = pl.BlockSpec((tm, d), lambda i: (i, 0))
    sd = jax.ShapeDtypeStruct
    return _hosted_call(
        body, comms, name=name, grid=(n_t,),
        out_shape=[sd((t, d), BF16), sd((t, d), BF16), sd((t, N_PROJ * d), BF16), sd((SUBLANE, d), F32),
                   sd((d, d), BF16), sd((d, d), BF16), sd((d, d), BF16)],
        in_specs=[row, row, row, row, row, pl.BlockSpec((tm, d), lambda i: (i, 5)), pl.BlockSpec((tm, d), lambda i: (i, 6)),
                  pl.BlockSpec((SUBLANE, d), lambda i: (0, 0)), ANY],
        out_specs=[row, row, ANY, pl.BlockSpec((SUBLANE, d), lambda i: (0, 0)), ANY, ANY, ANY],
        scratch_shapes=[pltpu.VMEM((d, d), BF16)] * 3 + [pltpu.VMEM((d, d), F32)] * 3
        + [pltpu.VMEM((2, tm, 2 * d), BF16), pltpu.SemaphoreType.DMA((3 * N_DEV,)), pltpu.SemaphoreType.DMA((2,))],
        args=(dx1b, oa, ob, ya, yb, p, p, gbias, gath))


DV_CONV_B_B, DV_BA, DV_BX, DV_SP, DV_CONV_A, DV_CONV_B = 0, 1, 2, 3, 4, 7
DV_ROWS = 16


def _mixer_bwd(dya, dyb, dp_gates, p, u_s, h_s, caw, cbw, vec, bda, bdx, name, comms=()):
    t, d = dya.shape
    tm = _time_tile(t)
    n_t = t // tm
    nk = tm // SUBLANE
    cw = min(MXU_TILE, d)
    nb = d // cw
    halo = 4 * SUBLANE
    ka, kb = CONV_A_K - 1, CONV_B_K - 1

    def body(dya_ref, dyb_ref, _, ba_ref, ca_ref, xa_ref, xb_ref, gb_ref, cah_ref, xah_ref, xbh_ref,
             u_ref, h_ref, hh_ref, caw_ref, cbw_ref, vec_ref, bda_ref, bdx_ref,
             dp_ref, dv_ref, dwa_ref, dwx_ref,
             zbuf, xbuf, hbuf, dczbuf, dubuf, a2buf, a1buf, lbuf, dcz_head, du_head, a_head, lam_head):
        i = pl.program_id(0)

        @pl.when(i == 0)
        def _():
            for ref in (dv_ref, dwa_ref, dwx_ref, dcz_head, du_head, a_head, lam_head):
                _zero(ref)

        has_prev = jnp.where(i < n_t - 1, 1.0, 0.0).astype(F32)
        row = lax.broadcasted_iota(jnp.int32, (SUBLANE, cw), 0)

        def colsum(v):
            return jnp.sum(v, axis=0, keepdims=True)

        for j in range(nb):
            cs = slice(j * cw, (j + 1) * cw)
            ca = ca_ref[:, cs].astype(F32)
            xa = xa_ref[:, cs].astype(F32)
            z = ca * xa
            z_before = cah_ref[:, cs].astype(F32) * xah_ref[:, cs].astype(F32) * has_prev
            _causal_fill(zbuf, z, z_before[halo - ka * SUBLANE:, :], ka, row)
            z2 = zbuf[0:tm, :]
            z1 = zbuf[SUBLANE:SUBLANE + tm, :]
            w0, w1, w2 = caw_ref[0:1, cs], caw_ref[1:2, cs], caw_ref[2:3, cs]
            cz = w0 * z2 + w1 * z1 + w2 * z
            dya = dya_ref[:, cs].astype(F32)
            dp_ref[:, 0 * d + j * cw:0 * d + (j + 1) * cw] = (dya * cz).astype(BF16)
            dcz = dya * ba_ref[:, cs].astype(F32)
            _anticausal_fill(dczbuf, dcz, dcz_head[:, cs], ka, row)
            dcz_head[:, cs] = dcz[0:ka * SUBLANE, :]
            dz = w2 * dcz + w1 * dczbuf[SUBLANE:SUBLANE + tm, :] + w0 * dczbuf[2 * SUBLANE:2 * SUBLANE + tm, :]
            dv_ref[DV_CONV_A + 0:DV_CONV_A + 1, cs] += colsum(dcz * z2)
            dv_ref[DV_CONV_A + 1:DV_CONV_A + 2, cs] += colsum(dcz * z1)
            dv_ref[DV_CONV_A + 2:DV_CONV_A + 3, cs] += colsum(dcz * z)
            dp_ref[:, 1 * d + j * cw:1 * d + (j + 1) * cw] = (dz * xa).astype(BF16)
            dp_ref[:, 2 * d + j * cw:2 * d + (j + 1) * cw] = (dz * ca).astype(BF16)
            h = h_ref[:, cs].astype(F32)
            h_before = hh_ref[:, cs].astype(F32) * has_prev
            _causal_fill(hbuf, h, h_before[halo - SUBLANE:, :], 1, row)
            h_prev = hbuf[0:tm, :]
            dyb = dyb_ref[:, cs].astype(F32)
            gel, dgel = _gelu_and_grad(gb_ref[:, cs].astype(F32))
            dp_ref[:, 4 * d + j * cw:4 * d + (j + 1) * cw] = (dyb * h * dgel).astype(BF16)
            ub = u_ref[:, cs]
            u = ub.astype(F32)
            sp = vec_ref[3:4, cs]
            r, gi, a, s, inv_s = _lru_gates(ub, bda_ref[j], bdx_ref[j], vec_ref[1:2, cs], vec_ref[2:3, cs], sp)
            _anticausal_fill(a2buf, a, a_head[:, cs], 1, row)
            a_head[:, cs] = a[0:SUBLANE, :]
            a1buf[...] = a2buf[SUBLANE:SUBLANE + tm, :]
            lbuf[...] = dyb * gel
            l_end, a_prod = _chain_scan(a1buf, lbuf, nk, reverse=True)
            a_inc, l_inc = _sublane_scan(a_prod, l_end, row, reverse=True)
            carry = lam_head[:, cs]
            ends = l_inc + a_inc * carry
            starts = jnp.where(row == SUBLANE - 1, carry, pltpu.roll(ends, SUBLANE - 1, 0))
            lam_head[:, cs] = jnp.broadcast_to(ends[0:1, :], (SUBLANE, cw))
            lam = (lbuf[...].reshape(nk, SUBLANE, cw) + a1buf[...].reshape(nk, SUBLANE, cw) * starts[None]).reshape(tm, cw)
            da = lam * h_prev
            iu = gi * u
            ds = lam * iu
            di = lam * s * u
            du = lam * s * gi
            dlog_a = da * a - ds * (a * a) * inv_s
            dv_ref[DV_SP:DV_SP + 1, cs] += colsum(dlog_a * r) * (-LRU_C)
            dpr = dlog_a * ((-LRU_C) * sp) * r * (1.0 - r)
            dpi = di * gi * (1.0 - gi)
            dv_ref[DV_BA:DV_BA + 1, cs] += colsum(dpr)
            dv_ref[DV_BX:DV_BX + 1, cs] += colsum(dpi)
            dprb = dpr.astype(BF16)
            dpib = dpi.astype(BF16)
            du = du + _dot_nt(dprb, bda_ref[j]) + _dot_nt(dpib, bdx_ref[j])
            dwa_ref[j] += _dot_tn(ub, dprb)
            dwx_ref[j] += _dot_tn(ub, dpib)
            xb = xb_ref[:, cs].astype(F32)
            x_before = xbh_ref[:, cs].astype(F32) * has_prev
            _causal_fill(xbuf, xb, x_before[halo - kb * SUBLANE:, :], kb, row)
            _anticausal_fill(dubuf, du, du_head[:, cs], kb, row)
            du_head[:, cs] = du[0:kb * SUBLANE, :]
            v0, v1, v2, v3 = cbw_ref[0:1, cs], cbw_ref[1:2, cs], cbw_ref[2:3, cs], cbw_ref[3:4, cs]
            dxb = (v3 * du + v2 * dubuf[SUBLANE:SUBLANE + tm, :] + v1 * dubuf[2 * SUBLANE:2 * SUBLANE + tm, :]
                   + v0 * dubuf[3 * SUBLANE:3 * SUBLANE + tm, :])
            dp_ref[:, 3 * d + j * cw:3 * d + (j + 1) * cw] = dxb.astype(BF16)
            dv_ref[DV_CONV_B_B:DV_CONV_B_B + 1, cs] += colsum(du)
            dv_ref[DV_CONV_B + 0:DV_CONV_B + 1, cs] += colsum(du * xbuf[0:tm, :])
            dv_ref[DV_CONV_B + 1:DV_CONV_B + 2, cs] += colsum(du * xbuf[SUBLANE:SUBLANE + tm, :])
            dv_ref[DV_CONV_B + 2:DV_CONV_B + 3, cs] += colsum(du * xbuf[2 * SUBLANE:2 * SUBLANE + tm, :])
            dv_ref[DV_CONV_B + 3:DV_CONV_B + 4, cs] += colsum(du * xb)

    rt = lambda i: n_t - 1 - i
    row_spec = pl.BlockSpec((tm, d), lambda i: (rt(i), 0))
    slab = lambda s: pl.BlockSpec((tm, d), lambda i, s=s: (rt(i), s))
    before = lambda s: pl.BlockSpec((halo, d), lambda i, s=s: (jnp.maximum(rt(i) * (tm // halo) - 1, 0), s))
    small = pl.BlockSpec((SUBLANE, d), lambda i: (0, 0))
    bd = pl.BlockSpec((nb, cw, cw), lambda i: (0, 0, 0))
    sd = jax.ShapeDtypeStruct
    wbuf = lambda n: pltpu.VMEM((tm + n * SUBLANE, cw), F32)
    head = lambda n: pltpu.VMEM((n * SUBLANE, d), F32)
    return _hosted_call(
        body, comms, name=name, grid=(n_t,),
        out_shape=[sd((t, N_PROJ * d), BF16), sd((DV_ROWS, d), F32), sd((nb, cw, cw), F32), sd((nb, cw, cw), F32)],
        in_specs=[row_spec, row_spec, ANY,
                  slab(0), slab(1), slab(2), slab(3), slab(4), before(1), before(2), before(3),
                  row_spec, row_spec, before(0), small, small, small, bd, bd],
        out_specs=[pl.BlockSpec((tm, 5 * d), lambda i: (rt(i), 0)), pl.BlockSpec((DV_ROWS, d), lambda i: (0, 0)), bd, bd],
        aliases={2: 0},
        scratch_shapes=[wbuf(ka), wbuf(kb), wbuf(1), wbuf(ka), wbuf(kb), wbuf(1),
                        pltpu.VMEM((tm, cw), F32), pltpu.VMEM((tm, cw), F32), head(ka), head(kb), head(1), head(1)],
        args=(dya, dyb, dp_gates, p, p, p, p, p, p, p, p, u_s, h_s, h_s, caw, cbw, vec, bda, bdx))


def _in_proj_bwd(dp, x, dx1, g_row, gath, grp, name, comms=()):
    t, d = x.shape
    tm = _row_tile(t, 512)
    n_in = N_PROJ * d

    def body(dp_ref, x_ref, dx1_ref, g_ref, gath_ref, dx_ref, dg_ref, w_in, sems):
        @pl.when(pl.program_id(0) == 0)
        def _():
            _load_weights(gath_ref, grp, ["in"], [w_in], sems)
            _zero(dg_ref)

        dh = _dot_nn(dp_ref[:, 0:d], w_in[0:d, :])
        for k in range(1, N_PROJ):
            dh = dh + _dot_nn(dp_ref[:, k * d:(k + 1) * d], w_in[k * d:(k + 1) * d, :])
        xf = x_ref[...]
        rstd = lax.rsqrt(jnp.mean(xf * xf, axis=-1, keepdims=True) + RMS_EPS)
        xh = xf * rstd
        g = g_ref[...]
        dg_ref[0:1, :] += jnp.sum(dh * xh, axis=0, keepdims=True)
        dxh = dh * g
        dx_ref[...] = dx1_ref[...] + rstd * (dxh - xh * jnp.mean(dxh * xh, axis=-1, keepdims=True))

    row = pl.BlockSpec((tm, d), lambda i: (i, 0))
    sd = jax.ShapeDtypeStruct
    return _hosted_call(
        body, comms, name=name, grid=(t // tm,),
        out_shape=[sd((t, d), F32), sd((SUBLANE, d), F32)],
        in_specs=[pl.BlockSpec((tm, n_in), lambda i: (i, 0)), row, row, pl.BlockSpec((1, d), lambda i: (0, 0)), ANY],
        out_specs=[row, pl.BlockSpec((SUBLANE, d), lambda i: (0, 0))],
        scratch_shapes=[pltpu.VMEM((n_in, d), BF16), pltpu.SemaphoreType.DMA((N_DEV,))],
        args=(dp, x, dx1, g_row, gath))


def _weight_grad(a, b, name):
    t, m = a.shape
    n = b.shape[1]
    bt = _row_tile(t, 1024)
    bm = m
    for div in (1, 2, 4, 8):
        if m % div == 0 and (m // div) % LANE == 0 and (m // div) * n * 4 <= (12 << 20):
            bm = m // div
            break
    n_t = t // bt

    def body(a_ref, b_ref, o_ref, acc):
        k = pl.program_id(1)

        @pl.when(k == 0)
        def _():
            _zero(acc)

        acc[...] += _dot_tn(a_ref[...], b_ref[...])

        @pl.when(k == n_t - 1)
        def _():
            o_ref[...] = acc[...].astype(BF16)

    return pl.pallas_call(
        body, name=name, grid=(m // bm, n_t),
        out_shape=jax.ShapeDtypeStruct((m, n), BF16),
        in_specs=[pl.BlockSpec((bt, bm), lambda i, k: (k, i)), pl.BlockSpec((bt, n), lambda i, k: (k, 0))],
        out_specs=pl.BlockSpec((bm, n), lambda i, k: (i, 0)),
        scratch_shapes=[pltpu.VMEM((bm, n), F32)],
        compiler_params=_params(2),
    )(a, b)


def _adamw(w, g, m, v, name):
    r, c = w.shape
    tr = _fit_rows(r, c * 4)
    c1 = 1.0 - ADAM_B1 ** ADAM_STEP
    c2 = 1.0 - ADAM_B2 ** ADAM_STEP

    def body(w_ref, g_ref, m_ref, v_ref, d_ref, nm_ref, nv_ref):
        g32 = g_ref[...]
        nm = ADAM_B1 * m_ref[...] + (1.0 - ADAM_B1) * g32
        nv = ADAM_B2 * v_ref[...] + (1.0 - ADAM_B2) * (g32 * g32)
        nm_ref[...] = nm
        nv_ref[...] = nv
        d_ref[...] = -ADAM_LR * ((nm / c1) / (jnp.sqrt(nv / c2) + ADAM_EPS) + ADAM_WD * w_ref[...])

    spec = pl.BlockSpec((tr, c), lambda i: (i, 0))
    return pl.pallas_call(
        body, name=name, grid=(r // tr,),
        out_shape=[jax.ShapeDtypeStruct((r, c), F32)] * 3,
        in_specs=[spec] * 4, out_specs=[spec] * 3,
        compiler_params=_params(),
    )(w, g, m, v)


def _pad_rows(a, mult=SUBLANE):
    pad = (-a.shape[0]) % mult
    return a if pad == 0 else jnp.concatenate([a, jnp.zeros((pad,) + a.shape[1:], a.dtype)], axis=0)


REPLICATED = ("ln1_g", "conv_b_b", "lru_wa", "lru_ba", "lru_wx", "lru_bx", "lru_lambda", "ln2_g", "final_g")
SMALL_SHARDED = ("conv_a_w", "conv_b_w", "gate_bias")
MATRICES = ("w_in", "w_out_a", "w_out_b", "w_o", "w_ffn_gate", "w_ffn_up", "w_ffn_down")
ORDER = ("ln1_g", "w_in", "conv_a_w", "conv_b_w", "conv_b_b", "lru_wa", "lru_ba", "lru_wx", "lru_bx", "lru_lambda",
         "w_out_a", "w_out_b", "gate_bias", "w_o", "ln2_g", "w_ffn_gate", "w_ffn_up", "w_ffn_down", "final_g")


def kernel(x, ln1_g, w_in, conv_a_w, conv_b_w, conv_b_b, lru_wa, lru_ba, lru_wx, lru_bx, lru_lambda, w_out_a, w_out_b, gate_bias, w_o, ln2_g, w_ffn_gate, w_ffn_up, w_ffn_down, final_g, loss_target, m_ln1_g, m_w_in, m_conv_a_w, m_conv_b_w, m_conv_b_b, m_lru_wa, m_lru_ba, m_lru_wx, m_lru_bx, m_lru_lambda, m_w_out_a, m_w_out_b, m_gate_bias, m_w_o, m_ln2_g, m_w_ffn_gate, m_w_ffn_up, m_w_ffn_down, m_final_g, v_ln1_g, v_w_in, v_conv_a_w, v_conv_b_w, v_conv_b_b, v_lru_wa, v_lru_ba, v_lru_wx, v_lru_bx, v_lru_lambda, v_w_out_a, v_w_out_b, v_gate_bias, v_w_o, v_ln2_g, v_w_ffn_gate, v_w_ffn_up, v_w_ffn_down, v_final_g):
    w = dict(ln1_g=ln1_g, w_in=w_in, conv_a_w=conv_a_w, conv_b_w=conv_b_w, conv_b_b=conv_b_b, lru_wa=lru_wa,
             lru_ba=lru_ba, lru_wx=lru_wx, lru_bx=lru_bx, lru_lambda=lru_lambda, w_out_a=w_out_a, w_out_b=w_out_b,
             gate_bias=gate_bias, w_o=w_o, ln2_g=ln2_g, w_ffn_gate=w_ffn_gate, w_ffn_up=w_ffn_up,
             w_ffn_down=w_ffn_down, final_g=final_g)
    mom = dict(ln1_g=m_ln1_g, w_in=m_w_in, conv_a_w=m_conv_a_w, conv_b_w=m_conv_b_w, conv_b_b=m_conv_b_b,
               lru_wa=m_lru_wa, lru_ba=m_lru_ba, lru_wx=m_lru_wx, lru_bx=m_lru_bx, lru_lambda=m_lru_lambda,
               w_out_a=m_w_out_a, w_out_b=m_w_out_b, gate_bias=m_gate_bias, w_o=m_w_o, ln2_g=m_ln2_g,
               w_ffn_gate=m_w_ffn_gate, w_ffn_up=m_w_ffn_up, w_ffn_down=m_w_ffn_down, final_g=m_final_g)
    var = dict(ln1_g=v_ln1_g, w_in=v_w_in, conv_a_w=v_conv_a_w, conv_b_w=v_conv_b_w, conv_b_b=v_conv_b_b,
               lru_wa=v_lru_wa, lru_ba=v_lru_ba, lru_wx=v_lru_wx, lru_bx=v_lru_bx, lru_lambda=v_lru_lambda,
               w_out_a=v_w_out_a, w_out_b=v_w_out_b, gate_bias=v_gate_bias, w_o=v_w_o, ln2_g=v_ln2_g,
               w_ffn_gate=v_w_ffn_gate, w_ffn_up=v_w_ffn_up, w_ffn_down=v_w_ffn_down, final_g=v_final_g)

    _, t, d = x.shape
    n_layers = w_in.shape[0]
    ff = w_ffn_down.shape[1] * N_DEV
    dd = d // N_DEV
    hd = d // LRU_HEADS
    cw = min(MXU_TILE, d)
    nb = d // cw
    hpt = cw // hd
    grp = _groups(d, ff)
    me = 4 * lax.axis_index("x") + 2 * lax.axis_index("y") + lax.axis_index("c")
    tm_time = _time_tile(t)
    x0 = _to_tile_order(x[0], tm_time)
    target = _to_tile_order(loss_target[0], tm_time)

    packed = [{"in": jnp.swapaxes(w_in[l], 0, 1).astype(BF16),
               "rest": jnp.concatenate([w_out_a[l], w_out_b[l], w_o[l], jnp.swapaxes(w_ffn_gate[l], 0, 1),
                                        jnp.swapaxes(w_ffn_up[l], 0, 1), w_ffn_down[l]], axis=0).astype(BF16)}
              for l in range(n_layers)]
    n_small = CONV_A_K + CONV_B_K + 2
    small = _pad_rows(jnp.concatenate([conv_a_w, conv_b_w, gate_bias], axis=1).reshape(n_layers * n_small, dd))
    sp = jax.nn.softplus(-lru_lambda)
    vec = [_pad_rows(jnp.stack([conv_b_b[l], lru_ba[l], lru_bx[l], sp[l]])) for l in range(n_layers)]
    eye = jnp.eye(hpt, dtype=F32)

    def block_diag(wh):
        return jnp.einsum("jkab,kl->jkalb", wh.reshape(nb, hpt, hd, hd), eye).reshape(nb, cw, cw).astype(BF16)

    bda = [block_diag(lru_wa[l]) for l in range(n_layers)]
    bdx = [block_diag(lru_wx[l]) for l in range(n_layers)]

    gath = [dict() for _ in range(n_layers)]
    (gath[0]["in"],), (small_g,) = _comm_call([_Gather(packed[0]["in"]), _Gather(small)], "gather_in_0")
    small_full = jnp.swapaxes(small_g[:, :n_layers * n_small], 0, 1).reshape(n_layers, n_small, d)
    caw = [_pad_rows(small_full[k, 0:CONV_A_K]) for k in range(n_layers)]
    cbw = [_pad_rows(small_full[k, CONV_A_K:CONV_A_K + CONV_B_K]) for k in range(n_layers)]
    gbias = [_pad_rows(small_full[k, CONV_A_K + CONV_B_K:]) for k in range(n_layers)]
    saved = []
    xl = x0
    for l in range(n_layers):
        more = l + 1 < n_layers
        (p, h1b, ya, yb, *kept), got = _in_proj_mixer_fwd(
            xl, ln1_g[l][None], gath[l]["in"], grp["in"], caw[l], cbw[l], vec[l], bda[l], bdx[l], f"in_proj_mixer_fwd_{l}",
            [_Gather(packed[0]["rest"])] if l == 0 else [])
        if l == 0:
            ((gath[0]["rest"],),) = got
        (x1, oa, ob), got = _merge_fwd(xl, ya, yb, p, gbias[l], gath[l]["rest"], grp["rest"], f"merge_fwd_{l}",
                                       [_Gather(packed[l + 1]["in"])] if more else [])
        if more:
            ((gath[l + 1]["in"],),) = got
        (x2, gg, uu), got = _ffn_fwd(x1, ln2_g[l][None], gath[l]["rest"], grp["rest"], f"ffn_fwd_{l}",
                                     [_Gather(packed[l + 1]["rest"])] if more else [])
        if more:
            ((gath[l + 1]["rest"],),) = got
        saved.append(dict(x=xl, p=p, h1b=h1b, ya=ya, yb=yb, mixer=kept, x1=x1, oa=oa, ob=ob, gg=gg, uu=uu))
        xl = x2
    loss_tile, dx, dfinal = _loss_head(xl, final_g[None], target, "loss_head")
    loss = lax.psum(loss_tile[0, 0], ("x", "y", "c"))

    def heads(dwb):
        blocks = jnp.diagonal(dwb.reshape(nb, hpt, hd, hpt, hd), axis1=1, axis2=3)
        return jnp.moveaxis(blocks, 3, 1).reshape(hd, d)

    layer_names = [n for n in REPLICATED if n != "final_g"] + list(SMALL_SHARDED)

    def layer_block(k):
        return jnp.concatenate([small_grads[k][n] for n in layer_names], axis=0)

    recv = [dict() for _ in range(n_layers)]
    small_grads = [None] * n_layers
    early_all = None
    xg = {"d": _Group(("d",), (ff // N_DEV,)), "gu": _Group(("g", "u"), (ff // N_DEV,) * 2),
          "out": _Group(("oa", "ob", "o"), (dd,) * 3), "in": grp["in"]}
    far_in = None
    for l in reversed(range(n_layers)):
        s = saved[l]
        (dgg, duu, dw_d), got = _ffn_bwd_act(dx, s["gg"], s["uu"], gath[l]["rest"], grp["rest"], f"ffn_bwd_act_{l}",
                                             [far_in] if far_in else [])
        if far_in:
            recv[l + 1]["in"].append(got[0][0])
        comms = [_Exchange({"d": dw_d}, xg["d"])]
        if l == 0:
            early = [layer_block(k) for k in range(1, n_layers)] + [_pad_rows(dfinal[0:1])]
            comms.append(_Gather(jnp.concatenate(early, axis=0)))
        (dx1, dx1b, h2b, dln2), got = _ffn_bwd_in(dgg, duu, dx, s["x1"], ln2_g[l][None], gath[l]["rest"], grp["rest"],
                                                  f"ffn_bwd_in_{l}", comms)
        recv[l]["d"] = [got[0][0]]
        if l == 0:
            early_all = got[1][0]
        dw_gu = {"g": _weight_grad(dgg, h2b, f"dw_ffn_gate_{l}"), "u": _weight_grad(duu, h2b, f"dw_ffn_up_{l}")}
        (dya, dyb, dp_gates, dgbias, dw_oa, dw_ob, dw_o), got = _merge_bwd(
            dx1b, s["oa"], s["ob"], s["ya"], s["yb"], s["p"], gbias[l], gath[l]["rest"], grp["rest"], f"merge_bwd_{l}",
            [_Exchange(dw_gu, xg["gu"])])
        recv[l]["gu"] = [got[0][0]]
        (dp, dv, dwa, dwx), got = _mixer_bwd(dya, dyb, dp_gates, s["p"], *s["mixer"], caw[l], cbw[l], vec[l], bda[l], bdx[l],
                                             f"mixer_bwd_{l}", [_Exchange({"oa": dw_oa, "ob": dw_ob, "o": dw_o}, xg["out"])])
        recv[l]["out"] = [got[0][0]]
        small_grads[l] = {
            "conv_b_b": dv[DV_CONV_B_B:DV_CONV_B_B + 1], "lru_wa": heads(dwa),
            "lru_ba": dv[DV_BA:DV_BA + 1], "lru_wx": heads(dwx), "lru_bx": dv[DV_BX:DV_BX + 1],
            "lru_lambda": dv[DV_SP:DV_SP + 1] * (-jax.nn.sigmoid(-lru_lambda[l]))[None], "ln2_g": dln2[0:1],
            "conv_a_w": dv[DV_CONV_A:DV_CONV_A + CONV_A_K], "conv_b_w": dv[DV_CONV_B:DV_CONV_B + CONV_B_K],
            "gate_bias": dgbias[0:2],
        }
        dw_in = {"in": _weight_grad(dp, s["h1b"], f"dw_in_{l}")}
        if l > 0:
            near_in, far_in = _Exchange(dw_in, xg["in"], NEAR_PEERS), _Exchange(dw_in, xg["in"], FAR_PEERS, local=False)
        else:
            near_in, far_in = _Exchange(dw_in, xg["in"]), None
        (dx, dln1), got = _in_proj_bwd(dp, s["x"], dx1, ln1_g[l][None], gath[l]["in"], grp["in"], f"in_proj_bwd_{l}", [near_in])
        recv[l]["in"] = [got[0][0]]
        small_grads[l]["ln1_g"] = dln1[0:1]
    grad_x = _from_tile_order(dx, tm_time)[None]

    g = {}
    gsum = [{k: _sum_slots(recv[l][k], f"sum_{k}_{l}") for k in xg} for l in range(n_layers)]

    def part(key):
        k = next(name for name, group in xg.items() if key in group.keys)
        o, r = xg[k].off[key], xg[k].rows[key]
        return jnp.stack([gsum[l][k][o:o + r] for l in range(n_layers)])

    g = {"w_in": jnp.swapaxes(part("in"), 1, 2), "w_out_a": part("oa"), "w_out_b": part("ob"), "w_o": part("o"),
         "w_ffn_gate": jnp.swapaxes(part("g"), 1, 2), "w_ffn_up": jnp.swapaxes(part("u"), 1, 2), "w_ffn_down": part("d")}
    ((late_all,),) = _comm_call([_Gather(layer_block(0).astype(BF16))], "gather_small_grads_0")
    early_sum = _sum_slots([early_all], "sum_small_grads")
    block_rows = late_all.shape[1]
    per_layer = [_sum_slots([late_all], "sum_small_grads_0")]
    per_layer += [early_sum[(k - 1) * block_rows:k * block_rows] for k in range(1, n_layers)]
    g["final_g"] = early_sum[(n_layers - 1) * block_rows].reshape(w["final_g"].shape)
    o = 0
    for n in layer_names:
        rows = small_grads[0][n].shape[0]
        stacked = jnp.concatenate([per_layer[k][o:o + rows] for k in range(n_layers)], axis=0)
        if n in SMALL_SHARDED:
            g[n] = lax.dynamic_slice_in_dim(stacked, me * dd, dd, axis=1).reshape(n_layers, rows, dd)
        else:
            g[n] = stacked.reshape(w[n].shape)
        o += rows

    delta, new_m, new_v = {}, {}, {}
    gate_maps = ("lru_wa", "lru_wx")
    for n in MATRICES + gate_maps:
        shape = w[n].shape
        flat = lambda a: a.reshape(-1, d if n in gate_maps else shape[-1])
        dl, nm, nv = _adamw(flat(w[n]), flat(g[n]), flat(mom[n]), flat(var[n]), f"adamw_{n}")
        delta[n], new_m[n], new_v[n] = dl.reshape(shape), nm.reshape(shape), nv.reshape(shape)
    vectors = tuple(n for n in REPLICATED if n not in gate_maps)
    for group, width, name in ((vectors, d, "adamw_replicated"), (SMALL_SHARDED, dd, "adamw_vectors")):
        cat = lambda src: _pad_rows(jnp.concatenate([src[n].reshape(-1, width) for n in group], axis=0))
        dl, nm, nv = _adamw(cat(w), cat(g), cat(mom), cat(var), name)
        o = 0
        for n in group:
            rows = w[n].size // width
            delta[n], new_m[n], new_v[n] = (a[o:o + rows].reshape(w[n].shape) for a in (dl, nm, nv))
            o += rows

    return (loss, grad_x, *[g[n] for n in ORDER], *[delta[n] for n in ORDER], *[new_m[n] for n in ORDER],
            *[new_v[n] for n in ORDER])
```

```python
import math

import jax
import jax.numpy as jnp
from jax import lax
from jax.experimental import pallas as pl
from jax.experimental.pallas import tpu as pltpu

F32 = jnp.float32
BF16 = jnp.bfloat16

N_DEV = 8
N_PROJ = 7
LRU_HEADS = 16
LRU_C = 8.0
RMS_EPS = 1e-6
CONV_A_K = 3
CONV_B_K = 4
GELU_C = math.sqrt(2.0 / math.pi)
GELU_A = 0.044715

ADAM_LR = 0.001
ADAM_B1 = 0.9
ADAM_B2 = 0.999
ADAM_EPS = 1e-08
ADAM_WD = 0.01
ADAM_STEP = 10

LANE = 128
SUBLANE = 8
MXU_TILE = 256
VMEM_LIMIT = 52 << 20
ALL_PEERS = tuple(range(1, N_DEV))
NEAR_PEERS = (1, 2, 3, 4, 5)
FAR_PEERS = (6, 7)
MESH = pl.DeviceIdType.MESH
ANY = pl.BlockSpec(memory_space=pl.ANY)


def _dot_nn(a, b):
    return lax.dot_general(a, b, (((1,), (0,)), ((), ())), preferred_element_type=F32)


def _dot_nt(a, b):
    return lax.dot_general(a, b, (((1,), (1,)), ((), ())), preferred_element_type=F32)


def _dot_tn(a, b):
    return lax.dot_general(a, b, (((0,), (0,)), ((), ())), preferred_element_type=F32)


def _sigmoid(x):
    return 1.0 / (1.0 + jnp.exp(-x))


def _gelu_and_grad(x):
    x2 = x * x
    t = jnp.tanh(GELU_C * x * (1.0 + GELU_A * x2))
    g = 0.5 * x * (1.0 + t)
    dg = 0.5 * (1.0 + t) + 0.5 * x * (1.0 - t * t) * GELU_C * (1.0 + 3.0 * GELU_A * x2)
    return g, dg


def _zero(ref):
    ref[...] = jnp.zeros(ref.shape, ref.dtype)


def _fit_rows(r, row_bytes, budget=1 << 20):
    fits = [t for t in range(16, r + 1, 16) if r % t == 0 and t * row_bytes <= budget]
    return max(fits) if fits else r


def _row_tile(t, want):
    tm = min(want, t // 2)
    assert t % tm == 0 and tm % SUBLANE == 0, (t, tm)
    return tm


def _params(n_grid=1, **kw):
    return pltpu.CompilerParams(dimension_semantics=("arbitrary",) * n_grid, vmem_limit_bytes=VMEM_LIMIT, **kw)


class _Group:
    def __init__(self, keys, rows):
        self.keys = keys
        self.rows = dict(zip(keys, rows))
        self.off, o = {}, 0
        for k in keys:
            self.off[k] = o
            o += self.rows[k]
        self.total = o


def _groups(d, ff):
    dd, ffs = d // N_DEV, ff // N_DEV
    return {"in": _Group(("in",), (N_PROJ * dd,)),
            "rest": _Group(("oa", "ob", "o", "g", "u", "d"), (dd, dd, dd, ffs, ffs, ffs))}


def _load_weights(g_ref, grp, keys, dsts, sems):
    copies = []
    for n, (k, dst) in enumerate(zip(keys, dsts)):
        rows, off = grp.rows[k], grp.off[k]
        copies += [pltpu.make_async_copy(g_ref.at[p, pl.ds(off, rows), :], dst.at[pl.ds(p * rows, rows), :],
                                         sems.at[n * N_DEV + p]) for p in range(N_DEV)]
    for c in copies:
        c.start()
    for c in copies:
        c.wait()


def _comm_sems():
    return [pltpu.SemaphoreType.DMA((N_DEV - 1,)), pltpu.SemaphoreType.DMA((N_DEV - 1,)), pltpu.SemaphoreType.DMA]


class _Gather:
    def __init__(self, x):
        self.inputs = [x]
        self.out_shape = [jax.ShapeDtypeStruct((N_DEV,) + x.shape, x.dtype)]
        self.scratch = _comm_sems()

    def _plan(self, ins, outs, scr):
        (x_ref,), (out_ref,), (send_sems, recv_sems, local_sem) = ins, outs, scr
        mx, my, mc = lax.axis_index("x"), lax.axis_index("y"), lax.axis_index("c")
        me, sibling = (mx, my, mc), (mx, my, 1 - mc)
        chips = [(1 - mx, my), (mx, 1 - my), (1 - mx, 1 - my)]

        def slot(px, py, pc):
            return out_ref.at[4 * px + 2 * py + pc]

        def copy(k, block, to, src=None):
            return pltpu.make_async_remote_copy(
                src_ref=slot(*block) if src is None else src, dst_ref=slot(*block),
                send_sem=send_sems.at[k], recv_sem=recv_sems.at[k], device_id=to, device_id_type=MESH)

        mine = lambda: pltpu.make_async_copy(x_ref, slot(*me), local_sem)
        first = [lambda: copy(0, me, sibling, src=x_ref)]
        first += [lambda j=j, chip=chip: copy(1 + j, me, (*chip, mc), src=x_ref) for j, chip in enumerate(chips)]
        landed = [lambda j=j, chip=chip: copy(1 + j, (*chip, mc), me) for j, chip in enumerate(chips)]
        passed = [lambda j=j, chip=chip: copy(4 + j, (*chip, mc), sibling) for j, chip in enumerate(chips)]
        from_sibling = [lambda: copy(0, sibling, me)]
        from_sibling += [lambda j=j, chip=chip: copy(4 + j, (*chip, 1 - mc), me) for j, chip in enumerate(chips)]
        return mine, first, landed, passed, from_sibling

    def start(self, ins, outs, scr):
        mine, first, _, _, _ = self._plan(ins, outs, scr)
        mine().start()
        for cp in first:
            cp().start()

    def mid(self, ins, outs, scr):
        _, _, landed, passed, _ = self._plan(ins, outs, scr)
        for got, fwd in zip(landed, passed):
            got().wait_recv()
            fwd().start()

    def finish(self, ins, outs, scr):
        mine, first, _, passed, from_sibling = self._plan(ins, outs, scr)
        for cp in from_sibling:
            cp().wait_recv()
        for cp in first + passed:
            cp().wait_send()
        mine().wait()


class _Exchange:
    def __init__(self, mats, grp, peers=ALL_PEERS, local=True):
        self.grp, self.peers, self.local = grp, tuple(peers), local
        self.inputs = [mats[k] for k in grp.keys]
        slots = len(self.peers) + (1 if local else 0)
        self.out_shape = [jax.ShapeDtypeStruct((slots, grp.total, self.inputs[0].shape[1]), BF16)]
        self.scratch = [pltpu.SemaphoreType.DMA((len(self.peers),)), pltpu.SemaphoreType.DMA((len(self.peers),)),
                        pltpu.SemaphoreType.DMA]

    def _pieces(self, g_refs, out_ref, q, dst_slot):
        out = []
        for g_ref, k in zip(g_refs, self.grp.keys):
            rows = self.grp.rows[k]
            out.append((g_ref.at[pl.ds(pl.multiple_of(q * rows, 16), rows), :],
                        out_ref.at[dst_slot, pl.ds(self.grp.off[k], rows), :]))
        return out

    def start(self, ins, outs, scr):
        (out_ref,), (send_sems, recv_sems, local_sem) = outs, scr
        mx, my, mc = lax.axis_index("x"), lax.axis_index("y"), lax.axis_index("c")
        if self.local:
            for s, t in self._pieces(ins, out_ref, 4 * mx + 2 * my + mc, 0):
                pltpu.make_async_copy(s, t, local_sem).start()
        for n, k in enumerate(self.peers):
            px, py, pc = mx ^ ((k >> 2) & 1), my ^ ((k >> 1) & 1), mc ^ (k & 1)
            for s, t in self._pieces(ins, out_ref, 4 * px + 2 * py + pc, n + (1 if self.local else 0)):
                pltpu.make_async_remote_copy(src_ref=s, dst_ref=t, send_sem=send_sems.at[n], recv_sem=recv_sems.at[n],
                                             device_id=(px, py, pc), device_id_type=MESH).start()

    def mid(self, ins, outs, scr):
        pass

    def finish(self, ins, outs, scr):
        (out_ref,), (send_sems, recv_sems, local_sem) = outs, scr
        mx, my, mc = lax.axis_index("x"), lax.axis_index("y"), lax.axis_index("c")
        whole = out_ref.at[0]
        for n in range(len(self.peers)):
            done = pltpu.make_async_remote_copy(src_ref=whole, dst_ref=whole, send_sem=send_sems.at[n],
                                                recv_sem=recv_sems.at[n], device_id=(mx, my, mc), device_id_type=MESH)
            done.wait_send()
            done.wait_recv()
        if self.local:
            pltpu.make_async_copy(whole, whole, local_sem).wait()


def _split(refs, sizes):
    out, pos = [], 0
    for n in sizes:
        out.append(refs[pos:pos + n])
        pos += n
    return out


def _hosted_call(body, comms, *, name, grid, in_specs, out_specs, out_shape, scratch_shapes, args, aliases=None):
    n_steps = grid[0]
    nc = len(comms)
    sizes = ([len(in_specs)] + [len(c.inputs) for c in comms] + [len(out_specs)] + [len(c.out_shape) for c in comms]
             + [len(scratch_shapes)] + [len(c.scratch) for c in comms])

    def hosted(*refs):
        parts = _split(refs, sizes)
        ins, c_ins = parts[0], parts[1:1 + nc]
        outs, c_outs = parts[1 + nc], parts[2 + nc:2 + 2 * nc]
        scr, c_scr = parts[2 + 2 * nc], parts[3 + 2 * nc:]
        step = pl.program_id(0)
        if comms:
            @pl.when(step == 0)
            def _():
                for c, a, b, s in zip(comms, c_ins, c_outs, c_scr):
                    c.start(a, b, s)

            @pl.when(step == max(n_steps - 2, 0))
            def _():
                for c, a, b, s in zip(comms, c_ins, c_outs, c_scr):
                    c.mid(a, b, s)

        body(*ins, *outs, *scr)
        if comms:
            @pl.when(step == n_steps - 1)
            def _():
                for c, a, b, s in zip(comms, c_ins, c_outs, c_scr):
                    c.finish(a, b, s)

    res = pl.pallas_call(
        hosted, name=name, grid=grid,
        out_shape=[*out_shape, *[o for c in comms for o in c.out_shape]],
        in_specs=[*in_specs, *[ANY for c in comms for _ in c.inputs]],
        out_specs=[*out_specs, *[ANY for c in comms for _ in c.out_shape]],
        scratch_shapes=[*scratch_shapes, *[s for c in comms for s in c.scratch]],
        input_output_aliases=aliases or {},
        compiler_params=_params(),
    )(*args, *[a for c in comms for a in c.inputs])
    main, rest = res[:len(out_specs)], res[len(out_specs):]
    return main, _split(rest, [len(c.out_shape) for c in comms])


def _comm_call(comms, name):
    sizes = [len(c.inputs) for c in comms] + [len(c.out_shape) for c in comms] + [len(c.scratch) for c in comms]
    nc = len(comms)

    def body(*refs):
        parts = _split(refs, sizes)
        triples = list(zip(comms, parts[:nc], parts[nc:2 * nc], parts[2 * nc:]))
        for phase in ("start", "mid", "finish"):
            for c, ins, outs, scr in triples:
                getattr(c, phase)(ins, outs, scr)

    res = pl.pallas_call(
        body, name=name, out_shape=[o for c in comms for o in c.out_shape],
        in_specs=[ANY for c in comms for _ in c.inputs], out_specs=[ANY for c in comms for _ in c.out_shape],
        scratch_shapes=[s for c in comms for s in c.scratch],
    )(*[a for c in comms for a in c.inputs])
    return _split(res, [len(c.out_shape) for c in comms])


def _sum_slots(xs, name):
    _, r, c = xs[0].shape
    tr = _fit_rows(r, c * 4)

    def body(*refs):
        acc = None
        for x_ref in refs[:-1]:
            for p in range(x_ref.shape[0]):
                v = x_ref[p].astype(F32)
                acc = v if acc is None else acc + v
        refs[-1][...] = acc

    return pl.pallas_call(
        body, name=name, grid=(r // tr,),
        out_shape=jax.ShapeDtypeStruct((r, c), F32),
        in_specs=[pl.BlockSpec((x.shape[0], tr, c), lambda i: (0, i, 0)) for x in xs],
        out_specs=pl.BlockSpec((tr, c), lambda i: (i, 0)),
        compiler_params=_params(),
    )(*xs)


def _time_tile(t):
    return _row_tile(t, 256)


def _to_tile_order(a, tm):
    t, c = a.shape
    return jnp.swapaxes(a.reshape(t // tm, SUBLANE, tm // SUBLANE, c), 1, 2).reshape(t, c)


def _from_tile_order(a, tm):
    t, c = a.shape
    return jnp.swapaxes(a.reshape(t // tm, tm // SUBLANE, SUBLANE, c), 1, 2).reshape(t, c)


def _causal_fill(buf, v, prev_tail, n, row):
    tm = v.shape[0]
    for q in range(n):
        cur = v[tm - SUBLANE * (n - q):tm - SUBLANE * (n - q - 1), :]
        prv = prev_tail[SUBLANE * q:SUBLANE * (q + 1), :]
        buf[SUBLANE * q:SUBLANE * (q + 1), :] = jnp.where(row == 0, pltpu.roll(prv, 1, 0), pltpu.roll(cur, 1, 0))
    buf[SUBLANE * n:, :] = v


def _anticausal_fill(buf, v, next_head, n, row):
    tm = v.shape[0]
    buf[0:tm, :] = v
    for q in range(n):
        cur = v[SUBLANE * q:SUBLANE * (q + 1), :]
        nxt = next_head[SUBLANE * q:SUBLANE * (q + 1), :]
        buf[tm + SUBLANE * q:tm + SUBLANE * (q + 1), :] = jnp.where(
            row == SUBLANE - 1, pltpu.roll(nxt, SUBLANE - 1, 0), pltpu.roll(cur, SUBLANE - 1, 0))


def _chain_scan(abuf, bbuf, nk, reverse):
    cw = abuf.shape[1]

    def step(n, carry):
        h, c = carry
        r0 = pl.multiple_of((nk - 1 - n if reverse else n) * SUBLANE, SUBLANE)
        ak = abuf[pl.ds(r0, SUBLANE), :]
        h = ak * h + bbuf[pl.ds(r0, SUBLANE), :]
        c = ak * c
        bbuf[pl.ds(r0, SUBLANE), :] = h
        abuf[pl.ds(r0, SUBLANE), :] = c
        return h, c

    return lax.fori_loop(0, nk, step, (jnp.zeros((SUBLANE, cw), F32), jnp.ones((SUBLANE, cw), F32)), unroll=True)


def _sublane_scan(a, b, row, reverse):
    for sh in (1, 2, 4):
        if reverse:
            m = row < SUBLANE - sh
            b = jnp.where(m, a * pltpu.roll(b, SUBLANE - sh, 0) + b, b)
            a = jnp.where(m, a * pltpu.roll(a, SUBLANE - sh, 0), a)
        else:
            m = row >= sh
            b = jnp.where(m, a * pltpu.roll(b, sh, 0) + b, b)
            a = jnp.where(m, a * pltpu.roll(a, sh, 0), a)
    return a, b


def _lru_gates(ub, bda, bdx, ba, bx, sp):
    r = _sigmoid(_dot_nn(ub, bda) + ba)
    i = _sigmoid(_dot_nn(ub, bdx) + bx)
    log_a = (-LRU_C) * r * sp
    a = jnp.exp(log_a)
    s2 = -jnp.tanh(log_a) * (1.0 + a * a)
    inv_s = lax.rsqrt(s2)
    s = jnp.where(s2 > 0.0, s2 * inv_s, 0.0)
    return r, i, a, s, inv_s


def _in_proj_mixer_fwd(x, g_row, gath, grp, caw, cbw, vec, bda, bdx, name, comms=()):
    t, d = x.shape
    n_in = N_PROJ * d
    tm = _time_tile(t)
    n_t = t // tm
    nk = tm // SUBLANE
    cw = min(MXU_TILE, d)
    nb = d // cw

    def body(x_ref, g_ref, gath_ref, caw_ref, cbw_ref, vec_ref, bda_ref, bdx_ref,
             p_hbm, h1_ref, ya_ref, yb_ref, u_ref, h_ref,
             w_in, sems, pbuf0, pbuf1, p_sems, zbuf, xbuf, abuf, bbuf, z_tail, x_tail, h_carry):
        i = pl.program_id(0)

        @pl.when(i == 0)
        def _():
            _load_weights(gath_ref, grp, ["in"], [w_in], sems)
            _zero(pbuf1)

        @pl.when(i <= 1)
        def _():
            _zero(z_tail)
            _zero(x_tail)
            _zero(h_carry)

        def work(cur, prev, k_cur):
            def store(buf, k, tile):
                return pltpu.make_async_copy(buf, p_hbm.at[pl.ds(tile * tm, tm), :], p_sems.at[k])

            @pl.when(i >= 2)
            def _():
                store(cur, k_cur, i - 2).wait()

            xf = x_ref[...]
            rstd = lax.rsqrt(jnp.mean(xf * xf, axis=-1, keepdims=True) + RMS_EPS)
            h1 = (xf * rstd * g_ref[...]).astype(BF16)
            h1_ref[...] = h1
            row = lax.broadcasted_iota(jnp.int32, (SUBLANE, cw), 0)
            piece = 2 * MXU_TILE
            pieces = list(range(0, n_in, piece))

            def project_some(everything=False):
                while pieces:
                    c0 = pieces.pop(0)
                    cur[:, c0:c0 + piece] = _dot_nt(h1, w_in[c0:c0 + piece, :]).astype(BF16)
                    if not everything:
                        break

            for j in range(nb):
                project_some()
                cs = slice(j * cw, (j + 1) * cw)
                ba_ref, ca_ref, xa_ref, xb_ref, gb_ref = (prev.at[:, k * d:(k + 1) * d] for k in range(5))
                z = ca_ref[:, cs].astype(F32) * xa_ref[:, cs].astype(F32)
                _causal_fill(zbuf, z, z_tail[:, cs], CONV_A_K - 1, row)
                z_tail[:, cs] = z[tm - (CONV_A_K - 1) * SUBLANE:, :]
                cz = caw_ref[0:1, cs] * zbuf[0:tm, :] + caw_ref[1:2, cs] * zbuf[SUBLANE:SUBLANE + tm, :] + caw_ref[2:3, cs] * z
                ya_ref[:, cs] = (ba_ref[:, cs].astype(F32) * cz).astype(BF16)
                project_some()
                xb = xb_ref[:, cs].astype(F32)
                _causal_fill(xbuf, xb, x_tail[:, cs], CONV_B_K - 1, row)
                x_tail[:, cs] = xb[tm - (CONV_B_K - 1) * SUBLANE:, :]
                u = (cbw_ref[0:1, cs] * xbuf[0:tm, :] + cbw_ref[1:2, cs] * xbuf[SUBLANE:SUBLANE + tm, :]
                     + cbw_ref[2:3, cs] * xbuf[2 * SUBLANE:2 * SUBLANE + tm, :] + cbw_ref[3:4, cs] * xb + vec_ref[0:1, cs])
                ub = u.astype(BF16)
                u = ub.astype(F32)
                _, gi, a, s, _ = _lru_gates(ub, bda_ref[j], bdx_ref[j], vec_ref[1:2, cs], vec_ref[2:3, cs], vec_ref[3:4, cs])
                abuf[...] = a
                bbuf[...] = s * (gi * u)
                project_some()
                h_end, a_prod = _chain_scan(abuf, bbuf, nk, reverse=False)
                a_inc, h_inc = _sublane_scan(a_prod, h_end, row, reverse=False)
                carry = h_carry[:, cs]
                ends = h_inc + a_inc * carry
                starts = jnp.where(row == 0, carry, pltpu.roll(ends, 1, 0))
                h_carry[:, cs] = jnp.broadcast_to(ends[SUBLANE - 1:SUBLANE, :], (SUBLANE, cw))
                h = (bbuf[...].reshape(nk, SUBLANE, cw) + abuf[...].reshape(nk, SUBLANE, cw) * starts[None]).reshape(tm, cw)
                project_some()
                gel, _ = _gelu_and_grad(gb_ref[:, cs].astype(F32))
                yb_ref[:, cs] = (h * gel).astype(BF16)
                u_ref[:, cs] = ub
                h_ref[:, cs] = h.astype(BF16)
            project_some(everything=True)

            @pl.when(i < n_t)
            def _():
                store(cur, k_cur, i).start()

            @pl.when(i == n_t)
            def _():
                store(prev, 1 - k_cur, n_t - 1).wait()

        pl.when(i % 2 == 0)(lambda: work(pbuf0, pbuf1, 0))
        pl.when(i % 2 == 1)(lambda: work(pbuf1, pbuf0, 1))

    small = pl.BlockSpec((SUBLANE, d), lambda i: (0, 0))
    bd = pl.BlockSpec((nb, cw, cw), lambda i: (0, 0, 0))
    now = pl.BlockSpec((tm, d), lambda i: (jnp.minimum(i, n_t - 1), 0))
    before = pl.BlockSpec((tm, d), lambda i: (jnp.maximum(i - 1, 0), 0))
    return _hosted_call(
        body, comms, name=name, grid=(n_t + 1,),
        out_shape=[jax.ShapeDtypeStruct((t, n_in), BF16)] + [jax.ShapeDtypeStruct((t, d), BF16)] * 5,
        in_specs=[now, pl.BlockSpec((1, d), lambda i: (0, 0)), ANY, small, small, small, bd, bd],
        out_specs=[ANY, now] + [before] * 4,
        scratch_shapes=[pltpu.VMEM((n_in, d), BF16), pltpu.SemaphoreType.DMA((N_DEV,)),
                        pltpu.VMEM((tm, n_in), BF16), pltpu.VMEM((tm, n_in), BF16), pltpu.SemaphoreType.DMA((2,)),
                        pltpu.VMEM((tm + (CONV_A_K - 1) * SUBLANE, cw), F32), pltpu.VMEM((tm + (CONV_B_K - 1) * SUBLANE, cw), F32),
                        pltpu.VMEM((tm, cw), F32), pltpu.VMEM((tm, cw), F32),
                        pltpu.VMEM(((CONV_A_K - 1) * SUBLANE, d), F32), pltpu.VMEM(((CONV_B_K - 1) * SUBLANE, d), F32),
                        pltpu.VMEM((SUBLANE, d), F32)],
        args=(x, g_row, gath, caw, cbw, vec, bda, bdx))


def _merge_fwd(x, ya, yb, p, gbias, gath, grp, name, comms=()):
    t, d = x.shape
    tm = _row_tile(t, 512)

    def body(x_ref, ya_ref, yb_ref, ga_ref, gb_ref, gbias_ref, gath_ref, x1_ref, oa_ref, ob_ref, w_oa, w_ob, w_o, sems):
        @pl.when(pl.program_id(0) == 0)
        def _():
            _load_weights(gath_ref, grp, ["oa", "ob", "o"], [w_oa, w_ob, w_o], sems)

        oa = _dot_nn(ya_ref[...], w_oa[...]).astype(BF16)
        ob = _dot_nn(yb_ref[...], w_ob[...]).astype(BF16)
        oa_ref[...] = oa
        ob_ref[...] = ob
        sa = _sigmoid(ga_ref[...] + gbias_ref[0:1, :].astype(BF16))
        sb = _sigmoid(gb_ref[...] + gbias_ref[1:2, :].astype(BF16))
        x1_ref[...] = x_ref[...] + _dot_nn(sa * oa + sb * ob, w_o[...])

    row = pl.BlockSpec((tm, d), lambda i: (i, 0))
    return _hosted_call(
        body, comms, name=name, grid=(t // tm,),
        out_shape=[jax.ShapeDtypeStruct((t, d), F32), jax.ShapeDtypeStruct((t, d), BF16), jax.ShapeDtypeStruct((t, d), BF16)],
        in_specs=[row, row, row, pl.BlockSpec((tm, d), lambda i: (i, 5)), pl.BlockSpec((tm, d), lambda i: (i, 6)),
                  pl.BlockSpec((SUBLANE, d), lambda i: (0, 0)), ANY],
        out_specs=[row, row, row],
        scratch_shapes=[pltpu.VMEM((d, d), BF16)] * 3 + [pltpu.SemaphoreType.DMA((3 * N_DEV,))],
        args=(x, ya, yb, p, p, gbias, gath))


def _ffn_fwd(x1, g_row, gath, grp, name, comms=()):
    t, d = x1.shape
    ff = grp.rows["g"] * N_DEV
    tm = _row_tile(t, 512)
    fc = MXU_TILE
    assert ff % fc == 0

    def body(x_ref, g_ref, gath_ref, x2_ref, gg_ref, uu_ref, w_g, w_u, w_d, acc, sems):
        @pl.when(pl.program_id(0) == 0)
        def _():
            _load_weights(gath_ref, grp, ["g", "u", "d"], [w_g, w_u, w_d], sems)

        xf = x_ref[...]
        rstd = lax.rsqrt(jnp.mean(xf * xf, axis=-1, keepdims=True) + RMS_EPS)
        h = (xf * rstd * g_ref[...]).astype(BF16)
        acc[...] = xf
        for c in range(ff // fc):
            fs = slice(c * fc, (c + 1) * fc)
            gg = _dot_nt(h, w_g[fs, :]).astype(BF16)
            uu = _dot_nt(h, w_u[fs, :]).astype(BF16)
            gg_ref[:, fs] = gg
            uu_ref[:, fs] = uu
            acc[...] += _dot_nn(gg * _sigmoid(gg) * uu, w_d[fs, :])
        x2_ref[...] = acc[...]

    row = pl.BlockSpec((tm, d), lambda i: (i, 0))
    wide = pl.BlockSpec((tm, ff), lambda i: (i, 0))
    return _hosted_call(
        body, comms, name=name, grid=(t // tm,),
        out_shape=[jax.ShapeDtypeStruct((t, d), F32), jax.ShapeDtypeStruct((t, ff), BF16), jax.ShapeDtypeStruct((t, ff), BF16)],
        in_specs=[row, pl.BlockSpec((1, d), lambda i: (0, 0)), ANY],
        out_specs=[row, wide, wide],
        scratch_shapes=[pltpu.VMEM((ff, d), BF16)] * 3 + [pltpu.VMEM((tm, d), F32), pltpu.SemaphoreType.DMA((3 * N_DEV,))],
        args=(x1, g_row, gath))


def _loss_head(x, g_row, target, name):
    t, d = x.shape
    tm = _row_tile(t, 512)

    def body(x_ref, g_ref, tgt_ref, loss_ref, dx_ref, dg_ref):
        @pl.when(pl.program_id(0) == 0)
        def _():
            _zero(loss_ref)
            _zero(dg_ref)

        xf = x_ref[...]
        rstd = lax.rsqrt(jnp.mean(xf * xf, axis=-1, keepdims=True) + RMS_EPS)
        xh = xf * rstd
        g = g_ref[...]
        err = xh * g - tgt_ref[...]
        loss_ref[...] += 0.5 * jnp.sum(jnp.sum(err * err, axis=-1, keepdims=True), axis=0, keepdims=True) * (1.0 / d)
        dy = err * (1.0 / d)
        dg_ref[0:1, :] += jnp.sum(dy * xh, axis=0, keepdims=True)
        dxh = dy * g
        dx_ref[...] = rstd * (dxh - xh * jnp.mean(dxh * xh, axis=-1, keepdims=True))

    row = pl.BlockSpec((tm, d), lambda i: (i, 0))
    return pl.pallas_call(
        body, name=name, grid=(t // tm,),
        out_shape=[jax.ShapeDtypeStruct((SUBLANE, LANE), F32), jax.ShapeDtypeStruct((t, d), F32),
                   jax.ShapeDtypeStruct((SUBLANE, d), F32)],
        in_specs=[row, pl.BlockSpec((1, d), lambda i: (0, 0)), row],
        out_specs=[pl.BlockSpec((SUBLANE, LANE), lambda i: (0, 0)), row, pl.BlockSpec((SUBLANE, d), lambda i: (0, 0))],
        compiler_params=_params(),
    )(x, g_row, target)


def _ffn_bwd_act(dx2, gg, uu, gath, grp, name, comms=()):
    t, d = dx2.shape
    ff = grp.rows["g"] * N_DEV
    tm = _row_tile(t, 512)
    fc = MXU_TILE
    n_t = t // tm

    def body(dx2_ref, gg_ref, uu_ref, gath_ref, dgg_ref, duu_ref, dwd_ref, w_d, acc, sems):
        @pl.when(pl.program_id(0) == 0)
        def _():
            _load_weights(gath_ref, grp, ["d"], [w_d], sems)
            _zero(acc)

        dx2b = dx2_ref[...].astype(BF16)
        for c in range(ff // fc):
            fs = slice(c * fc, (c + 1) * fc)
            df = _dot_nt(dx2b, w_d[fs, :]).astype(BF16)
            g = gg_ref[:, fs]
            u = uu_ref[:, fs]
            sg = _sigmoid(g)
            silu = g * sg
            acc[fs, :] += _dot_tn(silu * u, dx2b)
            duu_ref[:, fs] = df * silu
            dgg_ref[:, fs] = df * u * (sg * (1.0 + g * (1.0 - sg)))

        @pl.when(pl.program_id(0) == n_t - 1)
        def _():
            w_d[...] = acc[...].astype(BF16)
            out = pltpu.make_async_copy(w_d, dwd_ref, sems.at[0])
            out.start()
            out.wait()

    row = pl.BlockSpec((tm, d), lambda i: (i, 0))
    wide = pl.BlockSpec((tm, ff), lambda i: (i, 0))
    sd = jax.ShapeDtypeStruct
    return _hosted_call(
        body, comms, name=name, grid=(n_t,),
        out_shape=[sd((t, ff), BF16), sd((t, ff), BF16), sd((ff, d), BF16)],
        in_specs=[row, wide, wide, ANY],
        out_specs=[wide, wide, ANY],
        scratch_shapes=[pltpu.VMEM((ff, d), BF16), pltpu.VMEM((ff, d), F32), pltpu.SemaphoreType.DMA((N_DEV,))],
        args=(dx2, gg, uu, gath))


def _ffn_bwd_in(dgg, duu, dx2, x1, g_row, gath, grp, name, comms=()):
    t, d = x1.shape
    ff = grp.rows["g"] * N_DEV
    tm = _row_tile(t, 512)

    def body(dgg_ref, duu_ref, dx2_ref, x_ref, g_ref, gath_ref, dx1_ref, dx1b_ref, h_ref, dg_ref, w_g, w_u, sems):
        @pl.when(pl.program_id(0) == 0)
        def _():
            _load_weights(gath_ref, grp, ["g", "u"], [w_g, w_u], sems)
            _zero(dg_ref)

        dh = _dot_nn(dgg_ref[...], w_g[...]) + _dot_nn(duu_ref[...], w_u[...])
        xf = x_ref[...]
        rstd = lax.rsqrt(jnp.mean(xf * xf, axis=-1, keepdims=True) + RMS_EPS)
        xh = xf * rstd
        g = g_ref[...]
        h_ref[...] = (xh * g).astype(BF16)
        dg_ref[0:1, :] += jnp.sum(dh * xh, axis=0, keepdims=True)
        dxh = dh * g
        dx1 = dx2_ref[...] + rstd * (dxh - xh * jnp.mean(dxh * xh, axis=-1, keepdims=True))
        dx1_ref[...] = dx1
        dx1b_ref[...] = dx1.astype(BF16)

    row = pl.BlockSpec((tm, d), lambda i: (i, 0))
    wide = pl.BlockSpec((tm, ff), lambda i: (i, 0))
    sd = jax.ShapeDtypeStruct
    return _hosted_call(
        body, comms, name=name, grid=(t // tm,),
        out_shape=[sd((t, d), F32), sd((t, d), BF16), sd((t, d), BF16), sd((SUBLANE, d), F32)],
        in_specs=[wide, wide, row, row, pl.BlockSpec((1, d), lambda i: (0, 0)), ANY],
        out_specs=[row, row, row, pl.BlockSpec((SUBLANE, d), lambda i: (0, 0))],
        scratch_shapes=[pltpu.VMEM((ff, d), BF16)] * 2 + [pltpu.SemaphoreType.DMA((2 * N_DEV,))],
        args=(dgg, duu, dx2, x1, g_row, gath))


def _merge_bwd(dx1b, oa, ob, ya, yb, p, gbias, gath, grp, name, comms=()):
    t, d = oa.shape
    tm = _row_tile(t, 512)
    n_t = t // tm

    def body(dx_ref, oa_ref, ob_ref, ya_ref, yb_ref, ga_ref, gb_ref, gbias_ref, gath_ref,
             dya_ref, dyb_ref, dp_ref, dgb_ref, dwoa_ref, dwob_ref, dwo_ref,
             w_oa, w_ob, w_o, acc_oa, acc_ob, acc_o, stage, sems, out_sems):
        @pl.when(pl.program_id(0) == 0)
        def _():
            _load_weights(gath_ref, grp, ["oa", "ob", "o"], [w_oa, w_ob, w_o], sems)
            for ref in (dgb_ref, acc_oa, acc_ob, acc_o):
                _zero(ref)

        dxb = dx_ref[...]
        dm = _dot_nt(dxb, w_o[...]).astype(BF16)
        oa = oa_ref[...]
        ob = ob_ref[...]
        sa = _sigmoid(ga_ref[...] + gbias_ref[0:1, :].astype(BF16))
        sb = _sigmoid(gb_ref[...] + gbias_ref[1:2, :].astype(BF16))
        acc_o[...] += _dot_tn(sa * oa + sb * ob, dxb)
        doa = dm * sa
        dob = dm * sb
        acc_oa[...] += _dot_tn(ya_ref[...], doa)
        acc_ob[...] += _dot_tn(yb_ref[...], dob)
        dga = dm * oa * sa * (1.0 - sa)
        dgb = dm * ob * sb * (1.0 - sb)
        step = pl.program_id(0)
        slot = step % 2

        def to_dp(k, at_step):
            return pltpu.make_async_copy(stage.at[k], dp_ref.at[pl.ds(at_step * tm, tm), pl.ds(5 * d, 2 * d)], out_sems.at[k])

        @pl.when(step >= 2)
        def _():
            to_dp(slot, step - 2).wait()

        stage[slot, :, 0:d] = dga
        stage[slot, :, d:2 * d] = dgb
        to_dp(slot, step).start()
        ones = jnp.ones((SUBLANE, tm), BF16)
        dgb_ref[0:1, :] += _dot_nn(ones, dga)[0:1, :]
        dgb_ref[1:2, :] += _dot_nn(ones, dgb)[0:1, :]
        dya_ref[...] = _dot_nt(doa, w_oa[...]).astype(BF16)
        dyb_ref[...] = _dot_nt(dob, w_ob[...]).astype(BF16)

        @pl.when(pl.program_id(0) == n_t - 1)
        def _():
            outs = []
            for n, (acc, stage, dst) in enumerate(((acc_oa, w_oa, dwoa_ref), (acc_ob, w_ob, dwob_ref), (acc_o, w_o, dwo_ref))):
                stage[...] = acc[...].astype(BF16)
                outs.append(pltpu.make_async_copy(stage, dst, sems.at[n]))
                outs[-1].start()
            for cp in outs:
                cp.wait()
            for back in range(min(2, n_t)):
                to_dp((n_t - 1 - back) % 2, n_t - 1 - back).wait()

    row = pl.BlockSpec((tm, d), lambda i: (i, 0))
    sd = jax.ShapeDtypeStruct
    return _hosted_call(
        body, comms, name=name, grid=(n_t,),
        out_shape=[sd((t, d), BF16), sd((t, d), BF16), sd((t, N_PROJ * d), BF16), sd((SUBLANE, d), F32),
                   sd((d, d), BF16), sd((d, d), BF16), sd((d, d), BF16)],
        in_specs=[row, row, row, row, row, pl.BlockSpec((tm, d), lambda i: (i, 5)), pl.BlockSpec((tm, d), lambda i: (i, 6)),
                  pl.BlockSpec((SUBLANE, d), lambda i: (0, 0)), ANY],
        out_specs=[row, row, ANY, pl.BlockSpec((SUBLANE, d), lambda i: (0, 0)), ANY, ANY, ANY],
        scratch_shapes=[pltpu.VMEM((d, d), BF16)] * 3 + [pltpu.VMEM((d, d), F32)] * 3
        + [pltpu.VMEM((2, tm, 2 * d), BF16), pltpu.SemaphoreType.DMA((3 * N_DEV,)), pltpu.SemaphoreType.DMA((2,))],
        args=(dx1b, oa, ob, ya, yb, p, p, gbias, gath))


DV_CONV_B_B, DV_BA, DV_BX, DV_SP, DV_CONV_A, DV_CONV_B = 0, 1, 2, 3, 4, 7
DV_ROWS = 16


def _mixer_bwd(dya, dyb, dp_gates, p, u_s, h_s, caw, cbw, vec, bda, bdx, name, comms=()):
    t, d = dya.shape
    tm = _time_tile(t)
    n_t = t // tm
    nk = tm // SUBLANE
    cw = min(MXU_TILE, d)
    nb = d // cw
    halo = 4 * SUBLANE
    ka, kb = CONV_A_K - 1, CONV_B_K - 1

    def body(dya_ref, dyb_ref, _, ba_ref, ca_ref, xa_ref, xb_ref, gb_ref, cah_ref, xah_ref, xbh_ref,
             u_ref, h_ref, hh_ref, caw_ref, cbw_ref, vec_ref, bda_ref, bdx_ref,
             dp_ref, dv_ref, dwa_ref, dwx_ref,
             zbuf, xbuf, hbuf, dczbuf, dubuf, a2buf, a1buf, lbuf, dcz_head, du_head, a_head, lam_head):
        i = pl.program_id(0)

        @pl.when(i == 0)
        def _():
            for ref in (dv_ref, dwa_ref, dwx_ref, dcz_head, du_head, a_head, lam_head):
                _zero(ref)

        has_prev = jnp.where(i < n_t - 1, 1.0, 0.0).astype(F32)
        row = lax.broadcasted_iota(jnp.int32, (SUBLANE, cw), 0)

        def colsum(v):
            return jnp.sum(v, axis=0, keepdims=True)

        for j in range(nb):
            cs = slice(j * cw, (j + 1) * cw)
            ca = ca_ref[:, cs].astype(F32)
            xa = xa_ref[:, cs].astype(F32)
            z = ca * xa
            z_before = cah_ref[:, cs].astype(F32) * xah_ref[:, cs].astype(F32) * has_prev
            _causal_fill(zbuf, z, z_before[halo - ka * SUBLANE:, :], ka, row)
            z2 = zbuf[0:tm, :]
            z1 = zbuf[SUBLANE:SUBLANE + tm, :]
            w0, w1, w2 = caw_ref[0:1, cs], caw_ref[1:2, cs], caw_ref[2:3, cs]
            cz = w0 * z2 + w1 * z1 + w2 * z
            dya = dya_ref[:, cs].astype(F32)
            dp_ref[:, 0 * d + j * cw:0 * d + (j + 1) * cw] = (dya * cz).astype(BF16)
            dcz = dya * ba_ref[:, cs].astype(F32)
            _anticausal_fill(dczbuf, dcz, dcz_head[:, cs], ka, row)
            dcz_head[:, cs] = dcz[0:ka * SUBLANE, :]
            dz = w2 * dcz + w1 * dczbuf[SUBLANE:SUBLANE + tm, :] + w0 * dczbuf[2 * SUBLANE:2 * SUBLANE + tm, :]
            dv_ref[DV_CONV_A + 0:DV_CONV_A + 1, cs] += colsum(dcz * z2)
            dv_ref[DV_CONV_A + 1:DV_CONV_A + 2, cs] += colsum(dcz * z1)
            dv_ref[DV_CONV_A + 2:DV_CONV_A + 3, cs] += colsum(dcz * z)
            dp_ref[:, 1 * d + j * cw:1 * d + (j + 1) * cw] = (dz * xa).astype(BF16)
            dp_ref[:, 2 * d + j * cw:2 * d + (j + 1) * cw] = (dz * ca).astype(BF16)
            h = h_ref[:, cs].astype(F32)
            h_before = hh_ref[:, cs].astype(F32) * has_prev
            _causal_fill(hbuf, h, h_before[halo - SUBLANE:, :], 1, row)
            h_prev = hbuf[0:tm, :]
            dyb = dyb_ref[:, cs].astype(F32)
            gel, dgel = _gelu_and_grad(gb_ref[:, cs].astype(F32))
            dp_ref[:, 4 * d + j * cw:4 * d + (j + 1) * cw] = (dyb * h * dgel).astype(BF16)
            ub = u_ref[:, cs]
            u = ub.astype(F32)
            sp = vec_ref[3:4, cs]
            r, gi, a, s, inv_s = _lru_gates(ub, bda_ref[j], bdx_ref[j], vec_ref[1:2, cs], vec_ref[2:3, cs], sp)
            _anticausal_fill(a2buf, a, a_head[:, cs], 1, row)
            a_head[:, cs] = a[0:SUBLANE, :]
            a1buf[...] = a2buf[SUBLANE:SUBLANE + tm, :]
            lbuf[...] = dyb * gel
            l_end, a_prod = _chain_scan(a1buf, lbuf, nk, reverse=True)
            a_inc, l_inc = _sublane_scan(a_prod, l_end, row, reverse=True)
            carry = lam_head[:, cs]
            ends = l_inc + a_inc * carry
            starts = jnp.where(row == SUBLANE - 1, carry, pltpu.roll(ends, SUBLANE - 1, 0))
            lam_head[:, cs] = jnp.broadcast_to(ends[0:1, :], (SUBLANE, cw))
            lam = (lbuf[...].reshape(nk, SUBLANE, cw) + a1buf[...].reshape(nk, SUBLANE, cw) * starts[None]).reshape(tm, cw)
            da = lam * h_prev
            iu = gi * u
            ds = lam * iu
            di = lam * s * u
            du = lam * s * gi
            dlog_a = da * a - ds * (a * a) * inv_s
            dv_ref[DV_SP:DV_SP + 1, cs] += colsum(dlog_a * r) * (-LRU_C)
            dpr = dlog_a * ((-LRU_C) * sp) * r * (1.0 - r)
            dpi = di * gi * (1.0 - gi)
            dv_ref[DV_BA:DV_BA + 1, cs] += colsum(dpr)
            dv_ref[DV_BX:DV_BX + 1, cs] += colsum(dpi)
            dprb = dpr.astype(BF16)
            dpib = dpi.astype(BF16)
            du = du + _dot_nt(dprb, bda_ref[j]) + _dot_nt(dpib, bdx_ref[j])
            dwa_ref[j] += _dot_tn(ub, dprb)
            dwx_ref[j] += _dot_tn(ub, dpib)
            xb = xb_ref[:, cs].astype(F32)
            x_before = xbh_ref[:, cs].astype(F32) * has_prev
            _causal_fill(xbuf, xb, x_before[halo - kb * SUBLANE:, :], kb, row)
            _anticausal_fill(dubuf, du, du_head[:, cs], kb, row)
            du_head[:, cs] = du[0:kb * SUBLANE, :]
            v0, v1, v2, v3 = cbw_ref[0:1, cs], cbw_ref[1:2, cs], cbw_ref[2:3, cs], cbw_ref[3:4, cs]
            dxb = (v3 * du + v2 * dubuf[SUBLANE:SUBLANE + tm, :] + v1 * dubuf[2 * SUBLANE:2 * SUBLANE + tm, :]
                   + v0 * dubuf[3 * SUBLANE:3 * SUBLANE + tm, :])
            dp_ref[:, 3 * d + j * cw:3 * d + (j + 1) * cw] = dxb.astype(BF16)
            dv_ref[DV_CONV_B_B:DV_CONV_B_B + 1, cs] += colsum(du)
            dv_ref[DV_CONV_B + 0:DV_CONV_B + 1, cs] += colsum(du * xbuf[0:tm, :])
            dv_ref[DV_CONV_B + 1:DV_CONV_B + 2, cs] += colsum(du * xbuf[SUBLANE:SUBLANE + tm, :])
            dv_ref[DV_CONV_B + 2:DV_CONV_B + 3, cs] += colsum(du * xbuf[2 * SUBLANE:2 * SUBLANE + tm, :])
            dv_ref[DV_CONV_B + 3:DV_CONV_B + 4, cs] += colsum(du * xb)

    rt = lambda i: n_t - 1 - i
    row_spec = pl.BlockSpec((tm, d), lambda i: (rt(i), 0))
    slab = lambda s: pl.BlockSpec((tm, d), lambda i, s=s: (rt(i), s))
    before = lambda s: pl.BlockSpec((halo, d), lambda i, s=s: (jnp.maximum(rt(i) * (tm // halo) - 1, 0), s))
    small = pl.BlockSpec((SUBLANE, d), lambda i: (0, 0))
    bd = pl.BlockSpec((nb, cw, cw), lambda i: (0, 0, 0))
    sd = jax.ShapeDtypeStruct
    wbuf = lambda n: pltpu.VMEM((tm + n * SUBLANE, cw), F32)
    head = lambda n: pltpu.VMEM((n * SUBLANE, d), F32)
    return _hosted_call(
        body, comms, name=name, grid=(n_t,),
        out_shape=[sd((t, N_PROJ * d), BF16), sd((DV_ROWS, d), F32), sd((nb, cw, cw), F32), sd((nb, cw, cw), F32)],
        in_specs=[row_spec, row_spec, ANY,
                  slab(0), slab(1), slab(2), slab(3), slab(4), before(1), before(2), before(3),
                  row_spec, row_spec, before(0), small, small, small, bd, bd],
        out_specs=[pl.BlockSpec((tm, 5 * d), lambda i: (rt(i), 0)), pl.BlockSpec((DV_ROWS, d), lambda i: (0, 0)), bd, bd],
        aliases={2: 0},
        scratch_shapes=[wbuf(ka), wbuf(kb), wbuf(1), wbuf(ka), wbuf(kb), wbuf(1),
                        pltpu.VMEM((tm, cw), F32), pltpu.VMEM((tm, cw), F32), head(ka), head(kb), head(1), head(1)],
        args=(dya, dyb, dp_gates, p, p, p, p, p, p, p, p, u_s, h_s, h_s, caw, cbw, vec, bda, bdx))


def _in_proj_bwd(dp, x, dx1, g_row, gath, grp, name, comms=()):
    t, d = x.shape
    tm = _row_tile(t, 512)
    n_in = N_PROJ * d

    def body(dp_ref, x_ref, dx1_ref, g_ref, gath_ref, dx_ref, dg_ref, w_in, sems):
        @pl.when(pl.program_id(0) == 0)
        def _():
            _load_weights(gath_ref, grp, ["in"], [w_in], sems)
            _zero(dg_ref)

        dh = _dot_nn(dp_ref[:, 0:d], w_in[0:d, :])
        for k in range(1, N_PROJ):
            dh = dh + _dot_nn(dp_ref[:, k * d:(k + 1) * d], w_in[k * d:(k + 1) * d, :])
        xf = x_ref[...]
        rstd = lax.rsqrt(jnp.mean(xf * xf, axis=-1, keepdims=True) + RMS_EPS)
        xh = xf * rstd
        g = g_ref[...]
        dg_ref[0:1, :] += jnp.sum(dh * xh, axis=0, keepdims=True)
        dxh = dh * g
        dx_ref[...] = dx1_ref[...] + rstd * (dxh - xh * jnp.mean(dxh * xh, axis=-1, keepdims=True))

    row = pl.BlockSpec((tm, d), lambda i: (i, 0))
    sd = jax.ShapeDtypeStruct
    return _hosted_call(
        body, comms, name=name, grid=(t // tm,),
        out_shape=[sd((t, d), F32), sd((SUBLANE, d), F32)],
        in_specs=[pl.BlockSpec((tm, n_in), lambda i: (i, 0)), row, row, pl.BlockSpec((1, d), lambda i: (0, 0)), ANY],
        out_specs=[row, pl.BlockSpec((SUBLANE, d), lambda i: (0, 0))],
        scratch_shapes=[pltpu.VMEM((n_in, d), BF16), pltpu.SemaphoreType.DMA((N_DEV,))],
        args=(dp, x, dx1, g_row, gath))


def _weight_grad(a, b, name):
    t, m = a.shape
    n = b.shape[1]
    bt = _row_tile(t, 1024)
    bm = m
    for div in (1, 2, 4, 8):
        if m % div == 0 and (m // div) % LANE == 0 and (m // div) * n * 4 <= (12 << 20):
            bm = m // div
            break
    n_t = t // bt

    def body(a_ref, b_ref, o_ref, acc):
        k = pl.program_id(1)

        @pl.when(k == 0)
        def _():
            _zero(acc)

        acc[...] += _dot_tn(a_ref[...], b_ref[...])

        @pl.when(k == n_t - 1)
        def _():
            o_ref[...] = acc[...].astype(BF16)

    return pl.pallas_call(
        body, name=name, grid=(m // bm, n_t),
        out_shape=jax.ShapeDtypeStruct((m, n), BF16),
        in_specs=[pl.BlockSpec((bt, bm), lambda i, k: (k, i)), pl.BlockSpec((bt, n), lambda i, k: (k, 0))],
        out_specs=pl.BlockSpec((bm, n), lambda i, k: (i, 0)),
        scratch_shapes=[pltpu.VMEM((bm, n), F32)],
        compiler_params=_params(2),
    )(a, b)


def _adamw(w, g, m, v, name):
    r, c = w.shape
    tr = _fit_rows(r, c * 4)
    c1 = 1.0 - ADAM_B1 ** ADAM_STEP
    c2 = 1.0 - ADAM_B2 ** ADAM_STEP

    def body(w_ref, g_ref, m_ref, v_ref, d_ref, nm_ref, nv_ref):
        g32 = g_ref[...]
        nm = ADAM_B1 * m_ref[...] + (1.0 - ADAM_B1) * g32
        nv = ADAM_B2 * v_ref[...] + (1.0 - ADAM_B2) * (g32 * g32)
        nm_ref[...] = nm
        nv_ref[...] = nv
        d_ref[...] = -ADAM_LR * ((nm / c1) / (jnp.sqrt(nv / c2) + ADAM_EPS) + ADAM_WD * w_ref[...])

    spec = pl.BlockSpec((tr, c), lambda i: (i, 0))
    return pl.pallas_call(
        body, name=name, grid=(r // tr,),
        out_shape=[jax.ShapeDtypeStruct((r, c), F32)] * 3,
        in_specs=[spec] * 4, out_specs=[spec] * 3,
        compiler_params=_params(),
    )(w, g, m, v)


def _pad_rows(a, mult=SUBLANE):
    pad = (-a.shape[0]) % mult
    return a if pad == 0 else jnp.concatenate([a, jnp.zeros((pad,) + a.shape[1:], a.dtype)], axis=0)


REPLICATED = ("ln1_g", "conv_b_b", "lru_wa", "lru_ba", "lru_wx", "lru_bx", "lru_lambda", "ln2_g", "final_g")
SMALL_SHARDED = ("conv_a_w", "conv_b_w", "gate_bias")
MATRICES = ("w_in", "w_out_a", "w_out_b", "w_o", "w_ffn_gate", "w_ffn_up", "w_ffn_down")
ORDER = ("ln1_g", "w_in", "conv_a_w", "conv_b_w", "conv_b_b", "lru_wa", "lru_ba", "lru_wx", "lru_bx", "lru_lambda",
         "w_out_a", "w_out_b", "gate_bias", "w_o", "ln2_g", "w_ffn_gate", "w_ffn_up", "w_ffn_down", "final_g")


def kernel(x, ln1_g, w_in, conv_a_w, conv_b_w, conv_b_b, lru_wa, lru_ba, lru_wx, lru_bx, lru_lambda, w_out_a, w_out_b, gate_bias, w_o, ln2_g, w_ffn_gate, w_ffn_up, w_ffn_down, final_g, loss_target, m_ln1_g, m_w_in, m_conv_a_w, m_conv_b_w, m_conv_b_b, m_lru_wa, m_lru_ba, m_lru_wx, m_lru_bx, m_lru_lambda, m_w_out_a, m_w_out_b, m_gate_bias, m_w_o, m_ln2_g, m_w_ffn_gate, m_w_ffn_up, m_w_ffn_down, m_final_g, v_ln1_g, v_w_in, v_conv_a_w, v_conv_b_w, v_conv_b_b, v_lru_wa, v_lru_ba, v_lru_wx, v_lru_bx, v_lru_lambda, v_w_out_a, v_w_out_b, v_gate_bias, v_w_o, v_ln2_g, v_w_ffn_gate, v_w_ffn_up, v_w_ffn_down, v_final_g):
    w = dict(ln1_g=ln1_g, w_in=w_in, conv_a_w=conv_a_w, conv_b_w=conv_b_w, conv_b_b=conv_b_b, lru_wa=lru_wa,
             lru_ba=lru_ba, lru_wx=lru_wx, lru_bx=lru_bx, lru_lambda=lru_lambda, w_out_a=w_out_a, w_out_b=w_out_b,
             gate_bias=gate_bias, w_o=w_o, ln2_g=ln2_g, w_ffn_gate=w_ffn_gate, w_ffn_up=w_ffn_up,
             w_ffn_down=w_ffn_down, final_g=final_g)
    mom = dict(ln1_g=m_ln1_g, w_in=m_w_in, conv_a_w=m_conv_a_w, conv_b_w=m_conv_b_w, conv_b_b=m_conv_b_b,
               lru_wa=m_lru_wa, lru_ba=m_lru_ba, lru_wx=m_lru_wx, lru_bx=m_lru_bx, lru_lambda=m_lru_lambda,
               w_out_a=m_w_out_a, w_out_b=m_w_out_b, gate_bias=m_gate_bias, w_o=m_w_o, ln2_g=m_ln2_g,
               w_ffn_gate=m_w_ffn_gate, w_ffn_up=m_w_ffn_up, w_ffn_down=m_w_ffn_down, final_g=m_final_g)
    var = dict(ln1_g=v_ln1_g, w_in=v_w_in, conv_a_w=v_conv_a_w, conv_b_w=v_conv_b_w, conv_b_b=v_conv_b_b,
               lru_wa=v_lru_wa, lru_ba=v_lru_ba, lru_wx=v_lru_wx, lru_bx=v_lru_bx, lru_lambda=v_lru_lambda,
               w_out_a=v_w_out_a, w_out_b=v_w_out_b, gate_bias=v_gate_bias, w_o=v_w_o, ln2_g=v_ln2_g,
               w_ffn_gate=v_w_ffn_gate, w_ffn_up=v_w_ffn_up, w_ffn_down=v_w_ffn_down, final_g=v_final_g)

    _, t, d = x.shape
    n_layers = w_in.shape[0]
    ff = w_ffn_down.shape[1] * N_DEV
    dd = d // N_DEV
    hd = d // LRU_HEADS
    cw = min(MXU_TILE, d)
    nb = d // cw
    hpt = cw // hd
    grp = _groups(d, ff)
    me = 4 * lax.axis_index("x") + 2 * lax.axis_index("y") + lax.axis_index("c")
    tm_time = _time_tile(t)
    x0 = _to_tile_order(x[0], tm_time)
    target = _to_tile_order(loss_target[0], tm_time)

    packed = [{"in": jnp.swapaxes(w_in[l], 0, 1).astype(BF16),
               "rest": jnp.concatenate([w_out_a[l], w_out_b[l], w_o[l], jnp.swapaxes(w_ffn_gate[l], 0, 1),
                                        jnp.swapaxes(w_ffn_up[l], 0, 1), w_ffn_down[l]], axis=0).astype(BF16)}
              for l in range(n_layers)]
    n_small = CONV_A_K + CONV_B_K + 2
    small = _pad_rows(jnp.concatenate([conv_a_w, conv_b_w, gate_bias], axis=1).reshape(n_layers * n_small, dd))
    sp = jax.nn.softplus(-lru_lambda)
    vec = [_pad_rows(jnp.stack([conv_b_b[l], lru_ba[l], lru_bx[l], sp[l]])) for l in range(n_layers)]
    eye = jnp.eye(hpt, dtype=F32)

    def block_diag(wh):
        return jnp.einsum("jkab,kl->jkalb", wh.reshape(nb, hpt, hd, hd), eye).reshape(nb, cw, cw).astype(BF16)

    bda = [block_diag(lru_wa[l]) for l in range(n_layers)]
    bdx = [block_diag(lru_wx[l]) for l in range(n_layers)]

    gath = [dict() for _ in range(n_layers)]
    (gath[0]["in"],), (small_g,) = _comm_call([_Gather(packed[0]["in"]), _Gather(small)], "gather_in_0")
    small_full = jnp.swapaxes(small_g[:, :n_layers * n_small], 0, 1).reshape(n_layers, n_small, d)
    caw = [_pad_rows(small_full[k, 0:CONV_A_K]) for k in range(n_layers)]
    cbw = [_pad_rows(small_full[k, CONV_A_K:CONV_A_K + CONV_B_K]) for k in range(n_layers)]
    gbias = [_pad_rows(small_full[k, CONV_A_K + CONV_B_K:]) for k in range(n_layers)]
    saved = []
    xl = x0
    for l in range(n_layers):
        more = l + 1 < n_layers
        (p, h1b, ya, yb, *kept), got = _in_proj_mixer_fwd(
            xl, ln1_g[l][None], gath[l]["in"], grp["in"], caw[l], cbw[l], vec[l], bda[l], bdx[l], f"in_proj_mixer_fwd_{l}",
            [_Gather(packed[0]["rest"])] if l == 0 else [])
        if l == 0:
            ((gath[0]["rest"],),) = got
        (x1, oa, ob), got = _merge_fwd(xl, ya, yb, p, gbias[l], gath[l]["rest"], grp["rest"], f"merge_fwd_{l}",
                                       [_Gather(packed[l + 1]["in"])] if more else [])
        if more:
            ((gath[l + 1]["in"],),) = got
        (x2, gg, uu), got = _ffn_fwd(x1, ln2_g[l][None], gath[l]["rest"], grp["rest"], f"ffn_fwd_{l}",
                                     [_Gather(packed[l + 1]["rest"])] if more else [])
        if more:
            ((gath[l + 1]["rest"],),) = got
        saved.append(dict(x=xl, p=p, h1b=h1b, ya=ya, yb=yb, mixer=kept, x1=x1, oa=oa, ob=ob, gg=gg, uu=uu))
        xl = x2
    loss_tile, dx, dfinal = _loss_head(xl, final_g[None], target, "loss_head")
    loss = lax.psum(loss_tile[0, 0], ("x", "y", "c"))

    def heads(dwb):
        blocks = jnp.diagonal(dwb.reshape(nb, hpt, hd, hpt, hd), axis1=1, axis2=3)
        return jnp.moveaxis(blocks, 3, 1).reshape(hd, d)

    layer_names = [n for n in REPLICATED if n != "final_g"] + list(SMALL_SHARDED)

    def layer_block(k):
        return jnp.concatenate([small_grads[k][n] for n in layer_names], axis=0)

    recv = [dict() for _ in range(n_layers)]
    small_grads = [None] * n_layers
    early_all = None
    xg = {"d": _Group(("d",), (ff // N_DEV,)), "gu": _Group(("g", "u"), (ff // N_DEV,) * 2),
          "out": _Group(("oa", "ob", "o"), (dd,) * 3), "in": grp["in"]}
    far_in = None
    for l in reversed(range(n_layers)):
        s = saved[l]
        (dgg, duu, dw_d), got = _ffn_bwd_act(dx, s["gg"], s["uu"], gath[l]["rest"], grp["rest"], f"ffn_bwd_act_{l}",
                                             [far_in] if far_in else [])
        if far_in:
            recv[l + 1]["in"].append(got[0][0])
        comms = [_Exchange({"d": dw_d}, xg["d"])]
        if l == 0:
            early = [layer_block(k) for k in range(1, n_layers)] + [_pad_rows(dfinal[0:1])]
            comms.append(_Gather(jnp.concatenate(early, axis=0)))
        (dx1, dx1b, h2b, dln2), got = _ffn_bwd_in(dgg, duu, dx, s["x1"], ln2_g[l][None], gath[l]["rest"], grp["rest"],
                                                  f"ffn_bwd_in_{l}", comms)
        recv[l]["d"] = [got[0][0]]
        if l == 0:
            early_all = got[1][0]
        dw_gu = {"g": _weight_grad(dgg, h2b, f"dw_ffn_gate_{l}"), "u": _weight_grad(duu, h2b, f"dw_ffn_up_{l}")}
        (dya, dyb, dp_gates, dgbias, dw_oa, dw_ob, dw_o), got = _merge_bwd(
            dx1b, s["oa"], s["ob"], s["ya"], s["yb"], s["p"], gbias[l], gath[l]["rest"], grp["rest"], f"merge_bwd_{l}",
            [_Exchange(dw_gu, xg["gu"])])
        recv[l]["gu"] = [got[0][0]]
        (dp, dv, dwa, dwx), got = _mixer_bwd(dya, dyb, dp_gates, s["p"], *s["mixer"], caw[l], cbw[l], vec[l], bda[l], bdx[l],
                                             f"mixer_bwd_{l}", [_Exchange({"oa": dw_oa, "ob": dw_ob, "o": dw_o}, xg["out"])])
        recv[l]["out"] = [got[0][0]]
        small_grads[l] = {
            "conv_b_b": dv[DV_CONV_B_B:DV_CONV_B_B + 1], "lru_wa": heads(dwa),
            "lru_ba": dv[DV_BA:DV_BA + 1], "lru_wx": heads(dwx), "lru_bx": dv[DV_BX:DV_BX + 1],
            "lru_lambda": dv[DV_SP:DV_SP + 1] * (-jax.nn.sigmoid(-lru_lambda[l]))[None], "ln2_g": dln2[0:1],
            "conv_a_w": dv[DV_CONV_A:DV_CONV_A + CONV_A_K], "conv_b_w": dv[DV_CONV_B:DV_CONV_B + CONV_B_K],
            "gate_bias": dgbias[0:2],
        }
        dw_in = {"in": _weight_grad(dp, s["h1b"], f"dw_in_{l}")}
        if l > 0:
            near_in, far_in = _Exchange(dw_in, xg["in"], NEAR_PEERS), _Exchange(dw_in, xg["in"], FAR_PEERS, local=False)
        else:
            near_in, far_in = _Exchange(dw_in, xg["in"]), None
        (dx, dln1), got = _in_proj_bwd(dp, s["x"], dx1, ln1_g[l][None], gath[l]["in"], grp["in"], f"in_proj_bwd_{l}", [near_in])
        recv[l]["in"] = [got[0][0]]
        small_grads[l]["ln1_g"] = dln1[0:1]
    grad_x = _from_tile_order(dx, tm_time)[None]

    g = {}
    gsum = [{k: _sum_slots(recv[l][k], f"sum_{k}_{l}") for k in xg} for l in range(n_layers)]

    def part(key):
        k = next(name for name, group in xg.items() if key in group.keys)
        o, r = xg[k].off[key], xg[k].rows[key]
        return jnp.stack([gsum[l][k][o:o + r] for l in range(n_layers)])

    g = {"w_in": jnp.swapaxes(part("in"), 1, 2), "w_out_a": part("oa"), "w_out_b": part("ob"), "w_o": part("o"),
         "w_ffn_gate": jnp.swapaxes(part("g"), 1, 2), "w_ffn_up": jnp.swapaxes(part("u"), 1, 2), "w_ffn_down": part("d")}
    ((late_all,),) = _comm_call([_Gather(layer_block(0).astype(BF16))], "gather_small_grads_0")
    early_sum = _sum_slots([early_all], "sum_small_grads")
    block_rows = late_all.shape[1]
    per_layer = [_sum_slots([late_all], "sum_small_grads_0")]
    per_layer += [early_sum[(k - 1) * block_rows:k * block_rows] for k in range(1, n_layers)]
    g["final_g"] = early_sum[(n_layers - 1) * block_rows].reshape(w["final_g"].shape)
    o = 0
    for n in layer_names:
        rows = small_grads[0][n].shape[0]
        stacked = jnp.concatenate([per_layer[k][o:o + rows] for k in range(n_layers)], axis=0)
        if n in SMALL_SHARDED:
            g[n] = lax.dynamic_slice_in_dim(stacked, me * dd, dd, axis=1).reshape(n_layers, rows, dd)
        else:
            g[n] = stacked.reshape(w[n].shape)
        o += rows

    delta, new_m, new_v = {}, {}, {}
    gate_maps = ("lru_wa", "lru_wx")
    for n in MATRICES + gate_maps:
        shape = w[n].shape
        flat = lambda a: a.reshape(-1, d if n in gate_maps else shape[-1])
        dl, nm, nv = _adamw(flat(w[n]), flat(g[n]), flat(mom[n]), flat(var[n]), f"adamw_{n}")
        delta[n], new_m[n], new_v[n] = dl.reshape(shape), nm.reshape(shape), nv.reshape(shape)
    vectors = tuple(n for n in REPLICATED if n not in gate_maps)
    for group, width, name in ((vectors, d, "adamw_replicated"), (SMALL_SHARDED, dd, "adamw_vectors")):
        cat = lambda src: _pad_rows(jnp.concatenate([src[n].reshape(-1, width) for n in group], axis=0))
        dl, nm, nv = _adamw(cat(w), cat(g), cat(mom), cat(var), name)
        o = 0
        for n in group:
            rows = w[n].size // width
            delta[n], new_m[n], new_v[n] = (a[o:o + rows].reshape(w[n].shape) for a in (dl, nm, nv))
            o += rows

    return (loss, grad_x, *[g[n] for n in ORDER], *[delta[n] for n in ORDER], *[new_m[n] for n in ORDER],
            *[new_v[n] for n in ORDER])
```

```python
import math

import jax
import jax.numpy as jnp
from jax import lax
from jax.experimental import pallas as pl
from jax.experimental.pallas import tpu as pltpu

F32 = jnp.float32
BF16 = jnp.bfloat16

N_DEV = 8
N_PROJ = 7
LRU_HEADS = 16
LRU_C = 8.0
RMS_EPS = 1e-6
CONV_A_K = 3
CONV_B_K = 4
GELU_C = math.sqrt(2.0 / math.pi)
GELU_A = 0.044715

ADAM_LR = 0.001
ADAM_B1 = 0.9
ADAM_B2 = 0.999
ADAM_EPS = 1e-08
ADAM_WD = 0.01
ADAM_STEP = 10

LANE = 128
SUBLANE = 8
MXU_TILE = 256
VMEM_LIMIT = 52 << 20
ALL_PEERS = tuple(range(1, N_DEV))
NEAR_PEERS = (1, 2, 3, 4, 5)
FAR_PEERS = (6, 7)
MESH = pl.DeviceIdType.MESH
ANY = pl.BlockSpec(memory_space=pl.ANY)


def _dot_nn(a, b):
    return lax.dot_general(a, b, (((1,), (0,)), ((), ())), preferred_element_type=F32)


def _dot_nt(a, b):
    return lax.dot_general(a, b, (((1,), (1,)), ((), ())), preferred_element_type=F32)


def _dot_tn(a, b):
    return lax.dot_general(a, b, (((0,), (0,)), ((), ())), preferred_element_type=F32)


def _sigmoid(x):
    return 1.0 / (1.0 + jnp.exp(-x))


def _gelu_and_grad(x):
    x2 = x * x
    t = jnp.tanh(GELU_C * x * (1.0 + GELU_A * x2))
    g = 0.5 * x * (1.0 + t)
    dg = 0.5 * (1.0 + t) + 0.5 * x * (1.0 - t * t) * GELU_C * (1.0 + 3.0 * GELU_A * x2)
    return g, dg


def _zero(ref):
    ref[...] = jnp.zeros(ref.shape, ref.dtype)


def _fit_rows(r, row_bytes, budget=1 << 20):
    fits = [t for t in range(16, r + 1, 16) if r % t == 0 and t * row_bytes <= budget]
    return max(fits) if fits else r


def _row_tile(t, want):
    tm = min(want, t // 2)
    assert t % tm == 0 and tm % SUBLANE == 0, (t, tm)
    return tm


def _params(n_grid=1, **kw):
    return pltpu.CompilerParams(dimension_semantics=("arbitrary",) * n_grid, vmem_limit_bytes=VMEM_LIMIT, **kw)


class _Group:
    def __init__(self, keys, rows):
        self.keys = keys
        self.rows = dict(zip(keys, rows))
        self.off, o = {}, 0
        for k in keys:
            self.off[k] = o
            o += self.rows[k]
        self.total = o


def _groups(d, ff):
    dd, ffs = d // N_DEV, ff // N_DEV
    return {"in": _Group(("in",), (N_PROJ * dd,)),
            "rest": _Group(("oa", "ob", "o", "g", "u", "d"), (dd, dd, dd, ffs, ffs, ffs))}


def _load_weights(g_ref, grp, keys, dsts, sems):
    copies = []
    for n, (k, dst) in enumerate(zip(keys, dsts)):
        rows, off = grp.rows[k], grp.off[k]
        copies += [pltpu.make_async_copy(g_ref.at[p, pl.ds(off, rows), :], dst.at[pl.ds(p * rows, rows), :],
                                         sems.at[n * N_DEV + p]) for p in range(N_DEV)]
    for c in copies:
        c.start()
    for c in copies:
        c.wait()


def _comm_sems():
    return [pltpu.SemaphoreType.DMA((N_DEV - 1,)), pltpu.SemaphoreType.DMA((N_DEV - 1,)), pltpu.SemaphoreType.DMA]


class _Gather:
    def __init__(self, x):
        self.inputs = [x]
        self.out_shape = [jax.ShapeDtypeStruct((N_DEV,) + x.shape, x.dtype)]
        self.scratch = _comm_sems()

    def _plan(self, ins, outs, scr):
        (x_ref,), (out_ref,), (send_sems, recv_sems, local_sem) = ins, outs, scr
        mx, my, mc = lax.axis_index("x"), lax.axis_index("y"), lax.axis_index("c")
        me, sibling = (mx, my, mc), (mx, my, 1 - mc)
        xn, yn, dg = (1 - mx, my), (mx, 1 - my), (1 - mx, 1 - my)
        core0 = mc == 0
        relayed = (jnp.where(core0, 1 - mx, mx), jnp.where(core0, my, 1 - my))
        relay_to = (jnp.where(core0, mx, 1 - mx), jnp.where(core0, 1 - my, my))

        def slot(px, py, pc):
            return out_ref.at[4 * px + 2 * py + pc]

        def copy(k, block, to, src=None):
            return pltpu.make_async_remote_copy(
                src_ref=slot(*block) if src is None else src, dst_ref=slot(*block),
                send_sem=send_sems.at[k], recv_sem=recv_sems.at[k], device_id=to, device_id_type=MESH)

        mine = lambda: pltpu.make_async_copy(x_ref, slot(*me), local_sem)
        own = [lambda: copy(0, me, sibling, src=x_ref), lambda: copy(1, me, (*xn, mc), src=x_ref),
               lambda: copy(2, me, (*yn, mc), src=x_ref)]
        relay = lambda: copy(3, (*relayed, mc), (*relay_to, mc))
        passes = [lambda: copy(4, (*xn, mc), sibling), lambda: copy(5, (*yn, mc), sibling), lambda: copy(6, (*dg, mc), sibling)]
        arrival = lambda k: copy(k, me, me)
        return mine, own, relay, passes, arrival

    def start(self, ins, outs, scr):
        mine, own, _, _, _ = self._plan(ins, outs, scr)
        mine().start()
        for cp in own:
            cp().start()

    def relay(self, ins, outs, scr):
        _, _, relay, passes, arrival = self._plan(ins, outs, scr)
        arrival(1).wait_recv()
        arrival(2).wait_recv()
        relay().start()
        passes[0]().start()
        passes[1]().start()

    def mid(self, ins, outs, scr):
        _, _, _, passes, arrival = self._plan(ins, outs, scr)
        arrival(3).wait_recv()
        passes[2]().start()

    def finish(self, ins, outs, scr):
        mine, _, _, _, arrival = self._plan(ins, outs, scr)
        for k in (0, 4, 5, 6):
            arrival(k).wait_recv()
        for k in range(N_DEV - 1):
            arrival(k).wait_send()
        mine().wait()


class _Exchange:
    def __init__(self, mats, grp, peers=ALL_PEERS, local=True):
        self.grp, self.peers, self.local = grp, tuple(peers), local
        self.inputs = [mats[k] for k in grp.keys]
        slots = len(self.peers) + (1 if local else 0)
        self.out_shape = [jax.ShapeDtypeStruct((slots, grp.total, self.inputs[0].shape[1]), BF16)]
        self.scratch = [pltpu.SemaphoreType.DMA((len(self.peers),)), pltpu.SemaphoreType.DMA((len(self.peers),)),
                        pltpu.SemaphoreType.DMA]

    def _pieces(self, g_refs, out_ref, q, dst_slot):
        out = []
        for g_ref, k in zip(g_refs, self.grp.keys):
            rows = self.grp.rows[k]
            out.append((g_ref.at[pl.ds(pl.multiple_of(q * rows, 16), rows), :],
                        out_ref.at[dst_slot, pl.ds(self.grp.off[k], rows), :]))
        return out

    def start(self, ins, outs, scr):
        (out_ref,), (send_sems, recv_sems, local_sem) = outs, scr
        mx, my, mc = lax.axis_index("x"), lax.axis_index("y"), lax.axis_index("c")
        if self.local:
            for s, t in self._pieces(ins, out_ref, 4 * mx + 2 * my + mc, 0):
                pltpu.make_async_copy(s, t, local_sem).start()
        for n, k in enumerate(self.peers):
            px, py, pc = mx ^ ((k >> 2) & 1), my ^ ((k >> 1) & 1), mc ^ (k & 1)
            for s, t in self._pieces(ins, out_ref, 4 * px + 2 * py + pc, n + (1 if self.local else 0)):
                pltpu.make_async_remote_copy(src_ref=s, dst_ref=t, send_sem=send_sems.at[n], recv_sem=recv_sems.at[n],
                                             device_id=(px, py, pc), device_id_type=MESH).start()

    def relay(self, ins, outs, scr):
        pass

    def mid(self, ins, outs, scr):
        pass

    def finish(self, ins, outs, scr):
        (out_ref,), (send_sems, recv_sems, local_sem) = outs, scr
        mx, my, mc = lax.axis_index("x"), lax.axis_index("y"), lax.axis_index("c")
        whole = out_ref.at[0]
        for n in range(len(self.peers)):
            done = pltpu.make_async_remote_copy(src_ref=whole, dst_ref=whole, send_sem=send_sems.at[n],
                                                recv_sem=recv_sems.at[n], device_id=(mx, my, mc), device_id_type=MESH)
            done.wait_send()
            done.wait_recv()
        if self.local:
            pltpu.make_async_copy(whole, whole, local_sem).wait()


def _split(refs, sizes):
    out, pos = [], 0
    for n in sizes:
        out.append(refs[pos:pos + n])
        pos += n
    return out


def _hosted_call(body, comms, *, name, grid, in_specs, out_specs, out_shape, scratch_shapes, args, aliases=None):
    n_steps = grid[0]
    nc = len(comms)
    sizes = ([len(in_specs)] + [len(c.inputs) for c in comms] + [len(out_specs)] + [len(c.out_shape) for c in comms]
             + [len(scratch_shapes)] + [len(c.scratch) for c in comms])

    def hosted(*refs):
        parts = _split(refs, sizes)
        ins, c_ins = parts[0], parts[1:1 + nc]
        outs, c_outs = parts[1 + nc], parts[2 + nc:2 + 2 * nc]
        scr, c_scr = parts[2 + 2 * nc], parts[3 + 2 * nc:]
        step = pl.program_id(0)
        if comms:
            @pl.when(step == 0)
            def _():
                for c, a, b, s in zip(comms, c_ins, c_outs, c_scr):
                    c.start(a, b, s)

            relay_step = (3 * n_steps) // 5

            @pl.when(step == relay_step)
            def _():
                for c, a, b, s in zip(comms, c_ins, c_outs, c_scr):
                    c.relay(a, b, s)

            @pl.when(step == max(n_steps - 2, relay_step))
            def _():
                for c, a, b, s in zip(comms, c_ins, c_outs, c_scr):
                    c.mid(a, b, s)

        body(*ins, *outs, *scr)
        if comms:
            @pl.when(step == n_steps - 1)
            def _():
                for c, a, b, s in zip(comms, c_ins, c_outs, c_scr):
                    c.finish(a, b, s)

    res = pl.pallas_call(
        hosted, name=name, grid=grid,
        out_shape=[*out_shape, *[o for c in comms for o in c.out_shape]],
        in_specs=[*in_specs, *[ANY for c in comms for _ in c.inputs]],
        out_specs=[*out_specs, *[ANY for c in comms for _ in c.out_shape]],
        scratch_shapes=[*scratch_shapes, *[s for c in comms for s in c.scratch]],
        input_output_aliases=aliases or {},
        compiler_params=_params(),
    )(*args, *[a for c in comms for a in c.inputs])
    main, rest = res[:len(out_specs)], res[len(out_specs):]
    return main, _split(rest, [len(c.out_shape) for c in comms])


def _comm_call(comms, name):
    sizes = [len(c.inputs) for c in comms] + [len(c.out_shape) for c in comms] + [len(c.scratch) for c in comms]
    nc = len(comms)

    def body(*refs):
        parts = _split(refs, sizes)
        triples = list(zip(comms, parts[:nc], parts[nc:2 * nc], parts[2 * nc:]))
        for phase in ("start", "relay", "mid", "finish"):
            for c, ins, outs, scr in triples:
                getattr(c, phase)(ins, outs, scr)

    res = pl.pallas_call(
        body, name=name, out_shape=[o for c in comms for o in c.out_shape],
        in_specs=[ANY for c in comms for _ in c.inputs], out_specs=[ANY for c in comms for _ in c.out_shape],
        scratch_shapes=[s for c in comms for s in c.scratch],
    )(*[a for c in comms for a in c.inputs])
    return _split(res, [len(c.out_shape) for c in comms])


def _sum_slots(xs, name):
    _, r, c = xs[0].shape
    tr = _fit_rows(r, c * 4)

    def body(*refs):
        acc = None
        for x_ref in refs[:-1]:
            for p in range(x_ref.shape[0]):
                v = x_ref[p].astype(F32)
                acc = v if acc is None else acc + v
        refs[-1][...] = acc

    return pl.pallas_call(
        body, name=name, grid=(r // tr,),
        out_shape=jax.ShapeDtypeStruct((r, c), F32),
        in_specs=[pl.BlockSpec((x.shape[0], tr, c), lambda i: (0, i, 0)) for x in xs],
        out_specs=pl.BlockSpec((tr, c), lambda i: (i, 0)),
        compiler_params=_params(),
    )(*xs)


def _time_tile(t):
    return _row_tile(t, 256)


def _to_tile_order(a, tm):
    t, c = a.shape
    return jnp.swapaxes(a.reshape(t // tm, SUBLANE, tm // SUBLANE, c), 1, 2).reshape(t, c)


def _from_tile_order(a, tm):
    t, c = a.shape
    return jnp.swapaxes(a.reshape(t // tm, tm // SUBLANE, SUBLANE, c), 1, 2).reshape(t, c)


def _causal_fill(buf, v, prev_tail, n, row):
    tm = v.shape[0]
    for q in range(n):
        cur = v[tm - SUBLANE * (n - q):tm - SUBLANE * (n - q - 1), :]
        prv = prev_tail[SUBLANE * q:SUBLANE * (q + 1), :]
        buf[SUBLANE * q:SUBLANE * (q + 1), :] = jnp.where(row == 0, pltpu.roll(prv, 1, 0), pltpu.roll(cur, 1, 0))
    buf[SUBLANE * n:, :] = v


def _anticausal_fill(buf, v, next_head, n, row):
    tm = v.shape[0]
    buf[0:tm, :] = v
    for q in range(n):
        cur = v[SUBLANE * q:SUBLANE * (q + 1), :]
        nxt = next_head[SUBLANE * q:SUBLANE * (q + 1), :]
        buf[tm + SUBLANE * q:tm + SUBLANE * (q + 1), :] = jnp.where(
            row == SUBLANE - 1, pltpu.roll(nxt, SUBLANE - 1, 0), pltpu.roll(cur, SUBLANE - 1, 0))


def _chain_scan(abuf, bbuf, nk, reverse):
    cw = abuf.shape[1]

    def step(n, carry):
        h, c = carry
        r0 = pl.multiple_of((nk - 1 - n if reverse else n) * SUBLANE, SUBLANE)
        ak = abuf[pl.ds(r0, SUBLANE), :]
        h = ak * h + bbuf[pl.ds(r0, SUBLANE), :]
        c = ak * c
        bbuf[pl.ds(r0, SUBLANE), :] = h
        abuf[pl.ds(r0, SUBLANE), :] = c
        return h, c

    return lax.fori_loop(0, nk, step, (jnp.zeros((SUBLANE, cw), F32), jnp.ones((SUBLANE, cw), F32)), unroll=True)


def _sublane_scan(a, b, row, reverse):
    for sh in (1, 2, 4):
        if reverse:
            m = row < SUBLANE - sh
            b = jnp.where(m, a * pltpu.roll(b, SUBLANE - sh, 0) + b, b)
            a = jnp.where(m, a * pltpu.roll(a, SUBLANE - sh, 0), a)
        else:
            m = row >= sh
            b = jnp.where(m, a * pltpu.roll(b, sh, 0) + b, b)
            a = jnp.where(m, a * pltpu.roll(a, sh, 0), a)
    return a, b


def _lru_gates(ub, bda, bdx, ba, bx, sp):
    r = _sigmoid(_dot_nn(ub, bda) + ba)
    i = _sigmoid(_dot_nn(ub, bdx) + bx)
    log_a = (-LRU_C) * r * sp
    a = jnp.exp(log_a)
    s2 = -jnp.tanh(log_a) * (1.0 + a * a)
    inv_s = lax.rsqrt(s2)
    s = jnp.where(s2 > 0.0, s2 * inv_s, 0.0)
    return r, i, a, s, inv_s


def _in_proj_mixer_fwd(x, g_row, gath, grp, caw, cbw, vec, bda, bdx, name, comms=()):
    t, d = x.shape
    n_in = N_PROJ * d
    tm = _time_tile(t)
    nk = tm // SUBLANE
    cw = min(MXU_TILE, d)
    nb = d // cw

    def body(x_ref, g_ref, gath_ref, caw_ref, cbw_ref, vec_ref, bda_ref, bdx_ref,
             p_ref, h1_ref, ya_ref, yb_ref, u_ref, h_ref, w_in, sems, zbuf, xbuf, abuf, bbuf, z_tail, x_tail, h_carry):
        @pl.when(pl.program_id(0) == 0)
        def _():
            _load_weights(gath_ref, grp, ["in"], [w_in], sems)
            _zero(z_tail)
            _zero(x_tail)
            _zero(h_carry)

        def project_and_mix():
            xf = x_ref[...]
            rstd = lax.rsqrt(jnp.mean(xf * xf, axis=-1, keepdims=True) + RMS_EPS)
            h1 = (xf * rstd * g_ref[...]).astype(BF16)
            h1_ref[...] = h1
            for k in range(N_PROJ):
                p_ref[:, k * d:(k + 1) * d] = _dot_nt(h1, w_in[k * d:(k + 1) * d, :]).astype(BF16)
            row = lax.broadcasted_iota(jnp.int32, (SUBLANE, cw), 0)
            for j in range(nb):
                cs = slice(j * cw, (j + 1) * cw)
                ba_ref, ca_ref, xa_ref, xb_ref, gb_ref = (p_ref.at[:, k * d:(k + 1) * d] for k in range(5))
                z = ca_ref[:, cs].astype(F32) * xa_ref[:, cs].astype(F32)
                _causal_fill(zbuf, z, z_tail[:, cs], CONV_A_K - 1, row)
                z_tail[:, cs] = z[tm - (CONV_A_K - 1) * SUBLANE:, :]
                cz = caw_ref[0:1, cs] * zbuf[0:tm, :] + caw_ref[1:2, cs] * zbuf[SUBLANE:SUBLANE + tm, :] + caw_ref[2:3, cs] * z
                ya_ref[:, cs] = (ba_ref[:, cs].astype(F32) * cz).astype(BF16)
                xb = xb_ref[:, cs].astype(F32)
                _causal_fill(xbuf, xb, x_tail[:, cs], CONV_B_K - 1, row)
                x_tail[:, cs] = xb[tm - (CONV_B_K - 1) * SUBLANE:, :]
                u = (cbw_ref[0:1, cs] * xbuf[0:tm, :] + cbw_ref[1:2, cs] * xbuf[SUBLANE:SUBLANE + tm, :]
                     + cbw_ref[2:3, cs] * xbuf[2 * SUBLANE:2 * SUBLANE + tm, :] + cbw_ref[3:4, cs] * xb + vec_ref[0:1, cs])
                ub = u.astype(BF16)
                u = ub.astype(F32)
                _, gi, a, s, _ = _lru_gates(ub, bda_ref[j], bdx_ref[j], vec_ref[1:2, cs], vec_ref[2:3, cs], vec_ref[3:4, cs])
                abuf[...] = a
                bbuf[...] = s * (gi * u)
                h_end, a_prod = _chain_scan(abuf, bbuf, nk, reverse=False)
                a_inc, h_inc = _sublane_scan(a_prod, h_end, row, reverse=False)
                carry = h_carry[:, cs]
                ends = h_inc + a_inc * carry
                starts = jnp.where(row == 0, carry, pltpu.roll(ends, 1, 0))
                h_carry[:, cs] = jnp.broadcast_to(ends[SUBLANE - 1:SUBLANE, :], (SUBLANE, cw))
                h = (bbuf[...].reshape(nk, SUBLANE, cw) + abuf[...].reshape(nk, SUBLANE, cw) * starts[None]).reshape(tm, cw)
                gel, _ = _gelu_and_grad(gb_ref[:, cs].astype(F32))
                yb_ref[:, cs] = (h * gel).astype(BF16)
                u_ref[:, cs] = ub
                h_ref[:, cs] = h.astype(BF16)

        project_and_mix()

    small = pl.BlockSpec((SUBLANE, d), lambda i: (0, 0))
    bd = pl.BlockSpec((nb, cw, cw), lambda i: (0, 0, 0))
    row_spec = pl.BlockSpec((tm, d), lambda i: (i, 0))
    return _hosted_call(
        body, comms, name=name, grid=(t // tm,),
        out_shape=[jax.ShapeDtypeStruct((t, n_in), BF16)] + [jax.ShapeDtypeStruct((t, d), BF16)] * 5,
        in_specs=[row_spec, pl.BlockSpec((1, d), lambda i: (0, 0)), ANY, small, small, small, bd, bd],
        out_specs=[pl.BlockSpec((tm, n_in), lambda i: (i, 0))] + [row_spec] * 5,
        scratch_shapes=[pltpu.VMEM((n_in, d), BF16), pltpu.SemaphoreType.DMA((N_DEV,)),
                        pltpu.VMEM((tm + (CONV_A_K - 1) * SUBLANE, cw), F32), pltpu.VMEM((tm + (CONV_B_K - 1) * SUBLANE, cw), F32),
                        pltpu.VMEM((tm, cw), F32), pltpu.VMEM((tm, cw), F32),
                        pltpu.VMEM(((CONV_A_K - 1) * SUBLANE, d), F32), pltpu.VMEM(((CONV_B_K - 1) * SUBLANE, d), F32),
                        pltpu.VMEM((SUBLANE, d), F32)],
        args=(x, g_row, gath, caw, cbw, vec, bda, bdx))


def _merge_fwd(x, ya, yb, p, gbias, gath, grp, name, comms=()):
    t, d = x.shape
    tm = _row_tile(t, 512)

    def body(x_ref, ya_ref, yb_ref, ga_ref, gb_ref, gbias_ref, gath_ref, x1_ref, oa_ref, ob_ref, w_oa, w_ob, w_o, sems):
        @pl.when(pl.program_id(0) == 0)
        def _():
            _load_weights(gath_ref, grp, ["oa", "ob", "o"], [w_oa, w_ob, w_o], sems)

        oa = _dot_nn(ya_ref[...], w_oa[...]).astype(BF16)
        ob = _dot_nn(yb_ref[...], w_ob[...]).astype(BF16)
        oa_ref[...] = oa
        ob_ref[...] = ob
        sa = _sigmoid(ga_ref[...] + gbias_ref[0:1, :].astype(BF16))
        sb = _sigmoid(gb_ref[...] + gbias_ref[1:2, :].astype(BF16))
        x1_ref[...] = x_ref[...] + _dot_nn(sa * oa + sb * ob, w_o[...])

    row = pl.BlockSpec((tm, d), lambda i: (i, 0))
    return _hosted_call(
        body, comms, name=name, grid=(t // tm,),
        out_shape=[jax.ShapeDtypeStruct((t, d), F32), jax.ShapeDtypeStruct((t, d), BF16), jax.ShapeDtypeStruct((t, d), BF16)],
        in_specs=[row, row, row, pl.BlockSpec((tm, d), lambda i: (i, 5)), pl.BlockSpec((tm, d), lambda i: (i, 6)),
                  pl.BlockSpec((SUBLANE, d), lambda i: (0, 0)), ANY],
        out_specs=[row, row, row],
        scratch_shapes=[pltpu.VMEM((d, d), BF16)] * 3 + [pltpu.SemaphoreType.DMA((3 * N_DEV,))],
        args=(x, ya, yb, p, p, gbias, gath))


def _ffn_fwd(x1, g_row, gath, grp, name, comms=()):
    t, d = x1.shape
    ff = grp.rows["g"] * N_DEV
    tm = _row_tile(t, 512)
    fc = MXU_TILE
    assert ff % fc == 0

    def body(x_ref, g_ref, gath_ref, x2_ref, gg_ref, uu_ref, w_g, w_u, w_d, acc, sems):
        @pl.when(pl.program_id(0) == 0)
        def _():
            _load_weights(gath_ref, grp, ["g", "u", "d"], [w_g, w_u, w_d], sems)

        xf = x_ref[...]
        rstd = lax.rsqrt(jnp.mean(xf * xf, axis=-1, keepdims=True) + RMS_EPS)
        h = (xf * rstd * g_ref[...]).astype(BF16)
        acc[...] = xf
        for c in range(ff // fc):
            fs = slice(c * fc, (c + 1) * fc)
            gg = _dot_nt(h, w_g[fs, :]).astype(BF16)
            uu = _dot_nt(h, w_u[fs, :]).astype(BF16)
            gg_ref[:, fs] = gg
            uu_ref[:, fs] = uu
            acc[...] += _dot_nn(gg * _sigmoid(gg) * uu, w_d[fs, :])
        x2_ref[...] = acc[...]

    row = pl.BlockSpec((tm, d), lambda i: (i, 0))
    wide = pl.BlockSpec((tm, ff), lambda i: (i, 0))
    return _hosted_call(
        body, comms, name=name, grid=(t // tm,),
        out_shape=[jax.ShapeDtypeStruct((t, d), F32), jax.ShapeDtypeStruct((t, ff), BF16), jax.ShapeDtypeStruct((t, ff), BF16)],
        in_specs=[row, pl.BlockSpec((1, d), lambda i: (0, 0)), ANY],
        out_specs=[row, wide, wide],
        scratch_shapes=[pltpu.VMEM((ff, d), BF16)] * 3 + [pltpu.VMEM((tm, d), F32), pltpu.SemaphoreType.DMA((3 * N_DEV,))],
        args=(x1, g_row, gath))


def _loss_head(x, g_row, target, name):
    t, d = x.shape
    tm = _row_tile(t, 512)

    def body(x_ref, g_ref, tgt_ref, loss_ref, dx_ref, dg_ref):
        @pl.when(pl.program_id(0) == 0)
        def _():
            _zero(loss_ref)
            _zero(dg_ref)

        xf = x_ref[...]
        rstd = lax.rsqrt(jnp.mean(xf * xf, axis=-1, keepdims=True) + RMS_EPS)
        xh = xf * rstd
        g = g_ref[...]
        err = xh * g - tgt_ref[...]
        loss_ref[...] += 0.5 * jnp.sum(jnp.sum(err * err, axis=-1, keepdims=True), axis=0, keepdims=True) * (1.0 / d)
        dy = err * (1.0 / d)
        dg_ref[0:1, :] += jnp.sum(dy * xh, axis=0, keepdims=True)
        dxh = dy * g
        dx_ref[...] = rstd * (dxh - xh * jnp.mean(dxh * xh, axis=-1, keepdims=True))

    row = pl.BlockSpec((tm, d), lambda i: (i, 0))
    return pl.pallas_call(
        body, name=name, grid=(t // tm,),
        out_shape=[jax.ShapeDtypeStruct((SUBLANE, LANE), F32), jax.ShapeDtypeStruct((t, d), F32),
                   jax.ShapeDtypeStruct((SUBLANE, d), F32)],
        in_specs=[row, pl.BlockSpec((1, d), lambda i: (0, 0)), row],
        out_specs=[pl.BlockSpec((SUBLANE, LANE), lambda i: (0, 0)), row, pl.BlockSpec((SUBLANE, d), lambda i: (0, 0))],
        compiler_params=_params(),
    )(x, g_row, target)


def _ffn_bwd_act(dx2, gg, uu, gath, grp, name, comms=()):
    t, d = dx2.shape
    ff = grp.rows["g"] * N_DEV
    tm = _row_tile(t, 512)
    fc = MXU_TILE
    n_t = t // tm

    def body(dx2_ref, gg_ref, uu_ref, gath_ref, dgg_ref, duu_ref, dwd_ref, w_d, acc, sems):
        @pl.when(pl.program_id(0) == 0)
        def _():
            _load_weights(gath_ref, grp, ["d"], [w_d], sems)
            _zero(acc)

        dx2b = dx2_ref[...].astype(BF16)
        for c in range(ff // fc):
            fs = slice(c * fc, (c + 1) * fc)
            df = _dot_nt(dx2b, w_d[fs, :]).astype(BF16)
            g = gg_ref[:, fs]
            u = uu_ref[:, fs]
            sg = _sigmoid(g)
            silu = g * sg
            acc[fs, :] += _dot_tn(silu * u, dx2b)
            duu_ref[:, fs] = df * silu
            dgg_ref[:, fs] = df * u * (sg * (1.0 + g * (1.0 - sg)))

        @pl.when(pl.program_id(0) == n_t - 1)
        def _():
            w_d[...] = acc[...].astype(BF16)
            out = pltpu.make_async_copy(w_d, dwd_ref, sems.at[0])
            out.start()
            out.wait()

    row = pl.BlockSpec((tm, d), lambda i: (i, 0))
    wide = pl.BlockSpec((tm, ff), lambda i: (i, 0))
    sd = jax.ShapeDtypeStruct
    return _hosted_call(
        body, comms, name=name, grid=(n_t,),
        out_shape=[sd((t, ff), BF16), sd((t, ff), BF16), sd((ff, d), BF16)],
        in_specs=[row, wide, wide, ANY],
        out_specs=[wide, wide, ANY],
        scratch_shapes=[pltpu.VMEM((ff, d), BF16), pltpu.VMEM((ff, d), F32), pltpu.SemaphoreType.DMA((N_DEV,))],
        args=(dx2, gg, uu, gath))


def _ffn_bwd_in(dgg, duu, dx2, x1, g_row, gath, grp, name, comms=()):
    t, d = x1.shape
    ff = grp.rows["g"] * N_DEV
    tm = _row_tile(t, 512)

    def body(dgg_ref, duu_ref, dx2_ref, x_ref, g_ref, gath_ref, dx1_ref, dx1b_ref, h_ref, dg_ref, w_g, w_u, sems):
        @pl.when(pl.program_id(0) == 0)
        def _():
            _load_weights(gath_ref, grp, ["g", "u"], [w_g, w_u], sems)
            _zero(dg_ref)

        dh = _dot_nn(dgg_ref[...], w_g[...]) + _dot_nn(duu_ref[...], w_u[...])
        xf = x_ref[...]
        rstd = lax.rsqrt(jnp.mean(xf * xf, axis=-1, keepdims=True) + RMS_EPS)
        xh = xf * rstd
        g = g_ref[...]
        h_ref[...] = (xh * g).astype(BF16)
        dg_ref[0:1, :] += jnp.sum(dh * xh, axis=0, keepdims=True)
        dxh = dh * g
        dx1 = dx2_ref[...] + rstd * (dxh - xh * jnp.mean(dxh * xh, axis=-1, keepdims=True))
        dx1_ref[...] = dx1
        dx1b_ref[...] = dx1.astype(BF16)

    row = pl.BlockSpec((tm, d), lambda i: (i, 0))
    wide = pl.BlockSpec((tm, ff), lambda i: (i, 0))
    sd = jax.ShapeDtypeStruct
    return _hosted_call(
        body, comms, name=name, grid=(t // tm,),
        out_shape=[sd((t, d), F32), sd((t, d), BF16), sd((t, d), BF16), sd((SUBLANE, d), F32)],
        in_specs=[wide, wide, row, row, pl.BlockSpec((1, d), lambda i: (0, 0)), ANY],
        out_specs=[row, row, row, pl.BlockSpec((SUBLANE, d), lambda i: (0, 0))],
        scratch_shapes=[pltpu.VMEM((ff, d), BF16)] * 2 + [pltpu.SemaphoreType.DMA((2 * N_DEV,))],
        args=(dgg, duu, dx2, x1, g_row, gath))


def _merge_bwd(dx1b, oa, ob, ya, yb, p, gbias, gath, grp, name, comms=()):
    t, d = oa.shape
    tm = _row_tile(t, 512)
    n_t = t // tm

    def body(dx_ref, oa_ref, ob_ref, ya_ref, yb_ref, ga_ref, gb_ref, gbias_ref, gath_ref,
             dya_ref, dyb_ref, dp_ref, dgb_ref, dwoa_ref, dwob_ref, dwo_ref,
             w_oa, w_ob, w_o, acc_oa, acc_ob, acc_o, stage, sems, out_sems):
        @pl.when(pl.program_id(0) == 0)
        def _():
            _load_weights(gath_ref, grp, ["oa", "ob", "o"], [w_oa, w_ob, w_o], sems)
            for ref in (dgb_ref, acc_oa, acc_ob, acc_o):
                _zero(ref)

        dxb = dx_ref[...]
        dm = _dot_nt(dxb, w_o[...]).astype(BF16)
        oa = oa_ref[...]
        ob = ob_ref[...]
        sa = _sigmoid(ga_ref[...] + gbias_ref[0:1, :].astype(BF16))
        sb = _sigmoid(gb_ref[...] + gbias_ref[1:2, :].astype(BF16))
        acc_o[...] += _dot_tn(sa * oa + sb * ob, dxb)
        doa = dm * sa
        dob = dm * sb
        acc_oa[...] += _dot_tn(ya_ref[...], doa)
        acc_ob[...] += _dot_tn(yb_ref[...], dob)
        dga = dm * oa * sa * (1.0 - sa)
        dgb = dm * ob * sb * (1.0 - sb)
        step = pl.program_id(0)
        slot = step % 2

        def to_dp(k, at_step):
            return pltpu.make_async_copy(stage.at[k], dp_ref.at[pl.ds(at_step * tm, tm), pl.ds(5 * d, 2 * d)], out_sems.at[k])

        @pl.when(step >= 2)
        def _():
            to_dp(slot, step - 2).wait()

        stage[slot, :, 0:d] = dga
        stage[slot, :, d:2 * d] = dgb
        to_dp(slot, step).start()
        ones = jnp.ones((SUBLANE, tm), BF16)
        dgb_ref[0:1, :] += _dot_nn(ones, dga)[0:1, :]
        dgb_ref[1:2, :] += _dot_nn(ones, dgb)[0:1, :]
        dya_ref[...] = _dot_nt(doa, w_oa[...]).astype(BF16)
        dyb_ref[...] = _dot_nt(dob, w_ob[...]).astype(BF16)

        @pl.when(pl.program_id(0) == n_t - 1)
        def _():
            outs = []
            for n, (acc, stage, dst) in enumerate(((acc_oa, w_oa, dwoa_ref), (acc_ob, w_ob, dwob_ref), (acc_o, w_o, dwo_ref))):
                stage[...] = acc[...].astype(BF16)
                outs.append(pltpu.make_async_copy(stage, dst, sems.at[n]))
                outs[-1].start()
            for cp in outs:
                cp.wait()
            for back in range(min(2, n_t)):
                to_dp((n_t - 1 - back) % 2, n_t - 1 - back).wait()

    row = pl.BlockSpec((tm, d), lambda i: (i, 0))
    sd = jax.ShapeDtypeStruct
    return _hosted_call(
        body, comms, name=name, grid=(n_t,),
        out_shape=[sd((t, d), BF16), sd((t, d), BF16), sd((t, N_PROJ * d), BF16), sd((SUBLANE, d), F32),
                   sd((d, d), BF16), sd((d, d), BF16), sd((d, d), BF16)],
        in_specs=[row, row, row, row, row, pl.BlockSpec((tm, d), lambda i: (i, 5)), pl.BlockSpec((tm, d), lambda i: (i, 6)),
                  pl.BlockSpec((SUBLANE, d), lambda i: (0, 0)), ANY],
        out_specs=[row, row, ANY, pl.BlockSpec((SUBLANE, d), lambda i: (0, 0)), ANY, ANY, ANY],
        scratch_shapes=[pltpu.VMEM((d, d), BF16)] * 3 + [pltpu.VMEM((d, d), F32)] * 3
        + [pltpu.VMEM((2, tm, 2 * d), BF16), pltpu.SemaphoreType.DMA((3 * N_DEV,)), pltpu.SemaphoreType.DMA((2,))],
        args=(dx1b, oa, ob, ya, yb, p, p, gbias, gath))


DV_CONV_B_B, DV_BA, DV_BX, DV_SP, DV_CONV_A, DV_CONV_B = 0, 1, 2, 3, 4, 7
DV_ROWS = 16


def _mixer_bwd(dya, dyb, dp_gates, p, u_s, h_s, caw, cbw, vec, bda, bdx, name, comms=()):
    t, d = dya.shape
    tm = _time_tile(t)
    n_t = t // tm
    nk = tm // SUBLANE
    cw = min(MXU_TILE, d)
    nb = d // cw
    halo = 4 * SUBLANE
    ka, kb = CONV_A_K - 1, CONV_B_K - 1

    def body(dya_ref, dyb_ref, _, ba_ref, ca_ref, xa_ref, xb_ref, gb_ref, cah_ref, xah_ref, xbh_ref,
             u_ref, h_ref, hh_ref, caw_ref, cbw_ref, vec_ref, bda_ref, bdx_ref,
             dp_ref, dv_ref, dwa_ref, dwx_ref,
             zbuf, xbuf, hbuf, dczbuf, dubuf, a2buf, a1buf, lbuf, dcz_head, du_head, a_head, lam_head):
        i = pl.program_id(0)

        @pl.when(i == 0)
        def _():
            for ref in (dv_ref, dwa_ref, dwx_ref, dcz_head, du_head, a_head, lam_head):
                _zero(ref)

        has_prev = jnp.where(i < n_t - 1, 1.0, 0.0).astype(F32)
        row = lax.broadcasted_iota(jnp.int32, (SUBLANE, cw), 0)

        def colsum(v):
            return jnp.sum(v, axis=0, keepdims=True)

        for j in range(nb):
            cs = slice(j * cw, (j + 1) * cw)
            ca = ca_ref[:, cs].astype(F32)
            xa = xa_ref[:, cs].astype(F32)
            z = ca * xa
            z_before = cah_ref[:, cs].astype(F32) * xah_ref[:, cs].astype(F32) * has_prev
            _causal_fill(zbuf, z, z_before[halo - ka * SUBLANE:, :], ka, row)
            z2 = zbuf[0:tm, :]
            z1 = zbuf[SUBLANE:SUBLANE + tm, :]
            w0, w1, w2 = caw_ref[0:1, cs], caw_ref[1:2, cs], caw_ref[2:3, cs]
            cz = w0 * z2 + w1 * z1 + w2 * z
            dya = dya_ref[:, cs].astype(F32)
            dp_ref[:, 0 * d + j * cw:0 * d + (j + 1) * cw] = (dya * cz).astype(BF16)
            dcz = dya * ba_ref[:, cs].astype(F32)
            _anticausal_fill(dczbuf, dcz, dcz_head[:, cs], ka, row)
            dcz_head[:, cs] = dcz[0:ka * SUBLANE, :]
            dz = w2 * dcz + w1 * dczbuf[SUBLANE:SUBLANE + tm, :] + w0 * dczbuf[2 * SUBLANE:2 * SUBLANE + tm, :]
            dv_ref[DV_CONV_A + 0:DV_CONV_A + 1, cs] += colsum(dcz * z2)
            dv_ref[DV_CONV_A + 1:DV_CONV_A + 2, cs] += colsum(dcz * z1)
            dv_ref[DV_CONV_A + 2:DV_CONV_A + 3, cs] += colsum(dcz * z)
            dp_ref[:, 1 * d + j * cw:1 * d + (j + 1) * cw] = (dz * xa).astype(BF16)
            dp_ref[:, 2 * d + j * cw:2 * d + (j + 1) * cw] = (dz * ca).astype(BF16)
            h = h_ref[:, cs].astype(F32)
            h_before = hh_ref[:, cs].astype(F32) * has_prev
            _causal_fill(hbuf, h, h_before[halo - SUBLANE:, :], 1, row)
            h_prev = hbuf[0:tm, :]
            dyb = dyb_ref[:, cs].astype(F32)
            gel, dgel = _gelu_and_grad(gb_ref[:, cs].astype(F32))
            dp_ref[:, 4 * d + j * cw:4 * d + (j + 1) * cw] = (dyb * h * dgel).astype(BF16)
            ub = u_ref[:, cs]
            u = ub.astype(F32)
            sp = vec_ref[3:4, cs]
            r, gi, a, s, inv_s = _lru_gates(ub, bda_ref[j], bdx_ref[j], vec_ref[1:2, cs], vec_ref[2:3, cs], sp)
            _anticausal_fill(a2buf, a, a_head[:, cs], 1, row)
            a_head[:, cs] = a[0:SUBLANE, :]
            a1buf[...] = a2buf[SUBLANE:SUBLANE + tm, :]
            lbuf[...] = dyb * gel
            l_end, a_prod = _chain_scan(a1buf, lbuf, nk, reverse=True)
            a_inc, l_inc = _sublane_scan(a_prod, l_end, row, reverse=True)
            carry = lam_head[:, cs]
            ends = l_inc + a_inc * carry
            starts = jnp.where(row == SUBLANE - 1, carry, pltpu.roll(ends, SUBLANE - 1, 0))
            lam_head[:, cs] = jnp.broadcast_to(ends[0:1, :], (SUBLANE, cw))
            lam = (lbuf[...].reshape(nk, SUBLANE, cw) + a1buf[...].reshape(nk, SUBLANE, cw) * starts[None]).reshape(tm, cw)
            da = lam * h_prev
            iu = gi * u
            ds = lam * iu
            di = lam * s * u
            du = lam * s * gi
            dlog_a = da * a - ds * (a * a) * inv_s
            dv_ref[DV_SP:DV_SP + 1, cs] += colsum(dlog_a * r) * (-LRU_C)
            dpr = dlog_a * ((-LRU_C) * sp) * r * (1.0 - r)
            dpi = di * gi * (1.0 - gi)
            dv_ref[DV_BA:DV_BA + 1, cs] += colsum(dpr)
            dv_ref[DV_BX:DV_BX + 1, cs] += colsum(dpi)
            dprb = dpr.astype(BF16)
            dpib = dpi.astype(BF16)
            du = du + _dot_nt(dprb, bda_ref[j]) + _dot_nt(dpib, bdx_ref[j])
            dwa_ref[j] += _dot_tn(ub, dprb)
            dwx_ref[j] += _dot_tn(ub, dpib)
            xb = xb_ref[:, cs].astype(F32)
            x_before = xbh_ref[:, cs].astype(F32) * has_prev
            _causal_fill(xbuf, xb, x_before[halo - kb * SUBLANE:, :], kb, row)
            _anticausal_fill(dubuf, du, du_head[:, cs], kb, row)
            du_head[:, cs] = du[0:kb * SUBLANE, :]
            v0, v1, v2, v3 = cbw_ref[0:1, cs], cbw_ref[1:2, cs], cbw_ref[2:3, cs], cbw_ref[3:4, cs]
            dxb = (v3 * du + v2 * dubuf[SUBLANE:SUBLANE + tm, :] + v1 * dubuf[2 * SUBLANE:2 * SUBLANE + tm, :]
                   + v0 * dubuf[3 * SUBLANE:3 * SUBLANE + tm, :])
            dp_ref[:, 3 * d + j * cw:3 * d + (j + 1) * cw] = dxb.astype(BF16)
            dv_ref[DV_CONV_B_B:DV_CONV_B_B + 1, cs] += colsum(du)
            dv_ref[DV_CONV_B + 0:DV_CONV_B + 1, cs] += colsum(du * xbuf[0:tm, :])
            dv_ref[DV_CONV_B + 1:DV_CONV_B + 2, cs] += colsum(du * xbuf[SUBLANE:SUBLANE + tm, :])
            dv_ref[DV_CONV_B + 2:DV_CONV_B + 3, cs] += colsum(du * xbuf[2 * SUBLANE:2 * SUBLANE + tm, :])
            dv_ref[DV_CONV_B + 3:DV_CONV_B + 4, cs] += colsum(du * xb)

    rt = lambda i: n_t - 1 - i
    row_spec = pl.BlockSpec((tm, d), lambda i: (rt(i), 0))
    slab = lambda s: pl.BlockSpec((tm, d), lambda i, s=s: (rt(i), s))
    before = lambda s: pl.BlockSpec((halo, d), lambda i, s=s: (jnp.maximum(rt(i) * (tm // halo) - 1, 0), s))
    small = pl.BlockSpec((SUBLANE, d), lambda i: (0, 0))
    bd = pl.BlockSpec((nb, cw, cw), lambda i: (0, 0, 0))
    sd = jax.ShapeDtypeStruct
    wbuf = lambda n: pltpu.VMEM((tm + n * SUBLANE, cw), F32)
    head = lambda n: pltpu.VMEM((n * SUBLANE, d), F32)
    return _hosted_call(
        body, comms, name=name, grid=(n_t,),
        out_shape=[sd((t, N_PROJ * d), BF16), sd((DV_ROWS, d), F32), sd((nb, cw, cw), F32), sd((nb, cw, cw), F32)],
        in_specs=[row_spec, row_spec, ANY,
                  slab(0), slab(1), slab(2), slab(3), slab(4), before(1), before(2), before(3),
                  row_spec, row_spec, before(0), small, small, small, bd, bd],
        out_specs=[pl.BlockSpec((tm, 5 * d), lambda i: (rt(i), 0)), pl.BlockSpec((DV_ROWS, d), lambda i: (0, 0)), bd, bd],
        aliases={2: 0},
        scratch_shapes=[wbuf(ka), wbuf(kb), wbuf(1), wbuf(ka), wbuf(kb), wbuf(1),
                        pltpu.VMEM((tm, cw), F32), pltpu.VMEM((tm, cw), F32), head(ka), head(kb), head(1), head(1)],
        args=(dya, dyb, dp_gates, p, p, p, p, p, p, p, p, u_s, h_s, h_s, caw, cbw, vec, bda, bdx))


def _in_proj_bwd(dp, x, dx1, g_row, gath, grp, name, comms=()):
    t, d = x.shape
    tm = _row_tile(t, 512)
    n_in = N_PROJ * d

    def body(dp_ref, x_ref, dx1_ref, g_ref, gath_ref, dx_ref, dg_ref, w_in, sems):
        @pl.when(pl.program_id(0) == 0)
        def _():
            _load_weights(gath_ref, grp, ["in"], [w_in], sems)
            _zero(dg_ref)

        dh = _dot_nn(dp_ref[:, 0:d], w_in[0:d, :])
        for k in range(1, N_PROJ):
            dh = dh + _dot_nn(dp_ref[:, k * d:(k + 1) * d], w_in[k * d:(k + 1) * d, :])
        xf = x_ref[...]
        rstd = lax.rsqrt(jnp.mean(xf * xf, axis=-1, keepdims=True) + RMS_EPS)
        xh = xf * rstd
        g = g_ref[...]
        dg_ref[0:1, :] += jnp.sum(dh * xh, axis=0, keepdims=True)
        dxh = dh * g
        dx_ref[...] = dx1_ref[...] + rstd * (dxh - xh * jnp.mean(dxh * xh, axis=-1, keepdims=True))

    row = pl.BlockSpec((tm, d), lambda i: (i, 0))
    sd = jax.ShapeDtypeStruct
    return _hosted_call(
        body, comms, name=name, grid=(t // tm,),
        out_shape=[sd((t, d), F32), sd((SUBLANE, d), F32)],
        in_specs=[pl.BlockSpec((tm, n_in), lambda i: (i, 0)), row, row, pl.BlockSpec((1, d), lambda i: (0, 0)), ANY],
        out_specs=[row, pl.BlockSpec((SUBLANE, d), lambda i: (0, 0))],
        scratch_shapes=[pltpu.VMEM((n_in, d), BF16), pltpu.SemaphoreType.DMA((N_DEV,))],
        args=(dp, x, dx1, g_row, gath))


def _weight_grad(a, b, name):
    t, m = a.shape
    n = b.shape[1]
    bt = _row_tile(t, 1024)
    bm = m
    for div in (1, 2, 4, 8):
        if m % div == 0 and (m // div) % LANE == 0 and (m // div) * n * 4 <= (12 << 20):
            bm = m // div
            break
    n_t = t // bt

    def body(a_ref, b_ref, o_ref, acc):
        k = pl.program_id(1)

        @pl.when(k == 0)
        def _():
            _zero(acc)

        acc[...] += _dot_tn(a_ref[...], b_ref[...])

        @pl.when(k == n_t - 1)
        def _():
            o_ref[...] = acc[...].astype(BF16)

    return pl.pallas_call(
        body, name=name, grid=(m // bm, n_t),
        out_shape=jax.ShapeDtypeStruct((m, n), BF16),
        in_specs=[pl.BlockSpec((bt, bm), lambda i, k: (k, i)), pl.BlockSpec((bt, n), lambda i, k: (k, 0))],
        out_specs=pl.BlockSpec((bm, n), lambda i, k: (i, 0)),
        scratch_shapes=[pltpu.VMEM((bm, n), F32)],
        compiler_params=_params(2),
    )(a, b)


def _adamw(w, g, m, v, name):
    r, c = w.shape
    tr = _fit_rows(r, c * 4)
    c1 = 1.0 - ADAM_B1 ** ADAM_STEP
    c2 = 1.0 - ADAM_B2 ** ADAM_STEP

    def body(w_ref, g_ref, m_ref, v_ref, d_ref, nm_ref, nv_ref):
        g32 = g_ref[...]
        nm = ADAM_B1 * m_ref[...] + (1.0 - ADAM_B1) * g32
        nv = ADAM_B2 * v_ref[...] + (1.0 - ADAM_B2) * (g32 * g32)
        nm_ref[...] = nm
        nv_ref[...] = nv
        d_ref[...] = -ADAM_LR * ((nm / c1) / (jnp.sqrt(nv / c2) + ADAM_EPS) + ADAM_WD * w_ref[...])

    spec = pl.BlockSpec((tr, c), lambda i: (i, 0))
    return pl.pallas_call(
        body, name=name, grid=(r // tr,),
        out_shape=[jax.ShapeDtypeStruct((r, c), F32)] * 3,
        in_specs=[spec] * 4, out_specs=[spec] * 3,
        compiler_params=_params(),
    )(w, g, m, v)


def _pad_rows(a, mult=SUBLANE):
    pad = (-a.shape[0]) % mult
    return a if pad == 0 else jnp.concatenate([a, jnp.zeros((pad,) + a.shape[1:], a.dtype)], axis=0)


REPLICATED = ("ln1_g", "conv_b_b", "lru_wa", "lru_ba", "lru_wx", "lru_bx", "lru_lambda", "ln2_g", "final_g")
SMALL_SHARDED = ("conv_a_w", "conv_b_w", "gate_bias")
MATRICES = ("w_in", "w_out_a", "w_out_b", "w_o", "w_ffn_gate", "w_ffn_up", "w_ffn_down")
ORDER = ("ln1_g", "w_in", "conv_a_w", "conv_b_w", "conv_b_b", "lru_wa", "lru_ba", "lru_wx", "lru_bx", "lru_lambda",
         "w_out_a", "w_out_b", "gate_bias", "w_o", "ln2_g", "w_ffn_gate", "w_ffn_up", "w_ffn_down", "final_g")


def kernel(x, ln1_g, w_in, conv_a_w, conv_b_w, conv_b_b, lru_wa, lru_ba, lru_wx, lru_bx, lru_lambda, w_out_a, w_out_b, gate_bias, w_o, ln2_g, w_ffn_gate, w_ffn_up, w_ffn_down, final_g, loss_target, m_ln1_g, m_w_in, m_conv_a_w, m_conv_b_w, m_conv_b_b, m_lru_wa, m_lru_ba, m_lru_wx, m_lru_bx, m_lru_lambda, m_w_out_a, m_w_out_b, m_gate_bias, m_w_o, m_ln2_g, m_w_ffn_gate, m_w_ffn_up, m_w_ffn_down, m_final_g, v_ln1_g, v_w_in, v_conv_a_w, v_conv_b_w, v_conv_b_b, v_lru_wa, v_lru_ba, v_lru_wx, v_lru_bx, v_lru_lambda, v_w_out_a, v_w_out_b, v_gate_bias, v_w_o, v_ln2_g, v_w_ffn_gate, v_w_ffn_up, v_w_ffn_down, v_final_g):
    w = dict(ln1_g=ln1_g, w_in=w_in, conv_a_w=conv_a_w, conv_b_w=conv_b_w, conv_b_b=conv_b_b, lru_wa=lru_wa,
             lru_ba=lru_ba, lru_wx=lru_wx, lru_bx=lru_bx, lru_lambda=lru_lambda, w_out_a=w_out_a, w_out_b=w_out_b,
             gate_bias=gate_bias, w_o=w_o, ln2_g=ln2_g, w_ffn_gate=w_ffn_gate, w_ffn_up=w_ffn_up,
             w_ffn_down=w_ffn_down, final_g=final_g)
    mom = dict(ln1_g=m_ln1_g, w_in=m_w_in, conv_a_w=m_conv_a_w, conv_b_w=m_conv_b_w, conv_b_b=m_conv_b_b,
               lru_wa=m_lru_wa, lru_ba=m_lru_ba, lru_wx=m_lru_wx, lru_bx=m_lru_bx, lru_lambda=m_lru_lambda,
               w_out_a=m_w_out_a, w_out_b=m_w_out_b, gate_bias=m_gate_bias, w_o=m_w_o, ln2_g=m_ln2_g,
               w_ffn_gate=m_w_ffn_gate, w_ffn_up=m_w_ffn_up, w_ffn_down=m_w_ffn_down, final_g=m_final_g)
    var = dict(ln1_g=v_ln1_g, w_in=v_w_in, conv_a_w=v_conv_a_w, conv_b_w=v_conv_b_w, conv_b_b=v_conv_b_b,
               lru_wa=v_lru_wa, lru_ba=v_lru_ba, lru_wx=v_lru_wx, lru_bx=v_lru_bx, lru_lambda=v_lru_lambda,
               w_out_a=v_w_out_a, w_out_b=v_w_out_b, gate_bias=v_gate_bias, w_o=v_w_o, ln2_g=v_ln2_g,
               w_ffn_gate=v_w_ffn_gate, w_ffn_up=v_w_ffn_up, w_ffn_down=v_w_ffn_down, final_g=v_final_g)

    _, t, d = x.shape
    n_layers = w_in.shape[0]
    ff = w_ffn_down.shape[1] * N_DEV
    dd = d // N_DEV
    hd = d // LRU_HEADS
    cw = min(MXU_TILE, d)
    nb = d // cw
    hpt = cw // hd
    grp = _groups(d, ff)
    me = 4 * lax.axis_index("x") + 2 * lax.axis_index("y") + lax.axis_index("c")
    tm_time = _time_tile(t)
    x0 = _to_tile_order(x[0], tm_time)
    target = _to_tile_order(loss_target[0], tm_time)

    packed = [{"in": jnp.swapaxes(w_in[l], 0, 1).astype(BF16),
               "rest": jnp.concatenate([w_out_a[l], w_out_b[l], w_o[l], jnp.swapaxes(w_ffn_gate[l], 0, 1),
                                        jnp.swapaxes(w_ffn_up[l], 0, 1), w_ffn_down[l]], axis=0).astype(BF16)}
              for l in range(n_layers)]
    n_small = CONV_A_K + CONV_B_K + 2
    small = _pad_rows(jnp.concatenate([conv_a_w, conv_b_w, gate_bias], axis=1).reshape(n_layers * n_small, dd))
    sp = jax.nn.softplus(-lru_lambda)
    vec = [_pad_rows(jnp.stack([conv_b_b[l], lru_ba[l], lru_bx[l], sp[l]])) for l in range(n_layers)]
    eye = jnp.eye(hpt, dtype=F32)

    def block_diag(wh):
        return jnp.einsum("jkab,kl->jkalb", wh.reshape(nb, hpt, hd, hd), eye).reshape(nb, cw, cw).astype(BF16)

    bda = [block_diag(lru_wa[l]) for l in range(n_layers)]
    bdx = [block_diag(lru_wx[l]) for l in range(n_layers)]

    gath = [dict() for _ in range(n_layers)]
    (gath[0]["in"],), (small_g,) = _comm_call([_Gather(packed[0]["in"]), _Gather(small)], "gather_in_0")
    small_full = jnp.swapaxes(small_g[:, :n_layers * n_small], 0, 1).reshape(n_layers, n_small, d)
    caw = [_pad_rows(small_full[k, 0:CONV_A_K]) for k in range(n_layers)]
    cbw = [_pad_rows(small_full[k, CONV_A_K:CONV_A_K + CONV_B_K]) for k in range(n_layers)]
    gbias = [_pad_rows(small_full[k, CONV_A_K + CONV_B_K:]) for k in range(n_layers)]
    saved = []
    xl = x0
    for l in range(n_layers):
        more = l + 1 < n_layers
        (p, h1b, ya, yb, *kept), got = _in_proj_mixer_fwd(
            xl, ln1_g[l][None], gath[l]["in"], grp["in"], caw[l], cbw[l], vec[l], bda[l], bdx[l], f"in_proj_mixer_fwd_{l}",
            [_Gather(packed[0]["rest"])] if l == 0 else [])
        if l == 0:
            ((gath[0]["rest"],),) = got
        (x1, oa, ob), got = _merge_fwd(xl, ya, yb, p, gbias[l], gath[l]["rest"], grp["rest"], f"merge_fwd_{l}",
                                       [_Gather(packed[l + 1]["in"])] if more else [])
        if more:
            ((gath[l + 1]["in"],),) = got
        (x2, gg, uu), got = _ffn_fwd(x1, ln2_g[l][None], gath[l]["rest"], grp["rest"], f"ffn_fwd_{l}",
                                     [_Gather(packed[l + 1]["rest"])] if more else [])
        if more:
            ((gath[l + 1]["rest"],),) = got
        saved.append(dict(x=xl, p=p, h1b=h1b, ya=ya, yb=yb, mixer=kept, x1=x1, oa=oa, ob=ob, gg=gg, uu=uu))
        xl = x2
    loss_tile, dx, dfinal = _loss_head(xl, final_g[None], target, "loss_head")
    loss = lax.psum(loss_tile[0, 0], ("x", "y", "c"))

    def heads(dwb):
        blocks = jnp.diagonal(dwb.reshape(nb, hpt, hd, hpt, hd), axis1=1, axis2=3)
        return jnp.moveaxis(blocks, 3, 1).reshape(hd, d)

    layer_names = [n for n in REPLICATED if n != "final_g"] + list(SMALL_SHARDED)

    def layer_block(k):
        return jnp.concatenate([small_grads[k][n] for n in layer_names], axis=0)

    recv = [dict() for _ in range(n_layers)]
    small_grads = [None] * n_layers
    early_all = None
    xg = {"d": _Group(("d",), (ff // N_DEV,)), "gu": _Group(("g", "u"), (ff // N_DEV,) * 2),
          "out": _Group(("oa", "ob", "o"), (dd,) * 3), "in": grp["in"]}
    far_in = None
    for l in reversed(range(n_layers)):
        s = saved[l]
        (dgg, duu, dw_d), got = _ffn_bwd_act(dx, s["gg"], s["uu"], gath[l]["rest"], grp["rest"], f"ffn_bwd_act_{l}",
                                             [far_in] if far_in else [])
        if far_in:
            recv[l + 1]["in"].append(got[0][0])
        comms = [_Exchange({"d": dw_d}, xg["d"])]
        if l == 0:
            early = [layer_block(k) for k in range(1, n_layers)] + [_pad_rows(dfinal[0:1])]
            comms.append(_Gather(jnp.concatenate(early, axis=0)))
        (dx1, dx1b, h2b, dln2), got = _ffn_bwd_in(dgg, duu, dx, s["x1"], ln2_g[l][None], gath[l]["rest"], grp["rest"],
                                                  f"ffn_bwd_in_{l}", comms)
        recv[l]["d"] = [got[0][0]]
        if l == 0:
            early_all = got[1][0]
        dw_gu = {"g": _weight_grad(dgg, h2b, f"dw_ffn_gate_{l}"), "u": _weight_grad(duu, h2b, f"dw_ffn_up_{l}")}
        (dya, dyb, dp_gates, dgbias, dw_oa, dw_ob, dw_o), got = _merge_bwd(
            dx1b, s["oa"], s["ob"], s["ya"], s["yb"], s["p"], gbias[l], gath[l]["rest"], grp["rest"], f"merge_bwd_{l}",
            [_Exchange(dw_gu, xg["gu"])])
        recv[l]["gu"] = [got[0][0]]
        (dp, dv, dwa, dwx), got = _mixer_bwd(dya, dyb, dp_gates, s["p"], *s["mixer"], caw[l], cbw[l], vec[l], bda[l], bdx[l],
                                             f"mixer_bwd_{l}", [_Exchange({"oa": dw_oa, "ob": dw_ob, "o": dw_o}, xg["out"])])
        recv[l]["out"] = [got[0][0]]
        small_grads[l] = {
            "conv_b_b": dv[DV_CONV_B_B:DV_CONV_B_B + 1], "lru_wa": heads(dwa),
            "lru_ba": dv[DV_BA:DV_BA + 1], "lru_wx": heads(dwx), "lru_bx": dv[DV_BX:DV_BX + 1],
            "lru_lambda": dv[DV_SP:DV_SP + 1] * (-jax.nn.sigmoid(-lru_lambda[l]))[None], "ln2_g": dln2[0:1],
            "conv_a_w": dv[DV_CONV_A:DV_CONV_A + CONV_A_K], "conv_b_w": dv[DV_CONV_B:DV_CONV_B + CONV_B_K],
            "gate_bias": dgbias[0:2],
        }
        dw_in = {"in": _weight_grad(dp, s["h1b"], f"dw_in_{l}")}
        if l > 0:
            near_in, far_in = _Exchange(dw_in, xg["in"], NEAR_PEERS), _Exchange(dw_in, xg["in"], FAR_PEERS, local=False)
        else:
            near_in, far_in = _Exchange(dw_in, xg["in"]), None
        (dx, dln1), got = _in_proj_bwd(dp, s["x"], dx1, ln1_g[l][None], gath[l]["in"], grp["in"], f"in_proj_bwd_{l}", [near_in])
        recv[l]["in"] = [got[0][0]]
        small_grads[l]["ln1_g"] = dln1[0:1]
    grad_x = _from_tile_order(dx, tm_time)[None]

    g = {}
    gsum = [{k: _sum_slots(recv[l][k], f"sum_{k}_{l}") for k in xg} for l in range(n_layers)]

    def part(key):
        k = next(name for name, group in xg.items() if key in group.keys)
        o, r = xg[k].off[key], xg[k].rows[key]
        return jnp.stack([gsum[l][k][o:o + r] for l in range(n_layers)])

    g = {"w_in": jnp.swapaxes(part("in"), 1, 2), "w_out_a": part("oa"), "w_out_b": part("ob"), "w_o": part("o"),
         "w_ffn_gate": jnp.swapaxes(part("g"), 1, 2), "w_ffn_up": jnp.swapaxes(part("u"), 1, 2), "w_ffn_down": part("d")}
    ((late_all,),) = _comm_call([_Gather(layer_block(0).astype(BF16))], "gather_small_grads_0")
    early_sum = _sum_slots([early_all], "sum_small_grads")
    block_rows = late_all.shape[1]
    per_layer = [_sum_slots([late_all], "sum_small_grads_0")]
    per_layer += [early_sum[(k - 1) * block_rows:k * block_rows] for k in range(1, n_layers)]
    g["final_g"] = early_sum[(n_layers - 1) * block_rows].reshape(w["final_g"].shape)
    o = 0
    for n in layer_names:
        rows = small_grads[0][n].shape[0]
        stacked = jnp.concatenate([per_layer[k][o:o + rows] for k in range(n_layers)], axis=0)
        if n in SMALL_SHARDED:
            g[n] = lax.dynamic_slice_in_dim(stacked, me * dd, dd, axis=1).reshape(n_layers, rows, dd)
        else:
            g[n] = stacked.reshape(w[n].shape)
        o += rows

    delta, new_m, new_v = {}, {}, {}
    gate_maps = ("lru_wa", "lru_wx")
    for n in MATRICES + gate_maps:
        shape = w[n].shape
        flat = lambda a: a.reshape(-1, d if n in gate_maps else shape[-1])
        dl, nm, nv = _adamw(flat(w[n]), flat(g[n]), flat(mom[n]), flat(var[n]), f"adamw_{n}")
        delta[n], new_m[n], new_v[n] = dl.reshape(shape), nm.reshape(shape), nv.reshape(shape)
    vectors = tuple(n for n in REPLICATED if n not in gate_maps)
    for group, width, name in ((vectors, d, "adamw_replicated"), (SMALL_SHARDED, dd, "adamw_vectors")):
        cat = lambda src: _pad_rows(jnp.concatenate([src[n].reshape(-1, width) for n in group], axis=0))
        dl, nm, nv = _adamw(cat(w), cat(g), cat(mom), cat(var), name)
        o = 0
        for n in group:
            rows = w[n].size // width
            delta[n], new_m[n], new_v[n] = (a[o:o + rows].reshape(w[n].shape) for a in (dl, nm, nv))
            o += rows

    return (loss, grad_x, *[g[n] for n in ORDER], *[delta[n] for n in ORDER], *[new_m[n] for n in ORDER],
            *[new_v[n] for n in ORDER])
```

```python
import math

import jax
import jax.numpy as jnp
from jax import lax
from jax.experimental import pallas as pl
from jax.experimental.pallas import tpu as pltpu

F32 = jnp.float32
BF16 = jnp.bfloat16

N_DEV = 8
N_PROJ = 7
LRU_HEADS = 16
LRU_C = 8.0
RMS_EPS = 1e-6
CONV_A_K = 3
CONV_B_K = 4
GELU_C = math.sqrt(2.0 / math.pi)
GELU_A = 0.044715

ADAM_LR = 0.001
ADAM_B1 = 0.9
ADAM_B2 = 0.999
ADAM_EPS = 1e-08
ADAM_WD = 0.01
ADAM_STEP = 10

LANE = 128
SUBLANE = 8
MXU_TILE = 256
VMEM_LIMIT = 52 << 20
ALL_PEERS = tuple(range(1, N_DEV))
NEAR_PEERS = (1, 2, 3, 4, 5)
FAR_PEERS = (6, 7)
MESH = pl.DeviceIdType.MESH
ANY = pl.BlockSpec(memory_space=pl.ANY)


def _dot_nn(a, b):
    return lax.dot_general(a, b, (((1,), (0,)), ((), ())), preferred_element_type=F32)


def _dot_nt(a, b):
    return lax.dot_general(a, b, (((1,), (1,)), ((), ())), preferred_element_type=F32)


def _dot_tn(a, b):
    return lax.dot_general(a, b, (((0,), (0,)), ((), ())), preferred_element_type=F32)


def _sigmoid(x):
    return 1.0 / (1.0 + jnp.exp(-x))


def _gelu_and_grad(x):
    x2 = x * x
    t = jnp.tanh(GELU_C * x * (1.0 + GELU_A * x2))
    g = 0.5 * x * (1.0 + t)
    dg = 0.5 * (1.0 + t) + 0.5 * x * (1.0 - t * t) * GELU_C * (1.0 + 3.0 * GELU_A * x2)
    return g, dg


def _zero(ref):
    ref[...] = jnp.zeros(ref.shape, ref.dtype)


def _fit_rows(r, row_bytes, budget=1 << 20):
    fits = [t for t in range(16, r + 1, 16) if r % t == 0 and t * row_bytes <= budget]
    return max(fits) if fits else r


def _row_tile(t, want):
    tm = min(want, t // 2)
    assert t % tm == 0 and tm % SUBLANE == 0, (t, tm)
    return tm


def _params(n_grid=1, **kw):
    return pltpu.CompilerParams(dimension_semantics=("arbitrary",) * n_grid, vmem_limit_bytes=VMEM_LIMIT, **kw)


class _Group:
    def __init__(self, keys, rows):
        self.keys = keys
        self.rows = dict(zip(keys, rows))
        self.off, o = {}, 0
        for k in keys:
            self.off[k] = o
            o += self.rows[k]
        self.total = o


def _groups(d, ff):
    dd, ffs = d // N_DEV, ff // N_DEV
    return {"in": _Group(("in",), (N_PROJ * dd,)),
            "rest": _Group(("oa", "ob", "o", "g", "u", "d"), (dd, dd, dd, ffs, ffs, ffs))}


def _load_weights(g_ref, grp, keys, dsts, sems):
    copies = []
    for n, (k, dst) in enumerate(zip(keys, dsts)):
        rows, off = grp.rows[k], grp.off[k]
        copies += [pltpu.make_async_copy(g_ref.at[p, pl.ds(off, rows), :], dst.at[pl.ds(p * rows, rows), :],
                                         sems.at[n * N_DEV + p]) for p in range(N_DEV)]
    for c in copies:
        c.start()
    for c in copies:
        c.wait()


def _comm_sems():
    return [pltpu.SemaphoreType.DMA((N_DEV - 1,)), pltpu.SemaphoreType.DMA((N_DEV - 1,)), pltpu.SemaphoreType.DMA]


class _Gather:
    def __init__(self, x):
        self.inputs = [x]
        self.out_shape = [jax.ShapeDtypeStruct((N_DEV,) + x.shape, x.dtype)]
        self.scratch = _comm_sems()

    def _plan(self, ins, outs, scr):
        (x_ref,), (out_ref,), (send_sems, recv_sems, local_sem) = ins, outs, scr
        mx, my, mc = lax.axis_index("x"), lax.axis_index("y"), lax.axis_index("c")
        me, sibling = (mx, my, mc), (mx, my, 1 - mc)
        xn, yn, dg = (1 - mx, my), (mx, 1 - my), (1 - mx, 1 - my)
        core0 = mc == 0
        relayed = (jnp.where(core0, 1 - mx, mx), jnp.where(core0, my, 1 - my))
        relay_to = (jnp.where(core0, mx, 1 - mx), jnp.where(core0, 1 - my, my))

        def slot(px, py, pc):
            return out_ref.at[4 * px + 2 * py + pc]

        def copy(k, block, to, src=None):
            return pltpu.make_async_remote_copy(
                src_ref=slot(*block) if src is None else src, dst_ref=slot(*block),
                send_sem=send_sems.at[k], recv_sem=recv_sems.at[k], device_id=to, device_id_type=MESH)

        mine = lambda: pltpu.make_async_copy(x_ref, slot(*me), local_sem)
        own = [lambda: copy(0, me, sibling, src=x_ref), lambda: copy(1, me, (*xn, mc), src=x_ref),
               lambda: copy(2, me, (*yn, mc), src=x_ref)]
        relay = lambda: copy(3, (*relayed, mc), (*relay_to, mc))
        passes = [lambda: copy(4, (*xn, mc), sibling), lambda: copy(5, (*yn, mc), sibling), lambda: copy(6, (*dg, mc), sibling)]
        arrival = lambda k: copy(k, me, me)
        return mine, own, relay, passes, arrival

    def start(self, ins, outs, scr):
        mine, own, _, _, _ = self._plan(ins, outs, scr)
        mine().start()
        for cp in own:
            cp().start()

    def relay(self, ins, outs, scr):
        _, _, relay, passes, arrival = self._plan(ins, outs, scr)
        arrival(1).wait_recv()
        arrival(2).wait_recv()
        relay().start()
        passes[0]().start()
        passes[1]().start()

    def mid(self, ins, outs, scr):
        _, _, _, passes, arrival = self._plan(ins, outs, scr)
        arrival(3).wait_recv()
        passes[2]().start()

    def finish(self, ins, outs, scr):
        mine, _, _, _, arrival = self._plan(ins, outs, scr)
        for k in (0, 4, 5, 6):
            arrival(k).wait_recv()
        for k in range(N_DEV - 1):
            arrival(k).wait_send()
        mine().wait()


class _Exchange:
    def __init__(self, mats, grp, peers=ALL_PEERS, local=True):
        self.grp, self.peers, self.local = grp, tuple(peers), local
        self.inputs = [mats[k] for k in grp.keys]
        slots = len(self.peers) + (1 if local else 0)
        self.out_shape = [jax.ShapeDtypeStruct((slots, grp.total, self.inputs[0].shape[1]), BF16)]
        self.scratch = [pltpu.SemaphoreType.DMA((len(self.peers),)), pltpu.SemaphoreType.DMA((len(self.peers),)),
                        pltpu.SemaphoreType.DMA]

    def _pieces(self, g_refs, out_ref, q, dst_slot):
        out = []
        for g_ref, k in zip(g_refs, self.grp.keys):
            rows = self.grp.rows[k]
            out.append((g_ref.at[pl.ds(pl.multiple_of(q * rows, 16), rows), :],
                        out_ref.at[dst_slot, pl.ds(self.grp.off[k], rows), :]))
        return out

    def start(self, ins, outs, scr):
        (out_ref,), (send_sems, recv_sems, local_sem) = outs, scr
        mx, my, mc = lax.axis_index("x"), lax.axis_index("y"), lax.axis_index("c")
        if self.local:
            for s, t in self._pieces(ins, out_ref, 4 * mx + 2 * my + mc, 0):
                pltpu.make_async_copy(s, t, local_sem).start()
        for n, k in enumerate(self.peers):
            px, py, pc = mx ^ ((k >> 2) & 1), my ^ ((k >> 1) & 1), mc ^ (k & 1)
            for s, t in self._pieces(ins, out_ref, 4 * px + 2 * py + pc, n + (1 if self.local else 0)):
                pltpu.make_async_remote_copy(src_ref=s, dst_ref=t, send_sem=send_sems.at[n], recv_sem=recv_sems.at[n],
                                             device_id=(px, py, pc), device_id_type=MESH).start()

    def relay(self, ins, outs, scr):
        pass

    def mid(self, ins, outs, scr):
        pass

    def finish(self, ins, outs, scr):
        (out_ref,), (send_sems, recv_sems, local_sem) = outs, scr
        mx, my, mc = lax.axis_index("x"), lax.axis_index("y"), lax.axis_index("c")
        whole = out_ref.at[0]
        for n in range(len(self.peers)):
            done = pltpu.make_async_remote_copy(src_ref=whole, dst_ref=whole, send_sem=send_sems.at[n],
                                                recv_sem=recv_sems.at[n], device_id=(mx, my, mc), device_id_type=MESH)
            done.wait_send()
            done.wait_recv()
        if self.local:
            pltpu.make_async_copy(whole, whole, local_sem).wait()


def _split(refs, sizes):
    out, pos = [], 0
    for n in sizes:
        out.append(refs[pos:pos + n])
        pos += n
    return out


def _hosted_call(body, comms, *, name, grid, in_specs, out_specs, out_shape, scratch_shapes, args, aliases=None):
    n_steps = grid[0]
    nc = len(comms)
    sizes = ([len(in_specs)] + [len(c.inputs) for c in comms] + [len(out_specs)] + [len(c.out_shape) for c in comms]
             + [len(scratch_shapes)] + [len(c.scratch) for c in comms])

    def hosted(*refs):
        parts = _split(refs, sizes)
        ins, c_ins = parts[0], parts[1:1 + nc]
        outs, c_outs = parts[1 + nc], parts[2 + nc:2 + 2 * nc]
        scr, c_scr = parts[2 + 2 * nc], parts[3 + 2 * nc:]
        step = pl.program_id(0)
        if comms:
            @pl.when(step == 0)
            def _():
                for c, a, b, s in zip(comms, c_ins, c_outs, c_scr):
                    c.start(a, b, s)

            relay_step = (3 * n_steps) // 5

            @pl.when(step == relay_step)
            def _():
                for c, a, b, s in zip(comms, c_ins, c_outs, c_scr):
                    c.relay(a, b, s)

            @pl.when(step == max(n_steps - 2, relay_step))
            def _():
                for c, a, b, s in zip(comms, c_ins, c_outs, c_scr):
                    c.mid(a, b, s)

        body(*ins, *outs, *scr)
        if comms:
            @pl.when(step == n_steps - 1)
            def _():
                for c, a, b, s in zip(comms, c_ins, c_outs, c_scr):
                    c.finish(a, b, s)

    res = pl.pallas_call(
        hosted, name=name, grid=grid,
        out_shape=[*out_shape, *[o for c in comms for o in c.out_shape]],
        in_specs=[*in_specs, *[ANY for c in comms for _ in c.inputs]],
        out_specs=[*out_specs, *[ANY for c in comms for _ in c.out_shape]],
        scratch_shapes=[*scratch_shapes, *[s for c in comms for s in c.scratch]],
        input_output_aliases=aliases or {},
        compiler_params=_params(),
    )(*args, *[a for c in comms for a in c.inputs])
    main, rest = res[:len(out_specs)], res[len(out_specs):]
    return main, _split(rest, [len(c.out_shape) for c in comms])


def _comm_call(comms, name):
    sizes = [len(c.inputs) for c in comms] + [len(c.out_shape) for c in comms] + [len(c.scratch) for c in comms]
    nc = len(comms)

    def body(*refs):
        parts = _split(refs, sizes)
        triples = list(zip(comms, parts[:nc], parts[nc:2 * nc], parts[2 * nc:]))
        for phase in ("start", "relay", "mid", "finish"):
            for c, ins, outs, scr in triples:
                getattr(c, phase)(ins, outs, scr)

    res = pl.pallas_call(
        body, name=name, out_shape=[o for c in comms for o in c.out_shape],
        in_specs=[ANY for c in comms for _ in c.inputs], out_specs=[ANY for c in comms for _ in c.out_shape],
        scratch_shapes=[s for c in comms for s in c.scratch],
    )(*[a for c in comms for a in c.inputs])
    return _split(res, [len(c.out_shape) for c in comms])


def _sum_slots(xs, name):
    _, r, c = xs[0].shape
    tr = _fit_rows(r, c * 4)

    def body(*refs):
        acc = None
        for x_ref in refs[:-1]:
            for p in range(x_ref.shape[0]):
                v = x_ref[p].astype(F32)
                acc = v if acc is None else acc + v
        refs[-1][...] = acc

    return pl.pallas_call(
        body, name=name, grid=(r // tr,),
        out_shape=jax.ShapeDtypeStruct((r, c), F32),
        in_specs=[pl.BlockSpec((x.shape[0], tr, c), lambda i: (0, i, 0)) for x in xs],
        out_specs=pl.BlockSpec((tr, c), lambda i: (i, 0)),
        compiler_params=_params(),
    )(*xs)


def _time_tile(t):
    return _row_tile(t, 256)


def _to_tile_order(a, tm):
    t, c = a.shape
    return jnp.swapaxes(a.reshape(t // tm, SUBLANE, tm // SUBLANE, c), 1, 2).reshape(t, c)


def _from_tile_order(a, tm):
    t, c = a.shape
    return jnp.swapaxes(a.reshape(t // tm, tm // SUBLANE, SUBLANE, c), 1, 2).reshape(t, c)


def _causal_fill(buf, v, prev_tail, n, row):
    tm = v.shape[0]
    for q in range(n):
        cur = v[tm - SUBLANE * (n - q):tm - SUBLANE * (n - q - 1), :]
        prv = prev_tail[SUBLANE * q:SUBLANE * (q + 1), :]
        buf[SUBLANE * q:SUBLANE * (q + 1), :] = jnp.where(row == 0, pltpu.roll(prv, 1, 0), pltpu.roll(cur, 1, 0))
    buf[SUBLANE * n:, :] = v


def _anticausal_fill(buf, v, next_head, n, row):
    tm = v.shape[0]
    buf[0:tm, :] = v
    for q in range(n):
        cur = v[SUBLANE * q:SUBLANE * (q + 1), :]
        nxt = next_head[SUBLANE * q:SUBLANE * (q + 1), :]
        buf[tm + SUBLANE * q:tm + SUBLANE * (q + 1), :] = jnp.where(
            row == SUBLANE - 1, pltpu.roll(nxt, SUBLANE - 1, 0), pltpu.roll(cur, SUBLANE - 1, 0))


def _chain_scan(abuf, bbuf, nk, reverse):
    cw = abuf.shape[1]

    def step(n, carry):
        h, c = carry
        r0 = pl.multiple_of((nk - 1 - n if reverse else n) * SUBLANE, SUBLANE)
        ak = abuf[pl.ds(r0, SUBLANE), :]
        h = ak * h + bbuf[pl.ds(r0, SUBLANE), :]
        c = ak * c
        bbuf[pl.ds(r0, SUBLANE), :] = h
        abuf[pl.ds(r0, SUBLANE), :] = c
        return h, c

    return lax.fori_loop(0, nk, step, (jnp.zeros((SUBLANE, cw), F32), jnp.ones((SUBLANE, cw), F32)), unroll=True)


def _sublane_scan(a, b, row, reverse):
    for sh in (1, 2, 4):
        if reverse:
            m = row < SUBLANE - sh
            b = jnp.where(m, a * pltpu.roll(b, SUBLANE - sh, 0) + b, b)
            a = jnp.where(m, a * pltpu.roll(a, SUBLANE - sh, 0), a)
        else:
            m = row >= sh
            b = jnp.where(m, a * pltpu.roll(b, sh, 0) + b, b)
            a = jnp.where(m, a * pltpu.roll(a, sh, 0), a)
    return a, b


def _lru_gates(ub, bda, bdx, ba, bx, sp):
    r = _sigmoid(_dot_nn(ub, bda) + ba)
    i = _sigmoid(_dot_nn(ub, bdx) + bx)
    log_a = (-LRU_C) * r * sp
    a = jnp.exp(log_a)
    s2 = -jnp.tanh(log_a) * (1.0 + a * a)
    inv_s = lax.rsqrt(s2)
    s = jnp.where(s2 > 0.0, s2 * inv_s, 0.0)
    return r, i, a, s, inv_s


def _in_proj_mixer_fwd(x, g_row, gath, grp, caw, cbw, vec, bda, bdx, name, comms=()):
    t, d = x.shape
    n_in = N_PROJ * d
    tm = _time_tile(t)
    nk = tm // SUBLANE
    cw = min(MXU_TILE, d)
    nb = d // cw

    def body(x_ref, g_ref, gath_ref, caw_ref, cbw_ref, vec_ref, bda_ref, bdx_ref,
             p_ref, h1_ref, ya_ref, yb_ref, u_ref, h_ref, w_in, sems, zbuf, xbuf, abuf, bbuf, z_tail, x_tail, h_carry):
        @pl.when(pl.program_id(0) == 0)
        def _():
            _load_weights(gath_ref, grp, ["in"], [w_in], sems)
            _zero(z_tail)
            _zero(x_tail)
            _zero(h_carry)

        def project_and_mix():
            xf = x_ref[...]
            rstd = lax.rsqrt(jnp.mean(xf * xf, axis=-1, keepdims=True) + RMS_EPS)
            h1 = (xf * rstd * g_ref[...]).astype(BF16)
            h1_ref[...] = h1
            for k in range(N_PROJ):
                p_ref[:, k * d:(k + 1) * d] = _dot_nt(h1, w_in[k * d:(k + 1) * d, :]).astype(BF16)
            row = lax.broadcasted_iota(jnp.int32, (SUBLANE, cw), 0)
            for j in range(nb):
                cs = slice(j * cw, (j + 1) * cw)
                ba_ref, ca_ref, xa_ref, xb_ref, gb_ref = (p_ref.at[:, k * d:(k + 1) * d] for k in range(5))
                z = ca_ref[:, cs].astype(F32) * xa_ref[:, cs].astype(F32)
                _causal_fill(zbuf, z, z_tail[:, cs], CONV_A_K - 1, row)
                z_tail[:, cs] = z[tm - (CONV_A_K - 1) * SUBLANE:, :]
                cz = caw_ref[0:1, cs] * zbuf[0:tm, :] + caw_ref[1:2, cs] * zbuf[SUBLANE:SUBLANE + tm, :] + caw_ref[2:3, cs] * z
                ya_ref[:, cs] = (ba_ref[:, cs].astype(F32) * cz).astype(BF16)
                xb = xb_ref[:, cs].astype(F32)
                _causal_fill(xbuf, xb, x_tail[:, cs], CONV_B_K - 1, row)
                x_tail[:, cs] = xb[tm - (CONV_B_K - 1) * SUBLANE:, :]
                u = (cbw_ref[0:1, cs] * xbuf[0:tm, :] + cbw_ref[1:2, cs] * xbuf[SUBLANE:SUBLANE + tm, :]
                     + cbw_ref[2:3, cs] * xbuf[2 * SUBLANE:2 * SUBLANE + tm, :] + cbw_ref[3:4, cs] * xb + vec_ref[0:1, cs])
                ub = u.astype(BF16)
                u = ub.astype(F32)
                _, gi, a, s, _ = _lru_gates(ub, bda_ref[j], bdx_ref[j], vec_ref[1:2, cs], vec_ref[2:3, cs], vec_ref[3:4, cs])
                abuf[...] = a
                bbuf[...] = s * (gi * u)
                h_end, a_prod = _chain_scan(abuf, bbuf, nk, reverse=False)
                a_inc, h_inc = _sublane_scan(a_prod, h_end, row, reverse=False)
                carry = h_carry[:, cs]
                ends = h_inc + a_inc * carry
                starts = jnp.where(row == 0, carry, pltpu.roll(ends, 1, 0))
                h_carry[:, cs] = jnp.broadcast_to(ends[SUBLANE - 1:SUBLANE, :], (SUBLANE, cw))
                h = (bbuf[...].reshape(nk, SUBLANE, cw) + abuf[...].reshape(nk, SUBLANE, cw) * starts[None]).reshape(tm, cw)
                gel, _ = _gelu_and_grad(gb_ref[:, cs].astype(F32))
                yb_ref[:, cs] = (h * gel).astype(BF16)
                u_ref[:, cs] = ub
                h_ref[:, cs] = h.astype(BF16)

        project_and_mix()

    small = pl.BlockSpec((SUBLANE, d), lambda i: (0, 0))
    bd = pl.BlockSpec((nb, cw, cw), lambda i: (0, 0, 0))
    row_spec = pl.BlockSpec((tm, d), lambda i: (i, 0))
    return _hosted_call(
        body, comms, name=name, grid=(t // tm,),
        out_shape=[jax.ShapeDtypeStruct((t, n_in), BF16)] + [jax.ShapeDtypeStruct((t, d), BF16)] * 5,
        in_specs=[row_spec, pl.BlockSpec((1, d), lambda i: (0, 0)), ANY, small, small, small, bd, bd],
        out_specs=[pl.BlockSpec((tm, n_in), lambda i: (i, 0))] + [row_spec] * 5,
        scratch_shapes=[pltpu.VMEM((n_in, d), BF16), pltpu.SemaphoreType.DMA((N_DEV,)),
                        pltpu.VMEM((tm + (CONV_A_K - 1) * SUBLANE, cw), F32), pltpu.VMEM((tm + (CONV_B_K - 1) * SUBLANE, cw), F32),
                        pltpu.VMEM((tm, cw), F32), pltpu.VMEM((tm, cw), F32),
                        pltpu.VMEM(((CONV_A_K - 1) * SUBLANE, d), F32), pltpu.VMEM(((CONV_B_K - 1) * SUBLANE, d), F32),
                        pltpu.VMEM((SUBLANE, d), F32)],
        args=(x, g_row, gath, caw, cbw, vec, bda, bdx))


def _merge_fwd(x, ya, yb, p, gbias, gath, grp, name, comms=()):
    t, d = x.shape
    tm = _row_tile(t, 512)

    def body(x_ref, ya_ref, yb_ref, ga_ref, gb_ref, gbias_ref, gath_ref, x1_ref, oa_ref, ob_ref, w_oa, w_ob, w_o, sems):
        @pl.when(pl.program_id(0) == 0)
        def _():
            _load_weights(gath_ref, grp, ["oa", "ob", "o"], [w_oa, w_ob, w_o], sems)

        oa = _dot_nn(ya_ref[...], w_oa[...]).astype(BF16)
        ob = _dot_nn(yb_ref[...], w_ob[...]).astype(BF16)
        oa_ref[...] = oa
        ob_ref[...] = ob
        sa = _sigmoid(ga_ref[...] + gbias_ref[0:1, :].astype(BF16))
        sb = _sigmoid(gb_ref[...] + gbias_ref[1:2, :].astype(BF16))
        x1_ref[...] = x_ref[...] + _dot_nn(sa * oa + sb * ob, w_o[...])

    row = pl.BlockSpec((tm, d), lambda i: (i, 0))
    return _hosted_call(
        body, comms, name=name, grid=(t // tm,),
        out_shape=[jax.ShapeDtypeStruct((t, d), F32), jax.ShapeDtypeStruct((t, d), BF16), jax.ShapeDtypeStruct((t, d), BF16)],
        in_specs=[row, row, row, pl.BlockSpec((tm, d), lambda i: (i, 5)), pl.BlockSpec((tm, d), lambda i: (i, 6)),
                  pl.BlockSpec((SUBLANE, d), lambda i: (0, 0)), ANY],
        out_specs=[row, row, row],
        scratch_shapes=[pltpu.VMEM((d, d), BF16)] * 3 + [pltpu.SemaphoreType.DMA((3 * N_DEV,))],
        args=(x, ya, yb, p, p, gbias, gath))


def _ffn_fwd(x1, g_row, gath, grp, name, comms=()):
    t, d = x1.shape
    ff = grp.rows["g"] * N_DEV
    tm = _row_tile(t, 512)
    fc = MXU_TILE
    assert ff % fc == 0

    def body(x_ref, g_ref, gath_ref, x2_ref, gg_ref, uu_ref, w_g, w_u, w_d, acc, sems):
        @pl.when(pl.program_id(0) == 0)
        def _():
            _load_weights(gath_ref, grp, ["g", "u", "d"], [w_g, w_u, w_d], sems)

        xf = x_ref[...]
        rstd = lax.rsqrt(jnp.mean(xf * xf, axis=-1, keepdims=True) + RMS_EPS)
        h = (xf * rstd * g_ref[...]).astype(BF16)
        acc[...] = xf
        for c in range(ff // fc):
            fs = slice(c * fc, (c + 1) * fc)
            gg = _dot_nt(h, w_g[fs, :]).astype(BF16)
            uu = _dot_nt(h, w_u[fs, :]).astype(BF16)
            gg_ref[:, fs] = gg
            uu_ref[:, fs] = uu
            acc[...] += _dot_nn(gg * _sigmoid(gg) * uu, w_d[fs, :])
        x2_ref[...] = acc[...]

    row = pl.BlockSpec((tm, d), lambda i: (i, 0))
    wide = pl.BlockSpec((tm, ff), lambda i: (i, 0))
    return _hosted_call(
        body, comms, name=name, grid=(t // tm,),
        out_shape=[jax.ShapeDtypeStruct((t, d), F32), jax.ShapeDtypeStruct((t, ff), BF16), jax.ShapeDtypeStruct((t, ff), BF16)],
        in_specs=[row, pl.BlockSpec((1, d), lambda i: (0, 0)), ANY],
        out_specs=[row, wide, wide],
        scratch_shapes=[pltpu.VMEM((ff, d), BF16)] * 3 + [pltpu.VMEM((tm, d), F32), pltpu.SemaphoreType.DMA((3 * N_DEV,))],
        args=(x1, g_row, gath))


def _loss_head(x, g_row, target, name):
    t, d = x.shape
    tm = _row_tile(t, 512)

    def body(x_ref, g_ref, tgt_ref, loss_ref, dx_ref, dg_ref):
        @pl.when(pl.program_id(0) == 0)
        def _():
            _zero(loss_ref)
            _zero(dg_ref)

        xf = x_ref[...]
        rstd = lax.rsqrt(jnp.mean(xf * xf, axis=-1, keepdims=True) + RMS_EPS)
        xh = xf * rstd
        g = g_ref[...]
        err = xh * g - tgt_ref[...]
        loss_ref[...] += 0.5 * jnp.sum(jnp.sum(err * err, axis=-1, keepdims=True), axis=0, keepdims=True) * (1.0 / d)
        dy = err * (1.0 / d)
        dg_ref[0:1, :] += jnp.sum(dy * xh, axis=0, keepdims=True)
        dxh = dy * g
        dx_ref[...] = rstd * (dxh - xh * jnp.mean(dxh * xh, axis=-1, keepdims=True))

    row = pl.BlockSpec((tm, d), lambda i: (i, 0))
    return pl.pallas_call(
        body, name=name, grid=(t // tm,),
        out_shape=[jax.ShapeDtypeStruct((SUBLANE, LANE), F32), jax.ShapeDtypeStruct((t, d), F32),
                   jax.ShapeDtypeStruct((SUBLANE, d), F32)],
        in_specs=[row, pl.BlockSpec((1, d), lambda i: (0, 0)), row],
        out_specs=[pl.BlockSpec((SUBLANE, LANE), lambda i: (0, 0)), row, pl.BlockSpec((SUBLANE, d), lambda i: (0, 0))],
        compiler_params=_params(),
    )(x, g_row, target)


def _ffn_bwd_act(dx2, gg, uu, gath, grp, name, comms=()):
    t, d = dx2.shape
    ff = grp.rows["g"] * N_DEV
    tm = _row_tile(t, 512)
    fc = MXU_TILE
    n_t = t // tm

    def body(dx2_ref, gg_ref, uu_ref, gath_ref, dgg_ref, duu_ref, dwd_ref, w_d, acc, sems):
        @pl.when(pl.program_id(0) == 0)
        def _():
            _load_weights(gath_ref, grp, ["d"], [w_d], sems)
            _zero(acc)

        dx2b = dx2_ref[...].astype(BF16)
        for c in range(ff // fc):
            fs = slice(c * fc, (c + 1) * fc)
            df = _dot_nt(dx2b, w_d[fs, :]).astype(BF16)
            g = gg_ref[:, fs]
            u = uu_ref[:, fs]
            sg = _sigmoid(g)
            silu = g * sg
            acc[fs, :] += _dot_tn(silu * u, dx2b)
            duu_ref[:, fs] = df * silu
            dgg_ref[:, fs] = df * u * (sg * (1.0 + g * (1.0 - sg)))

        @pl.when(pl.program_id(0) == n_t - 1)
        def _():
            w_d[...] = acc[...].astype(BF16)
            out = pltpu.make_async_copy(w_d, dwd_ref, sems.at[0])
            out.start()
            out.wait()

    row = pl.BlockSpec((tm, d), lambda i: (i, 0))
    wide = pl.BlockSpec((tm, ff), lambda i: (i, 0))
    sd = jax.ShapeDtypeStruct
    return _hosted_call(
        body, comms, name=name, grid=(n_t,),
        out_shape=[sd((t, ff), BF16), sd((t, ff), BF16), sd((ff, d), BF16)],
        in_specs=[row, wide, wide, ANY],
        out_specs=[wide, wide, ANY],
        scratch_shapes=[pltpu.VMEM((ff, d), BF16), pltpu.VMEM((ff, d), F32), pltpu.SemaphoreType.DMA((N_DEV,))],
        args=(dx2, gg, uu, gath))


def _ffn_bwd_in(dgg, duu, dx2, x1, g_row, gath, grp, name, comms=()):
    t, d = x1.shape
    ff = grp.rows["g"] * N_DEV
    tm = _row_tile(t, 512)

    def body(dgg_ref, duu_ref, dx2_ref, x_ref, g_ref, gath_ref, dx1_ref, dx1b_ref, h_ref, dg_ref, w_g, w_u, sems):
        @pl.when(pl.program_id(0) == 0)
        def _():
            _load_weights(gath_ref, grp, ["g", "u"], [w_g, w_u], sems)
            _zero(dg_ref)

        dh = _dot_nn(dgg_ref[...], w_g[...]) + _dot_nn(duu_ref[...], w_u[...])
        xf = x_ref[...]
        rstd = lax.rsqrt(jnp.mean(xf * xf, axis=-1, keepdims=True) + RMS_EPS)
        xh = xf * rstd
        g = g_ref[...]
        h_ref[...] = (xh * g).astype(BF16)
        dg_ref[0:1, :] += jnp.sum(dh * xh, axis=0, keepdims=True)
        dxh = dh * g
        dx1 = dx2_ref[...] + rstd * (dxh - xh * jnp.mean(dxh * xh, axis=-1, keepdims=True))
        dx1_ref[...] = dx1
        dx1b_ref[...] = dx1.astype(BF16)

    row = pl.BlockSpec((tm, d), lambda i: (i, 0))
    wide = pl.BlockSpec((tm, ff), lambda i: (i, 0))
    sd = jax.ShapeDtypeStruct
    return _hosted_call(
        body, comms, name=name, grid=(t // tm,),
        out_shape=[sd((t, d), F32), sd((t, d), BF16), sd((t, d), BF16), sd((SUBLANE, d), F32)],
        in_specs=[wide, wide, row, row, pl.BlockSpec((1, d), lambda i: (0, 0)), ANY],
        out_specs=[row, row, row, pl.BlockSpec((SUBLANE, d), lambda i: (0, 0))],
        scratch_shapes=[pltpu.VMEM((ff, d), BF16)] * 2 + [pltpu.SemaphoreType.DMA((2 * N_DEV,))],
        args=(dgg, duu, dx2, x1, g_row, gath))


def _merge_bwd(dx1b, oa, ob, ya, yb, p, gbias, gath, grp, name, comms=()):
    t, d = oa.shape
    tm = _row_tile(t, 512)
    n_t = t // tm

    def body(dx_ref, oa_ref, ob_ref, ya_ref, yb_ref, ga_ref, gb_ref, gbias_ref, gath_ref,
             dya_ref, dyb_ref, dp_ref, dgb_ref, dwoa_ref, dwob_ref, dwo_ref,
             w_oa, w_ob, w_o, acc_oa, acc_ob, acc_o, stage, sems, out_sems):
        @pl.when(pl.program_id(0) == 0)
        def _():
            _load_weights(gath_ref, grp, ["oa", "ob", "o"], [w_oa, w_ob, w_o], sems)
            for ref in (dgb_ref, acc_oa, acc_ob, acc_o):
                _zero(ref)

        dxb = dx_ref[...]
        dm = _dot_nt(dxb, w_o[...]).astype(BF16)
        oa = oa_ref[...]
        ob = ob_ref[...]
        sa = _sigmoid(ga_ref[...] + gbias_ref[0:1, :].astype(BF16))
        sb = _sigmoid(gb_ref[...] + gbias_ref[1:2, :].astype(BF16))
        acc_o[...] += _dot_tn(sa * oa + sb * ob, dxb)
        doa = dm * sa
        dob = dm * sb
        acc_oa[...] += _dot_tn(ya_ref[...], doa)
        acc_ob[...] += _dot_tn(yb_ref[...], dob)
        dga = dm * oa * sa * (1.0 - sa)
        dgb = dm * ob * sb * (1.0 - sb)
        step = pl.program_id(0)
        slot = step % 2

        def to_dp(k, at_step):
            return pltpu.make_async_copy(stage.at[k], dp_ref.at[pl.ds(at_step * tm, tm), pl.ds(5 * d, 2 * d)], out_sems.at[k])

        @pl.when(step >= 2)
        def _():
            to_dp(slot, step - 2).wait()

        stage[slot, :, 0:d] = dga
        stage[slot, :, d:2 * d] = dgb
        to_dp(slot, step).start()
        ones = jnp.ones((SUBLANE, tm), BF16)
        dgb_ref[0:1, :] += _dot_nn(ones, dga)[0:1, :]
        dgb_ref[1:2, :] += _dot_nn(ones, dgb)[0:1, :]
        dya_ref[...] = _dot_nt(doa, w_oa[...]).astype(BF16)
        dyb_ref[...] = _dot_nt(dob, w_ob[...]).astype(BF16)

        @pl.when(pl.program_id(0) == n_t - 1)
        def _():
            outs = []
            for n, (acc, stage, dst) in enumerate(((acc_oa, w_oa, dwoa_ref), (acc_ob, w_ob, dwob_ref), (acc_o, w_o, dwo_ref))):
                stage[...] = acc[...].astype(BF16)
                outs.append(pltpu.make_async_copy(stage, dst, sems.at[n]))
                outs[-1].start()
            for cp in outs:
                cp.wait()
            for back in range(min(2, n_t)):
                to_dp((n_t - 1 - back) % 2, n_t - 1 - back).wait()

    row = pl.BlockSpec((tm, d), lambda i: (i, 0))
    sd = jax.ShapeDtypeStruct
    return _hosted_call(
        body, comms, name=name, grid=(n_t,),
        out_shape=[sd((t, d), BF16), sd((t, d), BF16), sd((t, N_PROJ * d), BF16), sd((SUBLANE, d), F32),
                   sd((d, d), BF16), sd((d, d), BF16), sd((d, d), BF16)],
        in_specs=[row, row, row, row, row, pl.BlockSpec((tm, d), lambda i: (i, 5)), pl.BlockSpec((tm, d), lambda i: (i, 6)),
                  pl.BlockSpec((SUBLANE, d), lambda i: (0, 0)), ANY],
        out_specs=[row, row, ANY, pl.BlockSpec((SUBLANE, d), lambda i: (0, 0)), ANY, ANY, ANY],
        scratch_shapes=[pltpu.VMEM((d, d), BF16)] * 3 + [pltpu.VMEM((d, d), F32)] * 3
        + [pltpu.VMEM((2, tm, 2 * d), BF16), pltpu.SemaphoreType.DMA((3 * N_DEV,)), pltpu.SemaphoreType.DMA((2,))],
        args=(dx1b, oa, ob, ya, yb, p, p, gbias, gath))


DV_CONV_B_B, DV_BA, DV_BX, DV_SP, DV_CONV_A, DV_CONV_B = 0, 1, 2, 3, 4, 7
DV_ROWS = 16


def _mixer_bwd(dya, dyb, dp_gates, p, u_s, h_s, caw, cbw, vec, bda, bdx, name, comms=()):
    t, d = dya.shape
    tm = _time_tile(t)
    n_t = t // tm
    nk = tm // SUBLANE
    cw = min(MXU_TILE, d)
    nb = d // cw
    halo = 4 * SUBLANE
    ka, kb = CONV_A_K - 1, CONV_B_K - 1

    def body(dya_ref, dyb_ref, _, ba_ref, ca_ref, xa_ref, xb_ref, gb_ref, cah_ref, xah_ref, xbh_ref,
             u_ref, h_ref, hh_ref, caw_ref, cbw_ref, vec_ref, bda_ref, bdx_ref,
             dp_ref, dv_ref, dwa_ref, dwx_ref,
             zbuf, xbuf, hbuf, dczbuf, dubuf, a2buf, a1buf, lbuf, dcz_head, du_head, a_head, lam_head):
        i = pl.program_id(0)

        @pl.when(i == 0)
        def _():
            for ref in (dv_ref, dwa_ref, dwx_ref, dcz_head, du_head, a_head, lam_head):
                _zero(ref)

        has_prev = jnp.where(i < n_t - 1, 1.0, 0.0).astype(F32)
        row = lax.broadcasted_iota(jnp.int32, (SUBLANE, cw), 0)

        def colsum(v):
            return jnp.sum(v, axis=0, keepdims=True)

        for j in range(nb):
            cs = slice(j * cw, (j + 1) * cw)
            ca = ca_ref[:, cs].astype(F32)
            xa = xa_ref[:, cs].astype(F32)
            z = ca * xa
            z_before = cah_ref[:, cs].astype(F32) * xah_ref[:, cs].astype(F32) * has_prev
            _causal_fill(zbuf, z, z_before[halo - ka * SUBLANE:, :], ka, row)
            z2 = zbuf[0:tm, :]
            z1 = zbuf[SUBLANE:SUBLANE + tm, :]
            w0, w1, w2 = caw_ref[0:1, cs], caw_ref[1:2, cs], caw_ref[2:3, cs]
            cz = w0 * z2 + w1 * z1 + w2 * z
            dya = dya_ref[:, cs].astype(F32)
            dp_ref[:, 0 * d + j * cw:0 * d + (j + 1) * cw] = (dya * cz).astype(BF16)
            dcz = dya * ba_ref[:, cs].astype(F32)
            _anticausal_fill(dczbuf, dcz, dcz_head[:, cs], ka, row)
            dcz_head[:, cs] = dcz[0:ka * SUBLANE, :]
            dz = w2 * dcz + w1 * dczbuf[SUBLANE:SUBLANE + tm, :] + w0 * dczbuf[2 * SUBLANE:2 * SUBLANE + tm, :]
            dv_ref[DV_CONV_A + 0:DV_CONV_A + 1, cs] += colsum(dcz * z2)
            dv_ref[DV_CONV_A + 1:DV_CONV_A + 2, cs] += colsum(dcz * z1)
            dv_ref[DV_CONV_A + 2:DV_CONV_A + 3, cs] += colsum(dcz * z)
            dp_ref[:, 1 * d + j * cw:1 * d + (j + 1) * cw] = (dz * xa).astype(BF16)
            dp_ref[:, 2 * d + j * cw:2 * d + (j + 1) * cw] = (dz * ca).astype(BF16)
            h = h_ref[:, cs].astype(F32)
            h_before = hh_ref[:, cs].astype(F32) * has_prev
            _causal_fill(hbuf, h, h_before[halo - SUBLANE:, :], 1, row)
            h_prev = hbuf[0:tm, :]
            dyb = dyb_ref[:, cs].astype(F32)
            gel, dgel = _gelu_and_grad(gb_ref[:, cs].astype(F32))
            dp_ref[:, 4 * d + j * cw:4 * d + (j + 1) * cw] = (dyb * h * dgel).astype(BF16)
            ub = u_ref[:, cs]
            u = ub.astype(F32)
            sp = vec_ref[3:4, cs]
            r, gi, a, s, inv_s = _lru_gates(ub, bda_ref[j], bdx_ref[j], vec_ref[1:2, cs], vec_ref[2:3, cs], sp)
            _anticausal_fill(a2buf, a, a_head[:, cs], 1, row)
            a_head[:, cs] = a[0:SUBLANE, :]
            a1buf[...] = a2buf[SUBLANE:SUBLANE + tm, :]
            lbuf[...] = dyb * gel
            l_end, a_prod = _chain_scan(a1buf, lbuf, nk, reverse=True)
            a_inc, l_inc = _sublane_scan(a_prod, l_end, row, reverse=True)
            carry = lam_head[:, cs]
            ends = l_inc + a_inc * carry
            starts = jnp.where(row == SUBLANE - 1, carry, pltpu.roll(ends, SUBLANE - 1, 0))
            lam_head[:, cs] = jnp.broadcast_to(ends[0:1, :], (SUBLANE, cw))
            lam = (lbuf[...].reshape(nk, SUBLANE, cw) + a1buf[...].reshape(nk, SUBLANE, cw) * starts[None]).reshape(tm, cw)
            da = lam * h_prev
            iu = gi * u
            ds = lam * iu
            di = lam * s * u
            du = lam * s * gi
            dlog_a = da * a - ds * (a * a) * inv_s
            dv_ref[DV_SP:DV_SP + 1, cs] += colsum(dlog_a * r) * (-LRU_C)
            dpr = dlog_a * ((-LRU_C) * sp) * r * (1.0 - r)
            dpi = di * gi * (1.0 - gi)
            dv_ref[DV_BA:DV_BA + 1, cs] += colsum(dpr)
            dv_ref[DV_BX:DV_BX + 1, cs] += colsum(dpi)
            dprb = dpr.astype(BF16)
            dpib = dpi.astype(BF16)
            du = du + _dot_nt(dprb, bda_ref[j]) + _dot_nt(dpib, bdx_ref[j])
            dwa_ref[j] += _dot_tn(ub, dprb)
            dwx_ref[j] += _dot_tn(ub, dpib)
            xb = xb_ref[:, cs].astype(F32)
            x_before = xbh_ref[:, cs].astype(F32) * has_prev
            _causal_fill(xbuf, xb, x_before[halo - kb * SUBLANE:, :], kb, row)
            _anticausal_fill(dubuf, du, du_head[:, cs], kb, row)
            du_head[:, cs] = du[0:kb * SUBLANE, :]
            v0, v1, v2, v3 = cbw_ref[0:1, cs], cbw_ref[1:2, cs], cbw_ref[2:3, cs], cbw_ref[3:4, cs]
            dxb = (v3 * du + v2 * dubuf[SUBLANE:SUBLANE + tm, :] + v1 * dubuf[2 * SUBLANE:2 * SUBLANE + tm, :]
                   + v0 * dubuf[3 * SUBLANE:3 * SUBLANE + tm, :])
            dp_ref[:, 3 * d + j * cw:3 * d + (j + 1) * cw] = dxb.astype(BF16)
            dv_ref[DV_CONV_B_B:DV_CONV_B_B + 1, cs] += colsum(du)
            dv_ref[DV_CONV_B + 0:DV_CONV_B + 1, cs] += colsum(du * xbuf[0:tm, :])
            dv_ref[DV_CONV_B + 1:DV_CONV_B + 2, cs] += colsum(du * xbuf[SUBLANE:SUBLANE + tm, :])
            dv_ref[DV_CONV_B + 2:DV_CONV_B + 3, cs] += colsum(du * xbuf[2 * SUBLANE:2 * SUBLANE + tm, :])
            dv_ref[DV_CONV_B + 3:DV_CONV_B + 4, cs] += colsum(du * xb)

    rt = lambda i: n_t - 1 - i
    row_spec = pl.BlockSpec((tm, d), lambda i: (rt(i), 0))
    slab = lambda s: pl.BlockSpec((tm, d), lambda i, s=s: (rt(i), s))
    before = lambda s: pl.BlockSpec((halo, d), lambda i, s=s: (jnp.maximum(rt(i) * (tm // halo) - 1, 0), s))
    small = pl.BlockSpec((SUBLANE, d), lambda i: (0, 0))
    bd = pl.BlockSpec((nb, cw, cw), lambda i: (0, 0, 0))
    sd = jax.ShapeDtypeStruct
    wbuf = lambda n: pltpu.VMEM((tm + n * SUBLANE, cw), F32)
    head = lambda n: pltpu.VMEM((n * SUBLANE, d), F32)
    return _hosted_call(
        body, comms, name=name, grid=(n_t,),
        out_shape=[sd((t, N_PROJ * d), BF16), sd((DV_ROWS, d), F32), sd((nb, cw, cw), F32), sd((nb, cw, cw), F32)],
        in_specs=[row_spec, row_spec, ANY,
                  slab(0), slab(1), slab(2), slab(3), slab(4), before(1), before(2), before(3),
                  row_spec, row_spec, before(0), small, small, small, bd, bd],
        out_specs=[pl.BlockSpec((tm, 5 * d), lambda i: (rt(i), 0)), pl.BlockSpec((DV_ROWS, d), lambda i: (0, 0)), bd, bd],
        aliases={2: 0},
        scratch_shapes=[wbuf(ka), wbuf(kb), wbuf(1), wbuf(ka), wbuf(kb), wbuf(1),
                        pltpu.VMEM((tm, cw), F32), pltpu.VMEM((tm, cw), F32), head(ka), head(kb), head(1), head(1)],
        args=(dya, dyb, dp_gates, p, p, p, p, p, p, p, p, u_s, h_s, h_s, caw, cbw, vec, bda, bdx))


def _in_proj_bwd(dp, x, dx1, g_row, gath, grp, name, comms=()):
    t, d = x.shape
    tm = _row_tile(t, 512)
    n_in = N_PROJ * d

    def body(dp_ref, x_ref, dx1_ref, g_ref, gath_ref, dx_ref, dg_ref, w_in, sems):
        @pl.when(pl.program_id(0) == 0)
        def _():
            _load_weights(gath_ref, grp, ["in"], [w_in], sems)
            _zero(dg_ref)

        dh = _dot_nn(dp_ref[:, 0:d], w_in[0:d, :])
        for k in range(1, N_PROJ):
            dh = dh + _dot_nn(dp_ref[:, k * d:(k + 1) * d], w_in[k * d:(k + 1) * d, :])
        xf = x_ref[...]
        rstd = lax.rsqrt(jnp.mean(xf * xf, axis=-1, keepdims=True) + RMS_EPS)
        xh = xf * rstd
        g = g_ref[...]
        dg_ref[0:1, :] += jnp.sum(dh * xh, axis=0, keepdims=True)
        dxh = dh * g
        dx_ref[...] = dx1_ref[...] + rstd * (dxh - xh * jnp.mean(dxh * xh, axis=-1, keepdims=True))

    row = pl.BlockSpec((tm, d), lambda i: (i, 0))
    sd = jax.ShapeDtypeStruct
    return _hosted_call(
        body, comms, name=name, grid=(t // tm,),
        out_shape=[sd((t, d), F32), sd((SUBLANE, d), F32)],
        in_specs=[pl.BlockSpec((tm, n_in), lambda i: (i, 0)), row, row, pl.BlockSpec((1, d), lambda i: (0, 0)), ANY],
        out_specs=[row, pl.BlockSpec((SUBLANE, d), lambda i: (0, 0))],
        scratch_shapes=[pltpu.VMEM((n_in, d), BF16), pltpu.SemaphoreType.DMA((N_DEV,))],
        args=(dp, x, dx1, g_row, gath))


def _weight_grad(a, b, name):
    t, m = a.shape
    n = b.shape[1]
    bt = _row_tile(t, 1024)
    bm = m
    for div in (1, 2, 4, 8):
        if m % div == 0 and (m // div) % LANE == 0 and (m // div) * n * 4 <= (12 << 20):
            bm = m // div
            break
    n_t = t // bt

    def body(a_ref, b_ref, o_ref, acc):
        k = pl.program_id(1)

        @pl.when(k == 0)
        def _():
            _zero(acc)

        acc[...] += _dot_tn(a_ref[...], b_ref[...])

        @pl.when(k == n_t - 1)
        def _():
            o_ref[...] = acc[...].astype(BF16)

    return pl.pallas_call(
        body, name=name, grid=(m // bm, n_t),
        out_shape=jax.ShapeDtypeStruct((m, n), BF16),
        in_specs=[pl.BlockSpec((bt, bm), lambda i, k: (k, i)), pl.BlockSpec((bt, n), lambda i, k: (k, 0))],
        out_specs=pl.BlockSpec((bm, n), lambda i, k: (i, 0)),
        scratch_shapes=[pltpu.VMEM((bm, n), F32)],
        compiler_params=_params(2),
    )(a, b)


def _adamw(w, g, m, v, name):
    r, c = w.shape
    tr = _fit_rows(r, c * 4)
    c1 = 1.0 - ADAM_B1 ** ADAM_STEP
    c2 = 1.0 - ADAM_B2 ** ADAM_STEP

    def body(w_ref, g_ref, m_ref, v_ref, d_ref, nm_ref, nv_ref):
        g32 = g_ref[...]
        nm = ADAM_B1 * m_ref[...] + (1.0 - ADAM_B1) * g32
        nv = ADAM_B2 * v_ref[...] + (1.0 - ADAM_B2) * (g32 * g32)
        nm_ref[...] = nm
        nv_ref[...] = nv
        d_ref[...] = -ADAM_LR * ((nm / c1) / (jnp.sqrt(nv / c2) + ADAM_EPS) + ADAM_WD * w_ref[...])

    spec = pl.BlockSpec((tr, c), lambda i: (i, 0))
    return pl.pallas_call(
        body, name=name, grid=(r // tr,),
        out_shape=[jax.ShapeDtypeStruct((r, c), F32)] * 3,
        in_specs=[spec] * 4, out_specs=[spec] * 3,
        compiler_params=_params(),
    )(w, g, m, v)


def _pad_rows(a, mult=SUBLANE):
    pad = (-a.shape[0]) % mult
    return a if pad == 0 else jnp.concatenate([a, jnp.zeros((pad,) + a.shape[1:], a.dtype)], axis=0)


REPLICATED = ("ln1_g", "conv_b_b", "lru_wa", "lru_ba", "lru_wx", "lru_bx", "lru_lambda", "ln2_g", "final_g")
SMALL_SHARDED = ("conv_a_w", "conv_b_w", "gate_bias")
MATRICES = ("w_in", "w_out_a", "w_out_b", "w_o", "w_ffn_gate", "w_ffn_up", "w_ffn_down")
ORDER = ("ln1_g", "w_in", "conv_a_w", "conv_b_w", "conv_b_b", "lru_wa", "lru_ba", "lru_wx", "lru_bx", "lru_lambda",
         "w_out_a", "w_out_b", "gate_bias", "w_o", "ln2_g", "w_ffn_gate", "w_ffn_up", "w_ffn_down", "final_g")


def kernel(x, ln1_g, w_in, conv_a_w, conv_b_w, conv_b_b, lru_wa, lru_ba, lru_wx, lru_bx, lru_lambda, w_out_a, w_out_b, gate_bias, w_o, ln2_g, w_ffn_gate, w_ffn_up, w_ffn_down, final_g, loss_target, m_ln1_g, m_w_in, m_conv_a_w, m_conv_b_w, m_conv_b_b, m_lru_wa, m_lru_ba, m_lru_wx, m_lru_bx, m_lru_lambda, m_w_out_a, m_w_out_b, m_gate_bias, m_w_o, m_ln2_g, m_w_ffn_gate, m_w_ffn_up, m_w_ffn_down, m_final_g, v_ln1_g, v_w_in, v_conv_a_w, v_conv_b_w, v_conv_b_b, v_lru_wa, v_lru_ba, v_lru_wx, v_lru_bx, v_lru_lambda, v_w_out_a, v_w_out_b, v_gate_bias, v_w_o, v_ln2_g, v_w_ffn_gate, v_w_ffn_up, v_w_ffn_down, v_final_g):
    w = dict(ln1_g=ln1_g, w_in=w_in, conv_a_w=conv_a_w, conv_b_w=conv_b_w, conv_b_b=conv_b_b, lru_wa=lru_wa,
             lru_ba=lru_ba, lru_wx=lru_wx, lru_bx=lru_bx, lru_lambda=lru_lambda, w_out_a=w_out_a, w_out_b=w_out_b,
             gate_bias=gate_bias, w_o=w_o, ln2_g=ln2_g, w_ffn_gate=w_ffn_gate, w_ffn_up=w_ffn_up,
             w_ffn_down=w_ffn_down, final_g=final_g)
    mom = dict(ln1_g=m_ln1_g, w_in=m_w_in, conv_a_w=m_conv_a_w, conv_b_w=m_conv_b_w, conv_b_b=m_conv_b_b,
               lru_wa=m_lru_wa, lru_ba=m_lru_ba, lru_wx=m_lru_wx, lru_bx=m_lru_bx, lru_lambda=m_lru_lambda,
               w_out_a=m_w_out_a, w_out_b=m_w_out_b, gate_bias=m_gate_bias, w_o=m_w_o, ln2_g=m_ln2_g,
               w_ffn_gate=m_w_ffn_gate, w_ffn_up=m_w_ffn_up, w_ffn_down=m_w_ffn_down, final_g=m_final_g)
    var = dict(ln1_g=v_ln1_g, w_in=v_w_in, conv_a_w=v_conv_a_w, conv_b_w=v_conv_b_w, conv_b_b=v_conv_b_b,
               lru_wa=v_lru_wa, lru_ba=v_lru_ba, lru_wx=v_lru_wx, lru_bx=v_lru_bx, lru_lambda=v_lru_lambda,
               w_out_a=v_w_out_a, w_out_b=v_w_out_b, gate_bias=v_gate_bias, w_o=v_w_o, ln2_g=v_ln2_g,
               w_ffn_gate=v_w_ffn_gate, w_ffn_up=v_w_ffn_up, w_ffn_down=v_w_ffn_down, final_g=v_final_g)

    _, t, d = x.shape
    n_layers = w_in.shape[0]
    ff = w_ffn_down.shape[1] * N_DEV
    dd = d // N_DEV
    hd = d // LRU_HEADS
    cw = min(MXU_TILE, d)
    nb = d // cw
    hpt = cw // hd
    grp = _groups(d, ff)
    me = 4 * lax.axis_index("x") + 2 * lax.axis_index("y") + lax.axis_index("c")
    tm_time = _time_tile(t)
    x0 = _to_tile_order(x[0], tm_time)
    target = _to_tile_order(loss_target[0], tm_time)

    packed = [{"in": jnp.swapaxes(w_in[l], 0, 1).astype(BF16),
               "rest": jnp.concatenate([w_out_a[l], w_out_b[l], w_o[l], jnp.swapaxes(w_ffn_gate[l], 0, 1),
                                        jnp.swapaxes(w_ffn_up[l], 0, 1), w_ffn_down[l]], axis=0).astype(BF16)}
              for l in range(n_layers)]
    n_small = CONV_A_K + CONV_B_K + 2
    small = _pad_rows(jnp.concatenate([conv_a_w, conv_b_w, gate_bias], axis=1).reshape(n_layers * n_small, dd))
    sp = jax.nn.softplus(-lru_lambda)
    vec = [_pad_rows(jnp.stack([conv_b_b[l], lru_ba[l], lru_bx[l], sp[l]])) for l in range(n_layers)]
    eye = jnp.eye(hpt, dtype=F32)

    def block_diag(wh):
        return jnp.einsum("jkab,kl->jkalb", wh.reshape(nb, hpt, hd, hd), eye).reshape(nb, cw, cw).astype(BF16)

    bda = [block_diag(lru_wa[l]) for l in range(n_layers)]
    bdx = [block_diag(lru_wx[l]) for l in range(n_layers)]

    gath = [dict() for _ in range(n_layers)]
    (gath[0]["in"],), (small_g,) = _comm_call([_Gather(packed[0]["in"]), _Gather(small)], "gather_in_0")
    small_full = jnp.swapaxes(small_g[:, :n_layers * n_small], 0, 1).reshape(n_layers, n_small, d)
    caw = [_pad_rows(small_full[k, 0:CONV_A_K]) for k in range(n_layers)]
    cbw = [_pad_rows(small_full[k, CONV_A_K:CONV_A_K + CONV_B_K]) for k in range(n_layers)]
    gbias = [_pad_rows(small_full[k, CONV_A_K + CONV_B_K:]) for k in range(n_layers)]
    saved = []
    xl = x0
    for l in range(n_layers):
        more = l + 1 < n_layers
        (p, h1b, ya, yb, *kept), got = _in_proj_mixer_fwd(
            xl, ln1_g[l][None], gath[l]["in"], grp["in"], caw[l], cbw[l], vec[l], bda[l], bdx[l], f"in_proj_mixer_fwd_{l}",
            [_Gather(packed[0]["rest"])] if l == 0 else [])
        if l == 0:
            ((gath[0]["rest"],),) = got
        (x1, oa, ob), got = _merge_fwd(xl, ya, yb, p, gbias[l], gath[l]["rest"], grp["rest"], f"merge_fwd_{l}",
                                       [_Gather(packed[l + 1]["in"])] if more else [])
        if more:
            ((gath[l + 1]["in"],),) = got
        (x2, gg, uu), got = _ffn_fwd(x1, ln2_g[l][None], gath[l]["rest"], grp["rest"], f"ffn_fwd_{l}",
                                     [_Gather(packed[l + 1]["rest"])] if more else [])
        if more:
            ((gath[l + 1]["rest"],),) = got
        saved.append(dict(x=xl, p=p, h1b=h1b, ya=ya, yb=yb, mixer=kept, x1=x1, oa=oa, ob=ob, gg=gg, uu=uu))
        xl = x2
    loss_tile, dx, dfinal = _loss_head(xl, final_g[None], target, "loss_head")
    loss = lax.psum(loss_tile[0, 0], ("x", "y", "c"))

    def heads(dwb):
        blocks = jnp.diagonal(dwb.reshape(nb, hpt, hd, hpt, hd), axis1=1, axis2=3)
        return jnp.moveaxis(blocks, 3, 1).reshape(hd, d)

    layer_names = [n for n in REPLICATED if n != "final_g"] + list(SMALL_SHARDED)

    def layer_block(k):
        return jnp.concatenate([small_grads[k][n] for n in layer_names], axis=0)

    recv = [dict() for _ in range(n_layers)]
    small_grads = [None] * n_layers
    early_all = None
    xg = {"d": _Group(("d",), (ff // N_DEV,)), "gu": _Group(("g", "u"), (ff // N_DEV,) * 2),
          "out": _Group(("oa", "ob", "o"), (dd,) * 3), "in": grp["in"]}
    far_in = None
    for l in reversed(range(n_layers)):
        s = saved[l]
        (dgg, duu, dw_d), got = _ffn_bwd_act(dx, s["gg"], s["uu"], gath[l]["rest"], grp["rest"], f"ffn_bwd_act_{l}",
                                             [far_in] if far_in else [])
        if far_in:
            recv[l + 1]["in"].append(got[0][0])
        (dx1, dx1b, h2b, dln2), got = _ffn_bwd_in(dgg, duu, dx, s["x1"], ln2_g[l][None], gath[l]["rest"], grp["rest"],
                                                  f"ffn_bwd_in_{l}", [_Exchange({"d": dw_d}, xg["d"])])
        recv[l]["d"] = [got[0][0]]
        dw_gu = {"g": _weight_grad(dgg, h2b, f"dw_ffn_gate_{l}"), "u": _weight_grad(duu, h2b, f"dw_ffn_up_{l}")}
        (dya, dyb, dp_gates, dgbias, dw_oa, dw_ob, dw_o), got = _merge_bwd(
            dx1b, s["oa"], s["ob"], s["ya"], s["yb"], s["p"], gbias[l], gath[l]["rest"], grp["rest"], f"merge_bwd_{l}",
            [_Exchange(dw_gu, xg["gu"])])
        recv[l]["gu"] = [got[0][0]]
        comms = [_Exchange({"oa": dw_oa, "ob": dw_ob, "o": dw_o}, xg["out"])]
        if l == 0:
            early = [layer_block(k) for k in range(1, n_layers)] + [_pad_rows(dfinal[0:1])]
            comms.append(_Gather(jnp.concatenate(early, axis=0)))
        (dp, dv, dwa, dwx), got = _mixer_bwd(dya, dyb, dp_gates, s["p"], *s["mixer"], caw[l], cbw[l], vec[l], bda[l], bdx[l],
                                             f"mixer_bwd_{l}", comms)
        recv[l]["out"] = [got[0][0]]
        if l == 0:
            early_all = got[1][0]
        small_grads[l] = {
            "conv_b_b": dv[DV_CONV_B_B:DV_CONV_B_B + 1], "lru_wa": heads(dwa),
            "lru_ba": dv[DV_BA:DV_BA + 1], "lru_wx": heads(dwx), "lru_bx": dv[DV_BX:DV_BX + 1],
            "lru_lambda": dv[DV_SP:DV_SP + 1] * (-jax.nn.sigmoid(-lru_lambda[l]))[None], "ln2_g": dln2[0:1],
            "conv_a_w": dv[DV_CONV_A:DV_CONV_A + CONV_A_K], "conv_b_w": dv[DV_CONV_B:DV_CONV_B + CONV_B_K],
            "gate_bias": dgbias[0:2],
        }
        dw_in = {"in": _weight_grad(dp, s["h1b"], f"dw_in_{l}")}
        if l > 0:
            near_in, far_in = _Exchange(dw_in, xg["in"], NEAR_PEERS), _Exchange(dw_in, xg["in"], FAR_PEERS, local=False)
        else:
            near_in, far_in = _Exchange(dw_in, xg["in"]), None
        (dx, dln1), got = _in_proj_bwd(dp, s["x"], dx1, ln1_g[l][None], gath[l]["in"], grp["in"], f"in_proj_bwd_{l}", [near_in])
        recv[l]["in"] = [got[0][0]]
        small_grads[l]["ln1_g"] = dln1[0:1]
    grad_x = _from_tile_order(dx, tm_time)[None]

    g = {}
    gsum = [{k: _sum_slots(recv[l][k], f"sum_{k}_{l}") for k in xg} for l in range(n_layers)]

    def part(key):
        k = next(name for name, group in xg.items() if key in group.keys)
        o, r = xg[k].off[key], xg[k].rows[key]
        return jnp.stack([gsum[l][k][o:o + r] for l in range(n_layers)])

    g = {"w_in": jnp.swapaxes(part("in"), 1, 2), "w_out_a": part("oa"), "w_out_b": part("ob"), "w_o": part("o"),
         "w_ffn_gate": jnp.swapaxes(part("g"), 1, 2), "w_ffn_up": jnp.swapaxes(part("u"), 1, 2), "w_ffn_down": part("d")}
    ((late_all,),) = _comm_call([_Gather(layer_block(0).astype(BF16))], "gather_small_grads_0")
    early_sum = _sum_slots([early_all], "sum_small_grads")
    block_rows = late_all.shape[1]
    per_layer = [_sum_slots([late_all], "sum_small_grads_0")]
    per_layer += [early_sum[(k - 1) * block_rows:k * block_rows] for k in range(1, n_layers)]
    g["final_g"] = early_sum[(n_layers - 1) * block_rows].reshape(w["final_g"].shape)
    o = 0
    for n in layer_names:
        rows = small_grads[0][n].shape[0]
        stacked = jnp.concatenate([per_layer[k][o:o + rows] for k in range(n_layers)], axis=0)
        if n in SMALL_SHARDED:
            g[n] = lax.dynamic_slice_in_dim(stacked, me * dd, dd, axis=1).reshape(n_layers, rows, dd)
        else:
            g[n] = stacked.reshape(w[n].shape)
        o += rows

    delta, new_m, new_v = {}, {}, {}
    gate_maps = ("lru_wa", "lru_wx")
    for n in MATRICES + gate_maps:
        shape = w[n].shape
        flat = lambda a: a.reshape(-1, d if n in gate_maps else shape[-1])
        dl, nm, nv = _adamw(flat(w[n]), flat(g[n]), flat(mom[n]), flat(var[n]), f"adamw_{n}")
        delta[n], new_m[n], new_v[n] = dl.reshape(shape), nm.reshape(shape), nv.reshape(shape)
    vectors = tuple(n for n in REPLICATED if n not in gate_maps)
    for group, width, name in ((vectors, d, "adamw_replicated"), (SMALL_SHARDED, dd, "adamw_vectors")):
        cat = lambda src: _pad_rows(jnp.concatenate([src[n].reshape(-1, width) for n in group], axis=0))
        dl, nm, nv = _adamw(cat(w), cat(g), cat(mom), cat(var), name)
        o = 0
        for n in group:
            rows = w[n].size // width
            delta[n], new_m[n], new_v[n] = (a[o:o + rows].reshape(w[n].shape) for a in (dl, nm, nv))
            o += rows

    return (loss, grad_x, *[g[n] for n in ORDER], *[delta[n] for n in ORDER], *[new_m[n] for n in ORDER],
            *[new_v[n] for n in ORDER])
```

```python
import math

import jax
import jax.numpy as jnp
from jax import lax
from jax.experimental import pallas as pl
from jax.experimental.pallas import tpu as pltpu

F32 = jnp.float32
BF16 = jnp.bfloat16

N_DEV = 8
N_PROJ = 7
LRU_HEADS = 16
LRU_C = 8.0
RMS_EPS = 1e-6
CONV_A_K = 3
CONV_B_K = 4
GELU_C = math.sqrt(2.0 / math.pi)
GELU_A = 0.044715

ADAM_LR = 0.001
ADAM_B1 = 0.9
ADAM_B2 = 0.999
ADAM_EPS = 1e-08
ADAM_WD = 0.01
ADAM_STEP = 10

LANE = 128
SUBLANE = 8
MXU_TILE = 256
VMEM_LIMIT = 52 << 20
ALL_PEERS = tuple(range(1, N_DEV))
NEAR_PEERS = (1, 2, 3, 4, 5)
FAR_PEERS = (6, 7)
MESH = pl.DeviceIdType.MESH
ANY = pl.BlockSpec(memory_space=pl.ANY)


def _dot_nn(a, b):
    return lax.dot_general(a, b, (((1,), (0,)), ((), ())), preferred_element_type=F32)


def _dot_nt(a, b):
    return lax.dot_general(a, b, (((1,), (1,)), ((), ())), preferred_element_type=F32)


def _dot_tn(a, b):
    return lax.dot_general(a, b, (((0,), (0,)), ((), ())), preferred_element_type=F32)


def _sigmoid(x):
    return 1.0 / (1.0 + jnp.exp(-x))


def _gelu_and_grad(x):
    x2 = x * x
    t = jnp.tanh(GELU_C * x * (1.0 + GELU_A * x2))
    g = 0.5 * x * (1.0 + t)
    dg = 0.5 * (1.0 + t) + 0.5 * x * (1.0 - t * t) * GELU_C * (1.0 + 3.0 * GELU_A * x2)
    return g, dg


def _zero(ref):
    ref[...] = jnp.zeros(ref.shape, ref.dtype)


def _fit_rows(r, row_bytes, budget=1 << 20):
    fits = [t for t in range(16, r + 1, 16) if r % t == 0 and t * row_bytes <= budget]
    return max(fits) if fits else r


def _row_tile(t, want):
    tm = min(want, t // 2)
    assert t % tm == 0 and tm % SUBLANE == 0, (t, tm)
    return tm


def _params(n_grid=1, **kw):
    return pltpu.CompilerParams(dimension_semantics=("arbitrary",) * n_grid, vmem_limit_bytes=VMEM_LIMIT, **kw)


class _Group:
    def __init__(self, keys, rows):
        self.keys = keys
        self.rows = dict(zip(keys, rows))
        self.off, o = {}, 0
        for k in keys:
            self.off[k] = o
            o += self.rows[k]
        self.total = o


def _groups(d, ff):
    dd, ffs = d // N_DEV, ff // N_DEV
    return {"in": _Group(("in",), (N_PROJ * dd,)),
            "rest": _Group(("oa", "ob", "o", "g", "u", "d"), (dd, dd, dd, ffs, ffs, ffs))}


def _load_weights(g_ref, grp, keys, dsts, sems):
    copies = []
    for n, (k, dst) in enumerate(zip(keys, dsts)):
        rows, off = grp.rows[k], grp.off[k]
        copies += [pltpu.make_async_copy(g_ref.at[p, pl.ds(off, rows), :], dst.at[pl.ds(p * rows, rows), :],
                                         sems.at[n * N_DEV + p]) for p in range(N_DEV)]
    for c in copies:
        c.start()
    for c in copies:
        c.wait()


def _comm_sems():
    return [pltpu.SemaphoreType.DMA((N_DEV - 1,)), pltpu.SemaphoreType.DMA((N_DEV - 1,)), pltpu.SemaphoreType.DMA]


class _Gather:
    def __init__(self, x):
        self.inputs = [x]
        self.out_shape = [jax.ShapeDtypeStruct((N_DEV,) + x.shape, x.dtype)]
        self.scratch = _comm_sems()

    def _plan(self, ins, outs, scr):
        (x_ref,), (out_ref,), (send_sems, recv_sems, local_sem) = ins, outs, scr
        mx, my, mc = lax.axis_index("x"), lax.axis_index("y"), lax.axis_index("c")
        me, sibling = (mx, my, mc), (mx, my, 1 - mc)
        xn, yn, dg = (1 - mx, my), (mx, 1 - my), (1 - mx, 1 - my)
        core0 = mc == 0
        relayed = (jnp.where(core0, 1 - mx, mx), jnp.where(core0, my, 1 - my))
        relay_to = (jnp.where(core0, mx, 1 - mx), jnp.where(core0, 1 - my, my))

        def slot(px, py, pc):
            return out_ref.at[4 * px + 2 * py + pc]

        def copy(k, block, to, src=None):
            return pltpu.make_async_remote_copy(
                src_ref=slot(*block) if src is None else src, dst_ref=slot(*block),
                send_sem=send_sems.at[k], recv_sem=recv_sems.at[k], device_id=to, device_id_type=MESH)

        mine = lambda: pltpu.make_async_copy(x_ref, slot(*me), local_sem)
        own = [lambda: copy(0, me, sibling, src=x_ref), lambda: copy(1, me, (*xn, mc), src=x_ref),
               lambda: copy(2, me, (*yn, mc), src=x_ref)]
        relay = lambda: copy(3, (*relayed, mc), (*relay_to, mc))
        passes = [lambda: copy(4, (*xn, mc), sibling), lambda: copy(5, (*yn, mc), sibling), lambda: copy(6, (*dg, mc), sibling)]
        arrival = lambda k: copy(k, me, me)
        return mine, own, relay, passes, arrival

    def start(self, ins, outs, scr):
        mine, own, _, _, _ = self._plan(ins, outs, scr)
        mine().start()
        for cp in own:
            cp().start()

    def relay(self, ins, outs, scr):
        _, _, relay, passes, arrival = self._plan(ins, outs, scr)
        arrival(1).wait_recv()
        arrival(2).wait_recv()
        relay().start()
        passes[0]().start()
        passes[1]().start()

    def mid(self, ins, outs, scr):
        _, _, _, passes, arrival = self._plan(ins, outs, scr)
        arrival(3).wait_recv()
        passes[2]().start()

    def finish(self, ins, outs, scr):
        mine, _, _, _, arrival = self._plan(ins, outs, scr)
        for k in (0, 4, 5, 6):
            arrival(k).wait_recv()
        for k in range(N_DEV - 1):
            arrival(k).wait_send()
        mine().wait()


class _Exchange:
    def __init__(self, mats, grp, peers=ALL_PEERS, local=True):
        self.grp, self.peers, self.local = grp, tuple(peers), local
        self.inputs = [mats[k] for k in grp.keys]
        slots = len(self.peers) + (1 if local else 0)
        self.out_shape = [jax.ShapeDtypeStruct((slots, grp.total, self.inputs[0].shape[1]), BF16)]
        self.scratch = [pltpu.SemaphoreType.DMA((len(self.peers),)), pltpu.SemaphoreType.DMA((len(self.peers),)),
                        pltpu.SemaphoreType.DMA]

    def _pieces(self, g_refs, out_ref, q, dst_slot):
        out = []
        for g_ref, k in zip(g_refs, self.grp.keys):
            rows = self.grp.rows[k]
            out.append((g_ref.at[pl.ds(pl.multiple_of(q * rows, 16), rows), :],
                        out_ref.at[dst_slot, pl.ds(self.grp.off[k], rows), :]))
        return out

    def start(self, ins, outs, scr):
        (out_ref,), (send_sems, recv_sems, local_sem) = outs, scr
        mx, my, mc = lax.axis_index("x"), lax.axis_index("y"), lax.axis_index("c")
        if self.local:
            for s, t in self._pieces(ins, out_ref, 4 * mx + 2 * my + mc, 0):
                pltpu.make_async_copy(s, t, local_sem).start()
        for n, k in enumerate(self.peers):
            px, py, pc = mx ^ ((k >> 2) & 1), my ^ ((k >> 1) & 1), mc ^ (k & 1)
            for s, t in self._pieces(ins, out_ref, 4 * px + 2 * py + pc, n + (1 if self.local else 0)):
                pltpu.make_async_remote_copy(src_ref=s, dst_ref=t, send_sem=send_sems.at[n], recv_sem=recv_sems.at[n],
                                             device_id=(px, py, pc), device_id_type=MESH).start()

    def relay(self, ins, outs, scr):
        pass

    def mid(self, ins, outs, scr):
        pass

    def finish(self, ins, outs, scr):
        (out_ref,), (send_sems, recv_sems, local_sem) = outs, scr
        mx, my, mc = lax.axis_index("x"), lax.axis_index("y"), lax.axis_index("c")
        whole = out_ref.at[0]
        for n in range(len(self.peers)):
            done = pltpu.make_async_remote_copy(src_ref=whole, dst_ref=whole, send_sem=send_sems.at[n],
                                                recv_sem=recv_sems.at[n], device_id=(mx, my, mc), device_id_type=MESH)
            done.wait_send()
            done.wait_recv()
        if self.local:
            pltpu.make_async_copy(whole, whole, local_sem).wait()


def _split(refs, sizes):
    out, pos = [], 0
    for n in sizes:
        out.append(refs[pos:pos + n])
        pos += n
    return out


def _hosted_call(body, comms, *, name, grid, in_specs, out_specs, out_shape, scratch_shapes, args, aliases=None):
    n_steps = grid[0]
    nc = len(comms)
    sizes = ([len(in_specs)] + [len(c.inputs) for c in comms] + [len(out_specs)] + [len(c.out_shape) for c in comms]
             + [len(scratch_shapes)] + [len(c.scratch) for c in comms])

    def hosted(*refs):
        parts = _split(refs, sizes)
        ins, c_ins = parts[0], parts[1:1 + nc]
        outs, c_outs = parts[1 + nc], parts[2 + nc:2 + 2 * nc]
        scr, c_scr = parts[2 + 2 * nc], parts[3 + 2 * nc:]
        step = pl.program_id(0)
        if comms:
            @pl.when(step == 0)
            def _():
                for c, a, b, s in zip(comms, c_ins, c_outs, c_scr):
                    c.start(a, b, s)

            relay_step = (3 * n_steps) // 5

            @pl.when(step == relay_step)
            def _():
                for c, a, b, s in zip(comms, c_ins, c_outs, c_scr):
                    c.relay(a, b, s)

            @pl.when(step == max(n_steps - 2, relay_step))
            def _():
                for c, a, b, s in zip(comms, c_ins, c_outs, c_scr):
                    c.mid(a, b, s)

        body(*ins, *outs, *scr)
        if comms:
            @pl.when(step == n_steps - 1)
            def _():
                for c, a, b, s in zip(comms, c_ins, c_outs, c_scr):
                    c.finish(a, b, s)

    res = pl.pallas_call(
        hosted, name=name, grid=grid,
        out_shape=[*out_shape, *[o for c in comms for o in c.out_shape]],
        in_specs=[*in_specs, *[ANY for c in comms for _ in c.inputs]],
        out_specs=[*out_specs, *[ANY for c in comms for _ in c.out_shape]],
        scratch_shapes=[*scratch_shapes, *[s for c in comms for s in c.scratch]],
        input_output_aliases=aliases or {},
        compiler_params=_params(),
    )(*args, *[a for c in comms for a in c.inputs])
    main, rest = res[:len(out_specs)], res[len(out_specs):]
    return main, _split(rest, [len(c.out_shape) for c in comms])


def _comm_call(comms, name):
    sizes = [len(c.inputs) for c in comms] + [len(c.out_shape) for c in comms] + [len(c.scratch) for c in comms]
    nc = len(comms)

    def body(*refs):
        parts = _split(refs, sizes)
        triples = list(zip(comms, parts[:nc], parts[nc:2 * nc], parts[2 * nc:]))
        for phase in ("start", "relay", "mid", "finish"):
            for c, ins, outs, scr in triples:
                getattr(c, phase)(ins, outs, scr)

    res = pl.pallas_call(
        body, name=name, out_shape=[o for c in comms for o in c.out_shape],
        in_specs=[ANY for c in comms for _ in c.inputs], out_specs=[ANY for c in comms for _ in c.out_shape],
        scratch_shapes=[s for c in comms for s in c.scratch],
    )(*[a for c in comms for a in c.inputs])
    return _split(res, [len(c.out_shape) for c in comms])


def _sum_slots(xs, name):
    _, r, c = xs[0].shape
    tr = _fit_rows(r, c * 4)

    def body(*refs):
        acc = None
        for x_ref in refs[:-1]:
            for p in range(x_ref.shape[0]):
                v = x_ref[p].astype(F32)
                acc = v if acc is None else acc + v
        refs[-1][...] = acc

    return pl.pallas_call(
        body, name=name, grid=(r // tr,),
        out_shape=jax.ShapeDtypeStruct((r, c), F32),
        in_specs=[pl.BlockSpec((x.shape[0], tr, c), lambda i: (0, i, 0)) for x in xs],
        out_specs=pl.BlockSpec((tr, c), lambda i: (i, 0)),
        compiler_params=_params(),
    )(*xs)


def _time_tile(t):
    return _row_tile(t, 256)


def _to_tile_order(a, tm):
    t, c = a.shape
    return jnp.swapaxes(a.reshape(t // tm, SUBLANE, tm // SUBLANE, c), 1, 2).reshape(t, c)


def _from_tile_order(a, tm):
    t, c = a.shape
    return jnp.swapaxes(a.reshape(t // tm, tm // SUBLANE, SUBLANE, c), 1, 2).reshape(t, c)


def _causal_fill(buf, v, prev_tail, n, row):
    tm = v.shape[0]
    for q in range(n):
        cur = v[tm - SUBLANE * (n - q):tm - SUBLANE * (n - q - 1), :]
        prv = prev_tail[SUBLANE * q:SUBLANE * (q + 1), :]
        buf[SUBLANE * q:SUBLANE * (q + 1), :] = jnp.where(row == 0, pltpu.roll(prv, 1, 0), pltpu.roll(cur, 1, 0))
    buf[SUBLANE * n:, :] = v


def _anticausal_fill(buf, v, next_head, n, row):
    tm = v.shape[0]
    buf[0:tm, :] = v
    for q in range(n):
        cur = v[SUBLANE * q:SUBLANE * (q + 1), :]
        nxt = next_head[SUBLANE * q:SUBLANE * (q + 1), :]
        buf[tm + SUBLANE * q:tm + SUBLANE * (q + 1), :] = jnp.where(
            row == SUBLANE - 1, pltpu.roll(nxt, SUBLANE - 1, 0), pltpu.roll(cur, SUBLANE - 1, 0))


def _chain_scan(abuf, bbuf, nk, reverse):
    cw = abuf.shape[1]

    def step(n, carry):
        h, c = carry
        r0 = pl.multiple_of((nk - 1 - n if reverse else n) * SUBLANE, SUBLANE)
        ak = abuf[pl.ds(r0, SUBLANE), :]
        h = ak * h + bbuf[pl.ds(r0, SUBLANE), :]
        c = ak * c
        bbuf[pl.ds(r0, SUBLANE), :] = h
        abuf[pl.ds(r0, SUBLANE), :] = c
        return h, c

    return lax.fori_loop(0, nk, step, (jnp.zeros((SUBLANE, cw), F32), jnp.ones((SUBLANE, cw), F32)), unroll=True)


def _sublane_scan(a, b, row, reverse):
    for sh in (1, 2, 4):
        if reverse:
            m = row < SUBLANE - sh
            b = jnp.where(m, a * pltpu.roll(b, SUBLANE - sh, 0) + b, b)
            a = jnp.where(m, a * pltpu.roll(a, SUBLANE - sh, 0), a)
        else:
            m = row >= sh
            b = jnp.where(m, a * pltpu.roll(b, sh, 0) + b, b)
            a = jnp.where(m, a * pltpu.roll(a, sh, 0), a)
    return a, b


def _lru_gates(ub, bda, bdx, ba, bx, sp):
    r = _sigmoid(_dot_nn(ub, bda) + ba)
    i = _sigmoid(_dot_nn(ub, bdx) + bx)
    log_a = (-LRU_C) * r * sp
    a = jnp.exp(log_a)
    s2 = -jnp.tanh(log_a) * (1.0 + a * a)
    inv_s = lax.rsqrt(s2)
    s = jnp.where(s2 > 0.0, s2 * inv_s, 0.0)
    return r, i, a, s, inv_s


def _in_proj_mixer_fwd(x, g_row, gath, grp, caw, cbw, vec, bda, bdx, name, comms=()):
    t, d = x.shape
    n_in = N_PROJ * d
    tm = _time_tile(t)
    nk = tm // SUBLANE
    cw = min(MXU_TILE, d)
    nb = d // cw

    def body(x_ref, g_ref, gath_ref, caw_ref, cbw_ref, vec_ref, bda_ref, bdx_ref,
             p_ref, h1_ref, ya_ref, yb_ref, u_ref, h_ref, w_in, sems, zbuf, xbuf, abuf, bbuf, z_tail, x_tail, h_carry):
        @pl.when(pl.program_id(0) == 0)
        def _():
            _load_weights(gath_ref, grp, ["in"], [w_in], sems)
            _zero(z_tail)
            _zero(x_tail)
            _zero(h_carry)

        def project_and_mix():
            xf = x_ref[...]
            rstd = lax.rsqrt(jnp.mean(xf * xf, axis=-1, keepdims=True) + RMS_EPS)
            h1 = (xf * rstd * g_ref[...]).astype(BF16)
            h1_ref[...] = h1
            for k in range(N_PROJ):
                p_ref[:, k * d:(k + 1) * d] = _dot_nt(h1, w_in[k * d:(k + 1) * d, :]).astype(BF16)
            row = lax.broadcasted_iota(jnp.int32, (SUBLANE, cw), 0)
            for j in range(nb):
                cs = slice(j * cw, (j + 1) * cw)
                ba_ref, ca_ref, xa_ref, xb_ref, gb_ref = (p_ref.at[:, k * d:(k + 1) * d] for k in range(5))
                z = ca_ref[:, cs].astype(F32) * xa_ref[:, cs].astype(F32)
                _causal_fill(zbuf, z, z_tail[:, cs], CONV_A_K - 1, row)
                z_tail[:, cs] = z[tm - (CONV_A_K - 1) * SUBLANE:, :]
                cz = caw_ref[0:1, cs] * zbuf[0:tm, :] + caw_ref[1:2, cs] * zbuf[SUBLANE:SUBLANE + tm, :] + caw_ref[2:3, cs] * z
                ya_ref[:, cs] = (ba_ref[:, cs].astype(F32) * cz).astype(BF16)
                xb = xb_ref[:, cs].astype(F32)
                _causal_fill(xbuf, xb, x_tail[:, cs], CONV_B_K - 1, row)
                x_tail[:, cs] = xb[tm - (CONV_B_K - 1) * SUBLANE:, :]
                u = (cbw_ref[0:1, cs] * xbuf[0:tm, :] + cbw_ref[1:2, cs] * xbuf[SUBLANE:SUBLANE + tm, :]
                     + cbw_ref[2:3, cs] * xbuf[2 * SUBLANE:2 * SUBLANE + tm, :] + cbw_ref[3:4, cs] * xb + vec_ref[0:1, cs])
                ub = u.astype(BF16)
                u = ub.astype(F32)
                _, gi, a, s, _ = _lru_gates(ub, bda_ref[j], bdx_ref[j], vec_ref[1:2, cs], vec_ref[2:3, cs], vec_ref[3:4, cs])
                abuf[...] = a
                bbuf[...] = s * (gi * u)
                h_end, a_prod = _chain_scan(abuf, bbuf, nk, reverse=False)
                a_inc, h_inc = _sublane_scan(a_prod, h_end, row, reverse=False)
                carry = h_carry[:, cs]
                ends = h_inc + a_inc * carry
                starts = jnp.where(row == 0, carry, pltpu.roll(ends, 1, 0))
                h_carry[:, cs] = jnp.broadcast_to(ends[SUBLANE - 1:SUBLANE, :], (SUBLANE, cw))
                h = (bbuf[...].reshape(nk, SUBLANE, cw) + abuf[...].reshape(nk, SUBLANE, cw) * starts[None]).reshape(tm, cw)
                gel, _ = _gelu_and_grad(gb_ref[:, cs].astype(F32))
                yb_ref[:, cs] = (h * gel).astype(BF16)
                u_ref[:, cs] = ub
                h_ref[:, cs] = h.astype(BF16)

        project_and_mix()

    small = pl.BlockSpec((SUBLANE, d), lambda i: (0, 0))
    bd = pl.BlockSpec((nb, cw, cw), lambda i: (0, 0, 0))
    row_spec = pl.BlockSpec((tm, d), lambda i: (i, 0))
    return _hosted_call(
        body, comms, name=name, grid=(t // tm,),
        out_shape=[jax.ShapeDtypeStruct((t, n_in), BF16)] + [jax.ShapeDtypeStruct((t, d), BF16)] * 5,
        in_specs=[row_spec, pl.BlockSpec((1, d), lambda i: (0, 0)), ANY, small, small, small, bd, bd],
        out_specs=[pl.BlockSpec((tm, n_in), lambda i: (i, 0))] + [row_spec] * 5,
        scratch_shapes=[pltpu.VMEM((n_in, d), BF16), pltpu.SemaphoreType.DMA((N_DEV,)),
                        pltpu.VMEM((tm + (CONV_A_K - 1) * SUBLANE, cw), F32), pltpu.VMEM((tm + (CONV_B_K - 1) * SUBLANE, cw), F32),
                        pltpu.VMEM((tm, cw), F32), pltpu.VMEM((tm, cw), F32),
                        pltpu.VMEM(((CONV_A_K - 1) * SUBLANE, d), F32), pltpu.VMEM(((CONV_B_K - 1) * SUBLANE, d), F32),
                        pltpu.VMEM((SUBLANE, d), F32)],
        args=(x, g_row, gath, caw, cbw, vec, bda, bdx))


def _merge_fwd(x, ya, yb, p, gbias, gath, grp, name, comms=()):
    t, d = x.shape
    tm = _row_tile(t, 512)

    def body(x_ref, ya_ref, yb_ref, ga_ref, gb_ref, gbias_ref, gath_ref, x1_ref, oa_ref, ob_ref, w_oa, w_ob, w_o, sems):
        @pl.when(pl.program_id(0) == 0)
        def _():
            _load_weights(gath_ref, grp, ["oa", "ob", "o"], [w_oa, w_ob, w_o], sems)

        oa = _dot_nn(ya_ref[...], w_oa[...]).astype(BF16)
        ob = _dot_nn(yb_ref[...], w_ob[...]).astype(BF16)
        oa_ref[...] = oa
        ob_ref[...] = ob
        sa = _sigmoid(ga_ref[...] + gbias_ref[0:1, :].astype(BF16))
        sb = _sigmoid(gb_ref[...] + gbias_ref[1:2, :].astype(BF16))
        x1_ref[...] = x_ref[...] + _dot_nn(sa * oa + sb * ob, w_o[...])

    row = pl.BlockSpec((tm, d), lambda i: (i, 0))
    return _hosted_call(
        body, comms, name=name, grid=(t // tm,),
        out_shape=[jax.ShapeDtypeStruct((t, d), F32), jax.ShapeDtypeStruct((t, d), BF16), jax.ShapeDtypeStruct((t, d), BF16)],
        in_specs=[row, row, row, pl.BlockSpec((tm, d), lambda i: (i, 5)), pl.BlockSpec((tm, d), lambda i: (i, 6)),
                  pl.BlockSpec((SUBLANE, d), lambda i: (0, 0)), ANY],
        out_specs=[row, row, row],
        scratch_shapes=[pltpu.VMEM((d, d), BF16)] * 3 + [pltpu.SemaphoreType.DMA((3 * N_DEV,))],
        args=(x, ya, yb, p, p, gbias, gath))


def _ffn_fwd(x1, g_row, gath, grp, name, comms=()):
    t, d = x1.shape
    ff = grp.rows["g"] * N_DEV
    tm = _row_tile(t, 512)
    fc = MXU_TILE
    assert ff % fc == 0

    def body(x_ref, g_ref, gath_ref, x2_ref, gg_ref, uu_ref, w_g, w_u, w_d, acc, sems):
        @pl.when(pl.program_id(0) == 0)
        def _():
            _load_weights(gath_ref, grp, ["g", "u", "d"], [w_g, w_u, w_d], sems)

        xf = x_ref[...]
        rstd = lax.rsqrt(jnp.mean(xf * xf, axis=-1, keepdims=True) + RMS_EPS)
        h = (xf * rstd * g_ref[...]).astype(BF16)
        acc[...] = xf
        for c in range(ff // fc):
            fs = slice(c * fc, (c + 1) * fc)
            gg = _dot_nt(h, w_g[fs, :]).astype(BF16)
            uu = _dot_nt(h, w_u[fs, :]).astype(BF16)
            gg_ref[:, fs] = gg
            uu_ref[:, fs] = uu
            acc[...] += _dot_nn(gg * _sigmoid(gg) * uu, w_d[fs, :])
        x2_ref[...] = acc[...]

    row = pl.BlockSpec((tm, d), lambda i: (i, 0))
    wide = pl.BlockSpec((tm, ff), lambda i: (i, 0))
    return _hosted_call(
        body, comms, name=name, grid=(t // tm,),
        out_shape=[jax.ShapeDtypeStruct((t, d), F32), jax.ShapeDtypeStruct((t, ff), BF16), jax.ShapeDtypeStruct((t, ff), BF16)],
        in_specs=[row, pl.BlockSpec((1, d), lambda i: (0, 0)), ANY],
        out_specs=[row, wide, wide],
        scratch_shapes=[pltpu.VMEM((ff, d), BF16)] * 3 + [pltpu.VMEM((tm, d), F32), pltpu.SemaphoreType.DMA((3 * N_DEV,))],
        args=(x1, g_row, gath))


def _loss_head(x, g_row, target, name):
    t, d = x.shape
    tm = _row_tile(t, 512)

    def body(x_ref, g_ref, tgt_ref, loss_ref, dx_ref, dg_ref):
        @pl.when(pl.program_id(0) == 0)
        def _():
            _zero(loss_ref)
            _zero(dg_ref)

        xf = x_ref[...]
        rstd = lax.rsqrt(jnp.mean(xf * xf, axis=-1, keepdims=True) + RMS_EPS)
        xh = xf * rstd
        g = g_ref[...]
        err = xh * g - tgt_ref[...]
        loss_ref[...] += 0.5 * jnp.sum(jnp.sum(err * err, axis=-1, keepdims=True), axis=0, keepdims=True) * (1.0 / d)
        dy = err * (1.0 / d)
        dg_ref[0:1, :] += jnp.sum(dy * xh, axis=0, keepdims=True)
        dxh = dy * g
        dx_ref[...] = rstd * (dxh - xh * jnp.mean(dxh * xh, axis=-1, keepdims=True))

    row = pl.BlockSpec((tm, d), lambda i: (i, 0))
    return pl.pallas_call(
        body, name=name, grid=(t // tm,),
        out_shape=[jax.ShapeDtypeStruct((SUBLANE, LANE), F32), jax.ShapeDtypeStruct((t, d), F32),
                   jax.ShapeDtypeStruct((SUBLANE, d), F32)],
        in_specs=[row, pl.BlockSpec((1, d), lambda i: (0, 0)), row],
        out_specs=[pl.BlockSpec((SUBLANE, LANE), lambda i: (0, 0)), row, pl.BlockSpec((SUBLANE, d), lambda i: (0, 0))],
        compiler_params=_params(),
    )(x, g_row, target)


def _ffn_bwd_act(dx2, gg, uu, gath, grp, name, comms=()):
    t, d = dx2.shape
    ff = grp.rows["g"] * N_DEV
    tm = _row_tile(t, 512)
    fc = MXU_TILE
    n_t = t // tm

    def body(dx2_ref, gg_ref, uu_ref, gath_ref, dgg_ref, duu_ref, dwd_ref, w_d, acc, sems):
        @pl.when(pl.program_id(0) == 0)
        def _():
            _load_weights(gath_ref, grp, ["d"], [w_d], sems)
            _zero(acc)

        dx2b = dx2_ref[...].astype(BF16)
        for c in range(ff // fc):
            fs = slice(c * fc, (c + 1) * fc)
            df = _dot_nt(dx2b, w_d[fs, :]).astype(BF16)
            g = gg_ref[:, fs]
            u = uu_ref[:, fs]
            sg = _sigmoid(g)
            silu = g * sg
            acc[fs, :] += _dot_tn(silu * u, dx2b)
            duu_ref[:, fs] = df * silu
            dgg_ref[:, fs] = df * u * (sg * (1.0 + g * (1.0 - sg)))

        @pl.when(pl.program_id(0) == n_t - 1)
        def _():
            w_d[...] = acc[...].astype(BF16)
            out = pltpu.make_async_copy(w_d, dwd_ref, sems.at[0])
            out.start()
            out.wait()

    row = pl.BlockSpec((tm, d), lambda i: (i, 0))
    wide = pl.BlockSpec((tm, ff), lambda i: (i, 0))
    sd = jax.ShapeDtypeStruct
    return _hosted_call(
        body, comms, name=name, grid=(n_t,),
        out_shape=[sd((t, ff), BF16), sd((t, ff), BF16), sd((ff, d), BF16)],
        in_specs=[row, wide, wide, ANY],
        out_specs=[wide, wide, ANY],
        scratch_shapes=[pltpu.VMEM((ff, d), BF16), pltpu.VMEM((ff, d), F32), pltpu.SemaphoreType.DMA((N_DEV,))],
        args=(dx2, gg, uu, gath))


def _ffn_bwd_in(dgg, duu, dx2, x1, g_row, gath, grp, name, comms=()):
    t, d = x1.shape
    ff = grp.rows["g"] * N_DEV
    tm = _row_tile(t, 512)

    def body(dgg_ref, duu_ref, dx2_ref, x_ref, g_ref, gath_ref, dx1_ref, dx1b_ref, h_ref, dg_ref, w_g, w_u, sems):
        @pl.when(pl.program_id(0) == 0)
        def _():
            _load_weights(gath_ref, grp, ["g", "u"], [w_g, w_u], sems)
            _zero(dg_ref)

        dh = _dot_nn(dgg_ref[...], w_g[...]) + _dot_nn(duu_ref[...], w_u[...])
        xf = x_ref[...]
        rstd = lax.rsqrt(jnp.mean(xf * xf, axis=-1, keepdims=True) + RMS_EPS)
        xh = xf * rstd
        g = g_ref[...]
        h_ref[...] = (xh * g).astype(BF16)
        dg_ref[0:1, :] += jnp.sum(dh * xh, axis=0, keepdims=True)
        dxh = dh * g
        dx1 = dx2_ref[...] + rstd * (dxh - xh * jnp.mean(dxh * xh, axis=-1, keepdims=True))
        dx1_ref[...] = dx1
        dx1b_ref[...] = dx1.astype(BF16)

    row = pl.BlockSpec((tm, d), lambda i: (i, 0))
    wide = pl.BlockSpec((tm, ff), lambda i: (i, 0))
    sd = jax.ShapeDtypeStruct
    return _hosted_call(
        body, comms, name=name, grid=(t // tm,),
        out_shape=[sd((t, d), F32), sd((t, d), BF16), sd((t, d), BF16), sd((SUBLANE, d), F32)],
        in_specs=[wide, wide, row, row, pl.BlockSpec((1, d), lambda i: (0, 0)), ANY],
        out_specs=[row, row, row, pl.BlockSpec((SUBLANE, d), lambda i: (0, 0))],
        scratch_shapes=[pltpu.VMEM((ff, d), BF16)] * 2 + [pltpu.SemaphoreType.DMA((2 * N_DEV,))],
        args=(dgg, duu, dx2, x1, g_row, gath))


def _merge_bwd(dx1b, oa, ob, ya, yb, p, gbias, gath, grp, name, comms=()):
    t, d = oa.shape
    tm = _row_tile(t, 512)
    n_t = t // tm

    def body(dx_ref, oa_ref, ob_ref, ya_ref, yb_ref, ga_ref, gb_ref, gbias_ref, gath_ref,
             dya_ref, dyb_ref, dp_ref, dgb_ref, dwoa_ref, dwob_ref, dwo_ref,
             w_oa, w_ob, w_o, acc_oa, acc_ob, acc_o, stage, sems, out_sems):
        @pl.when(pl.program_id(0) == 0)
        def _():
            _load_weights(gath_ref, grp, ["oa", "ob", "o"], [w_oa, w_ob, w_o], sems)
            for ref in (dgb_ref, acc_oa, acc_ob, acc_o):
                _zero(ref)

        dxb = dx_ref[...]
        dm = _dot_nt(dxb, w_o[...]).astype(BF16)
        oa = oa_ref[...]
        ob = ob_ref[...]
        sa = _sigmoid(ga_ref[...] + gbias_ref[0:1, :].astype(BF16))
        sb = _sigmoid(gb_ref[...] + gbias_ref[1:2, :].astype(BF16))
        acc_o[...] += _dot_tn(sa * oa + sb * ob, dxb)
        doa = dm * sa
        dob = dm * sb
        acc_oa[...] += _dot_tn(ya_ref[...], doa)
        acc_ob[...] += _dot_tn(yb_ref[...], dob)
        dga = dm * oa * sa * (1.0 - sa)
        dgb = dm * ob * sb * (1.0 - sb)
        step = pl.program_id(0)
        slot = step % 2

        def to_dp(k, at_step):
            return pltpu.make_async_copy(stage.at[k], dp_ref.at[pl.ds(at_step * tm, tm), pl.ds(5 * d, 2 * d)], out_sems.at[k])

        @pl.when(step >= 2)
        def _():
            to_dp(slot, step - 2).wait()

        stage[slot, :, 0:d] = dga
        stage[slot, :, d:2 * d] = dgb
        to_dp(slot, step).start()
        ones = jnp.ones((SUBLANE, tm), BF16)
        dgb_ref[0:1, :] += _dot_nn(ones, dga)[0:1, :]
        dgb_ref[1:2, :] += _dot_nn(ones, dgb)[0:1, :]
        dya_ref[...] = _dot_nt(doa, w_oa[...]).astype(BF16)
        dyb_ref[...] = _dot_nt(dob, w_ob[...]).astype(BF16)

        @pl.when(pl.program_id(0) == n_t - 1)
        def _():
            outs = []
            for n, (acc, stage, dst) in enumerate(((acc_oa, w_oa, dwoa_ref), (acc_ob, w_ob, dwob_ref), (acc_o, w_o, dwo_ref))):
                stage[...] = acc[...].astype(BF16)
                outs.append(pltpu.make_async_copy(stage, dst, sems.at[n]))
                outs[-1].start()
            for cp in outs:
                cp.wait()
            for back in range(min(2, n_t)):
                to_dp((n_t - 1 - back) % 2, n_t - 1 - back).wait()

    row = pl.BlockSpec((tm, d), lambda i: (i, 0))
    sd = jax.ShapeDtypeStruct
    return _hosted_call(
        body, comms, name=name, grid=(n_t,),
        out_shape=[sd((t, d), BF16), sd((t, d), BF16), sd((t, N_PROJ * d), BF16), sd((SUBLANE, d), F32),
                   sd((d, d), BF16), sd((d, d), BF16), sd((d, d), BF16)],
        in_specs=[row, row, row, row, row, pl.BlockSpec((tm, d), lambda i: (i, 5)), pl.BlockSpec((tm, d), lambda i: (i, 6)),
                  pl.BlockSpec((SUBLANE, d), lambda i: (0, 0)), ANY],
        out_specs=[row, row, ANY, pl.BlockSpec((SUBLANE, d), lambda i: (0, 0)), ANY, ANY, ANY],
        scratch_shapes=[pltpu.VMEM((d, d), BF16)] * 3 + [pltpu.VMEM((d, d), F32)] * 3
        + [pltpu.VMEM((2, tm, 2 * d), BF16), pltpu.SemaphoreType.DMA((3 * N_DEV,)), pltpu.SemaphoreType.DMA((2,))],
        args=(dx1b, oa, ob, ya, yb, p, p, gbias, gath))


DV_CONV_B_B, DV_BA, DV_BX, DV_SP, DV_CONV_A, DV_CONV_B = 0, 1, 2, 3, 4, 7
DV_ROWS = 16


def _mixer_bwd(dya, dyb, dp_gates, p, u_s, h_s, caw, cbw, vec, bda, bdx, name, comms=()):
    t, d = dya.shape
    tm = _time_tile(t)
    n_t = t // tm
    nk = tm // SUBLANE
    cw = min(MXU_TILE, d)
    nb = d // cw
    halo = 4 * SUBLANE
    ka, kb = CONV_A_K - 1, CONV_B_K - 1

    def body(dya_ref, dyb_ref, _, ba_ref, ca_ref, xa_ref, xb_ref, gb_ref, cah_ref, xah_ref, xbh_ref,
             u_ref, h_ref, hh_ref, caw_ref, cbw_ref, vec_ref, bda_ref, bdx_ref,
             dp_ref, dv_ref, dwa_ref, dwx_ref,
             zbuf, xbuf, hbuf, dczbuf, dubuf, a2buf, a1buf, lbuf, dcz_head, du_head, a_head, lam_head):
        i = pl.program_id(0)

        @pl.when(i == 0)
        def _():
            for ref in (dv_ref, dwa_ref, dwx_ref, dcz_head, du_head, a_head, lam_head):
                _zero(ref)

        has_prev = jnp.where(i < n_t - 1, 1.0, 0.0).astype(F32)
        row = lax.broadcasted_iota(jnp.int32, (SUBLANE, cw), 0)

        def colsum(v):
            return jnp.sum(v, axis=0, keepdims=True)

        for j in range(nb):
            cs = slice(j * cw, (j + 1) * cw)
            ca = ca_ref[:, cs].astype(F32)
            xa = xa_ref[:, cs].astype(F32)
            z = ca * xa
            z_before = cah_ref[:, cs].astype(F32) * xah_ref[:, cs].astype(F32) * has_prev
            _causal_fill(zbuf, z, z_before[halo - ka * SUBLANE:, :], ka, row)
            z2 = zbuf[0:tm, :]
            z1 = zbuf[SUBLANE:SUBLANE + tm, :]
            w0, w1, w2 = caw_ref[0:1, cs], caw_ref[1:2, cs], caw_ref[2:3, cs]
            cz = w0 * z2 + w1 * z1 + w2 * z
            dya = dya_ref[:, cs].astype(F32)
            dp_ref[:, 0 * d + j * cw:0 * d + (j + 1) * cw] = (dya * cz).astype(BF16)
            dcz = dya * ba_ref[:, cs].astype(F32)
            _anticausal_fill(dczbuf, dcz, dcz_head[:, cs], ka, row)
            dcz_head[:, cs] = dcz[0:ka * SUBLANE, :]
            dz = w2 * dcz + w1 * dczbuf[SUBLANE:SUBLANE + tm, :] + w0 * dczbuf[2 * SUBLANE:2 * SUBLANE + tm, :]
            dv_ref[DV_CONV_A + 0:DV_CONV_A + 1, cs] += colsum(dcz * z2)
            dv_ref[DV_CONV_A + 1:DV_CONV_A + 2, cs] += colsum(dcz * z1)
            dv_ref[DV_CONV_A + 2:DV_CONV_A + 3, cs] += colsum(dcz * z)
            dp_ref[:, 1 * d + j * cw:1 * d + (j + 1) * cw] = (dz * xa).astype(BF16)
            dp_ref[:, 2 * d + j * cw:2 * d + (j + 1) * cw] = (dz * ca).astype(BF16)
            h = h_ref[:, cs].astype(F32)
            h_before = hh_ref[:, cs].astype(F32) * has_prev
            _causal_fill(hbuf, h, h_before[halo - SUBLANE:, :], 1, row)
            h_prev = hbuf[0:tm, :]
            dyb = dyb_ref[:, cs].astype(F32)
            gel, dgel = _gelu_and_grad(gb_ref[:, cs].astype(F32))
            dp_ref[:, 4 * d + j * cw:4 * d + (j + 1) * cw] = (dyb * h * dgel).astype(BF16)
            ub = u_ref[:, cs]
            u = ub.astype(F32)
            sp = vec_ref[3:4, cs]
            r, gi, a, s, inv_s = _lru_gates(ub, bda_ref[j], bdx_ref[j], vec_ref[1:2, cs], vec_ref[2:3, cs], sp)
            _anticausal_fill(a2buf, a, a_head[:, cs], 1, row)
            a_head[:, cs] = a[0:SUBLANE, :]
            a1buf[...] = a2buf[SUBLANE:SUBLANE + tm, :]
            lbuf[...] = dyb * gel
            l_end, a_prod = _chain_scan(a1buf, lbuf, nk, reverse=True)
            a_inc, l_inc = _sublane_scan(a_prod, l_end, row, reverse=True)
            carry = lam_head[:, cs]
            ends = l_inc + a_inc * carry
            starts = jnp.where(row == SUBLANE - 1, carry, pltpu.roll(ends, SUBLANE - 1, 0))
            lam_head[:, cs] = jnp.broadcast_to(ends[0:1, :], (SUBLANE, cw))
            lam = (lbuf[...].reshape(nk, SUBLANE, cw) + a1buf[...].reshape(nk, SUBLANE, cw) * starts[None]).reshape(tm, cw)
            da = lam * h_prev
            iu = gi * u
            ds = lam * iu
            di = lam * s * u
            du = lam * s * gi
            dlog_a = da * a - ds * (a * a) * inv_s
            dv_ref[DV_SP:DV_SP + 1, cs] += colsum(dlog_a * r) * (-LRU_C)
            dpr = dlog_a * ((-LRU_C) * sp) * r * (1.0 - r)
            dpi = di * gi * (1.0 - gi)
            dv_ref[DV_BA:DV_BA + 1, cs] += colsum(dpr)
            dv_ref[DV_BX:DV_BX + 1, cs] += colsum(dpi)
            dprb = dpr.astype(BF16)
            dpib = dpi.astype(BF16)
            du = du + _dot_nt(dprb, bda_ref[j]) + _dot_nt(dpib, bdx_ref[j])
            dwa_ref[j] += _dot_tn(ub, dprb)
            dwx_ref[j] += _dot_tn(ub, dpib)
            xb = xb_ref[:, cs].astype(F32)
            x_before = xbh_ref[:, cs].astype(F32) * has_prev
            _causal_fill(xbuf, xb, x_before[halo - kb * SUBLANE:, :], kb, row)
            _anticausal_fill(dubuf, du, du_head[:, cs], kb, row)
            du_head[:, cs] = du[0:kb * SUBLANE, :]
            v0, v1, v2, v3 = cbw_ref[0:1, cs], cbw_ref[1:2, cs], cbw_ref[2:3, cs], cbw_ref[3:4, cs]
            dxb = (v3 * du + v2 * dubuf[SUBLANE:SUBLANE + tm, :] + v1 * dubuf[2 * SUBLANE:2 * SUBLANE + tm, :]
                   + v0 * dubuf[3 * SUBLANE:3 * SUBLANE + tm, :])
            dp_ref[:, 3 * d + j * cw:3 * d + (j + 1) * cw] = dxb.astype(BF16)
            dv_ref[DV_CONV_B_B:DV_CONV_B_B + 1, cs] += colsum(du)
            dv_ref[DV_CONV_B + 0:DV_CONV_B + 1, cs] += colsum(du * xbuf[0:tm, :])
            dv_ref[DV_CONV_B + 1:DV_CONV_B + 2, cs] += colsum(du * xbuf[SUBLANE:SUBLANE + tm, :])
            dv_ref[DV_CONV_B + 2:DV_CONV_B + 3, cs] += colsum(du * xbuf[2 * SUBLANE:2 * SUBLANE + tm, :])
            dv_ref[DV_CONV_B + 3:DV_CONV_B + 4, cs] += colsum(du * xb)

    rt = lambda i: n_t - 1 - i
    row_spec = pl.BlockSpec((tm, d), lambda i: (rt(i), 0))
    slab = lambda s: pl.BlockSpec((tm, d), lambda i, s=s: (rt(i), s))
    before = lambda s: pl.BlockSpec((halo, d), lambda i, s=s: (jnp.maximum(rt(i) * (tm // halo) - 1, 0), s))
    small = pl.BlockSpec((SUBLANE, d), lambda i: (0, 0))
    bd = pl.BlockSpec((nb, cw, cw), lambda i: (0, 0, 0))
    sd = jax.ShapeDtypeStruct
    wbuf = lambda n: pltpu.VMEM((tm + n * SUBLANE, cw), F32)
    head = lambda n: pltpu.VMEM((n * SUBLANE, d), F32)
    return _hosted_call(
        body, comms, name=name, grid=(n_t,),
        out_shape=[sd((t, N_PROJ * d), BF16), sd((DV_ROWS, d), F32), sd((nb, cw, cw), F32), sd((nb, cw, cw), F32)],
        in_specs=[row_spec, row_spec, ANY,
                  slab(0), slab(1), slab(2), slab(3), slab(4), before(1), before(2), before(3),
                  row_spec, row_spec, before(0), small, small, small, bd, bd],
        out_specs=[pl.BlockSpec((tm, 5 * d), lambda i: (rt(i), 0)), pl.BlockSpec((DV_ROWS, d), lambda i: (0, 0)), bd, bd],
        aliases={2: 0},
        scratch_shapes=[wbuf(ka), wbuf(kb), wbuf(1), wbuf(ka), wbuf(kb), wbuf(1),
                        pltpu.VMEM((tm, cw), F32), pltpu.VMEM((tm, cw), F32), head(ka), head(kb), head(1), head(1)],
        args=(dya, dyb, dp_gates, p, p, p, p, p, p, p, p, u_s, h_s, h_s, caw, cbw, vec, bda, bdx))


def _in_proj_bwd(dp, x, dx1, g_row, gath, grp, name, comms=()):
    t, d = x.shape
    tm = _row_tile(t, 512)
    n_in = N_PROJ * d

    def body(dp_ref, x_ref, dx1_ref, g_ref, gath_ref, dx_ref, dg_ref, w_in, sems):
        @pl.when(pl.program_id(0) == 0)
        def _():
            _load_weights(gath_ref, grp, ["in"], [w_in], sems)
            _zero(dg_ref)

        dh = _dot_nn(dp_ref[:, 0:d], w_in[0:d, :])
        for k in range(1, N_PROJ):
            dh = dh + _dot_nn(dp_ref[:, k * d:(k + 1) * d], w_in[k * d:(k + 1) * d, :])
        xf = x_ref[...]
        rstd = lax.rsqrt(jnp.mean(xf * xf, axis=-1, keepdims=True) + RMS_EPS)
        xh = xf * rstd
        g = g_ref[...]
        dg_ref[0:1, :] += jnp.sum(dh * xh, axis=0, keepdims=True)
        dxh = dh * g
        dx_ref[...] = dx1_ref[...] + rstd * (dxh - xh * jnp.mean(dxh * xh, axis=-1, keepdims=True))

    row = pl.BlockSpec((tm, d), lambda i: (i, 0))
    sd = jax.ShapeDtypeStruct
    return _hosted_call(
        body, comms, name=name, grid=(t // tm,),
        out_shape=[sd((t, d), F32), sd((SUBLANE, d), F32)],
        in_specs=[pl.BlockSpec((tm, n_in), lambda i: (i, 0)), row, row, pl.BlockSpec((1, d), lambda i: (0, 0)), ANY],
        out_specs=[row, pl.BlockSpec((SUBLANE, d), lambda i: (0, 0))],
        scratch_shapes=[pltpu.VMEM((n_in, d), BF16), pltpu.SemaphoreType.DMA((N_DEV,))],
        args=(dp, x, dx1, g_row, gath))


def _weight_grad(a, b, name):
    t, m = a.shape
    n = b.shape[1]
    bm = m
    for div in (1, 2, 4, 8):
        if m % div == 0 and (m // div) % LANE == 0 and (m // div) * n * 4 <= (12 << 20):
            bm = m // div
            break
    fixed = bm * n * (4 + 2 * 2)
    bt = next(_row_tile(t, want) for want in (2048, 1024, 512)
              if 2 * 2 * _row_tile(t, want) * (bm + n) + fixed <= VMEM_LIMIT - (12 << 20))
    n_t = t // bt

    def body(a_ref, b_ref, o_ref, acc):
        k = pl.program_id(1)

        @pl.when(k == 0)
        def _():
            _zero(acc)

        acc[...] += _dot_tn(a_ref[...], b_ref[...])

        @pl.when(k == n_t - 1)
        def _():
            o_ref[...] = acc[...].astype(BF16)

    return pl.pallas_call(
        body, name=name, grid=(m // bm, n_t),
        out_shape=jax.ShapeDtypeStruct((m, n), BF16),
        in_specs=[pl.BlockSpec((bt, bm), lambda i, k: (k, i)), pl.BlockSpec((bt, n), lambda i, k: (k, 0))],
        out_specs=pl.BlockSpec((bm, n), lambda i, k: (i, 0)),
        scratch_shapes=[pltpu.VMEM((bm, n), F32)],
        compiler_params=_params(2),
    )(a, b)


def _adamw(w, g, m, v, name):
    r, c = w.shape
    tr = _fit_rows(r, c * 4)
    c1 = 1.0 - ADAM_B1 ** ADAM_STEP
    c2 = 1.0 - ADAM_B2 ** ADAM_STEP

    def body(w_ref, g_ref, m_ref, v_ref, d_ref, nm_ref, nv_ref):
        g32 = g_ref[...]
        nm = ADAM_B1 * m_ref[...] + (1.0 - ADAM_B1) * g32
        nv = ADAM_B2 * v_ref[...] + (1.0 - ADAM_B2) * (g32 * g32)
        nm_ref[...] = nm
        nv_ref[...] = nv
        d_ref[...] = -ADAM_LR * ((nm / c1) / (jnp.sqrt(nv / c2) + ADAM_EPS) + ADAM_WD * w_ref[...])

    spec = pl.BlockSpec((tr, c), lambda i: (i, 0))
    return pl.pallas_call(
        body, name=name, grid=(r // tr,),
        out_shape=[jax.ShapeDtypeStruct((r, c), F32)] * 3,
        in_specs=[spec] * 4, out_specs=[spec] * 3,
        compiler_params=_params(),
    )(w, g, m, v)


def _pad_rows(a, mult=SUBLANE):
    pad = (-a.shape[0]) % mult
    return a if pad == 0 else jnp.concatenate([a, jnp.zeros((pad,) + a.shape[1:], a.dtype)], axis=0)


REPLICATED = ("ln1_g", "conv_b_b", "lru_wa", "lru_ba", "lru_wx", "lru_bx", "lru_lambda", "ln2_g", "final_g")
SMALL_SHARDED = ("conv_a_w", "conv_b_w", "gate_bias")
MATRICES = ("w_in", "w_out_a", "w_out_b", "w_o", "w_ffn_gate", "w_ffn_up", "w_ffn_down")
ORDER = ("ln1_g", "w_in", "conv_a_w", "conv_b_w", "conv_b_b", "lru_wa", "lru_ba", "lru_wx", "lru_bx", "lru_lambda",
         "w_out_a", "w_out_b", "gate_bias", "w_o", "ln2_g", "w_ffn_gate", "w_ffn_up", "w_ffn_down", "final_g")


def kernel(x, ln1_g, w_in, conv_a_w, conv_b_w, conv_b_b, lru_wa, lru_ba, lru_wx, lru_bx, lru_lambda, w_out_a, w_out_b, gate_bias, w_o, ln2_g, w_ffn_gate, w_ffn_up, w_ffn_down, final_g, loss_target, m_ln1_g, m_w_in, m_conv_a_w, m_conv_b_w, m_conv_b_b, m_lru_wa, m_lru_ba, m_lru_wx, m_lru_bx, m_lru_lambda, m_w_out_a, m_w_out_b, m_gate_bias, m_w_o, m_ln2_g, m_w_ffn_gate, m_w_ffn_up, m_w_ffn_down, m_final_g, v_ln1_g, v_w_in, v_conv_a_w, v_conv_b_w, v_conv_b_b, v_lru_wa, v_lru_ba, v_lru_wx, v_lru_bx, v_lru_lambda, v_w_out_a, v_w_out_b, v_gate_bias, v_w_o, v_ln2_g, v_w_ffn_gate, v_w_ffn_up, v_w_ffn_down, v_final_g):
    w = dict(ln1_g=ln1_g, w_in=w_in, conv_a_w=conv_a_w, conv_b_w=conv_b_w, conv_b_b=conv_b_b, lru_wa=lru_wa,
             lru_ba=lru_ba, lru_wx=lru_wx, lru_bx=lru_bx, lru_lambda=lru_lambda, w_out_a=w_out_a, w_out_b=w_out_b,
             gate_bias=gate_bias, w_o=w_o, ln2_g=ln2_g, w_ffn_gate=w_ffn_gate, w_ffn_up=w_ffn_up,
             w_ffn_down=w_ffn_down, final_g=final_g)
    mom = dict(ln1_g=m_ln1_g, w_in=m_w_in, conv_a_w=m_conv_a_w, conv_b_w=m_conv_b_w, conv_b_b=m_conv_b_b,
               lru_wa=m_lru_wa, lru_ba=m_lru_ba, lru_wx=m_lru_wx, lru_bx=m_lru_bx, lru_lambda=m_lru_lambda,
               w_out_a=m_w_out_a, w_out_b=m_w_out_b, gate_bias=m_gate_bias, w_o=m_w_o, ln2_g=m_ln2_g,
               w_ffn_gate=m_w_ffn_gate, w_ffn_up=m_w_ffn_up, w_ffn_down=m_w_ffn_down, final_g=m_final_g)
    var = dict(ln1_g=v_ln1_g, w_in=v_w_in, conv_a_w=v_conv_a_w, conv_b_w=v_conv_b_w, conv_b_b=v_conv_b_b,
               lru_wa=v_lru_wa, lru_ba=v_lru_ba, lru_wx=v_lru_wx, lru_bx=v_lru_bx, lru_lambda=v_lru_lambda,
               w_out_a=v_w_out_a, w_out_b=v_w_out_b, gate_bias=v_gate_bias, w_o=v_w_o, ln2_g=v_ln2_g,
               w_ffn_gate=v_w_ffn_gate, w_ffn_up=v_w_ffn_up, w_ffn_down=v_w_ffn_down, final_g=v_final_g)

    _, t, d = x.shape
    n_layers = w_in.shape[0]
    ff = w_ffn_down.shape[1] * N_DEV
    dd = d // N_DEV
    hd = d // LRU_HEADS
    cw = min(MXU_TILE, d)
    nb = d // cw
    hpt = cw // hd
    grp = _groups(d, ff)
    me = 4 * lax.axis_index("x") + 2 * lax.axis_index("y") + lax.axis_index("c")
    tm_time = _time_tile(t)
    x0 = _to_tile_order(x[0], tm_time)
    target = _to_tile_order(loss_target[0], tm_time)

    packed = [{"in": jnp.swapaxes(w_in[l], 0, 1).astype(BF16),
               "rest": jnp.concatenate([w_out_a[l], w_out_b[l], w_o[l], jnp.swapaxes(w_ffn_gate[l], 0, 1),
                                        jnp.swapaxes(w_ffn_up[l], 0, 1), w_ffn_down[l]], axis=0).astype(BF16)}
              for l in range(n_layers)]
    n_small = CONV_A_K + CONV_B_K + 2
    small = _pad_rows(jnp.concatenate([conv_a_w, conv_b_w, gate_bias], axis=1).reshape(n_layers * n_small, dd))
    sp = jax.nn.softplus(-lru_lambda)
    vec = [_pad_rows(jnp.stack([conv_b_b[l], lru_ba[l], lru_bx[l], sp[l]])) for l in range(n_layers)]
    eye = jnp.eye(hpt, dtype=F32)

    def block_diag(wh):
        return jnp.einsum("jkab,kl->jkalb", wh.reshape(nb, hpt, hd, hd), eye).reshape(nb, cw, cw).astype(BF16)

    bda = [block_diag(lru_wa[l]) for l in range(n_layers)]
    bdx = [block_diag(lru_wx[l]) for l in range(n_layers)]

    gath = [dict() for _ in range(n_layers)]
    (gath[0]["in"],), (small_g,) = _comm_call([_Gather(packed[0]["in"]), _Gather(small)], "gather_in_0")
    small_full = jnp.swapaxes(small_g[:, :n_layers * n_small], 0, 1).reshape(n_layers, n_small, d)
    caw = [_pad_rows(small_full[k, 0:CONV_A_K]) for k in range(n_layers)]
    cbw = [_pad_rows(small_full[k, CONV_A_K:CONV_A_K + CONV_B_K]) for k in range(n_layers)]
    gbias = [_pad_rows(small_full[k, CONV_A_K + CONV_B_K:]) for k in range(n_layers)]
    saved = []
    xl = x0
    for l in range(n_layers):
        more = l + 1 < n_layers
        (p, h1b, ya, yb, *kept), got = _in_proj_mixer_fwd(
            xl, ln1_g[l][None], gath[l]["in"], grp["in"], caw[l], cbw[l], vec[l], bda[l], bdx[l], f"in_proj_mixer_fwd_{l}",
            [_Gather(packed[0]["rest"])] if l == 0 else [])
        if l == 0:
            ((gath[0]["rest"],),) = got
        (x1, oa, ob), got = _merge_fwd(xl, ya, yb, p, gbias[l], gath[l]["rest"], grp["rest"], f"merge_fwd_{l}",
                                       [_Gather(packed[l + 1]["in"])] if more else [])
        if more:
            ((gath[l + 1]["in"],),) = got
        (x2, gg, uu), got = _ffn_fwd(x1, ln2_g[l][None], gath[l]["rest"], grp["rest"], f"ffn_fwd_{l}",
                                     [_Gather(packed[l + 1]["rest"])] if more else [])
        if more:
            ((gath[l + 1]["rest"],),) = got
        saved.append(dict(x=xl, p=p, h1b=h1b, ya=ya, yb=yb, mixer=kept, x1=x1, oa=oa, ob=ob, gg=gg, uu=uu))
        xl = x2
    loss_tile, dx, dfinal = _loss_head(xl, final_g[None], target, "loss_head")
    loss = lax.psum(loss_tile[0, 0], ("x", "y", "c"))

    def heads(dwb):
        blocks = jnp.diagonal(dwb.reshape(nb, hpt, hd, hpt, hd), axis1=1, axis2=3)
        return jnp.moveaxis(blocks, 3, 1).reshape(hd, d)

    layer_names = [n for n in REPLICATED if n != "final_g"] + list(SMALL_SHARDED)

    def layer_block(k):
        return jnp.concatenate([small_grads[k][n] for n in layer_names], axis=0)

    recv = [dict() for _ in range(n_layers)]
    small_grads = [None] * n_layers
    early_all = None
    xg = {"d": _Group(("d",), (ff // N_DEV,)), "gu": _Group(("g", "u"), (ff // N_DEV,) * 2),
          "out": _Group(("oa", "ob", "o"), (dd,) * 3), "in": grp["in"]}
    far_in = None
    for l in reversed(range(n_layers)):
        s = saved[l]
        (dgg, duu, dw_d), got = _ffn_bwd_act(dx, s["gg"], s["uu"], gath[l]["rest"], grp["rest"], f"ffn_bwd_act_{l}",
                                             [far_in] if far_in else [])
        if far_in:
            recv[l + 1]["in"].append(got[0][0])
        (dx1, dx1b, h2b, dln2), got = _ffn_bwd_in(dgg, duu, dx, s["x1"], ln2_g[l][None], gath[l]["rest"], grp["rest"],
                                                  f"ffn_bwd_in_{l}", [_Exchange({"d": dw_d}, xg["d"])])
        recv[l]["d"] = [got[0][0]]
        dw_gu = {"g": _weight_grad(dgg, h2b, f"dw_ffn_gate_{l}"), "u": _weight_grad(duu, h2b, f"dw_ffn_up_{l}")}
        (dya, dyb, dp_gates, dgbias, dw_oa, dw_ob, dw_o), got = _merge_bwd(
            dx1b, s["oa"], s["ob"], s["ya"], s["yb"], s["p"], gbias[l], gath[l]["rest"], grp["rest"], f"merge_bwd_{l}",
            [_Exchange(dw_gu, xg["gu"])])
        recv[l]["gu"] = [got[0][0]]
        comms = [_Exchange({"oa": dw_oa, "ob": dw_ob, "o": dw_o}, xg["out"])]
        if l == 0:
            early = [layer_block(k) for k in range(1, n_layers)] + [_pad_rows(dfinal[0:1])]
            comms.append(_Gather(jnp.concatenate(early, axis=0)))
        (dp, dv, dwa, dwx), got = _mixer_bwd(dya, dyb, dp_gates, s["p"], *s["mixer"], caw[l], cbw[l], vec[l], bda[l], bdx[l],
                                             f"mixer_bwd_{l}", comms)
        recv[l]["out"] = [got[0][0]]
        if l == 0:
            early_all = got[1][0]
        small_grads[l] = {
            "conv_b_b": dv[DV_CONV_B_B:DV_CONV_B_B + 1], "lru_wa": heads(dwa),
            "lru_ba": dv[DV_BA:DV_BA + 1], "lru_wx": heads(dwx), "lru_bx": dv[DV_BX:DV_BX + 1],
            "lru_lambda": dv[DV_SP:DV_SP + 1] * (-jax.nn.sigmoid(-lru_lambda[l]))[None], "ln2_g": dln2[0:1],
            "conv_a_w": dv[DV_CONV_A:DV_CONV_A + CONV_A_K], "conv_b_w": dv[DV_CONV_B:DV_CONV_B + CONV_B_K],
            "gate_bias": dgbias[0:2],
        }
        dw_in = {"in": _weight_grad(dp, s["h1b"], f"dw_in_{l}")}
        if l > 0:
            near_in, far_in = _Exchange(dw_in, xg["in"], NEAR_PEERS), _Exchange(dw_in, xg["in"], FAR_PEERS, local=False)
        else:
            near_in, far_in = _Exchange(dw_in, xg["in"]), None
        (dx, dln1), got = _in_proj_bwd(dp, s["x"], dx1, ln1_g[l][None], gath[l]["in"], grp["in"], f"in_proj_bwd_{l}", [near_in])
        recv[l]["in"] = [got[0][0]]
        small_grads[l]["ln1_g"] = dln1[0:1]
    grad_x = _from_tile_order(dx, tm_time)[None]

    g = {}
    gsum = [{k: _sum_slots(recv[l][k], f"sum_{k}_{l}") for k in xg} for l in range(n_layers)]

    def part(key):
        k = next(name for name, group in xg.items() if key in group.keys)
        o, r = xg[k].off[key], xg[k].rows[key]
        return jnp.stack([gsum[l][k][o:o + r] for l in range(n_layers)])

    g = {"w_in": jnp.swapaxes(part("in"), 1, 2), "w_out_a": part("oa"), "w_out_b": part("ob"), "w_o": part("o"),
         "w_ffn_gate": jnp.swapaxes(part("g"), 1, 2), "w_ffn_up": jnp.swapaxes(part("u"), 1, 2), "w_ffn_down": part("d")}
    ((late_all,),) = _comm_call([_Gather(layer_block(0).astype(BF16))], "gather_small_grads_0")
    early_sum = _sum_slots([early_all], "sum_small_grads")
    block_rows = late_all.shape[1]
    per_layer = [_sum_slots([late_all], "sum_small_grads_0")]
    per_layer += [early_sum[(k - 1) * block_rows:k * block_rows] for k in range(1, n_layers)]
    g["final_g"] = early_sum[(n_layers - 1) * block_rows].reshape(w["final_g"].shape)
    o = 0
    for n in layer_names:
        rows = small_grads[0][n].shape[0]
        stacked = jnp.concatenate([per_layer[k][o:o + rows] for k in range(n_layers)], axis=0)
        if n in SMALL_SHARDED:
            g[n] = lax.dynamic_slice_in_dim(stacked, me * dd, dd, axis=1).reshape(n_layers, rows, dd)
        else:
            g[n] = stacked.reshape(w[n].shape)
        o += rows

    delta, new_m, new_v = {}, {}, {}
    gate_maps = ("lru_wa", "lru_wx")
    for n in MATRICES + gate_maps:
        shape = w[n].shape
        flat = lambda a: a.reshape(-1, d if n in gate_maps else shape[-1])
        dl, nm, nv = _adamw(flat(w[n]), flat(g[n]), flat(mom[n]), flat(var[n]), f"adamw_{n}")
        delta[n], new_m[n], new_v[n] = dl.reshape(shape), nm.reshape(shape), nv.reshape(shape)
    vectors = tuple(n for n in REPLICATED if n not in gate_maps)
    for group, width, name in ((vectors, d, "adamw_replicated"), (SMALL_SHARDED, dd, "adamw_vectors")):
        cat = lambda src: _pad_rows(jnp.concatenate([src[n].reshape(-1, width) for n in group], axis=0))
        dl, nm, nv = _adamw(cat(w), cat(g), cat(mom), cat(var), name)
        o = 0
        for n in group:
            rows = w[n].size // width
            delta[n], new_m[n], new_v[n] = (a[o:o + rows].reshape(w[n].shape) for a in (dl, nm, nv))
            o += rows

    return (loss, grad_x, *[g[n] for n in ORDER], *[delta[n] for n in ORDER], *[new_m[n] for n in ORDER],
            *[new_v[n] for n in ORDER])
```

```python
import math

import jax
import jax.numpy as jnp
from jax import lax
from jax.experimental import pallas as pl
from jax.experimental.pallas import tpu as pltpu

F32 = jnp.float32
BF16 = jnp.bfloat16

N_DEV = 8
N_PROJ = 7
LRU_HEADS = 16
LRU_C = 8.0
RMS_EPS = 1e-6
CONV_A_K = 3
CONV_B_K = 4
GELU_C = math.sqrt(2.0 / math.pi)
GELU_A = 0.044715

ADAM_LR = 0.001
ADAM_B1 = 0.9
ADAM_B2 = 0.999
ADAM_EPS = 1e-08
ADAM_WD = 0.01
ADAM_STEP = 10

LANE = 128
SUBLANE = 8
MXU_TILE = 256
VMEM_LIMIT = 52 << 20
ALL_PEERS = tuple(range(1, N_DEV))
NEAR_PEERS = (1, 2, 3, 4, 5)
FAR_PEERS = (6, 7)
MESH = pl.DeviceIdType.MESH
ANY = pl.BlockSpec(memory_space=pl.ANY)


def _dot_nn(a, b):
    return lax.dot_general(a, b, (((1,), (0,)), ((), ())), preferred_element_type=F32)


def _dot_nt(a, b):
    return lax.dot_general(a, b, (((1,), (1,)), ((), ())), preferred_element_type=F32)


def _dot_tn(a, b):
    return lax.dot_general(a, b, (((0,), (0,)), ((), ())), preferred_element_type=F32)


def _sigmoid(x):
    return 1.0 / (1.0 + jnp.exp(-x))


def _gelu_and_grad(x):
    x2 = x * x
    t = jnp.tanh(GELU_C * x * (1.0 + GELU_A * x2))
    g = 0.5 * x * (1.0 + t)
    dg = 0.5 * (1.0 + t) + 0.5 * x * (1.0 - t * t) * GELU_C * (1.0 + 3.0 * GELU_A * x2)
    return g, dg


def _zero(ref):
    ref[...] = jnp.zeros(ref.shape, ref.dtype)


def _fit_rows(r, row_bytes, budget=1 << 20):
    fits = [t for t in range(16, r + 1, 16) if r % t == 0 and t * row_bytes <= budget]
    return max(fits) if fits else r


def _row_tile(t, want):
    tm = min(want, t // 2)
    assert t % tm == 0 and tm % SUBLANE == 0, (t, tm)
    return tm


def _params(n_grid=1, **kw):
    return pltpu.CompilerParams(dimension_semantics=("arbitrary",) * n_grid, vmem_limit_bytes=VMEM_LIMIT, **kw)


class _Group:
    def __init__(self, keys, rows):
        self.keys = keys
        self.rows = dict(zip(keys, rows))
        self.off, o = {}, 0
        for k in keys:
            self.off[k] = o
            o += self.rows[k]
        self.total = o


def _groups(d, ff):
    dd, ffs = d // N_DEV, ff // N_DEV
    return {"in": _Group(("in",), (N_PROJ * dd,)),
            "rest": _Group(("oa", "ob", "o", "g", "u", "d"), (dd, dd, dd, ffs, ffs, ffs))}


def _load_weights(g_ref, grp, keys, dsts, sems):
    copies = []
    for n, (k, dst) in enumerate(zip(keys, dsts)):
        rows, off = grp.rows[k], grp.off[k]
        copies += [pltpu.make_async_copy(g_ref.at[p, pl.ds(off, rows), :], dst.at[pl.ds(p * rows, rows), :],
                                         sems.at[n * N_DEV + p]) for p in range(N_DEV)]
    for c in copies:
        c.start()
    for c in copies:
        c.wait()


def _comm_sems():
    return [pltpu.SemaphoreType.DMA((N_DEV - 1,)), pltpu.SemaphoreType.DMA((N_DEV - 1,)), pltpu.SemaphoreType.DMA]


class _Gather:
    def __init__(self, x):
        self.inputs = [x]
        self.out_shape = [jax.ShapeDtypeStruct((N_DEV,) + x.shape, x.dtype)]
        self.scratch = _comm_sems()

    def _plan(self, ins, outs, scr):
        (x_ref,), (out_ref,), (send_sems, recv_sems, local_sem) = ins, outs, scr
        mx, my, mc = lax.axis_index("x"), lax.axis_index("y"), lax.axis_index("c")
        me, sibling = (mx, my, mc), (mx, my, 1 - mc)
        xn, yn, dg = (1 - mx, my), (mx, 1 - my), (1 - mx, 1 - my)
        core0 = mc == 0
        relayed = (jnp.where(core0, 1 - mx, mx), jnp.where(core0, my, 1 - my))
        relay_to = (jnp.where(core0, mx, 1 - mx), jnp.where(core0, 1 - my, my))

        def slot(px, py, pc):
            return out_ref.at[4 * px + 2 * py + pc]

        def copy(k, block, to, src=None):
            return pltpu.make_async_remote_copy(
                src_ref=slot(*block) if src is None else src, dst_ref=slot(*block),
                send_sem=send_sems.at[k], recv_sem=recv_sems.at[k], device_id=to, device_id_type=MESH)

        mine = lambda: pltpu.make_async_copy(x_ref, slot(*me), local_sem)
        own = [lambda: copy(0, me, sibling, src=x_ref), lambda: copy(1, me, (*xn, mc), src=x_ref),
               lambda: copy(2, me, (*yn, mc), src=x_ref)]
        relay = lambda: copy(3, (*relayed, mc), (*relay_to, mc))
        passes = [lambda: copy(4, (*xn, mc), sibling), lambda: copy(5, (*yn, mc), sibling), lambda: copy(6, (*dg, mc), sibling)]
        arrival = lambda k: copy(k, me, me)
        return mine, own, relay, passes, arrival

    def start(self, ins, outs, scr):
        mine, own, _, _, _ = self._plan(ins, outs, scr)
        mine().start()
        for cp in own:
            cp().start()

    def relay(self, ins, outs, scr):
        _, _, relay, passes, arrival = self._plan(ins, outs, scr)
        arrival(1).wait_recv()
        arrival(2).wait_recv()
        relay().start()
        passes[0]().start()
        passes[1]().start()

    def mid(self, ins, outs, scr):
        _, _, _, passes, arrival = self._plan(ins, outs, scr)
        arrival(3).wait_recv()
        passes[2]().start()

    def finish(self, ins, outs, scr):
        mine, _, _, _, arrival = self._plan(ins, outs, scr)
        for k in (0, 4, 5, 6):
            arrival(k).wait_recv()
        for k in range(N_DEV - 1):
            arrival(k).wait_send()
        mine().wait()


class _Exchange:
    def __init__(self, mats, grp, peers=ALL_PEERS, local=True):
        self.grp, self.peers, self.local = grp, tuple(peers), local
        self.inputs = [mats[k] for k in grp.keys]
        slots = len(self.peers) + (1 if local else 0)
        self.out_shape = [jax.ShapeDtypeStruct((slots, grp.total, self.inputs[0].shape[1]), BF16)]
        self.scratch = [pltpu.SemaphoreType.DMA((len(self.peers),)), pltpu.SemaphoreType.DMA((len(self.peers),)),
                        pltpu.SemaphoreType.DMA]

    def _pieces(self, g_refs, out_ref, q, dst_slot):
        out = []
        for g_ref, k in zip(g_refs, self.grp.keys):
            rows = self.grp.rows[k]
            out.append((g_ref.at[pl.ds(pl.multiple_of(q * rows, 16), rows), :],
                        out_ref.at[dst_slot, pl.ds(self.grp.off[k], rows), :]))
        return out

    def start(self, ins, outs, scr):
        (out_ref,), (send_sems, recv_sems, local_sem) = outs, scr
        mx, my, mc = lax.axis_index("x"), lax.axis_index("y"), lax.axis_index("c")
        if self.local:
            for s, t in self._pieces(ins, out_ref, 4 * mx + 2 * my + mc, 0):
                pltpu.make_async_copy(s, t, local_sem).start()
        for n, k in enumerate(self.peers):
            px, py, pc = mx ^ ((k >> 2) & 1), my ^ ((k >> 1) & 1), mc ^ (k & 1)
            for s, t in self._pieces(ins, out_ref, 4 * px + 2 * py + pc, n + (1 if self.local else 0)):
                pltpu.make_async_remote_copy(src_ref=s, dst_ref=t, send_sem=send_sems.at[n], recv_sem=recv_sems.at[n],
                                             device_id=(px, py, pc), device_id_type=MESH).start()

    def relay(self, ins, outs, scr):
        pass

    def mid(self, ins, outs, scr):
        pass

    def finish(self, ins, outs, scr):
        (out_ref,), (send_sems, recv_sems, local_sem) = outs, scr
        mx, my, mc = lax.axis_index("x"), lax.axis_index("y"), lax.axis_index("c")
        whole = out_ref.at[0]
        for n in range(len(self.peers)):
            done = pltpu.make_async_remote_copy(src_ref=whole, dst_ref=whole, send_sem=send_sems.at[n],
                                                recv_sem=recv_sems.at[n], device_id=(mx, my, mc), device_id_type=MESH)
            done.wait_send()
            done.wait_recv()
        if self.local:
            pltpu.make_async_copy(whole, whole, local_sem).wait()


def _split(refs, sizes):
    out, pos = [], 0
    for n in sizes:
        out.append(refs[pos:pos + n])
        pos += n
    return out


def _hosted_call(body, comms, *, name, grid, in_specs, out_specs, out_shape, scratch_shapes, args, aliases=None):
    n_steps = grid[0]
    nc = len(comms)
    sizes = ([len(in_specs)] + [len(c.inputs) for c in comms] + [len(out_specs)] + [len(c.out_shape) for c in comms]
             + [len(scratch_shapes)] + [len(c.scratch) for c in comms])

    def hosted(*refs):
        parts = _split(refs, sizes)
        ins, c_ins = parts[0], parts[1:1 + nc]
        outs, c_outs = parts[1 + nc], parts[2 + nc:2 + 2 * nc]
        scr, c_scr = parts[2 + 2 * nc], parts[3 + 2 * nc:]
        step = pl.program_id(0)
        if comms:
            @pl.when(step == 0)
            def _():
                for c, a, b, s in zip(comms, c_ins, c_outs, c_scr):
                    c.start(a, b, s)

            relay_step = (3 * n_steps) // 5

            @pl.when(step == relay_step)
            def _():
                for c, a, b, s in zip(comms, c_ins, c_outs, c_scr):
                    c.relay(a, b, s)

            @pl.when(step == max(n_steps - 2, relay_step))
            def _():
                for c, a, b, s in zip(comms, c_ins, c_outs, c_scr):
                    c.mid(a, b, s)

        body(*ins, *outs, *scr)
        if comms:
            @pl.when(step == n_steps - 1)
            def _():
                for c, a, b, s in zip(comms, c_ins, c_outs, c_scr):
                    c.finish(a, b, s)

    res = pl.pallas_call(
        hosted, name=name, grid=grid,
        out_shape=[*out_shape, *[o for c in comms for o in c.out_shape]],
        in_specs=[*in_specs, *[ANY for c in comms for _ in c.inputs]],
        out_specs=[*out_specs, *[ANY for c in comms for _ in c.out_shape]],
        scratch_shapes=[*scratch_shapes, *[s for c in comms for s in c.scratch]],
        input_output_aliases=aliases or {},
        compiler_params=_params(),
    )(*args, *[a for c in comms for a in c.inputs])
    main, rest = res[:len(out_specs)], res[len(out_specs):]
    return main, _split(rest, [len(c.out_shape) for c in comms])


def _comm_call(comms, name):
    sizes = [len(c.inputs) for c in comms] + [len(c.out_shape) for c in comms] + [len(c.scratch) for c in comms]
    nc = len(comms)

    def body(*refs):
        parts = _split(refs, sizes)
        triples = list(zip(comms, parts[:nc], parts[nc:2 * nc], parts[2 * nc:]))
        for phase in ("start", "relay", "mid", "finish"):
            for c, ins, outs, scr in triples:
                getattr(c, phase)(ins, outs, scr)

    res = pl.pallas_call(
        body, name=name, out_shape=[o for c in comms for o in c.out_shape],
        in_specs=[ANY for c in comms for _ in c.inputs], out_specs=[ANY for c in comms for _ in c.out_shape],
        scratch_shapes=[s for c in comms for s in c.scratch],
    )(*[a for c in comms for a in c.inputs])
    return _split(res, [len(c.out_shape) for c in comms])


def _sum_slots(xs, name):
    _, r, c = xs[0].shape
    tr = _fit_rows(r, c * 4)

    def body(*refs):
        acc = None
        for x_ref in refs[:-1]:
            for p in range(x_ref.shape[0]):
                v = x_ref[p].astype(F32)
                acc = v if acc is None else acc + v
        refs[-1][...] = acc

    return pl.pallas_call(
        body, name=name, grid=(r // tr,),
        out_shape=jax.ShapeDtypeStruct((r, c), F32),
        in_specs=[pl.BlockSpec((x.shape[0], tr, c), lambda i: (0, i, 0)) for x in xs],
        out_specs=pl.BlockSpec((tr, c), lambda i: (i, 0)),
        compiler_params=_params(),
    )(*xs)


def _time_tile(t):
    return _row_tile(t, 256)


def _to_tile_order(a, tm):
    t, c = a.shape
    return jnp.swapaxes(a.reshape(t // tm, SUBLANE, tm // SUBLANE, c), 1, 2).reshape(t, c)


def _from_tile_order(a, tm):
    t, c = a.shape
    return jnp.swapaxes(a.reshape(t // tm, tm // SUBLANE, SUBLANE, c), 1, 2).reshape(t, c)


def _causal_fill(buf, v, prev_tail, n, row):
    tm = v.shape[0]
    for q in range(n):
        cur = v[tm - SUBLANE * (n - q):tm - SUBLANE * (n - q - 1), :]
        prv = prev_tail[SUBLANE * q:SUBLANE * (q + 1), :]
        buf[SUBLANE * q:SUBLANE * (q + 1), :] = jnp.where(row == 0, pltpu.roll(prv, 1, 0), pltpu.roll(cur, 1, 0))
    buf[SUBLANE * n:, :] = v


def _anticausal_fill(buf, v, next_head, n, row):
    tm = v.shape[0]
    buf[0:tm, :] = v
    for q in range(n):
        cur = v[SUBLANE * q:SUBLANE * (q + 1), :]
        nxt = next_head[SUBLANE * q:SUBLANE * (q + 1), :]
        buf[tm + SUBLANE * q:tm + SUBLANE * (q + 1), :] = jnp.where(
            row == SUBLANE - 1, pltpu.roll(nxt, SUBLANE - 1, 0), pltpu.roll(cur, SUBLANE - 1, 0))


def _chain_scan(abuf, bbuf, nk, reverse):
    cw = abuf.shape[1]

    def step(n, carry):
        h, c = carry
        r0 = pl.multiple_of((nk - 1 - n if reverse else n) * SUBLANE, SUBLANE)
        ak = abuf[pl.ds(r0, SUBLANE), :]
        h = ak * h + bbuf[pl.ds(r0, SUBLANE), :]
        c = ak * c
        bbuf[pl.ds(r0, SUBLANE), :] = h
        abuf[pl.ds(r0, SUBLANE), :] = c
        return h, c

    return lax.fori_loop(0, nk, step, (jnp.zeros((SUBLANE, cw), F32), jnp.ones((SUBLANE, cw), F32)), unroll=True)


def _sublane_scan(a, b, row, reverse):
    for sh in (1, 2, 4):
        if reverse:
            m = row < SUBLANE - sh
            b = jnp.where(m, a * pltpu.roll(b, SUBLANE - sh, 0) + b, b)
            a = jnp.where(m, a * pltpu.roll(a, SUBLANE - sh, 0), a)
        else:
            m = row >= sh
            b = jnp.where(m, a * pltpu.roll(b, sh, 0) + b, b)
            a = jnp.where(m, a * pltpu.roll(a, sh, 0), a)
    return a, b


def _lru_gates(ub, bda, bdx, ba, bx, sp):
    r = _sigmoid(_dot_nn(ub, bda) + ba)
    i = _sigmoid(_dot_nn(ub, bdx) + bx)
    log_a = (-LRU_C) * r * sp
    a = jnp.exp(log_a)
    s2 = -jnp.tanh(log_a) * (1.0 + a * a)
    inv_s = lax.rsqrt(s2)
    s = jnp.where(s2 > 0.0, s2 * inv_s, 0.0)
    return r, i, a, s, inv_s


def _in_proj_mixer_fwd(x, g_row, gath, grp, caw, cbw, vec, bda, bdx, name, comms=()):
    t, d = x.shape
    n_in = N_PROJ * d
    tm = _time_tile(t)
    nk = tm // SUBLANE
    cw = min(MXU_TILE, d)
    nb = d // cw

    def body(x_ref, g_ref, gath_ref, caw_ref, cbw_ref, vec_ref, bda_ref, bdx_ref,
             p_ref, h1_ref, ya_ref, yb_ref, u_ref, h_ref, w_in, sems, zbuf, xbuf, abuf, bbuf, z_tail, x_tail, h_carry):
        @pl.when(pl.program_id(0) == 0)
        def _():
            _load_weights(gath_ref, grp, ["in"], [w_in], sems)
            _zero(z_tail)
            _zero(x_tail)
            _zero(h_carry)

        def project_and_mix():
            xf = x_ref[...]
            rstd = lax.rsqrt(jnp.mean(xf * xf, axis=-1, keepdims=True) + RMS_EPS)
            h1 = (xf * rstd * g_ref[...]).astype(BF16)
            h1_ref[...] = h1
            for k in range(N_PROJ):
                p_ref[:, k * d:(k + 1) * d] = _dot_nt(h1, w_in[k * d:(k + 1) * d, :]).astype(BF16)
            row = lax.broadcasted_iota(jnp.int32, (SUBLANE, cw), 0)
            for j in range(nb):
                cs = slice(j * cw, (j + 1) * cw)
                ba_ref, ca_ref, xa_ref, xb_ref, gb_ref = (p_ref.at[:, k * d:(k + 1) * d] for k in range(5))
                z = ca_ref[:, cs].astype(F32) * xa_ref[:, cs].astype(F32)
                _causal_fill(zbuf, z, z_tail[:, cs], CONV_A_K - 1, row)
                z_tail[:, cs] = z[tm - (CONV_A_K - 1) * SUBLANE:, :]
                cz = caw_ref[0:1, cs] * zbuf[0:tm, :] + caw_ref[1:2, cs] * zbuf[SUBLANE:SUBLANE + tm, :] + caw_ref[2:3, cs] * z
                ya_ref[:, cs] = (ba_ref[:, cs].astype(F32) * cz).astype(BF16)
                xb = xb_ref[:, cs].astype(F32)
                _causal_fill(xbuf, xb, x_tail[:, cs], CONV_B_K - 1, row)
                x_tail[:, cs] = xb[tm - (CONV_B_K - 1) * SUBLANE:, :]
                u = (cbw_ref[0:1, cs] * xbuf[0:tm, :] + cbw_ref[1:2, cs] * xbuf[SUBLANE:SUBLANE + tm, :]
                     + cbw_ref[2:3, cs] * xbuf[2 * SUBLANE:2 * SUBLANE + tm, :] + cbw_ref[3:4, cs] * xb + vec_ref[0:1, cs])
                ub = u.astype(BF16)
                u = ub.astype(F32)
                _, gi, a, s, _ = _lru_gates(ub, bda_ref[j], bdx_ref[j], vec_ref[1:2, cs], vec_ref[2:3, cs], vec_ref[3:4, cs])
                abuf[...] = a
                bbuf[...] = s * (gi * u)
                h_end, a_prod = _chain_scan(abuf, bbuf, nk, reverse=False)
                a_inc, h_inc = _sublane_scan(a_prod, h_end, row, reverse=False)
                carry = h_carry[:, cs]
                ends = h_inc + a_inc * carry
                starts = jnp.where(row == 0, carry, pltpu.roll(ends, 1, 0))
                h_carry[:, cs] = jnp.broadcast_to(ends[SUBLANE - 1:SUBLANE, :], (SUBLANE, cw))
                h = (bbuf[...].reshape(nk, SUBLANE, cw) + abuf[...].reshape(nk, SUBLANE, cw) * starts[None]).reshape(tm, cw)
                gel, _ = _gelu_and_grad(gb_ref[:, cs].astype(F32))
                yb_ref[:, cs] = (h * gel).astype(BF16)
                u_ref[:, cs] = ub
                h_ref[:, cs] = h.astype(BF16)

        project_and_mix()

    small = pl.BlockSpec((SUBLANE, d), lambda i: (0, 0))
    bd = pl.BlockSpec((nb, cw, cw), lambda i: (0, 0, 0))
    row_spec = pl.BlockSpec((tm, d), lambda i: (i, 0))
    return _hosted_call(
        body, comms, name=name, grid=(t // tm,),
        out_shape=[jax.ShapeDtypeStruct((t, n_in), BF16)] + [jax.ShapeDtypeStruct((t, d), BF16)] * 5,
        in_specs=[row_spec, pl.BlockSpec((1, d), lambda i: (0, 0)), ANY, small, small, small, bd, bd],
        out_specs=[pl.BlockSpec((tm, n_in), lambda i: (i, 0))] + [row_spec] * 5,
        scratch_shapes=[pltpu.VMEM((n_in, d), BF16), pltpu.SemaphoreType.DMA((N_DEV,)),
                        pltpu.VMEM((tm + (CONV_A_K - 1) * SUBLANE, cw), F32), pltpu.VMEM((tm + (CONV_B_K - 1) * SUBLANE, cw), F32),
                        pltpu.VMEM((tm, cw), F32), pltpu.VMEM((tm, cw), F32),
                        pltpu.VMEM(((CONV_A_K - 1) * SUBLANE, d), F32), pltpu.VMEM(((CONV_B_K - 1) * SUBLANE, d), F32),
                        pltpu.VMEM((SUBLANE, d), F32)],
        args=(x, g_row, gath, caw, cbw, vec, bda, bdx))


def _merge_fwd(x, ya, yb, p, gbias, gath, grp, name, comms=()):
    t, d = x.shape
    tm = _row_tile(t, 512)

    def body(x_ref, ya_ref, yb_ref, ga_ref, gb_ref, gbias_ref, gath_ref, x1_ref, oa_ref, ob_ref, w_oa, w_ob, w_o, sems):
        @pl.when(pl.program_id(0) == 0)
        def _():
            _load_weights(gath_ref, grp, ["oa", "ob", "o"], [w_oa, w_ob, w_o], sems)

        oa = _dot_nn(ya_ref[...], w_oa[...]).astype(BF16)
        ob = _dot_nn(yb_ref[...], w_ob[...]).astype(BF16)
        oa_ref[...] = oa
        ob_ref[...] = ob
        sa = _sigmoid(ga_ref[...] + gbias_ref[0:1, :].astype(BF16))
        sb = _sigmoid(gb_ref[...] + gbias_ref[1:2, :].astype(BF16))
        x1_ref[...] = x_ref[...] + _dot_nn(sa * oa + sb * ob, w_o[...])

    row = pl.BlockSpec((tm, d), lambda i: (i, 0))
    return _hosted_call(
        body, comms, name=name, grid=(t // tm,),
        out_shape=[jax.ShapeDtypeStruct((t, d), F32), jax.ShapeDtypeStruct((t, d), BF16), jax.ShapeDtypeStruct((t, d), BF16)],
        in_specs=[row, row, row, pl.BlockSpec((tm, d), lambda i: (i, 5)), pl.BlockSpec((tm, d), lambda i: (i, 6)),
                  pl.BlockSpec((SUBLANE, d), lambda i: (0, 0)), ANY],
        out_specs=[row, row, row],
        scratch_shapes=[pltpu.VMEM((d, d), BF16)] * 3 + [pltpu.SemaphoreType.DMA((3 * N_DEV,))],
        args=(x, ya, yb, p, p, gbias, gath))


def _ffn_fwd(x1, g_row, gath, grp, name, comms=()):
    t, d = x1.shape
    ff = grp.rows["g"] * N_DEV
    tm = _row_tile(t, 512)
    fc = MXU_TILE
    assert ff % fc == 0

    def body(x_ref, g_ref, gath_ref, x2_ref, gg_ref, uu_ref, w_g, w_u, w_d, acc, sems):
        @pl.when(pl.program_id(0) == 0)
        def _():
            _load_weights(gath_ref, grp, ["g", "u", "d"], [w_g, w_u, w_d], sems)

        xf = x_ref[...]
        rstd = lax.rsqrt(jnp.mean(xf * xf, axis=-1, keepdims=True) + RMS_EPS)
        h = (xf * rstd * g_ref[...]).astype(BF16)
        acc[...] = xf
        for c in range(ff // fc):
            fs = slice(c * fc, (c + 1) * fc)
            gg = _dot_nt(h, w_g[fs, :]).astype(BF16)
            uu = _dot_nt(h, w_u[fs, :]).astype(BF16)
            gg_ref[:, fs] = gg
            uu_ref[:, fs] = uu
            acc[...] += _dot_nn(gg * _sigmoid(gg) * uu, w_d[fs, :])
        x2_ref[...] = acc[...]

    row = pl.BlockSpec((tm, d), lambda i: (i, 0))
    wide = pl.BlockSpec((tm, ff), lambda i: (i, 0))
    return _hosted_call(
        body, comms, name=name, grid=(t // tm,),
        out_shape=[jax.ShapeDtypeStruct((t, d), F32), jax.ShapeDtypeStruct((t, ff), BF16), jax.ShapeDtypeStruct((t, ff), BF16)],
        in_specs=[row, pl.BlockSpec((1, d), lambda i: (0, 0)), ANY],
        out_specs=[row, wide, wide],
        scratch_shapes=[pltpu.VMEM((ff, d), BF16)] * 3 + [pltpu.VMEM((tm, d), F32), pltpu.SemaphoreType.DMA((3 * N_DEV,))],
        args=(x1, g_row, gath))


def _loss_head(x, g_row, target, name):
    t, d = x.shape
    tm = _row_tile(t, 512)

    def body(x_ref, g_ref, tgt_ref, loss_ref, dx_ref, dg_ref):
        @pl.when(pl.program_id(0) == 0)
        def _():
            _zero(loss_ref)
            _zero(dg_ref)

        xf = x_ref[...]
        rstd = lax.rsqrt(jnp.mean(xf * xf, axis=-1, keepdims=True) + RMS_EPS)
        xh = xf * rstd
        g = g_ref[...]
        err = xh * g - tgt_ref[...]
        loss_ref[...] += 0.5 * jnp.sum(jnp.sum(err * err, axis=-1, keepdims=True), axis=0, keepdims=True) * (1.0 / d)
        dy = err * (1.0 / d)
        dg_ref[0:1, :] += jnp.sum(dy * xh, axis=0, keepdims=True)
        dxh = dy * g
        dx_ref[...] = rstd * (dxh - xh * jnp.mean(dxh * xh, axis=-1, keepdims=True))

    row = pl.BlockSpec((tm, d), lambda i: (i, 0))
    return pl.pallas_call(
        body, name=name, grid=(t // tm,),
        out_shape=[jax.ShapeDtypeStruct((SUBLANE, LANE), F32), jax.ShapeDtypeStruct((t, d), F32),
                   jax.ShapeDtypeStruct((SUBLANE, d), F32)],
        in_specs=[row, pl.BlockSpec((1, d), lambda i: (0, 0)), row],
        out_specs=[pl.BlockSpec((SUBLANE, LANE), lambda i: (0, 0)), row, pl.BlockSpec((SUBLANE, d), lambda i: (0, 0))],
        compiler_params=_params(),
    )(x, g_row, target)


def _ffn_bwd_act(dx2, gg, uu, gath, grp, name, comms=()):
    t, d = dx2.shape
    ff = grp.rows["g"] * N_DEV
    tm = _row_tile(t, 512)
    fc = MXU_TILE
    n_t = t // tm

    def body(dx2_ref, gg_ref, uu_ref, gath_ref, dgg_ref, duu_ref, dwd_ref, w_d, acc, sems):
        @pl.when(pl.program_id(0) == 0)
        def _():
            _load_weights(gath_ref, grp, ["d"], [w_d], sems)
            _zero(acc)

        dx2b = dx2_ref[...].astype(BF16)
        for c in range(ff // fc):
            fs = slice(c * fc, (c + 1) * fc)
            df = _dot_nt(dx2b, w_d[fs, :]).astype(BF16)
            g = gg_ref[:, fs]
            u = uu_ref[:, fs]
            sg = _sigmoid(g)
            silu = g * sg
            acc[fs, :] += _dot_tn(silu * u, dx2b)
            duu_ref[:, fs] = df * silu
            dgg_ref[:, fs] = df * u * (sg * (1.0 + g * (1.0 - sg)))

        @pl.when(pl.program_id(0) == n_t - 1)
        def _():
            w_d[...] = acc[...].astype(BF16)
            out = pltpu.make_async_copy(w_d, dwd_ref, sems.at[0])
            out.start()
            out.wait()

    row = pl.BlockSpec((tm, d), lambda i: (i, 0))
    wide = pl.BlockSpec((tm, ff), lambda i: (i, 0))
    sd = jax.ShapeDtypeStruct
    return _hosted_call(
        body, comms, name=name, grid=(n_t,),
        out_shape=[sd((t, ff), BF16), sd((t, ff), BF16), sd((ff, d), BF16)],
        in_specs=[row, wide, wide, ANY],
        out_specs=[wide, wide, ANY],
        scratch_shapes=[pltpu.VMEM((ff, d), BF16), pltpu.VMEM((ff, d), F32), pltpu.SemaphoreType.DMA((N_DEV,))],
        args=(dx2, gg, uu, gath))


def _ffn_bwd_in(dgg, duu, dx2, x1, g_row, gath, grp, name, comms=()):
    t, d = x1.shape
    ff = grp.rows["g"] * N_DEV
    tm = _row_tile(t, 512)

    def body(dgg_ref, duu_ref, dx2_ref, x_ref, g_ref, gath_ref, dx1_ref, dx1b_ref, h_ref, dg_ref, w_g, w_u, sems):
        @pl.when(pl.program_id(0) == 0)
        def _():
            _load_weights(gath_ref, grp, ["g", "u"], [w_g, w_u], sems)
            _zero(dg_ref)

        dh = _dot_nn(dgg_ref[...], w_g[...]) + _dot_nn(duu_ref[...], w_u[...])
        xf = x_ref[...]
        rstd = lax.rsqrt(jnp.mean(xf * xf, axis=-1, keepdims=True) + RMS_EPS)
        xh = xf * rstd
        g = g_ref[...]
        h_ref[...] = (xh * g).astype(BF16)
        dg_ref[0:1, :] += jnp.sum(dh * xh, axis=0, keepdims=True)
        dxh = dh * g
        dx1 = dx2_ref[...] + rstd * (dxh - xh * jnp.mean(dxh * xh, axis=-1, keepdims=True))
        dx1_ref[...] = dx1
        dx1b_ref[...] = dx1.astype(BF16)

    row = pl.BlockSpec((tm, d), lambda i: (i, 0))
    wide = pl.BlockSpec((tm, ff), lambda i: (i, 0))
    sd = jax.ShapeDtypeStruct
    return _hosted_call(
        body, comms, name=name, grid=(t // tm,),
        out_shape=[sd((t, d), F32), sd((t, d), BF16), sd((t, d), BF16), sd((SUBLANE, d), F32)],
        in_specs=[wide, wide, row, row, pl.BlockSpec((1, d), lambda i: (0, 0)), ANY],
        out_specs=[row, row, row, pl.BlockSpec((SUBLANE, d), lambda i: (0, 0))],
        scratch_shapes=[pltpu.VMEM((ff, d), BF16)] * 2 + [pltpu.SemaphoreType.DMA((2 * N_DEV,))],
        args=(dgg, duu, dx2, x1, g_row, gath))


def _merge_bwd(dx1b, oa, ob, ya, yb, p, gbias, gath, grp, name, comms=()):
    t, d = oa.shape
    tm = _row_tile(t, 512)
    n_t = t // tm

    def body(dx_ref, oa_ref, ob_ref, ya_ref, yb_ref, ga_ref, gb_ref, gbias_ref, gath_ref,
             dya_ref, dyb_ref, dp_ref, dgb_ref, dwoa_ref, dwob_ref, dwo_ref,
             w_oa, w_ob, w_o, acc_oa, acc_ob, acc_o, stage, sems, out_sems):
        @pl.when(pl.program_id(0) == 0)
        def _():
            _load_weights(gath_ref, grp, ["oa", "ob", "o"], [w_oa, w_ob, w_o], sems)
            for ref in (dgb_ref, acc_oa, acc_ob, acc_o):
                _zero(ref)

        dxb = dx_ref[...]
        dm = _dot_nt(dxb, w_o[...]).astype(BF16)
        oa = oa_ref[...]
        ob = ob_ref[...]
        sa = _sigmoid(ga_ref[...] + gbias_ref[0:1, :].astype(BF16))
        sb = _sigmoid(gb_ref[...] + gbias_ref[1:2, :].astype(BF16))
        acc_o[...] += _dot_tn(sa * oa + sb * ob, dxb)
        doa = dm * sa
        dob = dm * sb
        acc_oa[...] += _dot_tn(ya_ref[...], doa)
        acc_ob[...] += _dot_tn(yb_ref[...], dob)
        dga = dm * oa * sa * (1.0 - sa)
        dgb = dm * ob * sb * (1.0 - sb)
        step = pl.program_id(0)
        slot = step % 2

        def to_dp(k, at_step):
            return pltpu.make_async_copy(stage.at[k], dp_ref.at[pl.ds(at_step * tm, tm), pl.ds(5 * d, 2 * d)], out_sems.at[k])

        @pl.when(step >= 2)
        def _():
            to_dp(slot, step - 2).wait()

        stage[slot, :, 0:d] = dga
        stage[slot, :, d:2 * d] = dgb
        to_dp(slot, step).start()
        ones = jnp.ones((SUBLANE, tm), BF16)
        dgb_ref[0:1, :] += _dot_nn(ones, dga)[0:1, :]
        dgb_ref[1:2, :] += _dot_nn(ones, dgb)[0:1, :]
        dya_ref[...] = _dot_nt(doa, w_oa[...]).astype(BF16)
        dyb_ref[...] = _dot_nt(dob, w_ob[...]).astype(BF16)

        @pl.when(pl.program_id(0) == n_t - 1)
        def _():
            outs = []
            for n, (acc, stage, dst) in enumerate(((acc_oa, w_oa, dwoa_ref), (acc_ob, w_ob, dwob_ref), (acc_o, w_o, dwo_ref))):
                stage[...] = acc[...].astype(BF16)
                outs.append(pltpu.make_async_copy(stage, dst, sems.at[n]))
                outs[-1].start()
            for cp in outs:
                cp.wait()
            for back in range(min(2, n_t)):
                to_dp((n_t - 1 - back) % 2, n_t - 1 - back).wait()

    row = pl.BlockSpec((tm, d), lambda i: (i, 0))
    sd = jax.ShapeDtypeStruct
    return _hosted_call(
        body, comms, name=name, grid=(n_t,),
        out_shape=[sd((t, d), BF16), sd((t, d), BF16), sd((t, N_PROJ * d), BF16), sd((SUBLANE, d), F32),
                   sd((d, d), BF16), sd((d, d), BF16), sd((d, d), BF16)],
        in_specs=[row, row, row, row, row, pl.BlockSpec((tm, d), lambda i: (i, 5)), pl.BlockSpec((tm, d), lambda i: (i, 6)),
                  pl.BlockSpec((SUBLANE, d), lambda i: (0, 0)), ANY],
        out_specs=[row, row, ANY, pl.BlockSpec((SUBLANE, d), lambda i: (0, 0)), ANY, ANY, ANY],
        scratch_shapes=[pltpu.VMEM((d, d), BF16)] * 3 + [pltpu.VMEM((d, d), F32)] * 3
        + [pltpu.VMEM((2, tm, 2 * d), BF16), pltpu.SemaphoreType.DMA((3 * N_DEV,)), pltpu.SemaphoreType.DMA((2,))],
        args=(dx1b, oa, ob, ya, yb, p, p, gbias, gath))


DV_CONV_B_B, DV_BA, DV_BX, DV_SP, DV_CONV_A, DV_CONV_B = 0, 1, 2, 3, 4, 7
DV_ROWS = 16


def _mixer_bwd(dya, dyb, dp_gates, p, u_s, h_s, caw, cbw, vec, bda, bdx, name, comms=()):
    t, d = dya.shape
    tm = _time_tile(t)
    n_t = t // tm
    nk = tm // SUBLANE
    cw = min(MXU_TILE, d)
    nb = d // cw
    halo = 4 * SUBLANE
    ka, kb = CONV_A_K - 1, CONV_B_K - 1

    def body(dya_ref, dyb_ref, _, ba_ref, ca_ref, xa_ref, xb_ref, gb_ref, cah_ref, xah_ref, xbh_ref,
             u_ref, h_ref, hh_ref, caw_ref, cbw_ref, vec_ref, bda_ref, bdx_ref,
             dp_ref, dv_ref, dwa_ref, dwx_ref,
             zbuf, xbuf, hbuf, dczbuf, dubuf, a2buf, a1buf, lbuf, dcz_head, du_head, a_head, lam_head):
        i = pl.program_id(0)

        @pl.when(i == 0)
        def _():
            for ref in (dv_ref, dwa_ref, dwx_ref, dcz_head, du_head, a_head, lam_head):
                _zero(ref)

        has_prev = jnp.where(i < n_t - 1, 1.0, 0.0).astype(F32)
        row = lax.broadcasted_iota(jnp.int32, (SUBLANE, cw), 0)

        def colsum(v):
            return jnp.sum(v, axis=0, keepdims=True)

        for j in range(nb):
            cs = slice(j * cw, (j + 1) * cw)
            ca = ca_ref[:, cs].astype(F32)
            xa = xa_ref[:, cs].astype(F32)
            z = ca * xa
            z_before = cah_ref[:, cs].astype(F32) * xah_ref[:, cs].astype(F32) * has_prev
            _causal_fill(zbuf, z, z_before[halo - ka * SUBLANE:, :], ka, row)
            z2 = zbuf[0:tm, :]
            z1 = zbuf[SUBLANE:SUBLANE + tm, :]
            w0, w1, w2 = caw_ref[0:1, cs], caw_ref[1:2, cs], caw_ref[2:3, cs]
            cz = w0 * z2 + w1 * z1 + w2 * z
            dya = dya_ref[:, cs].astype(F32)
            dp_ref[:, 0 * d + j * cw:0 * d + (j + 1) * cw] = (dya * cz).astype(BF16)
            dcz = dya * ba_ref[:, cs].astype(F32)
            _anticausal_fill(dczbuf, dcz, dcz_head[:, cs], ka, row)
            dcz_head[:, cs] = dcz[0:ka * SUBLANE, :]
            dz = w2 * dcz + w1 * dczbuf[SUBLANE:SUBLANE + tm, :] + w0 * dczbuf[2 * SUBLANE:2 * SUBLANE + tm, :]
            dv_ref[DV_CONV_A + 0:DV_CONV_A + 1, cs] += colsum(dcz * z2)
            dv_ref[DV_CONV_A + 1:DV_CONV_A + 2, cs] += colsum(dcz * z1)
            dv_ref[DV_CONV_A + 2:DV_CONV_A + 3, cs] += colsum(dcz * z)
            dp_ref[:, 1 * d + j * cw:1 * d + (j + 1) * cw] = (dz * xa).astype(BF16)
            dp_ref[:, 2 * d + j * cw:2 * d + (j + 1) * cw] = (dz * ca).astype(BF16)
            h = h_ref[:, cs].astype(F32)
            h_before = hh_ref[:, cs].astype(F32) * has_prev
            _causal_fill(hbuf, h, h_before[halo - SUBLANE:, :], 1, row)
            h_prev = hbuf[0:tm, :]
            dyb = dyb_ref[:, cs].astype(F32)
            gel, dgel = _gelu_and_grad(gb_ref[:, cs].astype(F32))
            dp_ref[:, 4 * d + j * cw:4 * d + (j + 1) * cw] = (dyb * h * dgel).astype(BF16)
            ub = u_ref[:, cs]
            u = ub.astype(F32)
            sp = vec_ref[3:4, cs]
            r, gi, a, s, inv_s = _lru_gates(ub, bda_ref[j], bdx_ref[j], vec_ref[1:2, cs], vec_ref[2:3, cs], sp)
            _anticausal_fill(a2buf, a, a_head[:, cs], 1, row)
            a_head[:, cs] = a[0:SUBLANE, :]
            a1buf[...] = a2buf[SUBLANE:SUBLANE + tm, :]
            lbuf[...] = dyb * gel
            l_end, a_prod = _chain_scan(a1buf, lbuf, nk, reverse=True)
            a_inc, l_inc = _sublane_scan(a_prod, l_end, row, reverse=True)
            carry = lam_head[:, cs]
            ends = l_inc + a_inc * carry
            starts = jnp.where(row == SUBLANE - 1, carry, pltpu.roll(ends, SUBLANE - 1, 0))
            lam_head[:, cs] = jnp.broadcast_to(ends[0:1, :], (SUBLANE, cw))
            lam = (lbuf[...].reshape(nk, SUBLANE, cw) + a1buf[...].reshape(nk, SUBLANE, cw) * starts[None]).reshape(tm, cw)
            da = lam * h_prev
            iu = gi * u
            ds = lam * iu
            di = lam * s * u
            du = lam * s * gi
            dlog_a = da * a - ds * (a * a) * inv_s
            dv_ref[DV_SP:DV_SP + 1, cs] += colsum(dlog_a * r) * (-LRU_C)
            dpr = dlog_a * ((-LRU_C) * sp) * r * (1.0 - r)
            dpi = di * gi * (1.0 - gi)
            dv_ref[DV_BA:DV_BA + 1, cs] += colsum(dpr)
            dv_ref[DV_BX:DV_BX + 1, cs] += colsum(dpi)
            dprb = dpr.astype(BF16)
            dpib = dpi.astype(BF16)
            du = du + _dot_nt(dprb, bda_ref[j]) + _dot_nt(dpib, bdx_ref[j])
            dwa_ref[j] += _dot_tn(ub, dprb)
            dwx_ref[j] += _dot_tn(ub, dpib)
            xb = xb_ref[:, cs].astype(F32)
            x_before = xbh_ref[:, cs].astype(F32) * has_prev
            _causal_fill(xbuf, xb, x_before[halo - kb * SUBLANE:, :], kb, row)
            _anticausal_fill(dubuf, du, du_head[:, cs], kb, row)
            du_head[:, cs] = du[0:kb * SUBLANE, :]
            v0, v1, v2, v3 = cbw_ref[0:1, cs], cbw_ref[1:2, cs], cbw_ref[2:3, cs], cbw_ref[3:4, cs]
            dxb = (v3 * du + v2 * dubuf[SUBLANE:SUBLANE + tm, :] + v1 * dubuf[2 * SUBLANE:2 * SUBLANE + tm, :]
                   + v0 * dubuf[3 * SUBLANE:3 * SUBLANE + tm, :])
            dp_ref[:, 3 * d + j * cw:3 * d + (j + 1) * cw] = dxb.astype(BF16)
            dv_ref[DV_CONV_B_B:DV_CONV_B_B + 1, cs] += colsum(du)
            dv_ref[DV_CONV_B + 0:DV_CONV_B + 1, cs] += colsum(du * xbuf[0:tm, :])
            dv_ref[DV_CONV_B + 1:DV_CONV_B + 2, cs] += colsum(du * xbuf[SUBLANE:SUBLANE + tm, :])
            dv_ref[DV_CONV_B + 2:DV_CONV_B + 3, cs] += colsum(du * xbuf[2 * SUBLANE:2 * SUBLANE + tm, :])
            dv_ref[DV_CONV_B + 3:DV_CONV_B + 4, cs] += colsum(du * xb)

    rt = lambda i: n_t - 1 - i
    row_spec = pl.BlockSpec((tm, d), lambda i: (rt(i), 0))
    slab = lambda s: pl.BlockSpec((tm, d), lambda i, s=s: (rt(i), s))
    before = lambda s: pl.BlockSpec((halo, d), lambda i, s=s: (jnp.maximum(rt(i) * (tm // halo) - 1, 0), s))
    small = pl.BlockSpec((SUBLANE, d), lambda i: (0, 0))
    bd = pl.BlockSpec((nb, cw, cw), lambda i: (0, 0, 0))
    sd = jax.ShapeDtypeStruct
    wbuf = lambda n: pltpu.VMEM((tm + n * SUBLANE, cw), F32)
    head = lambda n: pltpu.VMEM((n * SUBLANE, d), F32)
    return _hosted_call(
        body, comms, name=name, grid=(n_t,),
        out_shape=[sd((t, N_PROJ * d), BF16), sd((DV_ROWS, d), F32), sd((nb, cw, cw), F32), sd((nb, cw, cw), F32)],
        in_specs=[row_spec, row_spec, ANY,
                  slab(0), slab(1), slab(2), slab(3), slab(4), before(1), before(2), before(3),
                  row_spec, row_spec, before(0), small, small, small, bd, bd],
        out_specs=[pl.BlockSpec((tm, 5 * d), lambda i: (rt(i), 0)), pl.BlockSpec((DV_ROWS, d), lambda i: (0, 0)), bd, bd],
        aliases={2: 0},
        scratch_shapes=[wbuf(ka), wbuf(kb), wbuf(1), wbuf(ka), wbuf(kb), wbuf(1),
                        pltpu.VMEM((tm, cw), F32), pltpu.VMEM((tm, cw), F32), head(ka), head(kb), head(1), head(1)],
        args=(dya, dyb, dp_gates, p, p, p, p, p, p, p, p, u_s, h_s, h_s, caw, cbw, vec, bda, bdx))


def _in_proj_bwd(dp, x, dx1, g_row, gath, grp, name, comms=()):
    t, d = x.shape
    tm = _row_tile(t, 512)
    n_in = N_PROJ * d

    def body(dp_ref, x_ref, dx1_ref, g_ref, gath_ref, dx_ref, dg_ref, w_in, sems):
        @pl.when(pl.program_id(0) == 0)
        def _():
            _load_weights(gath_ref, grp, ["in"], [w_in], sems)
            _zero(dg_ref)

        dh = _dot_nn(dp_ref[:, 0:d], w_in[0:d, :])
        for k in range(1, N_PROJ):
            dh = dh + _dot_nn(dp_ref[:, k * d:(k + 1) * d], w_in[k * d:(k + 1) * d, :])
        xf = x_ref[...]
        rstd = lax.rsqrt(jnp.mean(xf * xf, axis=-1, keepdims=True) + RMS_EPS)
        xh = xf * rstd
        g = g_ref[...]
        dg_ref[0:1, :] += jnp.sum(dh * xh, axis=0, keepdims=True)
        dxh = dh * g
        dx_ref[...] = dx1_ref[...] + rstd * (dxh - xh * jnp.mean(dxh * xh, axis=-1, keepdims=True))

    row = pl.BlockSpec((tm, d), lambda i: (i, 0))
    sd = jax.ShapeDtypeStruct
    return _hosted_call(
        body, comms, name=name, grid=(t // tm,),
        out_shape=[sd((t, d), F32), sd((SUBLANE, d), F32)],
        in_specs=[pl.BlockSpec((tm, n_in), lambda i: (i, 0)), row, row, pl.BlockSpec((1, d), lambda i: (0, 0)), ANY],
        out_specs=[row, pl.BlockSpec((SUBLANE, d), lambda i: (0, 0))],
        scratch_shapes=[pltpu.VMEM((n_in, d), BF16), pltpu.SemaphoreType.DMA((N_DEV,))],
        args=(dp, x, dx1, g_row, gath))


def _weight_grad(a, b, name):
    t, m = a.shape
    n = b.shape[1]
    bm = m
    for div in (1, 2, 4, 8):
        if m % div == 0 and (m // div) % LANE == 0 and (m // div) * n * 4 <= (8 << 20):
            bm = m // div
            break
    fixed = bm * n * (4 + 2 * 2)
    bt = next(_row_tile(t, want) for want in (2048, 1024, 512)
              if 2 * 2 * _row_tile(t, want) * (bm + n) + fixed <= VMEM_LIMIT - (12 << 20))
    n_t = t // bt

    def body(a_ref, b_ref, o_ref, acc):
        k = pl.program_id(1)

        @pl.when(k == 0)
        def _():
            _zero(acc)

        acc[...] += _dot_tn(a_ref[...], b_ref[...])

        @pl.when(k == n_t - 1)
        def _():
            o_ref[...] = acc[...].astype(BF16)

    return pl.pallas_call(
        body, name=name, grid=(m // bm, n_t),
        out_shape=jax.ShapeDtypeStruct((m, n), BF16),
        in_specs=[pl.BlockSpec((bt, bm), lambda i, k: (k, i)), pl.BlockSpec((bt, n), lambda i, k: (k, 0))],
        out_specs=pl.BlockSpec((bm, n), lambda i, k: (i, 0)),
        scratch_shapes=[pltpu.VMEM((bm, n), F32)],
        compiler_params=_params(2),
    )(a, b)


def _adamw(w, g, m, v, name):
    r, c = w.shape
    tr = _fit_rows(r, c * 4)
    c1 = 1.0 - ADAM_B1 ** ADAM_STEP
    c2 = 1.0 - ADAM_B2 ** ADAM_STEP

    def body(w_ref, g_ref, m_ref, v_ref, d_ref, nm_ref, nv_ref):
        g32 = g_ref[...]
        nm = ADAM_B1 * m_ref[...] + (1.0 - ADAM_B1) * g32
        nv = ADAM_B2 * v_ref[...] + (1.0 - ADAM_B2) * (g32 * g32)
        nm_ref[...] = nm
        nv_ref[...] = nv
        d_ref[...] = -ADAM_LR * ((nm / c1) / (jnp.sqrt(nv / c2) + ADAM_EPS) + ADAM_WD * w_ref[...])

    spec = pl.BlockSpec((tr, c), lambda i: (i, 0))
    return pl.pallas_call(
        body, name=name, grid=(r // tr,),
        out_shape=[jax.ShapeDtypeStruct((r, c), F32)] * 3,
        in_specs=[spec] * 4, out_specs=[spec] * 3,
        compiler_params=_params(),
    )(w, g, m, v)


def _pad_rows(a, mult=SUBLANE):
    pad = (-a.shape[0]) % mult
    return a if pad == 0 else jnp.concatenate([a, jnp.zeros((pad,) + a.shape[1:], a.dtype)], axis=0)


REPLICATED = ("ln1_g", "conv_b_b", "lru_wa", "lru_ba", "lru_wx", "lru_bx", "lru_lambda", "ln2_g", "final_g")
SMALL_SHARDED = ("conv_a_w", "conv_b_w", "gate_bias")
MATRICES = ("w_in", "w_out_a", "w_out_b", "w_o", "w_ffn_gate", "w_ffn_up", "w_ffn_down")
ORDER = ("ln1_g", "w_in", "conv_a_w", "conv_b_w", "conv_b_b", "lru_wa", "lru_ba", "lru_wx", "lru_bx", "lru_lambda",
         "w_out_a", "w_out_b", "gate_bias", "w_o", "ln2_g", "w_ffn_gate", "w_ffn_up", "w_ffn_down", "final_g")


def kernel(x, ln1_g, w_in, conv_a_w, conv_b_w, conv_b_b, lru_wa, lru_ba, lru_wx, lru_bx, lru_lambda, w_out_a, w_out_b, gate_bias, w_o, ln2_g, w_ffn_gate, w_ffn_up, w_ffn_down, final_g, loss_target, m_ln1_g, m_w_in, m_conv_a_w, m_conv_b_w, m_conv_b_b, m_lru_wa, m_lru_ba, m_lru_wx, m_lru_bx, m_lru_lambda, m_w_out_a, m_w_out_b, m_gate_bias, m_w_o, m_ln2_g, m_w_ffn_gate, m_w_ffn_up, m_w_ffn_down, m_final_g, v_ln1_g, v_w_in, v_conv_a_w, v_conv_b_w, v_conv_b_b, v_lru_wa, v_lru_ba, v_lru_wx, v_lru_bx, v_lru_lambda, v_w_out_a, v_w_out_b, v_gate_bias, v_w_o, v_ln2_g, v_w_ffn_gate, v_w_ffn_up, v_w_ffn_down, v_final_g):
    w = dict(ln1_g=ln1_g, w_in=w_in, conv_a_w=conv_a_w, conv_b_w=conv_b_w, conv_b_b=conv_b_b, lru_wa=lru_wa,
             lru_ba=lru_ba, lru_wx=lru_wx, lru_bx=lru_bx, lru_lambda=lru_lambda, w_out_a=w_out_a, w_out_b=w_out_b,
             gate_bias=gate_bias, w_o=w_o, ln2_g=ln2_g, w_ffn_gate=w_ffn_gate, w_ffn_up=w_ffn_up,
             w_ffn_down=w_ffn_down, final_g=final_g)
    mom = dict(ln1_g=m_ln1_g, w_in=m_w_in, conv_a_w=m_conv_a_w, conv_b_w=m_conv_b_w, conv_b_b=m_conv_b_b,
               lru_wa=m_lru_wa, lru_ba=m_lru_ba, lru_wx=m_lru_wx, lru_bx=m_lru_bx, lru_lambda=m_lru_lambda,
               w_out_a=m_w_out_a, w_out_b=m_w_out_b, gate_bias=m_gate_bias, w_o=m_w_o, ln2_g=m_ln2_g,
               w_ffn_gate=m_w_ffn_gate, w_ffn_up=m_w_ffn_up, w_ffn_down=m_w_ffn_down, final_g=m_final_g)
    var = dict(ln1_g=v_ln1_g, w_in=v_w_in, conv_a_w=v_conv_a_w, conv_b_w=v_conv_b_w, conv_b_b=v_conv_b_b,
               lru_wa=v_lru_wa, lru_ba=v_lru_ba, lru_wx=v_lru_wx, lru_bx=v_lru_bx, lru_lambda=v_lru_lambda,
               w_out_a=v_w_out_a, w_out_b=v_w_out_b, gate_bias=v_gate_bias, w_o=v_w_o, ln2_g=v_ln2_g,
               w_ffn_gate=v_w_ffn_gate, w_ffn_up=v_w_ffn_up, w_ffn_down=v_w_ffn_down, final_g=v_final_g)

    _, t, d = x.shape
    n_layers = w_in.shape[0]
    ff = w_ffn_down.shape[1] * N_DEV
    dd = d // N_DEV
    hd = d // LRU_HEADS
    cw = min(MXU_TILE, d)
    nb = d // cw
    hpt = cw // hd
    grp = _groups(d, ff)
    me = 4 * lax.axis_index("x") + 2 * lax.axis_index("y") + lax.axis_index("c")
    tm_time = _time_tile(t)
    x0 = _to_tile_order(x[0], tm_time)
    target = _to_tile_order(loss_target[0], tm_time)

    packed = [{"in": jnp.swapaxes(w_in[l], 0, 1).astype(BF16),
               "rest": jnp.concatenate([w_out_a[l], w_out_b[l], w_o[l], jnp.swapaxes(w_ffn_gate[l], 0, 1),
                                        jnp.swapaxes(w_ffn_up[l], 0, 1), w_ffn_down[l]], axis=0).astype(BF16)}
              for l in range(n_layers)]
    n_small = CONV_A_K + CONV_B_K + 2
    small = _pad_rows(jnp.concatenate([conv_a_w, conv_b_w, gate_bias], axis=1).reshape(n_layers * n_small, dd))
    sp = jax.nn.softplus(-lru_lambda)
    vec = [_pad_rows(jnp.stack([conv_b_b[l], lru_ba[l], lru_bx[l], sp[l]])) for l in range(n_layers)]
    eye = jnp.eye(hpt, dtype=F32)

    def block_diag(wh):
        return jnp.einsum("jkab,kl->jkalb", wh.reshape(nb, hpt, hd, hd), eye).reshape(nb, cw, cw).astype(BF16)

    bda = [block_diag(lru_wa[l]) for l in range(n_layers)]
    bdx = [block_diag(lru_wx[l]) for l in range(n_layers)]

    gath = [dict() for _ in range(n_layers)]
    (gath[0]["in"],), (small_g,) = _comm_call([_Gather(packed[0]["in"]), _Gather(small)], "gather_in_0")
    small_full = jnp.swapaxes(small_g[:, :n_layers * n_small], 0, 1).reshape(n_layers, n_small, d)
    caw = [_pad_rows(small_full[k, 0:CONV_A_K]) for k in range(n_layers)]
    cbw = [_pad_rows(small_full[k, CONV_A_K:CONV_A_K + CONV_B_K]) for k in range(n_layers)]
    gbias = [_pad_rows(small_full[k, CONV_A_K + CONV_B_K:]) for k in range(n_layers)]
    saved = []
    xl = x0
    for l in range(n_layers):
        more = l + 1 < n_layers
        (p, h1b, ya, yb, *kept), got = _in_proj_mixer_fwd(
            xl, ln1_g[l][None], gath[l]["in"], grp["in"], caw[l], cbw[l], vec[l], bda[l], bdx[l], f"in_proj_mixer_fwd_{l}",
            [_Gather(packed[0]["rest"])] if l == 0 else [])
        if l == 0:
            ((gath[0]["rest"],),) = got
        (x1, oa, ob), got = _merge_fwd(xl, ya, yb, p, gbias[l], gath[l]["rest"], grp["rest"], f"merge_fwd_{l}",
                                       [_Gather(packed[l + 1]["in"])] if more else [])
        if more:
            ((gath[l + 1]["in"],),) = got
        (x2, gg, uu), got = _ffn_fwd(x1, ln2_g[l][None], gath[l]["rest"], grp["rest"], f"ffn_fwd_{l}",
                                     [_Gather(packed[l + 1]["rest"])] if more else [])
        if more:
            ((gath[l + 1]["rest"],),) = got
        saved.append(dict(x=xl, p=p, h1b=h1b, ya=ya, yb=yb, mixer=kept, x1=x1, oa=oa, ob=ob, gg=gg, uu=uu))
        xl = x2
    loss_tile, dx, dfinal = _loss_head(xl, final_g[None], target, "loss_head")
    loss = lax.psum(loss_tile[0, 0], ("x", "y", "c"))

    def heads(dwb):
        blocks = jnp.diagonal(dwb.reshape(nb, hpt, hd, hpt, hd), axis1=1, axis2=3)
        return jnp.moveaxis(blocks, 3, 1).reshape(hd, d)

    layer_names = [n for n in REPLICATED if n != "final_g"] + list(SMALL_SHARDED)

    def layer_block(k):
        return jnp.concatenate([small_grads[k][n] for n in layer_names], axis=0)

    recv = [dict() for _ in range(n_layers)]
    small_grads = [None] * n_layers
    early_all = None
    xg = {"d": _Group(("d",), (ff // N_DEV,)), "gu": _Group(("g", "u"), (ff // N_DEV,) * 2),
          "out": _Group(("oa", "ob", "o"), (dd,) * 3), "in": grp["in"]}
    far_in = None
    for l in reversed(range(n_layers)):
        s = saved[l]
        (dgg, duu, dw_d), got = _ffn_bwd_act(dx, s["gg"], s["uu"], gath[l]["rest"], grp["rest"], f"ffn_bwd_act_{l}",
                                             [far_in] if far_in else [])
        if far_in:
            recv[l + 1]["in"].append(got[0][0])
        (dx1, dx1b, h2b, dln2), got = _ffn_bwd_in(dgg, duu, dx, s["x1"], ln2_g[l][None], gath[l]["rest"], grp["rest"],
                                                  f"ffn_bwd_in_{l}", [_Exchange({"d": dw_d}, xg["d"])])
        recv[l]["d"] = [got[0][0]]
        dw_gu = {"g": _weight_grad(dgg, h2b, f"dw_ffn_gate_{l}"), "u": _weight_grad(duu, h2b, f"dw_ffn_up_{l}")}
        (dya, dyb, dp_gates, dgbias, dw_oa, dw_ob, dw_o), got = _merge_bwd(
            dx1b, s["oa"], s["ob"], s["ya"], s["yb"], s["p"], gbias[l], gath[l]["rest"], grp["rest"], f"merge_bwd_{l}",
            [_Exchange(dw_gu, xg["gu"])])
        recv[l]["gu"] = [got[0][0]]
        comms = [_Exchange({"oa": dw_oa, "ob": dw_ob, "o": dw_o}, xg["out"])]
        if l == 0:
            early = [layer_block(k) for k in range(1, n_layers)] + [_pad_rows(dfinal[0:1])]
            comms.append(_Gather(jnp.concatenate(early, axis=0)))
        (dp, dv, dwa, dwx), got = _mixer_bwd(dya, dyb, dp_gates, s["p"], *s["mixer"], caw[l], cbw[l], vec[l], bda[l], bdx[l],
                                             f"mixer_bwd_{l}", comms)
        recv[l]["out"] = [got[0][0]]
        if l == 0:
            early_all = got[1][0]
        small_grads[l] = {
            "conv_b_b": dv[DV_CONV_B_B:DV_CONV_B_B + 1], "lru_wa": heads(dwa),
            "lru_ba": dv[DV_BA:DV_BA + 1], "lru_wx": heads(dwx), "lru_bx": dv[DV_BX:DV_BX + 1],
            "lru_lambda": dv[DV_SP:DV_SP + 1] * (-jax.nn.sigmoid(-lru_lambda[l]))[None], "ln2_g": dln2[0:1],
            "conv_a_w": dv[DV_CONV_A:DV_CONV_A + CONV_A_K], "conv_b_w": dv[DV_CONV_B:DV_CONV_B + CONV_B_K],
            "gate_bias": dgbias[0:2],
        }
        dw_in = {"in": _weight_grad(dp, s["h1b"], f"dw_in_{l}")}
        if l > 0:
            near_in, far_in = _Exchange(dw_in, xg["in"], NEAR_PEERS), _Exchange(dw_in, xg["in"], FAR_PEERS, local=False)
        else:
            near_in, far_in = _Exchange(dw_in, xg["in"]), None
        (dx, dln1), got = _in_proj_bwd(dp, s["x"], dx1, ln1_g[l][None], gath[l]["in"], grp["in"], f"in_proj_bwd_{l}", [near_in])
        recv[l]["in"] = [got[0][0]]
        small_grads[l]["ln1_g"] = dln1[0:1]
    grad_x = _from_tile_order(dx, tm_time)[None]

    g = {}
    gsum = [{k: _sum_slots(recv[l][k], f"sum_{k}_{l}") for k in xg} for l in range(n_layers)]

    def part(key):
        k = next(name for name, group in xg.items() if key in group.keys)
        o, r = xg[k].off[key], xg[k].rows[key]
        return jnp.stack([gsum[l][k][o:o + r] for l in range(n_layers)])

    g = {"w_in": jnp.swapaxes(part("in"), 1, 2), "w_out_a": part("oa"), "w_out_b": part("ob"), "w_o": part("o"),
         "w_ffn_gate": jnp.swapaxes(part("g"), 1, 2), "w_ffn_up": jnp.swapaxes(part("u"), 1, 2), "w_ffn_down": part("d")}
    ((late_all,),) = _comm_call([_Gather(layer_block(0).astype(BF16))], "gather_small_grads_0")
    early_sum = _sum_slots([early_all], "sum_small_grads")
    block_rows = late_all.shape[1]
    per_layer = [_sum_slots([late_all], "sum_small_grads_0")]
    per_layer += [early_sum[(k - 1) * block_rows:k * block_rows] for k in range(1, n_layers)]
    g["final_g"] = early_sum[(n_layers - 1) * block_rows].reshape(w["final_g"].shape)
    o = 0
    for n in layer_names:
        rows = small_grads[0][n].shape[0]
        stacked = jnp.concatenate([per_layer[k][o:o + rows] for k in range(n_layers)], axis=0)
        if n in SMALL_SHARDED:
            g[n] = lax.dynamic_slice_in_dim(stacked, me * dd, dd, axis=1).reshape(n_layers, rows, dd)
        else:
            g[n] = stacked.reshape(w[n].shape)
        o += rows

    delta, new_m, new_v = {}, {}, {}
    gate_maps = ("lru_wa", "lru_wx")
    for n in MATRICES + gate_maps:
        shape = w[n].shape
        flat = lambda a: a.reshape(-1, d if n in gate_maps else shape[-1])
        dl, nm, nv = _adamw(flat(w[n]), flat(g[n]), flat(mom[n]), flat(var[n]), f"adamw_{n}")
        delta[n], new_m[n], new_v[n] = dl.reshape(shape), nm.reshape(shape), nv.reshape(shape)
    vectors = tuple(n for n in REPLICATED if n not in gate_maps)
    for group, width, name in ((vectors, d, "adamw_replicated"), (SMALL_SHARDED, dd, "adamw_vectors")):
        cat = lambda src: _pad_rows(jnp.concatenate([src[n].reshape(-1, width) for n in group], axis=0))
        dl, nm, nv = _adamw(cat(w), cat(g), cat(mom), cat(var), name)
        o = 0
        for n in group:
            rows = w[n].size // width
            delta[n], new_m[n], new_v[n] = (a[o:o + rows].reshape(w[n].shape) for a in (dl, nm, nv))
            o += rows

    return (loss, grad_x, *[g[n] for n in ORDER], *[delta[n] for n in ORDER], *[new_m[n] for n in ORDER],
            *[new_v[n] for n in ORDER])
```

```python
import math

import jax
import jax.numpy as jnp
from jax import lax
from jax.experimental import pallas as pl
from jax.experimental.pallas import tpu as pltpu

F32 = jnp.float32
BF16 = jnp.bfloat16

N_DEV = 8
N_PROJ = 7
LRU_HEADS = 16
LRU_C = 8.0
RMS_EPS = 1e-6
CONV_A_K = 3
CONV_B_K = 4
GELU_C = math.sqrt(2.0 / math.pi)
GELU_A = 0.044715

ADAM_LR = 0.001
ADAM_B1 = 0.9
ADAM_B2 = 0.999
ADAM_EPS = 1e-08
ADAM_WD = 0.01
ADAM_STEP = 10

LANE = 128
SUBLANE = 8
MXU_TILE = 256
VMEM_LIMIT = 52 << 20
ALL_PEERS = tuple(range(1, N_DEV))
NEAR_PEERS = (1, 2, 3, 4, 5)
FAR_PEERS = (6, 7)
MESH = pl.DeviceIdType.MESH
ANY = pl.BlockSpec(memory_space=pl.ANY)


def _dot_nn(a, b):
    return lax.dot_general(a, b, (((1,), (0,)), ((), ())), preferred_element_type=F32)


def _dot_nt(a, b):
    return lax.dot_general(a, b, (((1,), (1,)), ((), ())), preferred_element_type=F32)


def _dot_tn(a, b):
    return lax.dot_general(a, b, (((0,), (0,)), ((), ())), preferred_element_type=F32)


def _sigmoid(x):
    return 1.0 / (1.0 + jnp.exp(-x))


def _gelu_and_grad(x):
    x2 = x * x
    t = jnp.tanh(GELU_C * x * (1.0 + GELU_A * x2))
    g = 0.5 * x * (1.0 + t)
    dg = 0.5 * (1.0 + t) + 0.5 * x * (1.0 - t * t) * GELU_C * (1.0 + 3.0 * GELU_A * x2)
    return g, dg


def _zero(ref):
    ref[...] = jnp.zeros(ref.shape, ref.dtype)


def _fit_rows(r, row_bytes, budget=1 << 20):
    fits = [t for t in range(16, r + 1, 16) if r % t == 0 and t * row_bytes <= budget]
    return max(fits) if fits else r


def _row_tile(t, want):
    tm = min(want, t // 2)
    assert t % tm == 0 and tm % SUBLANE == 0, (t, tm)
    return tm


def _params(n_grid=1, **kw):
    return pltpu.CompilerParams(dimension_semantics=("arbitrary",) * n_grid, vmem_limit_bytes=VMEM_LIMIT, **kw)


class _Group:
    def __init__(self, keys, rows):
        self.keys = keys
        self.rows = dict(zip(keys, rows))
        self.off, o = {}, 0
        for k in keys:
            self.off[k] = o
            o += self.rows[k]
        self.total = o


def _groups(d, ff):
    dd, ffs = d // N_DEV, ff // N_DEV
    return {"in": _Group(("in",), (N_PROJ * dd,)),
            "rest": _Group(("oa", "ob", "o", "g", "u", "d"), (dd, dd, dd, ffs, ffs, ffs))}


def _load_weights(g_ref, grp, keys, dsts, sems):
    copies = []
    for n, (k, dst) in enumerate(zip(keys, dsts)):
        rows, off = grp.rows[k], grp.off[k]
        copies += [pltpu.make_async_copy(g_ref.at[p, pl.ds(off, rows), :], dst.at[pl.ds(p * rows, rows), :],
                                         sems.at[n * N_DEV + p]) for p in range(N_DEV)]
    for c in copies:
        c.start()
    for c in copies:
        c.wait()


def _comm_sems():
    return [pltpu.SemaphoreType.DMA((N_DEV - 1,)), pltpu.SemaphoreType.DMA((N_DEV - 1,)), pltpu.SemaphoreType.DMA]


class _Gather:
    def __init__(self, x):
        self.inputs = [x]
        self.out_shape = [jax.ShapeDtypeStruct((N_DEV,) + x.shape, x.dtype)]
        self.scratch = _comm_sems()

    def _plan(self, ins, outs, scr):
        (x_ref,), (out_ref,), (send_sems, recv_sems, local_sem) = ins, outs, scr
        mx, my, mc = lax.axis_index("x"), lax.axis_index("y"), lax.axis_index("c")
        me, sibling = (mx, my, mc), (mx, my, 1 - mc)
        xn, yn, dg = (1 - mx, my), (mx, 1 - my), (1 - mx, 1 - my)
        core0 = mc == 0
        relayed = (jnp.where(core0, 1 - mx, mx), jnp.where(core0, my, 1 - my))
        relay_to = (jnp.where(core0, mx, 1 - mx), jnp.where(core0, 1 - my, my))

        def slot(px, py, pc):
            return out_ref.at[4 * px + 2 * py + pc]

        def copy(k, block, to, src=None):
            return pltpu.make_async_remote_copy(
                src_ref=slot(*block) if src is None else src, dst_ref=slot(*block),
                send_sem=send_sems.at[k], recv_sem=recv_sems.at[k], device_id=to, device_id_type=MESH)

        mine = lambda: pltpu.make_async_copy(x_ref, slot(*me), local_sem)
        own = [lambda: copy(0, me, sibling, src=x_ref), lambda: copy(1, me, (*xn, mc), src=x_ref),
               lambda: copy(2, me, (*yn, mc), src=x_ref)]
        relay = lambda: copy(3, (*relayed, mc), (*relay_to, mc))
        passes = [lambda: copy(4, (*xn, mc), sibling), lambda: copy(5, (*yn, mc), sibling), lambda: copy(6, (*dg, mc), sibling)]
        arrival = lambda k: copy(k, me, me)
        return mine, own, relay, passes, arrival

    def start(self, ins, outs, scr):
        mine, own, _, _, _ = self._plan(ins, outs, scr)
        mine().start()
        for cp in own:
            cp().start()

    def relay(self, ins, outs, scr):
        _, _, relay, passes, arrival = self._plan(ins, outs, scr)
        arrival(1).wait_recv()
        arrival(2).wait_recv()
        relay().start()
        passes[0]().start()
        passes[1]().start()

    def mid(self, ins, outs, scr):
        _, _, _, passes, arrival = self._plan(ins, outs, scr)
        arrival(3).wait_recv()
        passes[2]().start()

    def finish(self, ins, outs, scr):
        mine, _, _, _, arrival = self._plan(ins, outs, scr)
        for k in (0, 4, 5, 6):
            arrival(k).wait_recv()
        for k in range(N_DEV - 1):
            arrival(k).wait_send()
        mine().wait()


class _Exchange:
    def __init__(self, mats, grp, peers=ALL_PEERS, local=True):
        self.grp, self.peers, self.local = grp, tuple(peers), local
        self.inputs = [mats[k] for k in grp.keys]
        slots = len(self.peers) + (1 if local else 0)
        self.out_shape = [jax.ShapeDtypeStruct((slots, grp.total, self.inputs[0].shape[1]), BF16)]
        self.scratch = [pltpu.SemaphoreType.DMA((len(self.peers),)), pltpu.SemaphoreType.DMA((len(self.peers),)),
                        pltpu.SemaphoreType.DMA]

    def _pieces(self, g_refs, out_ref, q, dst_slot):
        out = []
        for g_ref, k in zip(g_refs, self.grp.keys):
            rows = self.grp.rows[k]
            out.append((g_ref.at[pl.ds(pl.multiple_of(q * rows, 16), rows), :],
                        out_ref.at[dst_slot, pl.ds(self.grp.off[k], rows), :]))
        return out

    def start(self, ins, outs, scr):
        (out_ref,), (send_sems, recv_sems, local_sem) = outs, scr
        mx, my, mc = lax.axis_index("x"), lax.axis_index("y"), lax.axis_index("c")
        if self.local:
            for s, t in self._pieces(ins, out_ref, 4 * mx + 2 * my + mc, 0):
                pltpu.make_async_copy(s, t, local_sem).start()
        for n, k in enumerate(self.peers):
            px, py, pc = mx ^ ((k >> 2) & 1), my ^ ((k >> 1) & 1), mc ^ (k & 1)
            for s, t in self._pieces(ins, out_ref, 4 * px + 2 * py + pc, n + (1 if self.local else 0)):
                pltpu.make_async_remote_copy(src_ref=s, dst_ref=t, send_sem=send_sems.at[n], recv_sem=recv_sems.at[n],
                                             device_id=(px, py, pc), device_id_type=MESH).start()

    def relay(self, ins, outs, scr):
        pass

    def mid(self, ins, outs, scr):
        pass

    def finish(self, ins, outs, scr):
        (out_ref,), (send_sems, recv_sems, local_sem) = outs, scr
        mx, my, mc = lax.axis_index("x"), lax.axis_index("y"), lax.axis_index("c")
        whole = out_ref.at[0]
        for n in range(len(self.peers)):
            done = pltpu.make_async_remote_copy(src_ref=whole, dst_ref=whole, send_sem=send_sems.at[n],
                                                recv_sem=recv_sems.at[n], device_id=(mx, my, mc), device_id_type=MESH)
            done.wait_send()
            done.wait_recv()
        if self.local:
            pltpu.make_async_copy(whole, whole, local_sem).wait()


def _split(refs, sizes):
    out, pos = [], 0
    for n in sizes:
        out.append(refs[pos:pos + n])
        pos += n
    return out


def _hosted_call(body, comms, *, name, grid, in_specs, out_specs, out_shape, scratch_shapes, args, aliases=None):
    n_steps = grid[0]
    nc = len(comms)
    sizes = ([len(in_specs)] + [len(c.inputs) for c in comms] + [len(out_specs)] + [len(c.out_shape) for c in comms]
             + [len(scratch_shapes)] + [len(c.scratch) for c in comms])

    def hosted(*refs):
        parts = _split(refs, sizes)
        ins, c_ins = parts[0], parts[1:1 + nc]
        outs, c_outs = parts[1 + nc], parts[2 + nc:2 + 2 * nc]
        scr, c_scr = parts[2 + 2 * nc], parts[3 + 2 * nc:]
        step = pl.program_id(0)
        if comms:
            @pl.when(step == 0)
            def _():
                for c, a, b, s in zip(comms, c_ins, c_outs, c_scr):
                    c.start(a, b, s)

            relay_step = (3 * n_steps) // 5

            @pl.when(step == relay_step)
            def _():
                for c, a, b, s in zip(comms, c_ins, c_outs, c_scr):
                    c.relay(a, b, s)

            @pl.when(step == max(n_steps - 2, relay_step))
            def _():
                for c, a, b, s in zip(comms, c_ins, c_outs, c_scr):
                    c.mid(a, b, s)

        body(*ins, *outs, *scr)
        if comms:
            @pl.when(step == n_steps - 1)
            def _():
                for c, a, b, s in zip(comms, c_ins, c_outs, c_scr):
                    c.finish(a, b, s)

    res = pl.pallas_call(
        hosted, name=name, grid=grid,
        out_shape=[*out_shape, *[o for c in comms for o in c.out_shape]],
        in_specs=[*in_specs, *[ANY for c in comms for _ in c.inputs]],
        out_specs=[*out_specs, *[ANY for c in comms for _ in c.out_shape]],
        scratch_shapes=[*scratch_shapes, *[s for c in comms for s in c.scratch]],
        input_output_aliases=aliases or {},
        compiler_params=_params(),
    )(*args, *[a for c in comms for a in c.inputs])
    main, rest = res[:len(out_specs)], res[len(out_specs):]
    return main, _split(rest, [len(c.out_shape) for c in comms])


def _comm_call(comms, name):
    sizes = [len(c.inputs) for c in comms] + [len(c.out_shape) for c in comms] + [len(c.scratch) for c in comms]
    nc = len(comms)

    def body(*refs):
        parts = _split(refs, sizes)
        triples = list(zip(comms, parts[:nc], parts[nc:2 * nc], parts[2 * nc:]))
        for phase in ("start", "relay", "mid", "finish"):
            for c, ins, outs, scr in triples:
                getattr(c, phase)(ins, outs, scr)

    res = pl.pallas_call(
        body, name=name, out_shape=[o for c in comms for o in c.out_shape],
        in_specs=[ANY for c in comms for _ in c.inputs], out_specs=[ANY for c in comms for _ in c.out_shape],
        scratch_shapes=[s for c in comms for s in c.scratch],
    )(*[a for c in comms for a in c.inputs])
    return _split(res, [len(c.out_shape) for c in comms])


def _sum_slots(xs, name):
    _, r, c = xs[0].shape
    tr = _fit_rows(r, c * 4)

    def body(*refs):
        acc = None
        for x_ref in refs[:-1]:
            for p in range(x_ref.shape[0]):
                v = x_ref[p].astype(F32)
                acc = v if acc is None else acc + v
        refs[-1][...] = acc

    return pl.pallas_call(
        body, name=name, grid=(r // tr,),
        out_shape=jax.ShapeDtypeStruct((r, c), F32),
        in_specs=[pl.BlockSpec((x.shape[0], tr, c), lambda i: (0, i, 0)) for x in xs],
        out_specs=pl.BlockSpec((tr, c), lambda i: (i, 0)),
        compiler_params=_params(),
    )(*xs)


def _time_tile(t):
    return _row_tile(t, 256)


def _to_tile_order(a, tm):
    t, c = a.shape
    return jnp.swapaxes(a.reshape(t // tm, SUBLANE, tm // SUBLANE, c), 1, 2).reshape(t, c)


def _from_tile_order(a, tm):
    t, c = a.shape
    return jnp.swapaxes(a.reshape(t // tm, tm // SUBLANE, SUBLANE, c), 1, 2).reshape(t, c)


def _causal_fill(buf, v, prev_tail, n, row):
    tm = v.shape[0]
    for q in range(n):
        cur = v[tm - SUBLANE * (n - q):tm - SUBLANE * (n - q - 1), :]
        prv = prev_tail[SUBLANE * q:SUBLANE * (q + 1), :]
        buf[SUBLANE * q:SUBLANE * (q + 1), :] = jnp.where(row == 0, pltpu.roll(prv, 1, 0), pltpu.roll(cur, 1, 0))
    buf[SUBLANE * n:, :] = v


def _anticausal_fill(buf, v, next_head, n, row):
    tm = v.shape[0]
    buf[0:tm, :] = v
    for q in range(n):
        cur = v[SUBLANE * q:SUBLANE * (q + 1), :]
        nxt = next_head[SUBLANE * q:SUBLANE * (q + 1), :]
        buf[tm + SUBLANE * q:tm + SUBLANE * (q + 1), :] = jnp.where(
            row == SUBLANE - 1, pltpu.roll(nxt, SUBLANE - 1, 0), pltpu.roll(cur, SUBLANE - 1, 0))


def _chain_scan(abuf, bbuf, nk, reverse):
    cw = abuf.shape[1]

    def step(n, carry):
        h, c = carry
        r0 = pl.multiple_of((nk - 1 - n if reverse else n) * SUBLANE, SUBLANE)
        ak = abuf[pl.ds(r0, SUBLANE), :]
        h = ak * h + bbuf[pl.ds(r0, SUBLANE), :]
        c = ak * c
        bbuf[pl.ds(r0, SUBLANE), :] = h
        abuf[pl.ds(r0, SUBLANE), :] = c
        return h, c

    return lax.fori_loop(0, nk, step, (jnp.zeros((SUBLANE, cw), F32), jnp.ones((SUBLANE, cw), F32)), unroll=True)


def _sublane_scan(a, b, row, reverse):
    for sh in (1, 2, 4):
        if reverse:
            m = row < SUBLANE - sh
            b = jnp.where(m, a * pltpu.roll(b, SUBLANE - sh, 0) + b, b)
            a = jnp.where(m, a * pltpu.roll(a, SUBLANE - sh, 0), a)
        else:
            m = row >= sh
            b = jnp.where(m, a * pltpu.roll(b, sh, 0) + b, b)
            a = jnp.where(m, a * pltpu.roll(a, sh, 0), a)
    return a, b


def _lru_gates(ub, bda, bdx, ba, bx, sp):
    r = _sigmoid(_dot_nn(ub, bda) + ba)
    i = _sigmoid(_dot_nn(ub, bdx) + bx)
    log_a = (-LRU_C) * r * sp
    a = jnp.exp(log_a)
    s2 = -jnp.tanh(log_a) * (1.0 + a * a)
    inv_s = lax.rsqrt(s2)
    s = jnp.where(s2 > 0.0, s2 * inv_s, 0.0)
    return r, i, a, s, inv_s


def _in_proj_mixer_fwd(x, g_row, gath, grp, caw, cbw, vec, bda, bdx, name, comms=()):
    t, d = x.shape
    n_in = N_PROJ * d
    tm = _time_tile(t)
    nk = tm // SUBLANE
    cw = min(MXU_TILE, d)
    nb = d // cw

    def body(x_ref, g_ref, gath_ref, caw_ref, cbw_ref, vec_ref, bda_ref, bdx_ref,
             p_ref, h1_ref, ya_ref, yb_ref, u_ref, h_ref, w_in, sems, zbuf, xbuf, abuf, bbuf, z_tail, x_tail, h_carry):
        @pl.when(pl.program_id(0) == 0)
        def _():
            _load_weights(gath_ref, grp, ["in"], [w_in], sems)
            _zero(z_tail)
            _zero(x_tail)
            _zero(h_carry)

        def project_and_mix():
            xf = x_ref[...]
            rstd = lax.rsqrt(jnp.mean(xf * xf, axis=-1, keepdims=True) + RMS_EPS)
            h1 = (xf * rstd * g_ref[...]).astype(BF16)
            h1_ref[...] = h1
            for k in range(N_PROJ):
                p_ref[:, k * d:(k + 1) * d] = _dot_nt(h1, w_in[k * d:(k + 1) * d, :]).astype(BF16)
            row = lax.broadcasted_iota(jnp.int32, (SUBLANE, cw), 0)
            for j in range(nb):
                cs = slice(j * cw, (j + 1) * cw)
                ba_ref, ca_ref, xa_ref, xb_ref, gb_ref = (p_ref.at[:, k * d:(k + 1) * d] for k in range(5))
                z = ca_ref[:, cs].astype(F32) * xa_ref[:, cs].astype(F32)
                _causal_fill(zbuf, z, z_tail[:, cs], CONV_A_K - 1, row)
                z_tail[:, cs] = z[tm - (CONV_A_K - 1) * SUBLANE:, :]
                cz = caw_ref[0:1, cs] * zbuf[0:tm, :] + caw_ref[1:2, cs] * zbuf[SUBLANE:SUBLANE + tm, :] + caw_ref[2:3, cs] * z
                ya_ref[:, cs] = (ba_ref[:, cs].astype(F32) * cz).astype(BF16)
                xb = xb_ref[:, cs].astype(F32)
                _causal_fill(xbuf, xb, x_tail[:, cs], CONV_B_K - 1, row)
                x_tail[:, cs] = xb[tm - (CONV_B_K - 1) * SUBLANE:, :]
                u = (cbw_ref[0:1, cs] * xbuf[0:tm, :] + cbw_ref[1:2, cs] * xbuf[SUBLANE:SUBLANE + tm, :]
                     + cbw_ref[2:3, cs] * xbuf[2 * SUBLANE:2 * SUBLANE + tm, :] + cbw_ref[3:4, cs] * xb + vec_ref[0:1, cs])
                ub = u.astype(BF16)
                u = ub.astype(F32)
                _, gi, a, s, _ = _lru_gates(ub, bda_ref[j], bdx_ref[j], vec_ref[1:2, cs], vec_ref[2:3, cs], vec_ref[3:4, cs])
                abuf[...] = a
                bbuf[...] = s * (gi * u)
                h_end, a_prod = _chain_scan(abuf, bbuf, nk, reverse=False)
                a_inc, h_inc = _sublane_scan(a_prod, h_end, row, reverse=False)
                carry = h_carry[:, cs]
                ends = h_inc + a_inc * carry
                starts = jnp.where(row == 0, carry, pltpu.roll(ends, 1, 0))
                h_carry[:, cs] = jnp.broadcast_to(ends[SUBLANE - 1:SUBLANE, :], (SUBLANE, cw))
                h = (bbuf[...].reshape(nk, SUBLANE, cw) + abuf[...].reshape(nk, SUBLANE, cw) * starts[None]).reshape(tm, cw)
                gel, _ = _gelu_and_grad(gb_ref[:, cs].astype(F32))
                yb_ref[:, cs] = (h * gel).astype(BF16)
                u_ref[:, cs] = ub
                h_ref[:, cs] = h.astype(BF16)

        project_and_mix()

    small = pl.BlockSpec((SUBLANE, d), lambda i: (0, 0))
    bd = pl.BlockSpec((nb, cw, cw), lambda i: (0, 0, 0))
    row_spec = pl.BlockSpec((tm, d), lambda i: (i, 0))
    return _hosted_call(
        body, comms, name=name, grid=(t // tm,),
        out_shape=[jax.ShapeDtypeStruct((t, n_in), BF16)] + [jax.ShapeDtypeStruct((t, d), BF16)] * 5,
        in_specs=[row_spec, pl.BlockSpec((1, d), lambda i: (0, 0)), ANY, small, small, small, bd, bd],
        out_specs=[pl.BlockSpec((tm, n_in), lambda i: (i, 0))] + [row_spec] * 5,
        scratch_shapes=[pltpu.VMEM((n_in, d), BF16), pltpu.SemaphoreType.DMA((N_DEV,)),
                        pltpu.VMEM((tm + (CONV_A_K - 1) * SUBLANE, cw), F32), pltpu.VMEM((tm + (CONV_B_K - 1) * SUBLANE, cw), F32),
                        pltpu.VMEM((tm, cw), F32), pltpu.VMEM((tm, cw), F32),
                        pltpu.VMEM(((CONV_A_K - 1) * SUBLANE, d), F32), pltpu.VMEM(((CONV_B_K - 1) * SUBLANE, d), F32),
                        pltpu.VMEM((SUBLANE, d), F32)],
        args=(x, g_row, gath, caw, cbw, vec, bda, bdx))


def _merge_fwd(x, ya, yb, p, gbias, gath, grp, name, comms=()):
    t, d = x.shape
    tm = _row_tile(t, 512)

    def body(x_ref, ya_ref, yb_ref, ga_ref, gb_ref, gbias_ref, gath_ref, x1_ref, oa_ref, ob_ref, w_oa, w_ob, w_o, sems):
        @pl.when(pl.program_id(0) == 0)
        def _():
            _load_weights(gath_ref, grp, ["oa", "ob", "o"], [w_oa, w_ob, w_o], sems)

        oa = _dot_nn(ya_ref[...], w_oa[...]).astype(BF16)
        ob = _dot_nn(yb_ref[...], w_ob[...]).astype(BF16)
        oa_ref[...] = oa
        ob_ref[...] = ob
        sa = _sigmoid(ga_ref[...] + gbias_ref[0:1, :].astype(BF16))
        sb = _sigmoid(gb_ref[...] + gbias_ref[1:2, :].astype(BF16))
        x1_ref[...] = x_ref[...] + _dot_nn(sa * oa + sb * ob, w_o[...])

    row = pl.BlockSpec((tm, d), lambda i: (i, 0))
    return _hosted_call(
        body, comms, name=name, grid=(t // tm,),
        out_shape=[jax.ShapeDtypeStruct((t, d), F32), jax.ShapeDtypeStruct((t, d), BF16), jax.ShapeDtypeStruct((t, d), BF16)],
        in_specs=[row, row, row, pl.BlockSpec((tm, d), lambda i: (i, 5)), pl.BlockSpec((tm, d), lambda i: (i, 6)),
                  pl.BlockSpec((SUBLANE, d), lambda i: (0, 0)), ANY],
        out_specs=[row, row, row],
        scratch_shapes=[pltpu.VMEM((d, d), BF16)] * 3 + [pltpu.SemaphoreType.DMA((3 * N_DEV,))],
        args=(x, ya, yb, p, p, gbias, gath))


def _ffn_fwd(x1, g_row, gath, grp, name, comms=()):
    t, d = x1.shape
    ff = grp.rows["g"] * N_DEV
    tm = _row_tile(t, 512)
    fc = MXU_TILE
    assert ff % fc == 0

    def body(x_ref, g_ref, gath_ref, x2_ref, gg_ref, uu_ref, w_g, w_u, w_d, acc, sems):
        @pl.when(pl.program_id(0) == 0)
        def _():
            _load_weights(gath_ref, grp, ["g", "u", "d"], [w_g, w_u, w_d], sems)

        xf = x_ref[...]
        rstd = lax.rsqrt(jnp.mean(xf * xf, axis=-1, keepdims=True) + RMS_EPS)
        h = (xf * rstd * g_ref[...]).astype(BF16)
        acc[...] = xf
        for c in range(ff // fc):
            fs = slice(c * fc, (c + 1) * fc)
            gg = _dot_nt(h, w_g[fs, :]).astype(BF16)
            uu = _dot_nt(h, w_u[fs, :]).astype(BF16)
            gg_ref[:, fs] = gg
            uu_ref[:, fs] = uu
            acc[...] += _dot_nn(gg * _sigmoid(gg) * uu, w_d[fs, :])
        x2_ref[...] = acc[...]

    row = pl.BlockSpec((tm, d), lambda i: (i, 0))
    wide = pl.BlockSpec((tm, ff), lambda i: (i, 0))
    return _hosted_call(
        body, comms, name=name, grid=(t // tm,),
        out_shape=[jax.ShapeDtypeStruct((t, d), F32), jax.ShapeDtypeStruct((t, ff), BF16), jax.ShapeDtypeStruct((t, ff), BF16)],
        in_specs=[row, pl.BlockSpec((1, d), lambda i: (0, 0)), ANY],
        out_specs=[row, wide, wide],
        scratch_shapes=[pltpu.VMEM((ff, d), BF16)] * 3 + [pltpu.VMEM((tm, d), F32), pltpu.SemaphoreType.DMA((3 * N_DEV,))],
        args=(x1, g_row, gath))


def _loss_head(x, g_row, target, name):
    t, d = x.shape
    tm = _row_tile(t, 512)

    def body(x_ref, g_ref, tgt_ref, loss_ref, dx_ref, dg_ref):
        @pl.when(pl.program_id(0) == 0)
        def _():
            _zero(loss_ref)
            _zero(dg_ref)

        xf = x_ref[...]
        rstd = lax.rsqrt(jnp.mean(xf * xf, axis=-1, keepdims=True) + RMS_EPS)
        xh = xf * rstd
        g = g_ref[...]
        err = xh * g - tgt_ref[...]
        loss_ref[...] += 0.5 * jnp.sum(jnp.sum(err * err, axis=-1, keepdims=True), axis=0, keepdims=True) * (1.0 / d)
        dy = err * (1.0 / d)
        dg_ref[0:1, :] += jnp.sum(dy * xh, axis=0, keepdims=True)
        dxh = dy * g
        dx_ref[...] = rstd * (dxh - xh * jnp.mean(dxh * xh, axis=-1, keepdims=True))

    row = pl.BlockSpec((tm, d), lambda i: (i, 0))
    return pl.pallas_call(
        body, name=name, grid=(t // tm,),
        out_shape=[jax.ShapeDtypeStruct((SUBLANE, LANE), F32), jax.ShapeDtypeStruct((t, d), F32),
                   jax.ShapeDtypeStruct((SUBLANE, d), F32)],
        in_specs=[row, pl.BlockSpec((1, d), lambda i: (0, 0)), row],
        out_specs=[pl.BlockSpec((SUBLANE, LANE), lambda i: (0, 0)), row, pl.BlockSpec((SUBLANE, d), lambda i: (0, 0))],
        compiler_params=_params(),
    )(x, g_row, target)


def _ffn_bwd_act(dx2, gg, uu, gath, grp, name, comms=()):
    t, d = dx2.shape
    ff = grp.rows["g"] * N_DEV
    tm = _row_tile(t, 512)
    fc = MXU_TILE
    n_t = t // tm

    def body(dx2_ref, gg_ref, uu_ref, gath_ref, dgg_ref, duu_ref, dwd_ref, w_d, acc, sems):
        @pl.when(pl.program_id(0) == 0)
        def _():
            _load_weights(gath_ref, grp, ["d"], [w_d], sems)
            _zero(acc)

        dx2b = dx2_ref[...].astype(BF16)
        for c in range(ff // fc):
            fs = slice(c * fc, (c + 1) * fc)
            df = _dot_nt(dx2b, w_d[fs, :]).astype(BF16)
            g = gg_ref[:, fs]
            u = uu_ref[:, fs]
            sg = _sigmoid(g)
            silu = g * sg
            acc[fs, :] += _dot_tn(silu * u, dx2b)
            duu_ref[:, fs] = df * silu
            dgg_ref[:, fs] = df * u * (sg * (1.0 + g * (1.0 - sg)))

        @pl.when(pl.program_id(0) == n_t - 1)
        def _():
            w_d[...] = acc[...].astype(BF16)
            out = pltpu.make_async_copy(w_d, dwd_ref, sems.at[0])
            out.start()
            out.wait()

    row = pl.BlockSpec((tm, d), lambda i: (i, 0))
    wide = pl.BlockSpec((tm, ff), lambda i: (i, 0))
    sd = jax.ShapeDtypeStruct
    return _hosted_call(
        body, comms, name=name, grid=(n_t,),
        out_shape=[sd((t, ff), BF16), sd((t, ff), BF16), sd((ff, d), BF16)],
        in_specs=[row, wide, wide, ANY],
        out_specs=[wide, wide, ANY],
        scratch_shapes=[pltpu.VMEM((ff, d), BF16), pltpu.VMEM((ff, d), F32), pltpu.SemaphoreType.DMA((N_DEV,))],
        args=(dx2, gg, uu, gath))


def _ffn_bwd_in(dgg, duu, dx2, x1, g_row, gath, grp, name, comms=()):
    t, d = x1.shape
    ff = grp.rows["g"] * N_DEV
    tm = _row_tile(t, 512)

    def body(dgg_ref, duu_ref, dx2_ref, x_ref, g_ref, gath_ref, dx1_ref, dx1b_ref, h_ref, dg_ref, w_g, w_u, sems):
        @pl.when(pl.program_id(0) == 0)
        def _():
            _load_weights(gath_ref, grp, ["g", "u"], [w_g, w_u], sems)
            _zero(dg_ref)

        dh = _dot_nn(dgg_ref[...], w_g[...]) + _dot_nn(duu_ref[...], w_u[...])
        xf = x_ref[...]
        rstd = lax.rsqrt(jnp.mean(xf * xf, axis=-1, keepdims=True) + RMS_EPS)
        xh = xf * rstd
        g = g_ref[...]
        h_ref[...] = (xh * g).astype(BF16)
        dg_ref[0:1, :] += jnp.sum(dh * xh, axis=0, keepdims=True)
        dxh = dh * g
        dx1 = dx2_ref[...] + rstd * (dxh - xh * jnp.mean(dxh * xh, axis=-1, keepdims=True))
        dx1_ref[...] = dx1
        dx1b_ref[...] = dx1.astype(BF16)

    row = pl.BlockSpec((tm, d), lambda i: (i, 0))
    wide = pl.BlockSpec((tm, ff), lambda i: (i, 0))
    sd = jax.ShapeDtypeStruct
    return _hosted_call(
        body, comms, name=name, grid=(t // tm,),
        out_shape=[sd((t, d), F32), sd((t, d), BF16), sd((t, d), BF16), sd((SUBLANE, d), F32)],
        in_specs=[wide, wide, row, row, pl.BlockSpec((1, d), lambda i: (0, 0)), ANY],
        out_specs=[row, row, row, pl.BlockSpec((SUBLANE, d), lambda i: (0, 0))],
        scratch_shapes=[pltpu.VMEM((ff, d), BF16)] * 2 + [pltpu.SemaphoreType.DMA((2 * N_DEV,))],
        args=(dgg, duu, dx2, x1, g_row, gath))


def _merge_bwd(dx1b, oa, ob, ya, yb, p, gbias, gath, grp, name, comms=()):
    t, d = oa.shape
    tm = _row_tile(t, 512)
    n_t = t // tm

    def body(dx_ref, oa_ref, ob_ref, ya_ref, yb_ref, ga_ref, gb_ref, gbias_ref, gath_ref,
             dya_ref, dyb_ref, dp_ref, dgb_ref, dwoa_ref, dwob_ref, dwo_ref,
             w_oa, w_ob, w_o, acc_oa, acc_ob, acc_o, stage, sems, out_sems):
        @pl.when(pl.program_id(0) == 0)
        def _():
            _load_weights(gath_ref, grp, ["oa", "ob", "o"], [w_oa, w_ob, w_o], sems)
            for ref in (dgb_ref, acc_oa, acc_ob, acc_o):
                _zero(ref)

        dxb = dx_ref[...]
        dm = _dot_nt(dxb, w_o[...]).astype(BF16)
        oa = oa_ref[...]
        ob = ob_ref[...]
        sa = _sigmoid(ga_ref[...] + gbias_ref[0:1, :].astype(BF16))
        sb = _sigmoid(gb_ref[...] + gbias_ref[1:2, :].astype(BF16))
        acc_o[...] += _dot_tn(sa * oa + sb * ob, dxb)
        doa = dm * sa
        dob = dm * sb
        acc_oa[...] += _dot_tn(ya_ref[...], doa)
        acc_ob[...] += _dot_tn(yb_ref[...], dob)
        dga = dm * oa * sa * (1.0 - sa)
        dgb = dm * ob * sb * (1.0 - sb)
        step = pl.program_id(0)
        slot = step % 2

        def to_dp(k, at_step):
            return pltpu.make_async_copy(stage.at[k], dp_ref.at[pl.ds(at_step * tm, tm), pl.ds(5 * d, 2 * d)], out_sems.at[k])

        @pl.when(step >= 2)
        def _():
            to_dp(slot, step - 2).wait()

        stage[slot, :, 0:d] = dga
        stage[slot, :, d:2 * d] = dgb
        to_dp(slot, step).start()
        ones = jnp.ones((SUBLANE, tm), BF16)
        dgb_ref[0:1, :] += _dot_nn(ones, dga)[0:1, :]
        dgb_ref[1:2, :] += _dot_nn(ones, dgb)[0:1, :]
        dya_ref[...] = _dot_nt(doa, w_oa[...]).astype(BF16)
        dyb_ref[...] = _dot_nt(dob, w_ob[...]).astype(BF16)

        @pl.when(pl.program_id(0) == n_t - 1)
        def _():
            outs = []
            for n, (acc, stage, dst) in enumerate(((acc_oa, w_oa, dwoa_ref), (acc_ob, w_ob, dwob_ref), (acc_o, w_o, dwo_ref))):
                stage[...] = acc[...].astype(BF16)
                outs.append(pltpu.make_async_copy(stage, dst, sems.at[n]))
                outs[-1].start()
            for cp in outs:
                cp.wait()
            for back in range(min(2, n_t)):
                to_dp((n_t - 1 - back) % 2, n_t - 1 - back).wait()

    row = pl.BlockSpec((tm, d), lambda i: (i, 0))
    sd = jax.ShapeDtypeStruct
    return _hosted_call(
        body, comms, name=name, grid=(n_t,),
        out_shape=[sd((t, d), BF16), sd((t, d), BF16), sd((t, N_PROJ * d), BF16), sd((SUBLANE, d), F32),
                   sd((d, d), BF16), sd((d, d), BF16), sd((d, d), BF16)],
        in_specs=[row, row, row, row, row, pl.BlockSpec((tm, d), lambda i: (i, 5)), pl.BlockSpec((tm, d), lambda i: (i, 6)),
                  pl.BlockSpec((SUBLANE, d), lambda i: (0, 0)), ANY],
        out_specs=[row, row, ANY, pl.BlockSpec((SUBLANE, d), lambda i: (0, 0)), ANY, ANY, ANY],
        scratch_shapes=[pltpu.VMEM((d, d), BF16)] * 3 + [pltpu.VMEM((d, d), F32)] * 3
        + [pltpu.VMEM((2, tm, 2 * d), BF16), pltpu.SemaphoreType.DMA((3 * N_DEV,)), pltpu.SemaphoreType.DMA((2,))],
        args=(dx1b, oa, ob, ya, yb, p, p, gbias, gath))


DV_CONV_B_B, DV_BA, DV_BX, DV_SP, DV_CONV_A, DV_CONV_B = 0, 1, 2, 3, 4, 7
DV_ROWS = 16


def _mixer_bwd(dya, dyb, dp_gates, p, u_s, h_s, caw, cbw, vec, bda, bdx, name, comms=()):
    t, d = dya.shape
    tm = _time_tile(t)
    n_t = t // tm
    nk = tm // SUBLANE
    cw = min(MXU_TILE, d)
    nb = d // cw
    halo = 4 * SUBLANE
    ka, kb = CONV_A_K - 1, CONV_B_K - 1

    def body(dya_ref, dyb_ref, _, ba_ref, ca_ref, xa_ref, xb_ref, gb_ref, cah_ref, xah_ref, xbh_ref,
             u_ref, h_ref, hh_ref, caw_ref, cbw_ref, vec_ref, bda_ref, bdx_ref,
             dp_ref, dv_ref, dwa_ref, dwx_ref,
             zbuf, xbuf, hbuf, dczbuf, dubuf, a2buf, a1buf, lbuf, dcz_head, du_head, a_head, lam_head):
        i = pl.program_id(0)

        @pl.when(i == 0)
        def _():
            for ref in (dv_ref, dwa_ref, dwx_ref, dcz_head, du_head, a_head, lam_head):
                _zero(ref)

        has_prev = jnp.where(i < n_t - 1, 1.0, 0.0).astype(F32)
        row = lax.broadcasted_iota(jnp.int32, (SUBLANE, cw), 0)

        def colsum(v):
            return jnp.sum(v, axis=0, keepdims=True)

        for j in range(nb):
            cs = slice(j * cw, (j + 1) * cw)
            ca = ca_ref[:, cs].astype(F32)
            xa = xa_ref[:, cs].astype(F32)
            z = ca * xa
            z_before = cah_ref[:, cs].astype(F32) * xah_ref[:, cs].astype(F32) * has_prev
            _causal_fill(zbuf, z, z_before[halo - ka * SUBLANE:, :], ka, row)
            z2 = zbuf[0:tm, :]
            z1 = zbuf[SUBLANE:SUBLANE + tm, :]
            w0, w1, w2 = caw_ref[0:1, cs], caw_ref[1:2, cs], caw_ref[2:3, cs]
            cz = w0 * z2 + w1 * z1 + w2 * z
            dya = dya_ref[:, cs].astype(F32)
            dp_ref[:, 0 * d + j * cw:0 * d + (j + 1) * cw] = (dya * cz).astype(BF16)
            dcz = dya * ba_ref[:, cs].astype(F32)
            _anticausal_fill(dczbuf, dcz, dcz_head[:, cs], ka, row)
            dcz_head[:, cs] = dcz[0:ka * SUBLANE, :]
            dz = w2 * dcz + w1 * dczbuf[SUBLANE:SUBLANE + tm, :] + w0 * dczbuf[2 * SUBLANE:2 * SUBLANE + tm, :]
            dv_ref[DV_CONV_A + 0:DV_CONV_A + 1, cs] += colsum(dcz * z2)
            dv_ref[DV_CONV_A + 1:DV_CONV_A + 2, cs] += colsum(dcz * z1)
            dv_ref[DV_CONV_A + 2:DV_CONV_A + 3, cs] += colsum(dcz * z)
            dp_ref[:, 1 * d + j * cw:1 * d + (j + 1) * cw] = (dz * xa).astype(BF16)
            dp_ref[:, 2 * d + j * cw:2 * d + (j + 1) * cw] = (dz * ca).astype(BF16)
            h = h_ref[:, cs].astype(F32)
            h_before = hh_ref[:, cs].astype(F32) * has_prev
            _causal_fill(hbuf, h, h_before[halo - SUBLANE:, :], 1, row)
            h_prev = hbuf[0:tm, :]
            dyb = dyb_ref[:, cs].astype(F32)
            gel, dgel = _gelu_and_grad(gb_ref[:, cs].astype(F32))
            dp_ref[:, 4 * d + j * cw:4 * d + (j + 1) * cw] = (dyb * h * dgel).astype(BF16)
            ub = u_ref[:, cs]
            u = ub.astype(F32)
            sp = vec_ref[3:4, cs]
            r, gi, a, s, inv_s = _lru_gates(ub, bda_ref[j], bdx_ref[j], vec_ref[1:2, cs], vec_ref[2:3, cs], sp)
            _anticausal_fill(a2buf, a, a_head[:, cs], 1, row)
            a_head[:, cs] = a[0:SUBLANE, :]
            a1buf[...] = a2buf[SUBLANE:SUBLANE + tm, :]
            lbuf[...] = dyb * gel
            l_end, a_prod = _chain_scan(a1buf, lbuf, nk, reverse=True)
            a_inc, l_inc = _sublane_scan(a_prod, l_end, row, reverse=True)
            carry = lam_head[:, cs]
            ends = l_inc + a_inc * carry
            starts = jnp.where(row == SUBLANE - 1, carry, pltpu.roll(ends, SUBLANE - 1, 0))
            lam_head[:, cs] = jnp.broadcast_to(ends[0:1, :], (SUBLANE, cw))
            lam = (lbuf[...].reshape(nk, SUBLANE, cw) + a1buf[...].reshape(nk, SUBLANE, cw) * starts[None]).reshape(tm, cw)
            da = lam * h_prev
            iu = gi * u
            ds = lam * iu
            di = lam * s * u
            du = lam * s * gi
            dlog_a = da * a - ds * (a * a) * inv_s
            dv_ref[DV_SP:DV_SP + 1, cs] += colsum(dlog_a * r) * (-LRU_C)
            dpr = dlog_a * ((-LRU_C) * sp) * r * (1.0 - r)
            dpi = di * gi * (1.0 - gi)
            dv_ref[DV_BA:DV_BA + 1, cs] += colsum(dpr)
            dv_ref[DV_BX:DV_BX + 1, cs] += colsum(dpi)
            dprb = dpr.astype(BF16)
            dpib = dpi.astype(BF16)
            du = du + _dot_nt(dprb, bda_ref[j]) + _dot_nt(dpib, bdx_ref[j])
            dwa_ref[j] += _dot_tn(ub, dprb)
            dwx_ref[j] += _dot_tn(ub, dpib)
            xb = xb_ref[:, cs].astype(F32)
            x_before = xbh_ref[:, cs].astype(F32) * has_prev
            _causal_fill(xbuf, xb, x_before[halo - kb * SUBLANE:, :], kb, row)
            _anticausal_fill(dubuf, du, du_head[:, cs], kb, row)
            du_head[:, cs] = du[0:kb * SUBLANE, :]
            v0, v1, v2, v3 = cbw_ref[0:1, cs], cbw_ref[1:2, cs], cbw_ref[2:3, cs], cbw_ref[3:4, cs]
            dxb = (v3 * du + v2 * dubuf[SUBLANE:SUBLANE + tm, :] + v1 * dubuf[2 * SUBLANE:2 * SUBLANE + tm, :]
                   + v0 * dubuf[3 * SUBLANE:3 * SUBLANE + tm, :])
            dp_ref[:, 3 * d + j * cw:3 * d + (j + 1) * cw] = dxb.astype(BF16)
            dv_ref[DV_CONV_B_B:DV_CONV_B_B + 1, cs] += colsum(du)
            dv_ref[DV_CONV_B + 0:DV_CONV_B + 1, cs] += colsum(du * xbuf[0:tm, :])
            dv_ref[DV_CONV_B + 1:DV_CONV_B + 2, cs] += colsum(du * xbuf[SUBLANE:SUBLANE + tm, :])
            dv_ref[DV_CONV_B + 2:DV_CONV_B + 3, cs] += colsum(du * xbuf[2 * SUBLANE:2 * SUBLANE + tm, :])
            dv_ref[DV_CONV_B + 3:DV_CONV_B + 4, cs] += colsum(du * xb)

    rt = lambda i: n_t - 1 - i
    row_spec = pl.BlockSpec((tm, d), lambda i: (rt(i), 0))
    slab = lambda s: pl.BlockSpec((tm, d), lambda i, s=s: (rt(i), s))
    before = lambda s: pl.BlockSpec((halo, d), lambda i, s=s: (jnp.maximum(rt(i) * (tm // halo) - 1, 0), s))
    small = pl.BlockSpec((SUBLANE, d), lambda i: (0, 0))
    bd = pl.BlockSpec((nb, cw, cw), lambda i: (0, 0, 0))
    sd = jax.ShapeDtypeStruct
    wbuf = lambda n: pltpu.VMEM((tm + n * SUBLANE, cw), F32)
    head = lambda n: pltpu.VMEM((n * SUBLANE, d), F32)
    return _hosted_call(
        body, comms, name=name, grid=(n_t,),
        out_shape=[sd((t, N_PROJ * d), BF16), sd((DV_ROWS, d), F32), sd((nb, cw, cw), F32), sd((nb, cw, cw), F32)],
        in_specs=[row_spec, row_spec, ANY,
                  slab(0), slab(1), slab(2), slab(3), slab(4), before(1), before(2), before(3),
                  row_spec, row_spec, before(0), small, small, small, bd, bd],
        out_specs=[pl.BlockSpec((tm, 5 * d), lambda i: (rt(i), 0)), pl.BlockSpec((DV_ROWS, d), lambda i: (0, 0)), bd, bd],
        aliases={2: 0},
        scratch_shapes=[wbuf(ka), wbuf(kb), wbuf(1), wbuf(ka), wbuf(kb), wbuf(1),
                        pltpu.VMEM((tm, cw), F32), pltpu.VMEM((tm, cw), F32), head(ka), head(kb), head(1), head(1)],
        args=(dya, dyb, dp_gates, p, p, p, p, p, p, p, p, u_s, h_s, h_s, caw, cbw, vec, bda, bdx))


def _in_proj_bwd(dp, x, dx1, g_row, gath, grp, name, comms=()):
    t, d = x.shape
    tm = _row_tile(t, 512)
    n_in = N_PROJ * d

    def body(dp_ref, x_ref, dx1_ref, g_ref, gath_ref, dx_ref, dg_ref, w_in, sems):
        @pl.when(pl.program_id(0) == 0)
        def _():
            _load_weights(gath_ref, grp, ["in"], [w_in], sems)
            _zero(dg_ref)

        dh = _dot_nn(dp_ref[:, 0:d], w_in[0:d, :])
        for k in range(1, N_PROJ):
            dh = dh + _dot_nn(dp_ref[:, k * d:(k + 1) * d], w_in[k * d:(k + 1) * d, :])
        xf = x_ref[...]
        rstd = lax.rsqrt(jnp.mean(xf * xf, axis=-1, keepdims=True) + RMS_EPS)
        xh = xf * rstd
        g = g_ref[...]
        dg_ref[0:1, :] += jnp.sum(dh * xh, axis=0, keepdims=True)
        dxh = dh * g
        dx_ref[...] = dx1_ref[...] + rstd * (dxh - xh * jnp.mean(dxh * xh, axis=-1, keepdims=True))

    row = pl.BlockSpec((tm, d), lambda i: (i, 0))
    sd = jax.ShapeDtypeStruct
    return _hosted_call(
        body, comms, name=name, grid=(t // tm,),
        out_shape=[sd((t, d), F32), sd((SUBLANE, d), F32)],
        in_specs=[pl.BlockSpec((tm, n_in), lambda i: (i, 0)), row, row, pl.BlockSpec((1, d), lambda i: (0, 0)), ANY],
        out_specs=[row, pl.BlockSpec((SUBLANE, d), lambda i: (0, 0))],
        scratch_shapes=[pltpu.VMEM((n_in, d), BF16), pltpu.SemaphoreType.DMA((N_DEV,))],
        args=(dp, x, dx1, g_row, gath))


def _weight_grad(a, b, name):
    t, m = a.shape
    n = b.shape[1]
    bm = m
    for div in (1, 2, 4, 8):
        if m % div == 0 and (m // div) % LANE == 0 and (m // div) * n * 4 <= (12 << 20):
            bm = m // div
            break
    fixed = bm * n * (4 + 2 * 2)
    bt = next(_row_tile(t, want) for want in (2048, 1024, 512)
              if 2 * 2 * _row_tile(t, want) * (bm + n) + fixed <= VMEM_LIMIT - (12 << 20))
    n_t = t // bt

    def body(a_ref, b_ref, o_ref, acc):
        k = pl.program_id(1)

        @pl.when(k == 0)
        def _():
            _zero(acc)

        acc[...] += _dot_tn(a_ref[...], b_ref[...])

        @pl.when(k == n_t - 1)
        def _():
            o_ref[...] = acc[...].astype(BF16)

    return pl.pallas_call(
        body, name=name, grid=(m // bm, n_t),
        out_shape=jax.ShapeDtypeStruct((m, n), BF16),
        in_specs=[pl.BlockSpec((bt, bm), lambda i, k: (k, i)), pl.BlockSpec((bt, n), lambda i, k: (k, 0))],
        out_specs=pl.BlockSpec((bm, n), lambda i, k: (i, 0)),
        scratch_shapes=[pltpu.VMEM((bm, n), F32)],
        compiler_params=_params(2),
    )(a, b)


def _adamw(w, g, m, v, name):
    r, c = w.shape
    tr = _fit_rows(r, c * 4)
    c1 = 1.0 - ADAM_B1 ** ADAM_STEP
    c2 = 1.0 - ADAM_B2 ** ADAM_STEP

    def body(w_ref, g_ref, m_ref, v_ref, d_ref, nm_ref, nv_ref):
        g32 = g_ref[...]
        nm = ADAM_B1 * m_ref[...] + (1.0 - ADAM_B1) * g32
        nv = ADAM_B2 * v_ref[...] + (1.0 - ADAM_B2) * (g32 * g32)
        nm_ref[...] = nm
        nv_ref[...] = nv
        d_ref[...] = -ADAM_LR * ((nm / c1) / (jnp.sqrt(nv / c2) + ADAM_EPS) + ADAM_WD * w_ref[...])

    spec = pl.BlockSpec((tr, c), lambda i: (i, 0))
    return pl.pallas_call(
        body, name=name, grid=(r // tr,),
        out_shape=[jax.ShapeDtypeStruct((r, c), F32)] * 3,
        in_specs=[spec] * 4, out_specs=[spec] * 3,
        compiler_params=_params(),
    )(w, g, m, v)


def _pad_rows(a, mult=SUBLANE):
    pad = (-a.shape[0]) % mult
    return a if pad == 0 else jnp.concatenate([a, jnp.zeros((pad,) + a.shape[1:], a.dtype)], axis=0)


REPLICATED = ("ln1_g", "conv_b_b", "lru_wa", "lru_ba", "lru_wx", "lru_bx", "lru_lambda", "ln2_g", "final_g")
SMALL_SHARDED = ("conv_a_w", "conv_b_w", "gate_bias")
MATRICES = ("w_in", "w_out_a", "w_out_b", "w_o", "w_ffn_gate", "w_ffn_up", "w_ffn_down")
ORDER = ("ln1_g", "w_in", "conv_a_w", "conv_b_w", "conv_b_b", "lru_wa", "lru_ba", "lru_wx", "lru_bx", "lru_lambda",
         "w_out_a", "w_out_b", "gate_bias", "w_o", "ln2_g", "w_ffn_gate", "w_ffn_up", "w_ffn_down", "final_g")


def kernel(x, ln1_g, w_in, conv_a_w, conv_b_w, conv_b_b, lru_wa, lru_ba, lru_wx, lru_bx, lru_lambda, w_out_a, w_out_b, gate_bias, w_o, ln2_g, w_ffn_gate, w_ffn_up, w_ffn_down, final_g, loss_target, m_ln1_g, m_w_in, m_conv_a_w, m_conv_b_w, m_conv_b_b, m_lru_wa, m_lru_ba, m_lru_wx, m_lru_bx, m_lru_lambda, m_w_out_a, m_w_out_b, m_gate_bias, m_w_o, m_ln2_g, m_w_ffn_gate, m_w_ffn_up, m_w_ffn_down, m_final_g, v_ln1_g, v_w_in, v_conv_a_w, v_conv_b_w, v_conv_b_b, v_lru_wa, v_lru_ba, v_lru_wx, v_lru_bx, v_lru_lambda, v_w_out_a, v_w_out_b, v_gate_bias, v_w_o, v_ln2_g, v_w_ffn_gate, v_w_ffn_up, v_w_ffn_down, v_final_g):
    w = dict(ln1_g=ln1_g, w_in=w_in, conv_a_w=conv_a_w, conv_b_w=conv_b_w, conv_b_b=conv_b_b, lru_wa=lru_wa,
             lru_ba=lru_ba, lru_wx=lru_wx, lru_bx=lru_bx, lru_lambda=lru_lambda, w_out_a=w_out_a, w_out_b=w_out_b,
             gate_bias=gate_bias, w_o=w_o, ln2_g=ln2_g, w_ffn_gate=w_ffn_gate, w_ffn_up=w_ffn_up,
             w_ffn_down=w_ffn_down, final_g=final_g)
    mom = dict(ln1_g=m_ln1_g, w_in=m_w_in, conv_a_w=m_conv_a_w, conv_b_w=m_conv_b_w, conv_b_b=m_conv_b_b,
               lru_wa=m_lru_wa, lru_ba=m_lru_ba, lru_wx=m_lru_wx, lru_bx=m_lru_bx, lru_lambda=m_lru_lambda,
               w_out_a=m_w_out_a, w_out_b=m_w_out_b, gate_bias=m_gate_bias, w_o=m_w_o, ln2_g=m_ln2_g,
               w_ffn_gate=m_w_ffn_gate, w_ffn_up=m_w_ffn_up, w_ffn_down=m_w_ffn_down, final_g=m_final_g)
    var = dict(ln1_g=v_ln1_g, w_in=v_w_in, conv_a_w=v_conv_a_w, conv_b_w=v_conv_b_w, conv_b_b=v_conv_b_b,
               lru_wa=v_lru_wa, lru_ba=v_lru_ba, lru_wx=v_lru_wx, lru_bx=v_lru_bx, lru_lambda=v_lru_lambda,
               w_out_a=v_w_out_a, w_out_b=v_w_out_b, gate_bias=v_gate_bias, w_o=v_w_o, ln2_g=v_ln2_g,
               w_ffn_gate=v_w_ffn_gate, w_ffn_up=v_w_ffn_up, w_ffn_down=v_w_ffn_down, final_g=v_final_g)

    _, t, d = x.shape
    n_layers = w_in.shape[0]
    ff = w_ffn_down.shape[1] * N_DEV
    dd = d // N_DEV
    hd = d // LRU_HEADS
    cw = min(MXU_TILE, d)
    nb = d // cw
    hpt = cw // hd
    grp = _groups(d, ff)
    me = 4 * lax.axis_index("x") + 2 * lax.axis_index("y") + lax.axis_index("c")
    tm_time = _time_tile(t)
    x0 = _to_tile_order(x[0], tm_time)
    target = _to_tile_order(loss_target[0], tm_time)

    packed = [{"in": jnp.swapaxes(w_in[l], 0, 1).astype(BF16),
               "rest": jnp.concatenate([w_out_a[l], w_out_b[l], w_o[l], jnp.swapaxes(w_ffn_gate[l], 0, 1),
                                        jnp.swapaxes(w_ffn_up[l], 0, 1), w_ffn_down[l]], axis=0).astype(BF16)}
              for l in range(n_layers)]
    n_small = CONV_A_K + CONV_B_K + 2
    small = _pad_rows(jnp.concatenate([conv_a_w, conv_b_w, gate_bias], axis=1).reshape(n_layers * n_small, dd))
    sp = jax.nn.softplus(-lru_lambda)
    vec = [_pad_rows(jnp.stack([conv_b_b[l], lru_ba[l], lru_bx[l], sp[l]])) for l in range(n_layers)]
    eye = jnp.eye(hpt, dtype=F32)

    def block_diag(wh):
        return jnp.einsum("jkab,kl->jkalb", wh.reshape(nb, hpt, hd, hd), eye).reshape(nb, cw, cw).astype(BF16)

    bda = [block_diag(lru_wa[l]) for l in range(n_layers)]
    bdx = [block_diag(lru_wx[l]) for l in range(n_layers)]

    gath = [dict() for _ in range(n_layers)]
    (gath[0]["in"],), (small_g,) = _comm_call([_Gather(packed[0]["in"]), _Gather(small)], "gather_in_0")
    small_full = jnp.swapaxes(small_g[:, :n_layers * n_small], 0, 1).reshape(n_layers, n_small, d)
    caw = [_pad_rows(small_full[k, 0:CONV_A_K]) for k in range(n_layers)]
    cbw = [_pad_rows(small_full[k, CONV_A_K:CONV_A_K + CONV_B_K]) for k in range(n_layers)]
    gbias = [_pad_rows(small_full[k, CONV_A_K + CONV_B_K:]) for k in range(n_layers)]
    saved = []
    xl = x0
    for l in range(n_layers):
        more = l + 1 < n_layers
        comms = ([_Gather(packed[0]["rest"])] if l == 0 else []) + ([_Gather(packed[l + 1]["in"])] if more else [])
        (p, h1b, ya, yb, *kept), got = _in_proj_mixer_fwd(
            xl, ln1_g[l][None], gath[l]["in"], grp["in"], caw[l], cbw[l], vec[l], bda[l], bdx[l], f"in_proj_mixer_fwd_{l}", comms)
        if l == 0:
            gath[0]["rest"] = got[0][0]
        if more:
            gath[l + 1]["in"] = got[-1][0]
        (x1, oa, ob), _ = _merge_fwd(xl, ya, yb, p, gbias[l], gath[l]["rest"], grp["rest"], f"merge_fwd_{l}")
        (x2, gg, uu), got = _ffn_fwd(x1, ln2_g[l][None], gath[l]["rest"], grp["rest"], f"ffn_fwd_{l}",
                                     [_Gather(packed[l + 1]["rest"])] if more else [])
        if more:
            ((gath[l + 1]["rest"],),) = got
        saved.append(dict(x=xl, p=p, h1b=h1b, ya=ya, yb=yb, mixer=kept, x1=x1, oa=oa, ob=ob, gg=gg, uu=uu))
        xl = x2
    loss_tile, dx, dfinal = _loss_head(xl, final_g[None], target, "loss_head")
    loss = lax.psum(loss_tile[0, 0], ("x", "y", "c"))

    def heads(dwb):
        blocks = jnp.diagonal(dwb.reshape(nb, hpt, hd, hpt, hd), axis1=1, axis2=3)
        return jnp.moveaxis(blocks, 3, 1).reshape(hd, d)

    layer_names = [n for n in REPLICATED if n != "final_g"] + list(SMALL_SHARDED)

    def layer_block(k):
        return jnp.concatenate([small_grads[k][n] for n in layer_names], axis=0)

    recv = [dict() for _ in range(n_layers)]
    small_grads = [None] * n_layers
    early_all = None
    xg = {"d": _Group(("d",), (ff // N_DEV,)), "gu": _Group(("g", "u"), (ff // N_DEV,) * 2),
          "out": _Group(("oa", "ob", "o"), (dd,) * 3), "in": grp["in"]}
    far_in = None
    for l in reversed(range(n_layers)):
        s = saved[l]
        (dgg, duu, dw_d), got = _ffn_bwd_act(dx, s["gg"], s["uu"], gath[l]["rest"], grp["rest"], f"ffn_bwd_act_{l}",
                                             [far_in] if far_in else [])
        if far_in:
            recv[l + 1]["in"].append(got[0][0])
        (dx1, dx1b, h2b, dln2), got = _ffn_bwd_in(dgg, duu, dx, s["x1"], ln2_g[l][None], gath[l]["rest"], grp["rest"],
                                                  f"ffn_bwd_in_{l}", [_Exchange({"d": dw_d}, xg["d"])])
        recv[l]["d"] = [got[0][0]]
        dw_gu = {"g": _weight_grad(dgg, h2b, f"dw_ffn_gate_{l}"), "u": _weight_grad(duu, h2b, f"dw_ffn_up_{l}")}
        (dya, dyb, dp_gates, dgbias, dw_oa, dw_ob, dw_o), got = _merge_bwd(
            dx1b, s["oa"], s["ob"], s["ya"], s["yb"], s["p"], gbias[l], gath[l]["rest"], grp["rest"], f"merge_bwd_{l}",
            [_Exchange(dw_gu, xg["gu"])])
        recv[l]["gu"] = [got[0][0]]
        comms = [_Exchange({"oa": dw_oa, "ob": dw_ob, "o": dw_o}, xg["out"])]
        if l == 0:
            early = [layer_block(k) for k in range(1, n_layers)] + [_pad_rows(dfinal[0:1])]
            comms.append(_Gather(jnp.concatenate(early, axis=0)))
        (dp, dv, dwa, dwx), got = _mixer_bwd(dya, dyb, dp_gates, s["p"], *s["mixer"], caw[l], cbw[l], vec[l], bda[l], bdx[l],
                                             f"mixer_bwd_{l}", comms)
        recv[l]["out"] = [got[0][0]]
        if l == 0:
            early_all = got[1][0]
        small_grads[l] = {
            "conv_b_b": dv[DV_CONV_B_B:DV_CONV_B_B + 1], "lru_wa": heads(dwa),
            "lru_ba": dv[DV_BA:DV_BA + 1], "lru_wx": heads(dwx), "lru_bx": dv[DV_BX:DV_BX + 1],
            "lru_lambda": dv[DV_SP:DV_SP + 1] * (-jax.nn.sigmoid(-lru_lambda[l]))[None], "ln2_g": dln2[0:1],
            "conv_a_w": dv[DV_CONV_A:DV_CONV_A + CONV_A_K], "conv_b_w": dv[DV_CONV_B:DV_CONV_B + CONV_B_K],
            "gate_bias": dgbias[0:2],
        }
        dw_in = {"in": _weight_grad(dp, s["h1b"], f"dw_in_{l}")}
        if l > 0:
            near_in, far_in = _Exchange(dw_in, xg["in"], NEAR_PEERS), _Exchange(dw_in, xg["in"], FAR_PEERS, local=False)
        else:
            near_in, far_in = _Exchange(dw_in, xg["in"]), None
        (dx, dln1), got = _in_proj_bwd(dp, s["x"], dx1, ln1_g[l][None], gath[l]["in"], grp["in"], f"in_proj_bwd_{l}", [near_in])
        recv[l]["in"] = [got[0][0]]
        small_grads[l]["ln1_g"] = dln1[0:1]
    grad_x = _from_tile_order(dx, tm_time)[None]

    g = {}
    gsum = [{k: _sum_slots(recv[l][k], f"sum_{k}_{l}") for k in xg} for l in range(n_layers)]

    def part(key):
        k = next(name for name, group in xg.items() if key in group.keys)
        o, r = xg[k].off[key], xg[k].rows[key]
        return jnp.stack([gsum[l][k][o:o + r] for l in range(n_layers)])

    g = {"w_in": jnp.swapaxes(part("in"), 1, 2), "w_out_a": part("oa"), "w_out_b": part("ob"), "w_o": part("o"),
         "w_ffn_gate": jnp.swapaxes(part("g"), 1, 2), "w_ffn_up": jnp.swapaxes(part("u"), 1, 2), "w_ffn_down": part("d")}
    ((late_all,),) = _comm_call([_Gather(layer_block(0).astype(BF16))], "gather_small_grads_0")
    early_sum = _sum_slots([early_all], "sum_small_grads")
    block_rows = late_all.shape[1]
    per_layer = [_sum_slots([late_all], "sum_small_grads_0")]
    per_layer += [early_sum[(k - 1) * block_rows:k * block_rows] for k in range(1, n_layers)]
    g["final_g"] = early_sum[(n_layers - 1) * block_rows].reshape(w["final_g"].shape)
    o = 0
    for n in layer_names:
        rows = small_grads[0][n].shape[0]
        stacked = jnp.concatenate([per_layer[k][o:o + rows] for k in range(n_layers)], axis=0)
        if n in SMALL_SHARDED:
            g[n] = lax.dynamic_slice_in_dim(stacked, me * dd, dd, axis=1).reshape(n_layers, rows, dd)
        else:
            g[n] = stacked.reshape(w[n].shape)
        o += rows

    delta, new_m, new_v = {}, {}, {}
    gate_maps = ("lru_wa", "lru_wx")
    for n in MATRICES + gate_maps:
        shape = w[n].shape
        flat = lambda a: a.reshape(-1, d if n in gate_maps else shape[-1])
        dl, nm, nv = _adamw(flat(w[n]), flat(g[n]), flat(mom[n]), flat(var[n]), f"adamw_{n}")
        delta[n], new_m[n], new_v[n] = dl.reshape(shape), nm.reshape(shape), nv.reshape(shape)
    vectors = tuple(n for n in REPLICATED if n not in gate_maps)
    for group, width, name in ((vectors, d, "adamw_replicated"), (SMALL_SHARDED, dd, "adamw_vectors")):
        cat = lambda src: _pad_rows(jnp.concatenate([src[n].reshape(-1, width) for n in group], axis=0))
        dl, nm, nv = _adamw(cat(w), cat(g), cat(mom), cat(var), name)
        o = 0
        for n in group:
            rows = w[n].size // width
            delta[n], new_m[n], new_v[n] = (a[o:o + rows].reshape(w[n].shape) for a in (dl, nm, nv))
            o += rows

    return (loss, grad_x, *[g[n] for n in ORDER], *[delta[n] for n in ORDER], *[new_m[n] for n in ORDER],
            *[new_v[n] for n in ORDER])
```

```python
import math

import jax
import jax.numpy as jnp
from jax import lax
from jax.experimental import pallas as pl
from jax.experimental.pallas import tpu as pltpu

F32 = jnp.float32
BF16 = jnp.bfloat16

N_DEV = 8
N_PROJ = 7
LRU_HEADS = 16
LRU_C = 8.0
RMS_EPS = 1e-6
CONV_A_K = 3
CONV_B_K = 4
GELU_C = math.sqrt(2.0 / math.pi)
GELU_A = 0.044715

ADAM_LR = 0.001
ADAM_B1 = 0.9
ADAM_B2 = 0.999
ADAM_EPS = 1e-08
ADAM_WD = 0.01
ADAM_STEP = 10

LANE = 128
SUBLANE = 8
MXU_TILE = 256
VMEM_LIMIT = 52 << 20
ALL_PEERS = tuple(range(1, N_DEV))
NEAR_PEERS = (1, 2, 3, 4, 5)
FAR_PEERS = (6, 7)
MESH = pl.DeviceIdType.MESH
ANY = pl.BlockSpec(memory_space=pl.ANY)


def _dot_nn(a, b):
    return lax.dot_general(a, b, (((1,), (0,)), ((), ())), preferred_element_type=F32)


def _dot_nt(a, b):
    return lax.dot_general(a, b, (((1,), (1,)), ((), ())), preferred_element_type=F32)


def _dot_tn(a, b):
    return lax.dot_general(a, b, (((0,), (0,)), ((), ())), preferred_element_type=F32)


def _sigmoid(x):
    return 1.0 / (1.0 + jnp.exp(-x))


def _gelu_and_grad(x):
    x2 = x * x
    t = jnp.tanh(GELU_C * x * (1.0 + GELU_A * x2))
    g = 0.5 * x * (1.0 + t)
    dg = 0.5 * (1.0 + t) + 0.5 * x * (1.0 - t * t) * GELU_C * (1.0 + 3.0 * GELU_A * x2)
    return g, dg


def _zero(ref):
    ref[...] = jnp.zeros(ref.shape, ref.dtype)


def _fit_rows(r, row_bytes, budget=1 << 20):
    fits = [t for t in range(16, r + 1, 16) if r % t == 0 and t * row_bytes <= budget]
    return max(fits) if fits else r


def _row_tile(t, want):
    tm = min(want, t // 2)
    assert t % tm == 0 and tm % SUBLANE == 0, (t, tm)
    return tm


def _params(n_grid=1, **kw):
    return pltpu.CompilerParams(dimension_semantics=("arbitrary",) * n_grid, vmem_limit_bytes=VMEM_LIMIT, **kw)


class _Group:
    def __init__(self, keys, rows):
        self.keys = keys
        self.rows = dict(zip(keys, rows))
        self.off, o = {}, 0
        for k in keys:
            self.off[k] = o
            o += self.rows[k]
        self.total = o


def _groups(d, ff):
    dd, ffs = d // N_DEV, ff // N_DEV
    return {"in": _Group(("in",), (N_PROJ * dd,)),
            "rest": _Group(("oa", "ob", "o", "g", "u", "d"), (dd, dd, dd, ffs, ffs, ffs))}


def _load_weights(g_ref, grp, keys, dsts, sems):
    copies = []
    for n, (k, dst) in enumerate(zip(keys, dsts)):
        rows, off = grp.rows[k], grp.off[k]
        copies += [pltpu.make_async_copy(g_ref.at[p, pl.ds(off, rows), :], dst.at[pl.ds(p * rows, rows), :],
                                         sems.at[n * N_DEV + p]) for p in range(N_DEV)]
    for c in copies:
        c.start()
    for c in copies:
        c.wait()


def _comm_sems():
    return [pltpu.SemaphoreType.DMA((N_DEV - 1,)), pltpu.SemaphoreType.DMA((N_DEV - 1,)), pltpu.SemaphoreType.DMA]


class _Gather:
    def __init__(self, x):
        self.inputs = [x]
        self.out_shape = [jax.ShapeDtypeStruct((N_DEV,) + x.shape, x.dtype)]
        self.scratch = _comm_sems()

    def _plan(self, ins, outs, scr):
        (x_ref,), (out_ref,), (send_sems, recv_sems, local_sem) = ins, outs, scr
        mx, my, mc = lax.axis_index("x"), lax.axis_index("y"), lax.axis_index("c")
        me, sibling = (mx, my, mc), (mx, my, 1 - mc)
        xn, yn, dg = (1 - mx, my), (mx, 1 - my), (1 - mx, 1 - my)
        core0 = mc == 0
        relayed = (jnp.where(core0, 1 - mx, mx), jnp.where(core0, my, 1 - my))
        relay_to = (jnp.where(core0, mx, 1 - mx), jnp.where(core0, 1 - my, my))

        def slot(px, py, pc):
            return out_ref.at[4 * px + 2 * py + pc]

        def copy(k, block, to, src=None):
            return pltpu.make_async_remote_copy(
                src_ref=slot(*block) if src is None else src, dst_ref=slot(*block),
                send_sem=send_sems.at[k], recv_sem=recv_sems.at[k], device_id=to, device_id_type=MESH)

        mine = lambda: pltpu.make_async_copy(x_ref, slot(*me), local_sem)
        own = [lambda: copy(0, me, sibling, src=x_ref), lambda: copy(1, me, (*xn, mc), src=x_ref),
               lambda: copy(2, me, (*yn, mc), src=x_ref)]
        relay = lambda: copy(3, (*relayed, mc), (*relay_to, mc))
        passes = [lambda: copy(4, (*xn, mc), sibling), lambda: copy(5, (*yn, mc), sibling), lambda: copy(6, (*dg, mc), sibling)]
        arrival = lambda k: copy(k, me, me)
        return mine, own, relay, passes, arrival

    def start(self, ins, outs, scr):
        mine, own, _, _, _ = self._plan(ins, outs, scr)
        mine().start()
        for cp in own:
            cp().start()

    def relay(self, ins, outs, scr):
        _, _, relay, passes, arrival = self._plan(ins, outs, scr)
        arrival(1).wait_recv()
        arrival(2).wait_recv()
        relay().start()
        passes[0]().start()
        passes[1]().start()

    def mid(self, ins, outs, scr):
        _, _, _, passes, arrival = self._plan(ins, outs, scr)
        arrival(3).wait_recv()
        passes[2]().start()

    def finish(self, ins, outs, scr):
        mine, _, _, _, arrival = self._plan(ins, outs, scr)
        for k in (0, 4, 5, 6):
            arrival(k).wait_recv()
        for k in range(N_DEV - 1):
            arrival(k).wait_send()
        mine().wait()


class _Exchange:
    def __init__(self, mats, grp, peers=ALL_PEERS, local=True):
        self.grp, self.peers, self.local = grp, tuple(peers), local
        self.inputs = [mats[k] for k in grp.keys]
        slots = len(self.peers) + (1 if local else 0)
        self.out_shape = [jax.ShapeDtypeStruct((slots, grp.total, self.inputs[0].shape[1]), BF16)]
        self.scratch = [pltpu.SemaphoreType.DMA((len(self.peers),)), pltpu.SemaphoreType.DMA((len(self.peers),)),
                        pltpu.SemaphoreType.DMA]

    def _pieces(self, g_refs, out_ref, q, dst_slot):
        out = []
        for g_ref, k in zip(g_refs, self.grp.keys):
            rows = self.grp.rows[k]
            out.append((g_ref.at[pl.ds(pl.multiple_of(q * rows, 16), rows), :],
                        out_ref.at[dst_slot, pl.ds(self.grp.off[k], rows), :]))
        return out

    def start(self, ins, outs, scr):
        (out_ref,), (send_sems, recv_sems, local_sem) = outs, scr
        mx, my, mc = lax.axis_index("x"), lax.axis_index("y"), lax.axis_index("c")
        if self.local:
            for s, t in self._pieces(ins, out_ref, 4 * mx + 2 * my + mc, 0):
                pltpu.make_async_copy(s, t, local_sem).start()
        for n, k in enumerate(self.peers):
            px, py, pc = mx ^ ((k >> 2) & 1), my ^ ((k >> 1) & 1), mc ^ (k & 1)
            for s, t in self._pieces(ins, out_ref, 4 * px + 2 * py + pc, n + (1 if self.local else 0)):
                pltpu.make_async_remote_copy(src_ref=s, dst_ref=t, send_sem=send_sems.at[n], recv_sem=recv_sems.at[n],
                                             device_id=(px, py, pc), device_id_type=MESH).start()

    def relay(self, ins, outs, scr):
        pass

    def mid(self, ins, outs, scr):
        pass

    def finish(self, ins, outs, scr):
        (out_ref,), (send_sems, recv_sems, local_sem) = outs, scr
        mx, my, mc = lax.axis_index("x"), lax.axis_index("y"), lax.axis_index("c")
        whole = out_ref.at[0]
        for n in range(len(self.peers)):
            done = pltpu.make_async_remote_copy(src_ref=whole, dst_ref=whole, send_sem=send_sems.at[n],
                                                recv_sem=recv_sems.at[n], device_id=(mx, my, mc), device_id_type=MESH)
            done.wait_send()
            done.wait_recv()
        if self.local:
            pltpu.make_async_copy(whole, whole, local_sem).wait()


def _split(refs, sizes):
    out, pos = [], 0
    for n in sizes:
        out.append(refs[pos:pos + n])
        pos += n
    return out


def _hosted_call(body, comms, *, name, grid, in_specs, out_specs, out_shape, scratch_shapes, args, aliases=None):
    n_steps = grid[0]
    nc = len(comms)
    sizes = ([len(in_specs)] + [len(c.inputs) for c in comms] + [len(out_specs)] + [len(c.out_shape) for c in comms]
             + [len(scratch_shapes)] + [len(c.scratch) for c in comms])

    def hosted(*refs):
        parts = _split(refs, sizes)
        ins, c_ins = parts[0], parts[1:1 + nc]
        outs, c_outs = parts[1 + nc], parts[2 + nc:2 + 2 * nc]
        scr, c_scr = parts[2 + 2 * nc], parts[3 + 2 * nc:]
        step = pl.program_id(0)
        if comms:
            @pl.when(step == 0)
            def _():
                for c, a, b, s in zip(comms, c_ins, c_outs, c_scr):
                    c.start(a, b, s)

            relay_step = (3 * n_steps) // 5

            @pl.when(step == relay_step)
            def _():
                for c, a, b, s in zip(comms, c_ins, c_outs, c_scr):
                    c.relay(a, b, s)

            @pl.when(step == max(n_steps - 2, relay_step))
            def _():
                for c, a, b, s in zip(comms, c_ins, c_outs, c_scr):
                    c.mid(a, b, s)

        body(*ins, *outs, *scr)
        if comms:
            @pl.when(step == n_steps - 1)
            def _():
                for c, a, b, s in zip(comms, c_ins, c_outs, c_scr):
                    c.finish(a, b, s)

    res = pl.pallas_call(
        hosted, name=name, grid=grid,
        out_shape=[*out_shape, *[o for c in comms for o in c.out_shape]],
        in_specs=[*in_specs, *[ANY for c in comms for _ in c.inputs]],
        out_specs=[*out_specs, *[ANY for c in comms for _ in c.out_shape]],
        scratch_shapes=[*scratch_shapes, *[s for c in comms for s in c.scratch]],
        input_output_aliases=aliases or {},
        compiler_params=_params(),
    )(*args, *[a for c in comms for a in c.inputs])
    main, rest = res[:len(out_specs)], res[len(out_specs):]
    return main, _split(rest, [len(c.out_shape) for c in comms])


def _comm_call(comms, name):
    sizes = [len(c.inputs) for c in comms] + [len(c.out_shape) for c in comms] + [len(c.scratch) for c in comms]
    nc = len(comms)

    def body(*refs):
        parts = _split(refs, sizes)
        triples = list(zip(comms, parts[:nc], parts[nc:2 * nc], parts[2 * nc:]))
        for phase in ("start", "relay", "mid", "finish"):
            for c, ins, outs, scr in triples:
                getattr(c, phase)(ins, outs, scr)

    res = pl.pallas_call(
        body, name=name, out_shape=[o for c in comms for o in c.out_shape],
        in_specs=[ANY for c in comms for _ in c.inputs], out_specs=[ANY for c in comms for _ in c.out_shape],
        scratch_shapes=[s for c in comms for s in c.scratch],
    )(*[a for c in comms for a in c.inputs])
    return _split(res, [len(c.out_shape) for c in comms])


def _sum_slots(xs, name):
    _, r, c = xs[0].shape
    tr = _fit_rows(r, c * 4)

    def body(*refs):
        acc = None
        for x_ref in refs[:-1]:
            for p in range(x_ref.shape[0]):
                v = x_ref[p].astype(F32)
                acc = v if acc is None else acc + v
        refs[-1][...] = acc

    return pl.pallas_call(
        body, name=name, grid=(r // tr,),
        out_shape=jax.ShapeDtypeStruct((r, c), F32),
        in_specs=[pl.BlockSpec((x.shape[0], tr, c), lambda i: (0, i, 0)) for x in xs],
        out_specs=pl.BlockSpec((tr, c), lambda i: (i, 0)),
        compiler_params=_params(),
    )(*xs)


def _time_tile(t):
    return _row_tile(t, 256)


def _to_tile_order(a, tm):
    t, c = a.shape
    return jnp.swapaxes(a.reshape(t // tm, SUBLANE, tm // SUBLANE, c), 1, 2).reshape(t, c)


def _from_tile_order(a, tm):
    t, c = a.shape
    return jnp.swapaxes(a.reshape(t // tm, tm // SUBLANE, SUBLANE, c), 1, 2).reshape(t, c)


def _causal_fill(buf, v, prev_tail, n, row):
    tm = v.shape[0]
    for q in range(n):
        cur = v[tm - SUBLANE * (n - q):tm - SUBLANE * (n - q - 1), :]
        prv = prev_tail[SUBLANE * q:SUBLANE * (q + 1), :]
        buf[SUBLANE * q:SUBLANE * (q + 1), :] = jnp.where(row == 0, pltpu.roll(prv, 1, 0), pltpu.roll(cur, 1, 0))
    buf[SUBLANE * n:, :] = v


def _anticausal_fill(buf, v, next_head, n, row):
    tm = v.shape[0]
    buf[0:tm, :] = v
    for q in range(n):
        cur = v[SUBLANE * q:SUBLANE * (q + 1), :]
        nxt = next_head[SUBLANE * q:SUBLANE * (q + 1), :]
        buf[tm + SUBLANE * q:tm + SUBLANE * (q + 1), :] = jnp.where(
            row == SUBLANE - 1, pltpu.roll(nxt, SUBLANE - 1, 0), pltpu.roll(cur, SUBLANE - 1, 0))


def _chain_scan(abuf, bbuf, nk, reverse):
    cw = abuf.shape[1]

    def step(n, carry):
        h, c = carry
        r0 = pl.multiple_of((nk - 1 - n if reverse else n) * SUBLANE, SUBLANE)
        ak = abuf[pl.ds(r0, SUBLANE), :]
        h = ak * h + bbuf[pl.ds(r0, SUBLANE), :]
        c = ak * c
        bbuf[pl.ds(r0, SUBLANE), :] = h
        abuf[pl.ds(r0, SUBLANE), :] = c
        return h, c

    return lax.fori_loop(0, nk, step, (jnp.zeros((SUBLANE, cw), F32), jnp.ones((SUBLANE, cw), F32)), unroll=True)


def _sublane_scan(a, b, row, reverse):
    for sh in (1, 2, 4):
        if reverse:
            m = row < SUBLANE - sh
            b = jnp.where(m, a * pltpu.roll(b, SUBLANE - sh, 0) + b, b)
            a = jnp.where(m, a * pltpu.roll(a, SUBLANE - sh, 0), a)
        else:
            m = row >= sh
            b = jnp.where(m, a * pltpu.roll(b, sh, 0) + b, b)
            a = jnp.where(m, a * pltpu.roll(a, sh, 0), a)
    return a, b


def _lru_gates(ub, bda, bdx, ba, bx, sp):
    r = _sigmoid(_dot_nn(ub, bda) + ba)
    i = _sigmoid(_dot_nn(ub, bdx) + bx)
    log_a = (-LRU_C) * r * sp
    a = jnp.exp(log_a)
    s2 = -jnp.tanh(log_a) * (1.0 + a * a)
    inv_s = lax.rsqrt(s2)
    s = jnp.where(s2 > 0.0, s2 * inv_s, 0.0)
    return r, i, a, s, inv_s


def _in_proj_mixer_fwd(x, g_row, gath, grp, caw, cbw, vec, bda, bdx, name, comms=()):
    t, d = x.shape
    n_in = N_PROJ * d
    tm = _time_tile(t)
    nk = tm // SUBLANE
    cw = min(MXU_TILE, d)
    nb = d // cw

    def body(x_ref, g_ref, gath_ref, caw_ref, cbw_ref, vec_ref, bda_ref, bdx_ref,
             p_ref, h1_ref, ya_ref, yb_ref, u_ref, h_ref, w_in, sems, zbuf, xbuf, abuf, bbuf, z_tail, x_tail, h_carry):
        @pl.when(pl.program_id(0) == 0)
        def _():
            _load_weights(gath_ref, grp, ["in"], [w_in], sems)
            _zero(z_tail)
            _zero(x_tail)
            _zero(h_carry)

        def project_and_mix():
            xf = x_ref[...]
            rstd = lax.rsqrt(jnp.mean(xf * xf, axis=-1, keepdims=True) + RMS_EPS)
            h1 = (xf * rstd * g_ref[...]).astype(BF16)
            h1_ref[...] = h1
            for k in range(N_PROJ):
                p_ref[:, k * d:(k + 1) * d] = _dot_nt(h1, w_in[k * d:(k + 1) * d, :]).astype(BF16)
            row = lax.broadcasted_iota(jnp.int32, (SUBLANE, cw), 0)
            for j in range(nb):
                cs = slice(j * cw, (j + 1) * cw)
                ba_ref, ca_ref, xa_ref, xb_ref, gb_ref = (p_ref.at[:, k * d:(k + 1) * d] for k in range(5))
                z = ca_ref[:, cs].astype(F32) * xa_ref[:, cs].astype(F32)
                _causal_fill(zbuf, z, z_tail[:, cs], CONV_A_K - 1, row)
                z_tail[:, cs] = z[tm - (CONV_A_K - 1) * SUBLANE:, :]
                cz = caw_ref[0:1, cs] * zbuf[0:tm, :] + caw_ref[1:2, cs] * zbuf[SUBLANE:SUBLANE + tm, :] + caw_ref[2:3, cs] * z
                ya_ref[:, cs] = (ba_ref[:, cs].astype(F32) * cz).astype(BF16)
                xb = xb_ref[:, cs].astype(F32)
                _causal_fill(xbuf, xb, x_tail[:, cs], CONV_B_K - 1, row)
                x_tail[:, cs] = xb[tm - (CONV_B_K - 1) * SUBLANE:, :]
                u = (cbw_ref[0:1, cs] * xbuf[0:tm, :] + cbw_ref[1:2, cs] * xbuf[SUBLANE:SUBLANE + tm, :]
                     + cbw_ref[2:3, cs] * xbuf[2 * SUBLANE:2 * SUBLANE + tm, :] + cbw_ref[3:4, cs] * xb + vec_ref[0:1, cs])
                ub = u.astype(BF16)
                u = ub.astype(F32)
                _, gi, a, s, _ = _lru_gates(ub, bda_ref[j], bdx_ref[j], vec_ref[1:2, cs], vec_ref[2:3, cs], vec_ref[3:4, cs])
                abuf[...] = a
                bbuf[...] = s * (gi * u)
                h_end, a_prod = _chain_scan(abuf, bbuf, nk, reverse=False)
                a_inc, h_inc = _sublane_scan(a_prod, h_end, row, reverse=False)
                carry = h_carry[:, cs]
                ends = h_inc + a_inc * carry
                starts = jnp.where(row == 0, carry, pltpu.roll(ends, 1, 0))
                h_carry[:, cs] = jnp.broadcast_to(ends[SUBLANE - 1:SUBLANE, :], (SUBLANE, cw))
                h = (bbuf[...].reshape(nk, SUBLANE, cw) + abuf[...].reshape(nk, SUBLANE, cw) * starts[None]).reshape(tm, cw)
                gel, _ = _gelu_and_grad(gb_ref[:, cs].astype(F32))
                yb_ref[:, cs] = (h * gel).astype(BF16)
                u_ref[:, cs] = ub
                h_ref[:, cs] = h.astype(BF16)

        project_and_mix()

    small = pl.BlockSpec((SUBLANE, d), lambda i: (0, 0))
    bd = pl.BlockSpec((nb, cw, cw), lambda i: (0, 0, 0))
    row_spec = pl.BlockSpec((tm, d), lambda i: (i, 0))
    return _hosted_call(
        body, comms, name=name, grid=(t // tm,),
        out_shape=[jax.ShapeDtypeStruct((t, n_in), BF16)] + [jax.ShapeDtypeStruct((t, d), BF16)] * 5,
        in_specs=[row_spec, pl.BlockSpec((1, d), lambda i: (0, 0)), ANY, small, small, small, bd, bd],
        out_specs=[pl.BlockSpec((tm, n_in), lambda i: (i, 0))] + [row_spec] * 5,
        scratch_shapes=[pltpu.VMEM((n_in, d), BF16), pltpu.SemaphoreType.DMA((N_DEV,)),
                        pltpu.VMEM((tm + (CONV_A_K - 1) * SUBLANE, cw), F32), pltpu.VMEM((tm + (CONV_B_K - 1) * SUBLANE, cw), F32),
                        pltpu.VMEM((tm, cw), F32), pltpu.VMEM((tm, cw), F32),
                        pltpu.VMEM(((CONV_A_K - 1) * SUBLANE, d), F32), pltpu.VMEM(((CONV_B_K - 1) * SUBLANE, d), F32),
                        pltpu.VMEM((SUBLANE, d), F32)],
        args=(x, g_row, gath, caw, cbw, vec, bda, bdx))


def _merge_fwd(x, ya, yb, p, gbias, gath, grp, name, comms=()):
    t, d = x.shape
    tm = _row_tile(t, 512)

    def body(x_ref, ya_ref, yb_ref, ga_ref, gb_ref, gbias_ref, gath_ref, x1_ref, oa_ref, ob_ref, w_oa, w_ob, w_o, sems):
        @pl.when(pl.program_id(0) == 0)
        def _():
            _load_weights(gath_ref, grp, ["oa", "ob", "o"], [w_oa, w_ob, w_o], sems)

        oa = _dot_nn(ya_ref[...], w_oa[...]).astype(BF16)
        ob = _dot_nn(yb_ref[...], w_ob[...]).astype(BF16)
        oa_ref[...] = oa
        ob_ref[...] = ob
        sa = _sigmoid(ga_ref[...] + gbias_ref[0:1, :].astype(BF16))
        sb = _sigmoid(gb_ref[...] + gbias_ref[1:2, :].astype(BF16))
        x1_ref[...] = x_ref[...] + _dot_nn(sa * oa + sb * ob, w_o[...])

    row = pl.BlockSpec((tm, d), lambda i: (i, 0))
    return _hosted_call(
        body, comms, name=name, grid=(t // tm,),
        out_shape=[jax.ShapeDtypeStruct((t, d), F32), jax.ShapeDtypeStruct((t, d), BF16), jax.ShapeDtypeStruct((t, d), BF16)],
        in_specs=[row, row, row, pl.BlockSpec((tm, d), lambda i: (i, 5)), pl.BlockSpec((tm, d), lambda i: (i, 6)),
                  pl.BlockSpec((SUBLANE, d), lambda i: (0, 0)), ANY],
        out_specs=[row, row, row],
        scratch_shapes=[pltpu.VMEM((d, d), BF16)] * 3 + [pltpu.SemaphoreType.DMA((3 * N_DEV,))],
        args=(x, ya, yb, p, p, gbias, gath))


def _ffn_fwd(x1, g_row, gath, grp, name, comms=()):
    t, d = x1.shape
    ff = grp.rows["g"] * N_DEV
    tm = _row_tile(t, 512)
    fc = MXU_TILE
    assert ff % fc == 0

    def body(x_ref, g_ref, gath_ref, x2_ref, gg_ref, uu_ref, w_g, w_u, w_d, acc, sems):
        @pl.when(pl.program_id(0) == 0)
        def _():
            _load_weights(gath_ref, grp, ["g", "u", "d"], [w_g, w_u, w_d], sems)

        xf = x_ref[...]
        rstd = lax.rsqrt(jnp.mean(xf * xf, axis=-1, keepdims=True) + RMS_EPS)
        h = (xf * rstd * g_ref[...]).astype(BF16)
        acc[...] = xf
        for c in range(ff // fc):
            fs = slice(c * fc, (c + 1) * fc)
            gg = _dot_nt(h, w_g[fs, :]).astype(BF16)
            uu = _dot_nt(h, w_u[fs, :]).astype(BF16)
            gg_ref[:, fs] = gg
            uu_ref[:, fs] = uu
            acc[...] += _dot_nn(gg * _sigmoid(gg) * uu, w_d[fs, :])
        x2_ref[...] = acc[...]

    row = pl.BlockSpec((tm, d), lambda i: (i, 0))
    wide = pl.BlockSpec((tm, ff), lambda i: (i, 0))
    return _hosted_call(
        body, comms, name=name, grid=(t // tm,),
        out_shape=[jax.ShapeDtypeStruct((t, d), F32), jax.ShapeDtypeStruct((t, ff), BF16), jax.ShapeDtypeStruct((t, ff), BF16)],
        in_specs=[row, pl.BlockSpec((1, d), lambda i: (0, 0)), ANY],
        out_specs=[row, wide, wide],
        scratch_shapes=[pltpu.VMEM((ff, d), BF16)] * 3 + [pltpu.VMEM((tm, d), F32), pltpu.SemaphoreType.DMA((3 * N_DEV,))],
        args=(x1, g_row, gath))


def _loss_head(x, g_row, target, name):
    t, d = x.shape
    tm = _row_tile(t, 512)

    def body(x_ref, g_ref, tgt_ref, loss_ref, dx_ref, dg_ref):
        @pl.when(pl.program_id(0) == 0)
        def _():
            _zero(loss_ref)
            _zero(dg_ref)

        xf = x_ref[...]
        rstd = lax.rsqrt(jnp.mean(xf * xf, axis=-1, keepdims=True) + RMS_EPS)
        xh = xf * rstd
        g = g_ref[...]
        err = xh * g - tgt_ref[...]
        loss_ref[...] += 0.5 * jnp.sum(jnp.sum(err * err, axis=-1, keepdims=True), axis=0, keepdims=True) * (1.0 / d)
        dy = err * (1.0 / d)
        dg_ref[0:1, :] += jnp.sum(dy * xh, axis=0, keepdims=True)
        dxh = dy * g
        dx_ref[...] = rstd * (dxh - xh * jnp.mean(dxh * xh, axis=-1, keepdims=True))

    row = pl.BlockSpec((tm, d), lambda i: (i, 0))
    return pl.pallas_call(
        body, name=name, grid=(t // tm,),
        out_shape=[jax.ShapeDtypeStruct((SUBLANE, LANE), F32), jax.ShapeDtypeStruct((t, d), F32),
                   jax.ShapeDtypeStruct((SUBLANE, d), F32)],
        in_specs=[row, pl.BlockSpec((1, d), lambda i: (0, 0)), row],
        out_specs=[pl.BlockSpec((SUBLANE, LANE), lambda i: (0, 0)), row, pl.BlockSpec((SUBLANE, d), lambda i: (0, 0))],
        compiler_params=_params(),
    )(x, g_row, target)


def _ffn_bwd_act(dx2, gg, uu, gath, grp, name, comms=()):
    t, d = dx2.shape
    ff = grp.rows["g"] * N_DEV
    tm = _row_tile(t, 512)
    fc = MXU_TILE
    n_t = t // tm

    def body(dx2_ref, gg_ref, uu_ref, gath_ref, dgg_ref, duu_ref, dwd_ref, w_d, acc, sems):
        @pl.when(pl.program_id(0) == 0)
        def _():
            _load_weights(gath_ref, grp, ["d"], [w_d], sems)
            _zero(acc)

        dx2b = dx2_ref[...].astype(BF16)
        for c in range(ff // fc):
            fs = slice(c * fc, (c + 1) * fc)
            df = _dot_nt(dx2b, w_d[fs, :]).astype(BF16)
            g = gg_ref[:, fs]
            u = uu_ref[:, fs]
            sg = _sigmoid(g)
            silu = g * sg
            acc[fs, :] += _dot_tn(silu * u, dx2b)
            duu_ref[:, fs] = df * silu
            dgg_ref[:, fs] = df * u * (sg * (1.0 + g * (1.0 - sg)))

        @pl.when(pl.program_id(0) == n_t - 1)
        def _():
            w_d[...] = acc[...].astype(BF16)
            out = pltpu.make_async_copy(w_d, dwd_ref, sems.at[0])
            out.start()
            out.wait()

    row = pl.BlockSpec((tm, d), lambda i: (i, 0))
    wide = pl.BlockSpec((tm, ff), lambda i: (i, 0))
    sd = jax.ShapeDtypeStruct
    return _hosted_call(
        body, comms, name=name, grid=(n_t,),
        out_shape=[sd((t, ff), BF16), sd((t, ff), BF16), sd((ff, d), BF16)],
        in_specs=[row, wide, wide, ANY],
        out_specs=[wide, wide, ANY],
        scratch_shapes=[pltpu.VMEM((ff, d), BF16), pltpu.VMEM((ff, d), F32), pltpu.SemaphoreType.DMA((N_DEV,))],
        args=(dx2, gg, uu, gath))


def _ffn_bwd_in(dgg, duu, dx2, x1, g_row, gath, grp, name, comms=()):
    t, d = x1.shape
    ff = grp.rows["g"] * N_DEV
    tm = _row_tile(t, 512)

    def body(dgg_ref, duu_ref, dx2_ref, x_ref, g_ref, gath_ref, dx1_ref, dx1b_ref, h_ref, dg_ref, w_g, w_u, sems):
        @pl.when(pl.program_id(0) == 0)
        def _():
            _load_weights(gath_ref, grp, ["g", "u"], [w_g, w_u], sems)
            _zero(dg_ref)

        dh = _dot_nn(dgg_ref[...], w_g[...]) + _dot_nn(duu_ref[...], w_u[...])
        xf = x_ref[...]
        rstd = lax.rsqrt(jnp.mean(xf * xf, axis=-1, keepdims=True) + RMS_EPS)
        xh = xf * rstd
        g = g_ref[...]
        h_ref[...] = (xh * g).astype(BF16)
        dg_ref[0:1, :] += jnp.sum(dh * xh, axis=0, keepdims=True)
        dxh = dh * g
        dx1 = dx2_ref[...] + rstd * (dxh - xh * jnp.mean(dxh * xh, axis=-1, keepdims=True))
        dx1_ref[...] = dx1
        dx1b_ref[...] = dx1.astype(BF16)

    row = pl.BlockSpec((tm, d), lambda i: (i, 0))
    wide = pl.BlockSpec((tm, ff), lambda i: (i, 0))
    sd = jax.ShapeDtypeStruct
    return _hosted_call(
        body, comms, name=name, grid=(t // tm,),
        out_shape=[sd((t, d), F32), sd((t, d), BF16), sd((t, d), BF16), sd((SUBLANE, d), F32)],
        in_specs=[wide, wide, row, row, pl.BlockSpec((1, d), lambda i: (0, 0)), ANY],
        out_specs=[row, row, row, pl.BlockSpec((SUBLANE, d), lambda i: (0, 0))],
        scratch_shapes=[pltpu.VMEM((ff, d), BF16)] * 2 + [pltpu.SemaphoreType.DMA((2 * N_DEV,))],
        args=(dgg, duu, dx2, x1, g_row, gath))


def _merge_bwd(dx1b, oa, ob, ya, yb, p, gbias, gath, grp, name, comms=()):
    t, d = oa.shape
    tm = _row_tile(t, 512)
    n_t = t // tm

    def body(dx_ref, oa_ref, ob_ref, ya_ref, yb_ref, ga_ref, gb_ref, gbias_ref, gath_ref,
             dya_ref, dyb_ref, dp_ref, dgb_ref, dwoa_ref, dwob_ref, dwo_ref,
             w_oa, w_ob, w_o, acc_oa, acc_ob, acc_o, stage, sems, out_sems):
        @pl.when(pl.program_id(0) == 0)
        def _():
            _load_weights(gath_ref, grp, ["oa", "ob", "o"], [w_oa, w_ob, w_o], sems)
            for ref in (dgb_ref, acc_oa, acc_ob, acc_o):
                _zero(ref)

        dxb = dx_ref[...]
        dm = _dot_nt(dxb, w_o[...]).astype(BF16)
        oa = oa_ref[...]
        ob = ob_ref[...]
        sa = _sigmoid(ga_ref[...] + gbias_ref[0:1, :].astype(BF16))
        sb = _sigmoid(gb_ref[...] + gbias_ref[1:2, :].astype(BF16))
        acc_o[...] += _dot_tn(sa * oa + sb * ob, dxb)
        doa = dm * sa
        dob = dm * sb
        acc_oa[...] += _dot_tn(ya_ref[...], doa)
        acc_ob[...] += _dot_tn(yb_ref[...], dob)
        dga = dm * oa * sa * (1.0 - sa)
        dgb = dm * ob * sb * (1.0 - sb)
        step = pl.program_id(0)
        slot = step % 2

        def to_dp(k, at_step):
            return pltpu.make_async_copy(stage.at[k], dp_ref.at[pl.ds(at_step * tm, tm), pl.ds(5 * d, 2 * d)], out_sems.at[k])

        @pl.when(step >= 2)
        def _():
            to_dp(slot, step - 2).wait()

        stage[slot, :, 0:d] = dga
        stage[slot, :, d:2 * d] = dgb
        to_dp(slot, step).start()
        ones = jnp.ones((SUBLANE, tm), BF16)
        dgb_ref[0:1, :] += _dot_nn(ones, dga)[0:1, :]
        dgb_ref[1:2, :] += _dot_nn(ones, dgb)[0:1, :]
        dya_ref[...] = _dot_nt(doa, w_oa[...]).astype(BF16)
        dyb_ref[...] = _dot_nt(dob, w_ob[...]).astype(BF16)

        @pl.when(pl.program_id(0) == n_t - 1)
        def _():
            outs = []
            for n, (acc, stage, dst) in enumerate(((acc_oa, w_oa, dwoa_ref), (acc_ob, w_ob, dwob_ref), (acc_o, w_o, dwo_ref))):
                stage[...] = acc[...].astype(BF16)
                outs.append(pltpu.make_async_copy(stage, dst, sems.at[n]))
                outs[-1].start()
            for cp in outs:
                cp.wait()
            for back in range(min(2, n_t)):
                to_dp((n_t - 1 - back) % 2, n_t - 1 - back).wait()

    row = pl.BlockSpec((tm, d), lambda i: (i, 0))
    sd = jax.ShapeDtypeStruct
    return _hosted_call(
        body, comms, name=name, grid=(n_t,),
        out_shape=[sd((t, d), BF16), sd((t, d), BF16), sd((t, N_PROJ * d), BF16), sd((SUBLANE, d), F32),
                   sd((d, d), BF16), sd((d, d), BF16), sd((d, d), BF16)],
        in_specs=[row, row, row, row, row, pl.BlockSpec((tm, d), lambda i: (i, 5)), pl.BlockSpec((tm, d), lambda i: (i, 6)),
                  pl.BlockSpec((SUBLANE, d), lambda i: (0, 0)), ANY],
        out_specs=[row, row, ANY, pl.BlockSpec((SUBLANE, d), lambda i: (0, 0)), ANY, ANY, ANY],
        scratch_shapes=[pltpu.VMEM((d, d), BF16)] * 3 + [pltpu.VMEM((d, d), F32)] * 3
        + [pltpu.VMEM((2, tm, 2 * d), BF16), pltpu.SemaphoreType.DMA((3 * N_DEV,)), pltpu.SemaphoreType.DMA((2,))],
        args=(dx1b, oa, ob, ya, yb, p, p, gbias, gath))


DV_CONV_B_B, DV_BA, DV_BX, DV_SP, DV_CONV_A, DV_CONV_B = 0, 1, 2, 3, 4, 7
DV_ROWS = 16


def _mixer_bwd(dya, dyb, dp_gates, p, u_s, h_s, caw, cbw, vec, bda, bdx, name, comms=()):
    t, d = dya.shape
    tm = _time_tile(t)
    n_t = t // tm
    nk = tm // SUBLANE
    cw = min(MXU_TILE, d)
    nb = d // cw
    halo = 4 * SUBLANE
    ka, kb = CONV_A_K - 1, CONV_B_K - 1

    def body(dya_ref, dyb_ref, _, ba_ref, ca_ref, xa_ref, xb_ref, gb_ref, cah_ref, xah_ref, xbh_ref,
             u_ref, h_ref, hh_ref, caw_ref, cbw_ref, vec_ref, bda_ref, bdx_ref,
             dp_ref, dv_ref, dwa_ref, dwx_ref,
             zbuf, xbuf, hbuf, dczbuf, dubuf, a2buf, a1buf, lbuf, dcz_head, du_head, a_head, lam_head):
        i = pl.program_id(0)

        @pl.when(i == 0)
        def _():
            for ref in (dv_ref, dwa_ref, dwx_ref, dcz_head, du_head, a_head, lam_head):
                _zero(ref)

        has_prev = jnp.where(i < n_t - 1, 1.0, 0.0).astype(F32)
        row = lax.broadcasted_iota(jnp.int32, (SUBLANE, cw), 0)

        def colsum(v):
            return jnp.sum(v, axis=0, keepdims=True)

        for j in range(nb):
            cs = slice(j * cw, (j + 1) * cw)
            ca = ca_ref[:, cs].astype(F32)
            xa = xa_ref[:, cs].astype(F32)
            z = ca * xa
            z_before = cah_ref[:, cs].astype(F32) * xah_ref[:, cs].astype(F32) * has_prev
            _causal_fill(zbuf, z, z_before[halo - ka * SUBLANE:, :], ka, row)
            z2 = zbuf[0:tm, :]
            z1 = zbuf[SUBLANE:SUBLANE + tm, :]
            w0, w1, w2 = caw_ref[0:1, cs], caw_ref[1:2, cs], caw_ref[2:3, cs]
            cz = w0 * z2 + w1 * z1 + w2 * z
            dya = dya_ref[:, cs].astype(F32)
            dp_ref[:, 0 * d + j * cw:0 * d + (j + 1) * cw] = (dya * cz).astype(BF16)
            dcz = dya * ba_ref[:, cs].astype(F32)
            _anticausal_fill(dczbuf, dcz, dcz_head[:, cs], ka, row)
            dcz_head[:, cs] = dcz[0:ka * SUBLANE, :]
            dz = w2 * dcz + w1 * dczbuf[SUBLANE:SUBLANE + tm, :] + w0 * dczbuf[2 * SUBLANE:2 * SUBLANE + tm, :]
            dv_ref[DV_CONV_A + 0:DV_CONV_A + 1, cs] += colsum(dcz * z2)
            dv_ref[DV_CONV_A + 1:DV_CONV_A + 2, cs] += colsum(dcz * z1)
            dv_ref[DV_CONV_A + 2:DV_CONV_A + 3, cs] += colsum(dcz * z)
            dp_ref[:, 1 * d + j * cw:1 * d + (j + 1) * cw] = (dz * xa).astype(BF16)
            dp_ref[:, 2 * d + j * cw:2 * d + (j + 1) * cw] = (dz * ca).astype(BF16)
            h = h_ref[:, cs].astype(F32)
            h_before = hh_ref[:, cs].astype(F32) * has_prev
            _causal_fill(hbuf, h, h_before[halo - SUBLANE:, :], 1, row)
            h_prev = hbuf[0:tm, :]
            dyb = dyb_ref[:, cs].astype(F32)
            gel, dgel = _gelu_and_grad(gb_ref[:, cs].astype(F32))
            dp_ref[:, 4 * d + j * cw:4 * d + (j + 1) * cw] = (dyb * h * dgel).astype(BF16)
            ub = u_ref[:, cs]
            u = ub.astype(F32)
            sp = vec_ref[3:4, cs]
            r, gi, a, s, inv_s = _lru_gates(ub, bda_ref[j], bdx_ref[j], vec_ref[1:2, cs], vec_ref[2:3, cs], sp)
            _anticausal_fill(a2buf, a, a_head[:, cs], 1, row)
            a_head[:, cs] = a[0:SUBLANE, :]
            a1buf[...] = a2buf[SUBLANE:SUBLANE + tm, :]
            lbuf[...] = dyb * gel
            l_end, a_prod = _chain_scan(a1buf, lbuf, nk, reverse=True)
            a_inc, l_inc = _sublane_scan(a_prod, l_end, row, reverse=True)
            carry = lam_head[:, cs]
            ends = l_inc + a_inc * carry
            starts = jnp.where(row == SUBLANE - 1, carry, pltpu.roll(ends, SUBLANE - 1, 0))
            lam_head[:, cs] = jnp.broadcast_to(ends[0:1, :], (SUBLANE, cw))
            lam = (lbuf[...].reshape(nk, SUBLANE, cw) + a1buf[...].reshape(nk, SUBLANE, cw) * starts[None]).reshape(tm, cw)
            da = lam * h_prev
            iu = gi * u
            ds = lam * iu
            di = lam * s * u
            du = lam * s * gi
            dlog_a = da * a - ds * (a * a) * inv_s
            dv_ref[DV_SP:DV_SP + 1, cs] += colsum(dlog_a * r) * (-LRU_C)
            dpr = dlog_a * ((-LRU_C) * sp) * r * (1.0 - r)
            dpi = di * gi * (1.0 - gi)
            dv_ref[DV_BA:DV_BA + 1, cs] += colsum(dpr)
            dv_ref[DV_BX:DV_BX + 1, cs] += colsum(dpi)
            dprb = dpr.astype(BF16)
            dpib = dpi.astype(BF16)
            du = du + _dot_nt(dprb, bda_ref[j]) + _dot_nt(dpib, bdx_ref[j])
            dwa_ref[j] += _dot_tn(ub, dprb)
            dwx_ref[j] += _dot_tn(ub, dpib)
            xb = xb_ref[:, cs].astype(F32)
            x_before = xbh_ref[:, cs].astype(F32) * has_prev
            _causal_fill(xbuf, xb, x_before[halo - kb * SUBLANE:, :], kb, row)
            _anticausal_fill(dubuf, du, du_head[:, cs], kb, row)
            du_head[:, cs] = du[0:kb * SUBLANE, :]
            v0, v1, v2, v3 = cbw_ref[0:1, cs], cbw_ref[1:2, cs], cbw_ref[2:3, cs], cbw_ref[3:4, cs]
            dxb = (v3 * du + v2 * dubuf[SUBLANE:SUBLANE + tm, :] + v1 * dubuf[2 * SUBLANE:2 * SUBLANE + tm, :]
                   + v0 * dubuf[3 * SUBLANE:3 * SUBLANE + tm, :])
            dp_ref[:, 3 * d + j * cw:3 * d + (j + 1) * cw] = dxb.astype(BF16)
            dv_ref[DV_CONV_B_B:DV_CONV_B_B + 1, cs] += colsum(du)
            dv_ref[DV_CONV_B + 0:DV_CONV_B + 1, cs] += colsum(du * xbuf[0:tm, :])
            dv_ref[DV_CONV_B + 1:DV_CONV_B + 2, cs] += colsum(du * xbuf[SUBLANE:SUBLANE + tm, :])
            dv_ref[DV_CONV_B + 2:DV_CONV_B + 3, cs] += colsum(du * xbuf[2 * SUBLANE:2 * SUBLANE + tm, :])
            dv_ref[DV_CONV_B + 3:DV_CONV_B + 4, cs] += colsum(du * xb)

    rt = lambda i: n_t - 1 - i
    row_spec = pl.BlockSpec((tm, d), lambda i: (rt(i), 0))
    slab = lambda s: pl.BlockSpec((tm, d), lambda i, s=s: (rt(i), s))
    before = lambda s: pl.BlockSpec((halo, d), lambda i, s=s: (jnp.maximum(rt(i) * (tm // halo) - 1, 0), s))
    small = pl.BlockSpec((SUBLANE, d), lambda i: (0, 0))
    bd = pl.BlockSpec((nb, cw, cw), lambda i: (0, 0, 0))
    sd = jax.ShapeDtypeStruct
    wbuf = lambda n: pltpu.VMEM((tm + n * SUBLANE, cw), F32)
    head = lambda n: pltpu.VMEM((n * SUBLANE, d), F32)
    return _hosted_call(
        body, comms, name=name, grid=(n_t,),
        out_shape=[sd((t, N_PROJ * d), BF16), sd((DV_ROWS, d), F32), sd((nb, cw, cw), F32), sd((nb, cw, cw), F32)],
        in_specs=[row_spec, row_spec, ANY,
                  slab(0), slab(1), slab(2), slab(3), slab(4), before(1), before(2), before(3),
                  row_spec, row_spec, before(0), small, small, small, bd, bd],
        out_specs=[pl.BlockSpec((tm, 5 * d), lambda i: (rt(i), 0)), pl.BlockSpec((DV_ROWS, d), lambda i: (0, 0)), bd, bd],
        aliases={2: 0},
        scratch_shapes=[wbuf(ka), wbuf(kb), wbuf(1), wbuf(ka), wbuf(kb), wbuf(1),
                        pltpu.VMEM((tm, cw), F32), pltpu.VMEM((tm, cw), F32), head(ka), head(kb), head(1), head(1)],
        args=(dya, dyb, dp_gates, p, p, p, p, p, p, p, p, u_s, h_s, h_s, caw, cbw, vec, bda, bdx))


def _in_proj_bwd(dp, x, dx1, g_row, gath, grp, name, comms=()):
    t, d = x.shape
    tm = _row_tile(t, 512)
    n_in = N_PROJ * d

    def body(dp_ref, x_ref, dx1_ref, g_ref, gath_ref, dx_ref, dg_ref, w_in, sems):
        @pl.when(pl.program_id(0) == 0)
        def _():
            _load_weights(gath_ref, grp, ["in"], [w_in], sems)
            _zero(dg_ref)

        dh = _dot_nn(dp_ref[:, 0:d], w_in[0:d, :])
        for k in range(1, N_PROJ):
            dh = dh + _dot_nn(dp_ref[:, k * d:(k + 1) * d], w_in[k * d:(k + 1) * d, :])
        xf = x_ref[...]
        rstd = lax.rsqrt(jnp.mean(xf * xf, axis=-1, keepdims=True) + RMS_EPS)
        xh = xf * rstd
        g = g_ref[...]
        dg_ref[0:1, :] += jnp.sum(dh * xh, axis=0, keepdims=True)
        dxh = dh * g
        dx_ref[...] = dx1_ref[...] + rstd * (dxh - xh * jnp.mean(dxh * xh, axis=-1, keepdims=True))

    row = pl.BlockSpec((tm, d), lambda i: (i, 0))
    sd = jax.ShapeDtypeStruct
    return _hosted_call(
        body, comms, name=name, grid=(t // tm,),
        out_shape=[sd((t, d), F32), sd((SUBLANE, d), F32)],
        in_specs=[pl.BlockSpec((tm, n_in), lambda i: (i, 0)), row, row, pl.BlockSpec((1, d), lambda i: (0, 0)), ANY],
        out_specs=[row, pl.BlockSpec((SUBLANE, d), lambda i: (0, 0))],
        scratch_shapes=[pltpu.VMEM((n_in, d), BF16), pltpu.SemaphoreType.DMA((N_DEV,))],
        args=(dp, x, dx1, g_row, gath))


def _weight_grad(a, b, name):
    t, m = a.shape
    n = b.shape[1]
    bm = m
    for div in (1, 2, 4, 8):
        if m % div == 0 and (m // div) % LANE == 0 and (m // div) * n * 4 <= (12 << 20):
            bm = m // div
            break
    fixed = bm * n * (4 + 2 * 2)
    bt = next(_row_tile(t, want) for want in (2048, 1024, 512)
              if 2 * 2 * _row_tile(t, want) * (bm + n) + fixed <= VMEM_LIMIT - (12 << 20))
    n_t = t // bt

    def body(a_ref, b_ref, o_ref, acc):
        k = pl.program_id(1)

        @pl.when(k == 0)
        def _():
            _zero(acc)

        acc[...] += _dot_tn(a_ref[...], b_ref[...])

        @pl.when(k == n_t - 1)
        def _():
            o_ref[...] = acc[...].astype(BF16)

    return pl.pallas_call(
        body, name=name, grid=(m // bm, n_t),
        out_shape=jax.ShapeDtypeStruct((m, n), BF16),
        in_specs=[pl.BlockSpec((bt, bm), lambda i, k: (k, i)), pl.BlockSpec((bt, n), lambda i, k: (k, 0))],
        out_specs=pl.BlockSpec((bm, n), lambda i, k: (i, 0)),
        scratch_shapes=[pltpu.VMEM((bm, n), F32)],
        compiler_params=_params(2),
    )(a, b)


def _adamw(w, g, m, v, name):
    r, c = w.shape
    tr = _fit_rows(r, c * 4)
    c1 = 1.0 - ADAM_B1 ** ADAM_STEP
    c2 = 1.0 - ADAM_B2 ** ADAM_STEP

    def body(w_ref, g_ref, m_ref, v_ref, d_ref, nm_ref, nv_ref):
        g32 = g_ref[...]
        nm = ADAM_B1 * m_ref[...] + (1.0 - ADAM_B1) * g32
        nv = ADAM_B2 * v_ref[...] + (1.0 - ADAM_B2) * (g32 * g32)
        nm_ref[...] = nm
        nv_ref[...] = nv
        d_ref[...] = -ADAM_LR * ((nm / c1) / (jnp.sqrt(nv / c2) + ADAM_EPS) + ADAM_WD * w_ref[...])

    spec = pl.BlockSpec((tr, c), lambda i: (i, 0))
    return pl.pallas_call(
        body, name=name, grid=(r // tr,),
        out_shape=[jax.ShapeDtypeStruct((r, c), F32)] * 3,
        in_specs=[spec] * 4, out_specs=[spec] * 3,
        compiler_params=_params(),
    )(w, g, m, v)


def _pad_rows(a, mult=SUBLANE):
    pad = (-a.shape[0]) % mult
    return a if pad == 0 else jnp.concatenate([a, jnp.zeros((pad,) + a.shape[1:], a.dtype)], axis=0)


REPLICATED = ("ln1_g", "conv_b_b", "lru_wa", "lru_ba", "lru_wx", "lru_bx", "lru_lambda", "ln2_g", "final_g")
SMALL_SHARDED = ("conv_a_w", "conv_b_w", "gate_bias")
MATRICES = ("w_in", "w_out_a", "w_out_b", "w_o", "w_ffn_gate", "w_ffn_up", "w_ffn_down")
ORDER = ("ln1_g", "w_in", "conv_a_w", "conv_b_w", "conv_b_b", "lru_wa", "lru_ba", "lru_wx", "lru_bx", "lru_lambda",
         "w_out_a", "w_out_b", "gate_bias", "w_o", "ln2_g", "w_ffn_gate", "w_ffn_up", "w_ffn_down", "final_g")


def kernel(x, ln1_g, w_in, conv_a_w, conv_b_w, conv_b_b, lru_wa, lru_ba, lru_wx, lru_bx, lru_lambda, w_out_a, w_out_b, gate_bias, w_o, ln2_g, w_ffn_gate, w_ffn_up, w_ffn_down, final_g, loss_target, m_ln1_g, m_w_in, m_conv_a_w, m_conv_b_w, m_conv_b_b, m_lru_wa, m_lru_ba, m_lru_wx, m_lru_bx, m_lru_lambda, m_w_out_a, m_w_out_b, m_gate_bias, m_w_o, m_ln2_g, m_w_ffn_gate, m_w_ffn_up, m_w_ffn_down, m_final_g, v_ln1_g, v_w_in, v_conv_a_w, v_conv_b_w, v_conv_b_b, v_lru_wa, v_lru_ba, v_lru_wx, v_lru_bx, v_lru_lambda, v_w_out_a, v_w_out_b, v_gate_bias, v_w_o, v_ln2_g, v_w_ffn_gate, v_w_ffn_up, v_w_ffn_down, v_final_g):
    w = dict(ln1_g=ln1_g, w_in=w_in, conv_a_w=conv_a_w, conv_b_w=conv_b_w, conv_b_b=conv_b_b, lru_wa=lru_wa,
             lru_ba=lru_ba, lru_wx=lru_wx, lru_bx=lru_bx, lru_lambda=lru_lambda, w_out_a=w_out_a, w_out_b=w_out_b,
             gate_bias=gate_bias, w_o=w_o, ln2_g=ln2_g, w_ffn_gate=w_ffn_gate, w_ffn_up=w_ffn_up,
             w_ffn_down=w_ffn_down, final_g=final_g)
    mom = dict(ln1_g=m_ln1_g, w_in=m_w_in, conv_a_w=m_conv_a_w, conv_b_w=m_conv_b_w, conv_b_b=m_conv_b_b,
               lru_wa=m_lru_wa, lru_ba=m_lru_ba, lru_wx=m_lru_wx, lru_bx=m_lru_bx, lru_lambda=m_lru_lambda,
               w_out_a=m_w_out_a, w_out_b=m_w_out_b, gate_bias=m_gate_bias, w_o=m_w_o, ln2_g=m_ln2_g,
               w_ffn_gate=m_w_ffn_gate, w_ffn_up=m_w_ffn_up, w_ffn_down=m_w_ffn_down, final_g=m_final_g)
    var = dict(ln1_g=v_ln1_g, w_in=v_w_in, conv_a_w=v_conv_a_w, conv_b_w=v_conv_b_w, conv_b_b=v_conv_b_b,
               lru_wa=v_lru_wa, lru_ba=v_lru_ba, lru_wx=v_lru_wx, lru_bx=v_lru_bx, lru_lambda=v_lru_lambda,
               w_out_a=v_w_out_a, w_out_b=v_w_out_b, gate_bias=v_gate_bias, w_o=v_w_o, ln2_g=v_ln2_g,
               w_ffn_gate=v_w_ffn_gate, w_ffn_up=v_w_ffn_up, w_ffn_down=v_w_ffn_down, final_g=v_final_g)

    _, t, d = x.shape
    n_layers = w_in.shape[0]
    ff = w_ffn_down.shape[1] * N_DEV
    dd = d // N_DEV
    hd = d // LRU_HEADS
    cw = min(MXU_TILE, d)
    nb = d // cw
    hpt = cw // hd
    grp = _groups(d, ff)
    me = 4 * lax.axis_index("x") + 2 * lax.axis_index("y") + lax.axis_index("c")
    tm_time = _time_tile(t)
    x0 = _to_tile_order(x[0], tm_time)
    target = _to_tile_order(loss_target[0], tm_time)

    packed = [{"in": jnp.swapaxes(w_in[l], 0, 1).astype(BF16),
               "rest": jnp.concatenate([w_out_a[l], w_out_b[l], w_o[l], jnp.swapaxes(w_ffn_gate[l], 0, 1),
                                        jnp.swapaxes(w_ffn_up[l], 0, 1), w_ffn_down[l]], axis=0).astype(BF16)}
              for l in range(n_layers)]
    n_small = CONV_A_K + CONV_B_K + 2
    small = _pad_rows(jnp.concatenate([conv_a_w, conv_b_w, gate_bias], axis=1).reshape(n_layers * n_small, dd))
    sp = jax.nn.softplus(-lru_lambda)
    vec = [_pad_rows(jnp.stack([conv_b_b[l], lru_ba[l], lru_bx[l], sp[l]])) for l in range(n_layers)]
    eye = jnp.eye(hpt, dtype=F32)

    def block_diag(wh):
        return jnp.einsum("jkab,kl->jkalb", wh.reshape(nb, hpt, hd, hd), eye).reshape(nb, cw, cw).astype(BF16)

    bda = [block_diag(lru_wa[l]) for l in range(n_layers)]
    bdx = [block_diag(lru_wx[l]) for l in range(n_layers)]

    gath = [dict() for _ in range(n_layers)]
    (gath[0]["in"],), (small_g,) = _comm_call([_Gather(packed[0]["in"]), _Gather(small)], "gather_in_0")
    small_full = jnp.swapaxes(small_g[:, :n_layers * n_small], 0, 1).reshape(n_layers, n_small, d)
    caw = [_pad_rows(small_full[k, 0:CONV_A_K]) for k in range(n_layers)]
    cbw = [_pad_rows(small_full[k, CONV_A_K:CONV_A_K + CONV_B_K]) for k in range(n_layers)]
    gbias = [_pad_rows(small_full[k, CONV_A_K + CONV_B_K:]) for k in range(n_layers)]
    saved = []
    xl = x0
    for l in range(n_layers):
        more = l + 1 < n_layers
        comms = ([_Gather(packed[0]["rest"])] if l == 0 else []) + ([_Gather(packed[l + 1]["in"])] if more else [])
        (p, h1b, ya, yb, *kept), got = _in_proj_mixer_fwd(
            xl, ln1_g[l][None], gath[l]["in"], grp["in"], caw[l], cbw[l], vec[l], bda[l], bdx[l], f"in_proj_mixer_fwd_{l}", comms)
        if l == 0:
            gath[0]["rest"] = got[0][0]
        if more:
            gath[l + 1]["in"] = got[-1][0]
        (x1, oa, ob), _ = _merge_fwd(xl, ya, yb, p, gbias[l], gath[l]["rest"], grp["rest"], f"merge_fwd_{l}")
        (x2, gg, uu), got = _ffn_fwd(x1, ln2_g[l][None], gath[l]["rest"], grp["rest"], f"ffn_fwd_{l}",
                                     [_Gather(packed[l + 1]["rest"])] if more else [])
        if more:
            ((gath[l + 1]["rest"],),) = got
        saved.append(dict(x=xl, p=p, h1b=h1b, ya=ya, yb=yb, mixer=kept, x1=x1, oa=oa, ob=ob, gg=gg, uu=uu))
        xl = x2
    loss_tile, dx, dfinal = _loss_head(xl, final_g[None], target, "loss_head")
    loss = lax.psum(loss_tile[0, 0], ("x", "y", "c"))

    def heads(dwb):
        blocks = jnp.diagonal(dwb.reshape(nb, hpt, hd, hpt, hd), axis1=1, axis2=3)
        return jnp.moveaxis(blocks, 3, 1).reshape(hd, d)

    layer_names = [n for n in REPLICATED if n != "final_g"] + list(SMALL_SHARDED)

    def layer_block(k):
        return jnp.concatenate([small_grads[k][n] for n in layer_names], axis=0)

    recv = [dict() for _ in range(n_layers)]
    small_grads = [None] * n_layers
    early_all = None
    xg = {"d": _Group(("d",), (ff // N_DEV,)), "gu": _Group(("g", "u"), (ff // N_DEV,) * 2),
          "out": _Group(("oa", "ob", "o"), (dd,) * 3), "in": grp["in"]}
    far_in = None
    for l in reversed(range(n_layers)):
        s = saved[l]
        (dgg, duu, dw_d), got = _ffn_bwd_act(dx, s["gg"], s["uu"], gath[l]["rest"], grp["rest"], f"ffn_bwd_act_{l}",
                                             [far_in] if far_in else [])
        if far_in:
            recv[l + 1]["in"].append(got[0][0])
        (dx1, dx1b, h2b, dln2), got = _ffn_bwd_in(dgg, duu, dx, s["x1"], ln2_g[l][None], gath[l]["rest"], grp["rest"],
                                                  f"ffn_bwd_in_{l}", [_Exchange({"d": dw_d}, xg["d"])])
        recv[l]["d"] = [got[0][0]]
        dw_gu = {"g": _weight_grad(dgg, h2b, f"dw_ffn_gate_{l}"), "u": _weight_grad(duu, h2b, f"dw_ffn_up_{l}")}
        gu_here = [_Exchange(dw_gu, xg["gu"])] if l == 0 else []
        (dya, dyb, dp_gates, dgbias, dw_oa, dw_ob, dw_o), got = _merge_bwd(
            dx1b, s["oa"], s["ob"], s["ya"], s["yb"], s["p"], gbias[l], gath[l]["rest"], grp["rest"], f"merge_bwd_{l}", gu_here)
        if l == 0:
            recv[l]["gu"] = [got[0][0]]
        comms = [_Exchange({"oa": dw_oa, "ob": dw_ob, "o": dw_o}, xg["out"])]
        if l == 0:
            early = [layer_block(k) for k in range(1, n_layers)] + [_pad_rows(dfinal[0:1])]
            comms.append(_Gather(jnp.concatenate(early, axis=0)))
        else:
            comms.append(_Exchange(dw_gu, xg["gu"]))
        (dp, dv, dwa, dwx), got = _mixer_bwd(dya, dyb, dp_gates, s["p"], *s["mixer"], caw[l], cbw[l], vec[l], bda[l], bdx[l],
                                             f"mixer_bwd_{l}", comms)
        recv[l]["out"] = [got[0][0]]
        if l == 0:
            early_all = got[1][0]
        else:
            recv[l]["gu"] = [got[1][0]]
        small_grads[l] = {
            "conv_b_b": dv[DV_CONV_B_B:DV_CONV_B_B + 1], "lru_wa": heads(dwa),
            "lru_ba": dv[DV_BA:DV_BA + 1], "lru_wx": heads(dwx), "lru_bx": dv[DV_BX:DV_BX + 1],
            "lru_lambda": dv[DV_SP:DV_SP + 1] * (-jax.nn.sigmoid(-lru_lambda[l]))[None], "ln2_g": dln2[0:1],
            "conv_a_w": dv[DV_CONV_A:DV_CONV_A + CONV_A_K], "conv_b_w": dv[DV_CONV_B:DV_CONV_B + CONV_B_K],
            "gate_bias": dgbias[0:2],
        }
        dw_in = {"in": _weight_grad(dp, s["h1b"], f"dw_in_{l}")}
        if l > 0:
            near_in, far_in = _Exchange(dw_in, xg["in"], NEAR_PEERS), _Exchange(dw_in, xg["in"], FAR_PEERS, local=False)
        else:
            near_in, far_in = _Exchange(dw_in, xg["in"]), None
        (dx, dln1), got = _in_proj_bwd(dp, s["x"], dx1, ln1_g[l][None], gath[l]["in"], grp["in"], f"in_proj_bwd_{l}", [near_in])
        recv[l]["in"] = [got[0][0]]
        small_grads[l]["ln1_g"] = dln1[0:1]
    grad_x = _from_tile_order(dx, tm_time)[None]

    g = {}
    gsum = [{k: _sum_slots(recv[l][k], f"sum_{k}_{l}") for k in xg} for l in range(n_layers)]

    def part(key):
        k = next(name for name, group in xg.items() if key in group.keys)
        o, r = xg[k].off[key], xg[k].rows[key]
        return jnp.stack([gsum[l][k][o:o + r] for l in range(n_layers)])

    g = {"w_in": jnp.swapaxes(part("in"), 1, 2), "w_out_a": part("oa"), "w_out_b": part("ob"), "w_o": part("o"),
         "w_ffn_gate": jnp.swapaxes(part("g"), 1, 2), "w_ffn_up": jnp.swapaxes(part("u"), 1, 2), "w_ffn_down": part("d")}
    ((late_all,),) = _comm_call([_Gather(layer_block(0).astype(BF16))], "gather_small_grads_0")
    early_sum = _sum_slots([early_all], "sum_small_grads")
    block_rows = late_all.shape[1]
    per_layer = [_sum_slots([late_all], "sum_small_grads_0")]
    per_layer += [early_sum[(k - 1) * block_rows:k * block_rows] for k in range(1, n_layers)]
    g["final_g"] = early_sum[(n_layers - 1) * block_rows].reshape(w["final_g"].shape)
    o = 0
    for n in layer_names:
        rows = small_grads[0][n].shape[0]
        stacked = jnp.concatenate([per_layer[k][o:o + rows] for k in range(n_layers)], axis=0)
        if n in SMALL_SHARDED:
            g[n] = lax.dynamic_slice_in_dim(stacked, me * dd, dd, axis=1).reshape(n_layers, rows, dd)
        else:
            g[n] = stacked.reshape(w[n].shape)
        o += rows

    delta, new_m, new_v = {}, {}, {}
    gate_maps = ("lru_wa", "lru_wx")
    for n in MATRICES + gate_maps:
        shape = w[n].shape
        flat = lambda a: a.reshape(-1, d if n in gate_maps else shape[-1])
        dl, nm, nv = _adamw(flat(w[n]), flat(g[n]), flat(mom[n]), flat(var[n]), f"adamw_{n}")
        delta[n], new_m[n], new_v[n] = dl.reshape(shape), nm.reshape(shape), nv.reshape(shape)
    vectors = tuple(n for n in REPLICATED if n not in gate_maps)
    for group, width, name in ((vectors, d, "adamw_replicated"), (SMALL_SHARDED, dd, "adamw_vectors")):
        cat = lambda src: _pad_rows(jnp.concatenate([src[n].reshape(-1, width) for n in group], axis=0))
        dl, nm, nv = _adamw(cat(w), cat(g), cat(mom), cat(var), name)
        o = 0
        for n in group:
            rows = w[n].size // width
            delta[n], new_m[n], new_v[n] = (a[o:o + rows].reshape(w[n].shape) for a in (dl, nm, nv))
            o += rows

    return (loss, grad_x, *[g[n] for n in ORDER], *[delta[n] for n in ORDER], *[new_m[n] for n in ORDER],
            *[new_v[n] for n in ORDER])
```

```python
import math

import jax
import jax.numpy as jnp
from jax import lax
from jax.experimental import pallas as pl
from jax.experimental.pallas import tpu as pltpu

F32 = jnp.float32
BF16 = jnp.bfloat16

N_DEV = 8
N_PROJ = 7
LRU_HEADS = 16
LRU_C = 8.0
RMS_EPS = 1e-6
CONV_A_K = 3
CONV_B_K = 4
GELU_C = math.sqrt(2.0 / math.pi)
GELU_A = 0.044715

ADAM_LR = 0.001
ADAM_B1 = 0.9
ADAM_B2 = 0.999
ADAM_EPS = 1e-08
ADAM_WD = 0.01
ADAM_STEP = 10

LANE = 128
SUBLANE = 8
MXU_TILE = 256
VMEM_LIMIT = 52 << 20
ALL_PEERS = tuple(range(1, N_DEV))
NEAR_PEERS = (1, 2, 3, 4, 5)
FAR_PEERS = (6, 7)
MESH = pl.DeviceIdType.MESH
ANY = pl.BlockSpec(memory_space=pl.ANY)


def _dot_nn(a, b):
    return lax.dot_general(a, b, (((1,), (0,)), ((), ())), preferred_element_type=F32)


def _dot_nt(a, b):
    return lax.dot_general(a, b, (((1,), (1,)), ((), ())), preferred_element_type=F32)


def _dot_tn(a, b):
    return lax.dot_general(a, b, (((0,), (0,)), ((), ())), preferred_element_type=F32)


def _sigmoid(x):
    return 1.0 / (1.0 + jnp.exp(-x))


def _gelu_and_grad(x):
    x2 = x * x
    t = jnp.tanh(GELU_C * x * (1.0 + GELU_A * x2))
    g = 0.5 * x * (1.0 + t)
    dg = 0.5 * (1.0 + t) + 0.5 * x * (1.0 - t * t) * GELU_C * (1.0 + 3.0 * GELU_A * x2)
    return g, dg


def _zero(ref):
    ref[...] = jnp.zeros(ref.shape, ref.dtype)


def _fit_rows(r, row_bytes, budget=1 << 20):
    fits = [t for t in range(16, r + 1, 16) if r % t == 0 and t * row_bytes <= budget]
    return max(fits) if fits else r


def _row_tile(t, want):
    tm = min(want, t // 2)
    assert t % tm == 0 and tm % SUBLANE == 0, (t, tm)
    return tm


def _params(n_grid=1, **kw):
    return pltpu.CompilerParams(dimension_semantics=("arbitrary",) * n_grid, vmem_limit_bytes=VMEM_LIMIT, **kw)


class _Group:
    def __init__(self, keys, rows):
        self.keys = keys
        self.rows = dict(zip(keys, rows))
        self.off, o = {}, 0
        for k in keys:
            self.off[k] = o
            o += self.rows[k]
        self.total = o


def _groups(d, ff):
    dd, ffs = d // N_DEV, ff // N_DEV
    return {"in": _Group(("in",), (N_PROJ * dd,)),
            "rest": _Group(("oa", "ob", "o", "g", "u", "d"), (dd, dd, dd, ffs, ffs, ffs))}


def _load_weights(g_ref, grp, keys, dsts, sems):
    copies = []
    for n, (k, dst) in enumerate(zip(keys, dsts)):
        rows, off = grp.rows[k], grp.off[k]
        copies += [pltpu.make_async_copy(g_ref.at[p, pl.ds(off, rows), :], dst.at[pl.ds(p * rows, rows), :],
                                         sems.at[n * N_DEV + p]) for p in range(N_DEV)]
    for c in copies:
        c.start()
    for c in copies:
        c.wait()


def _comm_sems():
    return [pltpu.SemaphoreType.DMA((N_DEV - 1,)), pltpu.SemaphoreType.DMA((N_DEV - 1,)), pltpu.SemaphoreType.DMA]


class _Gather:
    def __init__(self, x):
        self.inputs = [x]
        self.out_shape = [jax.ShapeDtypeStruct((N_DEV,) + x.shape, x.dtype)]
        self.scratch = _comm_sems()

    def _plan(self, ins, outs, scr):
        (x_ref,), (out_ref,), (send_sems, recv_sems, local_sem) = ins, outs, scr
        mx, my, mc = lax.axis_index("x"), lax.axis_index("y"), lax.axis_index("c")
        me, sibling = (mx, my, mc), (mx, my, 1 - mc)
        xn, yn, dg = (1 - mx, my), (mx, 1 - my), (1 - mx, 1 - my)
        core0 = mc == 0
        relayed = (jnp.where(core0, 1 - mx, mx), jnp.where(core0, my, 1 - my))
        relay_to = (jnp.where(core0, mx, 1 - mx), jnp.where(core0, 1 - my, my))

        def slot(px, py, pc):
            return out_ref.at[4 * px + 2 * py + pc]

        def copy(k, block, to, src=None):
            return pltpu.make_async_remote_copy(
                src_ref=slot(*block) if src is None else src, dst_ref=slot(*block),
                send_sem=send_sems.at[k], recv_sem=recv_sems.at[k], device_id=to, device_id_type=MESH)

        mine = lambda: pltpu.make_async_copy(x_ref, slot(*me), local_sem)
        own = [lambda: copy(0, me, sibling, src=x_ref), lambda: copy(1, me, (*xn, mc), src=x_ref),
               lambda: copy(2, me, (*yn, mc), src=x_ref)]
        relay = lambda: copy(3, (*relayed, mc), (*relay_to, mc))
        passes = [lambda: copy(4, (*xn, mc), sibling), lambda: copy(5, (*yn, mc), sibling), lambda: copy(6, (*dg, mc), sibling)]
        arrival = lambda k: copy(k, me, me)
        return mine, own, relay, passes, arrival

    def start(self, ins, outs, scr):
        mine, own, _, _, _ = self._plan(ins, outs, scr)
        mine().start()
        for cp in own:
            cp().start()

    def relay(self, ins, outs, scr):
        _, _, relay, passes, arrival = self._plan(ins, outs, scr)
        arrival(1).wait_recv()
        arrival(2).wait_recv()
        relay().start()
        passes[0]().start()
        passes[1]().start()

    def mid(self, ins, outs, scr):
        _, _, _, passes, arrival = self._plan(ins, outs, scr)
        arrival(3).wait_recv()
        passes[2]().start()

    def finish(self, ins, outs, scr):
        mine, _, _, _, arrival = self._plan(ins, outs, scr)
        for k in (0, 4, 5, 6):
            arrival(k).wait_recv()
        for k in range(N_DEV - 1):
            arrival(k).wait_send()
        mine().wait()


class _Exchange:
    def __init__(self, mats, grp, peers=ALL_PEERS, local=True):
        self.grp, self.peers, self.local = grp, tuple(peers), local
        self.inputs = [mats[k] for k in grp.keys]
        slots = len(self.peers) + (1 if local else 0)
        self.out_shape = [jax.ShapeDtypeStruct((slots, grp.total, self.inputs[0].shape[1]), BF16)]
        self.scratch = [pltpu.SemaphoreType.DMA((len(self.peers),)), pltpu.SemaphoreType.DMA((len(self.peers),)),
                        pltpu.SemaphoreType.DMA]

    def _pieces(self, g_refs, out_ref, q, dst_slot):
        out = []
        for g_ref, k in zip(g_refs, self.grp.keys):
            rows = self.grp.rows[k]
            out.append((g_ref.at[pl.ds(pl.multiple_of(q * rows, 16), rows), :],
                        out_ref.at[dst_slot, pl.ds(self.grp.off[k], rows), :]))
        return out

    def start(self, ins, outs, scr):
        (out_ref,), (send_sems, recv_sems, local_sem) = outs, scr
        mx, my, mc = lax.axis_index("x"), lax.axis_index("y"), lax.axis_index("c")
        if self.local:
            for s, t in self._pieces(ins, out_ref, 4 * mx + 2 * my + mc, 0):
                pltpu.make_async_copy(s, t, local_sem).start()
        for n, k in enumerate(self.peers):
            px, py, pc = mx ^ ((k >> 2) & 1), my ^ ((k >> 1) & 1), mc ^ (k & 1)
            for s, t in self._pieces(ins, out_ref, 4 * px + 2 * py + pc, n + (1 if self.local else 0)):
                pltpu.make_async_remote_copy(src_ref=s, dst_ref=t, send_sem=send_sems.at[n], recv_sem=recv_sems.at[n],
                                             device_id=(px, py, pc), device_id_type=MESH).start()

    def relay(self, ins, outs, scr):
        pass

    def mid(self, ins, outs, scr):
        pass

    def finish(self, ins, outs, scr):
        (out_ref,), (send_sems, recv_sems, local_sem) = outs, scr
        mx, my, mc = lax.axis_index("x"), lax.axis_index("y"), lax.axis_index("c")
        whole = out_ref.at[0]
        for n in range(len(self.peers)):
            done = pltpu.make_async_remote_copy(src_ref=whole, dst_ref=whole, send_sem=send_sems.at[n],
                                                recv_sem=recv_sems.at[n], device_id=(mx, my, mc), device_id_type=MESH)
            done.wait_send()
            done.wait_recv()
        if self.local:
            pltpu.make_async_copy(whole, whole, local_sem).wait()


def _split(refs, sizes):
    out, pos = [], 0
    for n in sizes:
        out.append(refs[pos:pos + n])
        pos += n
    return out


def _hosted_call(body, comms, *, name, grid, in_specs, out_specs, out_shape, scratch_shapes, args, aliases=None):
    n_steps = grid[0]
    nc = len(comms)
    sizes = ([len(in_specs)] + [len(c.inputs) for c in comms] + [len(out_specs)] + [len(c.out_shape) for c in comms]
             + [len(scratch_shapes)] + [len(c.scratch) for c in comms])

    def hosted(*refs):
        parts = _split(refs, sizes)
        ins, c_ins = parts[0], parts[1:1 + nc]
        outs, c_outs = parts[1 + nc], parts[2 + nc:2 + 2 * nc]
        scr, c_scr = parts[2 + 2 * nc], parts[3 + 2 * nc:]
        step = pl.program_id(0)
        if comms:
            @pl.when(step == 0)
            def _():
                for c, a, b, s in zip(comms, c_ins, c_outs, c_scr):
                    c.start(a, b, s)

            relay_step = (3 * n_steps) // 5

            @pl.when(step == relay_step)
            def _():
                for c, a, b, s in zip(comms, c_ins, c_outs, c_scr):
                    c.relay(a, b, s)

            @pl.when(step == max(n_steps - 2, relay_step))
            def _():
                for c, a, b, s in zip(comms, c_ins, c_outs, c_scr):
                    c.mid(a, b, s)

        body(*ins, *outs, *scr)
        if comms:
            @pl.when(step == n_steps - 1)
            def _():
                for c, a, b, s in zip(comms, c_ins, c_outs, c_scr):
                    c.finish(a, b, s)

    res = pl.pallas_call(
        hosted, name=name, grid=grid,
        out_shape=[*out_shape, *[o for c in comms for o in c.out_shape]],
        in_specs=[*in_specs, *[ANY for c in comms for _ in c.inputs]],
        out_specs=[*out_specs, *[ANY for c in comms for _ in c.out_shape]],
        scratch_shapes=[*scratch_shapes, *[s for c in comms for s in c.scratch]],
        input_output_aliases=aliases or {},
        compiler_params=_params(),
    )(*args, *[a for c in comms for a in c.inputs])
    main, rest = res[:len(out_specs)], res[len(out_specs):]
    return main, _split(rest, [len(c.out_shape) for c in comms])


def _comm_call(comms, name):
    sizes = [len(c.inputs) for c in comms] + [len(c.out_shape) for c in comms] + [len(c.scratch) for c in comms]
    nc = len(comms)

    def body(*refs):
        parts = _split(refs, sizes)
        triples = list(zip(comms, parts[:nc], parts[nc:2 * nc], parts[2 * nc:]))
        for phase in ("start", "relay", "mid", "finish"):
            for c, ins, outs, scr in triples:
                getattr(c, phase)(ins, outs, scr)

    res = pl.pallas_call(
        body, name=name, out_shape=[o for c in comms for o in c.out_shape],
        in_specs=[ANY for c in comms for _ in c.inputs], out_specs=[ANY for c in comms for _ in c.out_shape],
        scratch_shapes=[s for c in comms for s in c.scratch],
    )(*[a for c in comms for a in c.inputs])
    return _split(res, [len(c.out_shape) for c in comms])


def _sum_slots(xs, name):
    _, r, c = xs[0].shape
    tr = _fit_rows(r, c * 4)

    def body(*refs):
        acc = None
        for x_ref in refs[:-1]:
            for p in range(x_ref.shape[0]):
                v = x_ref[p].astype(F32)
                acc = v if acc is None else acc + v
        refs[-1][...] = acc

    return pl.pallas_call(
        body, name=name, grid=(r // tr,),
        out_shape=jax.ShapeDtypeStruct((r, c), F32),
        in_specs=[pl.BlockSpec((x.shape[0], tr, c), lambda i: (0, i, 0)) for x in xs],
        out_specs=pl.BlockSpec((tr, c), lambda i: (i, 0)),
        compiler_params=_params(),
    )(*xs)


def _time_tile(t):
    return _row_tile(t, 256)


def _to_tile_order(a, tm):
    t, c = a.shape
    return jnp.swapaxes(a.reshape(t // tm, SUBLANE, tm // SUBLANE, c), 1, 2).reshape(t, c)


def _from_tile_order(a, tm):
    t, c = a.shape
    return jnp.swapaxes(a.reshape(t // tm, tm // SUBLANE, SUBLANE, c), 1, 2).reshape(t, c)


def _causal_fill(buf, v, prev_tail, n, row):
    tm = v.shape[0]
    for q in range(n):
        cur = v[tm - SUBLANE * (n - q):tm - SUBLANE * (n - q - 1), :]
        prv = prev_tail[SUBLANE * q:SUBLANE * (q + 1), :]
        buf[SUBLANE * q:SUBLANE * (q + 1), :] = jnp.where(row == 0, pltpu.roll(prv, 1, 0), pltpu.roll(cur, 1, 0))
    buf[SUBLANE * n:, :] = v


def _anticausal_fill(buf, v, next_head, n, row):
    tm = v.shape[0]
    buf[0:tm, :] = v
    for q in range(n):
        cur = v[SUBLANE * q:SUBLANE * (q + 1), :]
        nxt = next_head[SUBLANE * q:SUBLANE * (q + 1), :]
        buf[tm + SUBLANE * q:tm + SUBLANE * (q + 1), :] = jnp.where(
            row == SUBLANE - 1, pltpu.roll(nxt, SUBLANE - 1, 0), pltpu.roll(cur, SUBLANE - 1, 0))


def _chain_scan(abuf, bbuf, nk, reverse):
    cw = abuf.shape[1]

    def step(n, carry):
        h, c = carry
        r0 = pl.multiple_of((nk - 1 - n if reverse else n) * SUBLANE, SUBLANE)
        ak = abuf[pl.ds(r0, SUBLANE), :]
        h = ak * h + bbuf[pl.ds(r0, SUBLANE), :]
        c = ak * c
        bbuf[pl.ds(r0, SUBLANE), :] = h
        abuf[pl.ds(r0, SUBLANE), :] = c
        return h, c

    return lax.fori_loop(0, nk, step, (jnp.zeros((SUBLANE, cw), F32), jnp.ones((SUBLANE, cw), F32)), unroll=True)


def _sublane_scan(a, b, row, reverse):
    for sh in (1, 2, 4):
        if reverse:
            m = row < SUBLANE - sh
            b = jnp.where(m, a * pltpu.roll(b, SUBLANE - sh, 0) + b, b)
            a = jnp.where(m, a * pltpu.roll(a, SUBLANE - sh, 0), a)
        else:
            m = row >= sh
            b = jnp.where(m, a * pltpu.roll(b, sh, 0) + b, b)
            a = jnp.where(m, a * pltpu.roll(a, sh, 0), a)
    return a, b


def _lru_gates(ub, bda, bdx, ba, bx, sp):
    r = _sigmoid(_dot_nn(ub, bda) + ba)
    i = _sigmoid(_dot_nn(ub, bdx) + bx)
    log_a = (-LRU_C) * r * sp
    a = jnp.exp(log_a)
    s2 = -jnp.tanh(log_a) * (1.0 + a * a)
    inv_s = lax.rsqrt(s2)
    s = jnp.where(s2 > 0.0, s2 * inv_s, 0.0)
    return r, i, a, s, inv_s


def _in_proj_mixer_fwd(x, g_row, gath, grp, caw, cbw, vec, bda, bdx, name, comms=()):
    t, d = x.shape
    n_in = N_PROJ * d
    tm = _time_tile(t)
    nk = tm // SUBLANE
    cw = min(MXU_TILE, d)
    nb = d // cw

    def body(x_ref, g_ref, gath_ref, caw_ref, cbw_ref, vec_ref, bda_ref, bdx_ref,
             p_ref, h1_ref, ya_ref, yb_ref, u_ref, h_ref, w_in, sems, zbuf, xbuf, abuf, bbuf, z_tail, x_tail, h_carry):
        @pl.when(pl.program_id(0) == 0)
        def _():
            _load_weights(gath_ref, grp, ["in"], [w_in], sems)
            _zero(z_tail)
            _zero(x_tail)
            _zero(h_carry)

        def project_and_mix():
            xf = x_ref[...]
            rstd = lax.rsqrt(jnp.mean(xf * xf, axis=-1, keepdims=True) + RMS_EPS)
            h1 = (xf * rstd * g_ref[...]).astype(BF16)
            h1_ref[...] = h1
            for k in range(N_PROJ):
                p_ref[:, k * d:(k + 1) * d] = _dot_nt(h1, w_in[k * d:(k + 1) * d, :]).astype(BF16)
            row = lax.broadcasted_iota(jnp.int32, (SUBLANE, cw), 0)
            for j in range(nb):
                cs = slice(j * cw, (j + 1) * cw)
                ba_ref, ca_ref, xa_ref, xb_ref, gb_ref = (p_ref.at[:, k * d:(k + 1) * d] for k in range(5))
                z = ca_ref[:, cs].astype(F32) * xa_ref[:, cs].astype(F32)
                _causal_fill(zbuf, z, z_tail[:, cs], CONV_A_K - 1, row)
                z_tail[:, cs] = z[tm - (CONV_A_K - 1) * SUBLANE:, :]
                cz = caw_ref[0:1, cs] * zbuf[0:tm, :] + caw_ref[1:2, cs] * zbuf[SUBLANE:SUBLANE + tm, :] + caw_ref[2:3, cs] * z
                ya_ref[:, cs] = (ba_ref[:, cs].astype(F32) * cz).astype(BF16)
                xb = xb_ref[:, cs].astype(F32)
                _causal_fill(xbuf, xb, x_tail[:, cs], CONV_B_K - 1, row)
                x_tail[:, cs] = xb[tm - (CONV_B_K - 1) * SUBLANE:, :]
                u = (cbw_ref[0:1, cs] * xbuf[0:tm, :] + cbw_ref[1:2, cs] * xbuf[SUBLANE:SUBLANE + tm, :]
                     + cbw_ref[2:3, cs] * xbuf[2 * SUBLANE:2 * SUBLANE + tm, :] + cbw_ref[3:4, cs] * xb + vec_ref[0:1, cs])
                ub = u.astype(BF16)
                u = ub.astype(F32)
                _, gi, a, s, _ = _lru_gates(ub, bda_ref[j], bdx_ref[j], vec_ref[1:2, cs], vec_ref[2:3, cs], vec_ref[3:4, cs])
                abuf[...] = a
                bbuf[...] = s * (gi * u)
                h_end, a_prod = _chain_scan(abuf, bbuf, nk, reverse=False)
                a_inc, h_inc = _sublane_scan(a_prod, h_end, row, reverse=False)
                carry = h_carry[:, cs]
                ends = h_inc + a_inc * carry
                starts = jnp.where(row == 0, carry, pltpu.roll(ends, 1, 0))
                h_carry[:, cs] = jnp.broadcast_to(ends[SUBLANE - 1:SUBLANE, :], (SUBLANE, cw))
                h = (bbuf[...].reshape(nk, SUBLANE, cw) + abuf[...].reshape(nk, SUBLANE, cw) * starts[None]).reshape(tm, cw)
                gel, _ = _gelu_and_grad(gb_ref[:, cs].astype(F32))
                yb_ref[:, cs] = (h * gel).astype(BF16)
                u_ref[:, cs] = ub
                h_ref[:, cs] = h.astype(BF16)

        project_and_mix()

    small = pl.BlockSpec((SUBLANE, d), lambda i: (0, 0))
    bd = pl.BlockSpec((nb, cw, cw), lambda i: (0, 0, 0))
    row_spec = pl.BlockSpec((tm, d), lambda i: (i, 0))
    return _hosted_call(
        body, comms, name=name, grid=(t // tm,),
        out_shape=[jax.ShapeDtypeStruct((t, n_in), BF16)] + [jax.ShapeDtypeStruct((t, d), BF16)] * 5,
        in_specs=[row_spec, pl.BlockSpec((1, d), lambda i: (0, 0)), ANY, small, small, small, bd, bd],
        out_specs=[pl.BlockSpec((tm, n_in), lambda i: (i, 0))] + [row_spec] * 5,
        scratch_shapes=[pltpu.VMEM((n_in, d), BF16), pltpu.SemaphoreType.DMA((N_DEV,)),
                        pltpu.VMEM((tm + (CONV_A_K - 1) * SUBLANE, cw), F32), pltpu.VMEM((tm + (CONV_B_K - 1) * SUBLANE, cw), F32),
                        pltpu.VMEM((tm, cw), F32), pltpu.VMEM((tm, cw), F32),
                        pltpu.VMEM(((CONV_A_K - 1) * SUBLANE, d), F32), pltpu.VMEM(((CONV_B_K - 1) * SUBLANE, d), F32),
                        pltpu.VMEM((SUBLANE, d), F32)],
        args=(x, g_row, gath, caw, cbw, vec, bda, bdx))


def _merge_fwd(x, ya, yb, p, gbias, gath, grp, name, comms=()):
    t, d = x.shape
    tm = _row_tile(t, 512)

    def body(x_ref, ya_ref, yb_ref, ga_ref, gb_ref, gbias_ref, gath_ref, x1_ref, oa_ref, ob_ref, w_oa, w_ob, w_o, sems):
        @pl.when(pl.program_id(0) == 0)
        def _():
            _load_weights(gath_ref, grp, ["oa", "ob", "o"], [w_oa, w_ob, w_o], sems)

        oa = _dot_nn(ya_ref[...], w_oa[...]).astype(BF16)
        ob = _dot_nn(yb_ref[...], w_ob[...]).astype(BF16)
        oa_ref[...] = oa
        ob_ref[...] = ob
        sa = _sigmoid(ga_ref[...] + gbias_ref[0:1, :].astype(BF16))
        sb = _sigmoid(gb_ref[...] + gbias_ref[1:2, :].astype(BF16))
        x1_ref[...] = x_ref[...] + _dot_nn(sa * oa + sb * ob, w_o[...])

    row = pl.BlockSpec((tm, d), lambda i: (i, 0))
    return _hosted_call(
        body, comms, name=name, grid=(t // tm,),
        out_shape=[jax.ShapeDtypeStruct((t, d), F32), jax.ShapeDtypeStruct((t, d), BF16), jax.ShapeDtypeStruct((t, d), BF16)],
        in_specs=[row, row, row, pl.BlockSpec((tm, d), lambda i: (i, 5)), pl.BlockSpec((tm, d), lambda i: (i, 6)),
                  pl.BlockSpec((SUBLANE, d), lambda i: (0, 0)), ANY],
        out_specs=[row, row, row],
        scratch_shapes=[pltpu.VMEM((d, d), BF16)] * 3 + [pltpu.SemaphoreType.DMA((3 * N_DEV,))],
        args=(x, ya, yb, p, p, gbias, gath))


def _ffn_fwd(x1, g_row, gath, grp, name, comms=()):
    t, d = x1.shape
    ff = grp.rows["g"] * N_DEV
    tm = _row_tile(t, 512)
    fc = MXU_TILE
    assert ff % fc == 0

    def body(x_ref, g_ref, gath_ref, x2_ref, gg_ref, uu_ref, w_g, w_u, w_d, acc, sems):
        @pl.when(pl.program_id(0) == 0)
        def _():
            _load_weights(gath_ref, grp, ["g", "u", "d"], [w_g, w_u, w_d], sems)

        xf = x_ref[...]
        rstd = lax.rsqrt(jnp.mean(xf * xf, axis=-1, keepdims=True) + RMS_EPS)
        h = (xf * rstd * g_ref[...]).astype(BF16)
        acc[...] = xf
        for c in range(ff // fc):
            fs = slice(c * fc, (c + 1) * fc)
            gg = _dot_nt(h, w_g[fs, :]).astype(BF16)
            uu = _dot_nt(h, w_u[fs, :]).astype(BF16)
            gg_ref[:, fs] = gg
            uu_ref[:, fs] = uu
            acc[...] += _dot_nn(gg * _sigmoid(gg) * uu, w_d[fs, :])
        x2_ref[...] = acc[...]

    row = pl.BlockSpec((tm, d), lambda i: (i, 0))
    wide = pl.BlockSpec((tm, ff), lambda i: (i, 0))
    return _hosted_call(
        body, comms, name=name, grid=(t // tm,),
        out_shape=[jax.ShapeDtypeStruct((t, d), F32), jax.ShapeDtypeStruct((t, ff), BF16), jax.ShapeDtypeStruct((t, ff), BF16)],
        in_specs=[row, pl.BlockSpec((1, d), lambda i: (0, 0)), ANY],
        out_specs=[row, wide, wide],
        scratch_shapes=[pltpu.VMEM((ff, d), BF16)] * 3 + [pltpu.VMEM((tm, d), F32), pltpu.SemaphoreType.DMA((3 * N_DEV,))],
        args=(x1, g_row, gath))


def _loss_head(x, g_row, target, name):
    t, d = x.shape
    tm = _row_tile(t, 512)

    n_t = t // tm
    depth = min(3, n_t)

    def body(x_hbm, g_ref, tgt_hbm, loss_ref, dx_ref, dg_ref, xbuf, tbuf, sems):
        i = pl.program_id(0)

        def fetch(tile, slot):
            rows = pl.ds(tile * tm, tm)
            return (pltpu.make_async_copy(x_hbm.at[rows, :], xbuf.at[slot], sems.at[0, slot]),
                    pltpu.make_async_copy(tgt_hbm.at[rows, :], tbuf.at[slot], sems.at[1, slot]))

        @pl.when(i == 0)
        def _():
            _zero(loss_ref)
            _zero(dg_ref)
            for k in range(depth):
                for cp in fetch(k, k):
                    cp.start()

        slot = i % depth
        for cp in fetch(i, slot):
            cp.wait()
        xf = xbuf[slot]
        tgt = tbuf[slot]
        rstd = lax.rsqrt(jnp.mean(xf * xf, axis=-1, keepdims=True) + RMS_EPS)
        xh = xf * rstd
        g = g_ref[...]
        err = xh * g - tgt
        loss_ref[...] += 0.5 * jnp.sum(jnp.sum(err * err, axis=-1, keepdims=True), axis=0, keepdims=True) * (1.0 / d)
        dy = err * (1.0 / d)
        dg_ref[0:1, :] += jnp.sum(dy * xh, axis=0, keepdims=True)
        dxh = dy * g
        dx_ref[...] = rstd * (dxh - xh * jnp.mean(dxh * xh, axis=-1, keepdims=True))

        @pl.when(i + depth < n_t)
        def _():
            for cp in fetch(i + depth, slot):
                cp.start()

    row = pl.BlockSpec((tm, d), lambda i: (i, 0))
    return pl.pallas_call(
        body, name=name, grid=(n_t,),
        out_shape=[jax.ShapeDtypeStruct((SUBLANE, LANE), F32), jax.ShapeDtypeStruct((t, d), F32),
                   jax.ShapeDtypeStruct((SUBLANE, d), F32)],
        in_specs=[ANY, pl.BlockSpec((1, d), lambda i: (0, 0)), ANY],
        out_specs=[pl.BlockSpec((SUBLANE, LANE), lambda i: (0, 0)), row, pl.BlockSpec((SUBLANE, d), lambda i: (0, 0))],
        scratch_shapes=[pltpu.VMEM((depth, tm, d), F32), pltpu.VMEM((depth, tm, d), F32), pltpu.SemaphoreType.DMA((2, depth))],
        compiler_params=_params(),
    )(x, g_row, target)


def _ffn_bwd_act(dx2, gg, uu, gath, grp, name, comms=()):
    t, d = dx2.shape
    ff = grp.rows["g"] * N_DEV
    tm = _row_tile(t, 512)
    fc = MXU_TILE
    n_t = t // tm

    def body(dx2_ref, gg_ref, uu_ref, gath_ref, dgg_ref, duu_ref, dwd_ref, w_d, acc, sems):
        @pl.when(pl.program_id(0) == 0)
        def _():
            _load_weights(gath_ref, grp, ["d"], [w_d], sems)
            _zero(acc)

        dx2b = dx2_ref[...].astype(BF16)
        for c in range(ff // fc):
            fs = slice(c * fc, (c + 1) * fc)
            df = _dot_nt(dx2b, w_d[fs, :]).astype(BF16)
            g = gg_ref[:, fs]
            u = uu_ref[:, fs]
            sg = _sigmoid(g)
            silu = g * sg
            acc[fs, :] += _dot_tn(silu * u, dx2b)
            duu_ref[:, fs] = df * silu
            dgg_ref[:, fs] = df * u * (sg * (1.0 + g * (1.0 - sg)))

        @pl.when(pl.program_id(0) == n_t - 1)
        def _():
            w_d[...] = acc[...].astype(BF16)
            out = pltpu.make_async_copy(w_d, dwd_ref, sems.at[0])
            out.start()
            out.wait()

    row = pl.BlockSpec((tm, d), lambda i: (i, 0))
    wide = pl.BlockSpec((tm, ff), lambda i: (i, 0))
    sd = jax.ShapeDtypeStruct
    return _hosted_call(
        body, comms, name=name, grid=(n_t,),
        out_shape=[sd((t, ff), BF16), sd((t, ff), BF16), sd((ff, d), BF16)],
        in_specs=[row, wide, wide, ANY],
        out_specs=[wide, wide, ANY],
        scratch_shapes=[pltpu.VMEM((ff, d), BF16), pltpu.VMEM((ff, d), F32), pltpu.SemaphoreType.DMA((N_DEV,))],
        args=(dx2, gg, uu, gath))


def _ffn_bwd_in(dgg, duu, dx2, x1, g_row, gath, grp, name, comms=()):
    t, d = x1.shape
    ff = grp.rows["g"] * N_DEV
    tm = _row_tile(t, 512)

    def body(dgg_ref, duu_ref, dx2_ref, x_ref, g_ref, gath_ref, dx1_ref, dx1b_ref, h_ref, dg_ref, w_g, w_u, sems):
        @pl.when(pl.program_id(0) == 0)
        def _():
            _load_weights(gath_ref, grp, ["g", "u"], [w_g, w_u], sems)
            _zero(dg_ref)

        dh = _dot_nn(dgg_ref[...], w_g[...]) + _dot_nn(duu_ref[...], w_u[...])
        xf = x_ref[...]
        rstd = lax.rsqrt(jnp.mean(xf * xf, axis=-1, keepdims=True) + RMS_EPS)
        xh = xf * rstd
        g = g_ref[...]
        h_ref[...] = (xh * g).astype(BF16)
        dg_ref[0:1, :] += jnp.sum(dh * xh, axis=0, keepdims=True)
        dxh = dh * g
        dx1 = dx2_ref[...] + rstd * (dxh - xh * jnp.mean(dxh * xh, axis=-1, keepdims=True))
        dx1_ref[...] = dx1
        dx1b_ref[...] = dx1.astype(BF16)

    row = pl.BlockSpec((tm, d), lambda i: (i, 0))
    wide = pl.BlockSpec((tm, ff), lambda i: (i, 0))
    sd = jax.ShapeDtypeStruct
    return _hosted_call(
        body, comms, name=name, grid=(t // tm,),
        out_shape=[sd((t, d), F32), sd((t, d), BF16), sd((t, d), BF16), sd((SUBLANE, d), F32)],
        in_specs=[wide, wide, row, row, pl.BlockSpec((1, d), lambda i: (0, 0)), ANY],
        out_specs=[row, row, row, pl.BlockSpec((SUBLANE, d), lambda i: (0, 0))],
        scratch_shapes=[pltpu.VMEM((ff, d), BF16)] * 2 + [pltpu.SemaphoreType.DMA((2 * N_DEV,))],
        args=(dgg, duu, dx2, x1, g_row, gath))


def _merge_bwd(dx1b, oa, ob, ya, yb, p, gbias, gath, grp, name, comms=()):
    t, d = oa.shape
    tm = _row_tile(t, 512)
    n_t = t // tm

    def body(dx_ref, oa_ref, ob_ref, ya_ref, yb_ref, ga_ref, gb_ref, gbias_ref, gath_ref,
             dya_ref, dyb_ref, dp_ref, dgb_ref, dwoa_ref, dwob_ref, dwo_ref,
             w_oa, w_ob, w_o, acc_oa, acc_ob, acc_o, stage, sems, out_sems):
        @pl.when(pl.program_id(0) == 0)
        def _():
            _load_weights(gath_ref, grp, ["oa", "ob", "o"], [w_oa, w_ob, w_o], sems)
            for ref in (dgb_ref, acc_oa, acc_ob, acc_o):
                _zero(ref)

        dxb = dx_ref[...]
        dm = _dot_nt(dxb, w_o[...]).astype(BF16)
        oa = oa_ref[...]
        ob = ob_ref[...]
        sa = _sigmoid(ga_ref[...] + gbias_ref[0:1, :].astype(BF16))
        sb = _sigmoid(gb_ref[...] + gbias_ref[1:2, :].astype(BF16))
        acc_o[...] += _dot_tn(sa * oa + sb * ob, dxb)
        doa = dm * sa
        dob = dm * sb
        acc_oa[...] += _dot_tn(ya_ref[...], doa)
        acc_ob[...] += _dot_tn(yb_ref[...], dob)
        dga = dm * oa * sa * (1.0 - sa)
        dgb = dm * ob * sb * (1.0 - sb)
        step = pl.program_id(0)
        slot = step % 2

        def to_dp(k, at_step):
            return pltpu.make_async_copy(stage.at[k], dp_ref.at[pl.ds(at_step * tm, tm), pl.ds(5 * d, 2 * d)], out_sems.at[k])

        @pl.when(step >= 2)
        def _():
            to_dp(slot, step - 2).wait()

        stage[slot, :, 0:d] = dga
        stage[slot, :, d:2 * d] = dgb
        to_dp(slot, step).start()
        ones = jnp.ones((SUBLANE, tm), BF16)
        dgb_ref[0:1, :] += _dot_nn(ones, dga)[0:1, :]
        dgb_ref[1:2, :] += _dot_nn(ones, dgb)[0:1, :]
        dya_ref[...] = _dot_nt(doa, w_oa[...]).astype(BF16)
        dyb_ref[...] = _dot_nt(dob, w_ob[...]).astype(BF16)

        @pl.when(pl.program_id(0) == n_t - 1)
        def _():
            outs = []
            for n, (acc, stage, dst) in enumerate(((acc_oa, w_oa, dwoa_ref), (acc_ob, w_ob, dwob_ref), (acc_o, w_o, dwo_ref))):
                stage[...] = acc[...].astype(BF16)
                outs.append(pltpu.make_async_copy(stage, dst, sems.at[n]))
                outs[-1].start()
            for cp in outs:
                cp.wait()
            for back in range(min(2, n_t)):
                to_dp((n_t - 1 - back) % 2, n_t - 1 - back).wait()

    row = pl.BlockSpec((tm, d), lambda i: (i, 0))
    sd = jax.ShapeDtypeStruct
    return _hosted_call(
        body, comms, name=name, grid=(n_t,),
        out_shape=[sd((t, d), BF16), sd((t, d), BF16), sd((t, N_PROJ * d), BF16), sd((SUBLANE, d), F32),
                   sd((d, d), BF16), sd((d, d), BF16), sd((d, d), BF16)],
        in_specs=[row, row, row, row, row, pl.BlockSpec((tm, d), lambda i: (i, 5)), pl.BlockSpec((tm, d), lambda i: (i, 6)),
                  pl.BlockSpec((SUBLANE, d), lambda i: (0, 0)), ANY],
        out_specs=[row, row, ANY, pl.BlockSpec((SUBLANE, d), lambda i: (0, 0)), ANY, ANY, ANY],
        scratch_shapes=[pltpu.VMEM((d, d), BF16)] * 3 + [pltpu.VMEM((d, d), F32)] * 3
        + [pltpu.VMEM((2, tm, 2 * d), BF16), pltpu.SemaphoreType.DMA((3 * N_DEV,)), pltpu.SemaphoreType.DMA((2,))],
        args=(dx1b, oa, ob, ya, yb, p, p, gbias, gath))


DV_CONV_B_B, DV_BA, DV_BX, DV_SP, DV_CONV_A, DV_CONV_B = 0, 1, 2, 3, 4, 7
DV_ROWS = 16


def _mixer_bwd(dya, dyb, dp_gates, p, u_s, h_s, caw, cbw, vec, bda, bdx, name, comms=()):
    t, d = dya.shape
    tm = _time_tile(t)
    n_t = t // tm
    nk = tm // SUBLANE
    cw = min(MXU_TILE, d)
    nb = d // cw
    halo = 4 * SUBLANE
    ka, kb = CONV_A_K - 1, CONV_B_K - 1

    def body(dya_ref, dyb_ref, _, ba_ref, ca_ref, xa_ref, xb_ref, gb_ref, cah_ref, xah_ref, xbh_ref,
             u_ref, h_ref, hh_ref, caw_ref, cbw_ref, vec_ref, bda_ref, bdx_ref,
             dp_ref, dv_ref, dwa_ref, dwx_ref,
             zbuf, xbuf, hbuf, dczbuf, dubuf, a2buf, a1buf, lbuf, dcz_head, du_head, a_head, lam_head):
        i = pl.program_id(0)

        @pl.when(i == 0)
        def _():
            for ref in (dv_ref, dwa_ref, dwx_ref, dcz_head, du_head, a_head, lam_head):
                _zero(ref)

        has_prev = jnp.where(i < n_t - 1, 1.0, 0.0).astype(F32)
        row = lax.broadcasted_iota(jnp.int32, (SUBLANE, cw), 0)

        def colsum(v):
            return jnp.sum(v, axis=0, keepdims=True)

        for j in range(nb):
            cs = slice(j * cw, (j + 1) * cw)
            ca = ca_ref[:, cs].astype(F32)
            xa = xa_ref[:, cs].astype(F32)
            z = ca * xa
            z_before = cah_ref[:, cs].astype(F32) * xah_ref[:, cs].astype(F32) * has_prev
            _causal_fill(zbuf, z, z_before[halo - ka * SUBLANE:, :], ka, row)
            z2 = zbuf[0:tm, :]
            z1 = zbuf[SUBLANE:SUBLANE + tm, :]
            w0, w1, w2 = caw_ref[0:1, cs], caw_ref[1:2, cs], caw_ref[2:3, cs]
            cz = w0 * z2 + w1 * z1 + w2 * z
            dya = dya_ref[:, cs].astype(F32)
            dp_ref[:, 0 * d + j * cw:0 * d + (j + 1) * cw] = (dya * cz).astype(BF16)
            dcz = dya * ba_ref[:, cs].astype(F32)
            _anticausal_fill(dczbuf, dcz, dcz_head[:, cs], ka, row)
            dcz_head[:, cs] = dcz[0:ka * SUBLANE, :]
            dz = w2 * dcz + w1 * dczbuf[SUBLANE:SUBLANE + tm, :] + w0 * dczbuf[2 * SUBLANE:2 * SUBLANE + tm, :]
            dv_ref[DV_CONV_A + 0:DV_CONV_A + 1, cs] += colsum(dcz * z2)
            dv_ref[DV_CONV_A + 1:DV_CONV_A + 2, cs] += colsum(dcz * z1)
            dv_ref[DV_CONV_A + 2:DV_CONV_A + 3, cs] += colsum(dcz * z)
            dp_ref[:, 1 * d + j * cw:1 * d + (j + 1) * cw] = (dz * xa).astype(BF16)
            dp_ref[:, 2 * d + j * cw:2 * d + (j + 1) * cw] = (dz * ca).astype(BF16)
            h = h_ref[:, cs].astype(F32)
            h_before = hh_ref[:, cs].astype(F32) * has_prev
            _causal_fill(hbuf, h, h_before[halo - SUBLANE:, :], 1, row)
            h_prev = hbuf[0:tm, :]
            dyb = dyb_ref[:, cs].astype(F32)
            gel, dgel = _gelu_and_grad(gb_ref[:, cs].astype(F32))
            dp_ref[:, 4 * d + j * cw:4 * d + (j + 1) * cw] = (dyb * h * dgel).astype(BF16)
            ub = u_ref[:, cs]
            u = ub.astype(F32)
            sp = vec_ref[3:4, cs]
            r, gi, a, s, inv_s = _lru_gates(ub, bda_ref[j], bdx_ref[j], vec_ref[1:2, cs], vec_ref[2:3, cs], sp)
            _anticausal_fill(a2buf, a, a_head[:, cs], 1, row)
            a_head[:, cs] = a[0:SUBLANE, :]
            a1buf[...] = a2buf[SUBLANE:SUBLANE + tm, :]
            lbuf[...] = dyb * gel
            l_end, a_prod = _chain_scan(a1buf, lbuf, nk, reverse=True)
            a_inc, l_inc = _sublane_scan(a_prod, l_end, row, reverse=True)
            carry = lam_head[:, cs]
            ends = l_inc + a_inc * carry
            starts = jnp.where(row == SUBLANE - 1, carry, pltpu.roll(ends, SUBLANE - 1, 0))
            lam_head[:, cs] = jnp.broadcast_to(ends[0:1, :], (SUBLANE, cw))
            lam = (lbuf[...].reshape(nk, SUBLANE, cw) + a1buf[...].reshape(nk, SUBLANE, cw) * starts[None]).reshape(tm, cw)
            da = lam * h_prev
            iu = gi * u
            ds = lam * iu
            di = lam * s * u
            du = lam * s * gi
            dlog_a = da * a - ds * (a * a) * inv_s
            dv_ref[DV_SP:DV_SP + 1, cs] += colsum(dlog_a * r) * (-LRU_C)
            dpr = dlog_a * ((-LRU_C) * sp) * r * (1.0 - r)
            dpi = di * gi * (1.0 - gi)
            dv_ref[DV_BA:DV_BA + 1, cs] += colsum(dpr)
            dv_ref[DV_BX:DV_BX + 1, cs] += colsum(dpi)
            dprb = dpr.astype(BF16)
            dpib = dpi.astype(BF16)
            du = du + _dot_nt(dprb, bda_ref[j]) + _dot_nt(dpib, bdx_ref[j])
            dwa_ref[j] += _dot_tn(ub, dprb)
            dwx_ref[j] += _dot_tn(ub, dpib)
            xb = xb_ref[:, cs].astype(F32)
            x_before = xbh_ref[:, cs].astype(F32) * has_prev
            _causal_fill(xbuf, xb, x_before[halo - kb * SUBLANE:, :], kb, row)
            _anticausal_fill(dubuf, du, du_head[:, cs], kb, row)
            du_head[:, cs] = du[0:kb * SUBLANE, :]
            v0, v1, v2, v3 = cbw_ref[0:1, cs], cbw_ref[1:2, cs], cbw_ref[2:3, cs], cbw_ref[3:4, cs]
            dxb = (v3 * du + v2 * dubuf[SUBLANE:SUBLANE + tm, :] + v1 * dubuf[2 * SUBLANE:2 * SUBLANE + tm, :]
                   + v0 * dubuf[3 * SUBLANE:3 * SUBLANE + tm, :])
            dp_ref[:, 3 * d + j * cw:3 * d + (j + 1) * cw] = dxb.astype(BF16)
            dv_ref[DV_CONV_B_B:DV_CONV_B_B + 1, cs] += colsum(du)
            dv_ref[DV_CONV_B + 0:DV_CONV_B + 1, cs] += colsum(du * xbuf[0:tm, :])
            dv_ref[DV_CONV_B + 1:DV_CONV_B + 2, cs] += colsum(du * xbuf[SUBLANE:SUBLANE + tm, :])
            dv_ref[DV_CONV_B + 2:DV_CONV_B + 3, cs] += colsum(du * xbuf[2 * SUBLANE:2 * SUBLANE + tm, :])
            dv_ref[DV_CONV_B + 3:DV_CONV_B + 4, cs] += colsum(du * xb)

    rt = lambda i: n_t - 1 - i
    row_spec = pl.BlockSpec((tm, d), lambda i: (rt(i), 0))
    slab = lambda s: pl.BlockSpec((tm, d), lambda i, s=s: (rt(i), s))
    before = lambda s: pl.BlockSpec((halo, d), lambda i, s=s: (jnp.maximum(rt(i) * (tm // halo) - 1, 0), s))
    small = pl.BlockSpec((SUBLANE, d), lambda i: (0, 0))
    bd = pl.BlockSpec((nb, cw, cw), lambda i: (0, 0, 0))
    sd = jax.ShapeDtypeStruct
    wbuf = lambda n: pltpu.VMEM((tm + n * SUBLANE, cw), F32)
    head = lambda n: pltpu.VMEM((n * SUBLANE, d), F32)
    return _hosted_call(
        body, comms, name=name, grid=(n_t,),
        out_shape=[sd((t, N_PROJ * d), BF16), sd((DV_ROWS, d), F32), sd((nb, cw, cw), F32), sd((nb, cw, cw), F32)],
        in_specs=[row_spec, row_spec, ANY,
                  slab(0), slab(1), slab(2), slab(3), slab(4), before(1), before(2), before(3),
                  row_spec, row_spec, before(0), small, small, small, bd, bd],
        out_specs=[pl.BlockSpec((tm, 5 * d), lambda i: (rt(i), 0)), pl.BlockSpec((DV_ROWS, d), lambda i: (0, 0)), bd, bd],
        aliases={2: 0},
        scratch_shapes=[wbuf(ka), wbuf(kb), wbuf(1), wbuf(ka), wbuf(kb), wbuf(1),
                        pltpu.VMEM((tm, cw), F32), pltpu.VMEM((tm, cw), F32), head(ka), head(kb), head(1), head(1)],
        args=(dya, dyb, dp_gates, p, p, p, p, p, p, p, p, u_s, h_s, h_s, caw, cbw, vec, bda, bdx))


def _in_proj_bwd(dp, x, dx1, g_row, gath, grp, name, comms=()):
    t, d = x.shape
    tm = _row_tile(t, 512)
    n_in = N_PROJ * d

    def body(dp_ref, x_ref, dx1_ref, g_ref, gath_ref, dx_ref, dg_ref, w_in, sems):
        @pl.when(pl.program_id(0) == 0)
        def _():
            _load_weights(gath_ref, grp, ["in"], [w_in], sems)
            _zero(dg_ref)

        dh = _dot_nn(dp_ref[:, 0:d], w_in[0:d, :])
        for k in range(1, N_PROJ):
            dh = dh + _dot_nn(dp_ref[:, k * d:(k + 1) * d], w_in[k * d:(k + 1) * d, :])
        xf = x_ref[...]
        rstd = lax.rsqrt(jnp.mean(xf * xf, axis=-1, keepdims=True) + RMS_EPS)
        xh = xf * rstd
        g = g_ref[...]
        dg_ref[0:1, :] += jnp.sum(dh * xh, axis=0, keepdims=True)
        dxh = dh * g
        dx_ref[...] = dx1_ref[...] + rstd * (dxh - xh * jnp.mean(dxh * xh, axis=-1, keepdims=True))

    row = pl.BlockSpec((tm, d), lambda i: (i, 0))
    sd = jax.ShapeDtypeStruct
    return _hosted_call(
        body, comms, name=name, grid=(t // tm,),
        out_shape=[sd((t, d), F32), sd((SUBLANE, d), F32)],
        in_specs=[pl.BlockSpec((tm, n_in), lambda i: (i, 0)), row, row, pl.BlockSpec((1, d), lambda i: (0, 0)), ANY],
        out_specs=[row, pl.BlockSpec((SUBLANE, d), lambda i: (0, 0))],
        scratch_shapes=[pltpu.VMEM((n_in, d), BF16), pltpu.SemaphoreType.DMA((N_DEV,))],
        args=(dp, x, dx1, g_row, gath))


def _weight_grad(a, b, name):
    t, m = a.shape
    n = b.shape[1]
    bm = m
    for div in (1, 2, 4, 8):
        if m % div == 0 and (m // div) % LANE == 0 and (m // div) * n * 4 <= (12 << 20):
            bm = m // div
            break
    fixed = bm * n * (4 + 2 * 2)
    bt = next(_row_tile(t, want) for want in (2048, 1024, 512)
              if 2 * 2 * _row_tile(t, want) * (bm + n) + fixed <= VMEM_LIMIT - (12 << 20))
    n_t = t // bt

    def body(a_ref, b_ref, o_ref, acc):
        k = pl.program_id(1)

        @pl.when(k == 0)
        def _():
            _zero(acc)

        acc[...] += _dot_tn(a_ref[...], b_ref[...])

        @pl.when(k == n_t - 1)
        def _():
            o_ref[...] = acc[...].astype(BF16)

    return pl.pallas_call(
        body, name=name, grid=(m // bm, n_t),
        out_shape=jax.ShapeDtypeStruct((m, n), BF16),
        in_specs=[pl.BlockSpec((bt, bm), lambda i, k: (k, i)), pl.BlockSpec((bt, n), lambda i, k: (k, 0))],
        out_specs=pl.BlockSpec((bm, n), lambda i, k: (i, 0)),
        scratch_shapes=[pltpu.VMEM((bm, n), F32)],
        compiler_params=_params(2),
    )(a, b)


def _adamw(w, g, m, v, name):
    r, c = w.shape
    tr = _fit_rows(r, c * 4)
    c1 = 1.0 - ADAM_B1 ** ADAM_STEP
    c2 = 1.0 - ADAM_B2 ** ADAM_STEP

    def body(w_ref, g_ref, m_ref, v_ref, d_ref, nm_ref, nv_ref):
        g32 = g_ref[...]
        nm = ADAM_B1 * m_ref[...] + (1.0 - ADAM_B1) * g32
        nv = ADAM_B2 * v_ref[...] + (1.0 - ADAM_B2) * (g32 * g32)
        nm_ref[...] = nm
        nv_ref[...] = nv
        d_ref[...] = -ADAM_LR * ((nm / c1) / (jnp.sqrt(nv / c2) + ADAM_EPS) + ADAM_WD * w_ref[...])

    spec = pl.BlockSpec((tr, c), lambda i: (i, 0))
    return pl.pallas_call(
        body, name=name, grid=(r // tr,),
        out_shape=[jax.ShapeDtypeStruct((r, c), F32)] * 3,
        in_specs=[spec] * 4, out_specs=[spec] * 3,
        compiler_params=_params(),
    )(w, g, m, v)


def _pad_rows(a, mult=SUBLANE):
    pad = (-a.shape[0]) % mult
    return a if pad == 0 else jnp.concatenate([a, jnp.zeros((pad,) + a.shape[1:], a.dtype)], axis=0)


REPLICATED = ("ln1_g", "conv_b_b", "lru_wa", "lru_ba", "lru_wx", "lru_bx", "lru_lambda", "ln2_g", "final_g")
SMALL_SHARDED = ("conv_a_w", "conv_b_w", "gate_bias")
MATRICES = ("w_in", "w_out_a", "w_out_b", "w_o", "w_ffn_gate", "w_ffn_up", "w_ffn_down")
ORDER = ("ln1_g", "w_in", "conv_a_w", "conv_b_w", "conv_b_b", "lru_wa", "lru_ba", "lru_wx", "lru_bx", "lru_lambda",
         "w_out_a", "w_out_b", "gate_bias", "w_o", "ln2_g", "w_ffn_gate", "w_ffn_up", "w_ffn_down", "final_g")


def kernel(x, ln1_g, w_in, conv_a_w, conv_b_w, conv_b_b, lru_wa, lru_ba, lru_wx, lru_bx, lru_lambda, w_out_a, w_out_b, gate_bias, w_o, ln2_g, w_ffn_gate, w_ffn_up, w_ffn_down, final_g, loss_target, m_ln1_g, m_w_in, m_conv_a_w, m_conv_b_w, m_conv_b_b, m_lru_wa, m_lru_ba, m_lru_wx, m_lru_bx, m_lru_lambda, m_w_out_a, m_w_out_b, m_gate_bias, m_w_o, m_ln2_g, m_w_ffn_gate, m_w_ffn_up, m_w_ffn_down, m_final_g, v_ln1_g, v_w_in, v_conv_a_w, v_conv_b_w, v_conv_b_b, v_lru_wa, v_lru_ba, v_lru_wx, v_lru_bx, v_lru_lambda, v_w_out_a, v_w_out_b, v_gate_bias, v_w_o, v_ln2_g, v_w_ffn_gate, v_w_ffn_up, v_w_ffn_down, v_final_g):
    w = dict(ln1_g=ln1_g, w_in=w_in, conv_a_w=conv_a_w, conv_b_w=conv_b_w, conv_b_b=conv_b_b, lru_wa=lru_wa,
             lru_ba=lru_ba, lru_wx=lru_wx, lru_bx=lru_bx, lru_lambda=lru_lambda, w_out_a=w_out_a, w_out_b=w_out_b,
             gate_bias=gate_bias, w_o=w_o, ln2_g=ln2_g, w_ffn_gate=w_ffn_gate, w_ffn_up=w_ffn_up,
             w_ffn_down=w_ffn_down, final_g=final_g)
    mom = dict(ln1_g=m_ln1_g, w_in=m_w_in, conv_a_w=m_conv_a_w, conv_b_w=m_conv_b_w, conv_b_b=m_conv_b_b,
               lru_wa=m_lru_wa, lru_ba=m_lru_ba, lru_wx=m_lru_wx, lru_bx=m_lru_bx, lru_lambda=m_lru_lambda,
               w_out_a=m_w_out_a, w_out_b=m_w_out_b, gate_bias=m_gate_bias, w_o=m_w_o, ln2_g=m_ln2_g,
               w_ffn_gate=m_w_ffn_gate, w_ffn_up=m_w_ffn_up, w_ffn_down=m_w_ffn_down, final_g=m_final_g)
    var = dict(ln1_g=v_ln1_g, w_in=v_w_in, conv_a_w=v_conv_a_w, conv_b_w=v_conv_b_w, conv_b_b=v_conv_b_b,
               lru_wa=v_lru_wa, lru_ba=v_lru_ba, lru_wx=v_lru_wx, lru_bx=v_lru_bx, lru_lambda=v_lru_lambda,
               w_out_a=v_w_out_a, w_out_b=v_w_out_b, gate_bias=v_gate_bias, w_o=v_w_o, ln2_g=v_ln2_g,
               w_ffn_gate=v_w_ffn_gate, w_ffn_up=v_w_ffn_up, w_ffn_down=v_w_ffn_down, final_g=v_final_g)

    _, t, d = x.shape
    n_layers = w_in.shape[0]
    ff = w_ffn_down.shape[1] * N_DEV
    dd = d // N_DEV
    hd = d // LRU_HEADS
    cw = min(MXU_TILE, d)
    nb = d // cw
    hpt = cw // hd
    grp = _groups(d, ff)
    me = 4 * lax.axis_index("x") + 2 * lax.axis_index("y") + lax.axis_index("c")
    tm_time = _time_tile(t)
    x0 = _to_tile_order(x[0], tm_time)
    target = _to_tile_order(loss_target[0], tm_time)

    packed = [{"in": jnp.swapaxes(w_in[l], 0, 1).astype(BF16),
               "rest": jnp.concatenate([w_out_a[l], w_out_b[l], w_o[l], jnp.swapaxes(w_ffn_gate[l], 0, 1),
                                        jnp.swapaxes(w_ffn_up[l], 0, 1), w_ffn_down[l]], axis=0).astype(BF16)}
              for l in range(n_layers)]
    n_small = CONV_A_K + CONV_B_K + 2
    small = _pad_rows(jnp.concatenate([conv_a_w, conv_b_w, gate_bias], axis=1).reshape(n_layers * n_small, dd))
    sp = jax.nn.softplus(-lru_lambda)
    vec = [_pad_rows(jnp.stack([conv_b_b[l], lru_ba[l], lru_bx[l], sp[l]])) for l in range(n_layers)]
    eye = jnp.eye(hpt, dtype=F32)

    def block_diag(wh):
        return jnp.einsum("jkab,kl->jkalb", wh.reshape(nb, hpt, hd, hd), eye).reshape(nb, cw, cw).astype(BF16)

    bda = [block_diag(lru_wa[l]) for l in range(n_layers)]
    bdx = [block_diag(lru_wx[l]) for l in range(n_layers)]

    gath = [dict() for _ in range(n_layers)]
    (gath[0]["in"],), (small_g,) = _comm_call([_Gather(packed[0]["in"]), _Gather(small)], "gather_in_0")
    small_full = jnp.swapaxes(small_g[:, :n_layers * n_small], 0, 1).reshape(n_layers, n_small, d)
    caw = [_pad_rows(small_full[k, 0:CONV_A_K]) for k in range(n_layers)]
    cbw = [_pad_rows(small_full[k, CONV_A_K:CONV_A_K + CONV_B_K]) for k in range(n_layers)]
    gbias = [_pad_rows(small_full[k, CONV_A_K + CONV_B_K:]) for k in range(n_layers)]
    saved = []
    xl = x0
    for l in range(n_layers):
        more = l + 1 < n_layers
        comms = ([_Gather(packed[0]["rest"])] if l == 0 else []) + ([_Gather(packed[l + 1]["in"])] if more else [])
        (p, h1b, ya, yb, *kept), got = _in_proj_mixer_fwd(
            xl, ln1_g[l][None], gath[l]["in"], grp["in"], caw[l], cbw[l], vec[l], bda[l], bdx[l], f"in_proj_mixer_fwd_{l}", comms)
        if l == 0:
            gath[0]["rest"] = got[0][0]
        if more:
            gath[l + 1]["in"] = got[-1][0]
        (x1, oa, ob), _ = _merge_fwd(xl, ya, yb, p, gbias[l], gath[l]["rest"], grp["rest"], f"merge_fwd_{l}")
        (x2, gg, uu), got = _ffn_fwd(x1, ln2_g[l][None], gath[l]["rest"], grp["rest"], f"ffn_fwd_{l}",
                                     [_Gather(packed[l + 1]["rest"])] if more else [])
        if more:
            ((gath[l + 1]["rest"],),) = got
        saved.append(dict(x=xl, p=p, h1b=h1b, ya=ya, yb=yb, mixer=kept, x1=x1, oa=oa, ob=ob, gg=gg, uu=uu))
        xl = x2
    loss_tile, dx, dfinal = _loss_head(xl, final_g[None], target, "loss_head")
    loss = lax.psum(loss_tile[0, 0], ("x", "y", "c"))

    def heads(dwb):
        blocks = jnp.diagonal(dwb.reshape(nb, hpt, hd, hpt, hd), axis1=1, axis2=3)
        return jnp.moveaxis(blocks, 3, 1).reshape(hd, d)

    layer_names = [n for n in REPLICATED if n != "final_g"] + list(SMALL_SHARDED)

    def layer_block(k):
        return jnp.concatenate([small_grads[k][n] for n in layer_names], axis=0)

    recv = [dict() for _ in range(n_layers)]
    small_grads = [None] * n_layers
    early_all = None
    xg = {"d": _Group(("d",), (ff // N_DEV,)), "gu": _Group(("g", "u"), (ff // N_DEV,) * 2),
          "out": _Group(("oa", "ob", "o"), (dd,) * 3), "in": grp["in"]}
    far_in = None
    for l in reversed(range(n_layers)):
        s = saved[l]
        (dgg, duu, dw_d), got = _ffn_bwd_act(dx, s["gg"], s["uu"], gath[l]["rest"], grp["rest"], f"ffn_bwd_act_{l}",
                                             [far_in] if far_in else [])
        if far_in:
            recv[l + 1]["in"].append(got[0][0])
        (dx1, dx1b, h2b, dln2), got = _ffn_bwd_in(dgg, duu, dx, s["x1"], ln2_g[l][None], gath[l]["rest"], grp["rest"],
                                                  f"ffn_bwd_in_{l}", [_Exchange({"d": dw_d}, xg["d"])])
        recv[l]["d"] = [got[0][0]]
        dw_gu = {"g": _weight_grad(dgg, h2b, f"dw_ffn_gate_{l}"), "u": _weight_grad(duu, h2b, f"dw_ffn_up_{l}")}
        (dya, dyb, dp_gates, dgbias, dw_oa, dw_ob, dw_o), got = _merge_bwd(
            dx1b, s["oa"], s["ob"], s["ya"], s["yb"], s["p"], gbias[l], gath[l]["rest"], grp["rest"], f"merge_bwd_{l}",
            [_Exchange(dw_gu, xg["gu"])])
        recv[l]["gu"] = [got[0][0]]
        comms = [_Exchange({"oa": dw_oa, "ob": dw_ob, "o": dw_o}, xg["out"])]
        if l == 0:
            early = [layer_block(k) for k in range(1, n_layers)] + [_pad_rows(dfinal[0:1])]
            comms.append(_Gather(jnp.concatenate(early, axis=0)))
        (dp, dv, dwa, dwx), got = _mixer_bwd(dya, dyb, dp_gates, s["p"], *s["mixer"], caw[l], cbw[l], vec[l], bda[l], bdx[l],
                                             f"mixer_bwd_{l}", comms)
        recv[l]["out"] = [got[0][0]]
        if l == 0:
            early_all = got[1][0]
        small_grads[l] = {
            "conv_b_b": dv[DV_CONV_B_B:DV_CONV_B_B + 1], "lru_wa": heads(dwa),
            "lru_ba": dv[DV_BA:DV_BA + 1], "lru_wx": heads(dwx), "lru_bx": dv[DV_BX:DV_BX + 1],
            "lru_lambda": dv[DV_SP:DV_SP + 1] * (-jax.nn.sigmoid(-lru_lambda[l]))[None], "ln2_g": dln2[0:1],
            "conv_a_w": dv[DV_CONV_A:DV_CONV_A + CONV_A_K], "conv_b_w": dv[DV_CONV_B:DV_CONV_B + CONV_B_K],
            "gate_bias": dgbias[0:2],
        }
        dw_in = {"in": _weight_grad(dp, s["h1b"], f"dw_in_{l}")}
        if l > 0:
            near_in, far_in = _Exchange(dw_in, xg["in"], NEAR_PEERS), _Exchange(dw_in, xg["in"], FAR_PEERS, local=False)
        else:
            near_in, far_in = _Exchange(dw_in, xg["in"]), None
        (dx, dln1), got = _in_proj_bwd(dp, s["x"], dx1, ln1_g[l][None], gath[l]["in"], grp["in"], f"in_proj_bwd_{l}", [near_in])
        recv[l]["in"] = [got[0][0]]
        small_grads[l]["ln1_g"] = dln1[0:1]
    grad_x = _from_tile_order(dx, tm_time)[None]

    g = {}
    gsum = [{k: _sum_slots(recv[l][k], f"sum_{k}_{l}") for k in xg} for l in range(n_layers)]

    def part(key):
        k = next(name for name, group in xg.items() if key in group.keys)
        o, r = xg[k].off[key], xg[k].rows[key]
        return jnp.stack([gsum[l][k][o:o + r] for l in range(n_layers)])

    g = {"w_in": jnp.swapaxes(part("in"), 1, 2), "w_out_a": part("oa"), "w_out_b": part("ob"), "w_o": part("o"),
         "w_ffn_gate": jnp.swapaxes(part("g"), 1, 2), "w_ffn_up": jnp.swapaxes(part("u"), 1, 2), "w_ffn_down": part("d")}
    ((late_all,),) = _comm_call([_Gather(layer_block(0).astype(BF16))], "gather_small_grads_0")
    early_sum = _sum_slots([early_all], "sum_small_grads")
    block_rows = late_all.shape[1]
    per_layer = [_sum_slots([late_all], "sum_small_grads_0")]
    per_layer += [early_sum[(k - 1) * block_rows:k * block_rows] for k in range(1, n_layers)]
    g["final_g"] = early_sum[(n_layers - 1) * block_rows].reshape(w["final_g"].shape)
    o = 0
    for n in layer_names:
        rows = small_grads[0][n].shape[0]
        stacked = jnp.concatenate([per_layer[k][o:o + rows] for k in range(n_layers)], axis=0)
        if n in SMALL_SHARDED:
            g[n] = lax.dynamic_slice_in_dim(stacked, me * dd, dd, axis=1).reshape(n_layers, rows, dd)
        else:
            g[n] = stacked.reshape(w[n].shape)
        o += rows

    delta, new_m, new_v = {}, {}, {}
    gate_maps = ("lru_wa", "lru_wx")
    for n in MATRICES + gate_maps:
        shape = w[n].shape
        flat = lambda a: a.reshape(-1, d if n in gate_maps else shape[-1])
        dl, nm, nv = _adamw(flat(w[n]), flat(g[n]), flat(mom[n]), flat(var[n]), f"adamw_{n}")
        delta[n], new_m[n], new_v[n] = dl.reshape(shape), nm.reshape(shape), nv.reshape(shape)
    vectors = tuple(n for n in REPLICATED if n not in gate_maps)
    for group, width, name in ((vectors, d, "adamw_replicated"), (SMALL_SHARDED, dd, "adamw_vectors")):
        cat = lambda src: _pad_rows(jnp.concatenate([src[n].reshape(-1, width) for n in group], axis=0))
        dl, nm, nv = _adamw(cat(w), cat(g), cat(mom), cat(var), name)
        o = 0
        for n in group:
            rows = w[n].size // width
            delta[n], new_m[n], new_v[n] = (a[o:o + rows].reshape(w[n].shape) for a in (dl, nm, nv))
            o += rows

    return (loss, grad_x, *[g[n] for n in ORDER], *[delta[n] for n in ORDER], *[new_m[n] for n in ORDER],
            *[new_v[n] for n in ORDER])
```
